```python
import math
import jax, jax.numpy as jnp
from jax import lax
import numpy as np

D_MODEL = 1024
BATCH = 16
SEQ = 2048
DEPTH = 2

N_MIXERS = 2
EXPAND = 2
D_INNER = EXPAND * D_MODEL
POOL_WINDOWS = (2, 4, 8, 16)
N_POOL_GROUPS = len(POOL_WINDOWS)
POOL_GROUP_DIM = D_INNER // N_POOL_GROUPS
SB_HEADS = 16
SB_QK_DIM = 64
SB_V_DIM = D_INNER // SB_HEADS
SB_QK_WIDTH = SB_HEADS * SB_QK_DIM
Q_BLOCK = 128
RMS_EPS = 1e-6
N_POOL_LAYERS = (DEPTH + 1) // 2
N_SB_LAYERS = DEPTH // 2

kernel_name = "hybrid_pool_stickbreaking_trunk"


def rms_norm(x, g):
    x32 = x.astype(jnp.float32)
    inv = lax.rsqrt(jnp.mean(x32 * x32, axis=-1, keepdims=True) + RMS_EPS)
    return (x32 * inv * g.astype(jnp.float32)).astype(x.dtype)


def causal_pool_minus_self(u, window):
    S = u.shape[1]
    u32 = u.astype(jnp.float32)
    c = jnp.cumsum(u32, axis=1)
    c_shift = jnp.pad(c, ((0, 0), (window, 0), (0, 0)))[:, :S]
    count = jnp.minimum(jnp.arange(S) + 1, window).astype(jnp.float32)
    mean = (c - c_shift) / count[None, :, None]
    return (mean - u32).astype(u.dtype)


def pooling_mixer(u, w_in, w_group, scale, w_out):
    B, S, _ = u.shape
    proj = u @ w_in
    xb, z = jnp.split(proj, 2, axis=-1)
    xg = xb.reshape(B, S, N_POOL_GROUPS, POOL_GROUP_DIM)
    pooled = jnp.stack(
        [causal_pool_minus_self(xg[:, :, gi], w) for gi, w in enumerate(POOL_WINDOWS)],
        axis=2)
    mixed = jnp.einsum('bsgi,gio->bsgo', pooled, w_group).reshape(B, S, D_INNER)
    y = mixed * scale * jax.nn.silu(z)
    return y @ w_out


def stick_breaking_attention(q, k, v):
    S = q.shape[2]
    scale = 1.0 / math.sqrt(q.shape[-1])
    n_blocks = S // Q_BLOCK
    outs = []
    for bi in range(n_blocks):
        q_lo = bi * Q_BLOCK
        k_len = q_lo + Q_BLOCK
        qb = q[:, :, q_lo:q_lo + Q_BLOCK].astype(jnp.float32)
        kb = k[:, :, :k_len].astype(jnp.float32)
        vb = v[:, :, :k_len].astype(jnp.float32)
        z = jnp.einsum('bhqd,bhkd->bhqk', qb, kb) * scale
        q_pos = q_lo + jnp.arange(Q_BLOCK)
        k_pos = jnp.arange(k_len)
        mask = k_pos[None, :] < q_pos[:, None]
        log_beta = jax.nn.log_sigmoid(z)
        log_om = jnp.where(mask, jax.nn.log_sigmoid(-z), 0.0)
        later = lax.cumsum(log_om, axis=3, reverse=True) - log_om
        a = jnp.where(mask, jnp.exp(log_beta + later), 0.0)
        outs.append(jnp.einsum('bhqk,bhkd->bhqd', a, vb))
    return jnp.concatenate(outs, axis=2).astype(v.dtype)


def stick_breaking_mixer(u, w_in, w_out):
    B, S, _ = u.shape
    proj = u @ w_in
    q, k, v, z = jnp.split(
        proj, [SB_QK_WIDTH, 2 * SB_QK_WIDTH, 2 * SB_QK_WIDTH + D_INNER], axis=-1)
    q = q.reshape(B, S, SB_HEADS, SB_QK_DIM).transpose(0, 2, 1, 3)
    k = k.reshape(B, S, SB_HEADS, SB_QK_DIM).transpose(0, 2, 1, 3)
    v = v.reshape(B, S, SB_HEADS, SB_V_DIM).transpose(0, 2, 1, 3)
    o = stick_breaking_attention(q, k, v)
    o = o.transpose(0, 2, 1, 3).reshape(B, S, D_INNER)
    y = o * jax.nn.silu(z)
    return y @ w_out


def _fwd_setup_inputs(seed: int = 0) -> dict:
    key = jax.random.key(seed)
    ks = jax.random.split(key, 10)
    f32 = jnp.float32
    x = jax.random.normal(ks[0], (BATCH, SEQ, D_MODEL), f32)
    norm_g = 1.0 + 0.02 * jax.random.normal(ks[1], (DEPTH, D_MODEL), f32)
    pool_w_in = jax.random.normal(ks[2], (N_POOL_LAYERS, D_MODEL, 2 * D_INNER), f32) * D_MODEL ** -0.5
    pool_w = jax.random.normal(ks[3], (N_POOL_LAYERS, N_POOL_GROUPS, POOL_GROUP_DIM, POOL_GROUP_DIM), f32) * POOL_GROUP_DIM ** -0.5
    pool_scale = 1.0 + 0.02 * jax.random.normal(ks[4], (N_POOL_LAYERS, D_INNER), f32)
    pool_w_out = jax.random.normal(ks[5], (N_POOL_LAYERS, D_INNER, D_MODEL), f32) * D_INNER ** -0.5
    sb_w_in = jax.random.normal(ks[6], (N_SB_LAYERS, D_MODEL, 2 * SB_QK_WIDTH + 2 * D_INNER), f32) * D_MODEL ** -0.5
    sb_w_out = jax.random.normal(ks[7], (N_SB_LAYERS, D_INNER, D_MODEL), f32) * D_INNER ** -0.5
    norm_f = 1.0 + 0.02 * jax.random.normal(ks[8], (D_MODEL,), f32)
    return {"x": x, "norm_g": norm_g, "pool_w_in": pool_w_in, "pool_w": pool_w,
            "pool_scale": pool_scale, "pool_w_out": pool_w_out, "sb_w_in": sb_w_in,
            "sb_w_out": sb_w_out, "norm_f": norm_f}


def _fwd_reference(x, norm_g, pool_w_in, pool_w, pool_scale, pool_w_out, sb_w_in, sb_w_out, norm_f):
    h = x
    for i in range(DEPTH):
        u = rms_norm(h, norm_g[i])
        j = i // N_MIXERS
        if i % N_MIXERS == 0:
            h = h + pooling_mixer(u, pool_w_in[j], pool_w[j], pool_scale[j], pool_w_out[j])
        else:
            h = h + stick_breaking_mixer(u, sb_w_in[j], sb_w_out[j])
    return rms_norm(h, norm_f)


import jax as _jax
import jax.numpy as _jnp

TWIN_FORMAT = 'train_step'
FWD_PARAMS = ['x', 'norm_g', 'pool_w_in', 'pool_w', 'pool_scale', 'pool_w_out', 'sb_w_in', 'sb_w_out', 'norm_f']
TWIN_WEIGHTS = ['norm_g', 'pool_w_in', 'pool_w', 'pool_scale', 'pool_w_out', 'sb_w_in', 'sb_w_out', 'norm_f']
TWIN_DIFF_INPUT = 'x'
TWIN_INPUTS = ['x', 'norm_g', 'pool_w_in', 'pool_w', 'pool_scale', 'pool_w_out', 'sb_w_in', 'sb_w_out', 'norm_f', 'loss_target', 'm_norm_g', 'm_pool_w_in', 'm_pool_w', 'm_pool_scale', 'm_pool_w_out', 'm_sb_w_in', 'm_sb_w_out', 'm_norm_f', 'v_norm_g', 'v_pool_w_in', 'v_pool_w', 'v_pool_scale', 'v_pool_w_out', 'v_sb_w_in', 'v_sb_w_out', 'v_norm_f']
TWIN_OUTPUTS = ['loss', 'grad_x', 'grad_norm_g', 'grad_pool_w_in', 'grad_pool_w', 'grad_pool_scale', 'grad_pool_w_out', 'grad_sb_w_in', 'grad_sb_w_out', 'grad_norm_f', 'delta_norm_g', 'delta_pool_w_in', 'delta_pool_w', 'delta_pool_scale', 'delta_pool_w_out', 'delta_sb_w_in', 'delta_sb_w_out', 'delta_norm_f', 'new_m_norm_g', 'new_m_pool_w_in', 'new_m_pool_w', 'new_m_pool_scale', 'new_m_pool_w_out', 'new_m_sb_w_in', 'new_m_sb_w_out', 'new_m_norm_f', 'new_v_norm_g', 'new_v_pool_w_in', 'new_v_pool_w', 'new_v_pool_scale', 'new_v_pool_w_out', 'new_v_sb_w_in', 'new_v_sb_w_out', 'new_v_norm_f']
TWIN_LEAF_KINDS = {'loss': 'loss', 'grad_x': 'grad_x', 'grad_norm_g': 'grad_w', 'grad_pool_w_in': 'grad_w', 'grad_pool_w': 'grad_w', 'grad_pool_scale': 'grad_w', 'grad_pool_w_out': 'grad_w', 'grad_sb_w_in': 'grad_w', 'grad_sb_w_out': 'grad_w', 'grad_norm_f': 'grad_w', 'delta_norm_g': 'delta_w', 'delta_pool_w_in': 'delta_w', 'delta_pool_w': 'delta_w', 'delta_pool_scale': 'delta_w', 'delta_pool_w_out': 'delta_w', 'delta_sb_w_in': 'delta_w', 'delta_sb_w_out': 'delta_w', 'delta_norm_f': 'delta_w', 'new_m_norm_g': 'new_m', 'new_m_pool_w_in': 'new_m', 'new_m_pool_w': 'new_m', 'new_m_pool_scale': 'new_m', 'new_m_pool_w_out': 'new_m', 'new_m_sb_w_in': 'new_m', 'new_m_sb_w_out': 'new_m', 'new_m_norm_f': 'new_m', 'new_v_norm_g': 'new_v', 'new_v_pool_w_in': 'new_v', 'new_v_pool_w': 'new_v', 'new_v_pool_scale': 'new_v', 'new_v_pool_w_out': 'new_v', 'new_v_sb_w_in': 'new_v', 'new_v_sb_w_out': 'new_v', 'new_v_norm_f': 'new_v'}


def _forward(args):
    return _fwd_reference(*[args[k] for k in FWD_PARAMS])


def _output_shape():
    out = _jax.eval_shape(lambda: _forward(_fwd_setup_inputs(0)))
    return out.shape, out.dtype

N_MICROBATCH = 1
ADAM_LR = 0.001
ADAM_B1 = 0.9
ADAM_B2 = 0.999
ADAM_EPS = 1e-08
ADAM_WD = 0.01
ADAM_STEP = 10
PER_EXAMPLE_BATCH_AXIS = {'x': 0, 'loss_target': 0}
SHARED_INPUTS = []
_WEIGHT_DTYPES = {'norm_g': _jnp.float32, 'pool_w_in': _jnp.float32, 'pool_w': _jnp.float32, 'pool_scale': _jnp.float32, 'pool_w_out': _jnp.float32, 'sb_w_in': _jnp.float32, 'sb_w_out': _jnp.float32, 'norm_f': _jnp.float32}
MOMENT_SCALE = {'norm_g': 1.102427e-01, 'pool_w_in': 6.341049e-02, 'pool_w': 6.243724e-02, 'pool_scale': 6.148635e-02, 'pool_w_out': 8.806192e-02, 'sb_w_in': 3.700943e-02, 'sb_w_out': 5.711060e-02, 'norm_f': 3.200407e+01}


def _to_microbatches(a, axis):
    t = _jnp.moveaxis(a, axis, 0)
    t = t.reshape((N_MICROBATCH, t.shape[0] // N_MICROBATCH) + t.shape[1:])
    return _jnp.moveaxis(t, 1, axis + 1)


def setup_inputs(seed: int = 0) -> dict:
    inp = _fwd_setup_inputs(seed)
    key = _jax.random.fold_in(_jax.random.key(seed), 7919)
    shape, _ = _output_shape()
    out = dict(inp)
    out["loss_target"] = _jax.random.normal(_jax.random.fold_in(key, 0), shape, _jnp.float32)
    for i, name in enumerate(TWIN_WEIGHTS):
        w = inp[name].astype(_jnp.float32)
        if MOMENT_SCALE is None:
            s = _jnp.sqrt(_jnp.mean(_jnp.square(w)) + 1e-30)
        else:
            s = MOMENT_SCALE[name]
        km, kv = _jax.random.split(_jax.random.fold_in(key, i + 1))
        out[name] = w
        out["m_" + name] = s * _jax.random.normal(km, w.shape, _jnp.float32)
        out["v_" + name] = (s * s) * _jax.random.uniform(kv, w.shape, _jnp.float32, 0.5, 1.5)
    if N_MICROBATCH > 1:
        for name, axis in PER_EXAMPLE_BATCH_AXIS.items():
            out[name] = _to_microbatches(out[name], axis)
    return {'x': out['x'], 'norm_g': out['norm_g'], 'pool_w_in': out['pool_w_in'], 'pool_w': out['pool_w'], 'pool_scale': out['pool_scale'], 'pool_w_out': out['pool_w_out'], 'sb_w_in': out['sb_w_in'], 'sb_w_out': out['sb_w_out'], 'norm_f': out['norm_f'], 'loss_target': out['loss_target'], 'm_norm_g': out['m_norm_g'], 'm_pool_w_in': out['m_pool_w_in'], 'm_pool_w': out['m_pool_w'], 'm_pool_scale': out['m_pool_scale'], 'm_pool_w_out': out['m_pool_w_out'], 'm_sb_w_in': out['m_sb_w_in'], 'm_sb_w_out': out['m_sb_w_out'], 'm_norm_f': out['m_norm_f'], 'v_norm_g': out['v_norm_g'], 'v_pool_w_in': out['v_pool_w_in'], 'v_pool_w': out['v_pool_w'], 'v_pool_scale': out['v_pool_scale'], 'v_pool_w_out': out['v_pool_w_out'], 'v_sb_w_in': out['v_sb_w_in'], 'v_sb_w_out': out['v_sb_w_out'], 'v_norm_f': out['v_norm_f']}


def _loss(weights, diff, rest, loss_target):
    with _jax.named_scope("forward"):
        args = {**rest, TWIN_DIFF_INPUT: diff, **{k: w.astype(_WEIGHT_DTYPES[k]) for k, w in weights.items()}}
        y = _forward(args)
    with _jax.named_scope("loss_head"):
        err = _jnp.square(y.astype(_jnp.float32) - loss_target)
        return 0.5 * _jnp.sum(_jnp.mean(err, axis=-1)) if err.ndim else 0.5 * err


def _adamw(w, g, m, v):
    m = ADAM_B1 * m + (1.0 - ADAM_B1) * g
    v = ADAM_B2 * v + (1.0 - ADAM_B2) * _jnp.square(g)
    m_hat = m / (1.0 - ADAM_B1 ** ADAM_STEP)
    v_hat = v / (1.0 - ADAM_B2 ** ADAM_STEP)
    delta = -ADAM_LR * (m_hat / (_jnp.sqrt(v_hat) + ADAM_EPS) + ADAM_WD * w)
    return delta, m, v


def reference(x, norm_g, pool_w_in, pool_w, pool_scale, pool_w_out, sb_w_in, sb_w_out, norm_f, loss_target, m_norm_g, m_pool_w_in, m_pool_w, m_pool_scale, m_pool_w_out, m_sb_w_in, m_sb_w_out, m_norm_f, v_norm_g, v_pool_w_in, v_pool_w, v_pool_scale, v_pool_w_out, v_sb_w_in, v_sb_w_out, v_norm_f):
    given = dict(x=x, norm_g=norm_g, pool_w_in=pool_w_in, pool_w=pool_w, pool_scale=pool_scale, pool_w_out=pool_w_out, sb_w_in=sb_w_in, sb_w_out=sb_w_out, norm_f=norm_f, loss_target=loss_target, m_norm_g=m_norm_g, m_pool_w_in=m_pool_w_in, m_pool_w=m_pool_w, m_pool_scale=m_pool_scale, m_pool_w_out=m_pool_w_out, m_sb_w_in=m_sb_w_in, m_sb_w_out=m_sb_w_out, m_norm_f=m_norm_f, v_norm_g=v_norm_g, v_pool_w_in=v_pool_w_in, v_pool_w=v_pool_w, v_pool_scale=v_pool_scale, v_pool_w_out=v_pool_w_out, v_sb_w_in=v_sb_w_in, v_sb_w_out=v_sb_w_out, v_norm_f=v_norm_f)
    weights = {n: given[n] for n in TWIN_WEIGHTS}
    shared = {n: given[n] for n in SHARED_INPUTS}
    per_example = {n: given[n] for n in ['x']}
    grad_fn = _jax.value_and_grad(_loss, argnums=(0, 1))

    def one_microbatch(ex, loss_target):
        ex = dict(ex)
        diff = ex.pop(TWIN_DIFF_INPUT)
        return grad_fn(weights, diff, {**shared, **ex}, loss_target)

    if N_MICROBATCH == 1:
        loss, (grad_w, grad_x) = one_microbatch(per_example, given["loss_target"])
    else:
        def body(carry, xs):
            loss_sum, grad_sum = carry
            l_k, (gw_k, gx_k) = one_microbatch(xs[0], xs[1])
            with _jax.named_scope("update"):
                return (loss_sum + l_k, _jax.tree.map(_jnp.add, grad_sum, gw_k)), gx_k

        init = (_jnp.zeros((), _jnp.float32), _jax.tree.map(_jnp.zeros_like, weights))
        (loss, grad_w), grad_x = _jax.lax.scan(body, init, (per_example, given["loss_target"]))
    with _jax.named_scope("update"):
        delta_w, new_m, new_v = {}, {}, {}
        for n in TWIN_WEIGHTS:
            delta_w[n], new_m[n], new_v[n] = _adamw(weights[n], grad_w[n], given["m_" + n], given["v_" + n])
    return (loss, grad_x, *[grad_w[n] for n in TWIN_WEIGHTS], *[delta_w[n] for n in TWIN_WEIGHTS],
            *[new_m[n] for n in TWIN_WEIGHTS], *[new_v[n] for n in TWIN_WEIGHTS])
```

```python
import functools

import jax
import jax.numpy as jnp
from jax import lax
from jax.experimental import pallas as pl
from jax.experimental.pallas import tpu as pltpu

F32 = jnp.float32
BF16 = jnp.bfloat16
MESH = pl.DeviceIdType.MESH

D_MODEL = 1024
D_INNER = 2048
N_GROUPS = 4
GROUP_DIM = 512
HEAD_PAIR_QK = 128
HEAD_V = 128
N_HEAD_PAIRS = 8
QK_WIDTH = 1024
RMS_EPS = 1e-6
HALO = 16
N_CHIPS = 4
N_DEV = 8

ADAM_LR = 0.001
ADAM_B1 = 0.9
ADAM_B2 = 0.999
ADAM_EPS = 1e-08
ADAM_WD = 0.01
ADAM_STEP = 10

VMEM_LIMIT = 56 * 1024 * 1024

HBM_SPEC = pl.BlockSpec(memory_space=pltpu.HBM)


def _params(n_axes):
    return pltpu.CompilerParams(dimension_semantics=("arbitrary",) * n_axes,
                                vmem_limit_bytes=VMEM_LIMIT)


def _dot(a, b):
    return jnp.dot(a, b, preferred_element_type=F32)


def _dot_nt(a, b):
    return lax.dot_general(a, b, (((1,), (1,)), ((), ())), preferred_element_type=F32)


def _dot_tn(a, b):
    return lax.dot_general(a, b, (((0,), (0,)), ((), ())), preferred_element_type=F32)


def _sigmoid(z):
    return 1.0 / (1.0 + jnp.exp(-z))


def _rms_matmul(h, g_row, w4, col0, ncols, out_dtype, name, emit_u=False, tm=512, tn=512):
    T = h.shape[0]
    per_shard = w4.shape[2] // tn
    cb0 = col0 // tn

    def body(h_ref, g_ref, w_ref, *rest):
        if emit_u:
            o_ref, u_out, u_s = rest
        else:
            o_ref, u_s = rest

        @pl.when(pl.program_id(1) == 0)
        def _():
            x = h_ref[...]
            inv = lax.rsqrt(jnp.mean(x * x, axis=-1, keepdims=True) + RMS_EPS)
            u = (x * inv * g_ref[...]).astype(BF16)
            u_s[...] = u
            if emit_u:
                u_out[...] = u

        o_ref[...] = _dot(u_s[...], w_ref[0]).astype(out_dtype)

    out_shape = [jax.ShapeDtypeStruct((T, ncols), out_dtype)]
    out_specs = [pl.BlockSpec((tm, tn), lambda m, n: (m, n))]
    if emit_u:
        out_shape.append(jax.ShapeDtypeStruct((T, D_MODEL), BF16))
        out_specs.append(pl.BlockSpec((tm, D_MODEL), lambda m, n: (m, 0)))
    res = pl.pallas_call(
        body, name=name, grid=(T // tm, ncols // tn),
        in_specs=[pl.BlockSpec((tm, D_MODEL), lambda m, n: (m, 0)),
                  pl.BlockSpec((1, D_MODEL), lambda m, n: (0, 0)),
                  pl.BlockSpec((1, D_MODEL, tn),
                               lambda m, n: ((cb0 + n) // per_shard, 0, (cb0 + n) % per_shard))],
        out_specs=out_specs, out_shape=out_shape,
        scratch_shapes=[pltpu.VMEM((tm, D_MODEL), BF16)],
        compiler_params=_params(2),
    )(h, g_row, w4)
    return res if emit_u else res[0]


def _matmul_residual(a, w, res, name, tm=512, tn=512):
    T, K = a.shape
    N = w.shape[1]

    def body(a_ref, w_ref, r_ref, o_ref):
        o_ref[...] = r_ref[...] + _dot(a_ref[...], w_ref[...])

    return pl.pallas_call(
        body, name=name, grid=(T // tm, N // tn),
        in_specs=[pl.BlockSpec((tm, K), lambda m, n: (m, 0)),
                  pl.BlockSpec((K, tn), lambda m, n: (0, n)),
                  pl.BlockSpec((tm, tn), lambda m, n: (m, n))],
        out_specs=pl.BlockSpec((tm, tn), lambda m, n: (m, n)),
        out_shape=jax.ShapeDtypeStruct((T, N), F32),
        compiler_params=_params(2),
    )(a, w, res)


def _matmul_tn(a, b, a_col0, a_cols, b_col0, b_cols, out_shape, out_block, out_map, name,
               bm=512, bn=512, tk=512):
    T = a.shape[0]
    ab0, bb0 = a_col0 // bm, b_col0 // bn

    def body(a_ref, b_ref, o_ref):
        @pl.when(pl.program_id(2) == 0)
        def _():
            o_ref[...] = jnp.zeros_like(o_ref)

        part = _dot_tn(a_ref[...].astype(BF16), b_ref[...].astype(BF16))
        o_ref[...] += part.reshape(o_ref.shape)

    return pl.pallas_call(
        body, name=name, grid=(a_cols // bm, b_cols // bn, T // tk),
        in_specs=[pl.BlockSpec((tk, bm), lambda i, j, t: (t, ab0 + i)),
                  pl.BlockSpec((tk, bn), lambda i, j, t: (t, bb0 + j))],
        out_specs=pl.BlockSpec(out_block, out_map),
        out_shape=jax.ShapeDtypeStruct(out_shape, F32),
        compiler_params=_params(3),
    )(a, b)


def _matmul_nt_rms_bwd(dproj, w4, h, g_row, dres, name, tm=512, tk=512):
    T, cols = dproj.shape
    per_shard = w4.shape[2] // tk
    nk = cols // tk

    def body(dp_ref, w_ref, h_ref, g_ref, r_ref, dx_ref, dg_ref, acc):
        m, k = pl.program_id(0), pl.program_id(1)

        @pl.when(k == 0)
        def _():
            acc[...] = jnp.zeros_like(acc)

        @pl.when((k == 0) & (m == 0))
        def _():
            dg_ref[...] = jnp.zeros_like(dg_ref)

        acc[...] += _dot_nt(dp_ref[...], w_ref[0])

        @pl.when(k == nk - 1)
        def _():
            du = acc[...]
            x = h_ref[...]
            inv = lax.rsqrt(jnp.mean(x * x, axis=-1, keepdims=True) + RMS_EPS)
            xhat = x * inv
            dg_ref[...] += jnp.sum(du * xhat, axis=0, keepdims=True)
            dxh = du * g_ref[...]
            proj = jnp.mean(dxh * xhat, axis=-1, keepdims=True)
            dx_ref[...] = r_ref[...] + inv * (dxh - xhat * proj)

    return pl.pallas_call(
        body, name=name, grid=(T // tm, nk),
        in_specs=[pl.BlockSpec((tm, tk), lambda m, k: (m, k)),
                  pl.BlockSpec((1, D_MODEL, tk), lambda m, k: (k // per_shard, 0, k % per_shard)),
                  pl.BlockSpec((tm, D_MODEL), lambda m, k: (m, 0)),
                  pl.BlockSpec((1, D_MODEL), lambda m, k: (0, 0)),
                  pl.BlockSpec((tm, D_MODEL), lambda m, k: (m, 0))],
        out_specs=[pl.BlockSpec((tm, D_MODEL), lambda m, k: (m, 0)),
                   pl.BlockSpec((1, D_MODEL), lambda m, k: (0, 0))],
        out_shape=[jax.ShapeDtypeStruct((T, D_MODEL), F32),
                   jax.ShapeDtypeStruct((1, D_MODEL), F32)],
        scratch_shapes=[pltpu.VMEM((tm, D_MODEL), F32)],
        compiler_params=_params(2),
    )(dproj, w4, h, g_row, dres)


def _window_of(g):
    return jnp.left_shift(2, g)


def _select_stage(g, stages):
    res = stages[0]
    for i in range(1, len(stages)):
        res = jnp.where(g >= i, stages[i], res)
    return res


def _pool_fwd(proj0, wg4, scale_row, S, name, tm=256):
    T = proj0.shape[0]
    blocks_per_seq = S // tm
    hb = tm // HALO

    def body(x_ref, halo_ref, z_ref, w_ref, s_ref, y_ref, p_ref, mix_ref):
        m, g = pl.program_id(0), pl.program_id(1)
        first = (m % blocks_per_seq) == 0
        halo = jnp.where(first, 0.0, halo_ref[...])
        x = x_ref[...]
        ext = jnp.concatenate([halo, x], axis=0)
        stages = []
        cur = ext
        for sh in (1, 2, 4, 8):
            cur = cur + pltpu.roll(cur, sh, 0)
            stages.append(cur[HALO:, :])
        win_sum = _select_stage(g, stages)
        pos = (m % blocks_per_seq) * tm + lax.broadcasted_iota(jnp.int32, (tm, 1), 0)
        count = jnp.minimum(pos + 1, _window_of(g)).astype(F32)
        pooled = win_sum / count - x
        pooled_b = pooled.astype(BF16)
        mixed = _dot(pooled_b, w_ref[...].reshape(GROUP_DIM, GROUP_DIM))
        z = z_ref[...]
        y_ref[...] = (mixed * s_ref[...] * (z * _sigmoid(z))).astype(BF16)
        p_ref[...] = pooled_b
        mix_ref[...] = mixed

    blk = lambda m, g: (m, g)
    return pl.pallas_call(
        body, name=name, grid=(T // tm, N_GROUPS),
        in_specs=[pl.BlockSpec((tm, GROUP_DIM), blk),
                  pl.BlockSpec((HALO, GROUP_DIM), lambda m, g: (jnp.maximum(m * hb - 1, 0), g)),
                  pl.BlockSpec((tm, GROUP_DIM), lambda m, g: (m, N_GROUPS + g)),
                  pl.BlockSpec((N_CHIPS, GROUP_DIM // N_CHIPS, GROUP_DIM), lambda m, g: (0, g, 0)),
                  pl.BlockSpec((1, GROUP_DIM), lambda m, g: (0, g))],
        out_specs=[pl.BlockSpec((tm, GROUP_DIM), blk)] * 3,
        out_shape=[jax.ShapeDtypeStruct((T, D_INNER), BF16),
                   jax.ShapeDtypeStruct((T, D_INNER), BF16),
                   jax.ShapeDtypeStruct((T, D_INNER), F32)],
        compiler_params=_params(2),
    )(proj0, proj0, proj0, wg4, scale_row)


def _pool_gate_bwd(dh, w_out, proj0, mixed, scale_row, name, tm=512, tn=512):
    T = dh.shape[0]
    gate_b0 = D_INNER // tn

    def body(dh_ref, w_ref, z_ref, mix_ref, s_ref, dm_ref, dz_ref, ds_ref):
        @pl.when(pl.program_id(1) == 0)
        def _():
            ds_ref[...] = jnp.zeros_like(ds_ref)

        dy = _dot_nt(dh_ref[...].astype(BF16), w_ref[...])
        z = z_ref[...]
        sig = _sigmoid(z)
        silu = z * sig
        mixed = mix_ref[...]
        s = s_ref[...]
        dm_ref[...] = (dy * s * silu).astype(BF16)
        dz_ref[...] = (dy * mixed * s * (sig * (1.0 + z * (1.0 - sig)))).astype(BF16)
        ds_ref[...] += jnp.sum(dy * mixed * silu, axis=0, keepdims=True)

    return pl.pallas_call(
        body, name=name, grid=(D_INNER // tn, T // tm),
        in_specs=[pl.BlockSpec((tm, D_MODEL), lambda n, m: (m, 0)),
                  pl.BlockSpec((tn, D_MODEL), lambda n, m: (n, 0)),
                  pl.BlockSpec((tm, tn), lambda n, m: (m, gate_b0 + n)),
                  pl.BlockSpec((tm, tn), lambda n, m: (m, n)),
                  pl.BlockSpec((1, tn), lambda n, m: (0, n))],
        out_specs=[pl.BlockSpec((tm, tn), lambda n, m: (m, n)),
                   pl.BlockSpec((tm, tn), lambda n, m: (m, gate_b0 + n)),
                   pl.BlockSpec((1, tn), lambda n, m: (0, n))],
        out_shape=[jax.ShapeDtypeStruct((T, D_INNER), BF16),
                   jax.ShapeDtypeStruct((T, 2 * D_INNER), BF16),
                   jax.ShapeDtypeStruct((1, D_INNER), F32)],
        compiler_params=_params(2),
    )(dh, w_out, proj0, mixed, scale_row)


def _pool_bwd(dmixed, wg4, dproj0, S, name, tm=256):
    T = dmixed.shape[0]
    blocks_per_seq = S // tm
    hb = tm // HALO
    n_halo_blocks = T // HALO

    def body(dm_ref, halo_ref, w_ref, _, o_ref):
        m, g = pl.program_id(0), pl.program_id(1)
        ext = jnp.concatenate([dm_ref[...], halo_ref[...]], axis=0)
        dp = _dot_nt(ext, w_ref[...].reshape(GROUP_DIM, GROUP_DIM))
        pos = (m % blocks_per_seq) * tm + lax.broadcasted_iota(jnp.int32, (tm + HALO, 1), 0)
        count = jnp.minimum(pos + 1, _window_of(g)).astype(F32)
        c = jnp.where(pos < S, dp / count, 0.0)
        n = tm + HALO
        stages = []
        cur = c
        for sh in (1, 2, 4, 8):
            cur = cur + pltpu.roll(cur, n - sh, 0)
            stages.append(cur[:tm, :])
        o_ref[...] = (_select_stage(g, stages) - dp[:tm, :]).astype(BF16)

    blk = lambda m, g: (m, g)
    return pl.pallas_call(
        body, name=name, grid=(T // tm, N_GROUPS),
        in_specs=[pl.BlockSpec((tm, GROUP_DIM), blk),
                  pl.BlockSpec((HALO, GROUP_DIM),
                               lambda m, g: (jnp.minimum((m + 1) * hb, n_halo_blocks - 1), g)),
                  pl.BlockSpec((N_CHIPS, GROUP_DIM // N_CHIPS, GROUP_DIM), lambda m, g: (0, g, 0)),
                  HBM_SPEC],
        out_specs=pl.BlockSpec((tm, GROUP_DIM), blk),
        out_shape=jax.ShapeDtypeStruct(dproj0.shape, dproj0.dtype),
        input_output_aliases={3: 0},
        compiler_params=_params(2),
    )(dmixed, dmixed, wg4, dproj0)


TQ = 256


def _split_dot(x, m):
    hi = x.astype(BF16)
    lo = (x - hi.astype(F32)).astype(BF16)
    return _dot(hi, m) + _dot(lo, m)


def _log_terms(z):
    soft = jnp.log1p(jnp.exp(-jnp.abs(z)))
    log_beta = -(jnp.maximum(-z, 0.0) + soft)
    return log_beta, log_beta - z


def _attn_fwd(qkv, z1, S, name):
    T = qkv.shape[0]
    nq = S // TQ
    k_b0 = QK_WIDTH // HEAD_PAIR_QK
    v_b0 = 2 * QK_WIDTH // (2 * HEAD_V)

    def body(q_ref, k_ref, v_ref, z_ref, o_ref, y_ref, lt_ref):
        qi = pl.program_id(2)
        lane = lax.broadcasted_iota(jnp.int32, (1, HEAD_PAIR_QK), 1)
        row = lax.broadcasted_iota(jnp.int32, (TQ, TQ), 0)
        col = lax.broadcasted_iota(jnp.int32, (TQ, TQ), 1)
        later_in_block = (row > col).astype(BF16)
        q = q_ref[...]
        for hh in range(2):
            qm = jnp.where((lane // 64) == hh, q, jnp.zeros_like(q)) * 0.125

            def step(i, carry):
                run, acc = carry
                j = qi - i
                koff = pl.multiple_of(j * TQ, TQ)
                kb = k_ref[pl.ds(koff, TQ), :]
                vb = v_ref[pl.ds(koff, TQ), hh * HEAD_V:(hh + 1) * HEAD_V]
                z = _dot_nt(qm, kb)
                mask = (col + j * TQ) < (row + qi * TQ)
                log_beta, log_om = _log_terms(z)
                log_om = jnp.where(mask, log_om, 0.0)
                later = run + _split_dot(log_om, later_in_block)
                a = jnp.where(mask, jnp.exp(log_beta + later), 0.0)
                acc = acc + _dot(a.astype(BF16), vb)
                run = run + jnp.sum(log_om, axis=1, keepdims=True)
                return run, acc

            run, acc = lax.fori_loop(0, qi + 1, step,
                                     (jnp.zeros((TQ, 1), F32), jnp.zeros((TQ, HEAD_V), F32)))
            sl = slice(hh * HEAD_V, (hh + 1) * HEAD_V)
            z = z_ref[:, sl]
            o_ref[:, sl] = acc
            y_ref[:, sl] = (acc * (z * _sigmoid(z))).astype(BF16)
            lt_ref[:, hh:hh + 1] = run

    qblk = lambda b, p, i: (b * nq + i, p)
    return pl.pallas_call(
        body, name=name, grid=(T // S, N_HEAD_PAIRS, nq),
        in_specs=[pl.BlockSpec((TQ, HEAD_PAIR_QK), qblk),
                  pl.BlockSpec((S, HEAD_PAIR_QK), lambda b, p, i: (b, k_b0 + p)),
                  pl.BlockSpec((S, 2 * HEAD_V), lambda b, p, i: (b, v_b0 + p)),
                  pl.BlockSpec((TQ, 2 * HEAD_V), qblk)],
        out_specs=[pl.BlockSpec((TQ, 2 * HEAD_V), qblk),
                   pl.BlockSpec((TQ, 2 * HEAD_V), qblk),
                   pl.BlockSpec((None, TQ, 2), lambda b, p, i: (p, b * nq + i, 0))],
        out_shape=[jax.ShapeDtypeStruct((T, D_INNER), F32),
                   jax.ShapeDtypeStruct((T, D_INNER), BF16),
                   jax.ShapeDtypeStruct((N_HEAD_PAIRS, T, 2), F32)],
        compiler_params=_params(3),
    )(qkv, qkv, qkv, z1)


def _attn_gate_bwd(dh, w_out, z1, o, name, tm=512, tn=512):
    T = dh.shape[0]
    gate_b0 = (2 * QK_WIDTH + D_INNER) // tn

    def body(dh_ref, w_ref, z_ref, o_ref, do_ref, dz_ref, dh_s):
        @pl.when(pl.program_id(1) == 0)
        def _():
            dh_s[...] = dh_ref[...].astype(BF16)

        dy = _dot_nt(dh_s[...], w_ref[...])
        z = z_ref[...]
        sig = _sigmoid(z)
        do_ref[...] = (dy * (z * sig)).astype(BF16)
        dz_ref[...] = (dy * o_ref[...] * (sig * (1.0 + z * (1.0 - sig)))).astype(BF16)

    return pl.pallas_call(
        body, name=name, grid=(T // tm, D_INNER // tn),
        in_specs=[pl.BlockSpec((tm, D_MODEL), lambda m, n: (m, 0)),
                  pl.BlockSpec((tn, D_MODEL), lambda m, n: (n, 0)),
                  pl.BlockSpec((tm, tn), lambda m, n: (m, n)),
                  pl.BlockSpec((tm, tn), lambda m, n: (m, n))],
        out_specs=[pl.BlockSpec((tm, tn), lambda m, n: (m, n)),
                   pl.BlockSpec((tm, tn), lambda m, n: (m, gate_b0 + n))],
        out_shape=[jax.ShapeDtypeStruct((T, D_INNER), BF16),
                   jax.ShapeDtypeStruct((T, 2 * QK_WIDTH + 2 * D_INNER), BF16)],
        scratch_shapes=[pltpu.VMEM((tm, D_MODEL), BF16)],
        compiler_params=_params(2),
    )(dh, w_out, z1, o)


def _attn_bwd(qkv, do, ltot, dproj1, S, name):
    T = qkv.shape[0]
    nq = S // TQ
    k_b0 = QK_WIDTH // HEAD_PAIR_QK
    v_b0 = 2 * QK_WIDTH // (2 * HEAD_V)

    def body(q_ref, k_ref, v_ref, do_ref, lt_ref, _, out_ref, dq_s, dk_s, dv_s, dkb_s, dvb_s, sems):
        b, p = pl.program_id(0), pl.program_id(1)
        lane = lax.broadcasted_iota(jnp.int32, (1, HEAD_PAIR_QK), 1)
        row = lax.broadcasted_iota(jnp.int32, (TQ, TQ), 0)
        col = lax.broadcasted_iota(jnp.int32, (TQ, TQ), 1)
        upto = (row <= col).astype(BF16)
        before = (row < col).astype(BF16)
        dk_s[...] = jnp.zeros_like(dk_s)
        dv_s[...] = jnp.zeros_like(dv_s)

        def q_block(qi, _):
            qoff = pl.multiple_of(qi * TQ, TQ)
            q = q_ref[pl.ds(qoff, TQ), :]
            dq_heads = []
            for hh in range(2):
                head = (lane // 64) == hh
                vsl = slice(hh * HEAD_V, (hh + 1) * HEAD_V)
                qm = jnp.where(head, q, jnp.zeros_like(q)) * 0.125
                do_h = do_ref[pl.ds(qoff, TQ), vsl]
                total = lt_ref[pl.ds(qoff, TQ), hh:hh + 1]

                def k_block(j, carry):
                    g_before, lom_before, dq = carry
                    koff = pl.multiple_of(j * TQ, TQ)
                    kb = k_ref[pl.ds(koff, TQ), :]
                    vb = v_ref[pl.ds(koff, TQ), vsl]
                    z = _dot_nt(qm, kb)
                    mask = (col + j * TQ) < (row + qi * TQ)
                    log_beta, log_om = _log_terms(z)
                    log_om = jnp.where(mask, log_om, 0.0)
                    later = total - lom_before - _split_dot(log_om, upto)
                    a = jnp.where(mask, jnp.exp(log_beta + later), 0.0)
                    a_b = a.astype(BF16)
                    g = a * _dot_nt(do_h, vb)
                    g_excl = g_before + _split_dot(g, before)
                    beta = jnp.exp(log_beta)
                    dz = (g * (1.0 - beta) - jnp.where(mask, g_excl * beta, 0.0)).astype(BF16)
                    dq = dq + _dot(dz, kb)
                    dk_s[pl.ds(koff, TQ), :] += _dot_tn(dz, qm)
                    dv_s[pl.ds(koff, TQ), vsl] += _dot_tn(a_b, do_h)
                    g_before = g_before + jnp.sum(g, axis=1, keepdims=True)
                    lom_before = lom_before + jnp.sum(log_om, axis=1, keepdims=True)
                    return g_before, lom_before, dq

                zero = jnp.zeros((TQ, 1), F32)
                _, _, dq = lax.fori_loop(0, qi + 1, k_block,
                                         (zero, zero, jnp.zeros((TQ, HEAD_PAIR_QK), F32)))
                dq_heads.append(jnp.where(head, dq * 0.125, 0.0))
            dq_s[pl.ds(qoff, TQ), :] = (dq_heads[0] + dq_heads[1]).astype(BF16)
            return 0

        lax.fori_loop(0, nq, q_block, 0)
        dkb_s[...] = dk_s[...].astype(BF16)
        dvb_s[...] = dv_s[...].astype(BF16)
        rows = pl.ds(pl.multiple_of(b * S, TQ), S)
        copies = [
            pltpu.make_async_copy(
                dq_s, out_ref.at[rows, pl.ds(pl.multiple_of(p * HEAD_PAIR_QK, 128), HEAD_PAIR_QK)],
                sems.at[0]),
            pltpu.make_async_copy(
                dkb_s, out_ref.at[rows, pl.ds(pl.multiple_of(QK_WIDTH + p * HEAD_PAIR_QK, 128),
                                              HEAD_PAIR_QK)], sems.at[1]),
            pltpu.make_async_copy(
                dvb_s, out_ref.at[rows, pl.ds(pl.multiple_of(2 * QK_WIDTH + p * 2 * HEAD_V, 128),
                                              2 * HEAD_V)], sems.at[2]),
        ]
        for cp in copies:
            cp.start()
        for cp in copies:
            cp.wait()

    return pl.pallas_call(
        body, name=name, grid=(T // S, N_HEAD_PAIRS),
        in_specs=[pl.BlockSpec((S, HEAD_PAIR_QK), lambda b, p: (b, p)),
                  pl.BlockSpec((S, HEAD_PAIR_QK), lambda b, p: (b, k_b0 + p)),
                  pl.BlockSpec((S, 2 * HEAD_V), lambda b, p: (b, v_b0 + p)),
                  pl.BlockSpec((S, 2 * HEAD_V), lambda b, p: (b, p)),
                  pl.BlockSpec((None, S, 2), lambda b, p: (p, b, 0)),
                  HBM_SPEC],
        out_specs=HBM_SPEC,
        out_shape=jax.ShapeDtypeStruct(dproj1.shape, dproj1.dtype),
        input_output_aliases={5: 0},
        scratch_shapes=[pltpu.VMEM((S, HEAD_PAIR_QK), BF16),
                        pltpu.VMEM((S, HEAD_PAIR_QK), F32),
                        pltpu.VMEM((S, 2 * HEAD_V), F32),
                        pltpu.VMEM((S, HEAD_PAIR_QK), BF16),
                        pltpu.VMEM((S, 2 * HEAD_V), BF16),
                        pltpu.SemaphoreType.DMA((3,))],
        compiler_params=_params(2),
    )(qkv, qkv, qkv, do, ltot, dproj1)


def _loss_head(h, g_row, target, name, tm=512):
    T = h.shape[0]

    def body(h_ref, g_ref, t_ref, dh_ref, dg_ref, loss_ref):
        @pl.when(pl.program_id(0) == 0)
        def _():
            dg_ref[...] = jnp.zeros_like(dg_ref)
            loss_ref[...] = jnp.zeros_like(loss_ref)

        x = h_ref[...]
        inv = lax.rsqrt(jnp.mean(x * x, axis=-1, keepdims=True) + RMS_EPS)
        xhat = x * inv
        gain = g_ref[...]
        err = xhat * gain - t_ref[...]
        per_token = jnp.mean(err * err, axis=-1, keepdims=True)
        loss_ref[...] += 0.5 * jnp.sum(per_token, axis=0, keepdims=True)
        dy = err * (1.0 / D_MODEL)
        dg_ref[...] += jnp.sum(dy * xhat, axis=0, keepdims=True)
        dxh = dy * gain
        proj = jnp.mean(dxh * xhat, axis=-1, keepdims=True)
        dh_ref[...] = inv * (dxh - xhat * proj)

    return pl.pallas_call(
        body, name=name, grid=(T // tm,),
        in_specs=[pl.BlockSpec((tm, D_MODEL), lambda m: (m, 0)),
                  pl.BlockSpec((1, D_MODEL), lambda m: (0, 0)),
                  pl.BlockSpec((tm, D_MODEL), lambda m: (m, 0))],
        out_specs=[pl.BlockSpec((tm, D_MODEL), lambda m: (m, 0)),
                   pl.BlockSpec((1, D_MODEL), lambda m: (0, 0)),
                   pl.BlockSpec((1, 128), lambda m: (0, 0))],
        out_shape=[jax.ShapeDtypeStruct((T, D_MODEL), F32),
                   jax.ShapeDtypeStruct((1, D_MODEL), F32),
                   jax.ShapeDtypeStruct((1, 128), F32)],
        compiler_params=_params(1),
    )(h, g_row, target)


def _place():
    return lax.axis_index("x"), lax.axis_index("y"), lax.axis_index("c")


def _other_chips(x, y):
    return [(1 - x, y), (x, 1 - y), (1 - x, 1 - y)]


def _half(ref, c):
    hr = ref.shape[-2] // 2
    return pl.ds(pl.multiple_of(c * hr, 8), hr)


def _allgather_weights(shards, name):
    n = len(shards)

    def body(*refs):
        ins, outs = refs[:n], refs[n:2 * n]
        send_sems, recv_sems, fwd_send, fwd_recv, loc_sems = refs[2 * n:]
        x, y, c = _place()
        me = 2 * x + y
        chips = _other_chips(x, y)
        local = [pltpu.make_async_copy(ins[a], outs[a].at[me], loc_sems.at[a]) for a in range(n)]
        for cp in local:
            cp.start()

        def landing(a, chip, half_of):
            return outs[a].at[2 * chip[0] + chip[1], _half(ins[a], half_of)]

        def ici(a, k, chip_from, to):
            return pltpu.make_async_remote_copy(
                src_ref=ins[a].at[_half(ins[a], c)], dst_ref=landing(a, chip_from, c),
                send_sem=send_sems.at[a, k], recv_sem=recv_sems.at[a, k],
                device_id=to, device_id_type=MESH)

        def d2d(a, k, chip_from, half_of):
            return pltpu.make_async_remote_copy(
                src_ref=landing(a, chip_from, half_of), dst_ref=landing(a, chip_from, half_of),
                send_sem=fwd_send.at[a, k], recv_sem=fwd_recv.at[a, k],
                device_id=(x, y, 1 - c), device_id_type=MESH)

        sends = [ici(a, k, (x, y), (*chips[k], c)) for a in range(n) for k in range(3)]
        for cp in sends:
            cp.start()
        forwards = []
        for a in range(n):
            for k in range(3):
                ici(a, k, chips[k], (x, y, c)).wait_recv()
                fw = d2d(a, k, chips[k], c)
                fw.start()
                forwards.append(fw)
        for a in range(n):
            for k in range(3):
                d2d(a, k, chips[k], 1 - c).wait_recv()
        for cp in sends + forwards:
            cp.wait_send()
        for cp in local:
            cp.wait()

    return pl.pallas_call(
        body, name=name,
        in_specs=[HBM_SPEC] * n, out_specs=[HBM_SPEC] * n,
        out_shape=[jax.ShapeDtypeStruct((N_CHIPS,) + s.shape, s.dtype) for s in shards],
        scratch_shapes=[pltpu.SemaphoreType.DMA((n, 3)), pltpu.SemaphoreType.DMA((n, 3)),
                        pltpu.SemaphoreType.DMA((n, 3)), pltpu.SemaphoreType.DMA((n, 3)),
                        pltpu.SemaphoreType.DMA((n,))],
    )(*shards)


def _sibling_exchange(partials, small, name):
    n = len(partials)

    def body(*refs):
        ins, small_ref = refs[:n], refs[n]
        outs, small_all = refs[n + 1:2 * n + 1], refs[2 * n + 1]
        send_sems, recv_sems, s_send, s_recv, loc_sem = refs[2 * n + 2:]
        x, y, c = _place()
        me = 4 * x + 2 * y + c
        local = pltpu.make_async_copy(small_ref, small_all.at[me], loc_sem)
        local.start()
        big = [pltpu.make_async_remote_copy(
            src_ref=ins[a].at[:, _half(ins[a], 1 - c)], dst_ref=outs[a],
            send_sem=send_sems.at[a], recv_sem=recv_sems.at[a],
            device_id=(x, y, 1 - c), device_id_type=MESH) for a in range(n)]
        tiny = []
        for d in range(1, N_DEV):
            px, py, pc = x ^ ((d >> 2) & 1), y ^ ((d >> 1) & 1), c ^ (d & 1)
            tiny.append(pltpu.make_async_remote_copy(
                src_ref=small_ref, dst_ref=small_all.at[me],
                send_sem=s_send.at[d - 1], recv_sem=s_recv.at[d - 1],
                device_id=(px, py, pc), device_id_type=MESH))
        for cp in tiny + big:
            cp.start()
        for d in range(1, N_DEV):
            peer = me ^ d
            pltpu.make_async_remote_copy(
                src_ref=small_ref, dst_ref=small_all.at[peer],
                send_sem=s_send.at[d - 1], recv_sem=s_recv.at[d - 1],
                device_id=(x, y, c), device_id_type=MESH).wait_recv()
        for cp in big:
            cp.wait_recv()
        for cp in tiny + big:
            cp.wait_send()
        local.wait()

    return pl.pallas_call(
        body, name=name,
        in_specs=[HBM_SPEC] * (n + 1), out_specs=[HBM_SPEC] * (n + 1),
        out_shape=[jax.ShapeDtypeStruct((N_CHIPS, p.shape[1] // 2, p.shape[2]), F32)
                   for p in partials] + [jax.ShapeDtypeStruct((N_DEV,) + small.shape, F32)],
        scratch_shapes=[pltpu.SemaphoreType.DMA((n,)), pltpu.SemaphoreType.DMA((n,)),
                        pltpu.SemaphoreType.DMA((N_DEV - 1,)), pltpu.SemaphoreType.DMA((N_DEV - 1,)),
                        pltpu.SemaphoreType.DMA],
    )(*partials, small)


def _chip_sum(partial, from_sibling, c, name, tr=256):
    _, hr, C = from_sibling.shape
    nb = hr // tr

    def body(c_ref, p_ref, s_ref, o_ref):
        o_ref[...] = (p_ref[...] + s_ref[...]).astype(BF16)

    return pl.pallas_call(
        body, name=name,
        grid_spec=pltpu.PrefetchScalarGridSpec(
            num_scalar_prefetch=1, grid=(N_CHIPS, nb),
            in_specs=[pl.BlockSpec((1, tr, C), lambda j, i, c_ref: (j, c_ref[0] * nb + i, 0)),
                      pl.BlockSpec((1, tr, C), lambda j, i, c_ref: (j, i, 0))],
            out_specs=pl.BlockSpec((1, tr, C), lambda j, i, c_ref: (j, i, 0))),
        out_shape=jax.ShapeDtypeStruct(from_sibling.shape, BF16),
        compiler_params=_params(2),
    )(c, partial, from_sibling)


def _send_chip_sums(sums, name):
    n = len(sums)

    def body(*refs):
        ins, outs = refs[:n], refs[n:2 * n]
        send_sems, recv_sems = refs[2 * n:]
        x, y, c = _place()
        me = 2 * x + y
        chips = _other_chips(x, y)
        sends = [pltpu.make_async_remote_copy(
            src_ref=ins[a].at[2 * chips[k][0] + chips[k][1]], dst_ref=outs[a].at[k],
            send_sem=send_sems.at[a, k], recv_sem=recv_sems.at[a, k],
            device_id=(*chips[k], c), device_id_type=MESH) for a in range(n) for k in range(3)]
        for cp in sends:
            cp.start()
        for a in range(n):
            for k in range(3):
                pltpu.make_async_remote_copy(
                    src_ref=ins[a].at[me], dst_ref=outs[a].at[k],
                    send_sem=send_sems.at[a, k], recv_sem=recv_sems.at[a, k],
                    device_id=(x, y, c), device_id_type=MESH).wait_recv()
        for cp in sends:
            cp.wait_send()

    return pl.pallas_call(
        body, name=name,
        in_specs=[HBM_SPEC] * n, out_specs=[HBM_SPEC] * n,
        out_shape=[jax.ShapeDtypeStruct((3,) + s.shape[1:], BF16) for s in sums],
        scratch_shapes=[pltpu.SemaphoreType.DMA((n, 3)), pltpu.SemaphoreType.DMA((n, 3))],
    )(*sums)


def _reduce_half(partial, from_sibling, received, place, name, tr=256):
    _, hr, C = from_sibling.shape
    nb = hr // tr

    def body(p_ref, mine_ref, sib_ref, r_ref, o_ref):
        acc = mine_ref[0] + sib_ref[0]
        for k in range(3):
            acc = acc + r_ref[k].astype(F32)
        o_ref[...] = acc

    return pl.pallas_call(
        body, name=name,
        grid_spec=pltpu.PrefetchScalarGridSpec(
            num_scalar_prefetch=1, grid=(nb,),
            in_specs=[pl.BlockSpec((1, tr, C), lambda i, p: (p[0], p[1] * nb + i, 0)),
                      pl.BlockSpec((1, tr, C), lambda i, p: (p[0], i, 0)),
                      pl.BlockSpec((3, tr, C), lambda i, p: (0, i, 0))],
            out_specs=pl.BlockSpec((tr, C), lambda i, p: (i, 0))),
        out_shape=jax.ShapeDtypeStruct((hr, C), F32),
        compiler_params=_params(1),
    )(place, partial, from_sibling, received)


def _join_halves(halves, name):
    n = len(halves)

    def body(*refs):
        ins, outs = refs[:n], refs[n:2 * n]
        send_sems, recv_sems, loc_sems = refs[2 * n:]
        x, y, c = _place()
        local = [pltpu.make_async_copy(ins[a], outs[a].at[_half(outs[a], c)], loc_sems.at[a])
                 for a in range(n)]
        sends = [pltpu.make_async_remote_copy(
            src_ref=ins[a], dst_ref=outs[a].at[_half(outs[a], c)],
            send_sem=send_sems.at[a], recv_sem=recv_sems.at[a],
            device_id=(x, y, 1 - c), device_id_type=MESH) for a in range(n)]
        for cp in local + sends:
            cp.start()
        for a in range(n):
            pltpu.make_async_remote_copy(
                src_ref=ins[a], dst_ref=outs[a].at[_half(outs[a], 1 - c)],
                send_sem=send_sems.at[a], recv_sem=recv_sems.at[a],
                device_id=(x, y, c), device_id_type=MESH).wait_recv()
        for cp in sends:
            cp.wait_send()
        for cp in local:
            cp.wait()

    return pl.pallas_call(
        body, name=name,
        in_specs=[HBM_SPEC] * n, out_specs=[HBM_SPEC] * n,
        out_shape=[jax.ShapeDtypeStruct((2 * h.shape[0], h.shape[1]), F32) for h in halves],
        scratch_shapes=[pltpu.SemaphoreType.DMA((n,)), pltpu.SemaphoreType.DMA((n,)),
                        pltpu.SemaphoreType.DMA((n,))],
    )(*halves)


def _adamw_math(w, g, m, v):
    m = ADAM_B1 * m + (1.0 - ADAM_B1) * g
    v = ADAM_B2 * v + (1.0 - ADAM_B2) * (g * g)
    m_hat = m / (1.0 - ADAM_B1 ** ADAM_STEP)
    v_hat = v / (1.0 - ADAM_B2 ** ADAM_STEP)
    delta = -ADAM_LR * (m_hat / (jnp.sqrt(v_hat) + ADAM_EPS) + ADAM_WD * w)
    return delta, m, v


def _adamw(w, g, m, v, name, tr=256):
    R, C = w.shape
    tr = min(tr, R)

    def body(w_ref, g_ref, m_ref, v_ref, d_out, m_out, v_out):
        d_out[...], m_out[...], v_out[...] = _adamw_math(w_ref[...], g_ref[...], m_ref[...], v_ref[...])

    spec = pl.BlockSpec((tr, C), lambda i: (i, 0))
    return pl.pallas_call(
        body, name=name, grid=(R // tr,),
        in_specs=[spec] * 4, out_specs=[spec] * 3,
        out_shape=[jax.ShapeDtypeStruct((R, C), F32)] * 3,
        compiler_params=_params(1),
    )(w, g, m, v)


def _adamw_small(small_all, w, m, v, name):
    def body(s_ref, w_ref, m_ref, v_ref, g_out, d_out, m_out, v_out):
        g = s_ref[0]
        for d in range(1, N_DEV):
            g = g + s_ref[d]
        g_out[...] = g
        d_out[...], m_out[...], v_out[...] = _adamw_math(w_ref[...], g, m_ref[...], v_ref[...])

    vm = pl.BlockSpec(memory_space=pltpu.VMEM)
    return pl.pallas_call(
        body, name=name, in_specs=[vm] * 4, out_specs=[vm] * 4,
        out_shape=[jax.ShapeDtypeStruct(w.shape, F32)] * 4,
    )(small_all, w, m, v)


def _pack_small(norm_g, pool_scale, norm_f, extra_row):
    return jnp.concatenate([norm_g.reshape(2, D_MODEL), pool_scale.reshape(2, D_MODEL),
                            norm_f.reshape(1, D_MODEL), extra_row,
                            jnp.zeros((2, D_MODEL), F32)], axis=0)


def kernel(x, norm_g, pool_w_in, pool_w, pool_scale, pool_w_out, sb_w_in, sb_w_out, norm_f, loss_target, m_norm_g, m_pool_w_in, m_pool_w, m_pool_scale, m_pool_w_out, m_sb_w_in, m_sb_w_out, m_norm_f, v_norm_g, v_pool_w_in, v_pool_w, v_pool_scale, v_pool_w_out, v_sb_w_in, v_sb_w_out, v_norm_f):
    nb, S, _ = x.shape
    T = nb * S
    xt = x.reshape(T, D_MODEL)
    target = loss_target.reshape(T, D_MODEL)
    cx, cy, cc = _place()

    def shard2d(w):
        return w.reshape(-1, w.shape[-1])

    names = ("pool_w_in", "pool_w", "pool_w_out", "sb_w_in", "sb_w_out")
    w_shards = [shard2d(w) for w in (pool_w_in, pool_w, pool_w_out, sb_w_in, sb_w_out)]
    m_shards = [shard2d(w) for w in (m_pool_w_in, m_pool_w, m_pool_w_out, m_sb_w_in, m_sb_w_out)]
    v_shards = [shard2d(w) for w in (v_pool_w_in, v_pool_w, v_pool_w_out, v_sb_w_in, v_sb_w_out)]

    w_pin, w_g, w_pout, w_sin, w_sout = _allgather_weights(
        [w.astype(BF16) for w in w_shards], "allgather_weights")
    w_pout = w_pout.reshape(D_INNER, D_MODEL)
    w_sout = w_sout.reshape(D_INNER, D_MODEL)
    g0, g1, gf = norm_g[0:1], norm_g[1:2], norm_f.reshape(1, D_MODEL)

    proj0, u0 = _rms_matmul(xt, g0, w_pin, 0, 2 * D_INNER, F32, "pool_in_proj", emit_u=True)
    y0, pooled, mixed = _pool_fwd(proj0, w_g, pool_scale, S, "pool_mix")
    h1 = _matmul_residual(y0, w_pout, xt, "pool_out_proj")
    qkv, u1 = _rms_matmul(h1, g1, w_sin, 0, 2 * QK_WIDTH + D_INNER, BF16, "sb_qkv_proj", emit_u=True)
    z1 = _rms_matmul(h1, g1, w_sin, 2 * QK_WIDTH + D_INNER, D_INNER, F32, "sb_gate_proj")
    o, y1, ltot = _attn_fwd(qkv, z1, S, "sb_attention")
    h2 = _matmul_residual(y1, w_sout, h1, "sb_out_proj")
    dh2, d_norm_f, loss_row = _loss_head(h2, gf, target, "loss_head")

    rows = lambda i, j, t: (i, j)
    gw_sout = _matmul_tn(y1, dh2, 0, D_INNER, 0, D_MODEL, (D_INNER, D_MODEL), (512, 512), rows,
                         "grad_sb_w_out")
    do, dproj1 = _attn_gate_bwd(dh2, w_sout, z1, o, "sb_gate_bwd")
    dproj1 = _attn_bwd(qkv, do, ltot, dproj1, S, "sb_attention_bwd")
    n1 = 2 * QK_WIDTH + 2 * D_INNER
    gw_sin = _matmul_tn(u1, dproj1, 0, D_MODEL, 0, n1, (N_CHIPS, D_MODEL, n1 // 4), (1, 512, 512),
                        lambda i, j, t: (j // 3, i, j % 3), "grad_sb_w_in", bm=512, bn=512)
    dh1, d_g1 = _matmul_nt_rms_bwd(dproj1, w_sin, h1, g1, dh2, "sb_in_bwd")
    gw_pout = _matmul_tn(y0, dh1, 0, D_INNER, 0, D_MODEL, (D_INNER, D_MODEL), (512, 512), rows,
                         "grad_pool_w_out")
    dmixed, dproj0, d_scale = _pool_gate_bwd(dh1, w_pout, proj0, mixed, pool_scale, "pool_gate_bwd")
    gw_g = [_matmul_tn(pooled, dmixed, g * GROUP_DIM, GROUP_DIM, g * GROUP_DIM, GROUP_DIM,
                       (N_CHIPS, GROUP_DIM // 4, GROUP_DIM), (1, 128, 512),
                       lambda i, j, t: (i, 0, 0), "grad_pool_w_%d" % g, bm=128, bn=512)
            for g in range(N_GROUPS)]
    gw_g = jnp.concatenate(gw_g, axis=1)
    dproj0 = _pool_bwd(dmixed, w_g, dproj0, S, "pool_bwd")
    n0 = 2 * D_INNER
    gw_pin = _matmul_tn(u0, dproj0, 0, D_MODEL, 0, n0, (N_CHIPS, D_MODEL, n0 // 4), (1, 512, 512),
                        lambda i, j, t: (j // 2, i, j % 2), "grad_pool_w_in")
    dx, d_g0 = _matmul_nt_rms_bwd(dproj0, w_pin, xt, g0, dh1, "pool_in_bwd")

    partials = [gw_pin, gw_g, gw_pout.reshape(N_CHIPS, -1, D_MODEL), gw_sin,
                gw_sout.reshape(N_CHIPS, -1, D_MODEL)]
    small = _pack_small(jnp.concatenate([d_g0, d_g1], axis=0), d_scale, d_norm_f,
                        jnp.broadcast_to(loss_row[:, :1], (1, D_MODEL)))
    *from_sibling, small_all = _sibling_exchange(partials, small, "grad_sibling_exchange")
    c_arr = cc.reshape(1).astype(jnp.int32)
    place = jnp.stack([2 * cx + cy, cc]).astype(jnp.int32)
    sums = [_chip_sum(p, s, c_arr, "grad_chip_sum_" + nm)
            for p, s, nm in zip(partials, from_sibling, names)]
    received = _send_chip_sums(sums, "grad_chip_exchange")
    halves = [_reduce_half(p, s, r, place, "grad_reduce_" + nm)
              for p, s, r, nm in zip(partials, from_sibling, received, names)]
    grads = _join_halves(halves, "grad_join_halves")

    deltas, new_m, new_v = [], [], []
    for w, g, m, v, nm in zip(w_shards, grads, m_shards, v_shards, names):
        d, mm, vv = _adamw(w, g, m, v, "adamw_" + nm)
        deltas.append(d)
        new_m.append(mm)
        new_v.append(vv)

    zero_row = jnp.zeros((1, D_MODEL), F32)
    g_small, d_small, m_small, v_small = _adamw_small(
        small_all, _pack_small(norm_g, pool_scale, norm_f, zero_row),
        _pack_small(m_norm_g, m_pool_scale, m_norm_f, zero_row),
        _pack_small(v_norm_g, v_pool_scale, v_norm_f, zero_row + 1.0), "adamw_small")
    loss = g_small[5, 0]

    def unpack_small(a):
        return a[0:2], a[2:4].reshape(1, D_INNER), a[4]

    def assemble(big, small3):
        ng, ps, nf = small3
        return [ng, big[0].reshape(pool_w_in.shape), big[1].reshape(pool_w.shape), ps,
                big[2].reshape(pool_w_out.shape), big[3].reshape(sb_w_in.shape),
                big[4].reshape(sb_w_out.shape), nf]

    return (loss, dx.reshape(x.shape),
            *assemble(grads, unpack_small(g_small)),
            *assemble(deltas, unpack_small(d_small)),
            *assemble(new_m, unpack_small(m_small)),
            *assemble(new_v, unpack_small(v_small)))
```

```python
import functools

import jax
import jax.numpy as jnp
from jax import lax
from jax.experimental import pallas as pl
from jax.experimental.pallas import tpu as pltpu

F32 = jnp.float32
BF16 = jnp.bfloat16
MESH = pl.DeviceIdType.MESH

D_MODEL = 1024
D_INNER = 2048
N_GROUPS = 4
GROUP_DIM = 512
HEAD_PAIR_QK = 128
HEAD_V = 128
N_HEAD_PAIRS = 8
QK_WIDTH = 1024
RMS_EPS = 1e-6
HALO = 16
N_CHIPS = 4
N_DEV = 8

ADAM_LR = 0.001
ADAM_B1 = 0.9
ADAM_B2 = 0.999
ADAM_EPS = 1e-08
ADAM_WD = 0.01
ADAM_STEP = 10

VMEM_LIMIT = 56 * 1024 * 1024

HBM_SPEC = pl.BlockSpec(memory_space=pltpu.HBM)


def _params(n_axes):
    return pltpu.CompilerParams(dimension_semantics=("arbitrary",) * n_axes,
                                vmem_limit_bytes=VMEM_LIMIT)


def _dot(a, b):
    return jnp.dot(a, b, preferred_element_type=F32)


def _dot_nt(a, b):
    return lax.dot_general(a, b, (((1,), (1,)), ((), ())), preferred_element_type=F32)


def _dot_tn(a, b):
    return lax.dot_general(a, b, (((0,), (0,)), ((), ())), preferred_element_type=F32)


def _sigmoid(z):
    return 1.0 / (1.0 + jnp.exp(-z))


def _rms_matmul(h, g_row, w4, col0, ncols, out_dtype, name, emit_u=False, tm=512, tn=512):
    T = h.shape[0]
    per_shard = w4.shape[2] // tn
    cb0 = col0 // tn

    def body(h_ref, g_ref, w_ref, *rest):
        if emit_u:
            o_ref, u_out, u_s = rest
        else:
            o_ref, u_s = rest

        @pl.when(pl.program_id(1) == 0)
        def _():
            x = h_ref[...]
            inv = lax.rsqrt(jnp.mean(x * x, axis=-1, keepdims=True) + RMS_EPS)
            u = (x * inv * g_ref[...]).astype(BF16)
            u_s[...] = u
            if emit_u:
                u_out[...] = u

        o_ref[...] = _dot(u_s[...], w_ref[0]).astype(out_dtype)

    out_shape = [jax.ShapeDtypeStruct((T, ncols), out_dtype)]
    out_specs = [pl.BlockSpec((tm, tn), lambda m, n: (m, n))]
    if emit_u:
        out_shape.append(jax.ShapeDtypeStruct((T, D_MODEL), BF16))
        out_specs.append(pl.BlockSpec((tm, D_MODEL), lambda m, n: (m, 0)))
    res = pl.pallas_call(
        body, name=name, grid=(T // tm, ncols // tn),
        in_specs=[pl.BlockSpec((tm, D_MODEL), lambda m, n: (m, 0)),
                  pl.BlockSpec((1, D_MODEL), lambda m, n: (0, 0)),
                  pl.BlockSpec((1, D_MODEL, tn),
                               lambda m, n: ((cb0 + n) // per_shard, 0, (cb0 + n) % per_shard))],
        out_specs=out_specs, out_shape=out_shape,
        scratch_shapes=[pltpu.VMEM((tm, D_MODEL), BF16)],
        compiler_params=_params(2),
    )(h, g_row, w4)
    return res if emit_u else res[0]


def _matmul_residual(a, w, res, name, tm=512, tn=512):
    T, K = a.shape
    N = w.shape[1]

    def body(a_ref, w_ref, r_ref, o_ref):
        o_ref[...] = r_ref[...] + _dot(a_ref[...], w_ref[...])

    return pl.pallas_call(
        body, name=name, grid=(T // tm, N // tn),
        in_specs=[pl.BlockSpec((tm, K), lambda m, n: (m, 0)),
                  pl.BlockSpec((K, tn), lambda m, n: (0, n)),
                  pl.BlockSpec((tm, tn), lambda m, n: (m, n))],
        out_specs=pl.BlockSpec((tm, tn), lambda m, n: (m, n)),
        out_shape=jax.ShapeDtypeStruct((T, N), F32),
        compiler_params=_params(2),
    )(a, w, res)


def _matmul_tn(a, b, a_col0, a_cols, b_col0, b_cols, out_shape, out_block, out_map, name,
               bm=512, bn=512, tk=512):
    T = a.shape[0]
    ab0, bb0 = a_col0 // bm, b_col0 // bn

    def body(a_ref, b_ref, o_ref):
        @pl.when(pl.program_id(2) == 0)
        def _():
            o_ref[...] = jnp.zeros_like(o_ref)

        part = _dot_tn(a_ref[...].astype(BF16), b_ref[...].astype(BF16))
        o_ref[...] += part.reshape(o_ref.shape)

    return pl.pallas_call(
        body, name=name, grid=(a_cols // bm, b_cols // bn, T // tk),
        in_specs=[pl.BlockSpec((tk, bm), lambda i, j, t: (t, ab0 + i)),
                  pl.BlockSpec((tk, bn), lambda i, j, t: (t, bb0 + j))],
        out_specs=pl.BlockSpec(out_block, out_map),
        out_shape=jax.ShapeDtypeStruct(out_shape, F32),
        compiler_params=_params(3),
    )(a, b)


def _matmul_nt_rms_bwd(dproj, w4, h, g_row, dres, name, tm=512, tk=512):
    T, cols = dproj.shape
    per_shard = w4.shape[2] // tk
    nk = cols // tk

    def body(dp_ref, w_ref, h_ref, g_ref, r_ref, dx_ref, dg_ref, acc):
        m, k = pl.program_id(0), pl.program_id(1)

        @pl.when(k == 0)
        def _():
            acc[...] = jnp.zeros_like(acc)

        @pl.when((k == 0) & (m == 0))
        def _():
            dg_ref[...] = jnp.zeros_like(dg_ref)

        acc[...] += _dot_nt(dp_ref[...], w_ref[0])

        @pl.when(k == nk - 1)
        def _():
            du = acc[...]
            x = h_ref[...]
            inv = lax.rsqrt(jnp.mean(x * x, axis=-1, keepdims=True) + RMS_EPS)
            xhat = x * inv
            dg_ref[...] += jnp.sum(du * xhat, axis=0, keepdims=True)
            dxh = du * g_ref[...]
            proj = jnp.mean(dxh * xhat, axis=-1, keepdims=True)
            dx_ref[...] = r_ref[...] + inv * (dxh - xhat * proj)

    return pl.pallas_call(
        body, name=name, grid=(T // tm, nk),
        in_specs=[pl.BlockSpec((tm, tk), lambda m, k: (m, k)),
                  pl.BlockSpec((1, D_MODEL, tk), lambda m, k: (k // per_shard, 0, k % per_shard)),
                  pl.BlockSpec((tm, D_MODEL), lambda m, k: (m, 0)),
                  pl.BlockSpec((1, D_MODEL), lambda m, k: (0, 0)),
                  pl.BlockSpec((tm, D_MODEL), lambda m, k: (m, 0))],
        out_specs=[pl.BlockSpec((tm, D_MODEL), lambda m, k: (m, 0)),
                   pl.BlockSpec((1, D_MODEL), lambda m, k: (0, 0))],
        out_shape=[jax.ShapeDtypeStruct((T, D_MODEL), F32),
                   jax.ShapeDtypeStruct((1, D_MODEL), F32)],
        scratch_shapes=[pltpu.VMEM((tm, D_MODEL), F32)],
        compiler_params=_params(2),
    )(dproj, w4, h, g_row, dres)


def _window_of(g):
    return jnp.left_shift(2, g)


def _select_stage(g, stages):
    res = stages[0]
    for i in range(1, len(stages)):
        res = jnp.where(g >= i, stages[i], res)
    return res


def _pool_fwd(proj0, wg4, scale_row, S, name, tm=256):
    T = proj0.shape[0]
    blocks_per_seq = S // tm
    hb = tm // HALO

    def body(x_ref, halo_ref, z_ref, w_ref, s_ref, y_ref, p_ref, mix_ref):
        m, g = pl.program_id(0), pl.program_id(1)
        first = (m % blocks_per_seq) == 0
        halo = jnp.where(first, 0.0, halo_ref[...])
        x = x_ref[...]
        ext = jnp.concatenate([halo, x], axis=0)
        stages = []
        cur = ext
        for sh in (1, 2, 4, 8):
            cur = cur + pltpu.roll(cur, sh, 0)
            stages.append(cur[HALO:, :])
        win_sum = _select_stage(g, stages)
        pos = (m % blocks_per_seq) * tm + lax.broadcasted_iota(jnp.int32, (tm, 1), 0)
        count = jnp.minimum(pos + 1, _window_of(g)).astype(F32)
        pooled = win_sum / count - x
        pooled_b = pooled.astype(BF16)
        mixed = _dot(pooled_b, w_ref[...].reshape(GROUP_DIM, GROUP_DIM))
        z = z_ref[...]
        y_ref[...] = (mixed * s_ref[...] * (z * _sigmoid(z))).astype(BF16)
        p_ref[...] = pooled_b
        mix_ref[...] = mixed

    blk = lambda m, g: (m, g)
    return pl.pallas_call(
        body, name=name, grid=(T // tm, N_GROUPS),
        in_specs=[pl.BlockSpec((tm, GROUP_DIM), blk),
                  pl.BlockSpec((HALO, GROUP_DIM), lambda m, g: (jnp.maximum(m * hb - 1, 0), g)),
                  pl.BlockSpec((tm, GROUP_DIM), lambda m, g: (m, N_GROUPS + g)),
                  pl.BlockSpec((N_CHIPS, GROUP_DIM // N_CHIPS, GROUP_DIM), lambda m, g: (0, g, 0)),
                  pl.BlockSpec((1, GROUP_DIM), lambda m, g: (0, g))],
        out_specs=[pl.BlockSpec((tm, GROUP_DIM), blk)] * 3,
        out_shape=[jax.ShapeDtypeStruct((T, D_INNER), BF16),
                   jax.ShapeDtypeStruct((T, D_INNER), BF16),
                   jax.ShapeDtypeStruct((T, D_INNER), F32)],
        compiler_params=_params(2),
    )(proj0, proj0, proj0, wg4, scale_row)


def _pool_gate_bwd(dh, w_out, proj0, mixed, scale_row, name, tm=512, tn=512):
    T = dh.shape[0]
    gate_b0 = D_INNER // tn

    def body(dh_ref, w_ref, z_ref, mix_ref, s_ref, dm_ref, dz_ref, ds_ref):
        @pl.when(pl.program_id(1) == 0)
        def _():
            ds_ref[...] = jnp.zeros_like(ds_ref)

        dy = _dot_nt(dh_ref[...].astype(BF16), w_ref[...])
        z = z_ref[...]
        sig = _sigmoid(z)
        silu = z * sig
        mixed = mix_ref[...]
        s = s_ref[...]
        dm_ref[...] = (dy * s * silu).astype(BF16)
        dz_ref[...] = (dy * mixed * s * (sig * (1.0 + z * (1.0 - sig)))).astype(BF16)
        ds_ref[...] += jnp.sum(dy * mixed * silu, axis=0, keepdims=True)

    return pl.pallas_call(
        body, name=name, grid=(D_INNER // tn, T // tm),
        in_specs=[pl.BlockSpec((tm, D_MODEL), lambda n, m: (m, 0)),
                  pl.BlockSpec((tn, D_MODEL), lambda n, m: (n, 0)),
                  pl.BlockSpec((tm, tn), lambda n, m: (m, gate_b0 + n)),
                  pl.BlockSpec((tm, tn), lambda n, m: (m, n)),
                  pl.BlockSpec((1, tn), lambda n, m: (0, n))],
        out_specs=[pl.BlockSpec((tm, tn), lambda n, m: (m, n)),
                   pl.BlockSpec((tm, tn), lambda n, m: (m, gate_b0 + n)),
                   pl.BlockSpec((1, tn), lambda n, m: (0, n))],
        out_shape=[jax.ShapeDtypeStruct((T, D_INNER), BF16),
                   jax.ShapeDtypeStruct((T, 2 * D_INNER), BF16),
                   jax.ShapeDtypeStruct((1, D_INNER), F32)],
        compiler_params=_params(2),
    )(dh, w_out, proj0, mixed, scale_row)


def _pool_bwd(dmixed, wg4, dproj0, S, name, tm=256):
    T = dmixed.shape[0]
    blocks_per_seq = S // tm
    hb = tm // HALO
    n_halo_blocks = T // HALO

    def body(dm_ref, halo_ref, w_ref, _, o_ref):
        m, g = pl.program_id(0), pl.program_id(1)
        ext = jnp.concatenate([dm_ref[...], halo_ref[...]], axis=0)
        dp = _dot_nt(ext, w_ref[...].reshape(GROUP_DIM, GROUP_DIM))
        pos = (m % blocks_per_seq) * tm + lax.broadcasted_iota(jnp.int32, (tm + HALO, 1), 0)
        count = jnp.minimum(pos + 1, _window_of(g)).astype(F32)
        c = jnp.where(pos < S, dp / count, 0.0)
        n = tm + HALO
        stages = []
        cur = c
        for sh in (1, 2, 4, 8):
            cur = cur + pltpu.roll(cur, n - sh, 0)
            stages.append(cur[:tm, :])
        o_ref[...] = (_select_stage(g, stages) - dp[:tm, :]).astype(BF16)

    blk = lambda m, g: (m, g)
    return pl.pallas_call(
        body, name=name, grid=(T // tm, N_GROUPS),
        in_specs=[pl.BlockSpec((tm, GROUP_DIM), blk),
                  pl.BlockSpec((HALO, GROUP_DIM),
                               lambda m, g: (jnp.minimum((m + 1) * hb, n_halo_blocks - 1), g)),
                  pl.BlockSpec((N_CHIPS, GROUP_DIM // N_CHIPS, GROUP_DIM), lambda m, g: (0, g, 0)),
                  HBM_SPEC],
        out_specs=pl.BlockSpec((tm, GROUP_DIM), blk),
        out_shape=jax.ShapeDtypeStruct(dproj0.shape, dproj0.dtype),
        input_output_aliases={3: 0},
        compiler_params=_params(2),
    )(dmixed, dmixed, wg4, dproj0)


TQ = 256


def _split_dot(x, m):
    hi = x.astype(BF16)
    lo = (x - hi.astype(F32)).astype(BF16)
    return _dot(hi, m) + _dot(lo, m)


NEG_LOG2E = -1.4426950408889634


def _log_terms(z):
    soft = jnp.log(1.0 + jnp.exp2(jnp.abs(z) * NEG_LOG2E))
    log_beta = jnp.minimum(z, 0.0) - soft
    return log_beta, log_beta - z


def _head_masks():
    lane = lax.broadcasted_iota(jnp.int32, (1, HEAD_PAIR_QK), 1)
    return [(lane // 64) == hh for hh in range(2)]


def _attn_fwd(qkv, z1, S, name):
    T = qkv.shape[0]
    nq = S // TQ
    k_b0 = QK_WIDTH // HEAD_PAIR_QK
    v_b0 = 2 * QK_WIDTH // (2 * HEAD_V)

    def body(q_ref, k_ref, v_ref, z_ref, o_ref, y_ref, lt_ref):
        qi = pl.program_id(2)
        row = lax.broadcasted_iota(jnp.int32, (TQ, TQ), 0)
        col = lax.broadcasted_iota(jnp.int32, (TQ, TQ), 1)
        causal = col < row
        later_in_block = (row > col).astype(BF16)
        q = q_ref[...]
        qms = [jnp.where(hm, q, jnp.zeros_like(q)) * 0.125 for hm in _head_masks()]

        def step(j, carry, diagonal):
            koff = pl.multiple_of(j * TQ, TQ)
            kb = k_ref[pl.ds(koff, TQ), :]
            hs = range(2)
            run, acc = [carry[2 * hh] for hh in hs], [carry[2 * hh + 1] for hh in hs]
            z = [_dot_nt(qms[hh], kb) for hh in hs]
            terms = [_log_terms(z[hh]) for hh in hs]
            log_om = [jnp.where(causal, t[1], 0.0) if diagonal else t[1] for t in terms]
            later = [_split_dot(log_om[hh], later_in_block) for hh in hs]
            a = [jnp.exp(terms[hh][0] + (run[hh] + later[hh])) for hh in hs]
            if diagonal:
                a = [jnp.where(causal, a[hh], 0.0) for hh in hs]
            out = []
            for hh in hs:
                vb = v_ref[pl.ds(koff, TQ), hh * HEAD_V:(hh + 1) * HEAD_V]
                out += [run[hh] + jnp.sum(log_om[hh], axis=1, keepdims=True),
                        acc[hh] + _dot(a[hh].astype(BF16), vb)]
            return tuple(out)

        zero = (jnp.zeros((TQ, 1), F32), jnp.zeros((TQ, HEAD_V), F32))
        carry = step(qi, zero + zero, True)
        carry = lax.fori_loop(0, qi, lambda i, c: step(qi - 1 - i, c, False), carry)
        for hh in range(2):
            sl = slice(hh * HEAD_V, (hh + 1) * HEAD_V)
            acc = carry[2 * hh + 1]
            z = z_ref[:, sl]
            o_ref[:, sl] = acc
            y_ref[:, sl] = (acc * (z * _sigmoid(z))).astype(BF16)
            lt_ref[:, hh:hh + 1] = carry[2 * hh]

    qblk = lambda b, p, i: (b * nq + i, p)
    return pl.pallas_call(
        body, name=name, grid=(T // S, N_HEAD_PAIRS, nq),
        in_specs=[pl.BlockSpec((TQ, HEAD_PAIR_QK), qblk),
                  pl.BlockSpec((S, HEAD_PAIR_QK), lambda b, p, i: (b, k_b0 + p)),
                  pl.BlockSpec((S, 2 * HEAD_V), lambda b, p, i: (b, v_b0 + p)),
                  pl.BlockSpec((TQ, 2 * HEAD_V), qblk)],
        out_specs=[pl.BlockSpec((TQ, 2 * HEAD_V), qblk),
                   pl.BlockSpec((TQ, 2 * HEAD_V), qblk),
                   pl.BlockSpec((None, TQ, 2), lambda b, p, i: (p, b * nq + i, 0))],
        out_shape=[jax.ShapeDtypeStruct((T, D_INNER), F32),
                   jax.ShapeDtypeStruct((T, D_INNER), BF16),
                   jax.ShapeDtypeStruct((N_HEAD_PAIRS, T, 2), F32)],
        compiler_params=_params(3),
    )(qkv, qkv, qkv, z1)


def _attn_gate_bwd(dh, w_out, z1, o, name, tm=512, tn=512):
    T = dh.shape[0]
    gate_b0 = (2 * QK_WIDTH + D_INNER) // tn

    def body(dh_ref, w_ref, z_ref, o_ref, do_ref, dz_ref, dh_s):
        @pl.when(pl.program_id(1) == 0)
        def _():
            dh_s[...] = dh_ref[...].astype(BF16)

        dy = _dot_nt(dh_s[...], w_ref[...])
        z = z_ref[...]
        sig = _sigmoid(z)
        do_ref[...] = (dy * (z * sig)).astype(BF16)
        dz_ref[...] = (dy * o_ref[...] * (sig * (1.0 + z * (1.0 - sig)))).astype(BF16)

    return pl.pallas_call(
        body, name=name, grid=(T // tm, D_INNER // tn),
        in_specs=[pl.BlockSpec((tm, D_MODEL), lambda m, n: (m, 0)),
                  pl.BlockSpec((tn, D_MODEL), lambda m, n: (n, 0)),
                  pl.BlockSpec((tm, tn), lambda m, n: (m, n)),
                  pl.BlockSpec((tm, tn), lambda m, n: (m, n))],
        out_specs=[pl.BlockSpec((tm, tn), lambda m, n: (m, n)),
                   pl.BlockSpec((tm, tn), lambda m, n: (m, gate_b0 + n))],
        out_shape=[jax.ShapeDtypeStruct((T, D_INNER), BF16),
                   jax.ShapeDtypeStruct((T, 2 * QK_WIDTH + 2 * D_INNER), BF16)],
        scratch_shapes=[pltpu.VMEM((tm, D_MODEL), BF16)],
        compiler_params=_params(2),
    )(dh, w_out, z1, o)


def _attn_bwd(qkv, do, ltot, dproj1, S, name):
    T = qkv.shape[0]
    nq = S // TQ
    k_b0 = QK_WIDTH // HEAD_PAIR_QK
    v_b0 = 2 * QK_WIDTH // (2 * HEAD_V)

    def body(q_ref, k_ref, v_ref, do_ref, lt_ref, _, out_ref, dq_s, dk_s, dv_s, dkb_s, dvb_s, sems):
        b, p = pl.program_id(0), pl.program_id(1)
        row = lax.broadcasted_iota(jnp.int32, (TQ, TQ), 0)
        col = lax.broadcasted_iota(jnp.int32, (TQ, TQ), 1)
        causal = col < row
        upto = (row <= col).astype(BF16)
        before = (row < col).astype(BF16)
        heads = _head_masks()
        dk_s[...] = jnp.zeros_like(dk_s)
        dv_s[...] = jnp.zeros_like(dv_s)

        def q_block(qi, _):
            qoff = pl.multiple_of(qi * TQ, TQ)
            q = q_ref[pl.ds(qoff, TQ), :]
            qms = [jnp.where(hm, q, jnp.zeros_like(q)) * 0.125 for hm in heads]
            q_both = jnp.concatenate(qms, axis=0)
            vsl = [slice(hh * HEAD_V, (hh + 1) * HEAD_V) for hh in range(2)]
            do_h = [do_ref[pl.ds(qoff, TQ), sl] for sl in vsl]
            total = [lt_ref[pl.ds(qoff, TQ), hh:hh + 1] for hh in range(2)]

            def k_block(j, carry, diagonal):
                koff = pl.multiple_of(j * TQ, TQ)
                kb = k_ref[pl.ds(koff, TQ), :]
                hs = range(2)
                g_before = [carry[2 * hh] for hh in hs]
                lom_before = [carry[2 * hh + 1] for hh in hs]
                z = [_dot_nt(qms[hh], kb) for hh in hs]
                da = [_dot_nt(do_h[hh], v_ref[pl.ds(koff, TQ), vsl[hh]]) for hh in hs]
                terms = [_log_terms(z[hh]) for hh in hs]
                log_om = [jnp.where(causal, t[1], 0.0) if diagonal else t[1] for t in terms]
                prefix = [_split_dot(log_om[hh], upto) for hh in hs]
                a = [jnp.exp(terms[hh][0] + ((total[hh] - lom_before[hh]) - prefix[hh])) for hh in hs]
                if diagonal:
                    a = [jnp.where(causal, a[hh], 0.0) for hh in hs]
                g = [a[hh] * da[hh] for hh in hs]
                g_prefix = [_split_dot(g[hh], before) for hh in hs]
                out, dzs = [], []
                for hh in hs:
                    beta = jnp.exp(terms[hh][0])
                    g_excl = (g_before[hh] + g_prefix[hh]) * beta
                    if diagonal:
                        g_excl = jnp.where(causal, g_excl, 0.0)
                    dzs.append((g[hh] * (1.0 - beta) - g_excl).astype(BF16))
                    out += [g_before[hh] + jnp.sum(g[hh], axis=1, keepdims=True),
                            lom_before[hh] + jnp.sum(log_om[hh], axis=1, keepdims=True)]
                for hh in hs:
                    dv_s[pl.ds(koff, TQ), vsl[hh]] += _dot_tn(a[hh].astype(BF16), do_h[hh])
                k_both = jnp.concatenate([jnp.where(hm, kb, jnp.zeros_like(kb)) for hm in heads], axis=0)
                dq = carry[4] + _dot(jnp.concatenate(dzs, axis=1), k_both)
                dk_s[pl.ds(koff, TQ), :] += _dot_tn(jnp.concatenate(dzs, axis=0), q_both)
                return tuple(out) + (dq,)

            zero = jnp.zeros((TQ, 1), F32)
            carry = (zero, zero, zero, zero, jnp.zeros((TQ, HEAD_PAIR_QK), F32))
            carry = lax.fori_loop(0, qi, lambda j, c: k_block(j, c, False), carry)
            carry = k_block(qi, carry, True)
            dq_s[pl.ds(qoff, TQ), :] = (carry[4] * 0.125).astype(BF16)
            return 0

        lax.fori_loop(0, nq, q_block, 0)
        dkb_s[...] = dk_s[...].astype(BF16)
        dvb_s[...] = dv_s[...].astype(BF16)
        rows = pl.ds(pl.multiple_of(b * S, TQ), S)
        copies = [
            pltpu.make_async_copy(
                dq_s, out_ref.at[rows, pl.ds(pl.multiple_of(p * HEAD_PAIR_QK, 128), HEAD_PAIR_QK)],
                sems.at[0]),
            pltpu.make_async_copy(
                dkb_s, out_ref.at[rows, pl.ds(pl.multiple_of(QK_WIDTH + p * HEAD_PAIR_QK, 128),
                                              HEAD_PAIR_QK)], sems.at[1]),
            pltpu.make_async_copy(
                dvb_s, out_ref.at[rows, pl.ds(pl.multiple_of(2 * QK_WIDTH + p * 2 * HEAD_V, 128),
                                              2 * HEAD_V)], sems.at[2]),
        ]
        for cp in copies:
            cp.start()
        for cp in copies:
            cp.wait()

    return pl.pallas_call(
        body, name=name, grid=(T // S, N_HEAD_PAIRS),
        in_specs=[pl.BlockSpec((S, HEAD_PAIR_QK), lambda b, p: (b, p)),
                  pl.BlockSpec((S, HEAD_PAIR_QK), lambda b, p: (b, k_b0 + p)),
                  pl.BlockSpec((S, 2 * HEAD_V), lambda b, p: (b, v_b0 + p)),
                  pl.BlockSpec((S, 2 * HEAD_V), lambda b, p: (b, p)),
                  pl.BlockSpec((None, S, 2), lambda b, p: (p, b, 0)),
                  HBM_SPEC],
        out_specs=HBM_SPEC,
        out_shape=jax.ShapeDtypeStruct(dproj1.shape, dproj1.dtype),
        input_output_aliases={5: 0},
        scratch_shapes=[pltpu.VMEM((S, HEAD_PAIR_QK), BF16),
                        pltpu.VMEM((S, HEAD_PAIR_QK), F32),
                        pltpu.VMEM((S, 2 * HEAD_V), F32),
                        pltpu.VMEM((S, HEAD_PAIR_QK), BF16),
                        pltpu.VMEM((S, 2 * HEAD_V), BF16),
                        pltpu.SemaphoreType.DMA((3,))],
        compiler_params=_params(2),
    )(qkv, qkv, qkv, do, ltot, dproj1)


def _loss_head(h, g_row, target, name, tm=512):
    T = h.shape[0]

    def body(h_ref, g_ref, t_ref, dh_ref, dg_ref, loss_ref):
        @pl.when(pl.program_id(0) == 0)
        def _():
            dg_ref[...] = jnp.zeros_like(dg_ref)
            loss_ref[...] = jnp.zeros_like(loss_ref)

        x = h_ref[...]
        inv = lax.rsqrt(jnp.mean(x * x, axis=-1, keepdims=True) + RMS_EPS)
        xhat = x * inv
        gain = g_ref[...]
        err = xhat * gain - t_ref[...]
        per_token = jnp.mean(err * err, axis=-1, keepdims=True)
        loss_ref[...] += 0.5 * jnp.sum(per_token, axis=0, keepdims=True)
        dy = err * (1.0 / D_MODEL)
        dg_ref[...] += jnp.sum(dy * xhat, axis=0, keepdims=True)
        dxh = dy * gain
        proj = jnp.mean(dxh * xhat, axis=-1, keepdims=True)
        dh_ref[...] = inv * (dxh - xhat * proj)

    return pl.pallas_call(
        body, name=name, grid=(T // tm,),
        in_specs=[pl.BlockSpec((tm, D_MODEL), lambda m: (m, 0)),
                  pl.BlockSpec((1, D_MODEL), lambda m: (0, 0)),
                  pl.BlockSpec((tm, D_MODEL), lambda m: (m, 0))],
        out_specs=[pl.BlockSpec((tm, D_MODEL), lambda m: (m, 0)),
                   pl.BlockSpec((1, D_MODEL), lambda m: (0, 0)),
                   pl.BlockSpec((1, 128), lambda m: (0, 0))],
        out_shape=[jax.ShapeDtypeStruct((T, D_MODEL), F32),
                   jax.ShapeDtypeStruct((1, D_MODEL), F32),
                   jax.ShapeDtypeStruct((1, 128), F32)],
        compiler_params=_params(1),
    )(h, g_row, target)


def _place():
    return lax.axis_index("x"), lax.axis_index("y"), lax.axis_index("c")


def _other_chips(x, y):
    return [(1 - x, y), (x, 1 - y), (1 - x, 1 - y)]


def _half(ref, c):
    hr = ref.shape[-2] // 2
    return pl.ds(pl.multiple_of(c * hr, 8), hr)


def _allgather_weights(shards, name):
    n = len(shards)

    def body(*refs):
        ins, outs = refs[:n], refs[n:2 * n]
        send_sems, recv_sems, fwd_send, fwd_recv, loc_sems = refs[2 * n:]
        x, y, c = _place()
        me = 2 * x + y
        chips = _other_chips(x, y)
        local = [pltpu.make_async_copy(ins[a], outs[a].at[me], loc_sems.at[a]) for a in range(n)]
        for cp in local:
            cp.start()

        def landing(a, chip, half_of):
            return outs[a].at[2 * chip[0] + chip[1], _half(ins[a], half_of)]

        def ici(a, k, chip_from, to):
            return pltpu.make_async_remote_copy(
                src_ref=ins[a].at[_half(ins[a], c)], dst_ref=landing(a, chip_from, c),
                send_sem=send_sems.at[a, k], recv_sem=recv_sems.at[a, k],
                device_id=to, device_id_type=MESH)

        def d2d(a, k, chip_from, half_of):
            return pltpu.make_async_remote_copy(
                src_ref=landing(a, chip_from, half_of), dst_ref=landing(a, chip_from, half_of),
                send_sem=fwd_send.at[a, k], recv_sem=fwd_recv.at[a, k],
                device_id=(x, y, 1 - c), device_id_type=MESH)

        sends = [ici(a, k, (x, y), (*chips[k], c)) for a in range(n) for k in range(3)]
        for cp in sends:
            cp.start()
        forwards = []
        for a in range(n):
            for k in range(3):
                ici(a, k, chips[k], (x, y, c)).wait_recv()
                fw = d2d(a, k, chips[k], c)
                fw.start()
                forwards.append(fw)
        for a in range(n):
            for k in range(3):
                d2d(a, k, chips[k], 1 - c).wait_recv()
        for cp in sends + forwards:
            cp.wait_send()
        for cp in local:
            cp.wait()

    return pl.pallas_call(
        body, name=name,
        in_specs=[HBM_SPEC] * n, out_specs=[HBM_SPEC] * n,
        out_shape=[jax.ShapeDtypeStruct((N_CHIPS,) + s.shape, s.dtype) for s in shards],
        scratch_shapes=[pltpu.SemaphoreType.DMA((n, 3)), pltpu.SemaphoreType.DMA((n, 3)),
                        pltpu.SemaphoreType.DMA((n, 3)), pltpu.SemaphoreType.DMA((n, 3)),
                        pltpu.SemaphoreType.DMA((n,))],
    )(*shards)


def _sibling_exchange(partials, small, name):
    n = len(partials)

    def body(*refs):
        ins, small_ref = refs[:n], refs[n]
        outs, small_all = refs[n + 1:2 * n + 1], refs[2 * n + 1]
        send_sems, recv_sems, s_send, s_recv, loc_sem = refs[2 * n + 2:]
        x, y, c = _place()
        me = 4 * x + 2 * y + c
        local = pltpu.make_async_copy(small_ref, small_all.at[me], loc_sem)
        local.start()
        big = [pltpu.make_async_remote_copy(
            src_ref=ins[a].at[:, _half(ins[a], 1 - c)], dst_ref=outs[a],
            send_sem=send_sems.at[a], recv_sem=recv_sems.at[a],
            device_id=(x, y, 1 - c), device_id_type=MESH) for a in range(n)]
        tiny = []
        for d in range(1, N_DEV):
            px, py, pc = x ^ ((d >> 2) & 1), y ^ ((d >> 1) & 1), c ^ (d & 1)
            tiny.append(pltpu.make_async_remote_copy(
                src_ref=small_ref, dst_ref=small_all.at[me],
                send_sem=s_send.at[d - 1], recv_sem=s_recv.at[d - 1],
                device_id=(px, py, pc), device_id_type=MESH))
        for cp in tiny + big:
            cp.start()
        for d in range(1, N_DEV):
            peer = me ^ d
            pltpu.make_async_remote_copy(
                src_ref=small_ref, dst_ref=small_all.at[peer],
                send_sem=s_send.at[d - 1], recv_sem=s_recv.at[d - 1],
                device_id=(x, y, c), device_id_type=MESH).wait_recv()
        for cp in big:
            cp.wait_recv()
        for cp in tiny + big:
            cp.wait_send()
        local.wait()

    return pl.pallas_call(
        body, name=name,
        in_specs=[HBM_SPEC] * (n + 1), out_specs=[HBM_SPEC] * (n + 1),
        out_shape=[jax.ShapeDtypeStruct((N_CHIPS, p.shape[1] // 2, p.shape[2]), F32)
                   for p in partials] + [jax.ShapeDtypeStruct((N_DEV,) + small.shape, F32)],
        scratch_shapes=[pltpu.SemaphoreType.DMA((n,)), pltpu.SemaphoreType.DMA((n,)),
                        pltpu.SemaphoreType.DMA((N_DEV - 1,)), pltpu.SemaphoreType.DMA((N_DEV - 1,)),
                        pltpu.SemaphoreType.DMA],
    )(*partials, small)


def _chip_sum(partial, from_sibling, c, name, tr=256):
    _, hr, C = from_sibling.shape
    nb = hr // tr

    def body(c_ref, p_ref, s_ref, o_ref):
        o_ref[...] = (p_ref[...] + s_ref[...]).astype(BF16)

    return pl.pallas_call(
        body, name=name,
        grid_spec=pltpu.PrefetchScalarGridSpec(
            num_scalar_prefetch=1, grid=(N_CHIPS, nb),
            in_specs=[pl.BlockSpec((1, tr, C), lambda j, i, c_ref: (j, c_ref[0] * nb + i, 0)),
                      pl.BlockSpec((1, tr, C), lambda j, i, c_ref: (j, i, 0))],
            out_specs=pl.BlockSpec((1, tr, C), lambda j, i, c_ref: (j, i, 0))),
        out_shape=jax.ShapeDtypeStruct(from_sibling.shape, BF16),
        compiler_params=_params(2),
    )(c, partial, from_sibling)


def _send_chip_sums(sums, name):
    n = len(sums)

    def body(*refs):
        ins, outs = refs[:n], refs[n:2 * n]
        send_sems, recv_sems = refs[2 * n:]
        x, y, c = _place()
        me = 2 * x + y
        chips = _other_chips(x, y)
        sends = [pltpu.make_async_remote_copy(
            src_ref=ins[a].at[2 * chips[k][0] + chips[k][1]], dst_ref=outs[a].at[k],
            send_sem=send_sems.at[a, k], recv_sem=recv_sems.at[a, k],
            device_id=(*chips[k], c), device_id_type=MESH) for a in range(n) for k in range(3)]
        for cp in sends:
            cp.start()
        for a in range(n):
            for k in range(3):
                pltpu.make_async_remote_copy(
                    src_ref=ins[a].at[me], dst_ref=outs[a].at[k],
                    send_sem=send_sems.at[a, k], recv_sem=recv_sems.at[a, k],
                    device_id=(x, y, c), device_id_type=MESH).wait_recv()
        for cp in sends:
            cp.wait_send()

    return pl.pallas_call(
        body, name=name,
        in_specs=[HBM_SPEC] * n, out_specs=[HBM_SPEC] * n,
        out_shape=[jax.ShapeDtypeStruct((3,) + s.shape[1:], BF16) for s in sums],
        scratch_shapes=[pltpu.SemaphoreType.DMA((n, 3)), pltpu.SemaphoreType.DMA((n, 3))],
    )(*sums)


def _reduce_half(partial, from_sibling, received, place, name, tr=256):
    _, hr, C = from_sibling.shape
    nb = hr // tr

    def body(p_ref, mine_ref, sib_ref, r_ref, o_ref):
        acc = mine_ref[0] + sib_ref[0]
        for k in range(3):
            acc = acc + r_ref[k].astype(F32)
        o_ref[...] = acc

    return pl.pallas_call(
        body, name=name,
        grid_spec=pltpu.PrefetchScalarGridSpec(
            num_scalar_prefetch=1, grid=(nb,),
            in_specs=[pl.BlockSpec((1, tr, C), lambda i, p: (p[0], p[1] * nb + i, 0)),
                      pl.BlockSpec((1, tr, C), lambda i, p: (p[0], i, 0)),
                      pl.BlockSpec((3, tr, C), lambda i, p: (0, i, 0))],
            out_specs=pl.BlockSpec((tr, C), lambda i, p: (i, 0))),
        out_shape=jax.ShapeDtypeStruct((hr, C), F32),
        compiler_params=_params(1),
    )(place, partial, from_sibling, received)


def _join_halves(halves, name):
    n = len(halves)

    def body(*refs):
        ins, outs = refs[:n], refs[n:2 * n]
        send_sems, recv_sems, loc_sems = refs[2 * n:]
        x, y, c = _place()
        local = [pltpu.make_async_copy(ins[a], outs[a].at[_half(outs[a], c)], loc_sems.at[a])
                 for a in range(n)]
        sends = [pltpu.make_async_remote_copy(
            src_ref=ins[a], dst_ref=outs[a].at[_half(outs[a], c)],
            send_sem=send_sems.at[a], recv_sem=recv_sems.at[a],
            device_id=(x, y, 1 - c), device_id_type=MESH) for a in range(n)]
        for cp in local + sends:
            cp.start()
        for a in range(n):
            pltpu.make_async_remote_copy(
                src_ref=ins[a], dst_ref=outs[a].at[_half(outs[a], 1 - c)],
                send_sem=send_sems.at[a], recv_sem=recv_sems.at[a],
                device_id=(x, y, c), device_id_type=MESH).wait_recv()
        for cp in sends:
            cp.wait_send()
        for cp in local:
            cp.wait()

    return pl.pallas_call(
        body, name=name,
        in_specs=[HBM_SPEC] * n, out_specs=[HBM_SPEC] * n,
        out_shape=[jax.ShapeDtypeStruct((2 * h.shape[0], h.shape[1]), F32) for h in halves],
        scratch_shapes=[pltpu.SemaphoreType.DMA((n,)), pltpu.SemaphoreType.DMA((n,)),
                        pltpu.SemaphoreType.DMA((n,))],
    )(*halves)


def _adamw_math(w, g, m, v):
    m = ADAM_B1 * m + (1.0 - ADAM_B1) * g
    v = ADAM_B2 * v + (1.0 - ADAM_B2) * (g * g)
    m_hat = m / (1.0 - ADAM_B1 ** ADAM_STEP)
    v_hat = v / (1.0 - ADAM_B2 ** ADAM_STEP)
    delta = -ADAM_LR * (m_hat / (jnp.sqrt(v_hat) + ADAM_EPS) + ADAM_WD * w)
    return delta, m, v


def _adamw(w, g, m, v, name, tr=256):
    R, C = w.shape
    tr = min(tr, R)

    def body(w_ref, g_ref, m_ref, v_ref, d_out, m_out, v_out):
        d_out[...], m_out[...], v_out[...] = _adamw_math(w_ref[...], g_ref[...], m_ref[...], v_ref[...])

    spec = pl.BlockSpec((tr, C), lambda i: (i, 0))
    return pl.pallas_call(
        body, name=name, grid=(R // tr,),
        in_specs=[spec] * 4, out_specs=[spec] * 3,
        out_shape=[jax.ShapeDtypeStruct((R, C), F32)] * 3,
        compiler_params=_params(1),
    )(w, g, m, v)


def _adamw_small(small_all, w, m, v, name):
    def body(s_ref, w_ref, m_ref, v_ref, g_out, d_out, m_out, v_out):
        g = s_ref[0]
        for d in range(1, N_DEV):
            g = g + s_ref[d]
        g_out[...] = g
        d_out[...], m_out[...], v_out[...] = _adamw_math(w_ref[...], g, m_ref[...], v_ref[...])

    vm = pl.BlockSpec(memory_space=pltpu.VMEM)
    return pl.pallas_call(
        body, name=name, in_specs=[vm] * 4, out_specs=[vm] * 4,
        out_shape=[jax.ShapeDtypeStruct(w.shape, F32)] * 4,
    )(small_all, w, m, v)


def _pack_small(norm_g, pool_scale, norm_f, extra_row):
    return jnp.concatenate([norm_g.reshape(2, D_MODEL), pool_scale.reshape(2, D_MODEL),
                            norm_f.reshape(1, D_MODEL), extra_row,
                            jnp.zeros((2, D_MODEL), F32)], axis=0)


def kernel(x, norm_g, pool_w_in, pool_w, pool_scale, pool_w_out, sb_w_in, sb_w_out, norm_f, loss_target, m_norm_g, m_pool_w_in, m_pool_w, m_pool_scale, m_pool_w_out, m_sb_w_in, m_sb_w_out, m_norm_f, v_norm_g, v_pool_w_in, v_pool_w, v_pool_scale, v_pool_w_out, v_sb_w_in, v_sb_w_out, v_norm_f):
    nb, S, _ = x.shape
    T = nb * S
    xt = x.reshape(T, D_MODEL)
    target = loss_target.reshape(T, D_MODEL)
    cx, cy, cc = _place()

    def shard2d(w):
        return w.reshape(-1, w.shape[-1])

    names = ("pool_w_in", "pool_w", "pool_w_out", "sb_w_in", "sb_w_out")
    w_shards = [shard2d(w) for w in (pool_w_in, pool_w, pool_w_out, sb_w_in, sb_w_out)]
    m_shards = [shard2d(w) for w in (m_pool_w_in, m_pool_w, m_pool_w_out, m_sb_w_in, m_sb_w_out)]
    v_shards = [shard2d(w) for w in (v_pool_w_in, v_pool_w, v_pool_w_out, v_sb_w_in, v_sb_w_out)]

    w_pin, w_g, w_pout, w_sin, w_sout = _allgather_weights(
        [w.astype(BF16) for w in w_shards], "allgather_weights")
    w_pout = w_pout.reshape(D_INNER, D_MODEL)
    w_sout = w_sout.reshape(D_INNER, D_MODEL)
    g0, g1, gf = norm_g[0:1], norm_g[1:2], norm_f.reshape(1, D_MODEL)

    proj0, u0 = _rms_matmul(xt, g0, w_pin, 0, 2 * D_INNER, F32, "pool_in_proj", emit_u=True)
    y0, pooled, mixed = _pool_fwd(proj0, w_g, pool_scale, S, "pool_mix")
    h1 = _matmul_residual(y0, w_pout, xt, "pool_out_proj")
    qkv, u1 = _rms_matmul(h1, g1, w_sin, 0, 2 * QK_WIDTH + D_INNER, BF16, "sb_qkv_proj", emit_u=True)
    z1 = _rms_matmul(h1, g1, w_sin, 2 * QK_WIDTH + D_INNER, D_INNER, F32, "sb_gate_proj")
    o, y1, ltot = _attn_fwd(qkv, z1, S, "sb_attention")
    h2 = _matmul_residual(y1, w_sout, h1, "sb_out_proj")
    dh2, d_norm_f, loss_row = _loss_head(h2, gf, target, "loss_head")

    rows = lambda i, j, t: (i, j)
    gw_sout = _matmul_tn(y1, dh2, 0, D_INNER, 0, D_MODEL, (D_INNER, D_MODEL), (512, 512), rows,
                         "grad_sb_w_out")
    do, dproj1 = _attn_gate_bwd(dh2, w_sout, z1, o, "sb_gate_bwd")
    dproj1 = _attn_bwd(qkv, do, ltot, dproj1, S, "sb_attention_bwd")
    n1 = 2 * QK_WIDTH + 2 * D_INNER
    gw_sin = _matmul_tn(u1, dproj1, 0, D_MODEL, 0, n1, (N_CHIPS, D_MODEL, n1 // 4), (1, 512, 512),
                        lambda i, j, t: (j // 3, i, j % 3), "grad_sb_w_in", bm=512, bn=512)
    dh1, d_g1 = _matmul_nt_rms_bwd(dproj1, w_sin, h1, g1, dh2, "sb_in_bwd")
    gw_pout = _matmul_tn(y0, dh1, 0, D_INNER, 0, D_MODEL, (D_INNER, D_MODEL), (512, 512), rows,
                         "grad_pool_w_out")
    dmixed, dproj0, d_scale = _pool_gate_bwd(dh1, w_pout, proj0, mixed, pool_scale, "pool_gate_bwd")
    gw_g = [_matmul_tn(pooled, dmixed, g * GROUP_DIM, GROUP_DIM, g * GROUP_DIM, GROUP_DIM,
                       (N_CHIPS, GROUP_DIM // 4, GROUP_DIM), (1, 128, 512),
                       lambda i, j, t: (i, 0, 0), "grad_pool_w_%d" % g, bm=128, bn=512)
            for g in range(N_GROUPS)]
    gw_g = jnp.concatenate(gw_g, axis=1)
    dproj0 = _pool_bwd(dmixed, w_g, dproj0, S, "pool_bwd")
    n0 = 2 * D_INNER
    gw_pin = _matmul_tn(u0, dproj0, 0, D_MODEL, 0, n0, (N_CHIPS, D_MODEL, n0 // 4), (1, 512, 512),
                        lambda i, j, t: (j // 2, i, j % 2), "grad_pool_w_in")
    dx, d_g0 = _matmul_nt_rms_bwd(dproj0, w_pin, xt, g0, dh1, "pool_in_bwd")

    partials = [gw_pin, gw_g, gw_pout.reshape(N_CHIPS, -1, D_MODEL), gw_sin,
                gw_sout.reshape(N_CHIPS, -1, D_MODEL)]
    small = _pack_small(jnp.concatenate([d_g0, d_g1], axis=0), d_scale, d_norm_f,
                        jnp.broadcast_to(loss_row[:, :1], (1, D_MODEL)))
    *from_sibling, small_all = _sibling_exchange(partials, small, "grad_sibling_exchange")
    c_arr = cc.reshape(1).astype(jnp.int32)
    place = jnp.stack([2 * cx + cy, cc]).astype(jnp.int32)
    sums = [_chip_sum(p, s, c_arr, "grad_chip_sum_" + nm)
            for p, s, nm in zip(partials, from_sibling, names)]
    received = _send_chip_sums(sums, "grad_chip_exchange")
    halves = [_reduce_half(p, s, r, place, "grad_reduce_" + nm)
              for p, s, r, nm in zip(partials, from_sibling, received, names)]
    grads = _join_halves(halves, "grad_join_halves")

    deltas, new_m, new_v = [], [], []
    for w, g, m, v, nm in zip(w_shards, grads, m_shards, v_shards, names):
        d, mm, vv = _adamw(w, g, m, v, "adamw_" + nm)
        deltas.append(d)
        new_m.append(mm)
        new_v.append(vv)

    zero_row = jnp.zeros((1, D_MODEL), F32)
    g_small, d_small, m_small, v_small = _adamw_small(
        small_all, _pack_small(norm_g, pool_scale, norm_f, zero_row),
        _pack_small(m_norm_g, m_pool_scale, m_norm_f, zero_row),
        _pack_small(v_norm_g, v_pool_scale, v_norm_f, zero_row + 1.0), "adamw_small")
    loss = g_small[5, 0]

    def unpack_small(a):
        return a[0:2], a[2:4].reshape(1, D_INNER), a[4]

    def assemble(big, small3):
        ng, ps, nf = small3
        return [ng, big[0].reshape(pool_w_in.shape), big[1].reshape(pool_w.shape), ps,
                big[2].reshape(pool_w_out.shape), big[3].reshape(sb_w_in.shape),
                big[4].reshape(sb_w_out.shape), nf]

    return (loss, dx.reshape(x.shape),
            *assemble(grads, unpack_small(g_small)),
            *assemble(deltas, unpack_small(d_small)),
            *assemble(new_m, unpack_small(m_small)),
            *assemble(new_v, unpack_small(v_small)))
```

```python
import functools

import jax
import jax.numpy as jnp
from jax import lax
from jax.experimental import pallas as pl
from jax.experimental.pallas import tpu as pltpu

F32 = jnp.float32
BF16 = jnp.bfloat16
MESH = pl.DeviceIdType.MESH

D_MODEL = 1024
D_INNER = 2048
N_GROUPS = 4
GROUP_DIM = 512
HEAD_PAIR_QK = 128
HEAD_V = 128
N_HEAD_PAIRS = 8
QK_WIDTH = 1024
RMS_EPS = 1e-6
HALO = 16
N_CHIPS = 4
N_DEV = 8

ADAM_LR = 0.001
ADAM_B1 = 0.9
ADAM_B2 = 0.999
ADAM_EPS = 1e-08
ADAM_WD = 0.01
ADAM_STEP = 10

VMEM_LIMIT = 56 * 1024 * 1024

HBM_SPEC = pl.BlockSpec(memory_space=pltpu.HBM)


def _params(n_axes):
    return pltpu.CompilerParams(dimension_semantics=("arbitrary",) * n_axes,
                                vmem_limit_bytes=VMEM_LIMIT)


def _dot(a, b):
    return jnp.dot(a, b, preferred_element_type=F32)


def _dot_nt(a, b):
    return lax.dot_general(a, b, (((1,), (1,)), ((), ())), preferred_element_type=F32)


def _dot_tn(a, b):
    return lax.dot_general(a, b, (((0,), (0,)), ((), ())), preferred_element_type=F32)


def _sigmoid(z):
    return 1.0 / (1.0 + jnp.exp(-z))


def _rms_matmul(h, g_row, w4, outs, name, tm=1024, tn=512):
    T = h.shape[0]
    per_shard = w4.shape[2] // tn
    starts = [0]
    for width, _ in outs:
        starts.append(starts[-1] + width // tn)
    n_out = len(outs)

    def body(h_ref, g_ref, w_ref, *rest):
        o_refs, u_out, u_s = rest[:n_out], rest[n_out], rest[n_out + 1]
        n = pl.program_id(1)

        @pl.when(n == 0)
        def _():
            x = h_ref[...]
            inv = lax.rsqrt(jnp.mean(x * x, axis=-1, keepdims=True) + RMS_EPS)
            u = (x * inv * g_ref[...]).astype(BF16)
            u_s[...] = u
            u_out[...] = u

        res = _dot(u_s[...], w_ref[0])
        for k in range(n_out):
            @pl.when((n >= starts[k]) & (n < starts[k + 1]))
            def _():
                o_refs[k][...] = res.astype(o_refs[k].dtype)

    def out_map(k):
        return lambda m, n: (m, jnp.clip(n - starts[k], 0, starts[k + 1] - starts[k] - 1))

    return pl.pallas_call(
        body, name=name, grid=(T // tm, starts[-1]),
        in_specs=[pl.BlockSpec((tm, D_MODEL), lambda m, n: (m, 0)),
                  pl.BlockSpec((1, D_MODEL), lambda m, n: (0, 0)),
                  pl.BlockSpec((1, D_MODEL, tn), lambda m, n: (n // per_shard, 0, n % per_shard))],
        out_specs=[pl.BlockSpec((tm, tn), out_map(k)) for k in range(n_out)]
        + [pl.BlockSpec((tm, D_MODEL), lambda m, n: (m, 0))],
        out_shape=[jax.ShapeDtypeStruct((T, width), dt) for width, dt in outs]
        + [jax.ShapeDtypeStruct((T, D_MODEL), BF16)],
        scratch_shapes=[pltpu.VMEM((tm, D_MODEL), BF16)],
        compiler_params=_params(2),
    )(h, g_row, w4)


def _matmul_residual(a, w, res, name, tm=1024, tn=512):
    T, K = a.shape
    N = w.shape[1]

    def body(a_ref, w_ref, r_ref, o_ref):
        o_ref[...] = r_ref[...] + _dot(a_ref[...], w_ref[...])

    return pl.pallas_call(
        body, name=name, grid=(T // tm, N // tn),
        in_specs=[pl.BlockSpec((tm, K), lambda m, n: (m, 0)),
                  pl.BlockSpec((K, tn), lambda m, n: (0, n)),
                  pl.BlockSpec((tm, tn), lambda m, n: (m, n))],
        out_specs=pl.BlockSpec((tm, tn), lambda m, n: (m, n)),
        out_shape=jax.ShapeDtypeStruct((T, N), F32),
        compiler_params=_params(2),
    )(a, w, res)


def _matmul_tn(a, b, a_cols, b_cols, out_shape, out_block, out_map, name, bm, bn, tk=512,
               diagonal_blocks=False):
    T = a.shape[0]

    def body(a_ref, b_ref, o_ref):
        @pl.when(pl.program_id(2) == 0)
        def _():
            o_ref[...] = jnp.zeros_like(o_ref)

        part = _dot_tn(a_ref[...].astype(BF16), b_ref[...].astype(BF16))
        o_ref[...] += part.reshape(o_ref.shape)

    b_map = (lambda i, j, t: (t, i)) if diagonal_blocks else (lambda i, j, t: (t, j))
    return pl.pallas_call(
        body, name=name, grid=(a_cols // bm, 1 if diagonal_blocks else b_cols // bn, T // tk),
        in_specs=[pl.BlockSpec((tk, bm), lambda i, j, t: (t, i)),
                  pl.BlockSpec((tk, bn), b_map)],
        out_specs=pl.BlockSpec(out_block, out_map),
        out_shape=jax.ShapeDtypeStruct(out_shape, F32),
        compiler_params=_params(3),
    )(a, b)


def _matmul_nt_rms_bwd(dproj, w4, h, g_row, dres, name, tm=1024, tk=512):
    T, cols = dproj.shape
    per_shard = w4.shape[2] // tk
    nk = cols // tk

    def body(dp_ref, w_ref, h_ref, g_ref, r_ref, dx_ref, dg_ref, acc):
        m, k = pl.program_id(0), pl.program_id(1)

        @pl.when(k == 0)
        def _():
            acc[...] = jnp.zeros_like(acc)

        @pl.when((k == 0) & (m == 0))
        def _():
            dg_ref[...] = jnp.zeros_like(dg_ref)

        acc[...] += _dot_nt(dp_ref[...], w_ref[0])

        @pl.when(k == nk - 1)
        def _():
            du = acc[...]
            x = h_ref[...]
            inv = lax.rsqrt(jnp.mean(x * x, axis=-1, keepdims=True) + RMS_EPS)
            xhat = x * inv
            dg_ref[...] += jnp.sum(du * xhat, axis=0, keepdims=True)
            dxh = du * g_ref[...]
            proj = jnp.mean(dxh * xhat, axis=-1, keepdims=True)
            dx_ref[...] = r_ref[...] + inv * (dxh - xhat * proj)

    return pl.pallas_call(
        body, name=name, grid=(T // tm, nk),
        in_specs=[pl.BlockSpec((tm, tk), lambda m, k: (m, k)),
                  pl.BlockSpec((1, D_MODEL, tk), lambda m, k: (k // per_shard, 0, k % per_shard)),
                  pl.BlockSpec((tm, D_MODEL), lambda m, k: (m, 0)),
                  pl.BlockSpec((1, D_MODEL), lambda m, k: (0, 0)),
                  pl.BlockSpec((tm, D_MODEL), lambda m, k: (m, 0))],
        out_specs=[pl.BlockSpec((tm, D_MODEL), lambda m, k: (m, 0)),
                   pl.BlockSpec((1, D_MODEL), lambda m, k: (0, 0))],
        out_shape=[jax.ShapeDtypeStruct((T, D_MODEL), F32),
                   jax.ShapeDtypeStruct((1, D_MODEL), F32)],
        scratch_shapes=[pltpu.VMEM((tm, D_MODEL), F32)],
        compiler_params=_params(2),
    )(dproj, w4, h, g_row, dres)


def _window_of(g):
    return jnp.left_shift(2, g)


def _select_stage(g, stages):
    res = stages[0]
    for i in range(1, len(stages)):
        res = jnp.where(g >= i, stages[i], res)
    return res


def _pool_fwd(proj0, wg4, scale_row, S, name, tm=256):
    T = proj0.shape[0]
    blocks_per_seq = S // tm
    hb = tm // HALO

    def body(x_ref, halo_ref, z_ref, w_ref, s_ref, y_ref, p_ref, mix_ref):
        m, g = pl.program_id(0), pl.program_id(1)
        first = (m % blocks_per_seq) == 0
        halo = jnp.where(first, 0.0, halo_ref[...])
        x = x_ref[...]
        ext = jnp.concatenate([halo, x], axis=0)
        stages = []
        cur = ext
        for sh in (1, 2, 4, 8):
            cur = cur + pltpu.roll(cur, sh, 0)
            stages.append(cur[HALO:, :])
        win_sum = _select_stage(g, stages)
        pos = (m % blocks_per_seq) * tm + lax.broadcasted_iota(jnp.int32, (tm, 1), 0)
        count = jnp.minimum(pos + 1, _window_of(g)).astype(F32)
        pooled = win_sum / count - x
        pooled_b = pooled.astype(BF16)
        mixed = _dot(pooled_b, w_ref[...].reshape(GROUP_DIM, GROUP_DIM))
        z = z_ref[...]
        y_ref[...] = (mixed * s_ref[...] * (z * _sigmoid(z))).astype(BF16)
        p_ref[...] = pooled_b
        mix_ref[...] = mixed

    blk = lambda m, g: (m, g)
    return pl.pallas_call(
        body, name=name, grid=(T // tm, N_GROUPS),
        in_specs=[pl.BlockSpec((tm, GROUP_DIM), blk),
                  pl.BlockSpec((HALO, GROUP_DIM), lambda m, g: (jnp.maximum(m * hb - 1, 0), g)),
                  pl.BlockSpec((tm, GROUP_DIM), lambda m, g: (m, N_GROUPS + g)),
                  pl.BlockSpec((N_CHIPS, GROUP_DIM // N_CHIPS, GROUP_DIM), lambda m, g: (0, g, 0)),
                  pl.BlockSpec((1, GROUP_DIM), lambda m, g: (0, g))],
        out_specs=[pl.BlockSpec((tm, GROUP_DIM), blk)] * 3,
        out_shape=[jax.ShapeDtypeStruct((T, D_INNER), BF16),
                   jax.ShapeDtypeStruct((T, D_INNER), BF16),
                   jax.ShapeDtypeStruct((T, D_INNER), F32)],
        compiler_params=_params(2),
    )(proj0, proj0, proj0, wg4, scale_row)


def _pool_gate_bwd(dh, w_out, proj0, mixed, scale_row, name, tm=1024, tn=512):
    T = dh.shape[0]
    gate_b0 = D_INNER // tn

    def body(dh_ref, w_ref, z_ref, mix_ref, s_ref, dm_ref, dz_ref, ds_ref):
        @pl.when(pl.program_id(1) == 0)
        def _():
            ds_ref[...] = jnp.zeros_like(ds_ref)

        dy = _dot_nt(dh_ref[...].astype(BF16), w_ref[...])
        z = z_ref[...]
        sig = _sigmoid(z)
        silu = z * sig
        mixed = mix_ref[...]
        s = s_ref[...]
        dm_ref[...] = (dy * s * silu).astype(BF16)
        dz_ref[...] = (dy * mixed * s * (sig * (1.0 + z * (1.0 - sig)))).astype(BF16)
        ds_ref[...] += jnp.sum(dy * mixed * silu, axis=0, keepdims=True)

    return pl.pallas_call(
        body, name=name, grid=(D_INNER // tn, T // tm),
        in_specs=[pl.BlockSpec((tm, D_MODEL), lambda n, m: (m, 0)),
                  pl.BlockSpec((tn, D_MODEL), lambda n, m: (n, 0)),
                  pl.BlockSpec((tm, tn), lambda n, m: (m, gate_b0 + n)),
                  pl.BlockSpec((tm, tn), lambda n, m: (m, n)),
                  pl.BlockSpec((1, tn), lambda n, m: (0, n))],
        out_specs=[pl.BlockSpec((tm, tn), lambda n, m: (m, n)),
                   pl.BlockSpec((tm, tn), lambda n, m: (m, gate_b0 + n)),
                   pl.BlockSpec((1, tn), lambda n, m: (0, n))],
        out_shape=[jax.ShapeDtypeStruct((T, D_INNER), BF16),
                   jax.ShapeDtypeStruct((T, 2 * D_INNER), BF16),
                   jax.ShapeDtypeStruct((1, D_INNER), F32)],
        compiler_params=_params(2),
    )(dh, w_out, proj0, mixed, scale_row)


def _pool_bwd(dmixed, wg4, dproj0, S, name, tm=256):
    T = dmixed.shape[0]
    blocks_per_seq = S // tm
    hb = tm // HALO
    n_halo_blocks = T // HALO

    def body(dm_ref, halo_ref, w_ref, _, o_ref):
        m, g = pl.program_id(0), pl.program_id(1)
        ext = jnp.concatenate([dm_ref[...], halo_ref[...]], axis=0)
        dp = _dot_nt(ext, w_ref[...].reshape(GROUP_DIM, GROUP_DIM))
        pos = (m % blocks_per_seq) * tm + lax.broadcasted_iota(jnp.int32, (tm + HALO, 1), 0)
        count = jnp.minimum(pos + 1, _window_of(g)).astype(F32)
        c = jnp.where(pos < S, dp / count, 0.0)
        n = tm + HALO
        stages = []
        cur = c
        for sh in (1, 2, 4, 8):
            cur = cur + pltpu.roll(cur, n - sh, 0)
            stages.append(cur[:tm, :])
        o_ref[...] = (_select_stage(g, stages) - dp[:tm, :]).astype(BF16)

    blk = lambda m, g: (m, g)
    return pl.pallas_call(
        body, name=name, grid=(T // tm, N_GROUPS),
        in_specs=[pl.BlockSpec((tm, GROUP_DIM), blk),
                  pl.BlockSpec((HALO, GROUP_DIM),
                               lambda m, g: (jnp.minimum((m + 1) * hb, n_halo_blocks - 1), g)),
                  pl.BlockSpec((N_CHIPS, GROUP_DIM // N_CHIPS, GROUP_DIM), lambda m, g: (0, g, 0)),
                  HBM_SPEC],
        out_specs=pl.BlockSpec((tm, GROUP_DIM), blk),
        out_shape=jax.ShapeDtypeStruct(dproj0.shape, dproj0.dtype),
        input_output_aliases={3: 0},
        compiler_params=_params(2),
    )(dmixed, dmixed, wg4, dproj0)


TQ = 256


def _split_dot(x, m):
    hi = x.astype(BF16)
    lo = (x - hi.astype(F32)).astype(BF16)
    return _dot(hi, m) + _dot(lo, m)


NEG_LOG2E = -1.4426950408889634


def _log_terms(z):
    soft = jnp.log(1.0 + jnp.exp2(jnp.abs(z) * NEG_LOG2E))
    log_beta = jnp.minimum(z, 0.0) - soft
    return log_beta, log_beta - z


def _head_masks():
    lane = lax.broadcasted_iota(jnp.int32, (1, HEAD_PAIR_QK), 1)
    return [(lane // 64) == hh for hh in range(2)]


def _attn_fwd(qkv, z1, S, name):
    T = qkv.shape[0]
    nq = S // TQ
    k_b0 = QK_WIDTH // HEAD_PAIR_QK
    v_b0 = 2 * QK_WIDTH // (2 * HEAD_V)

    def body(q_ref, k_ref, v_ref, z_ref, o_ref, y_ref, lt_ref):
        qi = pl.program_id(2)
        row = lax.broadcasted_iota(jnp.int32, (TQ, TQ), 0)
        col = lax.broadcasted_iota(jnp.int32, (TQ, TQ), 1)
        causal = col < row
        later_in_block = (row > col).astype(BF16)
        q = q_ref[...]
        qms = [jnp.where(hm, q, jnp.zeros_like(q)) * 0.125 for hm in _head_masks()]

        def step(j, carry, diagonal):
            koff = pl.multiple_of(j * TQ, TQ)
            kb = k_ref[pl.ds(koff, TQ), :]
            hs = range(2)
            run, acc = [carry[2 * hh] for hh in hs], [carry[2 * hh + 1] for hh in hs]
            z = [_dot_nt(qms[hh], kb) for hh in hs]
            terms = [_log_terms(z[hh]) for hh in hs]
            log_om = [jnp.where(causal, t[1], 0.0) if diagonal else t[1] for t in terms]
            later = [_split_dot(log_om[hh], later_in_block) for hh in hs]
            a = [jnp.exp(terms[hh][0] + (run[hh] + later[hh])) for hh in hs]
            if diagonal:
                a = [jnp.where(causal, a[hh], 0.0) for hh in hs]
            out = []
            for hh in hs:
                vb = v_ref[pl.ds(koff, TQ), hh * HEAD_V:(hh + 1) * HEAD_V]
                out += [run[hh] + jnp.sum(log_om[hh], axis=1, keepdims=True),
                        acc[hh] + _dot(a[hh].astype(BF16), vb)]
            return tuple(out)

        zero = (jnp.zeros((TQ, 1), F32), jnp.zeros((TQ, HEAD_V), F32))
        carry = step(qi, zero + zero, True)
        carry = lax.fori_loop(0, qi, lambda i, c: step(qi - 1 - i, c, False), carry)
        for hh in range(2):
            sl = slice(hh * HEAD_V, (hh + 1) * HEAD_V)
            acc = carry[2 * hh + 1]
            z = z_ref[:, sl]
            o_ref[:, sl] = acc
            y_ref[:, sl] = (acc * (z * _sigmoid(z))).astype(BF16)
            lt_ref[:, hh:hh + 1] = carry[2 * hh]

    qblk = lambda b, p, i: (b * nq + i, p)
    return pl.pallas_call(
        body, name=name, grid=(T // S, N_HEAD_PAIRS, nq),
        in_specs=[pl.BlockSpec((TQ, HEAD_PAIR_QK), qblk),
                  pl.BlockSpec((S, HEAD_PAIR_QK), lambda b, p, i: (b, k_b0 + p)),
                  pl.BlockSpec((S, 2 * HEAD_V), lambda b, p, i: (b, v_b0 + p)),
                  pl.BlockSpec((TQ, 2 * HEAD_V), qblk)],
        out_specs=[pl.BlockSpec((TQ, 2 * HEAD_V), qblk),
                   pl.BlockSpec((TQ, 2 * HEAD_V), qblk),
                   pl.BlockSpec((None, TQ, 2), lambda b, p, i: (p, b * nq + i, 0))],
        out_shape=[jax.ShapeDtypeStruct((T, D_INNER), F32),
                   jax.ShapeDtypeStruct((T, D_INNER), BF16),
                   jax.ShapeDtypeStruct((N_HEAD_PAIRS, T, 2), F32)],
        compiler_params=_params(3),
    )(qkv, qkv, qkv, z1)


def _attn_gate_bwd(dh, w_out, z1, o, name, tm=1024, tn=512):
    T = dh.shape[0]
    gate_b0 = (2 * QK_WIDTH + D_INNER) // tn

    def body(dh_ref, w_ref, z_ref, o_ref, do_ref, dz_ref, dh_s):
        @pl.when(pl.program_id(1) == 0)
        def _():
            dh_s[...] = dh_ref[...].astype(BF16)

        dy = _dot_nt(dh_s[...], w_ref[...])
        z = z_ref[...]
        sig = _sigmoid(z)
        do_ref[...] = (dy * (z * sig)).astype(BF16)
        dz_ref[...] = (dy * o_ref[...] * (sig * (1.0 + z * (1.0 - sig)))).astype(BF16)

    return pl.pallas_call(
        body, name=name, grid=(T // tm, D_INNER // tn),
        in_specs=[pl.BlockSpec((tm, D_MODEL), lambda m, n: (m, 0)),
                  pl.BlockSpec((tn, D_MODEL), lambda m, n: (n, 0)),
                  pl.BlockSpec((tm, tn), lambda m, n: (m, n)),
                  pl.BlockSpec((tm, tn), lambda m, n: (m, n))],
        out_specs=[pl.BlockSpec((tm, tn), lambda m, n: (m, n)),
                   pl.BlockSpec((tm, tn), lambda m, n: (m, gate_b0 + n))],
        out_shape=[jax.ShapeDtypeStruct((T, D_INNER), BF16),
                   jax.ShapeDtypeStruct((T, 2 * QK_WIDTH + 2 * D_INNER), BF16)],
        scratch_shapes=[pltpu.VMEM((tm, D_MODEL), BF16)],
        compiler_params=_params(2),
    )(dh, w_out, z1, o)


def _attn_bwd(qkv, do, ltot, dproj1, S, name):
    T = qkv.shape[0]
    nq = S // TQ
    k_b0 = QK_WIDTH // HEAD_PAIR_QK
    v_b0 = 2 * QK_WIDTH // (2 * HEAD_V)

    def body(q_ref, k_ref, v_ref, do_ref, lt_ref, _, out_ref, dq_s, dk_s, dv_s, dkb_s, dvb_s, sems):
        b, p = pl.program_id(0), pl.program_id(1)
        row = lax.broadcasted_iota(jnp.int32, (TQ, TQ), 0)
        col = lax.broadcasted_iota(jnp.int32, (TQ, TQ), 1)
        causal = col < row
        upto = (row <= col).astype(BF16)
        before = (row < col).astype(BF16)
        heads = _head_masks()
        dk_s[...] = jnp.zeros_like(dk_s)
        dv_s[...] = jnp.zeros_like(dv_s)

        def q_block(qi, _):
            qoff = pl.multiple_of(qi * TQ, TQ)
            q = q_ref[pl.ds(qoff, TQ), :]
            qms = [jnp.where(hm, q, jnp.zeros_like(q)) * 0.125 for hm in heads]
            q_both = jnp.concatenate(qms, axis=0)
            vsl = [slice(hh * HEAD_V, (hh + 1) * HEAD_V) for hh in range(2)]
            do_h = [do_ref[pl.ds(qoff, TQ), sl] for sl in vsl]
            total = [lt_ref[pl.ds(qoff, TQ), hh:hh + 1] for hh in range(2)]

            def k_block(j, carry, diagonal):
                koff = pl.multiple_of(j * TQ, TQ)
                kb = k_ref[pl.ds(koff, TQ), :]
                hs = range(2)
                g_before = [carry[2 * hh] for hh in hs]
                lom_before = [carry[2 * hh + 1] for hh in hs]
                z = [_dot_nt(qms[hh], kb) for hh in hs]
                da = [_dot_nt(do_h[hh], v_ref[pl.ds(koff, TQ), vsl[hh]]) for hh in hs]
                terms = [_log_terms(z[hh]) for hh in hs]
                log_om = [jnp.where(causal, t[1], 0.0) if diagonal else t[1] for t in terms]
                prefix = [_split_dot(log_om[hh], upto) for hh in hs]
                a = [jnp.exp(terms[hh][0] + ((total[hh] - lom_before[hh]) - prefix[hh])) for hh in hs]
                if diagonal:
                    a = [jnp.where(causal, a[hh], 0.0) for hh in hs]
                g = [a[hh] * da[hh] for hh in hs]
                g_prefix = [_dot(g[hh].astype(BF16), before) for hh in hs]
                out, dzs = [], []
                for hh in hs:
                    beta = jnp.exp(terms[hh][0])
                    g_excl = (g_before[hh] + g_prefix[hh]) * beta
                    if diagonal:
                        g_excl = jnp.where(causal, g_excl, 0.0)
                    dzs.append((g[hh] * (1.0 - beta) - g_excl).astype(BF16))
                    out += [g_before[hh] + jnp.sum(g[hh], axis=1, keepdims=True),
                            lom_before[hh] + jnp.sum(log_om[hh], axis=1, keepdims=True)]
                for hh in hs:
                    dv_s[pl.ds(koff, TQ), vsl[hh]] += _dot_tn(a[hh].astype(BF16), do_h[hh])
                k_both = jnp.concatenate([jnp.where(hm, kb, jnp.zeros_like(kb)) for hm in heads], axis=0)
                dq = carry[4] + _dot(jnp.concatenate(dzs, axis=1), k_both)
                dk_s[pl.ds(koff, TQ), :] += _dot_tn(jnp.concatenate(dzs, axis=0), q_both)
                return tuple(out) + (dq,)

            zero = jnp.zeros((TQ, 1), F32)
            carry = (zero, zero, zero, zero, jnp.zeros((TQ, HEAD_PAIR_QK), F32))
            carry = lax.fori_loop(0, qi, lambda j, c: k_block(j, c, False), carry)
            carry = k_block(qi, carry, True)
            dq_s[pl.ds(qoff, TQ), :] = (carry[4] * 0.125).astype(BF16)
            return 0

        lax.fori_loop(0, nq, q_block, 0)
        dkb_s[...] = dk_s[...].astype(BF16)
        dvb_s[...] = dv_s[...].astype(BF16)
        rows = pl.ds(pl.multiple_of(b * S, TQ), S)
        copies = [
            pltpu.make_async_copy(
                dq_s, out_ref.at[rows, pl.ds(pl.multiple_of(p * HEAD_PAIR_QK, 128), HEAD_PAIR_QK)],
                sems.at[0]),
            pltpu.make_async_copy(
                dkb_s, out_ref.at[rows, pl.ds(pl.multiple_of(QK_WIDTH + p * HEAD_PAIR_QK, 128),
                                              HEAD_PAIR_QK)], sems.at[1]),
            pltpu.make_async_copy(
                dvb_s, out_ref.at[rows, pl.ds(pl.multiple_of(2 * QK_WIDTH + p * 2 * HEAD_V, 128),
                                              2 * HEAD_V)], sems.at[2]),
        ]
        for cp in copies:
            cp.start()
        for cp in copies:
            cp.wait()

    return pl.pallas_call(
        body, name=name, grid=(T // S, N_HEAD_PAIRS),
        in_specs=[pl.BlockSpec((S, HEAD_PAIR_QK), lambda b, p: (b, p)),
                  pl.BlockSpec((S, HEAD_PAIR_QK), lambda b, p: (b, k_b0 + p)),
                  pl.BlockSpec((S, 2 * HEAD_V), lambda b, p: (b, v_b0 + p)),
                  pl.BlockSpec((S, 2 * HEAD_V), lambda b, p: (b, p)),
                  pl.BlockSpec((None, S, 2), lambda b, p: (p, b, 0)),
                  HBM_SPEC],
        out_specs=HBM_SPEC,
        out_shape=jax.ShapeDtypeStruct(dproj1.shape, dproj1.dtype),
        input_output_aliases={5: 0},
        scratch_shapes=[pltpu.VMEM((S, HEAD_PAIR_QK), BF16),
                        pltpu.VMEM((S, HEAD_PAIR_QK), F32),
                        pltpu.VMEM((S, 2 * HEAD_V), F32),
                        pltpu.VMEM((S, HEAD_PAIR_QK), BF16),
                        pltpu.VMEM((S, 2 * HEAD_V), BF16),
                        pltpu.SemaphoreType.DMA((3,))],
        compiler_params=_params(2),
    )(qkv, qkv, qkv, do, ltot, dproj1)


def _loss_head(h, g_row, target, name, tm=512):
    T = h.shape[0]

    def body(h_ref, g_ref, t_ref, dh_ref, dg_ref, loss_ref):
        @pl.when(pl.program_id(0) == 0)
        def _():
            dg_ref[...] = jnp.zeros_like(dg_ref)
            loss_ref[...] = jnp.zeros_like(loss_ref)

        x = h_ref[...]
        inv = lax.rsqrt(jnp.mean(x * x, axis=-1, keepdims=True) + RMS_EPS)
        xhat = x * inv
        gain = g_ref[...]
        err = xhat * gain - t_ref[...]
        per_token = jnp.mean(err * err, axis=-1, keepdims=True)
        loss_ref[...] += 0.5 * jnp.sum(per_token, axis=0, keepdims=True)
        dy = err * (1.0 / D_MODEL)
        dg_ref[...] += jnp.sum(dy * xhat, axis=0, keepdims=True)
        dxh = dy * gain
        proj = jnp.mean(dxh * xhat, axis=-1, keepdims=True)
        dh_ref[...] = inv * (dxh - xhat * proj)

    return pl.pallas_call(
        body, name=name, grid=(T // tm,),
        in_specs=[pl.BlockSpec((tm, D_MODEL), lambda m: (m, 0)),
                  pl.BlockSpec((1, D_MODEL), lambda m: (0, 0)),
                  pl.BlockSpec((tm, D_MODEL), lambda m: (m, 0))],
        out_specs=[pl.BlockSpec((tm, D_MODEL), lambda m: (m, 0)),
                   pl.BlockSpec((1, D_MODEL), lambda m: (0, 0)),
                   pl.BlockSpec((1, 128), lambda m: (0, 0))],
        out_shape=[jax.ShapeDtypeStruct((T, D_MODEL), F32),
                   jax.ShapeDtypeStruct((1, D_MODEL), F32),
                   jax.ShapeDtypeStruct((1, 128), F32)],
        compiler_params=_params(1),
    )(h, g_row, target)


def _place():
    return lax.axis_index("x"), lax.axis_index("y"), lax.axis_index("c")


def _other_chips(x, y):
    return [(1 - x, y), (x, 1 - y), (1 - x, 1 - y)]


def _half(ref, c):
    hr = ref.shape[-2] // 2
    return pl.ds(pl.multiple_of(c * hr, 8), hr)


def _cast_to_slot(shard, chip, name, tr=256):
    R, C = shard.shape

    def body(chip_ref, w_ref, o_ref):
        o_ref[0] = w_ref[...].astype(BF16)

    return pl.pallas_call(
        body, name=name,
        grid_spec=pltpu.PrefetchScalarGridSpec(
            num_scalar_prefetch=1, grid=(R // tr,),
            in_specs=[pl.BlockSpec((tr, C), lambda i, chip_ref: (i, 0))],
            out_specs=pl.BlockSpec((1, tr, C), lambda i, chip_ref: (chip_ref[0], i, 0))),
        out_shape=jax.ShapeDtypeStruct((N_CHIPS, R, C), BF16),
        compiler_params=_params(1),
    )(chip, shard)


def _allgather_weights(slots, name):
    n = len(slots)

    def body(*refs):
        outs = refs[n:2 * n]
        send_sems, recv_sems, fwd_send, fwd_recv = refs[2 * n:]
        x, y, c = _place()
        chips = _other_chips(x, y)

        def landing(a, chip, half_of):
            return outs[a].at[2 * chip[0] + chip[1], _half(outs[a], half_of)]

        def ici(a, k, chip_from, to):
            return pltpu.make_async_remote_copy(
                src_ref=landing(a, chip_from, c), dst_ref=landing(a, chip_from, c),
                send_sem=send_sems.at[a, k], recv_sem=recv_sems.at[a, k],
                device_id=to, device_id_type=MESH)

        def d2d(a, k, chip_from, half_of):
            return pltpu.make_async_remote_copy(
                src_ref=landing(a, chip_from, half_of), dst_ref=landing(a, chip_from, half_of),
                send_sem=fwd_send.at[a, k], recv_sem=fwd_recv.at[a, k],
                device_id=(x, y, 1 - c), device_id_type=MESH)

        sends = [ici(a, k, (x, y), (*chips[k], c)) for a in range(n) for k in range(3)]
        for cp in sends:
            cp.start()
        forwards = []
        for a in range(n):
            for k in range(3):
                ici(a, k, chips[k], (x, y, c)).wait_recv()
                fw = d2d(a, k, chips[k], c)
                fw.start()
                forwards.append(fw)
        for a in range(n):
            for k in range(3):
                d2d(a, k, chips[k], 1 - c).wait_recv()
        for cp in sends + forwards:
            cp.wait_send()

    return pl.pallas_call(
        body, name=name,
        in_specs=[HBM_SPEC] * n, out_specs=[HBM_SPEC] * n,
        out_shape=[jax.ShapeDtypeStruct(s.shape, s.dtype) for s in slots],
        input_output_aliases={a: a for a in range(n)},
        scratch_shapes=[pltpu.SemaphoreType.DMA((n, 3)), pltpu.SemaphoreType.DMA((n, 3)),
                        pltpu.SemaphoreType.DMA((n, 3)), pltpu.SemaphoreType.DMA((n, 3))],
    )(*slots)


def _sibling_exchange(partials, small, name):
    n = len(partials)

    def body(*refs):
        ins, small_ref = refs[:n], refs[n]
        outs, small_all = refs[n + 1:2 * n + 1], refs[2 * n + 1]
        send_sems, recv_sems, s_send, s_recv, loc_sem = refs[2 * n + 2:]
        x, y, c = _place()
        me = 4 * x + 2 * y + c
        local = pltpu.make_async_copy(small_ref, small_all.at[me], loc_sem)
        local.start()
        big = [pltpu.make_async_remote_copy(
            src_ref=ins[a].at[:, _half(ins[a], 1 - c)], dst_ref=outs[a],
            send_sem=send_sems.at[a], recv_sem=recv_sems.at[a],
            device_id=(x, y, 1 - c), device_id_type=MESH) for a in range(n)]
        tiny = []
        for d in range(1, N_DEV):
            px, py, pc = x ^ ((d >> 2) & 1), y ^ ((d >> 1) & 1), c ^ (d & 1)
            tiny.append(pltpu.make_async_remote_copy(
                src_ref=small_ref, dst_ref=small_all.at[me],
                send_sem=s_send.at[d - 1], recv_sem=s_recv.at[d - 1],
                device_id=(px, py, pc), device_id_type=MESH))
        for cp in tiny + big:
            cp.start()
        for d in range(1, N_DEV):
            peer = me ^ d
            pltpu.make_async_remote_copy(
                src_ref=small_ref, dst_ref=small_all.at[peer],
                send_sem=s_send.at[d - 1], recv_sem=s_recv.at[d - 1],
                device_id=(x, y, c), device_id_type=MESH).wait_recv()
        for cp in big:
            cp.wait_recv()
        for cp in tiny + big:
            cp.wait_send()
        local.wait()

    return pl.pallas_call(
        body, name=name,
        in_specs=[HBM_SPEC] * (n + 1), out_specs=[HBM_SPEC] * (n + 1),
        out_shape=[jax.ShapeDtypeStruct((N_CHIPS, p.shape[1] // 2, p.shape[2]), F32)
                   for p in partials] + [jax.ShapeDtypeStruct((N_DEV,) + small.shape, F32)],
        scratch_shapes=[pltpu.SemaphoreType.DMA((n,)), pltpu.SemaphoreType.DMA((n,)),
                        pltpu.SemaphoreType.DMA((N_DEV - 1,)), pltpu.SemaphoreType.DMA((N_DEV - 1,)),
                        pltpu.SemaphoreType.DMA],
    )(*partials, small)


def _chip_sum(partial, from_sibling, c, name, tr=256):
    _, hr, C = from_sibling.shape
    nb = hr // tr

    def body(c_ref, p_ref, s_ref, o_ref):
        o_ref[...] = (p_ref[...] + s_ref[...]).astype(BF16)

    return pl.pallas_call(
        body, name=name,
        grid_spec=pltpu.PrefetchScalarGridSpec(
            num_scalar_prefetch=1, grid=(N_CHIPS, nb),
            in_specs=[pl.BlockSpec((1, tr, C), lambda j, i, c_ref: (j, c_ref[0] * nb + i, 0)),
                      pl.BlockSpec((1, tr, C), lambda j, i, c_ref: (j, i, 0))],
            out_specs=pl.BlockSpec((1, tr, C), lambda j, i, c_ref: (j, i, 0))),
        out_shape=jax.ShapeDtypeStruct(from_sibling.shape, BF16),
        compiler_params=_params(2),
    )(c, partial, from_sibling)


def _send_chip_sums(sums, name):
    n = len(sums)

    def body(*refs):
        ins, outs = refs[:n], refs[n:2 * n]
        send_sems, recv_sems = refs[2 * n:]
        x, y, c = _place()
        me = 2 * x + y
        chips = _other_chips(x, y)
        sends = [pltpu.make_async_remote_copy(
            src_ref=ins[a].at[2 * chips[k][0] + chips[k][1]], dst_ref=outs[a].at[k],
            send_sem=send_sems.at[a, k], recv_sem=recv_sems.at[a, k],
            device_id=(*chips[k], c), device_id_type=MESH) for a in range(n) for k in range(3)]
        for cp in sends:
            cp.start()
        for a in range(n):
            for k in range(3):
                pltpu.make_async_remote_copy(
                    src_ref=ins[a].at[me], dst_ref=outs[a].at[k],
                    send_sem=send_sems.at[a, k], recv_sem=recv_sems.at[a, k],
                    device_id=(x, y, c), device_id_type=MESH).wait_recv()
        for cp in sends:
            cp.wait_send()

    return pl.pallas_call(
        body, name=name,
        in_specs=[HBM_SPEC] * n, out_specs=[HBM_SPEC] * n,
        out_shape=[jax.ShapeDtypeStruct((3,) + s.shape[1:], BF16) for s in sums],
        scratch_shapes=[pltpu.SemaphoreType.DMA((n, 3)), pltpu.SemaphoreType.DMA((n, 3))],
    )(*sums)


def _reduce_half(partial, from_sibling, received, place, name, tr=256):
    _, hr, C = from_sibling.shape
    nb = hr // tr

    def body(p_ref, mine_ref, sib_ref, r_ref, o_ref):
        acc = mine_ref[0] + sib_ref[0]
        for k in range(3):
            acc = acc + r_ref[k].astype(F32)
        o_ref[...] = acc

    return pl.pallas_call(
        body, name=name,
        grid_spec=pltpu.PrefetchScalarGridSpec(
            num_scalar_prefetch=1, grid=(nb,),
            in_specs=[pl.BlockSpec((1, tr, C), lambda i, p: (p[0], p[1] * nb + i, 0)),
                      pl.BlockSpec((1, tr, C), lambda i, p: (p[0], i, 0)),
                      pl.BlockSpec((3, tr, C), lambda i, p: (0, i, 0))],
            out_specs=pl.BlockSpec((tr, C), lambda i, p: (p[1] * nb + i, 0))),
        out_shape=jax.ShapeDtypeStruct((2 * hr, C), F32),
        compiler_params=_params(1),
    )(place, partial, from_sibling, received)


def _join_halves(fulls, name):
    n = len(fulls)

    def body(*refs):
        outs = refs[n:2 * n]
        send_sems, recv_sems = refs[2 * n:]
        x, y, c = _place()

        def copy(a, half_of, to):
            rows = outs[a].at[_half(outs[a], half_of)]
            return pltpu.make_async_remote_copy(
                src_ref=rows, dst_ref=rows, send_sem=send_sems.at[a], recv_sem=recv_sems.at[a],
                device_id=to, device_id_type=MESH)

        sends = [copy(a, c, (x, y, 1 - c)) for a in range(n)]
        for cp in sends:
            cp.start()
        for a in range(n):
            copy(a, 1 - c, (x, y, c)).wait_recv()
        for cp in sends:
            cp.wait_send()

    return pl.pallas_call(
        body, name=name,
        in_specs=[HBM_SPEC] * n, out_specs=[HBM_SPEC] * n,
        out_shape=[jax.ShapeDtypeStruct(f.shape, F32) for f in fulls],
        input_output_aliases={a: a for a in range(n)},
        scratch_shapes=[pltpu.SemaphoreType.DMA((n,)), pltpu.SemaphoreType.DMA((n,))],
    )(*fulls)


def _adamw_math(w, g, m, v):
    m = ADAM_B1 * m + (1.0 - ADAM_B1) * g
    v = ADAM_B2 * v + (1.0 - ADAM_B2) * (g * g)
    m_hat = m / (1.0 - ADAM_B1 ** ADAM_STEP)
    v_hat = v / (1.0 - ADAM_B2 ** ADAM_STEP)
    delta = -ADAM_LR * (m_hat / (jnp.sqrt(v_hat) + ADAM_EPS) + ADAM_WD * w)
    return delta, m, v


def _adamw(w, g, m, v, name, tr=256):
    R, C = w.shape
    tr = min(tr, R)

    def body(w_ref, g_ref, m_ref, v_ref, d_out, m_out, v_out):
        d_out[...], m_out[...], v_out[...] = _adamw_math(w_ref[...], g_ref[...], m_ref[...], v_ref[...])

    spec = pl.BlockSpec((tr, C), lambda i: (i, 0))
    return pl.pallas_call(
        body, name=name, grid=(R // tr,),
        in_specs=[spec] * 4, out_specs=[spec] * 3,
        out_shape=[jax.ShapeDtypeStruct((R, C), F32)] * 3,
        compiler_params=_params(1),
    )(w, g, m, v)


def _adamw_small(small_all, w, m, v, name):
    def body(s_ref, w_ref, m_ref, v_ref, g_out, d_out, m_out, v_out):
        g = s_ref[0]
        for d in range(1, N_DEV):
            g = g + s_ref[d]
        g_out[...] = g
        d_out[...], m_out[...], v_out[...] = _adamw_math(w_ref[...], g, m_ref[...], v_ref[...])

    vm = pl.BlockSpec(memory_space=pltpu.VMEM)
    return pl.pallas_call(
        body, name=name, in_specs=[vm] * 4, out_specs=[vm] * 4,
        out_shape=[jax.ShapeDtypeStruct(w.shape, F32)] * 4,
    )(small_all, w, m, v)


def _pack_small(norm_g, pool_scale, norm_f, extra_row):
    return jnp.concatenate([norm_g.reshape(2, D_MODEL), pool_scale.reshape(2, D_MODEL),
                            norm_f.reshape(1, D_MODEL), extra_row,
                            jnp.zeros((2, D_MODEL), F32)], axis=0)


def kernel(x, norm_g, pool_w_in, pool_w, pool_scale, pool_w_out, sb_w_in, sb_w_out, norm_f, loss_target, m_norm_g, m_pool_w_in, m_pool_w, m_pool_scale, m_pool_w_out, m_sb_w_in, m_sb_w_out, m_norm_f, v_norm_g, v_pool_w_in, v_pool_w, v_pool_scale, v_pool_w_out, v_sb_w_in, v_sb_w_out, v_norm_f):
    nb, S, _ = x.shape
    T = nb * S
    xt = x.reshape(T, D_MODEL)
    target = loss_target.reshape(T, D_MODEL)
    cx, cy, cc = _place()

    def shard2d(w):
        return w.reshape(-1, w.shape[-1])

    names = ("pool_w_in", "pool_w", "pool_w_out", "sb_w_in", "sb_w_out")
    w_shards = [shard2d(w) for w in (pool_w_in, pool_w, pool_w_out, sb_w_in, sb_w_out)]
    m_shards = [shard2d(w) for w in (m_pool_w_in, m_pool_w, m_pool_w_out, m_sb_w_in, m_sb_w_out)]
    v_shards = [shard2d(w) for w in (v_pool_w_in, v_pool_w, v_pool_w_out, v_sb_w_in, v_sb_w_out)]

    chip = (2 * cx + cy).reshape(1).astype(jnp.int32)
    w_pin, w_g, w_pout, w_sin, w_sout = _allgather_weights(
        [_cast_to_slot(w, chip, "cast_" + nm) for w, nm in zip(w_shards, names)], "allgather_weights")
    w_pout = w_pout.reshape(D_INNER, D_MODEL)
    w_sout = w_sout.reshape(D_INNER, D_MODEL)
    g0, g1, gf = norm_g[0:1], norm_g[1:2], norm_f.reshape(1, D_MODEL)

    proj0, u0 = _rms_matmul(xt, g0, w_pin, [(2 * D_INNER, F32)], "pool_in_proj")
    y0, pooled, mixed = _pool_fwd(proj0, w_g, pool_scale, S, "pool_mix")
    h1 = _matmul_residual(y0, w_pout, xt, "pool_out_proj")
    qkv, z1, u1 = _rms_matmul(h1, g1, w_sin, [(2 * QK_WIDTH + D_INNER, BF16), (D_INNER, F32)],
                              "sb_in_proj")
    o, y1, ltot = _attn_fwd(qkv, z1, S, "sb_attention")
    h2 = _matmul_residual(y1, w_sout, h1, "sb_out_proj")
    dh2, d_norm_f, loss_row = _loss_head(h2, gf, target, "loss_head")

    shard = lambda i, j, t: (j, 0, 0)
    gw_sout = _matmul_tn(y1, dh2, D_INNER, D_MODEL, (D_INNER, D_MODEL), (1024, 1024),
                         lambda i, j, t: (i, j), "grad_sb_w_out", bm=1024, bn=1024)
    do, dproj1 = _attn_gate_bwd(dh2, w_sout, z1, o, "sb_gate_bwd")
    dproj1 = _attn_bwd(qkv, do, ltot, dproj1, S, "sb_attention_bwd")
    n1 = 2 * QK_WIDTH + 2 * D_INNER
    gw_sin = _matmul_tn(u1, dproj1, D_MODEL, n1, (N_CHIPS, D_MODEL, n1 // 4), (1, D_MODEL, n1 // 4),
                        shard, "grad_sb_w_in", bm=D_MODEL, bn=n1 // 4)
    dh1, d_g1 = _matmul_nt_rms_bwd(dproj1, w_sin, h1, g1, dh2, "sb_in_bwd")
    gw_pout = _matmul_tn(y0, dh1, D_INNER, D_MODEL, (D_INNER, D_MODEL), (1024, 1024),
                         lambda i, j, t: (i, j), "grad_pool_w_out", bm=1024, bn=1024)
    dmixed, dproj0, d_scale = _pool_gate_bwd(dh1, w_pout, proj0, mixed, pool_scale, "pool_gate_bwd")
    gw_g = _matmul_tn(pooled, dmixed, D_INNER, D_INNER, (N_CHIPS, GROUP_DIM, GROUP_DIM),
                      (N_CHIPS, GROUP_DIM // N_CHIPS, GROUP_DIM), lambda i, j, t: (0, i, 0),
                      "grad_pool_w", bm=GROUP_DIM, bn=GROUP_DIM, diagonal_blocks=True)
    dproj0 = _pool_bwd(dmixed, w_g, dproj0, S, "pool_bwd")
    n0 = 2 * D_INNER
    gw_pin = _matmul_tn(u0, dproj0, D_MODEL, n0, (N_CHIPS, D_MODEL, n0 // 4), (1, D_MODEL, n0 // 4),
                        shard, "grad_pool_w_in", bm=D_MODEL, bn=n0 // 4)
    dx, d_g0 = _matmul_nt_rms_bwd(dproj0, w_pin, xt, g0, dh1, "pool_in_bwd")

    partials = [gw_pin, gw_g, gw_pout.reshape(N_CHIPS, -1, D_MODEL), gw_sin,
                gw_sout.reshape(N_CHIPS, -1, D_MODEL)]
    small = _pack_small(jnp.concatenate([d_g0, d_g1], axis=0), d_scale, d_norm_f,
                        jnp.broadcast_to(loss_row[:, :1], (1, D_MODEL)))
    *from_sibling, small_all = _sibling_exchange(partials, small, "grad_sibling_exchange")
    c_arr = cc.reshape(1).astype(jnp.int32)
    place = jnp.stack([2 * cx + cy, cc]).astype(jnp.int32)
    sums = [_chip_sum(p, s, c_arr, "grad_chip_sum_" + nm)
            for p, s, nm in zip(partials, from_sibling, names)]
    received = _send_chip_sums(sums, "grad_chip_exchange")
    grads = _join_halves([_reduce_half(p, s, r, place, "grad_reduce_" + nm)
                          for p, s, r, nm in zip(partials, from_sibling, received, names)],
                         "grad_join_halves")

    deltas, new_m, new_v = [], [], []
    for w, g, m, v, nm in zip(w_shards, grads, m_shards, v_shards, names):
        d, mm, vv = _adamw(w, g, m, v, "adamw_" + nm)
        deltas.append(d)
        new_m.append(mm)
        new_v.append(vv)

    zero_row = jnp.zeros((1, D_MODEL), F32)
    g_small, d_small, m_small, v_small = _adamw_small(
        small_all, _pack_small(norm_g, pool_scale, norm_f, zero_row),
        _pack_small(m_norm_g, m_pool_scale, m_norm_f, zero_row),
        _pack_small(v_norm_g, v_pool_scale, v_norm_f, zero_row + 1.0), "adamw_small")
    loss = g_small[5, 0]

    def unpack_small(a):
        return a[0:2], a[2:4].reshape(1, D_INNER), a[4]

    def assemble(big, small3):
        ng, ps, nf = small3
        return [ng, big[0].reshape(pool_w_in.shape), big[1].reshape(pool_w.shape), ps,
                big[2].reshape(pool_w_out.shape), big[3].reshape(sb_w_in.shape),
                big[4].reshape(sb_w_out.shape), nf]

    return (loss, dx.reshape(x.shape),
            *assemble(grads, unpack_small(g_small)),
            *assemble(deltas, unpack_small(d_small)),
            *assemble(new_m, unpack_small(m_small)),
            *assemble(new_v, unpack_small(v_small)))
```

```python
import functools

import jax
import jax.numpy as jnp
from jax import lax
from jax.experimental import pallas as pl
from jax.experimental.pallas import tpu as pltpu

F32 = jnp.float32
BF16 = jnp.bfloat16
MESH = pl.DeviceIdType.MESH

D_MODEL = 1024
D_INNER = 2048
N_GROUPS = 4
GROUP_DIM = 512
HEAD_PAIR_QK = 128
HEAD_V = 128
N_HEAD_PAIRS = 8
QK_WIDTH = 1024
RMS_EPS = 1e-6
HALO = 16
N_CHIPS = 4
N_DEV = 8

ADAM_LR = 0.001
ADAM_B1 = 0.9
ADAM_B2 = 0.999
ADAM_EPS = 1e-08
ADAM_WD = 0.01
ADAM_STEP = 10

VMEM_LIMIT = 56 * 1024 * 1024

HBM_SPEC = pl.BlockSpec(memory_space=pltpu.HBM)


def _params(n_axes):
    return pltpu.CompilerParams(dimension_semantics=("arbitrary",) * n_axes,
                                vmem_limit_bytes=VMEM_LIMIT)


def _dot(a, b):
    return jnp.dot(a, b, preferred_element_type=F32)


def _dot_nt(a, b):
    return lax.dot_general(a, b, (((1,), (1,)), ((), ())), preferred_element_type=F32)


def _dot_tn(a, b):
    return lax.dot_general(a, b, (((0,), (0,)), ((), ())), preferred_element_type=F32)


def _sigmoid(z):
    return 1.0 / (1.0 + jnp.exp(-z))


def _rms_matmul(h, g_row, w4, outs, name, tm=1024, tn=512):
    T = h.shape[0]
    per_shard = w4.shape[2] // tn
    starts = [0]
    for width, _ in outs:
        starts.append(starts[-1] + width // tn)
    n_out = len(outs)

    def body(h_ref, g_ref, w_ref, *rest):
        o_refs, u_out, u_s = rest[:n_out], rest[n_out], rest[n_out + 1]
        n = pl.program_id(1)

        @pl.when(n == 0)
        def _():
            x = h_ref[...]
            inv = lax.rsqrt(jnp.mean(x * x, axis=-1, keepdims=True) + RMS_EPS)
            u = (x * inv * g_ref[...]).astype(BF16)
            u_s[...] = u
            u_out[...] = u

        res = _dot(u_s[...], w_ref[0])
        for k in range(n_out):
            @pl.when((n >= starts[k]) & (n < starts[k + 1]))
            def _():
                o_refs[k][...] = res.astype(o_refs[k].dtype)

    def out_map(k):
        return lambda m, n: (m, jnp.clip(n - starts[k], 0, starts[k + 1] - starts[k] - 1))

    return pl.pallas_call(
        body, name=name, grid=(T // tm, starts[-1]),
        in_specs=[pl.BlockSpec((tm, D_MODEL), lambda m, n: (m, 0)),
                  pl.BlockSpec((1, D_MODEL), lambda m, n: (0, 0)),
                  pl.BlockSpec((1, D_MODEL, tn), lambda m, n: (n // per_shard, 0, n % per_shard))],
        out_specs=[pl.BlockSpec((tm, tn), out_map(k)) for k in range(n_out)]
        + [pl.BlockSpec((tm, D_MODEL), lambda m, n: (m, 0))],
        out_shape=[jax.ShapeDtypeStruct((T, width), dt) for width, dt in outs]
        + [jax.ShapeDtypeStruct((T, D_MODEL), BF16)],
        scratch_shapes=[pltpu.VMEM((tm, D_MODEL), BF16)],
        compiler_params=_params(2),
    )(h, g_row, w4)


def _matmul_residual(a, w, res, name, tm=1024, tn=512):
    T, K = a.shape
    N = w.shape[1]

    def body(a_ref, w_ref, r_ref, o_ref):
        o_ref[...] = r_ref[...] + _dot(a_ref[...], w_ref[...])

    return pl.pallas_call(
        body, name=name, grid=(T // tm, N // tn),
        in_specs=[pl.BlockSpec((tm, K), lambda m, n: (m, 0)),
                  pl.BlockSpec((K, tn), lambda m, n: (0, n)),
                  pl.BlockSpec((tm, tn), lambda m, n: (m, n))],
        out_specs=pl.BlockSpec((tm, tn), lambda m, n: (m, n)),
        out_shape=jax.ShapeDtypeStruct((T, N), F32),
        compiler_params=_params(2),
    )(a, w, res)


def _matmul_tn(a, b, a_cols, b_cols, out_shape, out_block, out_map, name, bm, bn, tk=512,
               diagonal_blocks=False):
    T = a.shape[0]

    def body(a_ref, b_ref, o_ref):
        @pl.when(pl.program_id(2) == 0)
        def _():
            o_ref[...] = jnp.zeros_like(o_ref)

        part = _dot_tn(a_ref[...].astype(BF16), b_ref[...].astype(BF16))
        o_ref[...] += part.reshape(o_ref.shape)

    b_map = (lambda i, j, t: (t, i)) if diagonal_blocks else (lambda i, j, t: (t, j))
    return pl.pallas_call(
        body, name=name, grid=(a_cols // bm, 1 if diagonal_blocks else b_cols // bn, T // tk),
        in_specs=[pl.BlockSpec((tk, bm), lambda i, j, t: (t, i)),
                  pl.BlockSpec((tk, bn), b_map)],
        out_specs=pl.BlockSpec(out_block, out_map),
        out_shape=jax.ShapeDtypeStruct(out_shape, F32),
        compiler_params=_params(3),
    )(a, b)


def _matmul_nt_rms_bwd(dproj, w4, h, g_row, dres, name, tm=1024, tk=512):
    T, cols = dproj.shape
    per_shard = w4.shape[2] // tk
    nk = cols // tk

    def body(dp_ref, w_ref, h_ref, g_ref, r_ref, dx_ref, dg_ref, acc):
        m, k = pl.program_id(0), pl.program_id(1)

        @pl.when(k == 0)
        def _():
            acc[...] = jnp.zeros_like(acc)

        @pl.when((k == 0) & (m == 0))
        def _():
            dg_ref[...] = jnp.zeros_like(dg_ref)

        acc[...] += _dot_nt(dp_ref[...], w_ref[0])

        @pl.when(k == nk - 1)
        def _():
            du = acc[...]
            x = h_ref[...]
            inv = lax.rsqrt(jnp.mean(x * x, axis=-1, keepdims=True) + RMS_EPS)
            xhat = x * inv
            dg_ref[...] += jnp.sum(du * xhat, axis=0, keepdims=True)
            dxh = du * g_ref[...]
            proj = jnp.mean(dxh * xhat, axis=-1, keepdims=True)
            dx_ref[...] = r_ref[...] + inv * (dxh - xhat * proj)

    return pl.pallas_call(
        body, name=name, grid=(T // tm, nk),
        in_specs=[pl.BlockSpec((tm, tk), lambda m, k: (m, k)),
                  pl.BlockSpec((1, D_MODEL, tk), lambda m, k: (k // per_shard, 0, k % per_shard)),
                  pl.BlockSpec((tm, D_MODEL), lambda m, k: (m, 0)),
                  pl.BlockSpec((1, D_MODEL), lambda m, k: (0, 0)),
                  pl.BlockSpec((tm, D_MODEL), lambda m, k: (m, 0))],
        out_specs=[pl.BlockSpec((tm, D_MODEL), lambda m, k: (m, 0)),
                   pl.BlockSpec((1, D_MODEL), lambda m, k: (0, 0))],
        out_shape=[jax.ShapeDtypeStruct((T, D_MODEL), F32),
                   jax.ShapeDtypeStruct((1, D_MODEL), F32)],
        scratch_shapes=[pltpu.VMEM((tm, D_MODEL), F32)],
        compiler_params=_params(2),
    )(dproj, w4, h, g_row, dres)


def _window_of(g):
    return jnp.left_shift(2, g)


def _select_stage(g, stages):
    res = stages[0]
    for i in range(1, len(stages)):
        res = jnp.where(g >= i, stages[i], res)
    return res


def _pool_fwd(proj0, wg4, scale_row, S, name, tm=256):
    T = proj0.shape[0]
    blocks_per_seq = S // tm
    hb = tm // HALO

    def body(x_ref, halo_ref, z_ref, w_ref, s_ref, y_ref, p_ref, mix_ref):
        m, g = pl.program_id(0), pl.program_id(1)
        first = (m % blocks_per_seq) == 0
        halo = jnp.where(first, 0.0, halo_ref[...])
        x = x_ref[...]
        ext = jnp.concatenate([halo, x], axis=0)
        stages = []
        cur = ext
        for sh in (1, 2, 4, 8):
            cur = cur + pltpu.roll(cur, sh, 0)
            stages.append(cur[HALO:, :])
        win_sum = _select_stage(g, stages)
        pos = (m % blocks_per_seq) * tm + lax.broadcasted_iota(jnp.int32, (tm, 1), 0)
        count = jnp.minimum(pos + 1, _window_of(g)).astype(F32)
        pooled = win_sum / count - x
        pooled_b = pooled.astype(BF16)
        mixed = _dot(pooled_b, w_ref[...].reshape(GROUP_DIM, GROUP_DIM))
        z = z_ref[...]
        y_ref[...] = (mixed * s_ref[...] * (z * _sigmoid(z))).astype(BF16)
        p_ref[...] = pooled_b
        mix_ref[...] = mixed

    blk = lambda m, g: (m, g)
    return pl.pallas_call(
        body, name=name, grid=(T // tm, N_GROUPS),
        in_specs=[pl.BlockSpec((tm, GROUP_DIM), blk),
                  pl.BlockSpec((HALO, GROUP_DIM), lambda m, g: (jnp.maximum(m * hb - 1, 0), g)),
                  pl.BlockSpec((tm, GROUP_DIM), lambda m, g: (m, N_GROUPS + g)),
                  pl.BlockSpec((N_CHIPS, GROUP_DIM // N_CHIPS, GROUP_DIM), lambda m, g: (0, g, 0)),
                  pl.BlockSpec((1, GROUP_DIM), lambda m, g: (0, g))],
        out_specs=[pl.BlockSpec((tm, GROUP_DIM), blk)] * 3,
        out_shape=[jax.ShapeDtypeStruct((T, D_INNER), BF16),
                   jax.ShapeDtypeStruct((T, D_INNER), BF16),
                   jax.ShapeDtypeStruct((T, D_INNER), F32)],
        compiler_params=_params(2),
    )(proj0, proj0, proj0, wg4, scale_row)


def _pool_gate_bwd(dh, w_out, proj0, mixed, scale_row, name, tm=1024, tn=512):
    T = dh.shape[0]
    gate_b0 = D_INNER // tn

    def body(dh_ref, w_ref, z_ref, mix_ref, s_ref, dm_ref, dz_ref, ds_ref):
        @pl.when(pl.program_id(1) == 0)
        def _():
            ds_ref[...] = jnp.zeros_like(ds_ref)

        dy = _dot_nt(dh_ref[...].astype(BF16), w_ref[...])
        z = z_ref[...]
        sig = _sigmoid(z)
        silu = z * sig
        mixed = mix_ref[...]
        s = s_ref[...]
        dm_ref[...] = (dy * s * silu).astype(BF16)
        dz_ref[...] = (dy * mixed * s * (sig * (1.0 + z * (1.0 - sig)))).astype(BF16)
        ds_ref[...] += jnp.sum(dy * mixed * silu, axis=0, keepdims=True)

    return pl.pallas_call(
        body, name=name, grid=(D_INNER // tn, T // tm),
        in_specs=[pl.BlockSpec((tm, D_MODEL), lambda n, m: (m, 0)),
                  pl.BlockSpec((tn, D_MODEL), lambda n, m: (n, 0)),
                  pl.BlockSpec((tm, tn), lambda n, m: (m, gate_b0 + n)),
                  pl.BlockSpec((tm, tn), lambda n, m: (m, n)),
                  pl.BlockSpec((1, tn), lambda n, m: (0, n))],
        out_specs=[pl.BlockSpec((tm, tn), lambda n, m: (m, n)),
                   pl.BlockSpec((tm, tn), lambda n, m: (m, gate_b0 + n)),
                   pl.BlockSpec((1, tn), lambda n, m: (0, n))],
        out_shape=[jax.ShapeDtypeStruct((T, D_INNER), BF16),
                   jax.ShapeDtypeStruct((T, 2 * D_INNER), BF16),
                   jax.ShapeDtypeStruct((1, D_INNER), F32)],
        compiler_params=_params(2),
    )(dh, w_out, proj0, mixed, scale_row)


def _pool_bwd(dmixed, wg4, dproj0, S, name, tm=256):
    T = dmixed.shape[0]
    blocks_per_seq = S // tm
    hb = tm // HALO
    n_halo_blocks = T // HALO

    def body(dm_ref, halo_ref, w_ref, _, o_ref):
        m, g = pl.program_id(0), pl.program_id(1)
        ext = jnp.concatenate([dm_ref[...], halo_ref[...]], axis=0)
        dp = _dot_nt(ext, w_ref[...].reshape(GROUP_DIM, GROUP_DIM))
        pos = (m % blocks_per_seq) * tm + lax.broadcasted_iota(jnp.int32, (tm + HALO, 1), 0)
        count = jnp.minimum(pos + 1, _window_of(g)).astype(F32)
        c = jnp.where(pos < S, dp / count, 0.0)
        n = tm + HALO
        stages = []
        cur = c
        for sh in (1, 2, 4, 8):
            cur = cur + pltpu.roll(cur, n - sh, 0)
            stages.append(cur[:tm, :])
        o_ref[...] = (_select_stage(g, stages) - dp[:tm, :]).astype(BF16)

    blk = lambda m, g: (m, g)
    return pl.pallas_call(
        body, name=name, grid=(T // tm, N_GROUPS),
        in_specs=[pl.BlockSpec((tm, GROUP_DIM), blk),
                  pl.BlockSpec((HALO, GROUP_DIM),
                               lambda m, g: (jnp.minimum((m + 1) * hb, n_halo_blocks - 1), g)),
                  pl.BlockSpec((N_CHIPS, GROUP_DIM // N_CHIPS, GROUP_DIM), lambda m, g: (0, g, 0)),
                  HBM_SPEC],
        out_specs=pl.BlockSpec((tm, GROUP_DIM), blk),
        out_shape=jax.ShapeDtypeStruct(dproj0.shape, dproj0.dtype),
        input_output_aliases={3: 0},
        compiler_params=_params(2),
    )(dmixed, dmixed, wg4, dproj0)


TQ = 256


def _split_dot(x, m):
    hi = x.astype(BF16)
    lo = (x - hi.astype(F32)).astype(BF16)
    return _dot(hi, m) + _dot(lo, m)


NEG_LOG2E = -1.4426950408889634


def _log_terms(z):
    soft = jnp.log(1.0 + jnp.exp2(jnp.abs(z) * NEG_LOG2E))
    log_beta = jnp.minimum(z, 0.0) - soft
    return log_beta, log_beta - z


HEADS = 4
QK_W = HEADS * 64
V_W = HEADS * HEAD_V
N_HEAD_GROUPS = 16 // HEADS


def _masked_heads(x):
    lane = lax.broadcasted_iota(jnp.int32, (1, HEAD_PAIR_QK), 1)
    out = []
    for hh in range(HEADS):
        slab = x[:, (hh // 2) * HEAD_PAIR_QK:(hh // 2 + 1) * HEAD_PAIR_QK]
        out.append(jnp.where((lane // 64) == hh % 2, slab, jnp.zeros_like(slab)))
    return out


def _attn_fwd(qkv, z1, S, name):
    T = qkv.shape[0]
    nq = S // TQ
    k_b0 = QK_WIDTH // QK_W
    v_b0 = 2 * QK_WIDTH // V_W
    hs = range(HEADS)

    def body(q_ref, k_ref, v_ref, z_ref, o_ref, y_ref, lt_ref):
        qi = pl.program_id(2)
        row = lax.broadcasted_iota(jnp.int32, (TQ, TQ), 0)
        col = lax.broadcasted_iota(jnp.int32, (TQ, TQ), 1)
        causal = col < row
        later_in_block = (row > col).astype(BF16)
        qms = [qm * 0.125 for qm in _masked_heads(q_ref[...])]

        def step(j, carry, diagonal):
            koff = pl.multiple_of(j * TQ, TQ)
            kbs = [k_ref[pl.ds(koff, TQ), p * HEAD_PAIR_QK:(p + 1) * HEAD_PAIR_QK]
                   for p in range(HEADS // 2)]
            run, acc = [carry[2 * hh] for hh in hs], [carry[2 * hh + 1] for hh in hs]
            z = [_dot_nt(qms[hh], kbs[hh // 2]) for hh in hs]
            terms = [_log_terms(z[hh]) for hh in hs]
            log_om = [jnp.where(causal, t[1], 0.0) if diagonal else t[1] for t in terms]
            later = [_split_dot(log_om[hh], later_in_block) for hh in hs]
            a = [jnp.exp(terms[hh][0] + (run[hh] + later[hh])) for hh in hs]
            if diagonal:
                a = [jnp.where(causal, a[hh], 0.0) for hh in hs]
            out = []
            for hh in hs:
                vb = v_ref[pl.ds(koff, TQ), hh * HEAD_V:(hh + 1) * HEAD_V]
                out += [run[hh] + jnp.sum(log_om[hh], axis=1, keepdims=True),
                        acc[hh] + _dot(a[hh].astype(BF16), vb)]
            return tuple(out)

        zero = (jnp.zeros((TQ, 1), F32), jnp.zeros((TQ, HEAD_V), F32))
        carry = step(qi, zero * HEADS, True)
        carry = lax.fori_loop(0, qi, lambda i, c: step(qi - 1 - i, c, False), carry)
        for hh in hs:
            sl = slice(hh * HEAD_V, (hh + 1) * HEAD_V)
            acc = carry[2 * hh + 1]
            z = z_ref[:, sl]
            o_ref[:, sl] = acc
            y_ref[:, sl] = (acc * (z * _sigmoid(z))).astype(BF16)
            lt_ref[:, hh:hh + 1] = carry[2 * hh]

    qblk = lambda b, p, i: (b * nq + i, p)
    return pl.pallas_call(
        body, name=name, grid=(T // S, N_HEAD_GROUPS, nq),
        in_specs=[pl.BlockSpec((TQ, QK_W), qblk),
                  pl.BlockSpec((S, QK_W), lambda b, p, i: (b, k_b0 + p)),
                  pl.BlockSpec((S, V_W), lambda b, p, i: (b, v_b0 + p)),
                  pl.BlockSpec((TQ, V_W), qblk)],
        out_specs=[pl.BlockSpec((TQ, V_W), qblk),
                   pl.BlockSpec((TQ, V_W), qblk),
                   pl.BlockSpec((None, TQ, HEADS), lambda b, p, i: (p, b * nq + i, 0))],
        out_shape=[jax.ShapeDtypeStruct((T, D_INNER), F32),
                   jax.ShapeDtypeStruct((T, D_INNER), BF16),
                   jax.ShapeDtypeStruct((N_HEAD_GROUPS, T, HEADS), F32)],
        compiler_params=_params(3),
    )(qkv, qkv, qkv, z1)


def _attn_gate_bwd(dh, w_out, z1, o, name, tm=1024, tn=512):
    T = dh.shape[0]
    gate_b0 = (2 * QK_WIDTH + D_INNER) // tn

    def body(dh_ref, w_ref, z_ref, o_ref, do_ref, dz_ref, dh_s):
        @pl.when(pl.program_id(1) == 0)
        def _():
            dh_s[...] = dh_ref[...].astype(BF16)

        dy = _dot_nt(dh_s[...], w_ref[...])
        z = z_ref[...]
        sig = _sigmoid(z)
        do_ref[...] = (dy * (z * sig)).astype(BF16)
        dz_ref[...] = (dy * o_ref[...] * (sig * (1.0 + z * (1.0 - sig)))).astype(BF16)

    return pl.pallas_call(
        body, name=name, grid=(T // tm, D_INNER // tn),
        in_specs=[pl.BlockSpec((tm, D_MODEL), lambda m, n: (m, 0)),
                  pl.BlockSpec((tn, D_MODEL), lambda m, n: (n, 0)),
                  pl.BlockSpec((tm, tn), lambda m, n: (m, n)),
                  pl.BlockSpec((tm, tn), lambda m, n: (m, n))],
        out_specs=[pl.BlockSpec((tm, tn), lambda m, n: (m, n)),
                   pl.BlockSpec((tm, tn), lambda m, n: (m, gate_b0 + n))],
        out_shape=[jax.ShapeDtypeStruct((T, D_INNER), BF16),
                   jax.ShapeDtypeStruct((T, 2 * QK_WIDTH + 2 * D_INNER), BF16)],
        scratch_shapes=[pltpu.VMEM((tm, D_MODEL), BF16)],
        compiler_params=_params(2),
    )(dh, w_out, z1, o)


def _attn_bwd(qkv, do, ltot, dproj1, S, name):
    T = qkv.shape[0]
    nq = S // TQ
    k_b0 = QK_WIDTH // QK_W
    v_b0 = 2 * QK_WIDTH // V_W
    hs = range(HEADS)
    pairs = range(HEADS // 2)

    def body(q_ref, k_ref, v_ref, do_ref, lt_ref, _, out_ref, dq_s, dk_s, dv_s, dkb_s, dvb_s, sems):
        b, p = pl.program_id(0), pl.program_id(1)
        row = lax.broadcasted_iota(jnp.int32, (TQ, TQ), 0)
        col = lax.broadcasted_iota(jnp.int32, (TQ, TQ), 1)
        causal = col < row
        upto = (row <= col).astype(BF16)
        before = (row < col).astype(BF16)
        dk_s[...] = jnp.zeros_like(dk_s)
        dv_s[...] = jnp.zeros_like(dv_s)

        def q_block(qi, _):
            qoff = pl.multiple_of(qi * TQ, TQ)
            qms = [qm * 0.125 for qm in _masked_heads(q_ref[pl.ds(qoff, TQ), :])]
            vsl = [slice(hh * HEAD_V, (hh + 1) * HEAD_V) for hh in hs]
            psl = [slice(pp * HEAD_PAIR_QK, (pp + 1) * HEAD_PAIR_QK) for pp in pairs]
            do_h = [do_ref[pl.ds(qoff, TQ), sl] for sl in vsl]
            total = [lt_ref[pl.ds(qoff, TQ), hh:hh + 1] for hh in hs]

            def k_block(j, carry, diagonal):
                koff = pl.multiple_of(j * TQ, TQ)
                kms = _masked_heads(k_ref[pl.ds(koff, TQ), :])
                g_before = [carry[2 * hh] for hh in hs]
                lom_before = [carry[2 * hh + 1] for hh in hs]
                z = [_dot_nt(qms[hh], kms[hh]) for hh in hs]
                da = [_dot_nt(do_h[hh], v_ref[pl.ds(koff, TQ), vsl[hh]]) for hh in hs]
                terms = [_log_terms(z[hh]) for hh in hs]
                log_om = [jnp.where(causal, t[1], 0.0) if diagonal else t[1] for t in terms]
                prefix = [_split_dot(log_om[hh], upto) for hh in hs]
                a = [jnp.exp(terms[hh][0] + ((total[hh] - lom_before[hh]) - prefix[hh])) for hh in hs]
                if diagonal:
                    a = [jnp.where(causal, a[hh], 0.0) for hh in hs]
                g = [a[hh] * da[hh] for hh in hs]
                g_prefix = [_dot(g[hh].astype(BF16), before) for hh in hs]
                out, dzs = [], []
                for hh in hs:
                    beta = jnp.exp(terms[hh][0])
                    g_excl = (g_before[hh] + g_prefix[hh]) * beta
                    if diagonal:
                        g_excl = jnp.where(causal, g_excl, 0.0)
                    dzs.append((g[hh] * (1.0 - beta) - g_excl).astype(BF16))
                    out += [g_before[hh] + jnp.sum(g[hh], axis=1, keepdims=True),
                            lom_before[hh] + jnp.sum(log_om[hh], axis=1, keepdims=True)]
                for hh in hs:
                    dv_s[pl.ds(koff, TQ), vsl[hh]] += _dot_tn(a[hh].astype(BF16), do_h[hh])
                dq = []
                for pp in pairs:
                    pair = slice(2 * pp, 2 * pp + 2)
                    dq.append(carry[2 * HEADS + pp] + _dot(jnp.concatenate(dzs[pair], axis=1),
                                                           jnp.concatenate(kms[pair], axis=0)))
                    dk_s[pl.ds(koff, TQ), psl[pp]] += _dot_tn(jnp.concatenate(dzs[pair], axis=0),
                                                              jnp.concatenate(qms[pair], axis=0))
                return tuple(out) + tuple(dq)

            zero = jnp.zeros((TQ, 1), F32)
            carry = (zero,) * (2 * HEADS) + (jnp.zeros((TQ, HEAD_PAIR_QK), F32),) * (HEADS // 2)
            carry = lax.fori_loop(0, qi, lambda j, c: k_block(j, c, False), carry)
            carry = k_block(qi, carry, True)
            for pp in pairs:
                dq_s[pl.ds(qoff, TQ), psl[pp]] = (carry[2 * HEADS + pp] * 0.125).astype(BF16)
            return 0

        lax.fori_loop(0, nq, q_block, 0)
        dkb_s[...] = dk_s[...].astype(BF16)
        dvb_s[...] = dv_s[...].astype(BF16)
        rows = pl.ds(pl.multiple_of(b * S, TQ), S)
        copies = [
            pltpu.make_async_copy(
                dq_s, out_ref.at[rows, pl.ds(pl.multiple_of(p * QK_W, 128), QK_W)], sems.at[0]),
            pltpu.make_async_copy(
                dkb_s, out_ref.at[rows, pl.ds(pl.multiple_of(QK_WIDTH + p * QK_W, 128), QK_W)],
                sems.at[1]),
            pltpu.make_async_copy(
                dvb_s, out_ref.at[rows, pl.ds(pl.multiple_of(2 * QK_WIDTH + p * V_W, 128), V_W)],
                sems.at[2]),
        ]
        for cp in copies:
            cp.start()
        for cp in copies:
            cp.wait()

    return pl.pallas_call(
        body, name=name, grid=(T // S, N_HEAD_GROUPS),
        in_specs=[pl.BlockSpec((S, QK_W), lambda b, p: (b, p)),
                  pl.BlockSpec((S, QK_W), lambda b, p: (b, k_b0 + p)),
                  pl.BlockSpec((S, V_W), lambda b, p: (b, v_b0 + p)),
                  pl.BlockSpec((S, V_W), lambda b, p: (b, p)),
                  pl.BlockSpec((None, S, HEADS), lambda b, p: (p, b, 0)),
                  HBM_SPEC],
        out_specs=HBM_SPEC,
        out_shape=jax.ShapeDtypeStruct(dproj1.shape, dproj1.dtype),
        input_output_aliases={5: 0},
        scratch_shapes=[pltpu.VMEM((S, QK_W), BF16),
                        pltpu.VMEM((S, QK_W), F32),
                        pltpu.VMEM((S, V_W), F32),
                        pltpu.VMEM((S, QK_W), BF16),
                        pltpu.VMEM((S, V_W), BF16),
                        pltpu.SemaphoreType.DMA((3,))],
        compiler_params=_params(2),
    )(qkv, qkv, qkv, do, ltot, dproj1)


def _loss_head(h, g_row, target, name, tm=512):
    T = h.shape[0]

    def body(h_ref, g_ref, t_ref, dh_ref, dg_ref, loss_ref):
        @pl.when(pl.program_id(0) == 0)
        def _():
            dg_ref[...] = jnp.zeros_like(dg_ref)
            loss_ref[...] = jnp.zeros_like(loss_ref)

        x = h_ref[...]
        inv = lax.rsqrt(jnp.mean(x * x, axis=-1, keepdims=True) + RMS_EPS)
        xhat = x * inv
        gain = g_ref[...]
        err = xhat * gain - t_ref[...]
        per_token = jnp.mean(err * err, axis=-1, keepdims=True)
        loss_ref[...] += 0.5 * jnp.sum(per_token, axis=0, keepdims=True)
        dy = err * (1.0 / D_MODEL)
        dg_ref[...] += jnp.sum(dy * xhat, axis=0, keepdims=True)
        dxh = dy * gain
        proj = jnp.mean(dxh * xhat, axis=-1, keepdims=True)
        dh_ref[...] = inv * (dxh - xhat * proj)

    return pl.pallas_call(
        body, name=name, grid=(T // tm,),
        in_specs=[pl.BlockSpec((tm, D_MODEL), lambda m: (m, 0)),
                  pl.BlockSpec((1, D_MODEL), lambda m: (0, 0)),
                  pl.BlockSpec((tm, D_MODEL), lambda m: (m, 0))],
        out_specs=[pl.BlockSpec((tm, D_MODEL), lambda m: (m, 0)),
                   pl.BlockSpec((1, D_MODEL), lambda m: (0, 0)),
                   pl.BlockSpec((1, 128), lambda m: (0, 0))],
        out_shape=[jax.ShapeDtypeStruct((T, D_MODEL), F32),
                   jax.ShapeDtypeStruct((1, D_MODEL), F32),
                   jax.ShapeDtypeStruct((1, 128), F32)],
        compiler_params=_params(1),
    )(h, g_row, target)


def _place():
    return lax.axis_index("x"), lax.axis_index("y"), lax.axis_index("c")


def _other_chips(x, y):
    return [(1 - x, y), (x, 1 - y), (1 - x, 1 - y)]


def _half(ref, c):
    hr = ref.shape[-2] // 2
    return pl.ds(pl.multiple_of(c * hr, 8), hr)


def _cast_to_slot(shard, chip, name, tr=256):
    R, C = shard.shape

    def body(chip_ref, w_ref, o_ref):
        o_ref[0] = w_ref[...].astype(BF16)

    return pl.pallas_call(
        body, name=name,
        grid_spec=pltpu.PrefetchScalarGridSpec(
            num_scalar_prefetch=1, grid=(R // tr,),
            in_specs=[pl.BlockSpec((tr, C), lambda i, chip_ref: (i, 0))],
            out_specs=pl.BlockSpec((1, tr, C), lambda i, chip_ref: (chip_ref[0], i, 0))),
        out_shape=jax.ShapeDtypeStruct((N_CHIPS, R, C), BF16),
        compiler_params=_params(1),
    )(chip, shard)


def _allgather_weights(slots, name):
    n = len(slots)

    def body(*refs):
        outs = refs[n:2 * n]
        send_sems, recv_sems, fwd_send, fwd_recv = refs[2 * n:]
        x, y, c = _place()
        chips = _other_chips(x, y)

        def landing(a, chip, half_of):
            return outs[a].at[2 * chip[0] + chip[1], _half(outs[a], half_of)]

        def ici(a, k, chip_from, to):
            return pltpu.make_async_remote_copy(
                src_ref=landing(a, chip_from, c), dst_ref=landing(a, chip_from, c),
                send_sem=send_sems.at[a, k], recv_sem=recv_sems.at[a, k],
                device_id=to, device_id_type=MESH)

        def d2d(a, k, chip_from, half_of):
            return pltpu.make_async_remote_copy(
                src_ref=landing(a, chip_from, half_of), dst_ref=landing(a, chip_from, half_of),
                send_sem=fwd_send.at[a, k], recv_sem=fwd_recv.at[a, k],
                device_id=(x, y, 1 - c), device_id_type=MESH)

        sends = [ici(a, k, (x, y), (*chips[k], c)) for a in range(n) for k in range(3)]
        for cp in sends:
            cp.start()
        forwards = []
        for a in range(n):
            for k in range(3):
                ici(a, k, chips[k], (x, y, c)).wait_recv()
                fw = d2d(a, k, chips[k], c)
                fw.start()
                forwards.append(fw)
        for a in range(n):
            for k in range(3):
                d2d(a, k, chips[k], 1 - c).wait_recv()
        for cp in sends + forwards:
            cp.wait_send()

    return pl.pallas_call(
        body, name=name,
        in_specs=[HBM_SPEC] * n, out_specs=[HBM_SPEC] * n,
        out_shape=[jax.ShapeDtypeStruct(s.shape, s.dtype) for s in slots],
        input_output_aliases={a: a for a in range(n)},
        scratch_shapes=[pltpu.SemaphoreType.DMA((n, 3)), pltpu.SemaphoreType.DMA((n, 3)),
                        pltpu.SemaphoreType.DMA((n, 3)), pltpu.SemaphoreType.DMA((n, 3))],
    )(*slots)


def _sibling_exchange(partials, small, name):
    n = len(partials)

    def body(*refs):
        ins, small_ref = refs[:n], refs[n]
        outs, small_all = refs[n + 1:2 * n + 1], refs[2 * n + 1]
        send_sems, recv_sems, s_send, s_recv, loc_sem = refs[2 * n + 2:]
        x, y, c = _place()
        me = 4 * x + 2 * y + c
        local = pltpu.make_async_copy(small_ref, small_all.at[me], loc_sem)
        local.start()
        big = [pltpu.make_async_remote_copy(
            src_ref=ins[a].at[:, _half(ins[a], 1 - c)], dst_ref=outs[a],
            send_sem=send_sems.at[a], recv_sem=recv_sems.at[a],
            device_id=(x, y, 1 - c), device_id_type=MESH) for a in range(n)]
        tiny = []
        for d in range(1, N_DEV):
            px, py, pc = x ^ ((d >> 2) & 1), y ^ ((d >> 1) & 1), c ^ (d & 1)
            tiny.append(pltpu.make_async_remote_copy(
                src_ref=small_ref, dst_ref=small_all.at[me],
                send_sem=s_send.at[d - 1], recv_sem=s_recv.at[d - 1],
                device_id=(px, py, pc), device_id_type=MESH))
        for cp in tiny + big:
            cp.start()
        for d in range(1, N_DEV):
            peer = me ^ d
            pltpu.make_async_remote_copy(
                src_ref=small_ref, dst_ref=small_all.at[peer],
                send_sem=s_send.at[d - 1], recv_sem=s_recv.at[d - 1],
                device_id=(x, y, c), device_id_type=MESH).wait_recv()
        for cp in big:
            cp.wait_recv()
        for cp in tiny + big:
            cp.wait_send()
        local.wait()

    return pl.pallas_call(
        body, name=name,
        in_specs=[HBM_SPEC] * (n + 1), out_specs=[HBM_SPEC] * (n + 1),
        out_shape=[jax.ShapeDtypeStruct((N_CHIPS, p.shape[1] // 2, p.shape[2]), F32)
                   for p in partials] + [jax.ShapeDtypeStruct((N_DEV,) + small.shape, F32)],
        scratch_shapes=[pltpu.SemaphoreType.DMA((n,)), pltpu.SemaphoreType.DMA((n,)),
                        pltpu.SemaphoreType.DMA((N_DEV - 1,)), pltpu.SemaphoreType.DMA((N_DEV - 1,)),
                        pltpu.SemaphoreType.DMA],
    )(*partials, small)


def _chip_sum(partial, from_sibling, c, name, tr=256):
    _, hr, C = from_sibling.shape
    nb = hr // tr

    def body(c_ref, p_ref, s_ref, o_ref):
        o_ref[...] = (p_ref[...] + s_ref[...]).astype(BF16)

    return pl.pallas_call(
        body, name=name,
        grid_spec=pltpu.PrefetchScalarGridSpec(
            num_scalar_prefetch=1, grid=(N_CHIPS, nb),
            in_specs=[pl.BlockSpec((1, tr, C), lambda j, i, c_ref: (j, c_ref[0] * nb + i, 0)),
                      pl.BlockSpec((1, tr, C), lambda j, i, c_ref: (j, i, 0))],
            out_specs=pl.BlockSpec((1, tr, C), lambda j, i, c_ref: (j, i, 0))),
        out_shape=jax.ShapeDtypeStruct(from_sibling.shape, BF16),
        compiler_params=_params(2),
    )(c, partial, from_sibling)


def _send_chip_sums(sums, name):
    n = len(sums)

    def body(*refs):
        ins, outs = refs[:n], refs[n:2 * n]
        send_sems, recv_sems = refs[2 * n:]
        x, y, c = _place()
        me = 2 * x + y
        chips = _other_chips(x, y)
        sends = [pltpu.make_async_remote_copy(
            src_ref=ins[a].at[2 * chips[k][0] + chips[k][1]], dst_ref=outs[a].at[k],
            send_sem=send_sems.at[a, k], recv_sem=recv_sems.at[a, k],
            device_id=(*chips[k], c), device_id_type=MESH) for a in range(n) for k in range(3)]
        for cp in sends:
            cp.start()
        for a in range(n):
            for k in range(3):
                pltpu.make_async_remote_copy(
                    src_ref=ins[a].at[me], dst_ref=outs[a].at[k],
                    send_sem=send_sems.at[a, k], recv_sem=recv_sems.at[a, k],
                    device_id=(x, y, c), device_id_type=MESH).wait_recv()
        for cp in sends:
            cp.wait_send()

    return pl.pallas_call(
        body, name=name,
        in_specs=[HBM_SPEC] * n, out_specs=[HBM_SPEC] * n,
        out_shape=[jax.ShapeDtypeStruct((3,) + s.shape[1:], BF16) for s in sums],
        scratch_shapes=[pltpu.SemaphoreType.DMA((n, 3)), pltpu.SemaphoreType.DMA((n, 3))],
    )(*sums)


def _reduce_half(partial, from_sibling, received, place, name, tr=256):
    _, hr, C = from_sibling.shape
    nb = hr // tr

    def body(p_ref, mine_ref, sib_ref, r_ref, o_ref):
        acc = mine_ref[0] + sib_ref[0]
        for k in range(3):
            acc = acc + r_ref[k].astype(F32)
        o_ref[...] = acc

    return pl.pallas_call(
        body, name=name,
        grid_spec=pltpu.PrefetchScalarGridSpec(
            num_scalar_prefetch=1, grid=(nb,),
            in_specs=[pl.BlockSpec((1, tr, C), lambda i, p: (p[0], p[1] * nb + i, 0)),
                      pl.BlockSpec((1, tr, C), lambda i, p: (p[0], i, 0)),
                      pl.BlockSpec((3, tr, C), lambda i, p: (0, i, 0))],
            out_specs=pl.BlockSpec((tr, C), lambda i, p: (p[1] * nb + i, 0))),
        out_shape=jax.ShapeDtypeStruct((2 * hr, C), F32),
        compiler_params=_params(1),
    )(place, partial, from_sibling, received)


def _join_halves(fulls, name):
    n = len(fulls)

    def body(*refs):
        outs = refs[n:2 * n]
        send_sems, recv_sems = refs[2 * n:]
        x, y, c = _place()

        def copy(a, half_of, to):
            rows = outs[a].at[_half(outs[a], half_of)]
            return pltpu.make_async_remote_copy(
                src_ref=rows, dst_ref=rows, send_sem=send_sems.at[a], recv_sem=recv_sems.at[a],
                device_id=to, device_id_type=MESH)

        sends = [copy(a, c, (x, y, 1 - c)) for a in range(n)]
        for cp in sends:
            cp.start()
        for a in range(n):
            copy(a, 1 - c, (x, y, c)).wait_recv()
        for cp in sends:
            cp.wait_send()

    return pl.pallas_call(
        body, name=name,
        in_specs=[HBM_SPEC] * n, out_specs=[HBM_SPEC] * n,
        out_shape=[jax.ShapeDtypeStruct(f.shape, F32) for f in fulls],
        input_output_aliases={a: a for a in range(n)},
        scratch_shapes=[pltpu.SemaphoreType.DMA((n,)), pltpu.SemaphoreType.DMA((n,))],
    )(*fulls)


def _adamw_math(w, g, m, v):
    m = ADAM_B1 * m + (1.0 - ADAM_B1) * g
    v = ADAM_B2 * v + (1.0 - ADAM_B2) * (g * g)
    m_hat = m / (1.0 - ADAM_B1 ** ADAM_STEP)
    v_hat = v / (1.0 - ADAM_B2 ** ADAM_STEP)
    delta = -ADAM_LR * (m_hat / (jnp.sqrt(v_hat) + ADAM_EPS) + ADAM_WD * w)
    return delta, m, v


def _adamw(w, g, m, v, name, tr=256):
    R, C = w.shape
    tr = min(tr, R)

    def body(w_ref, g_ref, m_ref, v_ref, d_out, m_out, v_out):
        d_out[...], m_out[...], v_out[...] = _adamw_math(w_ref[...], g_ref[...], m_ref[...], v_ref[...])

    spec = pl.BlockSpec((tr, C), lambda i: (i, 0))
    return pl.pallas_call(
        body, name=name, grid=(R // tr,),
        in_specs=[spec] * 4, out_specs=[spec] * 3,
        out_shape=[jax.ShapeDtypeStruct((R, C), F32)] * 3,
        compiler_params=_params(1),
    )(w, g, m, v)


def _adamw_small(small_all, w, m, v, name):
    def body(s_ref, w_ref, m_ref, v_ref, g_out, d_out, m_out, v_out):
        g = s_ref[0]
        for d in range(1, N_DEV):
            g = g + s_ref[d]
        g_out[...] = g
        d_out[...], m_out[...], v_out[...] = _adamw_math(w_ref[...], g, m_ref[...], v_ref[...])

    vm = pl.BlockSpec(memory_space=pltpu.VMEM)
    return pl.pallas_call(
        body, name=name, in_specs=[vm] * 4, out_specs=[vm] * 4,
        out_shape=[jax.ShapeDtypeStruct(w.shape, F32)] * 4,
    )(small_all, w, m, v)


def _pack_small(norm_g, pool_scale, norm_f, extra_row):
    return jnp.concatenate([norm_g.reshape(2, D_MODEL), pool_scale.reshape(2, D_MODEL),
                            norm_f.reshape(1, D_MODEL), extra_row,
                            jnp.zeros((2, D_MODEL), F32)], axis=0)


def kernel(x, norm_g, pool_w_in, pool_w, pool_scale, pool_w_out, sb_w_in, sb_w_out, norm_f, loss_target, m_norm_g, m_pool_w_in, m_pool_w, m_pool_scale, m_pool_w_out, m_sb_w_in, m_sb_w_out, m_norm_f, v_norm_g, v_pool_w_in, v_pool_w, v_pool_scale, v_pool_w_out, v_sb_w_in, v_sb_w_out, v_norm_f):
    nb, S, _ = x.shape
    T = nb * S
    xt = x.reshape(T, D_MODEL)
    target = loss_target.reshape(T, D_MODEL)
    cx, cy, cc = _place()

    def shard2d(w):
        return w.reshape(-1, w.shape[-1])

    names = ("pool_w_in", "pool_w", "pool_w_out", "sb_w_in", "sb_w_out")
    w_shards = [shard2d(w) for w in (pool_w_in, pool_w, pool_w_out, sb_w_in, sb_w_out)]
    m_shards = [shard2d(w) for w in (m_pool_w_in, m_pool_w, m_pool_w_out, m_sb_w_in, m_sb_w_out)]
    v_shards = [shard2d(w) for w in (v_pool_w_in, v_pool_w, v_pool_w_out, v_sb_w_in, v_sb_w_out)]

    chip = (2 * cx + cy).reshape(1).astype(jnp.int32)
    w_pin, w_g, w_pout, w_sin, w_sout = _allgather_weights(
        [_cast_to_slot(w, chip, "cast_" + nm) for w, nm in zip(w_shards, names)], "allgather_weights")
    w_pout = w_pout.reshape(D_INNER, D_MODEL)
    w_sout = w_sout.reshape(D_INNER, D_MODEL)
    g0, g1, gf = norm_g[0:1], norm_g[1:2], norm_f.reshape(1, D_MODEL)

    proj0, u0 = _rms_matmul(xt, g0, w_pin, [(2 * D_INNER, F32)], "pool_in_proj")
    y0, pooled, mixed = _pool_fwd(proj0, w_g, pool_scale, S, "pool_mix")
    h1 = _matmul_residual(y0, w_pout, xt, "pool_out_proj")
    qkv, z1, u1 = _rms_matmul(h1, g1, w_sin, [(2 * QK_WIDTH + D_INNER, BF16), (D_INNER, F32)],
                              "sb_in_proj")
    o, y1, ltot = _attn_fwd(qkv, z1, S, "sb_attention")
    h2 = _matmul_residual(y1, w_sout, h1, "sb_out_proj")
    dh2, d_norm_f, loss_row = _loss_head(h2, gf, target, "loss_head")

    shard = lambda i, j, t: (j, 0, 0)
    gw_sout = _matmul_tn(y1, dh2, D_INNER, D_MODEL, (D_INNER, D_MODEL), (1024, 1024),
                         lambda i, j, t: (i, j), "grad_sb_w_out", bm=1024, bn=1024)
    do, dproj1 = _attn_gate_bwd(dh2, w_sout, z1, o, "sb_gate_bwd")
    dproj1 = _attn_bwd(qkv, do, ltot, dproj1, S, "sb_attention_bwd")
    n1 = 2 * QK_WIDTH + 2 * D_INNER
    gw_sin = _matmul_tn(u1, dproj1, D_MODEL, n1, (N_CHIPS, D_MODEL, n1 // 4), (1, D_MODEL, n1 // 4),
                        shard, "grad_sb_w_in", bm=D_MODEL, bn=n1 // 4)
    dh1, d_g1 = _matmul_nt_rms_bwd(dproj1, w_sin, h1, g1, dh2, "sb_in_bwd")
    gw_pout = _matmul_tn(y0, dh1, D_INNER, D_MODEL, (D_INNER, D_MODEL), (1024, 1024),
                         lambda i, j, t: (i, j), "grad_pool_w_out", bm=1024, bn=1024)
    dmixed, dproj0, d_scale = _pool_gate_bwd(dh1, w_pout, proj0, mixed, pool_scale, "pool_gate_bwd")
    gw_g = _matmul_tn(pooled, dmixed, D_INNER, D_INNER, (N_CHIPS, GROUP_DIM, GROUP_DIM),
                      (N_CHIPS, GROUP_DIM // N_CHIPS, GROUP_DIM), lambda i, j, t: (0, i, 0),
                      "grad_pool_w", bm=GROUP_DIM, bn=GROUP_DIM, diagonal_blocks=True)
    dproj0 = _pool_bwd(dmixed, w_g, dproj0, S, "pool_bwd")
    n0 = 2 * D_INNER
    gw_pin = _matmul_tn(u0, dproj0, D_MODEL, n0, (N_CHIPS, D_MODEL, n0 // 4), (1, D_MODEL, n0 // 4),
                        shard, "grad_pool_w_in", bm=D_MODEL, bn=n0 // 4)
    dx, d_g0 = _matmul_nt_rms_bwd(dproj0, w_pin, xt, g0, dh1, "pool_in_bwd")

    partials = [gw_pin, gw_g, gw_pout.reshape(N_CHIPS, -1, D_MODEL), gw_sin,
                gw_sout.reshape(N_CHIPS, -1, D_MODEL)]
    small = _pack_small(jnp.concatenate([d_g0, d_g1], axis=0), d_scale, d_norm_f,
                        jnp.broadcast_to(loss_row[:, :1], (1, D_MODEL)))
    *from_sibling, small_all = _sibling_exchange(partials, small, "grad_sibling_exchange")
    c_arr = cc.reshape(1).astype(jnp.int32)
    place = jnp.stack([2 * cx + cy, cc]).astype(jnp.int32)
    sums = [_chip_sum(p, s, c_arr, "grad_chip_sum_" + nm)
            for p, s, nm in zip(partials, from_sibling, names)]
    received = _send_chip_sums(sums, "grad_chip_exchange")
    grads = _join_halves([_reduce_half(p, s, r, place, "grad_reduce_" + nm)
                          for p, s, r, nm in zip(partials, from_sibling, received, names)],
                         "grad_join_halves")

    deltas, new_m, new_v = [], [], []
    for w, g, m, v, nm in zip(w_shards, grads, m_shards, v_shards, names):
        d, mm, vv = _adamw(w, g, m, v, "adamw_" + nm)
        deltas.append(d)
        new_m.append(mm)
        new_v.append(vv)

    zero_row = jnp.zeros((1, D_MODEL), F32)
    g_small, d_small, m_small, v_small = _adamw_small(
        small_all, _pack_small(norm_g, pool_scale, norm_f, zero_row),
        _pack_small(m_norm_g, m_pool_scale, m_norm_f, zero_row),
        _pack_small(v_norm_g, v_pool_scale, v_norm_f, zero_row + 1.0), "adamw_small")
    loss = g_small[5, 0]

    def unpack_small(a):
        return a[0:2], a[2:4].reshape(1, D_INNER), a[4]

    def assemble(big, small3):
        ng, ps, nf = small3
        return [ng, big[0].reshape(pool_w_in.shape), big[1].reshape(pool_w.shape), ps,
                big[2].reshape(pool_w_out.shape), big[3].reshape(sb_w_in.shape),
                big[4].reshape(sb_w_out.shape), nf]

    return (loss, dx.reshape(x.shape),
            *assemble(grads, unpack_small(g_small)),
            *assemble(deltas, unpack_small(d_small)),
            *assemble(new_m, unpack_small(m_small)),
            *assemble(new_v, unpack_small(v_small)))
```

```python
import functools

import jax
import jax.numpy as jnp
from jax import lax
from jax.experimental import pallas as pl
from jax.experimental.pallas import tpu as pltpu

F32 = jnp.float32
BF16 = jnp.bfloat16
MESH = pl.DeviceIdType.MESH

D_MODEL = 1024
D_INNER = 2048
N_GROUPS = 4
GROUP_DIM = 512
HEAD_PAIR_QK = 128
HEAD_V = 128
N_HEAD_PAIRS = 8
QK_WIDTH = 1024
RMS_EPS = 1e-6
HALO = 16
N_CHIPS = 4
N_DEV = 8

ADAM_LR = 0.001
ADAM_B1 = 0.9
ADAM_B2 = 0.999
ADAM_EPS = 1e-08
ADAM_WD = 0.01
ADAM_STEP = 10

VMEM_LIMIT = 56 * 1024 * 1024

HBM_SPEC = pl.BlockSpec(memory_space=pltpu.HBM)


def _params(n_axes):
    return pltpu.CompilerParams(dimension_semantics=("arbitrary",) * n_axes,
                                vmem_limit_bytes=VMEM_LIMIT)


def _dot(a, b):
    return jnp.dot(a, b, preferred_element_type=F32)


def _dot_nt(a, b):
    return lax.dot_general(a, b, (((1,), (1,)), ((), ())), preferred_element_type=F32)


def _dot_tn(a, b):
    return lax.dot_general(a, b, (((0,), (0,)), ((), ())), preferred_element_type=F32)


def _sigmoid(z):
    return 1.0 / (1.0 + jnp.exp(-z))


def _rms_matmul(h, g_row, w4, outs, name, tm=1024, tn=512):
    T = h.shape[0]
    per_shard = w4.shape[2] // tn
    starts = [0]
    for width, _ in outs:
        starts.append(starts[-1] + width // tn)
    n_out = len(outs)

    def body(h_ref, g_ref, w_ref, *rest):
        o_refs, u_out, u_s = rest[:n_out], rest[n_out], rest[n_out + 1]
        n = pl.program_id(1)

        @pl.when(n == 0)
        def _():
            x = h_ref[...]
            inv = lax.rsqrt(jnp.mean(x * x, axis=-1, keepdims=True) + RMS_EPS)
            u = (x * inv * g_ref[...]).astype(BF16)
            u_s[...] = u
            u_out[...] = u

        res = _dot(u_s[...], w_ref[0])
        for k in range(n_out):
            @pl.when((n >= starts[k]) & (n < starts[k + 1]))
            def _():
                o_refs[k][...] = res.astype(o_refs[k].dtype)

    def out_map(k):
        return lambda m, n: (m, jnp.clip(n - starts[k], 0, starts[k + 1] - starts[k] - 1))

    return pl.pallas_call(
        body, name=name, grid=(T // tm, starts[-1]),
        in_specs=[pl.BlockSpec((tm, D_MODEL), lambda m, n: (m, 0)),
                  pl.BlockSpec((1, D_MODEL), lambda m, n: (0, 0)),
                  pl.BlockSpec((1, D_MODEL, tn), lambda m, n: (n // per_shard, 0, n % per_shard))],
        out_specs=[pl.BlockSpec((tm, tn), out_map(k)) for k in range(n_out)]
        + [pl.BlockSpec((tm, D_MODEL), lambda m, n: (m, 0))],
        out_shape=[jax.ShapeDtypeStruct((T, width), dt) for width, dt in outs]
        + [jax.ShapeDtypeStruct((T, D_MODEL), BF16)],
        scratch_shapes=[pltpu.VMEM((tm, D_MODEL), BF16)],
        compiler_params=_params(2),
    )(h, g_row, w4)


def _matmul_residual(a, w, res, name, tm=1024, tn=512):
    T, K = a.shape
    N = w.shape[1]

    def body(a_ref, w_ref, r_ref, o_ref):
        o_ref[...] = r_ref[...] + _dot(a_ref[...], w_ref[...])

    return pl.pallas_call(
        body, name=name, grid=(T // tm, N // tn),
        in_specs=[pl.BlockSpec((tm, K), lambda m, n: (m, 0)),
                  pl.BlockSpec((K, tn), lambda m, n: (0, n)),
                  pl.BlockSpec((tm, tn), lambda m, n: (m, n))],
        out_specs=pl.BlockSpec((tm, tn), lambda m, n: (m, n)),
        out_shape=jax.ShapeDtypeStruct((T, N), F32),
        compiler_params=_params(2),
    )(a, w, res)


def _matmul_tn(a, b, a_cols, b_cols, out_shape, out_block, out_map, name, bm, bn, tk=512,
               diagonal_blocks=False):
    T = a.shape[0]

    def body(a_ref, b_ref, o_ref):
        @pl.when(pl.program_id(2) == 0)
        def _():
            o_ref[...] = jnp.zeros_like(o_ref)

        part = _dot_tn(a_ref[...].astype(BF16), b_ref[...].astype(BF16))
        o_ref[...] += part.reshape(o_ref.shape)

    b_map = (lambda i, j, t: (t, i)) if diagonal_blocks else (lambda i, j, t: (t, j))
    return pl.pallas_call(
        body, name=name, grid=(a_cols // bm, 1 if diagonal_blocks else b_cols // bn, T // tk),
        in_specs=[pl.BlockSpec((tk, bm), lambda i, j, t: (t, i)),
                  pl.BlockSpec((tk, bn), b_map)],
        out_specs=pl.BlockSpec(out_block, out_map),
        out_shape=jax.ShapeDtypeStruct(out_shape, F32),
        compiler_params=_params(3),
    )(a, b)


def _matmul_nt_rms_bwd(dproj, w4, h, g_row, dres, name, tm=1024, tk=512):
    T, cols = dproj.shape
    per_shard = w4.shape[2] // tk
    nk = cols // tk

    def body(dp_ref, w_ref, h_ref, g_ref, r_ref, dx_ref, dg_ref, acc):
        m, k = pl.program_id(0), pl.program_id(1)

        @pl.when(k == 0)
        def _():
            acc[...] = jnp.zeros_like(acc)

        @pl.when((k == 0) & (m == 0))
        def _():
            dg_ref[...] = jnp.zeros_like(dg_ref)

        acc[...] += _dot_nt(dp_ref[...], w_ref[0])

        @pl.when(k == nk - 1)
        def _():
            du = acc[...]
            x = h_ref[...]
            inv = lax.rsqrt(jnp.mean(x * x, axis=-1, keepdims=True) + RMS_EPS)
            xhat = x * inv
            dg_ref[...] += jnp.sum(du * xhat, axis=0, keepdims=True)
            dxh = du * g_ref[...]
            proj = jnp.mean(dxh * xhat, axis=-1, keepdims=True)
            dx_ref[...] = r_ref[...] + inv * (dxh - xhat * proj)

    return pl.pallas_call(
        body, name=name, grid=(T // tm, nk),
        in_specs=[pl.BlockSpec((tm, tk), lambda m, k: (m, k)),
                  pl.BlockSpec((1, D_MODEL, tk), lambda m, k: (k // per_shard, 0, k % per_shard)),
                  pl.BlockSpec((tm, D_MODEL), lambda m, k: (m, 0)),
                  pl.BlockSpec((1, D_MODEL), lambda m, k: (0, 0)),
                  pl.BlockSpec((tm, D_MODEL), lambda m, k: (m, 0))],
        out_specs=[pl.BlockSpec((tm, D_MODEL), lambda m, k: (m, 0)),
                   pl.BlockSpec((1, D_MODEL), lambda m, k: (0, 0))],
        out_shape=[jax.ShapeDtypeStruct((T, D_MODEL), F32),
                   jax.ShapeDtypeStruct((1, D_MODEL), F32)],
        scratch_shapes=[pltpu.VMEM((tm, D_MODEL), F32)],
        compiler_params=_params(2),
    )(dproj, w4, h, g_row, dres)


def _window_of(g):
    return jnp.left_shift(2, g)


def _select_stage(g, stages):
    res = stages[0]
    for i in range(1, len(stages)):
        res = jnp.where(g >= i, stages[i], res)
    return res


def _pool_fwd(proj0, wg4, scale_row, S, name, tm=256):
    T = proj0.shape[0]
    blocks_per_seq = S // tm
    hb = tm // HALO

    def body(x_ref, halo_ref, z_ref, w_ref, s_ref, y_ref, p_ref, mix_ref):
        m, g = pl.program_id(0), pl.program_id(1)
        first = (m % blocks_per_seq) == 0
        halo = jnp.where(first, 0.0, halo_ref[...].astype(F32))
        x = x_ref[...].astype(F32)
        ext = jnp.concatenate([halo, x], axis=0)
        stages = []
        cur = ext
        for sh in (1, 2, 4, 8):
            cur = cur + pltpu.roll(cur, sh, 0)
            stages.append(cur[HALO:, :])
        win_sum = _select_stage(g, stages)
        pos = (m % blocks_per_seq) * tm + lax.broadcasted_iota(jnp.int32, (tm, 1), 0)
        count = jnp.minimum(pos + 1, _window_of(g)).astype(F32)
        pooled = win_sum / count - x
        pooled_b = pooled.astype(BF16)
        mixed = _dot(pooled_b, w_ref[...].reshape(GROUP_DIM, GROUP_DIM))
        z = z_ref[...].astype(F32)
        y_ref[...] = (mixed * s_ref[...] * (z * _sigmoid(z))).astype(BF16)
        p_ref[...] = pooled_b
        mix_ref[...] = mixed.astype(BF16)

    blk = lambda m, g: (m, g)
    return pl.pallas_call(
        body, name=name, grid=(T // tm, N_GROUPS),
        in_specs=[pl.BlockSpec((tm, GROUP_DIM), blk),
                  pl.BlockSpec((HALO, GROUP_DIM), lambda m, g: (jnp.maximum(m * hb - 1, 0), g)),
                  pl.BlockSpec((tm, GROUP_DIM), lambda m, g: (m, N_GROUPS + g)),
                  pl.BlockSpec((N_CHIPS, GROUP_DIM // N_CHIPS, GROUP_DIM), lambda m, g: (0, g, 0)),
                  pl.BlockSpec((1, GROUP_DIM), lambda m, g: (0, g))],
        out_specs=[pl.BlockSpec((tm, GROUP_DIM), blk)] * 3,
        out_shape=[jax.ShapeDtypeStruct((T, D_INNER), BF16)] * 3,
        compiler_params=_params(2),
    )(proj0, proj0, proj0, wg4, scale_row)


def _pool_gate_bwd(dh, w_out, proj0, mixed, scale_row, name, tm=1024, tn=512):
    T = dh.shape[0]
    gate_b0 = D_INNER // tn

    def body(dh_ref, w_ref, z_ref, mix_ref, s_ref, dm_ref, dz_ref, ds_ref):
        @pl.when(pl.program_id(1) == 0)
        def _():
            ds_ref[...] = jnp.zeros_like(ds_ref)

        dy = _dot_nt(dh_ref[...].astype(BF16), w_ref[...])
        z = z_ref[...].astype(F32)
        sig = _sigmoid(z)
        silu = z * sig
        mixed = mix_ref[...].astype(F32)
        s = s_ref[...]
        dm_ref[...] = (dy * s * silu).astype(BF16)
        dz_ref[...] = (dy * mixed * s * (sig * (1.0 + z * (1.0 - sig)))).astype(BF16)
        ds_ref[...] += jnp.sum(dy * mixed * silu, axis=0, keepdims=True)

    return pl.pallas_call(
        body, name=name, grid=(D_INNER // tn, T // tm),
        in_specs=[pl.BlockSpec((tm, D_MODEL), lambda n, m: (m, 0)),
                  pl.BlockSpec((tn, D_MODEL), lambda n, m: (n, 0)),
                  pl.BlockSpec((tm, tn), lambda n, m: (m, gate_b0 + n)),
                  pl.BlockSpec((tm, tn), lambda n, m: (m, n)),
                  pl.BlockSpec((1, tn), lambda n, m: (0, n))],
        out_specs=[pl.BlockSpec((tm, tn), lambda n, m: (m, n)),
                   pl.BlockSpec((tm, tn), lambda n, m: (m, gate_b0 + n)),
                   pl.BlockSpec((1, tn), lambda n, m: (0, n))],
        out_shape=[jax.ShapeDtypeStruct((T, D_INNER), BF16),
                   jax.ShapeDtypeStruct((T, 2 * D_INNER), BF16),
                   jax.ShapeDtypeStruct((1, D_INNER), F32)],
        compiler_params=_params(2),
    )(dh, w_out, proj0, mixed, scale_row)


def _pool_bwd(dmixed, wg4, dproj0, S, name, tm=256):
    T = dmixed.shape[0]
    blocks_per_seq = S // tm
    hb = tm // HALO
    n_halo_blocks = T // HALO

    def body(dm_ref, halo_ref, w_ref, _, o_ref):
        m, g = pl.program_id(0), pl.program_id(1)
        ext = jnp.concatenate([dm_ref[...], halo_ref[...]], axis=0)
        dp = _dot_nt(ext, w_ref[...].reshape(GROUP_DIM, GROUP_DIM))
        pos = (m % blocks_per_seq) * tm + lax.broadcasted_iota(jnp.int32, (tm + HALO, 1), 0)
        count = jnp.minimum(pos + 1, _window_of(g)).astype(F32)
        c = jnp.where(pos < S, dp / count, 0.0)
        n = tm + HALO
        stages = []
        cur = c
        for sh in (1, 2, 4, 8):
            cur = cur + pltpu.roll(cur, n - sh, 0)
            stages.append(cur[:tm, :])
        o_ref[...] = (_select_stage(g, stages) - dp[:tm, :]).astype(BF16)

    blk = lambda m, g: (m, g)
    return pl.pallas_call(
        body, name=name, grid=(T // tm, N_GROUPS),
        in_specs=[pl.BlockSpec((tm, GROUP_DIM), blk),
                  pl.BlockSpec((HALO, GROUP_DIM),
                               lambda m, g: (jnp.minimum((m + 1) * hb, n_halo_blocks - 1), g)),
                  pl.BlockSpec((N_CHIPS, GROUP_DIM // N_CHIPS, GROUP_DIM), lambda m, g: (0, g, 0)),
                  HBM_SPEC],
        out_specs=pl.BlockSpec((tm, GROUP_DIM), blk),
        out_shape=jax.ShapeDtypeStruct(dproj0.shape, dproj0.dtype),
        input_output_aliases={3: 0},
        compiler_params=_params(2),
    )(dmixed, dmixed, wg4, dproj0)


TQ = 256


def _split_dot(x, m):
    hi = x.astype(BF16)
    lo = (x - hi.astype(F32)).astype(BF16)
    return _dot(hi, m) + _dot(lo, m)


NEG_LOG2E = -1.4426950408889634


def _log_terms(z):
    soft = jnp.log(1.0 + jnp.exp2(jnp.abs(z) * NEG_LOG2E))
    log_beta = jnp.minimum(z, 0.0) - soft
    return log_beta, log_beta - z


HEADS = 4
QK_W = HEADS * 64
V_W = HEADS * HEAD_V
N_HEAD_GROUPS = 16 // HEADS


def _masked_heads(x):
    lane = lax.broadcasted_iota(jnp.int32, (1, HEAD_PAIR_QK), 1)
    out = []
    for hh in range(HEADS):
        slab = x[:, (hh // 2) * HEAD_PAIR_QK:(hh // 2 + 1) * HEAD_PAIR_QK]
        out.append(jnp.where((lane // 64) == hh % 2, slab, jnp.zeros_like(slab)))
    return out


def _attn_fwd(qkv, z1, S, name):
    T = qkv.shape[0]
    nq = S // TQ
    k_b0 = QK_WIDTH // QK_W
    v_b0 = 2 * QK_WIDTH // V_W
    hs = range(HEADS)

    def body(q_ref, k_ref, v_ref, z_ref, o_ref, y_ref, lt_ref):
        qi = pl.program_id(2)
        row = lax.broadcasted_iota(jnp.int32, (TQ, TQ), 0)
        col = lax.broadcasted_iota(jnp.int32, (TQ, TQ), 1)
        causal = col < row
        later_in_block = (row > col).astype(BF16)
        qms = [qm * 0.125 for qm in _masked_heads(q_ref[...])]

        def step(j, carry, diagonal):
            koff = pl.multiple_of(j * TQ, TQ)
            kbs = [k_ref[pl.ds(koff, TQ), p * HEAD_PAIR_QK:(p + 1) * HEAD_PAIR_QK]
                   for p in range(HEADS // 2)]
            run, acc = [carry[2 * hh] for hh in hs], [carry[2 * hh + 1] for hh in hs]
            z = [_dot_nt(qms[hh], kbs[hh // 2]) for hh in hs]
            terms = [_log_terms(z[hh]) for hh in hs]
            log_om = [jnp.where(causal, t[1], 0.0) if diagonal else t[1] for t in terms]
            later = [_split_dot(log_om[hh], later_in_block) for hh in hs]
            a = [jnp.exp(terms[hh][0] + (run[hh] + later[hh])) for hh in hs]
            if diagonal:
                a = [jnp.where(causal, a[hh], 0.0) for hh in hs]
            out = []
            for hh in hs:
                vb = v_ref[pl.ds(koff, TQ), hh * HEAD_V:(hh + 1) * HEAD_V]
                out += [run[hh] + jnp.sum(log_om[hh], axis=1, keepdims=True),
                        acc[hh] + _dot(a[hh].astype(BF16), vb)]
            return tuple(out)

        zero = (jnp.zeros((TQ, 1), F32), jnp.zeros((TQ, HEAD_V), F32))
        carry = step(qi, zero * HEADS, True)
        carry = lax.fori_loop(0, qi, lambda i, c: step(qi - 1 - i, c, False), carry)
        for hh in hs:
            sl = slice(hh * HEAD_V, (hh + 1) * HEAD_V)
            acc = carry[2 * hh + 1]
            z = z_ref[:, sl].astype(F32)
            o_ref[:, sl] = acc.astype(BF16)
            y_ref[:, sl] = (acc * (z * _sigmoid(z))).astype(BF16)
            lt_ref[:, hh:hh + 1] = carry[2 * hh]

    qblk = lambda b, p, i: (b * nq + i, p)
    return pl.pallas_call(
        body, name=name, grid=(T // S, N_HEAD_GROUPS, nq),
        in_specs=[pl.BlockSpec((TQ, QK_W), qblk),
                  pl.BlockSpec((S, QK_W), lambda b, p, i: (b, k_b0 + p)),
                  pl.BlockSpec((S, V_W), lambda b, p, i: (b, v_b0 + p)),
                  pl.BlockSpec((TQ, V_W), qblk)],
        out_specs=[pl.BlockSpec((TQ, V_W), qblk),
                   pl.BlockSpec((TQ, V_W), qblk),
                   pl.BlockSpec((None, TQ, HEADS), lambda b, p, i: (p, b * nq + i, 0))],
        out_shape=[jax.ShapeDtypeStruct((T, D_INNER), BF16),
                   jax.ShapeDtypeStruct((T, D_INNER), BF16),
                   jax.ShapeDtypeStruct((N_HEAD_GROUPS, T, HEADS), F32)],
        compiler_params=_params(3),
    )(qkv, qkv, qkv, z1)


def _attn_gate_bwd(dh, w_out, z1, o, name, tm=1024, tn=512):
    T = dh.shape[0]
    gate_b0 = (2 * QK_WIDTH + D_INNER) // tn

    def body(dh_ref, w_ref, z_ref, o_ref, do_ref, dz_ref, dh_s):
        @pl.when(pl.program_id(1) == 0)
        def _():
            dh_s[...] = dh_ref[...].astype(BF16)

        dy = _dot_nt(dh_s[...], w_ref[...])
        z = z_ref[...].astype(F32)
        sig = _sigmoid(z)
        do_ref[...] = (dy * (z * sig)).astype(BF16)
        dz_ref[...] = (dy * o_ref[...].astype(F32) * (sig * (1.0 + z * (1.0 - sig)))).astype(BF16)

    return pl.pallas_call(
        body, name=name, grid=(T // tm, D_INNER // tn),
        in_specs=[pl.BlockSpec((tm, D_MODEL), lambda m, n: (m, 0)),
                  pl.BlockSpec((tn, D_MODEL), lambda m, n: (n, 0)),
                  pl.BlockSpec((tm, tn), lambda m, n: (m, n)),
                  pl.BlockSpec((tm, tn), lambda m, n: (m, n))],
        out_specs=[pl.BlockSpec((tm, tn), lambda m, n: (m, n)),
                   pl.BlockSpec((tm, tn), lambda m, n: (m, gate_b0 + n))],
        out_shape=[jax.ShapeDtypeStruct((T, D_INNER), BF16),
                   jax.ShapeDtypeStruct((T, 2 * QK_WIDTH + 2 * D_INNER), BF16)],
        scratch_shapes=[pltpu.VMEM((tm, D_MODEL), BF16)],
        compiler_params=_params(2),
    )(dh, w_out, z1, o)


def _attn_bwd(qkv, do, ltot, dproj1, S, name):
    T = qkv.shape[0]
    nq = S // TQ
    k_b0 = QK_WIDTH // QK_W
    v_b0 = 2 * QK_WIDTH // V_W
    hs = range(HEADS)
    pairs = range(HEADS // 2)

    def body(q_ref, k_ref, v_ref, do_ref, lt_ref, _, out_ref, dq_s, dk_s, dv_s, dkb_s, dvb_s, sems):
        b, p = pl.program_id(0), pl.program_id(1)
        row = lax.broadcasted_iota(jnp.int32, (TQ, TQ), 0)
        col = lax.broadcasted_iota(jnp.int32, (TQ, TQ), 1)
        causal = col < row
        upto = (row <= col).astype(BF16)
        before = (row < col).astype(BF16)
        dk_s[...] = jnp.zeros_like(dk_s)
        dv_s[...] = jnp.zeros_like(dv_s)

        def q_block(qi, _):
            qoff = pl.multiple_of(qi * TQ, TQ)
            qms = [qm * 0.125 for qm in _masked_heads(q_ref[pl.ds(qoff, TQ), :])]
            vsl = [slice(hh * HEAD_V, (hh + 1) * HEAD_V) for hh in hs]
            psl = [slice(pp * HEAD_PAIR_QK, (pp + 1) * HEAD_PAIR_QK) for pp in pairs]
            do_h = [do_ref[pl.ds(qoff, TQ), sl] for sl in vsl]
            total = [lt_ref[pl.ds(qoff, TQ), hh:hh + 1] for hh in hs]

            def k_block(j, carry, diagonal):
                koff = pl.multiple_of(j * TQ, TQ)
                kms = _masked_heads(k_ref[pl.ds(koff, TQ), :])
                g_before = [carry[2 * hh] for hh in hs]
                lom_before = [carry[2 * hh + 1] for hh in hs]
                z = [_dot_nt(qms[hh], kms[hh]) for hh in hs]
                da = [_dot_nt(do_h[hh], v_ref[pl.ds(koff, TQ), vsl[hh]]) for hh in hs]
                terms = [_log_terms(z[hh]) for hh in hs]
                log_om = [jnp.where(causal, t[1], 0.0) if diagonal else t[1] for t in terms]
                prefix = [_split_dot(log_om[hh], upto) for hh in hs]
                a = [jnp.exp(terms[hh][0] + ((total[hh] - lom_before[hh]) - prefix[hh])) for hh in hs]
                if diagonal:
                    a = [jnp.where(causal, a[hh], 0.0) for hh in hs]
                g = [a[hh] * da[hh] for hh in hs]
                g_prefix = [_dot(g[hh].astype(BF16), before) for hh in hs]
                out, dzs = [], []
                for hh in hs:
                    beta = jnp.exp(terms[hh][0])
                    g_excl = (g_before[hh] + g_prefix[hh]) * beta
                    if diagonal:
                        g_excl = jnp.where(causal, g_excl, 0.0)
                    dzs.append((g[hh] * (1.0 - beta) - g_excl).astype(BF16))
                    out += [g_before[hh] + jnp.sum(g[hh], axis=1, keepdims=True),
                            lom_before[hh] + jnp.sum(log_om[hh], axis=1, keepdims=True)]
                for hh in hs:
                    dv_s[pl.ds(koff, TQ), vsl[hh]] += _dot_tn(a[hh].astype(BF16), do_h[hh])
                dq = []
                for pp in pairs:
                    pair = slice(2 * pp, 2 * pp + 2)
                    dq.append(carry[2 * HEADS + pp] + _dot(jnp.concatenate(dzs[pair], axis=1),
                                                           jnp.concatenate(kms[pair], axis=0)))
                    dk_s[pl.ds(koff, TQ), psl[pp]] += _dot_tn(jnp.concatenate(dzs[pair], axis=0),
                                                              jnp.concatenate(qms[pair], axis=0))
                return tuple(out) + tuple(dq)

            zero = jnp.zeros((TQ, 1), F32)
            carry = (zero,) * (2 * HEADS) + (jnp.zeros((TQ, HEAD_PAIR_QK), F32),) * (HEADS // 2)
            carry = lax.fori_loop(0, qi, lambda j, c: k_block(j, c, False), carry)
            carry = k_block(qi, carry, True)
            for pp in pairs:
                dq_s[pl.ds(qoff, TQ), psl[pp]] = (carry[2 * HEADS + pp] * 0.125).astype(BF16)
            return 0

        lax.fori_loop(0, nq, q_block, 0)
        dkb_s[...] = dk_s[...].astype(BF16)
        dvb_s[...] = dv_s[...].astype(BF16)
        rows = pl.ds(pl.multiple_of(b * S, TQ), S)
        copies = [
            pltpu.make_async_copy(
                dq_s, out_ref.at[rows, pl.ds(pl.multiple_of(p * QK_W, 128), QK_W)], sems.at[0]),
            pltpu.make_async_copy(
                dkb_s, out_ref.at[rows, pl.ds(pl.multiple_of(QK_WIDTH + p * QK_W, 128), QK_W)],
                sems.at[1]),
            pltpu.make_async_copy(
                dvb_s, out_ref.at[rows, pl.ds(pl.multiple_of(2 * QK_WIDTH + p * V_W, 128), V_W)],
                sems.at[2]),
        ]
        for cp in copies:
            cp.start()
        for cp in copies:
            cp.wait()

    return pl.pallas_call(
        body, name=name, grid=(T // S, N_HEAD_GROUPS),
        in_specs=[pl.BlockSpec((S, QK_W), lambda b, p: (b, p)),
                  pl.BlockSpec((S, QK_W), lambda b, p: (b, k_b0 + p)),
                  pl.BlockSpec((S, V_W), lambda b, p: (b, v_b0 + p)),
                  pl.BlockSpec((S, V_W), lambda b, p: (b, p)),
                  pl.BlockSpec((None, S, HEADS), lambda b, p: (p, b, 0)),
                  HBM_SPEC],
        out_specs=HBM_SPEC,
        out_shape=jax.ShapeDtypeStruct(dproj1.shape, dproj1.dtype),
        input_output_aliases={5: 0},
        scratch_shapes=[pltpu.VMEM((S, QK_W), BF16),
                        pltpu.VMEM((S, QK_W), F32),
                        pltpu.VMEM((S, V_W), F32),
                        pltpu.VMEM((S, QK_W), BF16),
                        pltpu.VMEM((S, V_W), BF16),
                        pltpu.SemaphoreType.DMA((3,))],
        compiler_params=_params(2),
    )(qkv, qkv, qkv, do, ltot, dproj1)


def _loss_head(h, g_row, target, name, tm=512):
    T = h.shape[0]

    def body(h_ref, g_ref, t_ref, dh_ref, dg_ref, loss_ref):
        @pl.when(pl.program_id(0) == 0)
        def _():
            dg_ref[...] = jnp.zeros_like(dg_ref)
            loss_ref[...] = jnp.zeros_like(loss_ref)

        x = h_ref[...]
        inv = lax.rsqrt(jnp.mean(x * x, axis=-1, keepdims=True) + RMS_EPS)
        xhat = x * inv
        gain = g_ref[...]
        err = xhat * gain - t_ref[...]
        per_token = jnp.mean(err * err, axis=-1, keepdims=True)
        loss_ref[...] += 0.5 * jnp.sum(per_token, axis=0, keepdims=True)
        dy = err * (1.0 / D_MODEL)
        dg_ref[...] += jnp.sum(dy * xhat, axis=0, keepdims=True)
        dxh = dy * gain
        proj = jnp.mean(dxh * xhat, axis=-1, keepdims=True)
        dh_ref[...] = inv * (dxh - xhat * proj)

    return pl.pallas_call(
        body, name=name, grid=(T // tm,),
        in_specs=[pl.BlockSpec((tm, D_MODEL), lambda m: (m, 0)),
                  pl.BlockSpec((1, D_MODEL), lambda m: (0, 0)),
                  pl.BlockSpec((tm, D_MODEL), lambda m: (m, 0))],
        out_specs=[pl.BlockSpec((tm, D_MODEL), lambda m: (m, 0)),
                   pl.BlockSpec((1, D_MODEL), lambda m: (0, 0)),
                   pl.BlockSpec((1, 128), lambda m: (0, 0))],
        out_shape=[jax.ShapeDtypeStruct((T, D_MODEL), F32),
                   jax.ShapeDtypeStruct((1, D_MODEL), F32),
                   jax.ShapeDtypeStruct((1, 128), F32)],
        compiler_params=_params(1),
    )(h, g_row, target)


def _place():
    return lax.axis_index("x"), lax.axis_index("y"), lax.axis_index("c")


def _other_chips(x, y):
    return [(1 - x, y), (x, 1 - y), (1 - x, 1 - y)]


def _half(ref, c):
    hr = ref.shape[-2] // 2
    return pl.ds(pl.multiple_of(c * hr, 8), hr)


def _cast_to_slot(shard, chip, name, tr=256):
    R, C = shard.shape

    def body(chip_ref, w_ref, o_ref):
        o_ref[0] = w_ref[...].astype(BF16)

    return pl.pallas_call(
        body, name=name,
        grid_spec=pltpu.PrefetchScalarGridSpec(
            num_scalar_prefetch=1, grid=(R // tr,),
            in_specs=[pl.BlockSpec((tr, C), lambda i, chip_ref: (i, 0))],
            out_specs=pl.BlockSpec((1, tr, C), lambda i, chip_ref: (chip_ref[0], i, 0))),
        out_shape=jax.ShapeDtypeStruct((N_CHIPS, R, C), BF16),
        compiler_params=_params(1),
    )(chip, shard)


def _allgather_weights(slots, name):
    n = len(slots)

    def body(*refs):
        outs = refs[n:2 * n]
        send_sems, recv_sems, fwd_send, fwd_recv = refs[2 * n:]
        x, y, c = _place()
        chips = _other_chips(x, y)

        def landing(a, chip, half_of):
            return outs[a].at[2 * chip[0] + chip[1], _half(outs[a], half_of)]

        def ici(a, k, chip_from, to):
            return pltpu.make_async_remote_copy(
                src_ref=landing(a, chip_from, c), dst_ref=landing(a, chip_from, c),
                send_sem=send_sems.at[a, k], recv_sem=recv_sems.at[a, k],
                device_id=to, device_id_type=MESH)

        def d2d(a, k, chip_from, half_of):
            return pltpu.make_async_remote_copy(
                src_ref=landing(a, chip_from, half_of), dst_ref=landing(a, chip_from, half_of),
                send_sem=fwd_send.at[a, k], recv_sem=fwd_recv.at[a, k],
                device_id=(x, y, 1 - c), device_id_type=MESH)

        sends = [ici(a, k, (x, y), (*chips[k], c)) for a in range(n) for k in range(3)]
        for cp in sends:
            cp.start()
        forwards = []
        for a in range(n):
            for k in range(3):
                ici(a, k, chips[k], (x, y, c)).wait_recv()
                fw = d2d(a, k, chips[k], c)
                fw.start()
                forwards.append(fw)
        for a in range(n):
            for k in range(3):
                d2d(a, k, chips[k], 1 - c).wait_recv()
        for cp in sends + forwards:
            cp.wait_send()

    return pl.pallas_call(
        body, name=name,
        in_specs=[HBM_SPEC] * n, out_specs=[HBM_SPEC] * n,
        out_shape=[jax.ShapeDtypeStruct(s.shape, s.dtype) for s in slots],
        input_output_aliases={a: a for a in range(n)},
        scratch_shapes=[pltpu.SemaphoreType.DMA((n, 3)), pltpu.SemaphoreType.DMA((n, 3)),
                        pltpu.SemaphoreType.DMA((n, 3)), pltpu.SemaphoreType.DMA((n, 3))],
    )(*slots)


def _sibling_exchange(partials, small, name):
    n = len(partials)

    def body(*refs):
        ins, small_ref = refs[:n], refs[n]
        outs, small_all = refs[n + 1:2 * n + 1], refs[2 * n + 1]
        send_sems, recv_sems, s_send, s_recv, loc_sem = refs[2 * n + 2:]
        x, y, c = _place()
        me = 4 * x + 2 * y + c
        local = pltpu.make_async_copy(small_ref, small_all.at[me], loc_sem)
        local.start()
        big = [pltpu.make_async_remote_copy(
            src_ref=ins[a].at[:, _half(ins[a], 1 - c)], dst_ref=outs[a],
            send_sem=send_sems.at[a], recv_sem=recv_sems.at[a],
            device_id=(x, y, 1 - c), device_id_type=MESH) for a in range(n)]
        tiny = []
        for d in range(1, N_DEV):
            px, py, pc = x ^ ((d >> 2) & 1), y ^ ((d >> 1) & 1), c ^ (d & 1)
            tiny.append(pltpu.make_async_remote_copy(
                src_ref=small_ref, dst_ref=small_all.at[me],
                send_sem=s_send.at[d - 1], recv_sem=s_recv.at[d - 1],
                device_id=(px, py, pc), device_id_type=MESH))
        for cp in tiny + big:
            cp.start()
        for d in range(1, N_DEV):
            peer = me ^ d
            pltpu.make_async_remote_copy(
                src_ref=small_ref, dst_ref=small_all.at[peer],
                send_sem=s_send.at[d - 1], recv_sem=s_recv.at[d - 1],
                device_id=(x, y, c), device_id_type=MESH).wait_recv()
        for cp in big:
            cp.wait_recv()
        for cp in tiny + big:
            cp.wait_send()
        local.wait()

    return pl.pallas_call(
        body, name=name,
        in_specs=[HBM_SPEC] * (n + 1), out_specs=[HBM_SPEC] * (n + 1),
        out_shape=[jax.ShapeDtypeStruct((N_CHIPS, p.shape[1] // 2, p.shape[2]), F32)
                   for p in partials] + [jax.ShapeDtypeStruct((N_DEV,) + small.shape, F32)],
        scratch_shapes=[pltpu.SemaphoreType.DMA((n,)), pltpu.SemaphoreType.DMA((n,)),
                        pltpu.SemaphoreType.DMA((N_DEV - 1,)), pltpu.SemaphoreType.DMA((N_DEV - 1,)),
                        pltpu.SemaphoreType.DMA],
    )(*partials, small)


def _chip_sum(partial, from_sibling, c, name, tr=256):
    _, hr, C = from_sibling.shape
    nb = hr // tr

    def body(c_ref, p_ref, s_ref, o_ref):
        o_ref[...] = (p_ref[...] + s_ref[...]).astype(BF16)

    return pl.pallas_call(
        body, name=name,
        grid_spec=pltpu.PrefetchScalarGridSpec(
            num_scalar_prefetch=1, grid=(N_CHIPS, nb),
            in_specs=[pl.BlockSpec((1, tr, C), lambda j, i, c_ref: (j, c_ref[0] * nb + i, 0)),
                      pl.BlockSpec((1, tr, C), lambda j, i, c_ref: (j, i, 0))],
            out_specs=pl.BlockSpec((1, tr, C), lambda j, i, c_ref: (j, i, 0))),
        out_shape=jax.ShapeDtypeStruct(from_sibling.shape, BF16),
        compiler_params=_params(2),
    )(c, partial, from_sibling)


def _send_chip_sums(sums, name):
    n = len(sums)

    def body(*refs):
        ins, outs = refs[:n], refs[n:2 * n]
        send_sems, recv_sems = refs[2 * n:]
        x, y, c = _place()
        me = 2 * x + y
        chips = _other_chips(x, y)
        sends = [pltpu.make_async_remote_copy(
            src_ref=ins[a].at[2 * chips[k][0] + chips[k][1]], dst_ref=outs[a].at[k],
            send_sem=send_sems.at[a, k], recv_sem=recv_sems.at[a, k],
            device_id=(*chips[k], c), device_id_type=MESH) for a in range(n) for k in range(3)]
        for cp in sends:
            cp.start()
        for a in range(n):
            for k in range(3):
                pltpu.make_async_remote_copy(
                    src_ref=ins[a].at[me], dst_ref=outs[a].at[k],
                    send_sem=send_sems.at[a, k], recv_sem=recv_sems.at[a, k],
                    device_id=(x, y, c), device_id_type=MESH).wait_recv()
        for cp in sends:
            cp.wait_send()

    return pl.pallas_call(
        body, name=name,
        in_specs=[HBM_SPEC] * n, out_specs=[HBM_SPEC] * n,
        out_shape=[jax.ShapeDtypeStruct((3,) + s.shape[1:], BF16) for s in sums],
        scratch_shapes=[pltpu.SemaphoreType.DMA((n, 3)), pltpu.SemaphoreType.DMA((n, 3))],
    )(*sums)


def _reduce_half(partial, from_sibling, received, place, name, tr=256):
    _, hr, C = from_sibling.shape
    nb = hr // tr

    def body(p_ref, mine_ref, sib_ref, r_ref, o_ref):
        acc = mine_ref[0] + sib_ref[0]
        for k in range(3):
            acc = acc + r_ref[k].astype(F32)
        o_ref[...] = acc

    return pl.pallas_call(
        body, name=name,
        grid_spec=pltpu.PrefetchScalarGridSpec(
            num_scalar_prefetch=1, grid=(nb,),
            in_specs=[pl.BlockSpec((1, tr, C), lambda i, p: (p[0], p[1] * nb + i, 0)),
                      pl.BlockSpec((1, tr, C), lambda i, p: (p[0], i, 0)),
                      pl.BlockSpec((3, tr, C), lambda i, p: (0, i, 0))],
            out_specs=pl.BlockSpec((tr, C), lambda i, p: (p[1] * nb + i, 0))),
        out_shape=jax.ShapeDtypeStruct((2 * hr, C), F32),
        compiler_params=_params(1),
    )(place, partial, from_sibling, received)


def _join_halves(fulls, name):
    n = len(fulls)

    def body(*refs):
        outs = refs[n:2 * n]
        send_sems, recv_sems = refs[2 * n:]
        x, y, c = _place()

        def copy(a, half_of, to):
            rows = outs[a].at[_half(outs[a], half_of)]
            return pltpu.make_async_remote_copy(
                src_ref=rows, dst_ref=rows, send_sem=send_sems.at[a], recv_sem=recv_sems.at[a],
                device_id=to, device_id_type=MESH)

        sends = [copy(a, c, (x, y, 1 - c)) for a in range(n)]
        for cp in sends:
            cp.start()
        for a in range(n):
            copy(a, 1 - c, (x, y, c)).wait_recv()
        for cp in sends:
            cp.wait_send()

    return pl.pallas_call(
        body, name=name,
        in_specs=[HBM_SPEC] * n, out_specs=[HBM_SPEC] * n,
        out_shape=[jax.ShapeDtypeStruct(f.shape, F32) for f in fulls],
        input_output_aliases={a: a for a in range(n)},
        scratch_shapes=[pltpu.SemaphoreType.DMA((n,)), pltpu.SemaphoreType.DMA((n,))],
    )(*fulls)


def _adamw_math(w, g, m, v):
    m = ADAM_B1 * m + (1.0 - ADAM_B1) * g
    v = ADAM_B2 * v + (1.0 - ADAM_B2) * (g * g)
    m_hat = m / (1.0 - ADAM_B1 ** ADAM_STEP)
    v_hat = v / (1.0 - ADAM_B2 ** ADAM_STEP)
    delta = -ADAM_LR * (m_hat / (jnp.sqrt(v_hat) + ADAM_EPS) + ADAM_WD * w)
    return delta, m, v


def _adamw(w, g, m, v, name, tr=256):
    R, C = w.shape
    tr = min(tr, R)

    def body(w_ref, g_ref, m_ref, v_ref, d_out, m_out, v_out):
        d_out[...], m_out[...], v_out[...] = _adamw_math(w_ref[...], g_ref[...], m_ref[...], v_ref[...])

    spec = pl.BlockSpec((tr, C), lambda i: (i, 0))
    return pl.pallas_call(
        body, name=name, grid=(R // tr,),
        in_specs=[spec] * 4, out_specs=[spec] * 3,
        out_shape=[jax.ShapeDtypeStruct((R, C), F32)] * 3,
        compiler_params=_params(1),
    )(w, g, m, v)


def _adamw_small(small_all, w, m, v, name):
    def body(s_ref, w_ref, m_ref, v_ref, g_out, d_out, m_out, v_out):
        g = s_ref[0]
        for d in range(1, N_DEV):
            g = g + s_ref[d]
        g_out[...] = g
        d_out[...], m_out[...], v_out[...] = _adamw_math(w_ref[...], g, m_ref[...], v_ref[...])

    vm = pl.BlockSpec(memory_space=pltpu.VMEM)
    return pl.pallas_call(
        body, name=name, in_specs=[vm] * 4, out_specs=[vm] * 4,
        out_shape=[jax.ShapeDtypeStruct(w.shape, F32)] * 4,
    )(small_all, w, m, v)


def _pack_small(norm_g, pool_scale, norm_f, extra_row):
    return jnp.concatenate([norm_g.reshape(2, D_MODEL), pool_scale.reshape(2, D_MODEL),
                            norm_f.reshape(1, D_MODEL), extra_row,
                            jnp.zeros((2, D_MODEL), F32)], axis=0)


def kernel(x, norm_g, pool_w_in, pool_w, pool_scale, pool_w_out, sb_w_in, sb_w_out, norm_f, loss_target, m_norm_g, m_pool_w_in, m_pool_w, m_pool_scale, m_pool_w_out, m_sb_w_in, m_sb_w_out, m_norm_f, v_norm_g, v_pool_w_in, v_pool_w, v_pool_scale, v_pool_w_out, v_sb_w_in, v_sb_w_out, v_norm_f):
    nb, S, _ = x.shape
    T = nb * S
    xt = x.reshape(T, D_MODEL)
    target = loss_target.reshape(T, D_MODEL)
    cx, cy, cc = _place()

    def shard2d(w):
        return w.reshape(-1, w.shape[-1])

    names = ("pool_w_in", "pool_w", "pool_w_out", "sb_w_in", "sb_w_out")
    w_shards = [shard2d(w) for w in (pool_w_in, pool_w, pool_w_out, sb_w_in, sb_w_out)]
    m_shards = [shard2d(w) for w in (m_pool_w_in, m_pool_w, m_pool_w_out, m_sb_w_in, m_sb_w_out)]
    v_shards = [shard2d(w) for w in (v_pool_w_in, v_pool_w, v_pool_w_out, v_sb_w_in, v_sb_w_out)]

    chip = (2 * cx + cy).reshape(1).astype(jnp.int32)
    w_pin, w_g, w_pout, w_sin, w_sout = _allgather_weights(
        [_cast_to_slot(w, chip, "cast_" + nm) for w, nm in zip(w_shards, names)], "allgather_weights")
    w_pout = w_pout.reshape(D_INNER, D_MODEL)
    w_sout = w_sout.reshape(D_INNER, D_MODEL)
    g0, g1, gf = norm_g[0:1], norm_g[1:2], norm_f.reshape(1, D_MODEL)

    proj0, u0 = _rms_matmul(xt, g0, w_pin, [(2 * D_INNER, BF16)], "pool_in_proj")
    y0, pooled, mixed = _pool_fwd(proj0, w_g, pool_scale, S, "pool_mix")
    h1 = _matmul_residual(y0, w_pout, xt, "pool_out_proj")
    qkv, z1, u1 = _rms_matmul(h1, g1, w_sin, [(2 * QK_WIDTH + D_INNER, BF16), (D_INNER, BF16)],
                              "sb_in_proj")
    o, y1, ltot = _attn_fwd(qkv, z1, S, "sb_attention")
    h2 = _matmul_residual(y1, w_sout, h1, "sb_out_proj")
    dh2, d_norm_f, loss_row = _loss_head(h2, gf, target, "loss_head")

    shard = lambda i, j, t: (j, 0, 0)
    gw_sout = _matmul_tn(y1, dh2, D_INNER, D_MODEL, (D_INNER, D_MODEL), (1024, 1024),
                         lambda i, j, t: (i, j), "grad_sb_w_out", bm=1024, bn=1024)
    do, dproj1 = _attn_gate_bwd(dh2, w_sout, z1, o, "sb_gate_bwd")
    dproj1 = _attn_bwd(qkv, do, ltot, dproj1, S, "sb_attention_bwd")
    n1 = 2 * QK_WIDTH + 2 * D_INNER
    gw_sin = _matmul_tn(u1, dproj1, D_MODEL, n1, (N_CHIPS, D_MODEL, n1 // 4), (1, D_MODEL, n1 // 4),
                        shard, "grad_sb_w_in", bm=D_MODEL, bn=n1 // 4)
    dh1, d_g1 = _matmul_nt_rms_bwd(dproj1, w_sin, h1, g1, dh2, "sb_in_bwd")
    gw_pout = _matmul_tn(y0, dh1, D_INNER, D_MODEL, (D_INNER, D_MODEL), (1024, 1024),
                         lambda i, j, t: (i, j), "grad_pool_w_out", bm=1024, bn=1024)
    dmixed, dproj0, d_scale = _pool_gate_bwd(dh1, w_pout, proj0, mixed, pool_scale, "pool_gate_bwd")
    gw_g = _matmul_tn(pooled, dmixed, D_INNER, D_INNER, (N_CHIPS, GROUP_DIM, GROUP_DIM),
                      (N_CHIPS, GROUP_DIM // N_CHIPS, GROUP_DIM), lambda i, j, t: (0, i, 0),
                      "grad_pool_w", bm=GROUP_DIM, bn=GROUP_DIM, diagonal_blocks=True)
    dproj0 = _pool_bwd(dmixed, w_g, dproj0, S, "pool_bwd")
    n0 = 2 * D_INNER
    gw_pin = _matmul_tn(u0, dproj0, D_MODEL, n0, (N_CHIPS, D_MODEL, n0 // 4), (1, D_MODEL, n0 // 4),
                        shard, "grad_pool_w_in", bm=D_MODEL, bn=n0 // 4)
    dx, d_g0 = _matmul_nt_rms_bwd(dproj0, w_pin, xt, g0, dh1, "pool_in_bwd")

    partials = [gw_pin, gw_g, gw_pout.reshape(N_CHIPS, -1, D_MODEL), gw_sin,
                gw_sout.reshape(N_CHIPS, -1, D_MODEL)]
    small = _pack_small(jnp.concatenate([d_g0, d_g1], axis=0), d_scale, d_norm_f,
                        jnp.broadcast_to(loss_row[:, :1], (1, D_MODEL)))
    *from_sibling, small_all = _sibling_exchange(partials, small, "grad_sibling_exchange")
    c_arr = cc.reshape(1).astype(jnp.int32)
    place = jnp.stack([2 * cx + cy, cc]).astype(jnp.int32)
    sums = [_chip_sum(p, s, c_arr, "grad_chip_sum_" + nm)
            for p, s, nm in zip(partials, from_sibling, names)]
    received = _send_chip_sums(sums, "grad_chip_exchange")
    grads = _join_halves([_reduce_half(p, s, r, place, "grad_reduce_" + nm)
                          for p, s, r, nm in zip(partials, from_sibling, received, names)],
                         "grad_join_halves")

    deltas, new_m, new_v = [], [], []
    for w, g, m, v, nm in zip(w_shards, grads, m_shards, v_shards, names):
        d, mm, vv = _adamw(w, g, m, v, "adamw_" + nm)
        deltas.append(d)
        new_m.append(mm)
        new_v.append(vv)

    zero_row = jnp.zeros((1, D_MODEL), F32)
    g_small, d_small, m_small, v_small = _adamw_small(
        small_all, _pack_small(norm_g, pool_scale, norm_f, zero_row),
        _pack_small(m_norm_g, m_pool_scale, m_norm_f, zero_row),
        _pack_small(v_norm_g, v_pool_scale, v_norm_f, zero_row + 1.0), "adamw_small")
    loss = g_small[5, 0]

    def unpack_small(a):
        return a[0:2], a[2:4].reshape(1, D_INNER), a[4]

    def assemble(big, small3):
        ng, ps, nf = small3
        return [ng, big[0].reshape(pool_w_in.shape), big[1].reshape(pool_w.shape), ps,
                big[2].reshape(pool_w_out.shape), big[3].reshape(sb_w_in.shape),
                big[4].reshape(sb_w_out.shape), nf]

    return (loss, dx.reshape(x.shape),
            *assemble(grads, unpack_small(g_small)),
            *assemble(deltas, unpack_small(d_small)),
            *assemble(new_m, unpack_small(m_small)),
            *assemble(new_v, unpack_small(v_small)))
```

```python
import functools

import jax
import jax.numpy as jnp
from jax import lax
from jax.experimental import pallas as pl
from jax.experimental.pallas import tpu as pltpu

F32 = jnp.float32
BF16 = jnp.bfloat16
MESH = pl.DeviceIdType.MESH

D_MODEL = 1024
D_INNER = 2048
N_GROUPS = 4
GROUP_DIM = 512
HEAD_PAIR_QK = 128
HEAD_V = 128
N_HEAD_PAIRS = 8
QK_WIDTH = 1024
RMS_EPS = 1e-6
HALO = 16
N_CHIPS = 4
N_DEV = 8

ADAM_LR = 0.001
ADAM_B1 = 0.9
ADAM_B2 = 0.999
ADAM_EPS = 1e-08
ADAM_WD = 0.01
ADAM_STEP = 10

VMEM_LIMIT = 56 * 1024 * 1024

HBM_SPEC = pl.BlockSpec(memory_space=pltpu.HBM)


def _params(n_axes):
    return pltpu.CompilerParams(dimension_semantics=("arbitrary",) * n_axes,
                                vmem_limit_bytes=VMEM_LIMIT)


def _dot(a, b):
    return jnp.dot(a, b, preferred_element_type=F32)


def _dot_nt(a, b):
    return lax.dot_general(a, b, (((1,), (1,)), ((), ())), preferred_element_type=F32)


def _dot_tn(a, b):
    return lax.dot_general(a, b, (((0,), (0,)), ((), ())), preferred_element_type=F32)


def _sigmoid(z):
    return 1.0 / (1.0 + jnp.exp(-z))


def _rms_matmul(h, g_row, w4, outs, name, tm=1024, tn=512):
    T = h.shape[0]
    per_shard = w4.shape[2] // tn
    starts = [0]
    for width, _ in outs:
        starts.append(starts[-1] + width // tn)
    n_out = len(outs)

    def body(h_ref, g_ref, w_ref, *rest):
        o_refs, u_out, u_s = rest[:n_out], rest[n_out], rest[n_out + 1]
        n = pl.program_id(1)

        @pl.when(n == 0)
        def _():
            x = h_ref[...]
            inv = lax.rsqrt(jnp.mean(x * x, axis=-1, keepdims=True) + RMS_EPS)
            u = (x * inv * g_ref[...]).astype(BF16)
            u_s[...] = u
            u_out[...] = u

        res = _dot(u_s[...], w_ref[0])
        for k in range(n_out):
            @pl.when((n >= starts[k]) & (n < starts[k + 1]))
            def _():
                o_refs[k][...] = res.astype(o_refs[k].dtype)

    def out_map(k):
        return lambda m, n: (m, jnp.clip(n - starts[k], 0, starts[k + 1] - starts[k] - 1))

    return pl.pallas_call(
        body, name=name, grid=(T // tm, starts[-1]),
        in_specs=[pl.BlockSpec((tm, D_MODEL), lambda m, n: (m, 0)),
                  pl.BlockSpec((1, D_MODEL), lambda m, n: (0, 0)),
                  pl.BlockSpec((1, D_MODEL, tn), lambda m, n: (n // per_shard, 0, n % per_shard))],
        out_specs=[pl.BlockSpec((tm, tn), out_map(k)) for k in range(n_out)]
        + [pl.BlockSpec((tm, D_MODEL), lambda m, n: (m, 0))],
        out_shape=[jax.ShapeDtypeStruct((T, width), dt) for width, dt in outs]
        + [jax.ShapeDtypeStruct((T, D_MODEL), BF16)],
        scratch_shapes=[pltpu.VMEM((tm, D_MODEL), BF16)],
        compiler_params=_params(2),
    )(h, g_row, w4)


def _matmul_residual(a, w, res, name, tm=1024, tn=512):
    T, K = a.shape
    N = w.shape[1]

    def body(a_ref, w_ref, r_ref, o_ref):
        o_ref[...] = r_ref[...] + _dot(a_ref[...], w_ref[...])

    return pl.pallas_call(
        body, name=name, grid=(T // tm, N // tn),
        in_specs=[pl.BlockSpec((tm, K), lambda m, n: (m, 0)),
                  pl.BlockSpec((K, tn), lambda m, n: (0, n)),
                  pl.BlockSpec((tm, tn), lambda m, n: (m, n))],
        out_specs=pl.BlockSpec((tm, tn), lambda m, n: (m, n)),
        out_shape=jax.ShapeDtypeStruct((T, N), F32),
        compiler_params=_params(2),
    )(a, w, res)


def _matmul_tn(a, b, a_cols, b_cols, out_shape, out_block, out_map, name, bm, bn, tk=512,
               diagonal_blocks=False):
    T = a.shape[0]

    def body(a_ref, b_ref, o_ref):
        @pl.when(pl.program_id(2) == 0)
        def _():
            o_ref[...] = jnp.zeros_like(o_ref)

        part = _dot_tn(a_ref[...].astype(BF16), b_ref[...].astype(BF16))
        o_ref[...] += part.reshape(o_ref.shape)

    b_map = (lambda i, j, t: (t, i)) if diagonal_blocks else (lambda i, j, t: (t, j))
    return pl.pallas_call(
        body, name=name, grid=(a_cols // bm, 1 if diagonal_blocks else b_cols // bn, T // tk),
        in_specs=[pl.BlockSpec((tk, bm), lambda i, j, t: (t, i)),
                  pl.BlockSpec((tk, bn), b_map)],
        out_specs=pl.BlockSpec(out_block, out_map),
        out_shape=jax.ShapeDtypeStruct(out_shape, F32),
        compiler_params=_params(3),
    )(a, b)


def _matmul_nt_rms_bwd(dproj, w4, h, g_row, dres, name, tm=1024, tk=512):
    T, cols = dproj.shape
    per_shard = w4.shape[2] // tk
    nk = cols // tk

    def body(dp_ref, w_ref, h_ref, g_ref, r_ref, dx_ref, dg_ref, acc):
        m, k = pl.program_id(0), pl.program_id(1)

        @pl.when(k == 0)
        def _():
            acc[...] = jnp.zeros_like(acc)

        @pl.when((k == 0) & (m == 0))
        def _():
            dg_ref[...] = jnp.zeros_like(dg_ref)

        acc[...] += _dot_nt(dp_ref[...], w_ref[0])

        @pl.when(k == nk - 1)
        def _():
            du = acc[...]
            x = h_ref[...]
            inv = lax.rsqrt(jnp.mean(x * x, axis=-1, keepdims=True) + RMS_EPS)
            xhat = x * inv
            dg_ref[...] += jnp.sum(du * xhat, axis=0, keepdims=True)
            dxh = du * g_ref[...]
            proj = jnp.mean(dxh * xhat, axis=-1, keepdims=True)
            dx_ref[...] = r_ref[...] + inv * (dxh - xhat * proj)

    return pl.pallas_call(
        body, name=name, grid=(T // tm, nk),
        in_specs=[pl.BlockSpec((tm, tk), lambda m, k: (m, k)),
                  pl.BlockSpec((1, D_MODEL, tk), lambda m, k: (k // per_shard, 0, k % per_shard)),
                  pl.BlockSpec((tm, D_MODEL), lambda m, k: (m, 0)),
                  pl.BlockSpec((1, D_MODEL), lambda m, k: (0, 0)),
                  pl.BlockSpec((tm, D_MODEL), lambda m, k: (m, 0))],
        out_specs=[pl.BlockSpec((tm, D_MODEL), lambda m, k: (m, 0)),
                   pl.BlockSpec((1, D_MODEL), lambda m, k: (0, 0))],
        out_shape=[jax.ShapeDtypeStruct((T, D_MODEL), F32),
                   jax.ShapeDtypeStruct((1, D_MODEL), F32)],
        scratch_shapes=[pltpu.VMEM((tm, D_MODEL), F32)],
        compiler_params=_params(2),
    )(dproj, w4, h, g_row, dres)


def _window_of(g):
    return jnp.left_shift(2, g)


def _select_stage(g, stages):
    res = stages[0]
    for i in range(1, len(stages)):
        res = jnp.where(g >= i, stages[i], res)
    return res


def _pool_fwd(proj0, wg4, scale_row, S, name, tm=256):
    T = proj0.shape[0]
    blocks_per_seq = S // tm
    hb = tm // HALO

    def body(x_ref, halo_ref, z_ref, w_ref, s_ref, y_ref, p_ref, mix_ref):
        m, g = pl.program_id(0), pl.program_id(1)
        first = (m % blocks_per_seq) == 0
        halo = jnp.where(first, 0.0, halo_ref[...].astype(F32))
        x = x_ref[...].astype(F32)
        ext = jnp.concatenate([halo, x], axis=0)
        stages = []
        cur = ext
        for sh in (1, 2, 4, 8):
            cur = cur + pltpu.roll(cur, sh, 0)
            stages.append(cur[HALO:, :])
        win_sum = _select_stage(g, stages)
        pos = (m % blocks_per_seq) * tm + lax.broadcasted_iota(jnp.int32, (tm, 1), 0)
        count = jnp.minimum(pos + 1, _window_of(g)).astype(F32)
        pooled = win_sum / count - x
        pooled_b = pooled.astype(BF16)
        mixed = _dot(pooled_b, w_ref[...].reshape(GROUP_DIM, GROUP_DIM))
        z = z_ref[...].astype(F32)
        y_ref[...] = (mixed * s_ref[...] * (z * _sigmoid(z))).astype(BF16)
        p_ref[...] = pooled_b
        mix_ref[...] = mixed.astype(BF16)

    blk = lambda m, g: (m, g)
    return pl.pallas_call(
        body, name=name, grid=(T // tm, N_GROUPS),
        in_specs=[pl.BlockSpec((tm, GROUP_DIM), blk),
                  pl.BlockSpec((HALO, GROUP_DIM), lambda m, g: (jnp.maximum(m * hb - 1, 0), g)),
                  pl.BlockSpec((tm, GROUP_DIM), lambda m, g: (m, N_GROUPS + g)),
                  pl.BlockSpec((N_CHIPS, GROUP_DIM // N_CHIPS, GROUP_DIM), lambda m, g: (0, g, 0)),
                  pl.BlockSpec((1, GROUP_DIM), lambda m, g: (0, g))],
        out_specs=[pl.BlockSpec((tm, GROUP_DIM), blk)] * 3,
        out_shape=[jax.ShapeDtypeStruct((T, D_INNER), BF16)] * 3,
        compiler_params=_params(2),
    )(proj0, proj0, proj0, wg4, scale_row)


def _pool_gate_bwd(dh, w_out, proj0, mixed, scale_row, name, tm=1024, tn=512):
    T = dh.shape[0]
    gate_b0 = D_INNER // tn

    def body(dh_ref, w_ref, z_ref, mix_ref, s_ref, dm_ref, dz_ref, ds_ref):
        @pl.when(pl.program_id(1) == 0)
        def _():
            ds_ref[...] = jnp.zeros_like(ds_ref)

        dy = _dot_nt(dh_ref[...].astype(BF16), w_ref[...])
        z = z_ref[...].astype(F32)
        sig = _sigmoid(z)
        silu = z * sig
        mixed = mix_ref[...].astype(F32)
        s = s_ref[...]
        dm_ref[...] = (dy * s * silu).astype(BF16)
        dz_ref[...] = (dy * mixed * s * (sig * (1.0 + z * (1.0 - sig)))).astype(BF16)
        ds_ref[...] += jnp.sum(dy * mixed * silu, axis=0, keepdims=True)

    return pl.pallas_call(
        body, name=name, grid=(D_INNER // tn, T // tm),
        in_specs=[pl.BlockSpec((tm, D_MODEL), lambda n, m: (m, 0)),
                  pl.BlockSpec((tn, D_MODEL), lambda n, m: (n, 0)),
                  pl.BlockSpec((tm, tn), lambda n, m: (m, gate_b0 + n)),
                  pl.BlockSpec((tm, tn), lambda n, m: (m, n)),
                  pl.BlockSpec((1, tn), lambda n, m: (0, n))],
        out_specs=[pl.BlockSpec((tm, tn), lambda n, m: (m, n)),
                   pl.BlockSpec((tm, tn), lambda n, m: (m, gate_b0 + n)),
                   pl.BlockSpec((1, tn), lambda n, m: (0, n))],
        out_shape=[jax.ShapeDtypeStruct((T, D_INNER), BF16),
                   jax.ShapeDtypeStruct((T, 2 * D_INNER), BF16),
                   jax.ShapeDtypeStruct((1, D_INNER), F32)],
        compiler_params=_params(2),
    )(dh, w_out, proj0, mixed, scale_row)


def _pool_bwd(dmixed, wg4, dproj0, S, name, tm=256):
    T = dmixed.shape[0]
    blocks_per_seq = S // tm
    hb = tm // HALO
    n_halo_blocks = T // HALO

    def body(dm_ref, halo_ref, w_ref, _, o_ref):
        m, g = pl.program_id(0), pl.program_id(1)
        ext = jnp.concatenate([dm_ref[...], halo_ref[...]], axis=0)
        dp = _dot_nt(ext, w_ref[...].reshape(GROUP_DIM, GROUP_DIM))
        pos = (m % blocks_per_seq) * tm + lax.broadcasted_iota(jnp.int32, (tm + HALO, 1), 0)
        count = jnp.minimum(pos + 1, _window_of(g)).astype(F32)
        c = jnp.where(pos < S, dp / count, 0.0)
        n = tm + HALO
        stages = []
        cur = c
        for sh in (1, 2, 4, 8):
            cur = cur + pltpu.roll(cur, n - sh, 0)
            stages.append(cur[:tm, :])
        o_ref[...] = (_select_stage(g, stages) - dp[:tm, :]).astype(BF16)

    blk = lambda m, g: (m, g)
    return pl.pallas_call(
        body, name=name, grid=(T // tm, N_GROUPS),
        in_specs=[pl.BlockSpec((tm, GROUP_DIM), blk),
                  pl.BlockSpec((HALO, GROUP_DIM),
                               lambda m, g: (jnp.minimum((m + 1) * hb, n_halo_blocks - 1), g)),
                  pl.BlockSpec((N_CHIPS, GROUP_DIM // N_CHIPS, GROUP_DIM), lambda m, g: (0, g, 0)),
                  HBM_SPEC],
        out_specs=pl.BlockSpec((tm, GROUP_DIM), blk),
        out_shape=jax.ShapeDtypeStruct(dproj0.shape, dproj0.dtype),
        input_output_aliases={3: 0},
        compiler_params=_params(2),
    )(dmixed, dmixed, wg4, dproj0)


TQ = 256


def _split_dot(x, m):
    hi = x.astype(BF16)
    lo = (x - hi.astype(F32)).astype(BF16)
    return _dot(hi, m) + _dot(lo, m)


NEG_LOG2E = -1.4426950408889634


def _log_terms(z):
    soft = jnp.log(1.0 + jnp.exp2(jnp.abs(z) * NEG_LOG2E))
    log_beta = jnp.minimum(z, 0.0) - soft
    return log_beta, log_beta - z


HEADS = 4
QK_W = HEADS * 64
V_W = HEADS * HEAD_V
N_HEAD_GROUPS = 16 // HEADS


def _masked_heads(x):
    lane = lax.broadcasted_iota(jnp.int32, (1, HEAD_PAIR_QK), 1)
    out = []
    for hh in range(HEADS):
        slab = x[:, (hh // 2) * HEAD_PAIR_QK:(hh // 2 + 1) * HEAD_PAIR_QK]
        out.append(jnp.where((lane // 64) == hh % 2, slab, jnp.zeros_like(slab)))
    return out


def _attn_fwd(qkv, z1, S, name):
    T = qkv.shape[0]
    nq = S // TQ
    k_b0 = QK_WIDTH // QK_W
    v_b0 = 2 * QK_WIDTH // V_W
    hs = range(HEADS)

    def body(q_ref, k_ref, v_ref, z_ref, o_ref, y_ref, lt_ref):
        qi = pl.program_id(2)
        row = lax.broadcasted_iota(jnp.int32, (TQ, TQ), 0)
        col = lax.broadcasted_iota(jnp.int32, (TQ, TQ), 1)
        causal = col < row
        later_in_block = (row > col).astype(BF16)
        qms = [qm * 0.125 for qm in _masked_heads(q_ref[...])]

        def step(j, carry, diagonal):
            koff = pl.multiple_of(j * TQ, TQ)
            kbs = [k_ref[pl.ds(koff, TQ), p * HEAD_PAIR_QK:(p + 1) * HEAD_PAIR_QK]
                   for p in range(HEADS // 2)]
            run, acc = [carry[2 * hh] for hh in hs], [carry[2 * hh + 1] for hh in hs]
            z = [_dot_nt(qms[hh], kbs[hh // 2]) for hh in hs]
            terms = [_log_terms(z[hh]) for hh in hs]
            log_om = [jnp.where(causal, t[1], 0.0) if diagonal else t[1] for t in terms]
            later = [_split_dot(log_om[hh], later_in_block) for hh in hs]
            a = [jnp.exp(terms[hh][0] + (run[hh] + later[hh])) for hh in hs]
            if diagonal:
                a = [jnp.where(causal, a[hh], 0.0) for hh in hs]
            out = []
            for hh in hs:
                vb = v_ref[pl.ds(koff, TQ), hh * HEAD_V:(hh + 1) * HEAD_V]
                out += [run[hh] + jnp.sum(log_om[hh], axis=1, keepdims=True),
                        acc[hh] + _dot(a[hh].astype(BF16), vb)]
            return tuple(out)

        zero = (jnp.zeros((TQ, 1), F32), jnp.zeros((TQ, HEAD_V), F32))
        carry = step(qi, zero * HEADS, True)
        carry = lax.fori_loop(0, qi, lambda i, c: step(qi - 1 - i, c, False), carry)
        for hh in hs:
            sl = slice(hh * HEAD_V, (hh + 1) * HEAD_V)
            acc = carry[2 * hh + 1]
            z = z_ref[:, sl].astype(F32)
            o_ref[:, sl] = acc.astype(BF16)
            y_ref[:, sl] = (acc * (z * _sigmoid(z))).astype(BF16)
            lt_ref[:, hh:hh + 1] = carry[2 * hh]

    qblk = lambda b, p, i: (b * nq + i, p)
    return pl.pallas_call(
        body, name=name, grid=(T // S, N_HEAD_GROUPS, nq),
        in_specs=[pl.BlockSpec((TQ, QK_W), qblk),
                  pl.BlockSpec((S, QK_W), lambda b, p, i: (b, k_b0 + p)),
                  pl.BlockSpec((S, V_W), lambda b, p, i: (b, v_b0 + p)),
                  pl.BlockSpec((TQ, V_W), qblk)],
        out_specs=[pl.BlockSpec((TQ, V_W), qblk),
                   pl.BlockSpec((TQ, V_W), qblk),
                   pl.BlockSpec((None, TQ, HEADS), lambda b, p, i: (p, b * nq + i, 0))],
        out_shape=[jax.ShapeDtypeStruct((T, D_INNER), BF16),
                   jax.ShapeDtypeStruct((T, D_INNER), BF16),
                   jax.ShapeDtypeStruct((N_HEAD_GROUPS, T, HEADS), F32)],
        compiler_params=_params(3),
    )(qkv, qkv, qkv, z1)


def _attn_gate_bwd(dh, w_out, z1, o, name, tm=1024, tn=512):
    T = dh.shape[0]
    gate_b0 = (2 * QK_WIDTH + D_INNER) // tn

    def body(dh_ref, w_ref, z_ref, o_ref, do_ref, dz_ref, dh_s):
        @pl.when(pl.program_id(1) == 0)
        def _():
            dh_s[...] = dh_ref[...].astype(BF16)

        dy = _dot_nt(dh_s[...], w_ref[...])
        z = z_ref[...].astype(F32)
        sig = _sigmoid(z)
        do_ref[...] = (dy * (z * sig)).astype(BF16)
        dz_ref[...] = (dy * o_ref[...].astype(F32) * (sig * (1.0 + z * (1.0 - sig)))).astype(BF16)

    return pl.pallas_call(
        body, name=name, grid=(T // tm, D_INNER // tn),
        in_specs=[pl.BlockSpec((tm, D_MODEL), lambda m, n: (m, 0)),
                  pl.BlockSpec((tn, D_MODEL), lambda m, n: (n, 0)),
                  pl.BlockSpec((tm, tn), lambda m, n: (m, n)),
                  pl.BlockSpec((tm, tn), lambda m, n: (m, n))],
        out_specs=[pl.BlockSpec((tm, tn), lambda m, n: (m, n)),
                   pl.BlockSpec((tm, tn), lambda m, n: (m, gate_b0 + n))],
        out_shape=[jax.ShapeDtypeStruct((T, D_INNER), BF16),
                   jax.ShapeDtypeStruct((T, 2 * QK_WIDTH + 2 * D_INNER), BF16)],
        scratch_shapes=[pltpu.VMEM((tm, D_MODEL), BF16)],
        compiler_params=_params(2),
    )(dh, w_out, z1, o)


def _attn_bwd(qkv, do, ltot, dproj1, S, name):
    T = qkv.shape[0]
    nq = S // TQ
    k_b0 = QK_WIDTH // QK_W
    v_b0 = 2 * QK_WIDTH // V_W
    hs = range(HEADS)
    pairs = range(HEADS // 2)

    def body(q_ref, k_ref, v_ref, do_ref, lt_ref, _, out_ref, dq_s, dk_s, dv_s, dkb_s, dvb_s, sems):
        b, p = pl.program_id(0), pl.program_id(1)
        row = lax.broadcasted_iota(jnp.int32, (TQ, TQ), 0)
        col = lax.broadcasted_iota(jnp.int32, (TQ, TQ), 1)
        causal = col < row
        upto = (row <= col).astype(BF16)
        before = (row < col).astype(BF16)
        dk_s[...] = jnp.zeros_like(dk_s)
        dv_s[...] = jnp.zeros_like(dv_s)

        def q_block(qi, _):
            qoff = pl.multiple_of(qi * TQ, TQ)
            qms = [qm * 0.125 for qm in _masked_heads(q_ref[pl.ds(qoff, TQ), :])]
            vsl = [slice(hh * HEAD_V, (hh + 1) * HEAD_V) for hh in hs]
            psl = [slice(pp * HEAD_PAIR_QK, (pp + 1) * HEAD_PAIR_QK) for pp in pairs]
            do_h = [do_ref[pl.ds(qoff, TQ), sl] for sl in vsl]
            total = [lt_ref[pl.ds(qoff, TQ), hh:hh + 1] for hh in hs]

            def k_block(j, carry, diagonal):
                koff = pl.multiple_of(j * TQ, TQ)
                kms = _masked_heads(k_ref[pl.ds(koff, TQ), :])
                g_before = [carry[2 * hh] for hh in hs]
                lom_before = [carry[2 * hh + 1] for hh in hs]
                z = [_dot_nt(qms[hh], kms[hh]) for hh in hs]
                da = [_dot_nt(do_h[hh], v_ref[pl.ds(koff, TQ), vsl[hh]]) for hh in hs]
                terms = [_log_terms(z[hh]) for hh in hs]
                log_om = [jnp.where(causal, t[1], 0.0) if diagonal else t[1] for t in terms]
                prefix = [_split_dot(log_om[hh], upto) for hh in hs]
                a = [jnp.exp(terms[hh][0] + ((total[hh] - lom_before[hh]) - prefix[hh])) for hh in hs]
                if diagonal:
                    a = [jnp.where(causal, a[hh], 0.0) for hh in hs]
                g = [a[hh] * da[hh] for hh in hs]
                g_prefix = [_dot(g[hh].astype(BF16), before) for hh in hs]
                out, dzs = [], []
                for hh in hs:
                    beta = jnp.exp(terms[hh][0])
                    g_excl = (g_before[hh] + g_prefix[hh]) * beta
                    if diagonal:
                        g_excl = jnp.where(causal, g_excl, 0.0)
                    dzs.append((g[hh] * (1.0 - beta) - g_excl).astype(BF16))
                    out += [g_before[hh] + jnp.sum(g[hh], axis=1, keepdims=True),
                            lom_before[hh] + jnp.sum(log_om[hh], axis=1, keepdims=True)]
                for hh in hs:
                    dv_s[pl.ds(koff, TQ), vsl[hh]] += _dot_tn(a[hh].astype(BF16), do_h[hh])
                dq = []
                for pp in pairs:
                    pair = slice(2 * pp, 2 * pp + 2)
                    dq.append(carry[2 * HEADS + pp] + _dot(jnp.concatenate(dzs[pair], axis=1),
                                                           jnp.concatenate(kms[pair], axis=0)))
                    dk_s[pl.ds(koff, TQ), psl[pp]] += _dot_tn(jnp.concatenate(dzs[pair], axis=0),
                                                              jnp.concatenate(qms[pair], axis=0))
                return tuple(out) + tuple(dq)

            zero = jnp.zeros((TQ, 1), F32)
            carry = (zero,) * (2 * HEADS) + (jnp.zeros((TQ, HEAD_PAIR_QK), F32),) * (HEADS // 2)
            carry = lax.fori_loop(0, qi, lambda j, c: k_block(j, c, False), carry)
            carry = k_block(qi, carry, True)
            for pp in pairs:
                dq_s[pl.ds(qoff, TQ), psl[pp]] = (carry[2 * HEADS + pp] * 0.125).astype(BF16)
            return 0

        lax.fori_loop(0, nq, q_block, 0)
        dkb_s[...] = dk_s[...].astype(BF16)
        dvb_s[...] = dv_s[...].astype(BF16)
        rows = pl.ds(pl.multiple_of(b * S, TQ), S)
        copies = [
            pltpu.make_async_copy(
                dq_s, out_ref.at[rows, pl.ds(pl.multiple_of(p * QK_W, 128), QK_W)], sems.at[0]),
            pltpu.make_async_copy(
                dkb_s, out_ref.at[rows, pl.ds(pl.multiple_of(QK_WIDTH + p * QK_W, 128), QK_W)],
                sems.at[1]),
            pltpu.make_async_copy(
                dvb_s, out_ref.at[rows, pl.ds(pl.multiple_of(2 * QK_WIDTH + p * V_W, 128), V_W)],
                sems.at[2]),
        ]
        for cp in copies:
            cp.start()
        for cp in copies:
            cp.wait()

    return pl.pallas_call(
        body, name=name, grid=(T // S, N_HEAD_GROUPS),
        in_specs=[pl.BlockSpec((S, QK_W), lambda b, p: (b, p)),
                  pl.BlockSpec((S, QK_W), lambda b, p: (b, k_b0 + p)),
                  pl.BlockSpec((S, V_W), lambda b, p: (b, v_b0 + p)),
                  pl.BlockSpec((S, V_W), lambda b, p: (b, p)),
                  pl.BlockSpec((None, S, HEADS), lambda b, p: (p, b, 0)),
                  HBM_SPEC],
        out_specs=HBM_SPEC,
        out_shape=jax.ShapeDtypeStruct(dproj1.shape, dproj1.dtype),
        input_output_aliases={5: 0},
        scratch_shapes=[pltpu.VMEM((S, QK_W), BF16),
                        pltpu.VMEM((S, QK_W), F32),
                        pltpu.VMEM((S, V_W), F32),
                        pltpu.VMEM((S, QK_W), BF16),
                        pltpu.VMEM((S, V_W), BF16),
                        pltpu.SemaphoreType.DMA((3,))],
        compiler_params=_params(2),
    )(qkv, qkv, qkv, do, ltot, dproj1)


def _loss_head(h, g_row, target, name, tm=512):
    T = h.shape[0]

    def body(h_ref, g_ref, t_ref, dh_ref, dg_ref, loss_ref):
        @pl.when(pl.program_id(0) == 0)
        def _():
            dg_ref[...] = jnp.zeros_like(dg_ref)
            loss_ref[...] = jnp.zeros_like(loss_ref)

        x = h_ref[...]
        inv = lax.rsqrt(jnp.mean(x * x, axis=-1, keepdims=True) + RMS_EPS)
        xhat = x * inv
        gain = g_ref[...]
        err = xhat * gain - t_ref[...]
        per_token = jnp.mean(err * err, axis=-1, keepdims=True)
        loss_ref[...] += 0.5 * jnp.sum(per_token, axis=0, keepdims=True)
        dy = err * (1.0 / D_MODEL)
        dg_ref[...] += jnp.sum(dy * xhat, axis=0, keepdims=True)
        dxh = dy * gain
        proj = jnp.mean(dxh * xhat, axis=-1, keepdims=True)
        dh_ref[...] = inv * (dxh - xhat * proj)

    return pl.pallas_call(
        body, name=name, grid=(T // tm,),
        in_specs=[pl.BlockSpec((tm, D_MODEL), lambda m: (m, 0)),
                  pl.BlockSpec((1, D_MODEL), lambda m: (0, 0)),
                  pl.BlockSpec((tm, D_MODEL), lambda m: (m, 0))],
        out_specs=[pl.BlockSpec((tm, D_MODEL), lambda m: (m, 0)),
                   pl.BlockSpec((1, D_MODEL), lambda m: (0, 0)),
                   pl.BlockSpec((1, 128), lambda m: (0, 0))],
        out_shape=[jax.ShapeDtypeStruct((T, D_MODEL), F32),
                   jax.ShapeDtypeStruct((1, D_MODEL), F32),
                   jax.ShapeDtypeStruct((1, 128), F32)],
        compiler_params=_params(1),
    )(h, g_row, target)


def _place():
    return lax.axis_index("x"), lax.axis_index("y"), lax.axis_index("c")


def _other_chips(x, y):
    return [(1 - x, y), (x, 1 - y), (1 - x, 1 - y)]


def _half(ref, c):
    hr = ref.shape[-2] // 2
    return pl.ds(pl.multiple_of(c * hr, 8), hr)


def _cast_to_slot(shard, chip, name, tr=256):
    R, C = shard.shape

    def body(chip_ref, w_ref, o_ref):
        o_ref[0] = w_ref[...].astype(BF16)

    return pl.pallas_call(
        body, name=name,
        grid_spec=pltpu.PrefetchScalarGridSpec(
            num_scalar_prefetch=1, grid=(R // tr,),
            in_specs=[pl.BlockSpec((tr, C), lambda i, chip_ref: (i, 0))],
            out_specs=pl.BlockSpec((1, tr, C), lambda i, chip_ref: (chip_ref[0], i, 0))),
        out_shape=jax.ShapeDtypeStruct((N_CHIPS, R, C), BF16),
        compiler_params=_params(1),
    )(chip, shard)


def _weight_plan(bufs):
    x, y, c = _place()
    plan = []
    for buf in bufs:
        mine = buf.at[2 * x + y, _half(buf, c)]
        for ox, oy in _other_chips(x, y):
            plan.append((mine, mine, (ox, oy, c), buf.at[2 * ox + oy, _half(buf, c)]))
    return plan


def _chip_sum_plan(bufs):
    x, y, c = _place()
    n = len(bufs) // 2
    plan = []
    for sums, land in zip(bufs[:n], bufs[n:]):
        for k, (ox, oy) in enumerate(_other_chips(x, y)):
            plan.append((sums.at[2 * ox + oy], land.at[k], (ox, oy, c), land.at[k]))
    return plan


SEM_SPEC = pl.BlockSpec(memory_space=pltpu.SEMAPHORE)
ANY_SPEC = pl.BlockSpec(memory_space=pl.ANY)
DATAFLOW = pltpu.SideEffectType.DATAFLOW_SIDE_EFFECTING


def _in_hbm(a):
    return pltpu.with_memory_space_constraint(a, pltpu.HBM)


def _exchange_start(bufs, after, plan, n_copies, name):
    nb = len(bufs)

    def body(*refs):
        send_sems, recv_sems = refs[nb + 1], refs[nb + 2]
        for i, (src, dst, dev, _) in enumerate(plan(refs[:nb])):
            pltpu.make_async_remote_copy(
                src_ref=src, dst_ref=dst, send_sem=send_sems.at[i], recv_sem=recv_sems.at[i],
                device_id=dev, device_id_type=MESH).start()
        token = refs[-1]
        token[...] = jnp.zeros_like(token)

    res = pl.pallas_call(
        body, name=name,
        in_specs=[HBM_SPEC] * nb + [ANY_SPEC],
        out_specs=[SEM_SPEC, SEM_SPEC] + [HBM_SPEC] * nb + [pl.BlockSpec(memory_space=pltpu.VMEM)],
        out_shape=[pltpu.SemaphoreType.DMA((n_copies,)), pltpu.SemaphoreType.DMA((n_copies,))]
        + [pltpu.HBM(b.shape, b.dtype) for b in bufs] + [jax.ShapeDtypeStruct((8, 128), F32)],
        input_output_aliases={i: 2 + i for i in range(nb)},
        compiler_params=pltpu.CompilerParams(has_side_effects=DATAFLOW),
    )(*[_in_hbm(b) for b in bufs], after)
    return res[0], res[1], list(res[2:2 + nb]), res[-1]


def _exchange_wait(bufs, send_sems, recv_sems, after, plan, name):
    nb = len(bufs)

    def body(*refs):
        sends, recvs = refs[nb], refs[nb + 1]
        for i, (src, dst, dev, landing) in enumerate(plan(refs[:nb])):
            pltpu.make_async_remote_copy(
                src_ref=src, dst_ref=landing, send_sem=sends.at[i], recv_sem=recvs.at[i],
                device_id=dev, device_id_type=MESH).wait()

    res = pl.pallas_call(
        body, name=name,
        in_specs=[HBM_SPEC] * nb + [SEM_SPEC, SEM_SPEC, ANY_SPEC],
        out_specs=[HBM_SPEC] * nb,
        out_shape=[pltpu.HBM(b.shape, b.dtype) for b in bufs],
        input_output_aliases={i: i for i in range(nb)},
        compiler_params=pltpu.CompilerParams(has_side_effects=DATAFLOW),
    )(*bufs, send_sems, recv_sems, after)
    return list(res)


def _allgather_weights(slots, name, landed=False):
    n = len(slots)

    def body(*refs):
        outs = refs[n:2 * n]
        send_sems, recv_sems, fwd_send, fwd_recv = refs[2 * n:]
        x, y, c = _place()
        chips = _other_chips(x, y)

        def landing(a, chip, half_of):
            return outs[a].at[2 * chip[0] + chip[1], _half(outs[a], half_of)]

        def ici(a, k, chip_from, to):
            return pltpu.make_async_remote_copy(
                src_ref=landing(a, chip_from, c), dst_ref=landing(a, chip_from, c),
                send_sem=send_sems.at[a, k], recv_sem=recv_sems.at[a, k],
                device_id=to, device_id_type=MESH)

        def d2d(a, k, chip_from, half_of):
            return pltpu.make_async_remote_copy(
                src_ref=landing(a, chip_from, half_of), dst_ref=landing(a, chip_from, half_of),
                send_sem=fwd_send.at[a, k], recv_sem=fwd_recv.at[a, k],
                device_id=(x, y, 1 - c), device_id_type=MESH)

        sends = []
        if not landed:
            sends = [ici(a, k, (x, y), (*chips[k], c)) for a in range(n) for k in range(3)]
        for cp in sends:
            cp.start()
        forwards = []
        for a in range(n):
            for k in range(3):
                if not landed:
                    ici(a, k, chips[k], (x, y, c)).wait_recv()
                fw = d2d(a, k, chips[k], c)
                fw.start()
                forwards.append(fw)
        for a in range(n):
            for k in range(3):
                d2d(a, k, chips[k], 1 - c).wait_recv()
        for cp in sends + forwards:
            cp.wait_send()

    return pl.pallas_call(
        body, name=name,
        in_specs=[HBM_SPEC] * n, out_specs=[HBM_SPEC] * n,
        out_shape=[jax.ShapeDtypeStruct(s.shape, s.dtype) for s in slots],
        input_output_aliases={a: a for a in range(n)},
        scratch_shapes=[pltpu.SemaphoreType.DMA((n, 3)), pltpu.SemaphoreType.DMA((n, 3)),
                        pltpu.SemaphoreType.DMA((n, 3)), pltpu.SemaphoreType.DMA((n, 3))],
    )(*slots)


def _sibling_exchange(partials, small, name):
    n = len(partials)
    ns = 0 if small is None else 1

    def body(*refs):
        ins, outs = refs[:n], refs[n + ns:2 * n + ns]
        send_sems, recv_sems = refs[2 * (n + ns):2 * (n + ns) + 2]
        x, y, c = _place()
        me = 4 * x + 2 * y + c
        sends = [pltpu.make_async_remote_copy(
            src_ref=ins[a].at[:, _half(ins[a], 1 - c)], dst_ref=outs[a],
            send_sem=send_sems.at[a], recv_sem=recv_sems.at[a],
            device_id=(x, y, 1 - c), device_id_type=MESH) for a in range(n)]
        if ns:
            small_ref, small_all = refs[n], refs[2 * n + 1]
            s_send, s_recv, loc_sem = refs[2 * (n + ns) + 2:]
            local = pltpu.make_async_copy(small_ref, small_all.at[me], loc_sem)
            local.start()
            for d in range(1, N_DEV):
                px, py, pc = x ^ ((d >> 2) & 1), y ^ ((d >> 1) & 1), c ^ (d & 1)
                sends.append(pltpu.make_async_remote_copy(
                    src_ref=small_ref, dst_ref=small_all.at[me],
                    send_sem=s_send.at[d - 1], recv_sem=s_recv.at[d - 1],
                    device_id=(px, py, pc), device_id_type=MESH))
        for cp in sends:
            cp.start()
        if ns:
            for d in range(1, N_DEV):
                pltpu.make_async_remote_copy(
                    src_ref=small_ref, dst_ref=small_all.at[me ^ d],
                    send_sem=s_send.at[d - 1], recv_sem=s_recv.at[d - 1],
                    device_id=(x, y, c), device_id_type=MESH).wait_recv()
        for cp in sends[:n]:
            cp.wait_recv()
        for cp in sends:
            cp.wait_send()
        if ns:
            local.wait()

    out_shape = [jax.ShapeDtypeStruct((N_CHIPS, p.shape[1] // 2, p.shape[2]), F32) for p in partials]
    scratch = [pltpu.SemaphoreType.DMA((n,)), pltpu.SemaphoreType.DMA((n,))]
    if ns:
        out_shape.append(jax.ShapeDtypeStruct((N_DEV,) + small.shape, F32))
        scratch += [pltpu.SemaphoreType.DMA((N_DEV - 1,)), pltpu.SemaphoreType.DMA((N_DEV - 1,)),
                    pltpu.SemaphoreType.DMA]
    return pl.pallas_call(
        body, name=name,
        in_specs=[HBM_SPEC] * (n + ns), out_specs=[HBM_SPEC] * (n + ns),
        out_shape=out_shape, scratch_shapes=scratch,
    )(*partials, *([small] if ns else []))


def _chip_sum(partial, from_sibling, c, name, tr=256):
    _, hr, C = from_sibling.shape
    nb = hr // tr

    def body(c_ref, p_ref, s_ref, o_ref):
        o_ref[...] = (p_ref[...] + s_ref[...]).astype(BF16)

    return pl.pallas_call(
        body, name=name,
        grid_spec=pltpu.PrefetchScalarGridSpec(
            num_scalar_prefetch=1, grid=(N_CHIPS, nb),
            in_specs=[pl.BlockSpec((1, tr, C), lambda j, i, c_ref: (j, c_ref[0] * nb + i, 0)),
                      pl.BlockSpec((1, tr, C), lambda j, i, c_ref: (j, i, 0))],
            out_specs=pl.BlockSpec((1, tr, C), lambda j, i, c_ref: (j, i, 0))),
        out_shape=jax.ShapeDtypeStruct(from_sibling.shape, BF16),
        compiler_params=_params(2),
    )(c, partial, from_sibling)


def _send_chip_sums(sums, name):
    n = len(sums)

    def body(*refs):
        ins, outs = refs[:n], refs[n:2 * n]
        send_sems, recv_sems = refs[2 * n:]
        x, y, c = _place()
        me = 2 * x + y
        chips = _other_chips(x, y)
        sends = [pltpu.make_async_remote_copy(
            src_ref=ins[a].at[2 * chips[k][0] + chips[k][1]], dst_ref=outs[a].at[k],
            send_sem=send_sems.at[a, k], recv_sem=recv_sems.at[a, k],
            device_id=(*chips[k], c), device_id_type=MESH) for a in range(n) for k in range(3)]
        for cp in sends:
            cp.start()
        for a in range(n):
            for k in range(3):
                pltpu.make_async_remote_copy(
                    src_ref=ins[a].at[me], dst_ref=outs[a].at[k],
                    send_sem=send_sems.at[a, k], recv_sem=recv_sems.at[a, k],
                    device_id=(x, y, c), device_id_type=MESH).wait_recv()
        for cp in sends:
            cp.wait_send()

    return pl.pallas_call(
        body, name=name,
        in_specs=[HBM_SPEC] * n, out_specs=[HBM_SPEC] * n,
        out_shape=[jax.ShapeDtypeStruct((3,) + s.shape[1:], BF16) for s in sums],
        scratch_shapes=[pltpu.SemaphoreType.DMA((n, 3)), pltpu.SemaphoreType.DMA((n, 3))],
    )(*sums)


def _reduce_half(partial, from_sibling, received, place, name, tr=256):
    _, hr, C = from_sibling.shape
    nb = hr // tr

    def body(p_ref, mine_ref, sib_ref, r_ref, o_ref):
        acc = mine_ref[0] + sib_ref[0]
        for k in range(3):
            acc = acc + r_ref[k].astype(F32)
        o_ref[...] = acc

    return pl.pallas_call(
        body, name=name,
        grid_spec=pltpu.PrefetchScalarGridSpec(
            num_scalar_prefetch=1, grid=(nb,),
            in_specs=[pl.BlockSpec((1, tr, C), lambda i, p: (p[0], p[1] * nb + i, 0)),
                      pl.BlockSpec((1, tr, C), lambda i, p: (p[0], i, 0)),
                      pl.BlockSpec((3, tr, C), lambda i, p: (0, i, 0))],
            out_specs=pl.BlockSpec((tr, C), lambda i, p: (p[1] * nb + i, 0))),
        out_shape=jax.ShapeDtypeStruct((2 * hr, C), F32),
        compiler_params=_params(1),
    )(place, partial, from_sibling, received)


def _join_halves(fulls, name):
    n = len(fulls)

    def body(*refs):
        outs = refs[n:2 * n]
        send_sems, recv_sems = refs[2 * n:]
        x, y, c = _place()

        def copy(a, half_of, to):
            rows = outs[a].at[_half(outs[a], half_of)]
            return pltpu.make_async_remote_copy(
                src_ref=rows, dst_ref=rows, send_sem=send_sems.at[a], recv_sem=recv_sems.at[a],
                device_id=to, device_id_type=MESH)

        sends = [copy(a, c, (x, y, 1 - c)) for a in range(n)]
        for cp in sends:
            cp.start()
        for a in range(n):
            copy(a, 1 - c, (x, y, c)).wait_recv()
        for cp in sends:
            cp.wait_send()

    return pl.pallas_call(
        body, name=name,
        in_specs=[HBM_SPEC] * n, out_specs=[HBM_SPEC] * n,
        out_shape=[jax.ShapeDtypeStruct(f.shape, F32) for f in fulls],
        input_output_aliases={a: a for a in range(n)},
        scratch_shapes=[pltpu.SemaphoreType.DMA((n,)), pltpu.SemaphoreType.DMA((n,))],
    )(*fulls)


def _adamw_math(w, g, m, v):
    m = ADAM_B1 * m + (1.0 - ADAM_B1) * g
    v = ADAM_B2 * v + (1.0 - ADAM_B2) * (g * g)
    m_hat = m / (1.0 - ADAM_B1 ** ADAM_STEP)
    v_hat = v / (1.0 - ADAM_B2 ** ADAM_STEP)
    delta = -ADAM_LR * (m_hat / (jnp.sqrt(v_hat) + ADAM_EPS) + ADAM_WD * w)
    return delta, m, v


def _adamw(w, g, m, v, name, tr=256):
    R, C = w.shape
    tr = min(tr, R)

    def body(w_ref, g_ref, m_ref, v_ref, d_out, m_out, v_out):
        d_out[...], m_out[...], v_out[...] = _adamw_math(w_ref[...], g_ref[...], m_ref[...], v_ref[...])

    spec = pl.BlockSpec((tr, C), lambda i: (i, 0))
    return pl.pallas_call(
        body, name=name, grid=(R // tr,),
        in_specs=[spec] * 4, out_specs=[spec] * 3,
        out_shape=[jax.ShapeDtypeStruct((R, C), F32)] * 3,
        compiler_params=_params(1),
    )(w, g, m, v)


def _adamw_small(small_all, w, m, v, name):
    def body(s_ref, w_ref, m_ref, v_ref, g_out, d_out, m_out, v_out):
        g = s_ref[0]
        for d in range(1, N_DEV):
            g = g + s_ref[d]
        g_out[...] = g
        d_out[...], m_out[...], v_out[...] = _adamw_math(w_ref[...], g, m_ref[...], v_ref[...])

    vm = pl.BlockSpec(memory_space=pltpu.VMEM)
    return pl.pallas_call(
        body, name=name, in_specs=[vm] * 4, out_specs=[vm] * 4,
        out_shape=[jax.ShapeDtypeStruct(w.shape, F32)] * 4,
    )(small_all, w, m, v)


def _pack_small(norm_g, pool_scale, norm_f, extra_row):
    return jnp.concatenate([norm_g.reshape(2, D_MODEL), pool_scale.reshape(2, D_MODEL),
                            norm_f.reshape(1, D_MODEL), extra_row,
                            jnp.zeros((2, D_MODEL), F32)], axis=0)


def kernel(x, norm_g, pool_w_in, pool_w, pool_scale, pool_w_out, sb_w_in, sb_w_out, norm_f, loss_target, m_norm_g, m_pool_w_in, m_pool_w, m_pool_scale, m_pool_w_out, m_sb_w_in, m_sb_w_out, m_norm_f, v_norm_g, v_pool_w_in, v_pool_w, v_pool_scale, v_pool_w_out, v_sb_w_in, v_sb_w_out, v_norm_f):
    nb, S, _ = x.shape
    T = nb * S
    xt = x.reshape(T, D_MODEL)
    target = loss_target.reshape(T, D_MODEL)
    cx, cy, cc = _place()

    def shard2d(w):
        return w.reshape(-1, w.shape[-1])

    names = ("pool_w_in", "pool_w", "pool_w_out", "sb_w_in", "sb_w_out")
    w_shards = [shard2d(w) for w in (pool_w_in, pool_w, pool_w_out, sb_w_in, sb_w_out)]
    m_shards = [shard2d(w) for w in (m_pool_w_in, m_pool_w, m_pool_w_out, m_sb_w_in, m_sb_w_out)]
    v_shards = [shard2d(w) for w in (v_pool_w_in, v_pool_w, v_pool_w_out, v_sb_w_in, v_sb_w_out)]

    chip = (2 * cx + cy).reshape(1).astype(jnp.int32)
    c_arr = cc.reshape(1).astype(jnp.int32)
    place = jnp.stack([2 * cx + cy, cc]).astype(jnp.int32)
    slots = [_cast_to_slot(w, chip, "cast_" + nm) for w, nm in zip(w_shards, names)]
    g0, g1, gf = norm_g[0:1], norm_g[1:2], norm_f.reshape(1, D_MODEL)

    w_pin, w_g, w_pout = _allgather_weights(slots[:3], "allgather_pool_weights")
    w_pout = w_pout.reshape(D_INNER, D_MODEL)
    sb_send, sb_recv, sb_slots, token = _exchange_start(slots[3:], w_pin, _weight_plan, 6,
                                                        "sb_weights_start")

    proj0, u0 = _rms_matmul(xt, g0 + token[0:1, 0:1], w_pin, [(2 * D_INNER, BF16)], "pool_in_proj")
    y0, pooled, mixed = _pool_fwd(proj0, w_g, pool_scale, S, "pool_mix")
    sb_slots = _exchange_wait(sb_slots, sb_send, sb_recv, y0, _weight_plan, "sb_weights_wait")
    w_sin, w_sout = _allgather_weights(sb_slots, "sb_weights_forward", landed=True)
    w_sout = w_sout.reshape(D_INNER, D_MODEL)
    h1 = _matmul_residual(y0, w_pout, xt, "pool_out_proj")
    qkv, z1, u1 = _rms_matmul(h1, g1, w_sin, [(2 * QK_WIDTH + D_INNER, BF16), (D_INNER, BF16)],
                              "sb_in_proj")
    o, y1, ltot = _attn_fwd(qkv, z1, S, "sb_attention")
    h2 = _matmul_residual(y1, w_sout, h1, "sb_out_proj")
    dh2, d_norm_f, loss_row = _loss_head(h2, gf, target, "loss_head")

    shard = lambda i, j, t: (j, 0, 0)
    gw_sout = _matmul_tn(y1, dh2, D_INNER, D_MODEL, (D_INNER, D_MODEL), (1024, 1024),
                         lambda i, j, t: (i, j), "grad_sb_w_out", bm=1024, bn=1024)
    do, dproj1 = _attn_gate_bwd(dh2, w_sout, z1, o, "sb_gate_bwd")
    dproj1 = _attn_bwd(qkv, do, ltot, dproj1, S, "sb_attention_bwd")
    n1 = 2 * QK_WIDTH + 2 * D_INNER
    gw_sin = _matmul_tn(u1, dproj1, D_MODEL, n1, (N_CHIPS, D_MODEL, n1 // 4), (1, D_MODEL, n1 // 4),
                        shard, "grad_sb_w_in", bm=D_MODEL, bn=n1 // 4)

    sb_partials = [gw_sin, gw_sout.reshape(N_CHIPS, -1, D_MODEL)]
    sb_from_sibling = _sibling_exchange(sb_partials, None, "grad_sibling_exchange_sb")
    sb_sums = [_chip_sum(p, s, c_arr, "grad_chip_sum_" + nm)
               for p, s, nm in zip(sb_partials, sb_from_sibling, names[3:])]
    sb_lands = [lax.empty((3,) + s.shape[1:], BF16) for s in sb_sums]
    gs_send, gs_recv, sb_bufs, gtoken = _exchange_start(sb_sums + sb_lands, c_arr, _chip_sum_plan, 6,
                                                        "sb_grads_start")

    dh1, d_g1 = _matmul_nt_rms_bwd(dproj1, w_sin, h1, g1 + gtoken[0:1, 0:1], dh2, "sb_in_bwd")
    gw_pout = _matmul_tn(y0, dh1, D_INNER, D_MODEL, (D_INNER, D_MODEL), (1024, 1024),
                         lambda i, j, t: (i, j), "grad_pool_w_out", bm=1024, bn=1024)
    dmixed, dproj0, d_scale = _pool_gate_bwd(dh1, w_pout, proj0, mixed, pool_scale, "pool_gate_bwd")
    gw_g = _matmul_tn(pooled, dmixed, D_INNER, D_INNER, (N_CHIPS, GROUP_DIM, GROUP_DIM),
                      (N_CHIPS, GROUP_DIM // N_CHIPS, GROUP_DIM), lambda i, j, t: (0, i, 0),
                      "grad_pool_w", bm=GROUP_DIM, bn=GROUP_DIM, diagonal_blocks=True)
    dproj0 = _pool_bwd(dmixed, w_g, dproj0, S, "pool_bwd")
    n0 = 2 * D_INNER
    gw_pin = _matmul_tn(u0, dproj0, D_MODEL, n0, (N_CHIPS, D_MODEL, n0 // 4), (1, D_MODEL, n0 // 4),
                        shard, "grad_pool_w_in", bm=D_MODEL, bn=n0 // 4)
    dx, d_g0 = _matmul_nt_rms_bwd(dproj0, w_pin, xt, g0, dh1, "pool_in_bwd")

    sb_received = _exchange_wait(sb_bufs, gs_send, gs_recv, dx, _chip_sum_plan, "sb_grads_wait")[2:]
    pool_partials = [gw_pin, gw_g, gw_pout.reshape(N_CHIPS, -1, D_MODEL)]
    small = _pack_small(jnp.concatenate([d_g0, d_g1], axis=0), d_scale, d_norm_f,
                        jnp.broadcast_to(loss_row[:, :1], (1, D_MODEL)))
    *pool_from_sibling, small_all = _sibling_exchange(pool_partials, small, "grad_sibling_exchange_pool")
    pool_sums = [_chip_sum(p, s, c_arr, "grad_chip_sum_" + nm)
                 for p, s, nm in zip(pool_partials, pool_from_sibling, names[:3])]
    pool_received = _send_chip_sums(pool_sums, "grad_chip_exchange_pool")
    grads = _join_halves(
        [_reduce_half(p, s, r, place, "grad_reduce_" + nm)
         for p, s, r, nm in zip(pool_partials + sb_partials,
                                list(pool_from_sibling) + list(sb_from_sibling),
                                list(pool_received) + list(sb_received), names)],
        "grad_join_halves")

    deltas, new_m, new_v = [], [], []
    for w, g, m, v, nm in zip(w_shards, grads, m_shards, v_shards, names):
        d, mm, vv = _adamw(w, g, m, v, "adamw_" + nm)
        deltas.append(d)
        new_m.append(mm)
        new_v.append(vv)

    zero_row = jnp.zeros((1, D_MODEL), F32)
    g_small, d_small, m_small, v_small = _adamw_small(
        small_all, _pack_small(norm_g, pool_scale, norm_f, zero_row),
        _pack_small(m_norm_g, m_pool_scale, m_norm_f, zero_row),
        _pack_small(v_norm_g, v_pool_scale, v_norm_f, zero_row + 1.0), "adamw_small")
    loss = g_small[5, 0]

    def unpack_small(a):
        return a[0:2], a[2:4].reshape(1, D_INNER), a[4]

    def assemble(big, small3):
        ng, ps, nf = small3
        return [ng, big[0].reshape(pool_w_in.shape), big[1].reshape(pool_w.shape), ps,
                big[2].reshape(pool_w_out.shape), big[3].reshape(sb_w_in.shape),
                big[4].reshape(sb_w_out.shape), nf]

    return (loss, dx.reshape(x.shape),
            *assemble(grads, unpack_small(g_small)),
            *assemble(deltas, unpack_small(d_small)),
            *assemble(new_m, unpack_small(m_small)),
            *assemble(new_v, unpack_small(v_small)))
```

```python
import functools

import jax
import jax.numpy as jnp
from jax import lax
from jax.experimental import pallas as pl
from jax.experimental.pallas import tpu as pltpu

F32 = jnp.float32
BF16 = jnp.bfloat16
MESH = pl.DeviceIdType.MESH

D_MODEL = 1024
D_INNER = 2048
N_GROUPS = 4
GROUP_DIM = 512
HEAD_PAIR_QK = 128
HEAD_V = 128
N_HEAD_PAIRS = 8
QK_WIDTH = 1024
RMS_EPS = 1e-6
HALO = 16
N_CHIPS = 4
N_DEV = 8

ADAM_LR = 0.001
ADAM_B1 = 0.9
ADAM_B2 = 0.999
ADAM_EPS = 1e-08
ADAM_WD = 0.01
ADAM_STEP = 10

VMEM_LIMIT = 56 * 1024 * 1024

HBM_SPEC = pl.BlockSpec(memory_space=pltpu.HBM)


def _params(n_axes):
    return pltpu.CompilerParams(dimension_semantics=("arbitrary",) * n_axes,
                                vmem_limit_bytes=VMEM_LIMIT)


def _dot(a, b):
    return jnp.dot(a, b, preferred_element_type=F32)


def _dot_nt(a, b):
    return lax.dot_general(a, b, (((1,), (1,)), ((), ())), preferred_element_type=F32)


def _dot_tn(a, b):
    return lax.dot_general(a, b, (((0,), (0,)), ((), ())), preferred_element_type=F32)


def _sigmoid(z):
    return 1.0 / (1.0 + jnp.exp(-z))


def _rms_matmul(h, g_row, w4, outs, name, tm=1024, tn=512):
    T = h.shape[0]
    per_shard = w4.shape[2] // tn
    starts = [0]
    for width, _ in outs:
        starts.append(starts[-1] + width // tn)
    n_out = len(outs)

    def body(h_ref, g_ref, w_ref, *rest):
        o_refs, u_out, u_s = rest[:n_out], rest[n_out], rest[n_out + 1]
        n = pl.program_id(1)

        @pl.when(n == 0)
        def _():
            x = h_ref[...]
            inv = lax.rsqrt(jnp.mean(x * x, axis=-1, keepdims=True) + RMS_EPS)
            u = (x * inv * g_ref[...]).astype(BF16)
            u_s[...] = u
            u_out[...] = u

        res = _dot(u_s[...], w_ref[0])
        for k in range(n_out):
            @pl.when((n >= starts[k]) & (n < starts[k + 1]))
            def _():
                o_refs[k][...] = res.astype(o_refs[k].dtype)

    def out_map(k):
        return lambda m, n: (m, jnp.clip(n - starts[k], 0, starts[k + 1] - starts[k] - 1))

    return pl.pallas_call(
        body, name=name, grid=(T // tm, starts[-1]),
        in_specs=[pl.BlockSpec((tm, D_MODEL), lambda m, n: (m, 0)),
                  pl.BlockSpec((1, D_MODEL), lambda m, n: (0, 0)),
                  pl.BlockSpec((1, D_MODEL, tn), lambda m, n: (n // per_shard, 0, n % per_shard))],
        out_specs=[pl.BlockSpec((tm, tn), out_map(k)) for k in range(n_out)]
        + [pl.BlockSpec((tm, D_MODEL), lambda m, n: (m, 0))],
        out_shape=[jax.ShapeDtypeStruct((T, width), dt) for width, dt in outs]
        + [jax.ShapeDtypeStruct((T, D_MODEL), BF16)],
        scratch_shapes=[pltpu.VMEM((tm, D_MODEL), BF16)],
        compiler_params=_params(2),
    )(h, g_row, w4)


def _matmul_residual(a, w, res, name, tm=1024, tn=512):
    T, K = a.shape
    N = w.shape[1]

    def body(a_ref, w_ref, r_ref, o_ref):
        o_ref[...] = r_ref[...] + _dot(a_ref[...], w_ref[...])

    return pl.pallas_call(
        body, name=name, grid=(T // tm, N // tn),
        in_specs=[pl.BlockSpec((tm, K), lambda m, n: (m, 0)),
                  pl.BlockSpec((K, tn), lambda m, n: (0, n)),
                  pl.BlockSpec((tm, tn), lambda m, n: (m, n))],
        out_specs=pl.BlockSpec((tm, tn), lambda m, n: (m, n)),
        out_shape=jax.ShapeDtypeStruct((T, N), F32),
        compiler_params=_params(2),
    )(a, w, res)


def _matmul_tn(a, b, a_cols, b_cols, out_shape, out_block, out_map, name, bm, bn, tk=512,
               diagonal_blocks=False):
    T = a.shape[0]

    def body(a_ref, b_ref, o_ref):
        @pl.when(pl.program_id(2) == 0)
        def _():
            o_ref[...] = jnp.zeros_like(o_ref)

        part = _dot_tn(a_ref[...].astype(BF16), b_ref[...].astype(BF16))
        o_ref[...] += part.reshape(o_ref.shape)

    b_map = (lambda i, j, t: (t, i)) if diagonal_blocks else (lambda i, j, t: (t, j))
    return pl.pallas_call(
        body, name=name, grid=(a_cols // bm, 1 if diagonal_blocks else b_cols // bn, T // tk),
        in_specs=[pl.BlockSpec((tk, bm), lambda i, j, t: (t, i)),
                  pl.BlockSpec((tk, bn), b_map)],
        out_specs=pl.BlockSpec(out_block, out_map),
        out_shape=jax.ShapeDtypeStruct(out_shape, F32),
        compiler_params=_params(3),
    )(a, b)


def _matmul_nt_rms_bwd(dproj, w4, h, g_row, dres, name, tm=1024, tk=512):
    T, cols = dproj.shape
    per_shard = w4.shape[2] // tk
    nk = cols // tk

    def body(dp_ref, w_ref, h_ref, g_ref, r_ref, dx_ref, dg_ref, acc):
        m, k = pl.program_id(0), pl.program_id(1)

        @pl.when(k == 0)
        def _():
            acc[...] = jnp.zeros_like(acc)

        @pl.when((k == 0) & (m == 0))
        def _():
            dg_ref[...] = jnp.zeros_like(dg_ref)

        acc[...] += _dot_nt(dp_ref[...], w_ref[0])

        @pl.when(k == nk - 1)
        def _():
            du = acc[...]
            x = h_ref[...]
            inv = lax.rsqrt(jnp.mean(x * x, axis=-1, keepdims=True) + RMS_EPS)
            xhat = x * inv
            dg_ref[...] += jnp.sum(du * xhat, axis=0, keepdims=True)
            dxh = du * g_ref[...]
            proj = jnp.mean(dxh * xhat, axis=-1, keepdims=True)
            dx_ref[...] = r_ref[...] + inv * (dxh - xhat * proj)

    return pl.pallas_call(
        body, name=name, grid=(T // tm, nk),
        in_specs=[pl.BlockSpec((tm, tk), lambda m, k: (m, k)),
                  pl.BlockSpec((1, D_MODEL, tk), lambda m, k: (k // per_shard, 0, k % per_shard)),
                  pl.BlockSpec((tm, D_MODEL), lambda m, k: (m, 0)),
                  pl.BlockSpec((1, D_MODEL), lambda m, k: (0, 0)),
                  pl.BlockSpec((tm, D_MODEL), lambda m, k: (m, 0))],
        out_specs=[pl.BlockSpec((tm, D_MODEL), lambda m, k: (m, 0)),
                   pl.BlockSpec((1, D_MODEL), lambda m, k: (0, 0))],
        out_shape=[jax.ShapeDtypeStruct((T, D_MODEL), F32),
                   jax.ShapeDtypeStruct((1, D_MODEL), F32)],
        scratch_shapes=[pltpu.VMEM((tm, D_MODEL), F32)],
        compiler_params=_params(2),
    )(dproj, w4, h, g_row, dres)


def _window_of(g):
    return jnp.left_shift(2, g)


def _select_stage(g, stages):
    res = stages[0]
    for i in range(1, len(stages)):
        res = jnp.where(g >= i, stages[i], res)
    return res


def _pool_fwd(proj0, wg4, scale_row, S, name, tm=256):
    T = proj0.shape[0]
    blocks_per_seq = S // tm
    hb = tm // HALO

    def body(x_ref, halo_ref, z_ref, w_ref, s_ref, y_ref, p_ref, mix_ref):
        m, g = pl.program_id(0), pl.program_id(1)
        first = (m % blocks_per_seq) == 0
        halo = jnp.where(first, 0.0, halo_ref[...].astype(F32))
        x = x_ref[...].astype(F32)
        ext = jnp.concatenate([halo, x], axis=0)
        stages = []
        cur = ext
        for sh in (1, 2, 4, 8):
            cur = cur + pltpu.roll(cur, sh, 0)
            stages.append(cur[HALO:, :])
        win_sum = _select_stage(g, stages)
        pos = (m % blocks_per_seq) * tm + lax.broadcasted_iota(jnp.int32, (tm, 1), 0)
        count = jnp.minimum(pos + 1, _window_of(g)).astype(F32)
        pooled = win_sum / count - x
        pooled_b = pooled.astype(BF16)
        mixed = _dot(pooled_b, w_ref[...].reshape(GROUP_DIM, GROUP_DIM))
        z = z_ref[...].astype(F32)
        y_ref[...] = (mixed * s_ref[...] * (z * _sigmoid(z))).astype(BF16)
        p_ref[...] = pooled_b
        mix_ref[...] = mixed.astype(BF16)

    blk = lambda m, g: (m, g)
    return pl.pallas_call(
        body, name=name, grid=(T // tm, N_GROUPS),
        in_specs=[pl.BlockSpec((tm, GROUP_DIM), blk),
                  pl.BlockSpec((HALO, GROUP_DIM), lambda m, g: (jnp.maximum(m * hb - 1, 0), g)),
                  pl.BlockSpec((tm, GROUP_DIM), lambda m, g: (m, N_GROUPS + g)),
                  pl.BlockSpec((N_CHIPS, GROUP_DIM // N_CHIPS, GROUP_DIM), lambda m, g: (0, g, 0)),
                  pl.BlockSpec((1, GROUP_DIM), lambda m, g: (0, g))],
        out_specs=[pl.BlockSpec((tm, GROUP_DIM), blk)] * 3,
        out_shape=[jax.ShapeDtypeStruct((T, D_INNER), BF16)] * 3,
        compiler_params=_params(2),
    )(proj0, proj0, proj0, wg4, scale_row)


def _pool_gate_bwd(dh, w_out, proj0, mixed, scale_row, name, tm=1024, tn=512):
    T = dh.shape[0]
    gate_b0 = D_INNER // tn

    def body(dh_ref, w_ref, z_ref, mix_ref, s_ref, dm_ref, dz_ref, ds_ref):
        @pl.when(pl.program_id(1) == 0)
        def _():
            ds_ref[...] = jnp.zeros_like(ds_ref)

        dy = _dot_nt(dh_ref[...].astype(BF16), w_ref[...])
        z = z_ref[...].astype(F32)
        sig = _sigmoid(z)
        silu = z * sig
        mixed = mix_ref[...].astype(F32)
        s = s_ref[...]
        dm_ref[...] = (dy * s * silu).astype(BF16)
        dz_ref[...] = (dy * mixed * s * (sig * (1.0 + z * (1.0 - sig)))).astype(BF16)
        ds_ref[...] += jnp.sum(dy * mixed * silu, axis=0, keepdims=True)

    return pl.pallas_call(
        body, name=name, grid=(D_INNER // tn, T // tm),
        in_specs=[pl.BlockSpec((tm, D_MODEL), lambda n, m: (m, 0)),
                  pl.BlockSpec((tn, D_MODEL), lambda n, m: (n, 0)),
                  pl.BlockSpec((tm, tn), lambda n, m: (m, gate_b0 + n)),
                  pl.BlockSpec((tm, tn), lambda n, m: (m, n)),
                  pl.BlockSpec((1, tn), lambda n, m: (0, n))],
        out_specs=[pl.BlockSpec((tm, tn), lambda n, m: (m, n)),
                   pl.BlockSpec((tm, tn), lambda n, m: (m, gate_b0 + n)),
                   pl.BlockSpec((1, tn), lambda n, m: (0, n))],
        out_shape=[jax.ShapeDtypeStruct((T, D_INNER), BF16),
                   jax.ShapeDtypeStruct((T, 2 * D_INNER), BF16),
                   jax.ShapeDtypeStruct((1, D_INNER), F32)],
        compiler_params=_params(2),
    )(dh, w_out, proj0, mixed, scale_row)


def _pool_bwd(dmixed, wg4, dproj0, S, name, tm=256):
    T = dmixed.shape[0]
    blocks_per_seq = S // tm
    hb = tm // HALO
    n_halo_blocks = T // HALO

    def body(dm_ref, halo_ref, w_ref, _, o_ref):
        m, g = pl.program_id(0), pl.program_id(1)
        ext = jnp.concatenate([dm_ref[...], halo_ref[...]], axis=0)
        dp = _dot_nt(ext, w_ref[...].reshape(GROUP_DIM, GROUP_DIM))
        pos = (m % blocks_per_seq) * tm + lax.broadcasted_iota(jnp.int32, (tm + HALO, 1), 0)
        count = jnp.minimum(pos + 1, _window_of(g)).astype(F32)
        c = jnp.where(pos < S, dp / count, 0.0)
        n = tm + HALO
        stages = []
        cur = c
        for sh in (1, 2, 4, 8):
            cur = cur + pltpu.roll(cur, n - sh, 0)
            stages.append(cur[:tm, :])
        o_ref[...] = (_select_stage(g, stages) - dp[:tm, :]).astype(BF16)

    blk = lambda m, g: (m, g)
    return pl.pallas_call(
        body, name=name, grid=(T // tm, N_GROUPS),
        in_specs=[pl.BlockSpec((tm, GROUP_DIM), blk),
                  pl.BlockSpec((HALO, GROUP_DIM),
                               lambda m, g: (jnp.minimum((m + 1) * hb, n_halo_blocks - 1), g)),
                  pl.BlockSpec((N_CHIPS, GROUP_DIM // N_CHIPS, GROUP_DIM), lambda m, g: (0, g, 0)),
                  HBM_SPEC],
        out_specs=pl.BlockSpec((tm, GROUP_DIM), blk),
        out_shape=jax.ShapeDtypeStruct(dproj0.shape, dproj0.dtype),
        input_output_aliases={3: 0},
        compiler_params=_params(2),
    )(dmixed, dmixed, wg4, dproj0)


TQ = 256


def _split_dot(x, m):
    hi = x.astype(BF16)
    lo = (x - hi.astype(F32)).astype(BF16)
    return _dot(hi, m) + _dot(lo, m)


NEG_LOG2E = -1.4426950408889634


def _log_terms(z):
    soft = jnp.log(1.0 + jnp.exp2(jnp.abs(z) * NEG_LOG2E))
    log_beta = jnp.minimum(z, 0.0) - soft
    return log_beta, log_beta - z


N_HEADS = 16
FWD_HEADS = BWD_HEADS = 4


def _masked_heads(x, heads):
    lane = lax.broadcasted_iota(jnp.int32, (1, HEAD_PAIR_QK), 1)
    out = []
    for hh in range(heads):
        slab = x[:, (hh // 2) * HEAD_PAIR_QK:(hh // 2 + 1) * HEAD_PAIR_QK]
        out.append(jnp.where((lane // 64) == hh % 2, slab, jnp.zeros_like(slab)))
    return out


def _attn_fwd(qkv, z1, S, name):
    T = qkv.shape[0]
    nq = S // TQ
    HEADS, QK_W, V_W = FWD_HEADS, FWD_HEADS * 64, FWD_HEADS * HEAD_V
    k_b0 = QK_WIDTH // QK_W
    v_b0 = 2 * QK_WIDTH // V_W
    hs = range(HEADS)

    def body(q_ref, k_ref, v_ref, z_ref, o_ref, y_ref, lt_ref):
        qi = pl.program_id(2)
        row = lax.broadcasted_iota(jnp.int32, (TQ, TQ), 0)
        col = lax.broadcasted_iota(jnp.int32, (TQ, TQ), 1)
        causal = col < row
        later_in_block = (row > col).astype(BF16)
        qms = [qm * 0.125 for qm in _masked_heads(q_ref[...], HEADS)]

        def step(j, carry, diagonal):
            koff = pl.multiple_of(j * TQ, TQ)
            kbs = [k_ref[pl.ds(koff, TQ), p * HEAD_PAIR_QK:(p + 1) * HEAD_PAIR_QK]
                   for p in range(HEADS // 2)]
            run, acc = [carry[2 * hh] for hh in hs], [carry[2 * hh + 1] for hh in hs]
            z = [_dot_nt(qms[hh], kbs[hh // 2]) for hh in hs]
            terms = [_log_terms(z[hh]) for hh in hs]
            log_om = [jnp.where(causal, t[1], 0.0) if diagonal else t[1] for t in terms]
            later = [_split_dot(log_om[hh], later_in_block) for hh in hs]
            a = [jnp.exp(terms[hh][0] + (run[hh] + later[hh])) for hh in hs]
            if diagonal:
                a = [jnp.where(causal, a[hh], 0.0) for hh in hs]
            out = []
            for hh in hs:
                vb = v_ref[pl.ds(koff, TQ), hh * HEAD_V:(hh + 1) * HEAD_V]
                out += [run[hh] + jnp.sum(log_om[hh], axis=1, keepdims=True),
                        acc[hh] + _dot(a[hh].astype(BF16), vb)]
            return tuple(out)

        zero = (jnp.zeros((TQ, 1), F32), jnp.zeros((TQ, HEAD_V), F32))
        carry = step(qi, zero * HEADS, True)
        carry = lax.fori_loop(0, qi, lambda i, c: step(qi - 1 - i, c, False), carry)
        for hh in hs:
            sl = slice(hh * HEAD_V, (hh + 1) * HEAD_V)
            acc = carry[2 * hh + 1]
            z = z_ref[:, sl].astype(F32)
            o_ref[:, sl] = acc.astype(BF16)
            y_ref[:, sl] = (acc * (z * _sigmoid(z))).astype(BF16)
            lt_ref[:, hh:hh + 1] = carry[2 * hh]

    qblk = lambda b, p, i: (b * nq + i, p)
    return pl.pallas_call(
        body, name=name, grid=(T // S, N_HEADS // HEADS, nq),
        in_specs=[pl.BlockSpec((TQ, QK_W), qblk),
                  pl.BlockSpec((S, QK_W), lambda b, p, i: (b, k_b0 + p)),
                  pl.BlockSpec((S, V_W), lambda b, p, i: (b, v_b0 + p)),
                  pl.BlockSpec((TQ, V_W), qblk)],
        out_specs=[pl.BlockSpec((TQ, V_W), qblk),
                   pl.BlockSpec((TQ, V_W), qblk),
                   pl.BlockSpec((None, TQ, HEADS), lambda b, p, i: (p, b * nq + i, 0))],
        out_shape=[jax.ShapeDtypeStruct((T, D_INNER), BF16),
                   jax.ShapeDtypeStruct((T, D_INNER), BF16),
                   jax.ShapeDtypeStruct((N_HEADS // HEADS, T, HEADS), F32)],
        compiler_params=_params(3),
    )(qkv, qkv, qkv, z1)


def _attn_gate_bwd(dh, w_out, z1, o, name, tm=1024, tn=512):
    T = dh.shape[0]
    gate_b0 = (2 * QK_WIDTH + D_INNER) // tn

    def body(dh_ref, w_ref, z_ref, o_ref, do_ref, dz_ref, dh_s):
        @pl.when(pl.program_id(1) == 0)
        def _():
            dh_s[...] = dh_ref[...].astype(BF16)

        dy = _dot_nt(dh_s[...], w_ref[...])
        z = z_ref[...].astype(F32)
        sig = _sigmoid(z)
        do_ref[...] = (dy * (z * sig)).astype(BF16)
        dz_ref[...] = (dy * o_ref[...].astype(F32) * (sig * (1.0 + z * (1.0 - sig)))).astype(BF16)

    return pl.pallas_call(
        body, name=name, grid=(T // tm, D_INNER // tn),
        in_specs=[pl.BlockSpec((tm, D_MODEL), lambda m, n: (m, 0)),
                  pl.BlockSpec((tn, D_MODEL), lambda m, n: (n, 0)),
                  pl.BlockSpec((tm, tn), lambda m, n: (m, n)),
                  pl.BlockSpec((tm, tn), lambda m, n: (m, n))],
        out_specs=[pl.BlockSpec((tm, tn), lambda m, n: (m, n)),
                   pl.BlockSpec((tm, tn), lambda m, n: (m, gate_b0 + n))],
        out_shape=[jax.ShapeDtypeStruct((T, D_INNER), BF16),
                   jax.ShapeDtypeStruct((T, 2 * QK_WIDTH + 2 * D_INNER), BF16)],
        scratch_shapes=[pltpu.VMEM((tm, D_MODEL), BF16)],
        compiler_params=_params(2),
    )(dh, w_out, z1, o)


def _attn_bwd(qkv, do, ltot, dproj1, S, name):
    T = qkv.shape[0]
    nq = S // TQ
    HEADS, QK_W, V_W = BWD_HEADS, BWD_HEADS * 64, BWD_HEADS * HEAD_V
    k_b0 = QK_WIDTH // QK_W
    v_b0 = 2 * QK_WIDTH // V_W
    hs = range(HEADS)
    pairs = range(HEADS // 2)

    def body(q_ref, k_ref, v_ref, do_ref, lt_ref, _, out_ref, dq_s, dk_s, dv_s, dkb_s, dvb_s, sems):
        b, p = pl.program_id(0), pl.program_id(1)
        row = lax.broadcasted_iota(jnp.int32, (TQ, TQ), 0)
        col = lax.broadcasted_iota(jnp.int32, (TQ, TQ), 1)
        causal = col < row
        upto = (row <= col).astype(BF16)
        before = (row < col).astype(BF16)
        dk_s[...] = jnp.zeros_like(dk_s)
        dv_s[...] = jnp.zeros_like(dv_s)

        def q_block(qi, _):
            qoff = pl.multiple_of(qi * TQ, TQ)
            qms = [qm * 0.125 for qm in _masked_heads(q_ref[pl.ds(qoff, TQ), :], HEADS)]
            vsl = [slice(hh * HEAD_V, (hh + 1) * HEAD_V) for hh in hs]
            psl = [slice(pp * HEAD_PAIR_QK, (pp + 1) * HEAD_PAIR_QK) for pp in pairs]
            do_h = [do_ref[pl.ds(qoff, TQ), sl] for sl in vsl]
            total = [lt_ref[pl.ds(qoff, TQ), hh:hh + 1] for hh in hs]

            def k_block(j, carry, diagonal):
                koff = pl.multiple_of(j * TQ, TQ)
                kms = _masked_heads(k_ref[pl.ds(koff, TQ), :], HEADS)
                g_before = [carry[2 * hh] for hh in hs]
                lom_before = [carry[2 * hh + 1] for hh in hs]
                z = [_dot_nt(qms[hh], kms[hh]) for hh in hs]
                da = [_dot_nt(do_h[hh], v_ref[pl.ds(koff, TQ), vsl[hh]]) for hh in hs]
                terms = [_log_terms(z[hh]) for hh in hs]
                log_om = [jnp.where(causal, t[1], 0.0) if diagonal else t[1] for t in terms]
                prefix = [_split_dot(log_om[hh], upto) for hh in hs]
                a = [jnp.exp(terms[hh][0] + ((total[hh] - lom_before[hh]) - prefix[hh])) for hh in hs]
                if diagonal:
                    a = [jnp.where(causal, a[hh], 0.0) for hh in hs]
                g = [a[hh] * da[hh] for hh in hs]
                g_prefix = [_dot(g[hh].astype(BF16), before) for hh in hs]
                out, dzs = [], []
                for hh in hs:
                    beta = jnp.exp(terms[hh][0])
                    g_excl = (g_before[hh] + g_prefix[hh]) * beta
                    if diagonal:
                        g_excl = jnp.where(causal, g_excl, 0.0)
                    dzs.append((g[hh] * (1.0 - beta) - g_excl).astype(BF16))
                    out += [g_before[hh] + jnp.sum(g[hh], axis=1, keepdims=True),
                            lom_before[hh] + jnp.sum(log_om[hh], axis=1, keepdims=True)]
                for hh in hs:
                    dv_s[pl.ds(koff, TQ), vsl[hh]] += _dot_tn(a[hh].astype(BF16), do_h[hh])
                dq = []
                for pp in pairs:
                    pair = slice(2 * pp, 2 * pp + 2)
                    dq.append(carry[2 * HEADS + pp] + _dot(jnp.concatenate(dzs[pair], axis=1),
                                                           jnp.concatenate(kms[pair], axis=0)))
                    dk_s[pl.ds(koff, TQ), psl[pp]] += _dot_tn(jnp.concatenate(dzs[pair], axis=0),
                                                              jnp.concatenate(qms[pair], axis=0))
                return tuple(out) + tuple(dq)

            zero = jnp.zeros((TQ, 1), F32)
            carry = (zero,) * (2 * HEADS) + (jnp.zeros((TQ, HEAD_PAIR_QK), F32),) * (HEADS // 2)
            carry = lax.fori_loop(0, qi, lambda j, c: k_block(j, c, False), carry)
            carry = k_block(qi, carry, True)
            for pp in pairs:
                dq_s[pl.ds(qoff, TQ), psl[pp]] = (carry[2 * HEADS + pp] * 0.125).astype(BF16)
            return 0

        lax.fori_loop(0, nq, q_block, 0)
        dkb_s[...] = dk_s[...].astype(BF16)
        dvb_s[...] = dv_s[...].astype(BF16)
        rows = pl.ds(pl.multiple_of(b * S, TQ), S)
        copies = [
            pltpu.make_async_copy(
                dq_s, out_ref.at[rows, pl.ds(pl.multiple_of(p * QK_W, 128), QK_W)], sems.at[0]),
            pltpu.make_async_copy(
                dkb_s, out_ref.at[rows, pl.ds(pl.multiple_of(QK_WIDTH + p * QK_W, 128), QK_W)],
                sems.at[1]),
            pltpu.make_async_copy(
                dvb_s, out_ref.at[rows, pl.ds(pl.multiple_of(2 * QK_WIDTH + p * V_W, 128), V_W)],
                sems.at[2]),
        ]
        for cp in copies:
            cp.start()
        for cp in copies:
            cp.wait()

    return pl.pallas_call(
        body, name=name, grid=(T // S, N_HEADS // HEADS),
        in_specs=[pl.BlockSpec((S, QK_W), lambda b, p: (b, p)),
                  pl.BlockSpec((S, QK_W), lambda b, p: (b, k_b0 + p)),
                  pl.BlockSpec((S, V_W), lambda b, p: (b, v_b0 + p)),
                  pl.BlockSpec((S, V_W), lambda b, p: (b, p)),
                  pl.BlockSpec((None, S, HEADS), lambda b, p: (p, b, 0)),
                  HBM_SPEC],
        out_specs=HBM_SPEC,
        out_shape=jax.ShapeDtypeStruct(dproj1.shape, dproj1.dtype),
        input_output_aliases={5: 0},
        scratch_shapes=[pltpu.VMEM((S, QK_W), BF16),
                        pltpu.VMEM((S, QK_W), F32),
                        pltpu.VMEM((S, V_W), F32),
                        pltpu.VMEM((S, QK_W), BF16),
                        pltpu.VMEM((S, V_W), BF16),
                        pltpu.SemaphoreType.DMA((3,))],
        compiler_params=_params(2),
    )(qkv, qkv, qkv, do, ltot, dproj1)


def _loss_head(h, g_row, target, name, tm=512):
    T = h.shape[0]

    def body(h_ref, g_ref, t_ref, dh_ref, dg_ref, loss_ref):
        @pl.when(pl.program_id(0) == 0)
        def _():
            dg_ref[...] = jnp.zeros_like(dg_ref)
            loss_ref[...] = jnp.zeros_like(loss_ref)

        x = h_ref[...]
        inv = lax.rsqrt(jnp.mean(x * x, axis=-1, keepdims=True) + RMS_EPS)
        xhat = x * inv
        gain = g_ref[...]
        err = xhat * gain - t_ref[...]
        per_token = jnp.mean(err * err, axis=-1, keepdims=True)
        loss_ref[...] += 0.5 * jnp.sum(per_token, axis=0, keepdims=True)
        dy = err * (1.0 / D_MODEL)
        dg_ref[...] += jnp.sum(dy * xhat, axis=0, keepdims=True)
        dxh = dy * gain
        proj = jnp.mean(dxh * xhat, axis=-1, keepdims=True)
        dh_ref[...] = inv * (dxh - xhat * proj)

    return pl.pallas_call(
        body, name=name, grid=(T // tm,),
        in_specs=[pl.BlockSpec((tm, D_MODEL), lambda m: (m, 0)),
                  pl.BlockSpec((1, D_MODEL), lambda m: (0, 0)),
                  pl.BlockSpec((tm, D_MODEL), lambda m: (m, 0))],
        out_specs=[pl.BlockSpec((tm, D_MODEL), lambda m: (m, 0)),
                   pl.BlockSpec((1, D_MODEL), lambda m: (0, 0)),
                   pl.BlockSpec((1, 128), lambda m: (0, 0))],
        out_shape=[jax.ShapeDtypeStruct((T, D_MODEL), F32),
                   jax.ShapeDtypeStruct((1, D_MODEL), F32),
                   jax.ShapeDtypeStruct((1, 128), F32)],
        compiler_params=_params(1),
    )(h, g_row, target)


def _place():
    return lax.axis_index("x"), lax.axis_index("y"), lax.axis_index("c")


def _other_chips(x, y):
    return [(1 - x, y), (x, 1 - y), (1 - x, 1 - y)]


def _half(ref, c):
    hr = ref.shape[-2] // 2
    return pl.ds(pl.multiple_of(c * hr, 8), hr)


def _cast_to_slot(shard, chip, name, tr=256):
    R, C = shard.shape

    def body(chip_ref, w_ref, o_ref):
        o_ref[0] = w_ref[...].astype(BF16)

    return pl.pallas_call(
        body, name=name,
        grid_spec=pltpu.PrefetchScalarGridSpec(
            num_scalar_prefetch=1, grid=(R // tr,),
            in_specs=[pl.BlockSpec((tr, C), lambda i, chip_ref: (i, 0))],
            out_specs=pl.BlockSpec((1, tr, C), lambda i, chip_ref: (chip_ref[0], i, 0))),
        out_shape=jax.ShapeDtypeStruct((N_CHIPS, R, C), BF16),
        compiler_params=_params(1),
    )(chip, shard)


def _weight_plan(bufs):
    x, y, c = _place()
    plan = []
    for buf in bufs:
        mine = buf.at[2 * x + y, _half(buf, c)]
        for ox, oy in _other_chips(x, y):
            plan.append((mine, mine, (ox, oy, c), buf.at[2 * ox + oy, _half(buf, c)]))
    return plan


def _chip_sum_plan(bufs):
    x, y, c = _place()
    n = len(bufs) // 2
    plan = []
    for sums, land in zip(bufs[:n], bufs[n:]):
        for k, (ox, oy) in enumerate(_other_chips(x, y)):
            plan.append((sums.at[2 * ox + oy], land.at[k], (ox, oy, c), land.at[k]))
    return plan


SEM_SPEC = pl.BlockSpec(memory_space=pltpu.SEMAPHORE)
ANY_SPEC = pl.BlockSpec(memory_space=pl.ANY)
DATAFLOW = pltpu.SideEffectType.DATAFLOW_SIDE_EFFECTING


def _in_hbm(a):
    return pltpu.with_memory_space_constraint(a, pltpu.HBM)


def _exchange_start(bufs, after, plan, n_copies, name):
    nb = len(bufs)

    def body(*refs):
        send_sems, recv_sems = refs[nb + 1], refs[nb + 2]
        for i, (src, dst, dev, _) in enumerate(plan(refs[:nb])):
            pltpu.make_async_remote_copy(
                src_ref=src, dst_ref=dst, send_sem=send_sems.at[i], recv_sem=recv_sems.at[i],
                device_id=dev, device_id_type=MESH).start()
        token = refs[-1]
        token[...] = jnp.zeros_like(token)

    res = pl.pallas_call(
        body, name=name,
        in_specs=[HBM_SPEC] * nb + [ANY_SPEC],
        out_specs=[SEM_SPEC, SEM_SPEC] + [HBM_SPEC] * nb + [pl.BlockSpec(memory_space=pltpu.VMEM)],
        out_shape=[pltpu.SemaphoreType.DMA((n_copies,)), pltpu.SemaphoreType.DMA((n_copies,))]
        + [pltpu.HBM(b.shape, b.dtype) for b in bufs] + [jax.ShapeDtypeStruct((8, 128), F32)],
        input_output_aliases={i: 2 + i for i in range(nb)},
        compiler_params=pltpu.CompilerParams(has_side_effects=DATAFLOW),
    )(*[_in_hbm(b) for b in bufs], after)
    return res[0], res[1], list(res[2:2 + nb]), res[-1]


def _exchange_wait(bufs, send_sems, recv_sems, after, plan, name):
    nb = len(bufs)

    def body(*refs):
        sends, recvs = refs[nb], refs[nb + 1]
        for i, (src, dst, dev, landing) in enumerate(plan(refs[:nb])):
            pltpu.make_async_remote_copy(
                src_ref=src, dst_ref=landing, send_sem=sends.at[i], recv_sem=recvs.at[i],
                device_id=dev, device_id_type=MESH).wait()

    res = pl.pallas_call(
        body, name=name,
        in_specs=[HBM_SPEC] * nb + [SEM_SPEC, SEM_SPEC, ANY_SPEC],
        out_specs=[HBM_SPEC] * nb,
        out_shape=[pltpu.HBM(b.shape, b.dtype) for b in bufs],
        input_output_aliases={i: i for i in range(nb)},
        compiler_params=pltpu.CompilerParams(has_side_effects=DATAFLOW),
    )(*bufs, send_sems, recv_sems, after)
    return list(res)


def _allgather_weights(slots, name, landed=False):
    n = len(slots)

    def body(*refs):
        outs = refs[n:2 * n]
        send_sems, recv_sems, fwd_send, fwd_recv = refs[2 * n:]
        x, y, c = _place()
        chips = _other_chips(x, y)

        def landing(a, chip, half_of):
            return outs[a].at[2 * chip[0] + chip[1], _half(outs[a], half_of)]

        def ici(a, k, chip_from, to):
            return pltpu.make_async_remote_copy(
                src_ref=landing(a, chip_from, c), dst_ref=landing(a, chip_from, c),
                send_sem=send_sems.at[a, k], recv_sem=recv_sems.at[a, k],
                device_id=to, device_id_type=MESH)

        def d2d(a, k, chip_from, half_of):
            return pltpu.make_async_remote_copy(
                src_ref=landing(a, chip_from, half_of), dst_ref=landing(a, chip_from, half_of),
                send_sem=fwd_send.at[a, k], recv_sem=fwd_recv.at[a, k],
                device_id=(x, y, 1 - c), device_id_type=MESH)

        sends = []
        if not landed:
            sends = [ici(a, k, (x, y), (*chips[k], c)) for a in range(n) for k in range(3)]
        for cp in sends:
            cp.start()
        forwards = []
        for a in range(n):
            for k in range(3):
                if not landed:
                    ici(a, k, chips[k], (x, y, c)).wait_recv()
                fw = d2d(a, k, chips[k], c)
                fw.start()
                forwards.append(fw)
        for a in range(n):
            for k in range(3):
                d2d(a, k, chips[k], 1 - c).wait_recv()
        for cp in sends + forwards:
            cp.wait_send()

    return pl.pallas_call(
        body, name=name,
        in_specs=[HBM_SPEC] * n, out_specs=[HBM_SPEC] * n,
        out_shape=[jax.ShapeDtypeStruct(s.shape, s.dtype) for s in slots],
        input_output_aliases={a: a for a in range(n)},
        scratch_shapes=[pltpu.SemaphoreType.DMA((n, 3)), pltpu.SemaphoreType.DMA((n, 3)),
                        pltpu.SemaphoreType.DMA((n, 3)), pltpu.SemaphoreType.DMA((n, 3))],
    )(*slots)


def _sibling_exchange(partials, small, name):
    n = len(partials)
    ns = 0 if small is None else 1

    def body(*refs):
        ins, outs = refs[:n], refs[n + ns:2 * n + ns]
        send_sems, recv_sems = refs[2 * (n + ns):2 * (n + ns) + 2]
        x, y, c = _place()
        me = 4 * x + 2 * y + c
        sends = [pltpu.make_async_remote_copy(
            src_ref=ins[a].at[:, _half(ins[a], 1 - c)], dst_ref=outs[a],
            send_sem=send_sems.at[a], recv_sem=recv_sems.at[a],
            device_id=(x, y, 1 - c), device_id_type=MESH) for a in range(n)]
        if ns:
            small_ref, small_all = refs[n], refs[2 * n + 1]
            s_send, s_recv, loc_sem = refs[2 * (n + ns) + 2:]
            local = pltpu.make_async_copy(small_ref, small_all.at[me], loc_sem)
            local.start()
            for d in range(1, N_DEV):
                px, py, pc = x ^ ((d >> 2) & 1), y ^ ((d >> 1) & 1), c ^ (d & 1)
                sends.append(pltpu.make_async_remote_copy(
                    src_ref=small_ref, dst_ref=small_all.at[me],
                    send_sem=s_send.at[d - 1], recv_sem=s_recv.at[d - 1],
                    device_id=(px, py, pc), device_id_type=MESH))
        for cp in sends:
            cp.start()
        if ns:
            for d in range(1, N_DEV):
                pltpu.make_async_remote_copy(
                    src_ref=small_ref, dst_ref=small_all.at[me ^ d],
                    send_sem=s_send.at[d - 1], recv_sem=s_recv.at[d - 1],
                    device_id=(x, y, c), device_id_type=MESH).wait_recv()
        for cp in sends[:n]:
            cp.wait_recv()
        for cp in sends:
            cp.wait_send()
        if ns:
            local.wait()

    out_shape = [jax.ShapeDtypeStruct((N_CHIPS, p.shape[1] // 2, p.shape[2]), F32) for p in partials]
    scratch = [pltpu.SemaphoreType.DMA((max(n, 1),)), pltpu.SemaphoreType.DMA((max(n, 1),))]
    if ns:
        out_shape.append(jax.ShapeDtypeStruct((N_DEV,) + small.shape, F32))
        scratch += [pltpu.SemaphoreType.DMA((N_DEV - 1,)), pltpu.SemaphoreType.DMA((N_DEV - 1,)),
                    pltpu.SemaphoreType.DMA]
    return pl.pallas_call(
        body, name=name,
        in_specs=[HBM_SPEC] * (n + ns), out_specs=[HBM_SPEC] * (n + ns),
        out_shape=out_shape, scratch_shapes=scratch,
    )(*partials, *([small] if ns else []))


def _chip_sum(partial, from_sibling, c, name, tr=256):
    _, hr, C = from_sibling.shape
    nb = hr // tr

    def body(c_ref, p_ref, s_ref, o_ref):
        o_ref[...] = (p_ref[...] + s_ref[...]).astype(BF16)

    return pl.pallas_call(
        body, name=name,
        grid_spec=pltpu.PrefetchScalarGridSpec(
            num_scalar_prefetch=1, grid=(N_CHIPS, nb),
            in_specs=[pl.BlockSpec((1, tr, C), lambda j, i, c_ref: (j, c_ref[0] * nb + i, 0)),
                      pl.BlockSpec((1, tr, C), lambda j, i, c_ref: (j, i, 0))],
            out_specs=pl.BlockSpec((1, tr, C), lambda j, i, c_ref: (j, i, 0))),
        out_shape=jax.ShapeDtypeStruct(from_sibling.shape, BF16),
        compiler_params=_params(2),
    )(c, partial, from_sibling)


def _reduce_half(partial, from_sibling, received, place, name, tr=256):
    _, hr, C = from_sibling.shape
    nb = hr // tr

    def body(p_ref, mine_ref, sib_ref, r_ref, o_ref):
        acc = mine_ref[0] + sib_ref[0]
        for k in range(3):
            acc = acc + r_ref[k].astype(F32)
        o_ref[...] = acc

    return pl.pallas_call(
        body, name=name,
        grid_spec=pltpu.PrefetchScalarGridSpec(
            num_scalar_prefetch=1, grid=(nb,),
            in_specs=[pl.BlockSpec((1, tr, C), lambda i, p: (p[0], p[1] * nb + i, 0)),
                      pl.BlockSpec((1, tr, C), lambda i, p: (p[0], i, 0)),
                      pl.BlockSpec((3, tr, C), lambda i, p: (0, i, 0))],
            out_specs=pl.BlockSpec((tr, C), lambda i, p: (p[1] * nb + i, 0))),
        out_shape=jax.ShapeDtypeStruct((2 * hr, C), F32),
        compiler_params=_params(1),
    )(place, partial, from_sibling, received)


def _join_halves(fulls, name):
    n = len(fulls)

    def body(*refs):
        outs = refs[n:2 * n]
        send_sems, recv_sems = refs[2 * n:]
        x, y, c = _place()

        def copy(a, half_of, to):
            rows = outs[a].at[_half(outs[a], half_of)]
            return pltpu.make_async_remote_copy(
                src_ref=rows, dst_ref=rows, send_sem=send_sems.at[a], recv_sem=recv_sems.at[a],
                device_id=to, device_id_type=MESH)

        sends = [copy(a, c, (x, y, 1 - c)) for a in range(n)]
        for cp in sends:
            cp.start()
        for a in range(n):
            copy(a, 1 - c, (x, y, c)).wait_recv()
        for cp in sends:
            cp.wait_send()

    return pl.pallas_call(
        body, name=name,
        in_specs=[HBM_SPEC] * n, out_specs=[HBM_SPEC] * n,
        out_shape=[jax.ShapeDtypeStruct(f.shape, F32) for f in fulls],
        input_output_aliases={a: a for a in range(n)},
        scratch_shapes=[pltpu.SemaphoreType.DMA((n,)), pltpu.SemaphoreType.DMA((n,))],
    )(*fulls)


def _adamw_math(w, g, m, v):
    m = ADAM_B1 * m + (1.0 - ADAM_B1) * g
    v = ADAM_B2 * v + (1.0 - ADAM_B2) * (g * g)
    m_hat = m / (1.0 - ADAM_B1 ** ADAM_STEP)
    v_hat = v / (1.0 - ADAM_B2 ** ADAM_STEP)
    delta = -ADAM_LR * (m_hat / (jnp.sqrt(v_hat) + ADAM_EPS) + ADAM_WD * w)
    return delta, m, v


def _adamw(w, g, m, v, name, tr=256):
    R, C = w.shape
    tr = min(tr, R)

    def body(w_ref, g_ref, m_ref, v_ref, d_out, m_out, v_out):
        d_out[...], m_out[...], v_out[...] = _adamw_math(w_ref[...], g_ref[...], m_ref[...], v_ref[...])

    spec = pl.BlockSpec((tr, C), lambda i: (i, 0))
    return pl.pallas_call(
        body, name=name, grid=(R // tr,),
        in_specs=[spec] * 4, out_specs=[spec] * 3,
        out_shape=[jax.ShapeDtypeStruct((R, C), F32)] * 3,
        compiler_params=_params(1),
    )(w, g, m, v)


def _adamw_small(small_all, w, m, v, name):
    def body(s_ref, w_ref, m_ref, v_ref, g_out, d_out, m_out, v_out):
        g = s_ref[0]
        for d in range(1, N_DEV):
            g = g + s_ref[d]
        g_out[...] = g
        d_out[...], m_out[...], v_out[...] = _adamw_math(w_ref[...], g, m_ref[...], v_ref[...])

    vm = pl.BlockSpec(memory_space=pltpu.VMEM)
    return pl.pallas_call(
        body, name=name, in_specs=[vm] * 4, out_specs=[vm] * 4,
        out_shape=[jax.ShapeDtypeStruct(w.shape, F32)] * 4,
    )(small_all, w, m, v)


def _pack_small(norm_g, pool_scale, norm_f, extra_row):
    return jnp.concatenate([norm_g.reshape(2, D_MODEL), pool_scale.reshape(2, D_MODEL),
                            norm_f.reshape(1, D_MODEL), extra_row,
                            jnp.zeros((2, D_MODEL), F32)], axis=0)


def kernel(x, norm_g, pool_w_in, pool_w, pool_scale, pool_w_out, sb_w_in, sb_w_out, norm_f, loss_target, m_norm_g, m_pool_w_in, m_pool_w, m_pool_scale, m_pool_w_out, m_sb_w_in, m_sb_w_out, m_norm_f, v_norm_g, v_pool_w_in, v_pool_w, v_pool_scale, v_pool_w_out, v_sb_w_in, v_sb_w_out, v_norm_f):
    nb, S, _ = x.shape
    T = nb * S
    xt = x.reshape(T, D_MODEL)
    target = loss_target.reshape(T, D_MODEL)
    cx, cy, cc = _place()

    def shard2d(w):
        return w.reshape(-1, w.shape[-1])

    names = ("pool_w_in", "pool_w", "pool_w_out", "sb_w_in", "sb_w_out")
    w_shards = [shard2d(w) for w in (pool_w_in, pool_w, pool_w_out, sb_w_in, sb_w_out)]
    m_shards = [shard2d(w) for w in (m_pool_w_in, m_pool_w, m_pool_w_out, m_sb_w_in, m_sb_w_out)]
    v_shards = [shard2d(w) for w in (v_pool_w_in, v_pool_w, v_pool_w_out, v_sb_w_in, v_sb_w_out)]

    chip = (2 * cx + cy).reshape(1).astype(jnp.int32)
    c_arr = cc.reshape(1).astype(jnp.int32)
    place = jnp.stack([2 * cx + cy, cc]).astype(jnp.int32)
    slots = [_cast_to_slot(w, chip, "cast_" + nm) for w, nm in zip(w_shards, names)]
    g0, g1, gf = norm_g[0:1], norm_g[1:2], norm_f.reshape(1, D_MODEL)

    w_pin, w_g, w_pout = _allgather_weights(slots[:3], "allgather_pool_weights")
    w_pout = w_pout.reshape(D_INNER, D_MODEL)
    sb_send, sb_recv, sb_slots, token = _exchange_start(slots[3:], w_pin, _weight_plan, 6,
                                                        "sb_weights_start")

    proj0, u0 = _rms_matmul(xt, g0 + token[0:1, 0:1], w_pin, [(2 * D_INNER, BF16)], "pool_in_proj")
    y0, pooled, mixed = _pool_fwd(proj0, w_g, pool_scale, S, "pool_mix")
    sb_slots = _exchange_wait(sb_slots, sb_send, sb_recv, y0, _weight_plan, "sb_weights_wait")
    w_sin, w_sout = _allgather_weights(sb_slots, "sb_weights_forward", landed=True)
    w_sout = w_sout.reshape(D_INNER, D_MODEL)
    h1 = _matmul_residual(y0, w_pout, xt, "pool_out_proj")
    qkv, z1, u1 = _rms_matmul(h1, g1, w_sin, [(2 * QK_WIDTH + D_INNER, BF16), (D_INNER, BF16)],
                              "sb_in_proj")
    o, y1, ltot = _attn_fwd(qkv, z1, S, "sb_attention")
    h2 = _matmul_residual(y1, w_sout, h1, "sb_out_proj")
    dh2, d_norm_f, loss_row = _loss_head(h2, gf, target, "loss_head")

    shard = lambda i, j, t: (j, 0, 0)
    gw_sout = _matmul_tn(y1, dh2, D_INNER, D_MODEL, (D_INNER, D_MODEL), (1024, 1024),
                         lambda i, j, t: (i, j), "grad_sb_w_out", bm=1024, bn=1024)
    do, dproj1 = _attn_gate_bwd(dh2, w_sout, z1, o, "sb_gate_bwd")
    dproj1 = _attn_bwd(qkv, do, ltot, dproj1, S, "sb_attention_bwd")
    n1 = 2 * QK_WIDTH + 2 * D_INNER
    gw_sin = _matmul_tn(u1, dproj1, D_MODEL, n1, (N_CHIPS, D_MODEL, n1 // 4), (1, D_MODEL, n1 // 4),
                        shard, "grad_sb_w_in", bm=D_MODEL, bn=n1 // 4)

    def reduce_start(partials, tag):
        from_sibling = _sibling_exchange(partials, None, "grad_sibling_exchange_" + tag)
        sums = [_chip_sum(p, s, c_arr, "grad_chip_sum_%s_%d" % (tag, i))
                for i, (p, s) in enumerate(zip(partials, from_sibling))]
        lands = [lax.empty((3,) + s.shape[1:], BF16) for s in sums]
        send, recv, bufs, token = _exchange_start(sums + lands, c_arr, _chip_sum_plan, 3 * len(sums),
                                                  "grad_chip_exchange_start_" + tag)
        return (partials, list(from_sibling), send, recv, bufs), token[0:1, 0:1]

    def reduce_finish(started, after, tag):
        partials, from_sibling, send, recv, bufs = started
        received = _exchange_wait(bufs, send, recv, after, _chip_sum_plan,
                                  "grad_chip_exchange_wait_" + tag)[len(partials):]
        return [_reduce_half(p, s, r, place, "grad_reduce_%s_%d" % (tag, i))
                for i, (p, s, r) in enumerate(zip(partials, from_sibling, received))]

    sb_started, token = reduce_start([gw_sin, gw_sout.reshape(N_CHIPS, -1, D_MODEL)], "sb")
    dh1, d_g1 = _matmul_nt_rms_bwd(dproj1, w_sin, h1, g1 + token, dh2, "sb_in_bwd")
    gw_pout = _matmul_tn(y0, dh1, D_INNER, D_MODEL, (D_INNER, D_MODEL), (1024, 1024),
                         lambda i, j, t: (i, j), "grad_pool_w_out", bm=1024, bn=1024)
    dmixed, dproj0, d_scale = _pool_gate_bwd(dh1, w_pout, proj0, mixed, pool_scale, "pool_gate_bwd")
    gw_g = _matmul_tn(pooled, dmixed, D_INNER, D_INNER, (N_CHIPS, GROUP_DIM, GROUP_DIM),
                      (N_CHIPS, GROUP_DIM // N_CHIPS, GROUP_DIM), lambda i, j, t: (0, i, 0),
                      "grad_pool_w", bm=GROUP_DIM, bn=GROUP_DIM, diagonal_blocks=True)
    mix_started, _ = reduce_start([gw_g, gw_pout.reshape(N_CHIPS, -1, D_MODEL)], "pool_mix")
    dproj0 = _pool_bwd(dmixed, w_g, dproj0, S, "pool_bwd")
    n0 = 2 * D_INNER
    gw_pin = _matmul_tn(u0, dproj0, D_MODEL, n0, (N_CHIPS, D_MODEL, n0 // 4), (1, D_MODEL, n0 // 4),
                        shard, "grad_pool_w_in", bm=D_MODEL, bn=n0 // 4)
    pin_started, token = reduce_start([gw_pin], "pool_in")
    dx, d_g0 = _matmul_nt_rms_bwd(dproj0, w_pin, xt, g0 + token, dh1, "pool_in_bwd")

    small = _pack_small(jnp.concatenate([d_g0, d_g1], axis=0), d_scale, d_norm_f,
                        jnp.broadcast_to(loss_row[:, :1], (1, D_MODEL)))
    small_all, = _sibling_exchange([], small, "small_sums_exchange")
    grads = _join_halves(reduce_finish(pin_started, dx, "pool_in")
                         + reduce_finish(mix_started, dx, "pool_mix")
                         + reduce_finish(sb_started, dx, "sb"), "grad_join_halves")

    deltas, new_m, new_v = [], [], []
    for w, g, m, v, nm in zip(w_shards, grads, m_shards, v_shards, names):
        d, mm, vv = _adamw(w, g, m, v, "adamw_" + nm)
        deltas.append(d)
        new_m.append(mm)
        new_v.append(vv)

    zero_row = jnp.zeros((1, D_MODEL), F32)
    g_small, d_small, m_small, v_small = _adamw_small(
        small_all, _pack_small(norm_g, pool_scale, norm_f, zero_row),
        _pack_small(m_norm_g, m_pool_scale, m_norm_f, zero_row),
        _pack_small(v_norm_g, v_pool_scale, v_norm_f, zero_row + 1.0), "adamw_small")
    loss = g_small[5, 0]

    def unpack_small(a):
        return a[0:2], a[2:4].reshape(1, D_INNER), a[4]

    def assemble(big, small3):
        ng, ps, nf = small3
        return [ng, big[0].reshape(pool_w_in.shape), big[1].reshape(pool_w.shape), ps,
                big[2].reshape(pool_w_out.shape), big[3].reshape(sb_w_in.shape),
                big[4].reshape(sb_w_out.shape), nf]

    return (loss, dx.reshape(x.shape),
            *assemble(grads, unpack_small(g_small)),
            *assemble(deltas, unpack_small(d_small)),
            *assemble(new_m, unpack_small(m_small)),
            *assemble(new_v, unpack_small(v_small)))
```

```python
import functools

import jax
import jax.numpy as jnp
from jax import lax
from jax.experimental import pallas as pl
from jax.experimental.pallas import tpu as pltpu

F32 = jnp.float32
BF16 = jnp.bfloat16
MESH = pl.DeviceIdType.MESH

D_MODEL = 1024
D_INNER = 2048
N_GROUPS = 4
GROUP_DIM = 512
HEAD_PAIR_QK = 128
HEAD_V = 128
N_HEAD_PAIRS = 8
QK_WIDTH = 1024
RMS_EPS = 1e-6
HALO = 16
N_CHIPS = 4
N_DEV = 8

ADAM_LR = 0.001
ADAM_B1 = 0.9
ADAM_B2 = 0.999
ADAM_EPS = 1e-08
ADAM_WD = 0.01
ADAM_STEP = 10

VMEM_LIMIT = 56 * 1024 * 1024

HBM_SPEC = pl.BlockSpec(memory_space=pltpu.HBM)


def _params(n_axes):
    return pltpu.CompilerParams(dimension_semantics=("arbitrary",) * n_axes,
                                vmem_limit_bytes=VMEM_LIMIT)


def _dot(a, b):
    return jnp.dot(a, b, preferred_element_type=F32)


def _dot_nt(a, b):
    return lax.dot_general(a, b, (((1,), (1,)), ((), ())), preferred_element_type=F32)


def _dot_tn(a, b):
    return lax.dot_general(a, b, (((0,), (0,)), ((), ())), preferred_element_type=F32)


def _sigmoid(z):
    return 1.0 / (1.0 + jnp.exp(-z))


def _rms_matmul(h, g_row, w4, outs, name, tm=1024, tn=512):
    T = h.shape[0]
    per_shard = w4.shape[2] // tn
    starts = [0]
    for width, _ in outs:
        starts.append(starts[-1] + width // tn)
    n_out = len(outs)

    def body(h_ref, g_ref, w_ref, *rest):
        o_refs, u_out, u_s = rest[:n_out], rest[n_out], rest[n_out + 1]
        n = pl.program_id(1)

        @pl.when(n == 0)
        def _():
            x = h_ref[...]
            inv = lax.rsqrt(jnp.mean(x * x, axis=-1, keepdims=True) + RMS_EPS)
            u = (x * inv * g_ref[...]).astype(BF16)
            u_s[...] = u
            u_out[...] = u

        res = _dot(u_s[...], w_ref[0])
        for k in range(n_out):
            @pl.when((n >= starts[k]) & (n < starts[k + 1]))
            def _():
                o_refs[k][...] = res.astype(o_refs[k].dtype)

    def out_map(k):
        return lambda m, n: (m, jnp.clip(n - starts[k], 0, starts[k + 1] - starts[k] - 1))

    return pl.pallas_call(
        body, name=name, grid=(T // tm, starts[-1]),
        in_specs=[pl.BlockSpec((tm, D_MODEL), lambda m, n: (m, 0)),
                  pl.BlockSpec((1, D_MODEL), lambda m, n: (0, 0)),
                  pl.BlockSpec((1, D_MODEL, tn), lambda m, n: (n // per_shard, 0, n % per_shard))],
        out_specs=[pl.BlockSpec((tm, tn), out_map(k)) for k in range(n_out)]
        + [pl.BlockSpec((tm, D_MODEL), lambda m, n: (m, 0))],
        out_shape=[jax.ShapeDtypeStruct((T, width), dt) for width, dt in outs]
        + [jax.ShapeDtypeStruct((T, D_MODEL), BF16)],
        scratch_shapes=[pltpu.VMEM((tm, D_MODEL), BF16)],
        compiler_params=_params(2),
    )(h, g_row, w4)


def _matmul_residual(a, w, res, name, tm=1024, tn=512):
    T, K = a.shape
    N = w.shape[1]

    def body(a_ref, w_ref, r_ref, o_ref):
        o_ref[...] = r_ref[...] + _dot(a_ref[...], w_ref[...])

    return pl.pallas_call(
        body, name=name, grid=(T // tm, N // tn),
        in_specs=[pl.BlockSpec((tm, K), lambda m, n: (m, 0)),
                  pl.BlockSpec((K, tn), lambda m, n: (0, n)),
                  pl.BlockSpec((tm, tn), lambda m, n: (m, n))],
        out_specs=pl.BlockSpec((tm, tn), lambda m, n: (m, n)),
        out_shape=jax.ShapeDtypeStruct((T, N), F32),
        compiler_params=_params(2),
    )(a, w, res)


def _matmul_tn(a, b, a_cols, b_cols, out_shape, out_block, out_map, name, bm, bn, tk=512,
               diagonal_blocks=False):
    T = a.shape[0]

    def body(a_ref, b_ref, o_ref):
        @pl.when(pl.program_id(2) == 0)
        def _():
            o_ref[...] = jnp.zeros_like(o_ref)

        part = _dot_tn(a_ref[...].astype(BF16), b_ref[...].astype(BF16))
        o_ref[...] += part.reshape(o_ref.shape)

    b_map = (lambda i, j, t: (t, i)) if diagonal_blocks else (lambda i, j, t: (t, j))
    return pl.pallas_call(
        body, name=name, grid=(a_cols // bm, 1 if diagonal_blocks else b_cols // bn, T // tk),
        in_specs=[pl.BlockSpec((tk, bm), lambda i, j, t: (t, i)),
                  pl.BlockSpec((tk, bn), b_map)],
        out_specs=pl.BlockSpec(out_block, out_map),
        out_shape=jax.ShapeDtypeStruct(out_shape, F32),
        compiler_params=_params(3),
    )(a, b)


def _matmul_nt_rms_bwd(dproj, w4, h, g_row, dres, name, tm=1024, tk=512):
    T, cols = dproj.shape
    per_shard = w4.shape[2] // tk
    nk = cols // tk

    def body(dp_ref, w_ref, h_ref, g_ref, r_ref, dx_ref, dg_ref, acc):
        m, k = pl.program_id(0), pl.program_id(1)

        @pl.when(k == 0)
        def _():
            acc[...] = jnp.zeros_like(acc)

        @pl.when((k == 0) & (m == 0))
        def _():
            dg_ref[...] = jnp.zeros_like(dg_ref)

        acc[...] += _dot_nt(dp_ref[...], w_ref[0])

        @pl.when(k == nk - 1)
        def _():
            du = acc[...]
            x = h_ref[...]
            inv = lax.rsqrt(jnp.mean(x * x, axis=-1, keepdims=True) + RMS_EPS)
            xhat = x * inv
            dg_ref[...] += jnp.sum(du * xhat, axis=0, keepdims=True)
            dxh = du * g_ref[...]
            proj = jnp.mean(dxh * xhat, axis=-1, keepdims=True)
            dx_ref[...] = r_ref[...] + inv * (dxh - xhat * proj)

    return pl.pallas_call(
        body, name=name, grid=(T // tm, nk),
        in_specs=[pl.BlockSpec((tm, tk), lambda m, k: (m, k)),
                  pl.BlockSpec((1, D_MODEL, tk), lambda m, k: (k // per_shard, 0, k % per_shard)),
                  pl.BlockSpec((tm, D_MODEL), lambda m, k: (m, 0)),
                  pl.BlockSpec((1, D_MODEL), lambda m, k: (0, 0)),
                  pl.BlockSpec((tm, D_MODEL), lambda m, k: (m, 0))],
        out_specs=[pl.BlockSpec((tm, D_MODEL), lambda m, k: (m, 0)),
                   pl.BlockSpec((1, D_MODEL), lambda m, k: (0, 0))],
        out_shape=[jax.ShapeDtypeStruct((T, D_MODEL), F32),
                   jax.ShapeDtypeStruct((1, D_MODEL), F32)],
        scratch_shapes=[pltpu.VMEM((tm, D_MODEL), F32)],
        compiler_params=_params(2),
    )(dproj, w4, h, g_row, dres)


def _window_of(g):
    return jnp.left_shift(2, g)


def _select_stage(g, stages):
    res = stages[0]
    for i in range(1, len(stages)):
        res = jnp.where(g >= i, stages[i], res)
    return res


def _pool_fwd(proj0, wg4, scale_row, S, name, tm=256):
    T = proj0.shape[0]
    blocks_per_seq = S // tm
    hb = tm // HALO

    def body(x_ref, halo_ref, z_ref, w_ref, s_ref, y_ref, p_ref, mix_ref):
        m, g = pl.program_id(0), pl.program_id(1)
        first = (m % blocks_per_seq) == 0
        halo = jnp.where(first, 0.0, halo_ref[...].astype(F32))
        x = x_ref[...].astype(F32)
        ext = jnp.concatenate([halo, x], axis=0)
        stages = []
        cur = ext
        for sh in (1, 2, 4, 8):
            cur = cur + pltpu.roll(cur, sh, 0)
            stages.append(cur[HALO:, :])
        win_sum = _select_stage(g, stages)
        pos = (m % blocks_per_seq) * tm + lax.broadcasted_iota(jnp.int32, (tm, 1), 0)
        count = jnp.minimum(pos + 1, _window_of(g)).astype(F32)
        pooled = win_sum / count - x
        pooled_b = pooled.astype(BF16)
        mixed = _dot(pooled_b, w_ref[...].reshape(GROUP_DIM, GROUP_DIM))
        z = z_ref[...].astype(F32)
        y_ref[...] = (mixed * s_ref[...] * (z * _sigmoid(z))).astype(BF16)
        p_ref[...] = pooled_b
        mix_ref[...] = mixed.astype(BF16)

    blk = lambda m, g: (m, g)
    return pl.pallas_call(
        body, name=name, grid=(T // tm, N_GROUPS),
        in_specs=[pl.BlockSpec((tm, GROUP_DIM), blk),
                  pl.BlockSpec((HALO, GROUP_DIM), lambda m, g: (jnp.maximum(m * hb - 1, 0), g)),
                  pl.BlockSpec((tm, GROUP_DIM), lambda m, g: (m, N_GROUPS + g)),
                  pl.BlockSpec((N_CHIPS, GROUP_DIM // N_CHIPS, GROUP_DIM), lambda m, g: (0, g, 0)),
                  pl.BlockSpec((1, GROUP_DIM), lambda m, g: (0, g))],
        out_specs=[pl.BlockSpec((tm, GROUP_DIM), blk)] * 3,
        out_shape=[jax.ShapeDtypeStruct((T, D_INNER), BF16)] * 3,
        compiler_params=_params(2),
    )(proj0, proj0, proj0, wg4, scale_row)


def _pool_gate_bwd(dh, w_out, proj0, mixed, scale_row, name, tm=1024, tn=512):
    T = dh.shape[0]
    gate_b0 = D_INNER // tn

    def body(dh_ref, w_ref, z_ref, mix_ref, s_ref, dm_ref, dz_ref, ds_ref):
        @pl.when(pl.program_id(1) == 0)
        def _():
            ds_ref[...] = jnp.zeros_like(ds_ref)

        dy = _dot_nt(dh_ref[...].astype(BF16), w_ref[...])
        z = z_ref[...].astype(F32)
        sig = _sigmoid(z)
        silu = z * sig
        mixed = mix_ref[...].astype(F32)
        s = s_ref[...]
        dm_ref[...] = (dy * s * silu).astype(BF16)
        dz_ref[...] = (dy * mixed * s * (sig * (1.0 + z * (1.0 - sig)))).astype(BF16)
        ds_ref[...] += jnp.sum(dy * mixed * silu, axis=0, keepdims=True)

    return pl.pallas_call(
        body, name=name, grid=(D_INNER // tn, T // tm),
        in_specs=[pl.BlockSpec((tm, D_MODEL), lambda n, m: (m, 0)),
                  pl.BlockSpec((tn, D_MODEL), lambda n, m: (n, 0)),
                  pl.BlockSpec((tm, tn), lambda n, m: (m, gate_b0 + n)),
                  pl.BlockSpec((tm, tn), lambda n, m: (m, n)),
                  pl.BlockSpec((1, tn), lambda n, m: (0, n))],
        out_specs=[pl.BlockSpec((tm, tn), lambda n, m: (m, n)),
                   pl.BlockSpec((tm, tn), lambda n, m: (m, gate_b0 + n)),
                   pl.BlockSpec((1, tn), lambda n, m: (0, n))],
        out_shape=[jax.ShapeDtypeStruct((T, D_INNER), BF16),
                   jax.ShapeDtypeStruct((T, 2 * D_INNER), BF16),
                   jax.ShapeDtypeStruct((1, D_INNER), F32)],
        compiler_params=_params(2),
    )(dh, w_out, proj0, mixed, scale_row)


def _pool_bwd(dmixed, wg4, dproj0, S, name, tm=256):
    T = dmixed.shape[0]
    blocks_per_seq = S // tm
    hb = tm // HALO
    n_halo_blocks = T // HALO

    def body(dm_ref, halo_ref, w_ref, _, o_ref):
        m, g = pl.program_id(0), pl.program_id(1)
        ext = jnp.concatenate([dm_ref[...], halo_ref[...]], axis=0)
        dp = _dot_nt(ext, w_ref[...].reshape(GROUP_DIM, GROUP_DIM))
        pos = (m % blocks_per_seq) * tm + lax.broadcasted_iota(jnp.int32, (tm + HALO, 1), 0)
        count = jnp.minimum(pos + 1, _window_of(g)).astype(F32)
        c = jnp.where(pos < S, dp / count, 0.0)
        n = tm + HALO
        stages = []
        cur = c
        for sh in (1, 2, 4, 8):
            cur = cur + pltpu.roll(cur, n - sh, 0)
            stages.append(cur[:tm, :])
        o_ref[...] = (_select_stage(g, stages) - dp[:tm, :]).astype(BF16)

    blk = lambda m, g: (m, g)
    return pl.pallas_call(
        body, name=name, grid=(T // tm, N_GROUPS),
        in_specs=[pl.BlockSpec((tm, GROUP_DIM), blk),
                  pl.BlockSpec((HALO, GROUP_DIM),
                               lambda m, g: (jnp.minimum((m + 1) * hb, n_halo_blocks - 1), g)),
                  pl.BlockSpec((N_CHIPS, GROUP_DIM // N_CHIPS, GROUP_DIM), lambda m, g: (0, g, 0)),
                  HBM_SPEC],
        out_specs=pl.BlockSpec((tm, GROUP_DIM), blk),
        out_shape=jax.ShapeDtypeStruct(dproj0.shape, dproj0.dtype),
        input_output_aliases={3: 0},
        compiler_params=_params(2),
    )(dmixed, dmixed, wg4, dproj0)


TQ = 256


def _split_dot(x, m):
    hi = x.astype(BF16)
    lo = (x - hi.astype(F32)).astype(BF16)
    return _dot(hi, m) + _dot(lo, m)


NEG_LOG2E = -1.4426950408889634


def _log_terms(z):
    soft = jnp.log(1.0 + jnp.exp2(jnp.abs(z) * NEG_LOG2E))
    log_beta = jnp.minimum(z, 0.0) - soft
    return log_beta, log_beta - z


N_HEADS = 16
FWD_HEADS = BWD_HEADS = 4


def _masked_heads(x, heads):
    lane = lax.broadcasted_iota(jnp.int32, (1, HEAD_PAIR_QK), 1)
    out = []
    for hh in range(heads):
        slab = x[:, (hh // 2) * HEAD_PAIR_QK:(hh // 2 + 1) * HEAD_PAIR_QK]
        out.append(jnp.where((lane // 64) == hh % 2, slab, jnp.zeros_like(slab)))
    return out


def _attn_fwd(qkvz, S, name):
    T = qkvz.shape[0]
    nq = S // TQ
    HEADS, QK_W, V_W = FWD_HEADS, FWD_HEADS * 64, FWD_HEADS * HEAD_V
    k_b0 = QK_WIDTH // QK_W
    v_b0 = 2 * QK_WIDTH // V_W
    z_b0 = (2 * QK_WIDTH + D_INNER) // V_W
    hs = range(HEADS)

    def body(q_ref, k_ref, v_ref, z_ref, o_ref, y_ref, lt_ref):
        qi = pl.program_id(2)
        row = lax.broadcasted_iota(jnp.int32, (TQ, TQ), 0)
        col = lax.broadcasted_iota(jnp.int32, (TQ, TQ), 1)
        causal = col < row
        later_in_block = (row > col).astype(BF16)
        qms = [qm * 0.125 for qm in _masked_heads(q_ref[...], HEADS)]

        def step(j, carry, diagonal):
            koff = pl.multiple_of(j * TQ, TQ)
            kbs = [k_ref[pl.ds(koff, TQ), p * HEAD_PAIR_QK:(p + 1) * HEAD_PAIR_QK]
                   for p in range(HEADS // 2)]
            run, acc = [carry[2 * hh] for hh in hs], [carry[2 * hh + 1] for hh in hs]
            z = [_dot_nt(qms[hh], kbs[hh // 2]) for hh in hs]
            terms = [_log_terms(z[hh]) for hh in hs]
            log_om = [jnp.where(causal, t[1], 0.0) if diagonal else t[1] for t in terms]
            later = [_split_dot(log_om[hh], later_in_block) for hh in hs]
            a = [jnp.exp(terms[hh][0] + (run[hh] + later[hh])) for hh in hs]
            if diagonal:
                a = [jnp.where(causal, a[hh], 0.0) for hh in hs]
            out = []
            for hh in hs:
                vb = v_ref[pl.ds(koff, TQ), hh * HEAD_V:(hh + 1) * HEAD_V]
                out += [run[hh] + jnp.sum(log_om[hh], axis=1, keepdims=True),
                        acc[hh] + _dot(a[hh].astype(BF16), vb)]
            return tuple(out)

        zero = (jnp.zeros((TQ, 1), F32), jnp.zeros((TQ, HEAD_V), F32))
        carry = step(qi, zero * HEADS, True)
        carry = lax.fori_loop(0, qi, lambda i, c: step(qi - 1 - i, c, False), carry)
        for hh in hs:
            sl = slice(hh * HEAD_V, (hh + 1) * HEAD_V)
            acc = carry[2 * hh + 1]
            z = z_ref[:, sl].astype(F32)
            o_ref[:, sl] = acc.astype(BF16)
            y_ref[:, sl] = (acc * (z * _sigmoid(z))).astype(BF16)
            lt_ref[:, hh:hh + 1] = carry[2 * hh]

    qblk = lambda b, p, i: (b * nq + i, p)
    return pl.pallas_call(
        body, name=name, grid=(T // S, N_HEADS // HEADS, nq),
        in_specs=[pl.BlockSpec((TQ, QK_W), qblk),
                  pl.BlockSpec((S, QK_W), lambda b, p, i: (b, k_b0 + p)),
                  pl.BlockSpec((S, V_W), lambda b, p, i: (b, v_b0 + p)),
                  pl.BlockSpec((TQ, V_W), lambda b, p, i: (b * nq + i, z_b0 + p))],
        out_specs=[pl.BlockSpec((TQ, V_W), qblk),
                   pl.BlockSpec((TQ, V_W), qblk),
                   pl.BlockSpec((None, TQ, HEADS), lambda b, p, i: (p, b * nq + i, 0))],
        out_shape=[jax.ShapeDtypeStruct((T, D_INNER), BF16),
                   jax.ShapeDtypeStruct((T, D_INNER), BF16),
                   jax.ShapeDtypeStruct((N_HEADS // HEADS, T, HEADS), F32)],
        compiler_params=_params(3),
    )(qkvz, qkvz, qkvz, qkvz)


def _attn_gate_bwd(dh, w_out, qkvz, o, name, tm=1024, tn=512):
    T = dh.shape[0]
    gate_b0 = (2 * QK_WIDTH + D_INNER) // tn

    def body(dh_ref, w_ref, z_ref, o_ref, do_ref, dz_ref, dh_s):
        @pl.when(pl.program_id(1) == 0)
        def _():
            dh_s[...] = dh_ref[...].astype(BF16)

        dy = _dot_nt(dh_s[...], w_ref[...])
        z = z_ref[...].astype(F32)
        sig = _sigmoid(z)
        do_ref[...] = (dy * (z * sig)).astype(BF16)
        dz_ref[...] = (dy * o_ref[...].astype(F32) * (sig * (1.0 + z * (1.0 - sig)))).astype(BF16)

    return pl.pallas_call(
        body, name=name, grid=(T // tm, D_INNER // tn),
        in_specs=[pl.BlockSpec((tm, D_MODEL), lambda m, n: (m, 0)),
                  pl.BlockSpec((tn, D_MODEL), lambda m, n: (n, 0)),
                  pl.BlockSpec((tm, tn), lambda m, n: (m, gate_b0 + n)),
                  pl.BlockSpec((tm, tn), lambda m, n: (m, n))],
        out_specs=[pl.BlockSpec((tm, tn), lambda m, n: (m, n)),
                   pl.BlockSpec((tm, tn), lambda m, n: (m, gate_b0 + n))],
        out_shape=[jax.ShapeDtypeStruct((T, D_INNER), BF16),
                   jax.ShapeDtypeStruct((T, 2 * QK_WIDTH + 2 * D_INNER), BF16)],
        scratch_shapes=[pltpu.VMEM((tm, D_MODEL), BF16)],
        compiler_params=_params(2),
    )(dh, w_out, qkvz, o)


def _attn_bwd(qkv, do, ltot, dproj1, S, name):
    T = qkv.shape[0]
    nq = S // TQ
    HEADS, QK_W, V_W = BWD_HEADS, BWD_HEADS * 64, BWD_HEADS * HEAD_V
    k_b0 = QK_WIDTH // QK_W
    v_b0 = 2 * QK_WIDTH // V_W
    hs = range(HEADS)
    pairs = range(HEADS // 2)

    def body(q_ref, k_ref, v_ref, do_ref, lt_ref, _, out_ref, dq_s, dk_s, dv_s, dkb_s, dvb_s, sems):
        b, p = pl.program_id(0), pl.program_id(1)
        row = lax.broadcasted_iota(jnp.int32, (TQ, TQ), 0)
        col = lax.broadcasted_iota(jnp.int32, (TQ, TQ), 1)
        causal = col < row
        upto = (row <= col).astype(BF16)
        before = (row < col).astype(BF16)
        dk_s[...] = jnp.zeros_like(dk_s)
        dv_s[...] = jnp.zeros_like(dv_s)

        def q_block(qi, _):
            qoff = pl.multiple_of(qi * TQ, TQ)
            qms = [qm * 0.125 for qm in _masked_heads(q_ref[pl.ds(qoff, TQ), :], HEADS)]
            vsl = [slice(hh * HEAD_V, (hh + 1) * HEAD_V) for hh in hs]
            psl = [slice(pp * HEAD_PAIR_QK, (pp + 1) * HEAD_PAIR_QK) for pp in pairs]
            do_h = [do_ref[pl.ds(qoff, TQ), sl] for sl in vsl]
            total = [lt_ref[pl.ds(qoff, TQ), hh:hh + 1] for hh in hs]

            def k_block(j, carry, diagonal):
                koff = pl.multiple_of(j * TQ, TQ)
                kms = _masked_heads(k_ref[pl.ds(koff, TQ), :], HEADS)
                g_before = [carry[2 * hh] for hh in hs]
                lom_before = [carry[2 * hh + 1] for hh in hs]
                z = [_dot_nt(qms[hh], kms[hh]) for hh in hs]
                da = [_dot_nt(do_h[hh], v_ref[pl.ds(koff, TQ), vsl[hh]]) for hh in hs]
                terms = [_log_terms(z[hh]) for hh in hs]
                log_om = [jnp.where(causal, t[1], 0.0) if diagonal else t[1] for t in terms]
                prefix = [_split_dot(log_om[hh], upto) for hh in hs]
                a = [jnp.exp(terms[hh][0] + ((total[hh] - lom_before[hh]) - prefix[hh])) for hh in hs]
                if diagonal:
                    a = [jnp.where(causal, a[hh], 0.0) for hh in hs]
                g = [a[hh] * da[hh] for hh in hs]
                g_prefix = [_dot(g[hh].astype(BF16), before) for hh in hs]
                out, dzs = [], []
                for hh in hs:
                    beta = jnp.exp(terms[hh][0])
                    g_excl = (g_before[hh] + g_prefix[hh]) * beta
                    if diagonal:
                        g_excl = jnp.where(causal, g_excl, 0.0)
                    dzs.append((g[hh] * (1.0 - beta) - g_excl).astype(BF16))
                    out += [g_before[hh] + jnp.sum(g[hh], axis=1, keepdims=True),
                            lom_before[hh] + jnp.sum(log_om[hh], axis=1, keepdims=True)]
                for hh in hs:
                    dv_s[pl.ds(koff, TQ), vsl[hh]] += _dot_tn(a[hh].astype(BF16), do_h[hh])
                dq = []
                for pp in pairs:
                    pair = slice(2 * pp, 2 * pp + 2)
                    dq.append(carry[2 * HEADS + pp] + _dot(jnp.concatenate(dzs[pair], axis=1),
                                                           jnp.concatenate(kms[pair], axis=0)))
                    dk_s[pl.ds(koff, TQ), psl[pp]] += _dot_tn(jnp.concatenate(dzs[pair], axis=0),
                                                              jnp.concatenate(qms[pair], axis=0))
                return tuple(out) + tuple(dq)

            zero = jnp.zeros((TQ, 1), F32)
            carry = (zero,) * (2 * HEADS) + (jnp.zeros((TQ, HEAD_PAIR_QK), F32),) * (HEADS // 2)
            carry = lax.fori_loop(0, qi, lambda j, c: k_block(j, c, False), carry)
            carry = k_block(qi, carry, True)
            for pp in pairs:
                dq_s[pl.ds(qoff, TQ), psl[pp]] = (carry[2 * HEADS + pp] * 0.125).astype(BF16)
            return 0

        lax.fori_loop(0, nq, q_block, 0)
        dkb_s[...] = dk_s[...].astype(BF16)
        dvb_s[...] = dv_s[...].astype(BF16)
        rows = pl.ds(pl.multiple_of(b * S, TQ), S)
        copies = [
            pltpu.make_async_copy(
                dq_s, out_ref.at[rows, pl.ds(pl.multiple_of(p * QK_W, 128), QK_W)], sems.at[0]),
            pltpu.make_async_copy(
                dkb_s, out_ref.at[rows, pl.ds(pl.multiple_of(QK_WIDTH + p * QK_W, 128), QK_W)],
                sems.at[1]),
            pltpu.make_async_copy(
                dvb_s, out_ref.at[rows, pl.ds(pl.multiple_of(2 * QK_WIDTH + p * V_W, 128), V_W)],
                sems.at[2]),
        ]
        for cp in copies:
            cp.start()
        for cp in copies:
            cp.wait()

    return pl.pallas_call(
        body, name=name, grid=(T // S, N_HEADS // HEADS),
        in_specs=[pl.BlockSpec((S, QK_W), lambda b, p: (b, p)),
                  pl.BlockSpec((S, QK_W), lambda b, p: (b, k_b0 + p)),
                  pl.BlockSpec((S, V_W), lambda b, p: (b, v_b0 + p)),
                  pl.BlockSpec((S, V_W), lambda b, p: (b, p)),
                  pl.BlockSpec((None, S, HEADS), lambda b, p: (p, b, 0)),
                  HBM_SPEC],
        out_specs=HBM_SPEC,
        out_shape=jax.ShapeDtypeStruct(dproj1.shape, dproj1.dtype),
        input_output_aliases={5: 0},
        scratch_shapes=[pltpu.VMEM((S, QK_W), BF16),
                        pltpu.VMEM((S, QK_W), F32),
                        pltpu.VMEM((S, V_W), F32),
                        pltpu.VMEM((S, QK_W), BF16),
                        pltpu.VMEM((S, V_W), BF16),
                        pltpu.SemaphoreType.DMA((3,))],
        compiler_params=_params(2),
    )(qkv, qkv, qkv, do, ltot, dproj1)


def _loss_head(h, g_row, target, name, tm=512):
    T = h.shape[0]

    def body(h_ref, g_ref, t_ref, dh_ref, dg_ref, loss_ref):
        @pl.when(pl.program_id(0) == 0)
        def _():
            dg_ref[...] = jnp.zeros_like(dg_ref)
            loss_ref[...] = jnp.zeros_like(loss_ref)

        x = h_ref[...]
        inv = lax.rsqrt(jnp.mean(x * x, axis=-1, keepdims=True) + RMS_EPS)
        xhat = x * inv
        gain = g_ref[...]
        err = xhat * gain - t_ref[...]
        per_token = jnp.mean(err * err, axis=-1, keepdims=True)
        loss_ref[...] += 0.5 * jnp.sum(per_token, axis=0, keepdims=True)
        dy = err * (1.0 / D_MODEL)
        dg_ref[...] += jnp.sum(dy * xhat, axis=0, keepdims=True)
        dxh = dy * gain
        proj = jnp.mean(dxh * xhat, axis=-1, keepdims=True)
        dh_ref[...] = inv * (dxh - xhat * proj)

    return pl.pallas_call(
        body, name=name, grid=(T // tm,),
        in_specs=[pl.BlockSpec((tm, D_MODEL), lambda m: (m, 0)),
                  pl.BlockSpec((1, D_MODEL), lambda m: (0, 0)),
                  pl.BlockSpec((tm, D_MODEL), lambda m: (m, 0))],
        out_specs=[pl.BlockSpec((tm, D_MODEL), lambda m: (m, 0)),
                   pl.BlockSpec((1, D_MODEL), lambda m: (0, 0)),
                   pl.BlockSpec((1, 128), lambda m: (0, 0))],
        out_shape=[jax.ShapeDtypeStruct((T, D_MODEL), F32),
                   jax.ShapeDtypeStruct((1, D_MODEL), F32),
                   jax.ShapeDtypeStruct((1, 128), F32)],
        compiler_params=_params(1),
    )(h, g_row, target)


def _place():
    return lax.axis_index("x"), lax.axis_index("y"), lax.axis_index("c")


def _other_chips(x, y):
    return [(1 - x, y), (x, 1 - y), (1 - x, 1 - y)]


def _half(ref, c):
    hr = ref.shape[-2] // 2
    return pl.ds(pl.multiple_of(c * hr, 8), hr)


def _cast_to_slot(shard, chip, name, tr=256):
    R, C = shard.shape

    def body(chip_ref, w_ref, o_ref):
        o_ref[0] = w_ref[...].astype(BF16)

    return pl.pallas_call(
        body, name=name,
        grid_spec=pltpu.PrefetchScalarGridSpec(
            num_scalar_prefetch=1, grid=(R // tr,),
            in_specs=[pl.BlockSpec((tr, C), lambda i, chip_ref: (i, 0))],
            out_specs=pl.BlockSpec((1, tr, C), lambda i, chip_ref: (chip_ref[0], i, 0))),
        out_shape=jax.ShapeDtypeStruct((N_CHIPS, R, C), BF16),
        compiler_params=_params(1),
    )(chip, shard)


def _weight_plan(bufs):
    x, y, c = _place()
    plan = []
    for buf in bufs:
        mine = buf.at[2 * x + y, _half(buf, c)]
        for ox, oy in _other_chips(x, y):
            plan.append((mine, mine, (ox, oy, c), buf.at[2 * ox + oy, _half(buf, c)]))
    return plan


def _chip_sum_plan(bufs):
    x, y, c = _place()
    n = len(bufs) // 2
    plan = []
    for sums, land in zip(bufs[:n], bufs[n:]):
        for k, (ox, oy) in enumerate(_other_chips(x, y)):
            plan.append((sums.at[2 * ox + oy], land.at[k], (ox, oy, c), land.at[k]))
    return plan


def _sibling_plan(bufs):
    x, y, c = _place()
    n = len(bufs) // 2
    return [(p.at[:, _half(p, 1 - c)], land, (x, y, 1 - c), land)
            for p, land in zip(bufs[:n], bufs[n:])]


SEM_SPEC = pl.BlockSpec(memory_space=pltpu.SEMAPHORE)
ANY_SPEC = pl.BlockSpec(memory_space=pl.ANY)
DATAFLOW = pltpu.SideEffectType.DATAFLOW_SIDE_EFFECTING


def _in_hbm(a):
    return pltpu.with_memory_space_constraint(a, pltpu.HBM)


def _exchange_start(bufs, after, plan, n_copies, name):
    nb = len(bufs)

    def body(*refs):
        send_sems, recv_sems = refs[nb + 1], refs[nb + 2]
        for i, (src, dst, dev, _) in enumerate(plan(refs[:nb])):
            pltpu.make_async_remote_copy(
                src_ref=src, dst_ref=dst, send_sem=send_sems.at[i], recv_sem=recv_sems.at[i],
                device_id=dev, device_id_type=MESH).start()
        token = refs[-1]
        token[...] = jnp.zeros_like(token)

    res = pl.pallas_call(
        body, name=name,
        in_specs=[HBM_SPEC] * nb + [ANY_SPEC],
        out_specs=[SEM_SPEC, SEM_SPEC] + [HBM_SPEC] * nb + [pl.BlockSpec(memory_space=pltpu.VMEM)],
        out_shape=[pltpu.SemaphoreType.DMA((n_copies,)), pltpu.SemaphoreType.DMA((n_copies,))]
        + [pltpu.HBM(b.shape, b.dtype) for b in bufs] + [jax.ShapeDtypeStruct((8, 128), F32)],
        input_output_aliases={i: 2 + i for i in range(nb)},
        compiler_params=pltpu.CompilerParams(has_side_effects=DATAFLOW),
    )(*[_in_hbm(b) for b in bufs], after)
    return res[0], res[1], list(res[2:2 + nb]), res[-1]


def _exchange_wait(bufs, send_sems, recv_sems, after, plan, name):
    nb = len(bufs)

    def body(*refs):
        sends, recvs = refs[nb], refs[nb + 1]
        for i, (src, dst, dev, landing) in enumerate(plan(refs[:nb])):
            pltpu.make_async_remote_copy(
                src_ref=src, dst_ref=landing, send_sem=sends.at[i], recv_sem=recvs.at[i],
                device_id=dev, device_id_type=MESH).wait()

    res = pl.pallas_call(
        body, name=name,
        in_specs=[HBM_SPEC] * nb + [SEM_SPEC, SEM_SPEC, ANY_SPEC],
        out_specs=[HBM_SPEC] * nb,
        out_shape=[pltpu.HBM(b.shape, b.dtype) for b in bufs],
        input_output_aliases={i: i for i in range(nb)},
        compiler_params=pltpu.CompilerParams(has_side_effects=DATAFLOW),
    )(*bufs, send_sems, recv_sems, after)
    return list(res)


def _allgather_weights(slots, name, landed=False):
    n = len(slots)

    def body(*refs):
        outs = refs[n:2 * n]
        send_sems, recv_sems, fwd_send, fwd_recv = refs[2 * n:]
        x, y, c = _place()
        chips = _other_chips(x, y)

        def landing(a, chip, half_of):
            return outs[a].at[2 * chip[0] + chip[1], _half(outs[a], half_of)]

        def ici(a, k, chip_from, to):
            return pltpu.make_async_remote_copy(
                src_ref=landing(a, chip_from, c), dst_ref=landing(a, chip_from, c),
                send_sem=send_sems.at[a, k], recv_sem=recv_sems.at[a, k],
                device_id=to, device_id_type=MESH)

        def d2d(a, k, chip_from, half_of):
            return pltpu.make_async_remote_copy(
                src_ref=landing(a, chip_from, half_of), dst_ref=landing(a, chip_from, half_of),
                send_sem=fwd_send.at[a, k], recv_sem=fwd_recv.at[a, k],
                device_id=(x, y, 1 - c), device_id_type=MESH)

        sends = []
        if not landed:
            sends = [ici(a, k, (x, y), (*chips[k], c)) for a in range(n) for k in range(3)]
        for cp in sends:
            cp.start()
        forwards = []
        for a in range(n):
            for k in range(3):
                if not landed:
                    ici(a, k, chips[k], (x, y, c)).wait_recv()
                fw = d2d(a, k, chips[k], c)
                fw.start()
                forwards.append(fw)
        for a in range(n):
            for k in range(3):
                d2d(a, k, chips[k], 1 - c).wait_recv()
        for cp in sends + forwards:
            cp.wait_send()

    return pl.pallas_call(
        body, name=name,
        in_specs=[HBM_SPEC] * n, out_specs=[HBM_SPEC] * n,
        out_shape=[jax.ShapeDtypeStruct(s.shape, s.dtype) for s in slots],
        input_output_aliases={a: a for a in range(n)},
        scratch_shapes=[pltpu.SemaphoreType.DMA((n, 3)), pltpu.SemaphoreType.DMA((n, 3)),
                        pltpu.SemaphoreType.DMA((n, 3)), pltpu.SemaphoreType.DMA((n, 3))],
    )(*slots)


def _sibling_exchange(partials, small, name):
    n = len(partials)
    ns = 0 if small is None else 1

    def body(*refs):
        ins, outs = refs[:n], refs[n + ns:2 * n + ns]
        send_sems, recv_sems = refs[2 * (n + ns):2 * (n + ns) + 2]
        x, y, c = _place()
        me = 4 * x + 2 * y + c
        sends = [pltpu.make_async_remote_copy(
            src_ref=ins[a].at[:, _half(ins[a], 1 - c)], dst_ref=outs[a],
            send_sem=send_sems.at[a], recv_sem=recv_sems.at[a],
            device_id=(x, y, 1 - c), device_id_type=MESH) for a in range(n)]
        if ns:
            small_ref, small_all = refs[n], refs[2 * n + 1]
            s_send, s_recv, loc_sem = refs[2 * (n + ns) + 2:]
            local = pltpu.make_async_copy(small_ref, small_all.at[me], loc_sem)
            local.start()
            for d in range(1, N_DEV):
                px, py, pc = x ^ ((d >> 2) & 1), y ^ ((d >> 1) & 1), c ^ (d & 1)
                sends.append(pltpu.make_async_remote_copy(
                    src_ref=small_ref, dst_ref=small_all.at[me],
                    send_sem=s_send.at[d - 1], recv_sem=s_recv.at[d - 1],
                    device_id=(px, py, pc), device_id_type=MESH))
        for cp in sends:
            cp.start()
        if ns:
            for d in range(1, N_DEV):
                pltpu.make_async_remote_copy(
                    src_ref=small_ref, dst_ref=small_all.at[me ^ d],
                    send_sem=s_send.at[d - 1], recv_sem=s_recv.at[d - 1],
                    device_id=(x, y, c), device_id_type=MESH).wait_recv()
        for cp in sends[:n]:
            cp.wait_recv()
        for cp in sends:
            cp.wait_send()
        if ns:
            local.wait()

    out_shape = [jax.ShapeDtypeStruct((N_CHIPS, p.shape[1] // 2, p.shape[2]), F32) for p in partials]
    scratch = [pltpu.SemaphoreType.DMA((max(n, 1),)), pltpu.SemaphoreType.DMA((max(n, 1),))]
    if ns:
        out_shape.append(jax.ShapeDtypeStruct((N_DEV,) + small.shape, F32))
        scratch += [pltpu.SemaphoreType.DMA((N_DEV - 1,)), pltpu.SemaphoreType.DMA((N_DEV - 1,)),
                    pltpu.SemaphoreType.DMA]
    return pl.pallas_call(
        body, name=name,
        in_specs=[HBM_SPEC] * (n + ns), out_specs=[HBM_SPEC] * (n + ns),
        out_shape=out_shape, scratch_shapes=scratch,
    )(*partials, *([small] if ns else []))


def _chip_sum(partial, from_sibling, c, name, tr=256):
    _, hr, C = from_sibling.shape
    nb = hr // tr

    def body(c_ref, p_ref, s_ref, o_ref):
        o_ref[...] = (p_ref[...] + s_ref[...]).astype(BF16)

    return pl.pallas_call(
        body, name=name,
        grid_spec=pltpu.PrefetchScalarGridSpec(
            num_scalar_prefetch=1, grid=(N_CHIPS, nb),
            in_specs=[pl.BlockSpec((1, tr, C), lambda j, i, c_ref: (j, c_ref[0] * nb + i, 0)),
                      pl.BlockSpec((1, tr, C), lambda j, i, c_ref: (j, i, 0))],
            out_specs=pl.BlockSpec((1, tr, C), lambda j, i, c_ref: (j, i, 0))),
        out_shape=jax.ShapeDtypeStruct(from_sibling.shape, BF16),
        compiler_params=_params(2),
    )(c, partial, from_sibling)


def _reduce_half(partial, from_sibling, received, place, name, tr=256):
    _, hr, C = from_sibling.shape
    nb = hr // tr

    def body(p_ref, mine_ref, sib_ref, r_ref, o_ref):
        acc = mine_ref[0] + sib_ref[0]
        for k in range(3):
            acc = acc + r_ref[k].astype(F32)
        o_ref[...] = acc

    return pl.pallas_call(
        body, name=name,
        grid_spec=pltpu.PrefetchScalarGridSpec(
            num_scalar_prefetch=1, grid=(nb,),
            in_specs=[pl.BlockSpec((1, tr, C), lambda i, p: (p[0], p[1] * nb + i, 0)),
                      pl.BlockSpec((1, tr, C), lambda i, p: (p[0], i, 0)),
                      pl.BlockSpec((3, tr, C), lambda i, p: (0, i, 0))],
            out_specs=pl.BlockSpec((tr, C), lambda i, p: (p[1] * nb + i, 0))),
        out_shape=jax.ShapeDtypeStruct((2 * hr, C), F32),
        compiler_params=_params(1),
    )(place, partial, from_sibling, received)


def _join_halves(fulls, name):
    n = len(fulls)

    def body(*refs):
        outs = refs[n:2 * n]
        send_sems, recv_sems = refs[2 * n:]
        x, y, c = _place()

        def copy(a, half_of, to):
            rows = outs[a].at[_half(outs[a], half_of)]
            return pltpu.make_async_remote_copy(
                src_ref=rows, dst_ref=rows, send_sem=send_sems.at[a], recv_sem=recv_sems.at[a],
                device_id=to, device_id_type=MESH)

        sends = [copy(a, c, (x, y, 1 - c)) for a in range(n)]
        for cp in sends:
            cp.start()
        for a in range(n):
            copy(a, 1 - c, (x, y, c)).wait_recv()
        for cp in sends:
            cp.wait_send()

    return pl.pallas_call(
        body, name=name,
        in_specs=[HBM_SPEC] * n, out_specs=[HBM_SPEC] * n,
        out_shape=[jax.ShapeDtypeStruct(f.shape, F32) for f in fulls],
        input_output_aliases={a: a for a in range(n)},
        scratch_shapes=[pltpu.SemaphoreType.DMA((n,)), pltpu.SemaphoreType.DMA((n,))],
    )(*fulls)


def _adamw_math(w, g, m, v):
    m = ADAM_B1 * m + (1.0 - ADAM_B1) * g
    v = ADAM_B2 * v + (1.0 - ADAM_B2) * (g * g)
    m_hat = m / (1.0 - ADAM_B1 ** ADAM_STEP)
    v_hat = v / (1.0 - ADAM_B2 ** ADAM_STEP)
    delta = -ADAM_LR * (m_hat / (jnp.sqrt(v_hat) + ADAM_EPS) + ADAM_WD * w)
    return delta, m, v


def _adamw(w, g, m, v, name, tr=256):
    R, C = w.shape
    tr = min(tr, R)

    def body(w_ref, g_ref, m_ref, v_ref, d_out, m_out, v_out):
        d_out[...], m_out[...], v_out[...] = _adamw_math(w_ref[...], g_ref[...], m_ref[...], v_ref[...])

    spec = pl.BlockSpec((tr, C), lambda i: (i, 0))
    return pl.pallas_call(
        body, name=name, grid=(R // tr,),
        in_specs=[spec] * 4, out_specs=[spec] * 3,
        out_shape=[jax.ShapeDtypeStruct((R, C), F32)] * 3,
        compiler_params=_params(1),
    )(w, g, m, v)


def _adamw_small(small_all, w, m, v, name):
    def body(s_ref, w_ref, m_ref, v_ref, g_out, d_out, m_out, v_out):
        g = s_ref[0]
        for d in range(1, N_DEV):
            g = g + s_ref[d]
        g_out[...] = g
        d_out[...], m_out[...], v_out[...] = _adamw_math(w_ref[...], g, m_ref[...], v_ref[...])

    vm = pl.BlockSpec(memory_space=pltpu.VMEM)
    return pl.pallas_call(
        body, name=name, in_specs=[vm] * 4, out_specs=[vm] * 4,
        out_shape=[jax.ShapeDtypeStruct(w.shape, F32)] * 4,
    )(small_all, w, m, v)


def _pack_small(norm_g, pool_scale, norm_f, extra_row):
    return jnp.concatenate([norm_g.reshape(2, D_MODEL), pool_scale.reshape(2, D_MODEL),
                            norm_f.reshape(1, D_MODEL), extra_row,
                            jnp.zeros((2, D_MODEL), F32)], axis=0)


def kernel(x, norm_g, pool_w_in, pool_w, pool_scale, pool_w_out, sb_w_in, sb_w_out, norm_f, loss_target, m_norm_g, m_pool_w_in, m_pool_w, m_pool_scale, m_pool_w_out, m_sb_w_in, m_sb_w_out, m_norm_f, v_norm_g, v_pool_w_in, v_pool_w, v_pool_scale, v_pool_w_out, v_sb_w_in, v_sb_w_out, v_norm_f):
    nb, S, _ = x.shape
    T = nb * S
    xt = x.reshape(T, D_MODEL)
    target = loss_target.reshape(T, D_MODEL)
    cx, cy, cc = _place()

    def shard2d(w):
        return w.reshape(-1, w.shape[-1])

    names = ("pool_w_in", "pool_w", "pool_w_out", "sb_w_in", "sb_w_out")
    w_shards = [shard2d(w) for w in (pool_w_in, pool_w, pool_w_out, sb_w_in, sb_w_out)]
    m_shards = [shard2d(w) for w in (m_pool_w_in, m_pool_w, m_pool_w_out, m_sb_w_in, m_sb_w_out)]
    v_shards = [shard2d(w) for w in (v_pool_w_in, v_pool_w, v_pool_w_out, v_sb_w_in, v_sb_w_out)]

    chip = (2 * cx + cy).reshape(1).astype(jnp.int32)
    c_arr = cc.reshape(1).astype(jnp.int32)
    place = jnp.stack([2 * cx + cy, cc]).astype(jnp.int32)
    slots = [_cast_to_slot(w, chip, "cast_" + nm) for w, nm in zip(w_shards, names)]
    g0, g1, gf = norm_g[0:1], norm_g[1:2], norm_f.reshape(1, D_MODEL)

    w_pin, w_g, w_pout = _allgather_weights(slots[:3], "allgather_pool_weights")
    w_pout = w_pout.reshape(D_INNER, D_MODEL)
    sb_send, sb_recv, sb_slots, token = _exchange_start(slots[3:], w_pin, _weight_plan, 6,
                                                        "sb_weights_start")

    proj0, u0 = _rms_matmul(xt, g0 + token[0:1, 0:1], w_pin, [(2 * D_INNER, BF16)], "pool_in_proj",
                            tn=w_pin.shape[2])
    y0, pooled, mixed = _pool_fwd(proj0, w_g, pool_scale, S, "pool_mix")
    sb_slots = _exchange_wait(sb_slots, sb_send, sb_recv, y0, _weight_plan, "sb_weights_wait")
    w_sin, w_sout = _allgather_weights(sb_slots, "sb_weights_forward", landed=True)
    w_sout = w_sout.reshape(D_INNER, D_MODEL)
    h1 = _matmul_residual(y0, w_pout, xt, "pool_out_proj")
    n1 = 2 * QK_WIDTH + 2 * D_INNER
    qkvz, u1 = _rms_matmul(h1, g1, w_sin, [(n1, BF16)], "sb_in_proj", tn=w_sin.shape[2])
    o, y1, ltot = _attn_fwd(qkvz, S, "sb_attention")
    h2 = _matmul_residual(y1, w_sout, h1, "sb_out_proj")
    dh2, d_norm_f, loss_row = _loss_head(h2, gf, target, "loss_head")

    shard = lambda i, j, t: (j, 0, 0)
    gw_sout = _matmul_tn(y1, dh2, D_INNER, D_MODEL, (D_INNER, D_MODEL), (1024, 1024),
                         lambda i, j, t: (i, j), "grad_sb_w_out", bm=1024, bn=1024)
    do, dproj1 = _attn_gate_bwd(dh2, w_sout, qkvz, o, "sb_gate_bwd")
    dproj1 = _attn_bwd(qkvz, do, ltot, dproj1, S, "sb_attention_bwd")
    gw_sin = _matmul_tn(u1, dproj1, D_MODEL, n1, (N_CHIPS, D_MODEL, n1 // 4), (1, D_MODEL, n1 // 4),
                        shard, "grad_sb_w_in", bm=D_MODEL, bn=n1 // 4)

    def reduce_start(partials, tag, work=None):
        n, done = len(partials), None
        if work is None:
            from_sibling = _sibling_exchange(partials, None, "grad_sibling_exchange_" + tag)
        else:
            lands = [lax.empty((N_CHIPS, p.shape[1] // 2, p.shape[2]), F32) for p in partials]
            send, recv, bufs, token = _exchange_start(partials + lands, c_arr, _sibling_plan, n,
                                                      "grad_sibling_start_" + tag)
            done = work(token[0:1, 0:1])
            bufs = _exchange_wait(bufs, send, recv, done[0], _sibling_plan, "grad_sibling_wait_" + tag)
            partials, from_sibling = bufs[:n], bufs[n:]
        sums = [_chip_sum(p, s, c_arr, "grad_chip_sum_%s_%d" % (tag, i))
                for i, (p, s) in enumerate(zip(partials, from_sibling))]
        lands = [lax.empty((3,) + s.shape[1:], BF16) for s in sums]
        send, recv, bufs, token = _exchange_start(sums + lands, c_arr, _chip_sum_plan, 3 * len(sums),
                                                  "grad_chip_exchange_start_" + tag)
        return (partials, list(from_sibling), send, recv, bufs), token[0:1, 0:1], done

    def reduce_finish(started, after, tag):
        partials, from_sibling, send, recv, bufs = started
        received = _exchange_wait(bufs, send, recv, after, _chip_sum_plan,
                                  "grad_chip_exchange_wait_" + tag)[len(partials):]
        return [_reduce_half(p, s, r, place, "grad_reduce_%s_%d" % (tag, i))
                for i, (p, s, r) in enumerate(zip(partials, from_sibling, received))]

    sb_started, _, (dh1, d_g1) = reduce_start(
        [gw_sin, gw_sout.reshape(N_CHIPS, -1, D_MODEL)], "sb",
        lambda token: _matmul_nt_rms_bwd(dproj1, w_sin, h1, g1 + token, dh2, "sb_in_bwd",
                                         tk=w_sin.shape[2]))
    gw_pout = _matmul_tn(y0, dh1, D_INNER, D_MODEL, (D_INNER, D_MODEL), (1024, 1024),
                         lambda i, j, t: (i, j), "grad_pool_w_out", bm=1024, bn=1024)
    dmixed, dproj0, d_scale = _pool_gate_bwd(dh1, w_pout, proj0, mixed, pool_scale, "pool_gate_bwd")
    gw_g = _matmul_tn(pooled, dmixed, D_INNER, D_INNER, (N_CHIPS, GROUP_DIM, GROUP_DIM),
                      (N_CHIPS, GROUP_DIM // N_CHIPS, GROUP_DIM), lambda i, j, t: (0, i, 0),
                      "grad_pool_w", bm=GROUP_DIM, bn=GROUP_DIM, diagonal_blocks=True)
    mix_started, _, (dproj0,) = reduce_start(
        [gw_g, gw_pout.reshape(N_CHIPS, -1, D_MODEL)], "pool_mix",
        lambda token: (_pool_bwd(dmixed, w_g, dproj0, S, "pool_bwd"),))
    n0 = 2 * D_INNER
    gw_pin = _matmul_tn(u0, dproj0, D_MODEL, n0, (N_CHIPS, D_MODEL, n0 // 4), (1, D_MODEL, n0 // 4),
                        shard, "grad_pool_w_in", bm=D_MODEL, bn=n0 // 4)
    pin_started, token, _ = reduce_start([gw_pin], "pool_in")
    dx, d_g0 = _matmul_nt_rms_bwd(dproj0, w_pin, xt, g0 + token, dh1, "pool_in_bwd", tk=w_pin.shape[2])

    small = _pack_small(jnp.concatenate([d_g0, d_g1], axis=0), d_scale, d_norm_f,
                        jnp.broadcast_to(loss_row[:, :1], (1, D_MODEL)))
    small_all, = _sibling_exchange([], small, "small_sums_exchange")
    grads = _join_halves(reduce_finish(pin_started, dx, "pool_in")
                         + reduce_finish(mix_started, dx, "pool_mix")
                         + reduce_finish(sb_started, dx, "sb"), "grad_join_halves")

    deltas, new_m, new_v = [], [], []
    for w, g, m, v, nm in zip(w_shards, grads, m_shards, v_shards, names):
        d, mm, vv = _adamw(w, g, m, v, "adamw_" + nm)
        deltas.append(d)
        new_m.append(mm)
        new_v.append(vv)

    zero_row = jnp.zeros((1, D_MODEL), F32)
    g_small, d_small, m_small, v_small = _adamw_small(
        small_all, _pack_small(norm_g, pool_scale, norm_f, zero_row),
        _pack_small(m_norm_g, m_pool_scale, m_norm_f, zero_row),
        _pack_small(v_norm_g, v_pool_scale, v_norm_f, zero_row + 1.0), "adamw_small")
    loss = g_small[5, 0]

    def unpack_small(a):
        return a[0:2], a[2:4].reshape(1, D_INNER), a[4]

    def assemble(big, small3):
        ng, ps, nf = small3
        return [ng, big[0].reshape(pool_w_in.shape), big[1].reshape(pool_w.shape), ps,
                big[2].reshape(pool_w_out.shape), big[3].reshape(sb_w_in.shape),
                big[4].reshape(sb_w_out.shape), nf]

    return (loss, dx.reshape(x.shape),
            *assemble(grads, unpack_small(g_small)),
            *assemble(deltas, unpack_small(d_small)),
            *assemble(new_m, unpack_small(m_small)),
            *assemble(new_v, unpack_small(v_small)))
```

```python
import functools

import jax
import jax.numpy as jnp
from jax import lax
from jax.experimental import pallas as pl
from jax.experimental.pallas import tpu as pltpu

F32 = jnp.float32
BF16 = jnp.bfloat16
MESH = pl.DeviceIdType.MESH

D_MODEL = 1024
D_INNER = 2048
N_GROUPS = 4
GROUP_DIM = 512
HEAD_PAIR_QK = 128
HEAD_V = 128
N_HEAD_PAIRS = 8
QK_WIDTH = 1024
RMS_EPS = 1e-6
HALO = 16
N_CHIPS = 4
N_DEV = 8

ADAM_LR = 0.001
ADAM_B1 = 0.9
ADAM_B2 = 0.999
ADAM_EPS = 1e-08
ADAM_WD = 0.01
ADAM_STEP = 10

VMEM_LIMIT = 56 * 1024 * 1024

HBM_SPEC = pl.BlockSpec(memory_space=pltpu.HBM)


def _params(n_axes):
    return pltpu.CompilerParams(dimension_semantics=("arbitrary",) * n_axes,
                                vmem_limit_bytes=VMEM_LIMIT)


def _dot(a, b):
    return jnp.dot(a, b, preferred_element_type=F32)


def _dot_nt(a, b):
    return lax.dot_general(a, b, (((1,), (1,)), ((), ())), preferred_element_type=F32)


def _dot_tn(a, b):
    return lax.dot_general(a, b, (((0,), (0,)), ((), ())), preferred_element_type=F32)


def _sigmoid(z):
    return 1.0 / (1.0 + jnp.exp(-z))


def _rms_matmul(h, g_row, w4, outs, name, tm=1024, tn=512):
    T = h.shape[0]
    per_shard = w4.shape[2] // tn
    starts = [0]
    for width, _ in outs:
        starts.append(starts[-1] + width // tn)
    n_out = len(outs)

    def body(h_ref, g_ref, w_ref, *rest):
        o_refs, u_out, u_s = rest[:n_out], rest[n_out], rest[n_out + 1]
        n = pl.program_id(1)

        @pl.when(n == 0)
        def _():
            x = h_ref[...]
            inv = lax.rsqrt(jnp.mean(x * x, axis=-1, keepdims=True) + RMS_EPS)
            u = (x * inv * g_ref[...]).astype(BF16)
            u_s[...] = u
            u_out[...] = u

        res = _dot(u_s[...], w_ref[0])
        for k in range(n_out):
            @pl.when((n >= starts[k]) & (n < starts[k + 1]))
            def _():
                o_refs[k][...] = res.astype(o_refs[k].dtype)

    def out_map(k):
        return lambda m, n: (m, jnp.clip(n - starts[k], 0, starts[k + 1] - starts[k] - 1))

    return pl.pallas_call(
        body, name=name, grid=(T // tm, starts[-1]),
        in_specs=[pl.BlockSpec((tm, D_MODEL), lambda m, n: (m, 0)),
                  pl.BlockSpec((1, D_MODEL), lambda m, n: (0, 0)),
                  pl.BlockSpec((1, D_MODEL, tn), lambda m, n: (n // per_shard, 0, n % per_shard))],
        out_specs=[pl.BlockSpec((tm, tn), out_map(k)) for k in range(n_out)]
        + [pl.BlockSpec((tm, D_MODEL), lambda m, n: (m, 0))],
        out_shape=[jax.ShapeDtypeStruct((T, width), dt) for width, dt in outs]
        + [jax.ShapeDtypeStruct((T, D_MODEL), BF16)],
        scratch_shapes=[pltpu.VMEM((tm, D_MODEL), BF16)],
        compiler_params=_params(2),
    )(h, g_row, w4)


def _matmul_residual(a, w, res, name, tm=1024, tn=512):
    T, K = a.shape
    N = w.shape[1]

    def body(a_ref, w_ref, r_ref, o_ref):
        o_ref[...] = r_ref[...] + _dot(a_ref[...], w_ref[...])

    return pl.pallas_call(
        body, name=name, grid=(T // tm, N // tn),
        in_specs=[pl.BlockSpec((tm, K), lambda m, n: (m, 0)),
                  pl.BlockSpec((K, tn), lambda m, n: (0, n)),
                  pl.BlockSpec((tm, tn), lambda m, n: (m, n))],
        out_specs=pl.BlockSpec((tm, tn), lambda m, n: (m, n)),
        out_shape=jax.ShapeDtypeStruct((T, N), F32),
        compiler_params=_params(2),
    )(a, w, res)


def _matmul_tn(a, b, a_cols, b_cols, out_shape, out_block, out_map, name, bm, bn, tk=512,
               diagonal_blocks=False):
    T = a.shape[0]

    def body(a_ref, b_ref, o_ref):
        @pl.when(pl.program_id(2) == 0)
        def _():
            o_ref[...] = jnp.zeros_like(o_ref)

        part = _dot_tn(a_ref[...].astype(BF16), b_ref[...].astype(BF16))
        o_ref[...] += part.reshape(o_ref.shape)

    b_map = (lambda i, j, t: (t, i)) if diagonal_blocks else (lambda i, j, t: (t, j))
    return pl.pallas_call(
        body, name=name, grid=(a_cols // bm, 1 if diagonal_blocks else b_cols // bn, T // tk),
        in_specs=[pl.BlockSpec((tk, bm), lambda i, j, t: (t, i)),
                  pl.BlockSpec((tk, bn), b_map)],
        out_specs=pl.BlockSpec(out_block, out_map),
        out_shape=jax.ShapeDtypeStruct(out_shape, F32),
        compiler_params=_params(3),
    )(a, b)


def _matmul_nt_rms_bwd(dproj, w4, h, g_row, dres, name, tm=1024, tk=512):
    T, cols = dproj.shape
    per_shard = w4.shape[2] // tk
    nk = cols // tk

    def body(dp_ref, w_ref, h_ref, g_ref, r_ref, dx_ref, dg_ref, acc):
        m, k = pl.program_id(0), pl.program_id(1)

        @pl.when(k == 0)
        def _():
            acc[...] = jnp.zeros_like(acc)

        @pl.when((k == 0) & (m == 0))
        def _():
            dg_ref[...] = jnp.zeros_like(dg_ref)

        acc[...] += _dot_nt(dp_ref[...], w_ref[0])

        @pl.when(k == nk - 1)
        def _():
            du = acc[...]
            x = h_ref[...]
            inv = lax.rsqrt(jnp.mean(x * x, axis=-1, keepdims=True) + RMS_EPS)
            xhat = x * inv
            dg_ref[...] += jnp.sum(du * xhat, axis=0, keepdims=True)
            dxh = du * g_ref[...]
            proj = jnp.mean(dxh * xhat, axis=-1, keepdims=True)
            dx_ref[...] = r_ref[...] + inv * (dxh - xhat * proj)

    return pl.pallas_call(
        body, name=name, grid=(T // tm, nk),
        in_specs=[pl.BlockSpec((tm, tk), lambda m, k: (m, k)),
                  pl.BlockSpec((1, D_MODEL, tk), lambda m, k: (k // per_shard, 0, k % per_shard)),
                  pl.BlockSpec((tm, D_MODEL), lambda m, k: (m, 0)),
                  pl.BlockSpec((1, D_MODEL), lambda m, k: (0, 0)),
                  pl.BlockSpec((tm, D_MODEL), lambda m, k: (m, 0))],
        out_specs=[pl.BlockSpec((tm, D_MODEL), lambda m, k: (m, 0)),
                   pl.BlockSpec((1, D_MODEL), lambda m, k: (0, 0))],
        out_shape=[jax.ShapeDtypeStruct((T, D_MODEL), F32),
                   jax.ShapeDtypeStruct((1, D_MODEL), F32)],
        scratch_shapes=[pltpu.VMEM((tm, D_MODEL), F32)],
        compiler_params=_params(2),
    )(dproj, w4, h, g_row, dres)


def _window_of(g):
    return jnp.left_shift(2, g)


def _select_stage(g, stages):
    res = stages[0]
    for i in range(1, len(stages)):
        res = jnp.where(g >= i, stages[i], res)
    return res


def _pool_fwd(proj0, wg4, scale_row, S, name, tm=256):
    T = proj0.shape[0]
    blocks_per_seq = S // tm
    hb = tm // HALO

    def body(x_ref, halo_ref, z_ref, w_ref, s_ref, y_ref, p_ref, mix_ref):
        m, g = pl.program_id(0), pl.program_id(1)
        first = (m % blocks_per_seq) == 0
        halo = jnp.where(first, 0.0, halo_ref[...].astype(F32))
        x = x_ref[...].astype(F32)
        ext = jnp.concatenate([halo, x], axis=0)
        stages = []
        cur = ext
        for sh in (1, 2, 4, 8):
            cur = cur + pltpu.roll(cur, sh, 0)
            stages.append(cur[HALO:, :])
        win_sum = _select_stage(g, stages)
        pos = (m % blocks_per_seq) * tm + lax.broadcasted_iota(jnp.int32, (tm, 1), 0)
        count = jnp.minimum(pos + 1, _window_of(g)).astype(F32)
        pooled = win_sum / count - x
        pooled_b = pooled.astype(BF16)
        mixed = _dot(pooled_b, w_ref[...].reshape(GROUP_DIM, GROUP_DIM))
        z = z_ref[...].astype(F32)
        y_ref[...] = (mixed * s_ref[...] * (z * _sigmoid(z))).astype(BF16)
        p_ref[...] = pooled_b
        mix_ref[...] = mixed.astype(BF16)

    blk = lambda m, g: (m, g)
    return pl.pallas_call(
        body, name=name, grid=(T // tm, N_GROUPS),
        in_specs=[pl.BlockSpec((tm, GROUP_DIM), blk),
                  pl.BlockSpec((HALO, GROUP_DIM), lambda m, g: (jnp.maximum(m * hb - 1, 0), g)),
                  pl.BlockSpec((tm, GROUP_DIM), lambda m, g: (m, N_GROUPS + g)),
                  pl.BlockSpec((N_CHIPS, GROUP_DIM // N_CHIPS, GROUP_DIM), lambda m, g: (0, g, 0)),
                  pl.BlockSpec((1, GROUP_DIM), lambda m, g: (0, g))],
        out_specs=[pl.BlockSpec((tm, GROUP_DIM), blk)] * 3,
        out_shape=[jax.ShapeDtypeStruct((T, D_INNER), BF16)] * 3,
        compiler_params=_params(2),
    )(proj0, proj0, proj0, wg4, scale_row)


def _pool_gate_bwd(dh, w_out, proj0, mixed, scale_row, name, tm=1024, tn=512):
    T = dh.shape[0]
    gate_b0 = D_INNER // tn

    def body(dh_ref, w_ref, z_ref, mix_ref, s_ref, dm_ref, dz_ref, ds_ref):
        @pl.when(pl.program_id(1) == 0)
        def _():
            ds_ref[...] = jnp.zeros_like(ds_ref)

        dy = _dot_nt(dh_ref[...].astype(BF16), w_ref[...])
        z = z_ref[...].astype(F32)
        sig = _sigmoid(z)
        silu = z * sig
        mixed = mix_ref[...].astype(F32)
        s = s_ref[...]
        dm_ref[...] = (dy * s * silu).astype(BF16)
        dz_ref[...] = (dy * mixed * s * (sig * (1.0 + z * (1.0 - sig)))).astype(BF16)
        ds_ref[...] += jnp.sum(dy * mixed * silu, axis=0, keepdims=True)

    return pl.pallas_call(
        body, name=name, grid=(D_INNER // tn, T // tm),
        in_specs=[pl.BlockSpec((tm, D_MODEL), lambda n, m: (m, 0)),
                  pl.BlockSpec((tn, D_MODEL), lambda n, m: (n, 0)),
                  pl.BlockSpec((tm, tn), lambda n, m: (m, gate_b0 + n)),
                  pl.BlockSpec((tm, tn), lambda n, m: (m, n)),
                  pl.BlockSpec((1, tn), lambda n, m: (0, n))],
        out_specs=[pl.BlockSpec((tm, tn), lambda n, m: (m, n)),
                   pl.BlockSpec((tm, tn), lambda n, m: (m, gate_b0 + n)),
                   pl.BlockSpec((1, tn), lambda n, m: (0, n))],
        out_shape=[jax.ShapeDtypeStruct((T, D_INNER), BF16),
                   jax.ShapeDtypeStruct((T, 2 * D_INNER), BF16),
                   jax.ShapeDtypeStruct((1, D_INNER), F32)],
        compiler_params=_params(2),
    )(dh, w_out, proj0, mixed, scale_row)


def _pool_bwd(dmixed, wg4, dproj0, S, name, tm=256):
    T = dmixed.shape[0]
    blocks_per_seq = S // tm
    hb = tm // HALO
    n_halo_blocks = T // HALO

    def body(dm_ref, halo_ref, w_ref, _, o_ref):
        m, g = pl.program_id(0), pl.program_id(1)
        ext = jnp.concatenate([dm_ref[...], halo_ref[...]], axis=0)
        dp = _dot_nt(ext, w_ref[...].reshape(GROUP_DIM, GROUP_DIM))
        pos = (m % blocks_per_seq) * tm + lax.broadcasted_iota(jnp.int32, (tm + HALO, 1), 0)
        count = jnp.minimum(pos + 1, _window_of(g)).astype(F32)
        c = jnp.where(pos < S, dp / count, 0.0)
        n = tm + HALO
        stages = []
        cur = c
        for sh in (1, 2, 4, 8):
            cur = cur + pltpu.roll(cur, n - sh, 0)
            stages.append(cur[:tm, :])
        o_ref[...] = (_select_stage(g, stages) - dp[:tm, :]).astype(BF16)

    blk = lambda m, g: (m, g)
    return pl.pallas_call(
        body, name=name, grid=(T // tm, N_GROUPS),
        in_specs=[pl.BlockSpec((tm, GROUP_DIM), blk),
                  pl.BlockSpec((HALO, GROUP_DIM),
                               lambda m, g: (jnp.minimum((m + 1) * hb, n_halo_blocks - 1), g)),
                  pl.BlockSpec((N_CHIPS, GROUP_DIM // N_CHIPS, GROUP_DIM), lambda m, g: (0, g, 0)),
                  HBM_SPEC],
        out_specs=pl.BlockSpec((tm, GROUP_DIM), blk),
        out_shape=jax.ShapeDtypeStruct(dproj0.shape, dproj0.dtype),
        input_output_aliases={3: 0},
        compiler_params=_params(2),
    )(dmixed, dmixed, wg4, dproj0)


TQ = 256


def _split_dot(x, m):
    hi = x.astype(BF16)
    lo = (x - hi.astype(F32)).astype(BF16)
    return _dot(hi, m) + _dot(lo, m)


NEG_LOG2E = -1.4426950408889634


def _log_terms(z):
    soft = jnp.log(1.0 + jnp.exp2(jnp.abs(z) * NEG_LOG2E))
    log_beta = jnp.minimum(z, 0.0) - soft
    return log_beta, log_beta - z


N_HEADS = 16
FWD_HEADS = BWD_HEADS = 4


def _masked_heads(x, heads):
    lane = lax.broadcasted_iota(jnp.int32, (1, HEAD_PAIR_QK), 1)
    out = []
    for hh in range(heads):
        slab = x[:, (hh // 2) * HEAD_PAIR_QK:(hh // 2 + 1) * HEAD_PAIR_QK]
        out.append(jnp.where((lane // 64) == hh % 2, slab, jnp.zeros_like(slab)))
    return out


def _attn_fwd(qkvz, S, name):
    T = qkvz.shape[0]
    nq = S // TQ
    HEADS, QK_W, V_W = FWD_HEADS, FWD_HEADS * 64, FWD_HEADS * HEAD_V
    k_b0 = QK_WIDTH // QK_W
    v_b0 = 2 * QK_WIDTH // V_W
    z_b0 = (2 * QK_WIDTH + D_INNER) // V_W
    hs = range(HEADS)

    def body(q_ref, k_ref, v_ref, z_ref, o_ref, y_ref, lt_ref):
        qi = pl.program_id(2)
        row = lax.broadcasted_iota(jnp.int32, (TQ, TQ), 0)
        col = lax.broadcasted_iota(jnp.int32, (TQ, TQ), 1)
        causal = col < row
        later_in_block = (row > col).astype(BF16)
        qms = [qm * 0.125 for qm in _masked_heads(q_ref[...], HEADS)]

        def step(j, carry, diagonal):
            koff = pl.multiple_of(j * TQ, TQ)
            kbs = [k_ref[pl.ds(koff, TQ), p * HEAD_PAIR_QK:(p + 1) * HEAD_PAIR_QK]
                   for p in range(HEADS // 2)]
            run, acc = [carry[2 * hh] for hh in hs], [carry[2 * hh + 1] for hh in hs]
            z = [_dot_nt(qms[hh], kbs[hh // 2]) for hh in hs]
            terms = [_log_terms(z[hh]) for hh in hs]
            log_om = [jnp.where(causal, t[1], 0.0) if diagonal else t[1] for t in terms]
            later = [_split_dot(log_om[hh], later_in_block) for hh in hs]
            a = [jnp.exp(terms[hh][0] + (run[hh] + later[hh])) for hh in hs]
            if diagonal:
                a = [jnp.where(causal, a[hh], 0.0) for hh in hs]
            out = []
            for hh in hs:
                vb = v_ref[pl.ds(koff, TQ), hh * HEAD_V:(hh + 1) * HEAD_V]
                out += [run[hh] + jnp.sum(log_om[hh], axis=1, keepdims=True),
                        acc[hh] + _dot(a[hh].astype(BF16), vb)]
            return tuple(out)

        zero = (jnp.zeros((TQ, 1), F32), jnp.zeros((TQ, HEAD_V), F32))
        carry = step(qi, zero * HEADS, True)
        carry = lax.fori_loop(0, qi, lambda i, c: step(qi - 1 - i, c, False), carry)
        for hh in hs:
            sl = slice(hh * HEAD_V, (hh + 1) * HEAD_V)
            acc = carry[2 * hh + 1]
            z = z_ref[:, sl].astype(F32)
            o_ref[:, sl] = acc.astype(BF16)
            y_ref[:, sl] = (acc * (z * _sigmoid(z))).astype(BF16)
            lt_ref[:, hh:hh + 1] = carry[2 * hh]

    qblk = lambda b, p, i: (b * nq + i, p)
    return pl.pallas_call(
        body, name=name, grid=(T // S, N_HEADS // HEADS, nq),
        in_specs=[pl.BlockSpec((TQ, QK_W), qblk),
                  pl.BlockSpec((S, QK_W), lambda b, p, i: (b, k_b0 + p)),
                  pl.BlockSpec((S, V_W), lambda b, p, i: (b, v_b0 + p)),
                  pl.BlockSpec((TQ, V_W), lambda b, p, i: (b * nq + i, z_b0 + p))],
        out_specs=[pl.BlockSpec((TQ, V_W), qblk),
                   pl.BlockSpec((TQ, V_W), qblk),
                   pl.BlockSpec((None, TQ, HEADS), lambda b, p, i: (p, b * nq + i, 0))],
        out_shape=[jax.ShapeDtypeStruct((T, D_INNER), BF16),
                   jax.ShapeDtypeStruct((T, D_INNER), BF16),
                   jax.ShapeDtypeStruct((N_HEADS // HEADS, T, HEADS), F32)],
        compiler_params=_params(3),
    )(qkvz, qkvz, qkvz, qkvz)


def _attn_gate_bwd(dh, w_out, qkvz, o, name, tm=1024, tn=512):
    T = dh.shape[0]
    gate_b0 = (2 * QK_WIDTH + D_INNER) // tn

    def body(dh_ref, w_ref, z_ref, o_ref, do_ref, dz_ref, dh_s):
        @pl.when(pl.program_id(1) == 0)
        def _():
            dh_s[...] = dh_ref[...].astype(BF16)

        dy = _dot_nt(dh_s[...], w_ref[...])
        z = z_ref[...].astype(F32)
        sig = _sigmoid(z)
        do_ref[...] = (dy * (z * sig)).astype(BF16)
        dz_ref[...] = (dy * o_ref[...].astype(F32) * (sig * (1.0 + z * (1.0 - sig)))).astype(BF16)

    return pl.pallas_call(
        body, name=name, grid=(T // tm, D_INNER // tn),
        in_specs=[pl.BlockSpec((tm, D_MODEL), lambda m, n: (m, 0)),
                  pl.BlockSpec((tn, D_MODEL), lambda m, n: (n, 0)),
                  pl.BlockSpec((tm, tn), lambda m, n: (m, gate_b0 + n)),
                  pl.BlockSpec((tm, tn), lambda m, n: (m, n))],
        out_specs=[pl.BlockSpec((tm, tn), lambda m, n: (m, n)),
                   pl.BlockSpec((tm, tn), lambda m, n: (m, gate_b0 + n))],
        out_shape=[jax.ShapeDtypeStruct((T, D_INNER), BF16),
                   jax.ShapeDtypeStruct((T, 2 * QK_WIDTH + 2 * D_INNER), BF16)],
        scratch_shapes=[pltpu.VMEM((tm, D_MODEL), BF16)],
        compiler_params=_params(2),
    )(dh, w_out, qkvz, o)


def _attn_bwd(qkv, do, ltot, dproj1, S, name):
    T = qkv.shape[0]
    nq = S // TQ
    HEADS, QK_W, V_W = BWD_HEADS, BWD_HEADS * 64, BWD_HEADS * HEAD_V
    k_b0 = QK_WIDTH // QK_W
    v_b0 = 2 * QK_WIDTH // V_W
    hs = range(HEADS)
    pairs = range(HEADS // 2)

    def body(q_ref, k_ref, v_ref, do_ref, lt_ref, _, out_ref, dq_s, dk_s, dv_s, dkb_s, dvb_s, sems):
        b, p = pl.program_id(0), pl.program_id(1)
        row = lax.broadcasted_iota(jnp.int32, (TQ, TQ), 0)
        col = lax.broadcasted_iota(jnp.int32, (TQ, TQ), 1)
        causal = col < row
        upto = (row <= col).astype(BF16)
        before = (row < col).astype(BF16)
        dk_s[...] = jnp.zeros_like(dk_s)
        dv_s[...] = jnp.zeros_like(dv_s)

        def q_block(qi, _):
            qoff = pl.multiple_of(qi * TQ, TQ)
            qms = [qm * 0.125 for qm in _masked_heads(q_ref[pl.ds(qoff, TQ), :], HEADS)]
            vsl = [slice(hh * HEAD_V, (hh + 1) * HEAD_V) for hh in hs]
            psl = [slice(pp * HEAD_PAIR_QK, (pp + 1) * HEAD_PAIR_QK) for pp in pairs]
            do_h = [do_ref[pl.ds(qoff, TQ), sl] for sl in vsl]
            total = [lt_ref[pl.ds(qoff, TQ), hh:hh + 1] for hh in hs]

            def k_block(j, carry, diagonal):
                koff = pl.multiple_of(j * TQ, TQ)
                kms = _masked_heads(k_ref[pl.ds(koff, TQ), :], HEADS)
                g_before = [carry[2 * hh] for hh in hs]
                lom_before = [carry[2 * hh + 1] for hh in hs]
                z = [_dot_nt(qms[hh], kms[hh]) for hh in hs]
                da = [_dot_nt(do_h[hh], v_ref[pl.ds(koff, TQ), vsl[hh]]) for hh in hs]
                terms = [_log_terms(z[hh]) for hh in hs]
                log_om = [jnp.where(causal, t[1], 0.0) if diagonal else t[1] for t in terms]
                prefix = [_split_dot(log_om[hh], upto) for hh in hs]
                a = [jnp.exp(terms[hh][0] + ((total[hh] - lom_before[hh]) - prefix[hh])) for hh in hs]
                if diagonal:
                    a = [jnp.where(causal, a[hh], 0.0) for hh in hs]
                g = [a[hh] * da[hh] for hh in hs]
                g_prefix = [_dot(g[hh].astype(BF16), before) for hh in hs]
                out, dzs = [], []
                for hh in hs:
                    beta = jnp.exp(terms[hh][0])
                    g_excl = (g_before[hh] + g_prefix[hh]) * beta
                    if diagonal:
                        g_excl = jnp.where(causal, g_excl, 0.0)
                    dzs.append((g[hh] * (1.0 - beta) - g_excl).astype(BF16))
                    out += [g_before[hh] + jnp.sum(g[hh], axis=1, keepdims=True),
                            lom_before[hh] + jnp.sum(log_om[hh], axis=1, keepdims=True)]
                for hh in hs:
                    dv_s[pl.ds(koff, TQ), vsl[hh]] += _dot_tn(a[hh].astype(BF16), do_h[hh])
                dq = []
                for pp in pairs:
                    pair = slice(2 * pp, 2 * pp + 2)
                    dq.append(carry[2 * HEADS + pp] + _dot(jnp.concatenate(dzs[pair], axis=1),
                                                           jnp.concatenate(kms[pair], axis=0)))
                    dk_s[pl.ds(koff, TQ), psl[pp]] += _dot_tn(jnp.concatenate(dzs[pair], axis=0),
                                                              jnp.concatenate(qms[pair], axis=0))
                return tuple(out) + tuple(dq)

            zero = jnp.zeros((TQ, 1), F32)
            carry = (zero,) * (2 * HEADS) + (jnp.zeros((TQ, HEAD_PAIR_QK), F32),) * (HEADS // 2)
            carry = lax.fori_loop(0, qi, lambda j, c: k_block(j, c, False), carry)
            carry = k_block(qi, carry, True)
            for pp in pairs:
                dq_s[pl.ds(qoff, TQ), psl[pp]] = (carry[2 * HEADS + pp] * 0.125).astype(BF16)
            return 0

        lax.fori_loop(0, nq, q_block, 0)
        dkb_s[...] = dk_s[...].astype(BF16)
        dvb_s[...] = dv_s[...].astype(BF16)
        rows = pl.ds(pl.multiple_of(b * S, TQ), S)
        copies = [
            pltpu.make_async_copy(
                dq_s, out_ref.at[rows, pl.ds(pl.multiple_of(p * QK_W, 128), QK_W)], sems.at[0]),
            pltpu.make_async_copy(
                dkb_s, out_ref.at[rows, pl.ds(pl.multiple_of(QK_WIDTH + p * QK_W, 128), QK_W)],
                sems.at[1]),
            pltpu.make_async_copy(
                dvb_s, out_ref.at[rows, pl.ds(pl.multiple_of(2 * QK_WIDTH + p * V_W, 128), V_W)],
                sems.at[2]),
        ]
        for cp in copies:
            cp.start()
        for cp in copies:
            cp.wait()

    return pl.pallas_call(
        body, name=name, grid=(T // S, N_HEADS // HEADS),
        in_specs=[pl.BlockSpec((S, QK_W), lambda b, p: (b, p)),
                  pl.BlockSpec((S, QK_W), lambda b, p: (b, k_b0 + p)),
                  pl.BlockSpec((S, V_W), lambda b, p: (b, v_b0 + p)),
                  pl.BlockSpec((S, V_W), lambda b, p: (b, p)),
                  pl.BlockSpec((None, S, HEADS), lambda b, p: (p, b, 0)),
                  HBM_SPEC],
        out_specs=HBM_SPEC,
        out_shape=jax.ShapeDtypeStruct(dproj1.shape, dproj1.dtype),
        input_output_aliases={5: 0},
        scratch_shapes=[pltpu.VMEM((S, QK_W), BF16),
                        pltpu.VMEM((S, QK_W), F32),
                        pltpu.VMEM((S, V_W), F32),
                        pltpu.VMEM((S, QK_W), BF16),
                        pltpu.VMEM((S, V_W), BF16),
                        pltpu.SemaphoreType.DMA((3,))],
        compiler_params=_params(2),
    )(qkv, qkv, qkv, do, ltot, dproj1)


def _loss_head(h, g_row, target, name, tm=512):
    T = h.shape[0]

    def body(h_ref, g_ref, t_ref, dh_ref, dg_ref, loss_ref):
        @pl.when(pl.program_id(0) == 0)
        def _():
            dg_ref[...] = jnp.zeros_like(dg_ref)
            loss_ref[...] = jnp.zeros_like(loss_ref)

        x = h_ref[...]
        inv = lax.rsqrt(jnp.mean(x * x, axis=-1, keepdims=True) + RMS_EPS)
        xhat = x * inv
        gain = g_ref[...]
        err = xhat * gain - t_ref[...]
        per_token = jnp.mean(err * err, axis=-1, keepdims=True)
        loss_ref[...] += 0.5 * jnp.sum(per_token, axis=0, keepdims=True)
        dy = err * (1.0 / D_MODEL)
        dg_ref[...] += jnp.sum(dy * xhat, axis=0, keepdims=True)
        dxh = dy * gain
        proj = jnp.mean(dxh * xhat, axis=-1, keepdims=True)
        dh_ref[...] = inv * (dxh - xhat * proj)

    return pl.pallas_call(
        body, name=name, grid=(T // tm,),
        in_specs=[pl.BlockSpec((tm, D_MODEL), lambda m: (m, 0)),
                  pl.BlockSpec((1, D_MODEL), lambda m: (0, 0)),
                  pl.BlockSpec((tm, D_MODEL), lambda m: (m, 0))],
        out_specs=[pl.BlockSpec((tm, D_MODEL), lambda m: (m, 0)),
                   pl.BlockSpec((1, D_MODEL), lambda m: (0, 0)),
                   pl.BlockSpec((1, 128), lambda m: (0, 0))],
        out_shape=[jax.ShapeDtypeStruct((T, D_MODEL), F32),
                   jax.ShapeDtypeStruct((1, D_MODEL), F32),
                   jax.ShapeDtypeStruct((1, 128), F32)],
        compiler_params=_params(1),
    )(h, g_row, target)


def _place():
    return lax.axis_index("x"), lax.axis_index("y"), lax.axis_index("c")


def _other_chips(x, y):
    return [(1 - x, y), (x, 1 - y), (1 - x, 1 - y)]


def _half(ref, c):
    hr = ref.shape[-2] // 2
    return pl.ds(pl.multiple_of(c * hr, 8), hr)


def _cast_to_slot(shard, chip, name, tr=256):
    R, C = shard.shape

    def body(chip_ref, w_ref, o_ref):
        o_ref[0] = w_ref[...].astype(BF16)

    return pl.pallas_call(
        body, name=name,
        grid_spec=pltpu.PrefetchScalarGridSpec(
            num_scalar_prefetch=1, grid=(R // tr,),
            in_specs=[pl.BlockSpec((tr, C), lambda i, chip_ref: (i, 0))],
            out_specs=pl.BlockSpec((1, tr, C), lambda i, chip_ref: (chip_ref[0], i, 0))),
        out_shape=jax.ShapeDtypeStruct((N_CHIPS, R, C), BF16),
        compiler_params=_params(1),
    )(chip, shard)


def _weight_plan(bufs):
    x, y, c = _place()
    plan = []
    for buf in bufs:
        mine = buf.at[2 * x + y, _half(buf, c)]
        for ox, oy in _other_chips(x, y):
            plan.append((mine, mine, (ox, oy, c), buf.at[2 * ox + oy, _half(buf, c)]))
    return plan


def _chip_sum_plan(bufs):
    x, y, c = _place()
    n = len(bufs) // 2
    plan = []
    for sums, land in zip(bufs[:n], bufs[n:]):
        for k, (ox, oy) in enumerate(_other_chips(x, y)):
            plan.append((sums.at[2 * ox + oy], land.at[k], (ox, oy, c), land.at[k]))
    return plan


SEM_SPEC = pl.BlockSpec(memory_space=pltpu.SEMAPHORE)
ANY_SPEC = pl.BlockSpec(memory_space=pl.ANY)
DATAFLOW = pltpu.SideEffectType.DATAFLOW_SIDE_EFFECTING


def _in_hbm(a):
    return pltpu.with_memory_space_constraint(a, pltpu.HBM)


def _exchange_start(bufs, after, plan, n_copies, name):
    nb = len(bufs)

    def body(*refs):
        send_sems, recv_sems = refs[nb + 1], refs[nb + 2]
        for i, (src, dst, dev, _) in enumerate(plan(refs[:nb])):
            pltpu.make_async_remote_copy(
                src_ref=src, dst_ref=dst, send_sem=send_sems.at[i], recv_sem=recv_sems.at[i],
                device_id=dev, device_id_type=MESH).start()
        token = refs[-1]
        token[...] = jnp.zeros_like(token)

    res = pl.pallas_call(
        body, name=name,
        in_specs=[HBM_SPEC] * nb + [ANY_SPEC],
        out_specs=[SEM_SPEC, SEM_SPEC] + [HBM_SPEC] * nb + [pl.BlockSpec(memory_space=pltpu.VMEM)],
        out_shape=[pltpu.SemaphoreType.DMA((n_copies,)), pltpu.SemaphoreType.DMA((n_copies,))]
        + [pltpu.HBM(b.shape, b.dtype) for b in bufs] + [jax.ShapeDtypeStruct((8, 128), F32)],
        input_output_aliases={i: 2 + i for i in range(nb)},
        compiler_params=pltpu.CompilerParams(has_side_effects=DATAFLOW),
    )(*[_in_hbm(b) for b in bufs], after)
    return res[0], res[1], list(res[2:2 + nb]), res[-1]


def _exchange_wait(bufs, send_sems, recv_sems, after, plan, name):
    nb = len(bufs)

    def body(*refs):
        sends, recvs = refs[nb], refs[nb + 1]
        for i, (src, dst, dev, landing) in enumerate(plan(refs[:nb])):
            pltpu.make_async_remote_copy(
                src_ref=src, dst_ref=landing, send_sem=sends.at[i], recv_sem=recvs.at[i],
                device_id=dev, device_id_type=MESH).wait()

    res = pl.pallas_call(
        body, name=name,
        in_specs=[HBM_SPEC] * nb + [SEM_SPEC, SEM_SPEC, ANY_SPEC],
        out_specs=[HBM_SPEC] * nb,
        out_shape=[pltpu.HBM(b.shape, b.dtype) for b in bufs],
        input_output_aliases={i: i for i in range(nb)},
        compiler_params=pltpu.CompilerParams(has_side_effects=DATAFLOW),
    )(*bufs, send_sems, recv_sems, after)
    return list(res)


def _allgather_weights(slots, name, landed=False):
    n = len(slots)

    def body(*refs):
        outs = refs[n:2 * n]
        send_sems, recv_sems, fwd_send, fwd_recv = refs[2 * n:]
        x, y, c = _place()
        chips = _other_chips(x, y)

        def landing(a, chip, half_of):
            return outs[a].at[2 * chip[0] + chip[1], _half(outs[a], half_of)]

        def ici(a, k, chip_from, to):
            return pltpu.make_async_remote_copy(
                src_ref=landing(a, chip_from, c), dst_ref=landing(a, chip_from, c),
                send_sem=send_sems.at[a, k], recv_sem=recv_sems.at[a, k],
                device_id=to, device_id_type=MESH)

        def d2d(a, k, chip_from, half_of):
            return pltpu.make_async_remote_copy(
                src_ref=landing(a, chip_from, half_of), dst_ref=landing(a, chip_from, half_of),
                send_sem=fwd_send.at[a, k], recv_sem=fwd_recv.at[a, k],
                device_id=(x, y, 1 - c), device_id_type=MESH)

        sends = []
        if not landed:
            sends = [ici(a, k, (x, y), (*chips[k], c)) for a in range(n) for k in range(3)]
        for cp in sends:
            cp.start()
        forwards = []
        for a in range(n):
            for k in range(3):
                if not landed:
                    ici(a, k, chips[k], (x, y, c)).wait_recv()
                fw = d2d(a, k, chips[k], c)
                fw.start()
                forwards.append(fw)
        for a in range(n):
            for k in range(3):
                d2d(a, k, chips[k], 1 - c).wait_recv()
        for cp in sends + forwards:
            cp.wait_send()

    return pl.pallas_call(
        body, name=name,
        in_specs=[HBM_SPEC] * n, out_specs=[HBM_SPEC] * n,
        out_shape=[jax.ShapeDtypeStruct(s.shape, s.dtype) for s in slots],
        input_output_aliases={a: a for a in range(n)},
        scratch_shapes=[pltpu.SemaphoreType.DMA((n, 3)), pltpu.SemaphoreType.DMA((n, 3)),
                        pltpu.SemaphoreType.DMA((n, 3)), pltpu.SemaphoreType.DMA((n, 3))],
    )(*slots)


def _sibling_exchange(partials, small, name):
    n = len(partials)
    ns = 0 if small is None else 1

    def body(*refs):
        ins, outs = refs[:n], refs[n + ns:2 * n + ns]
        send_sems, recv_sems = refs[2 * (n + ns):2 * (n + ns) + 2]
        x, y, c = _place()
        me = 4 * x + 2 * y + c
        sends = [pltpu.make_async_remote_copy(
            src_ref=ins[a].at[:, _half(ins[a], 1 - c)], dst_ref=outs[a],
            send_sem=send_sems.at[a], recv_sem=recv_sems.at[a],
            device_id=(x, y, 1 - c), device_id_type=MESH) for a in range(n)]
        if ns:
            small_ref, small_all = refs[n], refs[2 * n + 1]
            s_send, s_recv, loc_sem = refs[2 * (n + ns) + 2:]
            local = pltpu.make_async_copy(small_ref, small_all.at[me], loc_sem)
            local.start()
            for d in range(1, N_DEV):
                px, py, pc = x ^ ((d >> 2) & 1), y ^ ((d >> 1) & 1), c ^ (d & 1)
                sends.append(pltpu.make_async_remote_copy(
                    src_ref=small_ref, dst_ref=small_all.at[me],
                    send_sem=s_send.at[d - 1], recv_sem=s_recv.at[d - 1],
                    device_id=(px, py, pc), device_id_type=MESH))
        for cp in sends:
            cp.start()
        if ns:
            for d in range(1, N_DEV):
                pltpu.make_async_remote_copy(
                    src_ref=small_ref, dst_ref=small_all.at[me ^ d],
                    send_sem=s_send.at[d - 1], recv_sem=s_recv.at[d - 1],
                    device_id=(x, y, c), device_id_type=MESH).wait_recv()
        for cp in sends[:n]:
            cp.wait_recv()
        for cp in sends:
            cp.wait_send()
        if ns:
            local.wait()

    out_shape = [jax.ShapeDtypeStruct((N_CHIPS, p.shape[1] // 2, p.shape[2]), F32) for p in partials]
    scratch = [pltpu.SemaphoreType.DMA((max(n, 1),)), pltpu.SemaphoreType.DMA((max(n, 1),))]
    if ns:
        out_shape.append(jax.ShapeDtypeStruct((N_DEV,) + small.shape, F32))
        scratch += [pltpu.SemaphoreType.DMA((N_DEV - 1,)), pltpu.SemaphoreType.DMA((N_DEV - 1,)),
                    pltpu.SemaphoreType.DMA]
    return pl.pallas_call(
        body, name=name,
        in_specs=[HBM_SPEC] * (n + ns), out_specs=[HBM_SPEC] * (n + ns),
        out_shape=out_shape, scratch_shapes=scratch,
    )(*partials, *([small] if ns else []))


def _chip_sum(partial, from_sibling, c, name, tr=256):
    _, hr, C = from_sibling.shape
    nb = hr // tr

    def body(c_ref, p_ref, s_ref, o_ref):
        o_ref[...] = (p_ref[...] + s_ref[...]).astype(BF16)

    return pl.pallas_call(
        body, name=name,
        grid_spec=pltpu.PrefetchScalarGridSpec(
            num_scalar_prefetch=1, grid=(N_CHIPS, nb),
            in_specs=[pl.BlockSpec((1, tr, C), lambda j, i, c_ref: (j, c_ref[0] * nb + i, 0)),
                      pl.BlockSpec((1, tr, C), lambda j, i, c_ref: (j, i, 0))],
            out_specs=pl.BlockSpec((1, tr, C), lambda j, i, c_ref: (j, i, 0))),
        out_shape=jax.ShapeDtypeStruct(from_sibling.shape, BF16),
        compiler_params=_params(2),
    )(c, partial, from_sibling)


def _reduce_half(partial, from_sibling, received, place, name, tr=256):
    _, hr, C = from_sibling.shape
    nb = hr // tr

    def body(p_ref, mine_ref, sib_ref, r_ref, o_ref):
        acc = mine_ref[0] + sib_ref[0]
        for k in range(3):
            acc = acc + r_ref[k].astype(F32)
        o_ref[...] = acc

    return pl.pallas_call(
        body, name=name,
        grid_spec=pltpu.PrefetchScalarGridSpec(
            num_scalar_prefetch=1, grid=(nb,),
            in_specs=[pl.BlockSpec((1, tr, C), lambda i, p: (p[0], p[1] * nb + i, 0)),
                      pl.BlockSpec((1, tr, C), lambda i, p: (p[0], i, 0)),
                      pl.BlockSpec((3, tr, C), lambda i, p: (0, i, 0))],
            out_specs=pl.BlockSpec((tr, C), lambda i, p: (p[1] * nb + i, 0))),
        out_shape=jax.ShapeDtypeStruct((2 * hr, C), F32),
        compiler_params=_params(1),
    )(place, partial, from_sibling, received)


def _join_halves(fulls, name):
    n = len(fulls)

    def body(*refs):
        outs = refs[n:2 * n]
        send_sems, recv_sems = refs[2 * n:]
        x, y, c = _place()

        def copy(a, half_of, to):
            rows = outs[a].at[_half(outs[a], half_of)]
            return pltpu.make_async_remote_copy(
                src_ref=rows, dst_ref=rows, send_sem=send_sems.at[a], recv_sem=recv_sems.at[a],
                device_id=to, device_id_type=MESH)

        sends = [copy(a, c, (x, y, 1 - c)) for a in range(n)]
        for cp in sends:
            cp.start()
        for a in range(n):
            copy(a, 1 - c, (x, y, c)).wait_recv()
        for cp in sends:
            cp.wait_send()

    return pl.pallas_call(
        body, name=name,
        in_specs=[HBM_SPEC] * n, out_specs=[HBM_SPEC] * n,
        out_shape=[jax.ShapeDtypeStruct(f.shape, F32) for f in fulls],
        input_output_aliases={a: a for a in range(n)},
        scratch_shapes=[pltpu.SemaphoreType.DMA((n,)), pltpu.SemaphoreType.DMA((n,))],
    )(*fulls)


def _adamw_math(w, g, m, v):
    m = ADAM_B1 * m + (1.0 - ADAM_B1) * g
    v = ADAM_B2 * v + (1.0 - ADAM_B2) * (g * g)
    m_hat = m / (1.0 - ADAM_B1 ** ADAM_STEP)
    v_hat = v / (1.0 - ADAM_B2 ** ADAM_STEP)
    delta = -ADAM_LR * (m_hat / (jnp.sqrt(v_hat) + ADAM_EPS) + ADAM_WD * w)
    return delta, m, v


def _adamw(w, g, m, v, name, tr=256):
    R, C = w.shape
    tr = min(tr, R)

    def body(w_ref, g_ref, m_ref, v_ref, d_out, m_out, v_out):
        d_out[...], m_out[...], v_out[...] = _adamw_math(w_ref[...], g_ref[...], m_ref[...], v_ref[...])

    spec = pl.BlockSpec((tr, C), lambda i: (i, 0))
    return pl.pallas_call(
        body, name=name, grid=(R // tr,),
        in_specs=[spec] * 4, out_specs=[spec] * 3,
        out_shape=[jax.ShapeDtypeStruct((R, C), F32)] * 3,
        compiler_params=_params(1),
    )(w, g, m, v)


def _adamw_small(small_all, w, m, v, name):
    def body(s_ref, w_ref, m_ref, v_ref, g_out, d_out, m_out, v_out):
        g = s_ref[0]
        for d in range(1, N_DEV):
            g = g + s_ref[d]
        g_out[...] = g
        d_out[...], m_out[...], v_out[...] = _adamw_math(w_ref[...], g, m_ref[...], v_ref[...])

    vm = pl.BlockSpec(memory_space=pltpu.VMEM)
    return pl.pallas_call(
        body, name=name, in_specs=[vm] * 4, out_specs=[vm] * 4,
        out_shape=[jax.ShapeDtypeStruct(w.shape, F32)] * 4,
    )(small_all, w, m, v)


def _pack_small(norm_g, pool_scale, norm_f, extra_row):
    return jnp.concatenate([norm_g.reshape(2, D_MODEL), pool_scale.reshape(2, D_MODEL),
                            norm_f.reshape(1, D_MODEL), extra_row,
                            jnp.zeros((2, D_MODEL), F32)], axis=0)


def kernel(x, norm_g, pool_w_in, pool_w, pool_scale, pool_w_out, sb_w_in, sb_w_out, norm_f, loss_target, m_norm_g, m_pool_w_in, m_pool_w, m_pool_scale, m_pool_w_out, m_sb_w_in, m_sb_w_out, m_norm_f, v_norm_g, v_pool_w_in, v_pool_w, v_pool_scale, v_pool_w_out, v_sb_w_in, v_sb_w_out, v_norm_f):
    nb, S, _ = x.shape
    T = nb * S
    xt = x.reshape(T, D_MODEL)
    target = loss_target.reshape(T, D_MODEL)
    cx, cy, cc = _place()

    def shard2d(w):
        return w.reshape(-1, w.shape[-1])

    names = ("pool_w_in", "pool_w", "pool_w_out", "sb_w_in", "sb_w_out")
    w_shards = [shard2d(w) for w in (pool_w_in, pool_w, pool_w_out, sb_w_in, sb_w_out)]
    m_shards = [shard2d(w) for w in (m_pool_w_in, m_pool_w, m_pool_w_out, m_sb_w_in, m_sb_w_out)]
    v_shards = [shard2d(w) for w in (v_pool_w_in, v_pool_w, v_pool_w_out, v_sb_w_in, v_sb_w_out)]

    chip = (2 * cx + cy).reshape(1).astype(jnp.int32)
    c_arr = cc.reshape(1).astype(jnp.int32)
    place = jnp.stack([2 * cx + cy, cc]).astype(jnp.int32)
    slots = [_cast_to_slot(w, chip, "cast_" + nm) for w, nm in zip(w_shards, names)]
    g0, g1, gf = norm_g[0:1], norm_g[1:2], norm_f.reshape(1, D_MODEL)

    w_pin, w_g, w_pout = _allgather_weights(slots[:3], "allgather_pool_weights")
    w_pout = w_pout.reshape(D_INNER, D_MODEL)
    sb_send, sb_recv, sb_slots, token = _exchange_start(slots[3:], w_pin, _weight_plan, 6,
                                                        "sb_weights_start")

    proj0, u0 = _rms_matmul(xt, g0 + token[0:1, 0:1], w_pin, [(2 * D_INNER, BF16)], "pool_in_proj",
                            tn=w_pin.shape[2])
    y0, pooled, mixed = _pool_fwd(proj0, w_g, pool_scale, S, "pool_mix")
    sb_slots = _exchange_wait(sb_slots, sb_send, sb_recv, y0, _weight_plan, "sb_weights_wait")
    w_sin, w_sout = _allgather_weights(sb_slots, "sb_weights_forward", landed=True)
    w_sout = w_sout.reshape(D_INNER, D_MODEL)
    h1 = _matmul_residual(y0, w_pout, xt, "pool_out_proj")
    n1 = 2 * QK_WIDTH + 2 * D_INNER
    qkvz, u1 = _rms_matmul(h1, g1, w_sin, [(n1, BF16)], "sb_in_proj", tn=w_sin.shape[2])
    o, y1, ltot = _attn_fwd(qkvz, S, "sb_attention")
    h2 = _matmul_residual(y1, w_sout, h1, "sb_out_proj")
    dh2, d_norm_f, loss_row = _loss_head(h2, gf, target, "loss_head")

    def reduce_start(partials, tag):
        from_sibling = _sibling_exchange(partials, None, "grad_sibling_exchange_" + tag)
        sums = [_chip_sum(p, s, c_arr, "grad_chip_sum_%s_%d" % (tag, i))
                for i, (p, s) in enumerate(zip(partials, from_sibling))]
        lands = [lax.empty((3,) + s.shape[1:], BF16) for s in sums]
        send, recv, bufs, token = _exchange_start(sums + lands, c_arr, _chip_sum_plan, 3 * len(sums),
                                                  "grad_chip_exchange_start_" + tag)
        return (partials, list(from_sibling), send, recv, bufs), token[0:1, 0:1]

    def reduce_finish(started, after, tag):
        partials, from_sibling, send, recv, bufs = started
        received = _exchange_wait(bufs, send, recv, after, _chip_sum_plan,
                                  "grad_chip_exchange_wait_" + tag)[len(partials):]
        return [_reduce_half(p, s, r, place, "grad_reduce_%s_%d" % (tag, i))
                for i, (p, s, r) in enumerate(zip(partials, from_sibling, received))]

    shard = lambda i, j, t: (j, 0, 0)
    gw_sout = _matmul_tn(y1, dh2, D_INNER, D_MODEL, (D_INNER, D_MODEL), (1024, 1024),
                         lambda i, j, t: (i, j), "grad_sb_w_out", bm=1024, bn=1024)
    sout_started, _ = reduce_start([gw_sout.reshape(N_CHIPS, -1, D_MODEL)], "sb_out")
    do, dproj1 = _attn_gate_bwd(dh2, w_sout, qkvz, o, "sb_gate_bwd")
    dproj1 = _attn_bwd(qkvz, do, ltot, dproj1, S, "sb_attention_bwd")
    gw_sin = _matmul_tn(u1, dproj1, D_MODEL, n1, (N_CHIPS, D_MODEL, n1 // 4), (1, D_MODEL, n1 // 4),
                        shard, "grad_sb_w_in", bm=D_MODEL, bn=n1 // 4)
    sin_started, token = reduce_start([gw_sin], "sb_in")
    dh1, d_g1 = _matmul_nt_rms_bwd(dproj1, w_sin, h1, g1 + token, dh2, "sb_in_bwd", tk=w_sin.shape[2])
    gw_pout = _matmul_tn(y0, dh1, D_INNER, D_MODEL, (D_INNER, D_MODEL), (1024, 1024),
                         lambda i, j, t: (i, j), "grad_pool_w_out", bm=1024, bn=1024)
    dmixed, dproj0, d_scale = _pool_gate_bwd(dh1, w_pout, proj0, mixed, pool_scale, "pool_gate_bwd")
    gw_g = _matmul_tn(pooled, dmixed, D_INNER, D_INNER, (N_CHIPS, GROUP_DIM, GROUP_DIM),
                      (N_CHIPS, GROUP_DIM // N_CHIPS, GROUP_DIM), lambda i, j, t: (0, i, 0),
                      "grad_pool_w", bm=GROUP_DIM, bn=GROUP_DIM, diagonal_blocks=True)
    mix_started, _ = reduce_start([gw_g, gw_pout.reshape(N_CHIPS, -1, D_MODEL)], "pool_mix")
    dproj0 = _pool_bwd(dmixed, w_g, dproj0, S, "pool_bwd")
    n0 = 2 * D_INNER
    gw_pin = _matmul_tn(u0, dproj0, D_MODEL, n0, (N_CHIPS, D_MODEL, n0 // 4), (1, D_MODEL, n0 // 4),
                        shard, "grad_pool_w_in", bm=D_MODEL, bn=n0 // 4)
    pin_started, token = reduce_start([gw_pin], "pool_in")
    dx, d_g0 = _matmul_nt_rms_bwd(dproj0, w_pin, xt, g0 + token, dh1, "pool_in_bwd", tk=w_pin.shape[2])

    small = _pack_small(jnp.concatenate([d_g0, d_g1], axis=0), d_scale, d_norm_f,
                        jnp.broadcast_to(loss_row[:, :1], (1, D_MODEL)))
    small_all, = _sibling_exchange([], small, "small_sums_exchange")
    grads = _join_halves(reduce_finish(pin_started, dx, "pool_in")
                         + reduce_finish(mix_started, dx, "pool_mix")
                         + reduce_finish(sin_started, dx, "sb_in")
                         + reduce_finish(sout_started, dx, "sb_out"), "grad_join_halves")

    deltas, new_m, new_v = [], [], []
    for w, g, m, v, nm in zip(w_shards, grads, m_shards, v_shards, names):
        d, mm, vv = _adamw(w, g, m, v, "adamw_" + nm)
        deltas.append(d)
        new_m.append(mm)
        new_v.append(vv)

    zero_row = jnp.zeros((1, D_MODEL), F32)
    g_small, d_small, m_small, v_small = _adamw_small(
        small_all, _pack_small(norm_g, pool_scale, norm_f, zero_row),
        _pack_small(m_norm_g, m_pool_scale, m_norm_f, zero_row),
        _pack_small(v_norm_g, v_pool_scale, v_norm_f, zero_row + 1.0), "adamw_small")
    loss = g_small[5, 0]

    def unpack_small(a):
        return a[0:2], a[2:4].reshape(1, D_INNER), a[4]

    def assemble(big, small3):
        ng, ps, nf = small3
        return [ng, big[0].reshape(pool_w_in.shape), big[1].reshape(pool_w.shape), ps,
                big[2].reshape(pool_w_out.shape), big[3].reshape(sb_w_in.shape),
                big[4].reshape(sb_w_out.shape), nf]

    return (loss, dx.reshape(x.shape),
            *assemble(grads, unpack_small(g_small)),
            *assemble(deltas, unpack_small(d_small)),
            *assemble(new_m, unpack_small(m_small)),
            *assemble(new_v, unpack_small(v_small)))
```

```python
import functools

import jax
import jax.numpy as jnp
from jax import lax
from jax.experimental import pallas as pl
from jax.experimental.pallas import tpu as pltpu

F32 = jnp.float32
BF16 = jnp.bfloat16
MESH = pl.DeviceIdType.MESH

D_MODEL = 1024
D_INNER = 2048
N_GROUPS = 4
GROUP_DIM = 512
HEAD_PAIR_QK = 128
HEAD_V = 128
N_HEAD_PAIRS = 8
QK_WIDTH = 1024
RMS_EPS = 1e-6
HALO = 16
N_CHIPS = 4
N_DEV = 8

ADAM_LR = 0.001
ADAM_B1 = 0.9
ADAM_B2 = 0.999
ADAM_EPS = 1e-08
ADAM_WD = 0.01
ADAM_STEP = 10

VMEM_LIMIT = 56 * 1024 * 1024

HBM_SPEC = pl.BlockSpec(memory_space=pltpu.HBM)


def _params(n_axes):
    return pltpu.CompilerParams(dimension_semantics=("arbitrary",) * n_axes,
                                vmem_limit_bytes=VMEM_LIMIT)


def _dot(a, b):
    return jnp.dot(a, b, preferred_element_type=F32)


def _dot_nt(a, b):
    return lax.dot_general(a, b, (((1,), (1,)), ((), ())), preferred_element_type=F32)


def _dot_tn(a, b):
    return lax.dot_general(a, b, (((0,), (0,)), ((), ())), preferred_element_type=F32)


def _sigmoid(z):
    return 1.0 / (1.0 + jnp.exp(-z))


def _rms_matmul(h, g_row, w4, outs, name, tm=1024, tn=512):
    T = h.shape[0]
    per_shard = w4.shape[2] // tn
    starts = [0]
    for width, _ in outs:
        starts.append(starts[-1] + width // tn)
    n_out = len(outs)

    def body(h_ref, g_ref, w_ref, *rest):
        o_refs, u_out, u_s = rest[:n_out], rest[n_out], rest[n_out + 1]
        n = pl.program_id(1)

        @pl.when(n == 0)
        def _():
            x = h_ref[...]
            inv = lax.rsqrt(jnp.mean(x * x, axis=-1, keepdims=True) + RMS_EPS)
            u = (x * inv * g_ref[...]).astype(BF16)
            u_s[...] = u
            u_out[...] = u

        res = _dot(u_s[...], w_ref[0])
        for k in range(n_out):
            @pl.when((n >= starts[k]) & (n < starts[k + 1]))
            def _():
                o_refs[k][...] = res.astype(o_refs[k].dtype)

    def out_map(k):
        return lambda m, n: (m, jnp.clip(n - starts[k], 0, starts[k + 1] - starts[k] - 1))

    return pl.pallas_call(
        body, name=name, grid=(T // tm, starts[-1]),
        in_specs=[pl.BlockSpec((tm, D_MODEL), lambda m, n: (m, 0)),
                  pl.BlockSpec((1, D_MODEL), lambda m, n: (0, 0)),
                  pl.BlockSpec((1, D_MODEL, tn), lambda m, n: (n // per_shard, 0, n % per_shard))],
        out_specs=[pl.BlockSpec((tm, tn), out_map(k)) for k in range(n_out)]
        + [pl.BlockSpec((tm, D_MODEL), lambda m, n: (m, 0))],
        out_shape=[jax.ShapeDtypeStruct((T, width), dt) for width, dt in outs]
        + [jax.ShapeDtypeStruct((T, D_MODEL), BF16)],
        scratch_shapes=[pltpu.VMEM((tm, D_MODEL), BF16)],
        compiler_params=_params(2),
    )(h, g_row, w4)


def _matmul_residual(a, w, res, name, tm=1024, tn=512):
    T, K = a.shape
    N = w.shape[1]

    def body(a_ref, w_ref, r_ref, o_ref):
        o_ref[...] = r_ref[...] + _dot(a_ref[...], w_ref[...])

    return pl.pallas_call(
        body, name=name, grid=(T // tm, N // tn),
        in_specs=[pl.BlockSpec((tm, K), lambda m, n: (m, 0)),
                  pl.BlockSpec((K, tn), lambda m, n: (0, n)),
                  pl.BlockSpec((tm, tn), lambda m, n: (m, n))],
        out_specs=pl.BlockSpec((tm, tn), lambda m, n: (m, n)),
        out_shape=jax.ShapeDtypeStruct((T, N), F32),
        compiler_params=_params(2),
    )(a, w, res)


def _matmul_tn(a, b, a_cols, b_cols, out_shape, out_block, out_map, name, bm, bn, tk=512,
               diagonal_blocks=False):
    T = a.shape[0]

    def body(a_ref, b_ref, o_ref):
        @pl.when(pl.program_id(2) == 0)
        def _():
            o_ref[...] = jnp.zeros_like(o_ref)

        part = _dot_tn(a_ref[...].astype(BF16), b_ref[...].astype(BF16))
        o_ref[...] += part.reshape(o_ref.shape)

    b_map = (lambda i, j, t: (t, i)) if diagonal_blocks else (lambda i, j, t: (t, j))
    return pl.pallas_call(
        body, name=name, grid=(a_cols // bm, 1 if diagonal_blocks else b_cols // bn, T // tk),
        in_specs=[pl.BlockSpec((tk, bm), lambda i, j, t: (t, i)),
                  pl.BlockSpec((tk, bn), b_map)],
        out_specs=pl.BlockSpec(out_block, out_map),
        out_shape=jax.ShapeDtypeStruct(out_shape, F32),
        compiler_params=_params(3),
    )(a, b)


def _matmul_nt_rms_bwd(dproj, w4, h, g_row, dres, name, tm=1024, tk=512):
    T, cols = dproj.shape
    per_shard = w4.shape[2] // tk
    nk = cols // tk

    def body(dp_ref, w_ref, h_ref, g_ref, r_ref, dx_ref, dg_ref, acc):
        m, k = pl.program_id(0), pl.program_id(1)

        @pl.when(k == 0)
        def _():
            acc[...] = jnp.zeros_like(acc)

        @pl.when((k == 0) & (m == 0))
        def _():
            dg_ref[...] = jnp.zeros_like(dg_ref)

        acc[...] += _dot_nt(dp_ref[...], w_ref[0])

        @pl.when(k == nk - 1)
        def _():
            du = acc[...]
            x = h_ref[...]
            inv = lax.rsqrt(jnp.mean(x * x, axis=-1, keepdims=True) + RMS_EPS)
            xhat = x * inv
            dg_ref[...] += jnp.sum(du * xhat, axis=0, keepdims=True)
            dxh = du * g_ref[...]
            proj = jnp.mean(dxh * xhat, axis=-1, keepdims=True)
            dx_ref[...] = r_ref[...] + inv * (dxh - xhat * proj)

    return pl.pallas_call(
        body, name=name, grid=(T // tm, nk),
        in_specs=[pl.BlockSpec((tm, tk), lambda m, k: (m, k)),
                  pl.BlockSpec((1, D_MODEL, tk), lambda m, k: (k // per_shard, 0, k % per_shard)),
                  pl.BlockSpec((tm, D_MODEL), lambda m, k: (m, 0)),
                  pl.BlockSpec((1, D_MODEL), lambda m, k: (0, 0)),
                  pl.BlockSpec((tm, D_MODEL), lambda m, k: (m, 0))],
        out_specs=[pl.BlockSpec((tm, D_MODEL), lambda m, k: (m, 0)),
                   pl.BlockSpec((1, D_MODEL), lambda m, k: (0, 0))],
        out_shape=[jax.ShapeDtypeStruct((T, D_MODEL), F32),
                   jax.ShapeDtypeStruct((1, D_MODEL), F32)],
        scratch_shapes=[pltpu.VMEM((tm, D_MODEL), F32)],
        compiler_params=_params(2),
    )(dproj, w4, h, g_row, dres)


def _window_of(g):
    return jnp.left_shift(2, g)


def _select_stage(g, stages):
    res = stages[0]
    for i in range(1, len(stages)):
        res = jnp.where(g >= i, stages[i], res)
    return res


def _pool_fwd(proj0, wg4, scale_row, S, name, tm=1024):
    T = proj0.shape[0]
    tm = min(tm, S)
    blocks_per_seq = S // tm
    hb = tm // HALO

    def body(x_ref, halo_ref, z_ref, w_ref, s_ref, y_ref, p_ref, mix_ref):
        m, g = pl.program_id(0), pl.program_id(1)
        first = (m % blocks_per_seq) == 0
        halo = jnp.where(first, 0.0, halo_ref[...].astype(F32))
        x = x_ref[...].astype(F32)
        ext = jnp.concatenate([halo, x], axis=0)
        stages = []
        cur = ext
        for sh in (1, 2, 4, 8):
            cur = cur + pltpu.roll(cur, sh, 0)
            stages.append(cur[HALO:, :])
        win_sum = _select_stage(g, stages)
        pos = (m % blocks_per_seq) * tm + lax.broadcasted_iota(jnp.int32, (tm, 1), 0)
        count = jnp.minimum(pos + 1, _window_of(g)).astype(F32)
        pooled = win_sum / count - x
        pooled_b = pooled.astype(BF16)
        mixed = _dot(pooled_b, w_ref[...].reshape(GROUP_DIM, GROUP_DIM))
        z = z_ref[...].astype(F32)
        y_ref[...] = (mixed * s_ref[...] * (z * _sigmoid(z))).astype(BF16)
        p_ref[...] = pooled_b
        mix_ref[...] = mixed.astype(BF16)

    blk = lambda m, g: (m, g)
    return pl.pallas_call(
        body, name=name, grid=(T // tm, N_GROUPS),
        in_specs=[pl.BlockSpec((tm, GROUP_DIM), blk),
                  pl.BlockSpec((HALO, GROUP_DIM), lambda m, g: (jnp.maximum(m * hb - 1, 0), g)),
                  pl.BlockSpec((tm, GROUP_DIM), lambda m, g: (m, N_GROUPS + g)),
                  pl.BlockSpec((N_CHIPS, GROUP_DIM // N_CHIPS, GROUP_DIM), lambda m, g: (0, g, 0)),
                  pl.BlockSpec((1, GROUP_DIM), lambda m, g: (0, g))],
        out_specs=[pl.BlockSpec((tm, GROUP_DIM), blk)] * 3,
        out_shape=[jax.ShapeDtypeStruct((T, D_INNER), BF16)] * 3,
        compiler_params=_params(2),
    )(proj0, proj0, proj0, wg4, scale_row)


def _pool_gate_bwd(dh, w_out, proj0, mixed, scale_row, name, tm=1024, tn=512):
    T = dh.shape[0]
    gate_b0 = D_INNER // tn

    def body(dh_ref, w_ref, z_ref, mix_ref, s_ref, dm_ref, dz_ref, ds_ref):
        @pl.when(pl.program_id(1) == 0)
        def _():
            ds_ref[...] = jnp.zeros_like(ds_ref)

        dy = _dot_nt(dh_ref[...].astype(BF16), w_ref[...])
        z = z_ref[...].astype(F32)
        sig = _sigmoid(z)
        silu = z * sig
        mixed = mix_ref[...].astype(F32)
        s = s_ref[...]
        dm_ref[...] = (dy * s * silu).astype(BF16)
        dz_ref[...] = (dy * mixed * s * (sig * (1.0 + z * (1.0 - sig)))).astype(BF16)
        ds_ref[...] += jnp.sum(dy * mixed * silu, axis=0, keepdims=True)

    return pl.pallas_call(
        body, name=name, grid=(D_INNER // tn, T // tm),
        in_specs=[pl.BlockSpec((tm, D_MODEL), lambda n, m: (m, 0)),
                  pl.BlockSpec((tn, D_MODEL), lambda n, m: (n, 0)),
                  pl.BlockSpec((tm, tn), lambda n, m: (m, gate_b0 + n)),
                  pl.BlockSpec((tm, tn), lambda n, m: (m, n)),
                  pl.BlockSpec((1, tn), lambda n, m: (0, n))],
        out_specs=[pl.BlockSpec((tm, tn), lambda n, m: (m, n)),
                   pl.BlockSpec((tm, tn), lambda n, m: (m, gate_b0 + n)),
                   pl.BlockSpec((1, tn), lambda n, m: (0, n))],
        out_shape=[jax.ShapeDtypeStruct((T, D_INNER), BF16),
                   jax.ShapeDtypeStruct((T, 2 * D_INNER), BF16),
                   jax.ShapeDtypeStruct((1, D_INNER), F32)],
        compiler_params=_params(2),
    )(dh, w_out, proj0, mixed, scale_row)


def _pool_bwd(dmixed, wg4, dproj0, S, name, tm=1024):
    T = dmixed.shape[0]
    tm = min(tm, S)
    blocks_per_seq = S // tm
    hb = tm // HALO
    n_halo_blocks = T // HALO

    def body(dm_ref, halo_ref, w_ref, _, o_ref):
        m, g = pl.program_id(0), pl.program_id(1)
        ext = jnp.concatenate([dm_ref[...], halo_ref[...]], axis=0)
        dp = _dot_nt(ext, w_ref[...].reshape(GROUP_DIM, GROUP_DIM))
        pos = (m % blocks_per_seq) * tm + lax.broadcasted_iota(jnp.int32, (tm + HALO, 1), 0)
        count = jnp.minimum(pos + 1, _window_of(g)).astype(F32)
        c = jnp.where(pos < S, dp / count, 0.0)
        n = tm + HALO
        stages = []
        cur = c
        for sh in (1, 2, 4, 8):
            cur = cur + pltpu.roll(cur, n - sh, 0)
            stages.append(cur[:tm, :])
        o_ref[...] = (_select_stage(g, stages) - dp[:tm, :]).astype(BF16)

    blk = lambda m, g: (m, g)
    return pl.pallas_call(
        body, name=name, grid=(T // tm, N_GROUPS),
        in_specs=[pl.BlockSpec((tm, GROUP_DIM), blk),
                  pl.BlockSpec((HALO, GROUP_DIM),
                               lambda m, g: (jnp.minimum((m + 1) * hb, n_halo_blocks - 1), g)),
                  pl.BlockSpec((N_CHIPS, GROUP_DIM // N_CHIPS, GROUP_DIM), lambda m, g: (0, g, 0)),
                  HBM_SPEC],
        out_specs=pl.BlockSpec((tm, GROUP_DIM), blk),
        out_shape=jax.ShapeDtypeStruct(dproj0.shape, dproj0.dtype),
        input_output_aliases={3: 0},
        compiler_params=_params(2),
    )(dmixed, dmixed, wg4, dproj0)


TQ = 256


def _split_dot(x, m):
    hi = x.astype(BF16)
    lo = (x - hi.astype(F32)).astype(BF16)
    return _dot(hi, m) + _dot(lo, m)


NEG_LOG2E = -1.4426950408889634


def _log_terms(z):
    soft = jnp.log(1.0 + jnp.exp2(jnp.abs(z) * NEG_LOG2E))
    log_beta = jnp.minimum(z, 0.0) - soft
    return log_beta, log_beta - z


N_HEADS = 16
FWD_HEADS = BWD_HEADS = 4


def _masked_heads(x, heads):
    lane = lax.broadcasted_iota(jnp.int32, (1, HEAD_PAIR_QK), 1)
    out = []
    for hh in range(heads):
        slab = x[:, (hh // 2) * HEAD_PAIR_QK:(hh // 2 + 1) * HEAD_PAIR_QK]
        out.append(jnp.where((lane // 64) == hh % 2, slab, jnp.zeros_like(slab)))
    return out


def _attn_fwd(qkvz, S, name):
    T = qkvz.shape[0]
    nq = S // TQ
    HEADS, QK_W, V_W = FWD_HEADS, FWD_HEADS * 64, FWD_HEADS * HEAD_V
    k_b0 = QK_WIDTH // QK_W
    v_b0 = 2 * QK_WIDTH // V_W
    z_b0 = (2 * QK_WIDTH + D_INNER) // V_W
    hs = range(HEADS)

    def body(q_ref, k_ref, v_ref, z_ref, o_ref, y_ref, lt_ref):
        qi = pl.program_id(2)
        row = lax.broadcasted_iota(jnp.int32, (TQ, TQ), 0)
        col = lax.broadcasted_iota(jnp.int32, (TQ, TQ), 1)
        causal = col < row
        later_in_block = (row > col).astype(BF16)
        qms = [qm * 0.125 for qm in _masked_heads(q_ref[...], HEADS)]

        def step(j, carry, diagonal):
            koff = pl.multiple_of(j * TQ, TQ)
            kbs = [k_ref[pl.ds(koff, TQ), p * HEAD_PAIR_QK:(p + 1) * HEAD_PAIR_QK]
                   for p in range(HEADS // 2)]
            run, acc = [carry[2 * hh] for hh in hs], [carry[2 * hh + 1] for hh in hs]
            z = [_dot_nt(qms[hh], kbs[hh // 2]) for hh in hs]
            terms = [_log_terms(z[hh]) for hh in hs]
            log_om = [jnp.where(causal, t[1], 0.0) if diagonal else t[1] for t in terms]
            later = [_split_dot(log_om[hh], later_in_block) for hh in hs]
            a = [jnp.exp(terms[hh][0] + (run[hh] + later[hh])) for hh in hs]
            if diagonal:
                a = [jnp.where(causal, a[hh], 0.0) for hh in hs]
            out = []
            for hh in hs:
                vb = v_ref[pl.ds(koff, TQ), hh * HEAD_V:(hh + 1) * HEAD_V]
                out += [run[hh] + jnp.sum(log_om[hh], axis=1, keepdims=True),
                        acc[hh] + _dot(a[hh].astype(BF16), vb)]
            return tuple(out)

        zero = (jnp.zeros((TQ, 1), F32), jnp.zeros((TQ, HEAD_V), F32))
        carry = step(qi, zero * HEADS, True)
        carry = lax.fori_loop(0, qi, lambda i, c: step(qi - 1 - i, c, False), carry)
        for hh in hs:
            sl = slice(hh * HEAD_V, (hh + 1) * HEAD_V)
            acc = carry[2 * hh + 1]
            z = z_ref[:, sl].astype(F32)
            o_ref[:, sl] = acc.astype(BF16)
            y_ref[:, sl] = (acc * (z * _sigmoid(z))).astype(BF16)
            lt_ref[:, hh:hh + 1] = carry[2 * hh]

    qblk = lambda b, p, i: (b * nq + i, p)
    return pl.pallas_call(
        body, name=name, grid=(T // S, N_HEADS // HEADS, nq),
        in_specs=[pl.BlockSpec((TQ, QK_W), qblk),
                  pl.BlockSpec((S, QK_W), lambda b, p, i: (b, k_b0 + p)),
                  pl.BlockSpec((S, V_W), lambda b, p, i: (b, v_b0 + p)),
                  pl.BlockSpec((TQ, V_W), lambda b, p, i: (b * nq + i, z_b0 + p))],
        out_specs=[pl.BlockSpec((TQ, V_W), qblk),
                   pl.BlockSpec((TQ, V_W), qblk),
                   pl.BlockSpec((None, TQ, HEADS), lambda b, p, i: (p, b * nq + i, 0))],
        out_shape=[jax.ShapeDtypeStruct((T, D_INNER), BF16),
                   jax.ShapeDtypeStruct((T, D_INNER), BF16),
                   jax.ShapeDtypeStruct((N_HEADS // HEADS, T, HEADS), F32)],
        compiler_params=_params(3),
    )(qkvz, qkvz, qkvz, qkvz)


def _attn_gate_bwd(dh, w_out, qkvz, o, name, tm=1024, tn=512):
    T = dh.shape[0]
    gate_b0 = (2 * QK_WIDTH + D_INNER) // tn

    def body(dh_ref, w_ref, z_ref, o_ref, do_ref, dz_ref, dh_s):
        @pl.when(pl.program_id(1) == 0)
        def _():
            dh_s[...] = dh_ref[...].astype(BF16)

        dy = _dot_nt(dh_s[...], w_ref[...])
        z = z_ref[...].astype(F32)
        sig = _sigmoid(z)
        do_ref[...] = (dy * (z * sig)).astype(BF16)
        dz_ref[...] = (dy * o_ref[...].astype(F32) * (sig * (1.0 + z * (1.0 - sig)))).astype(BF16)

    return pl.pallas_call(
        body, name=name, grid=(T // tm, D_INNER // tn),
        in_specs=[pl.BlockSpec((tm, D_MODEL), lambda m, n: (m, 0)),
                  pl.BlockSpec((tn, D_MODEL), lambda m, n: (n, 0)),
                  pl.BlockSpec((tm, tn), lambda m, n: (m, gate_b0 + n)),
                  pl.BlockSpec((tm, tn), lambda m, n: (m, n))],
        out_specs=[pl.BlockSpec((tm, tn), lambda m, n: (m, n)),
                   pl.BlockSpec((tm, tn), lambda m, n: (m, gate_b0 + n))],
        out_shape=[jax.ShapeDtypeStruct((T, D_INNER), BF16),
                   jax.ShapeDtypeStruct((T, 2 * QK_WIDTH + 2 * D_INNER), BF16)],
        scratch_shapes=[pltpu.VMEM((tm, D_MODEL), BF16)],
        compiler_params=_params(2),
    )(dh, w_out, qkvz, o)


def _attn_bwd(qkv, do, ltot, dproj1, S, name):
    T = qkv.shape[0]
    nq = S // TQ
    HEADS, QK_W, V_W = BWD_HEADS, BWD_HEADS * 64, BWD_HEADS * HEAD_V
    k_b0 = QK_WIDTH // QK_W
    v_b0 = 2 * QK_WIDTH // V_W
    hs = range(HEADS)
    pairs = range(HEADS // 2)

    def body(q_ref, k_ref, v_ref, do_ref, lt_ref, _, out_ref, dq_s, dk_s, dv_s, dkb_s, dvb_s, sems):
        b, p = pl.program_id(0), pl.program_id(1)
        row = lax.broadcasted_iota(jnp.int32, (TQ, TQ), 0)
        col = lax.broadcasted_iota(jnp.int32, (TQ, TQ), 1)
        causal = col < row
        upto = (row <= col).astype(BF16)
        before = (row < col).astype(BF16)
        dk_s[...] = jnp.zeros_like(dk_s)
        dv_s[...] = jnp.zeros_like(dv_s)

        def q_block(qi, _):
            qoff = pl.multiple_of(qi * TQ, TQ)
            qms = [qm * 0.125 for qm in _masked_heads(q_ref[pl.ds(qoff, TQ), :], HEADS)]
            vsl = [slice(hh * HEAD_V, (hh + 1) * HEAD_V) for hh in hs]
            psl = [slice(pp * HEAD_PAIR_QK, (pp + 1) * HEAD_PAIR_QK) for pp in pairs]
            do_h = [do_ref[pl.ds(qoff, TQ), sl] for sl in vsl]
            total = [lt_ref[pl.ds(qoff, TQ), hh:hh + 1] for hh in hs]

            def k_block(j, carry, diagonal):
                koff = pl.multiple_of(j * TQ, TQ)
                kms = _masked_heads(k_ref[pl.ds(koff, TQ), :], HEADS)
                g_before = [carry[2 * hh] for hh in hs]
                lom_before = [carry[2 * hh + 1] for hh in hs]
                z = [_dot_nt(qms[hh], kms[hh]) for hh in hs]
                da = [_dot_nt(do_h[hh], v_ref[pl.ds(koff, TQ), vsl[hh]]) for hh in hs]
                terms = [_log_terms(z[hh]) for hh in hs]
                log_om = [jnp.where(causal, t[1], 0.0) if diagonal else t[1] for t in terms]
                prefix = [_split_dot(log_om[hh], upto) for hh in hs]
                a = [jnp.exp(terms[hh][0] + ((total[hh] - lom_before[hh]) - prefix[hh])) for hh in hs]
                if diagonal:
                    a = [jnp.where(causal, a[hh], 0.0) for hh in hs]
                g = [a[hh] * da[hh] for hh in hs]
                g_prefix = [_dot(g[hh].astype(BF16), before) for hh in hs]
                out, dzs = [], []
                for hh in hs:
                    beta = jnp.exp(terms[hh][0])
                    g_excl = (g_before[hh] + g_prefix[hh]) * beta
                    if diagonal:
                        g_excl = jnp.where(causal, g_excl, 0.0)
                    dzs.append((g[hh] * (1.0 - beta) - g_excl).astype(BF16))
                    out += [g_before[hh] + jnp.sum(g[hh], axis=1, keepdims=True),
                            lom_before[hh] + jnp.sum(log_om[hh], axis=1, keepdims=True)]
                for hh in hs:
                    dv_s[pl.ds(koff, TQ), vsl[hh]] += _dot_tn(a[hh].astype(BF16), do_h[hh])
                dq = []
                for pp in pairs:
                    pair = slice(2 * pp, 2 * pp + 2)
                    dq.append(carry[2 * HEADS + pp] + _dot(jnp.concatenate(dzs[pair], axis=1),
                                                           jnp.concatenate(kms[pair], axis=0)))
                    dk_s[pl.ds(koff, TQ), psl[pp]] += _dot_tn(jnp.concatenate(dzs[pair], axis=0),
                                                              jnp.concatenate(qms[pair], axis=0))
                return tuple(out) + tuple(dq)

            zero = jnp.zeros((TQ, 1), F32)
            carry = (zero,) * (2 * HEADS) + (jnp.zeros((TQ, HEAD_PAIR_QK), F32),) * (HEADS // 2)
            carry = lax.fori_loop(0, qi, lambda j, c: k_block(j, c, False), carry)
            carry = k_block(qi, carry, True)
            for pp in pairs:
                dq_s[pl.ds(qoff, TQ), psl[pp]] = (carry[2 * HEADS + pp] * 0.125).astype(BF16)
            return 0

        lax.fori_loop(0, nq, q_block, 0)
        dkb_s[...] = dk_s[...].astype(BF16)
        dvb_s[...] = dv_s[...].astype(BF16)
        rows = pl.ds(pl.multiple_of(b * S, TQ), S)
        copies = [
            pltpu.make_async_copy(
                dq_s, out_ref.at[rows, pl.ds(pl.multiple_of(p * QK_W, 128), QK_W)], sems.at[0]),
            pltpu.make_async_copy(
                dkb_s, out_ref.at[rows, pl.ds(pl.multiple_of(QK_WIDTH + p * QK_W, 128), QK_W)],
                sems.at[1]),
            pltpu.make_async_copy(
                dvb_s, out_ref.at[rows, pl.ds(pl.multiple_of(2 * QK_WIDTH + p * V_W, 128), V_W)],
                sems.at[2]),
        ]
        for cp in copies:
            cp.start()
        for cp in copies:
            cp.wait()

    return pl.pallas_call(
        body, name=name, grid=(T // S, N_HEADS // HEADS),
        in_specs=[pl.BlockSpec((S, QK_W), lambda b, p: (b, p)),
                  pl.BlockSpec((S, QK_W), lambda b, p: (b, k_b0 + p)),
                  pl.BlockSpec((S, V_W), lambda b, p: (b, v_b0 + p)),
                  pl.BlockSpec((S, V_W), lambda b, p: (b, p)),
                  pl.BlockSpec((None, S, HEADS), lambda b, p: (p, b, 0)),
                  HBM_SPEC],
        out_specs=HBM_SPEC,
        out_shape=jax.ShapeDtypeStruct(dproj1.shape, dproj1.dtype),
        input_output_aliases={5: 0},
        scratch_shapes=[pltpu.VMEM((S, QK_W), BF16),
                        pltpu.VMEM((S, QK_W), F32),
                        pltpu.VMEM((S, V_W), F32),
                        pltpu.VMEM((S, QK_W), BF16),
                        pltpu.VMEM((S, V_W), BF16),
                        pltpu.SemaphoreType.DMA((3,))],
        compiler_params=_params(2),
    )(qkv, qkv, qkv, do, ltot, dproj1)


def _loss_head(h, g_row, target, name, tm=512):
    T = h.shape[0]

    def body(h_ref, g_ref, t_ref, dh_ref, dg_ref, loss_ref):
        @pl.when(pl.program_id(0) == 0)
        def _():
            dg_ref[...] = jnp.zeros_like(dg_ref)
            loss_ref[...] = jnp.zeros_like(loss_ref)

        x = h_ref[...]
        inv = lax.rsqrt(jnp.mean(x * x, axis=-1, keepdims=True) + RMS_EPS)
        xhat = x * inv
        gain = g_ref[...]
        err = xhat * gain - t_ref[...]
        per_token = jnp.mean(err * err, axis=-1, keepdims=True)
        loss_ref[...] += 0.5 * jnp.sum(per_token, axis=0, keepdims=True)
        dy = err * (1.0 / D_MODEL)
        dg_ref[...] += jnp.sum(dy * xhat, axis=0, keepdims=True)
        dxh = dy * gain
        proj = jnp.mean(dxh * xhat, axis=-1, keepdims=True)
        dh_ref[...] = inv * (dxh - xhat * proj)

    return pl.pallas_call(
        body, name=name, grid=(T // tm,),
        in_specs=[pl.BlockSpec((tm, D_MODEL), lambda m: (m, 0)),
                  pl.BlockSpec((1, D_MODEL), lambda m: (0, 0)),
                  pl.BlockSpec((tm, D_MODEL), lambda m: (m, 0))],
        out_specs=[pl.BlockSpec((tm, D_MODEL), lambda m: (m, 0)),
                   pl.BlockSpec((1, D_MODEL), lambda m: (0, 0)),
                   pl.BlockSpec((1, 128), lambda m: (0, 0))],
        out_shape=[jax.ShapeDtypeStruct((T, D_MODEL), F32),
                   jax.ShapeDtypeStruct((1, D_MODEL), F32),
                   jax.ShapeDtypeStruct((1, 128), F32)],
        compiler_params=_params(1),
    )(h, g_row, target)


def _place():
    return lax.axis_index("x"), lax.axis_index("y"), lax.axis_index("c")


def _other_chips(x, y):
    return [(1 - x, y), (x, 1 - y), (1 - x, 1 - y)]


def _half(ref, c):
    hr = ref.shape[-2] // 2
    return pl.ds(pl.multiple_of(c * hr, 8), hr)


def _cast_to_slot(shard, chip, name, tr=256):
    R, C = shard.shape

    def body(chip_ref, w_ref, o_ref):
        o_ref[0] = w_ref[...].astype(BF16)

    return pl.pallas_call(
        body, name=name,
        grid_spec=pltpu.PrefetchScalarGridSpec(
            num_scalar_prefetch=1, grid=(R // tr,),
            in_specs=[pl.BlockSpec((tr, C), lambda i, chip_ref: (i, 0))],
            out_specs=pl.BlockSpec((1, tr, C), lambda i, chip_ref: (chip_ref[0], i, 0))),
        out_shape=jax.ShapeDtypeStruct((N_CHIPS, R, C), BF16),
        compiler_params=_params(1),
    )(chip, shard)


def _weight_plan(bufs):
    x, y, c = _place()
    plan = []
    for buf in bufs:
        mine = buf.at[2 * x + y, _half(buf, c)]
        for ox, oy in _other_chips(x, y):
            plan.append((mine, mine, (ox, oy, c), buf.at[2 * ox + oy, _half(buf, c)]))
    return plan


def _chip_sum_plan(bufs):
    x, y, c = _place()
    n = len(bufs) // 2
    plan = []
    for sums, land in zip(bufs[:n], bufs[n:]):
        for k, (ox, oy) in enumerate(_other_chips(x, y)):
            plan.append((sums.at[2 * ox + oy], land.at[k], (ox, oy, c), land.at[k]))
    return plan


SEM_SPEC = pl.BlockSpec(memory_space=pltpu.SEMAPHORE)
ANY_SPEC = pl.BlockSpec(memory_space=pl.ANY)
DATAFLOW = pltpu.SideEffectType.DATAFLOW_SIDE_EFFECTING


def _in_hbm(a):
    return pltpu.with_memory_space_constraint(a, pltpu.HBM)


def _exchange_start(bufs, after, plan, n_copies, name):
    nb = len(bufs)

    def body(*refs):
        send_sems, recv_sems = refs[nb + 1], refs[nb + 2]
        for i, (src, dst, dev, _) in enumerate(plan(refs[:nb])):
            pltpu.make_async_remote_copy(
                src_ref=src, dst_ref=dst, send_sem=send_sems.at[i], recv_sem=recv_sems.at[i],
                device_id=dev, device_id_type=MESH).start()
        token = refs[-1]
        token[...] = jnp.zeros_like(token)

    res = pl.pallas_call(
        body, name=name,
        in_specs=[HBM_SPEC] * nb + [ANY_SPEC],
        out_specs=[SEM_SPEC, SEM_SPEC] + [HBM_SPEC] * nb + [pl.BlockSpec(memory_space=pltpu.VMEM)],
        out_shape=[pltpu.SemaphoreType.DMA((n_copies,)), pltpu.SemaphoreType.DMA((n_copies,))]
        + [pltpu.HBM(b.shape, b.dtype) for b in bufs] + [jax.ShapeDtypeStruct((8, 128), F32)],
        input_output_aliases={i: 2 + i for i in range(nb)},
        compiler_params=pltpu.CompilerParams(has_side_effects=DATAFLOW),
    )(*[_in_hbm(b) for b in bufs], after)
    return res[0], res[1], list(res[2:2 + nb]), res[-1]


def _exchange_wait(bufs, send_sems, recv_sems, after, plan, name):
    nb = len(bufs)

    def body(*refs):
        sends, recvs = refs[nb], refs[nb + 1]
        for i, (src, dst, dev, landing) in enumerate(plan(refs[:nb])):
            pltpu.make_async_remote_copy(
                src_ref=src, dst_ref=landing, send_sem=sends.at[i], recv_sem=recvs.at[i],
                device_id=dev, device_id_type=MESH).wait()

    res = pl.pallas_call(
        body, name=name,
        in_specs=[HBM_SPEC] * nb + [SEM_SPEC, SEM_SPEC, ANY_SPEC],
        out_specs=[HBM_SPEC] * nb,
        out_shape=[pltpu.HBM(b.shape, b.dtype) for b in bufs],
        input_output_aliases={i: i for i in range(nb)},
        compiler_params=pltpu.CompilerParams(has_side_effects=DATAFLOW),
    )(*bufs, send_sems, recv_sems, after)
    return list(res)


def _allgather_weights(slots, name, landed=False):
    n = len(slots)

    def body(*refs):
        outs = refs[n:2 * n]
        send_sems, recv_sems, fwd_send, fwd_recv = refs[2 * n:]
        x, y, c = _place()
        chips = _other_chips(x, y)

        def landing(a, chip, half_of):
            return outs[a].at[2 * chip[0] + chip[1], _half(outs[a], half_of)]

        def ici(a, k, chip_from, to):
            return pltpu.make_async_remote_copy(
                src_ref=landing(a, chip_from, c), dst_ref=landing(a, chip_from, c),
                send_sem=send_sems.at[a, k], recv_sem=recv_sems.at[a, k],
                device_id=to, device_id_type=MESH)

        def d2d(a, k, chip_from, half_of):
            return pltpu.make_async_remote_copy(
                src_ref=landing(a, chip_from, half_of), dst_ref=landing(a, chip_from, half_of),
                send_sem=fwd_send.at[a, k], recv_sem=fwd_recv.at[a, k],
                device_id=(x, y, 1 - c), device_id_type=MESH)

        sends = []
        if not landed:
            sends = [ici(a, k, (x, y), (*chips[k], c)) for a in range(n) for k in range(3)]
        for cp in sends:
            cp.start()
        forwards = []
        for a in range(n):
            for k in range(3):
                if not landed:
                    ici(a, k, chips[k], (x, y, c)).wait_recv()
                fw = d2d(a, k, chips[k], c)
                fw.start()
                forwards.append(fw)
        for a in range(n):
            for k in range(3):
                d2d(a, k, chips[k], 1 - c).wait_recv()
        for cp in sends + forwards:
            cp.wait_send()

    return pl.pallas_call(
        body, name=name,
        in_specs=[HBM_SPEC] * n, out_specs=[HBM_SPEC] * n,
        out_shape=[jax.ShapeDtypeStruct(s.shape, s.dtype) for s in slots],
        input_output_aliases={a: a for a in range(n)},
        scratch_shapes=[pltpu.SemaphoreType.DMA((n, 3)), pltpu.SemaphoreType.DMA((n, 3)),
                        pltpu.SemaphoreType.DMA((n, 3)), pltpu.SemaphoreType.DMA((n, 3))],
    )(*slots)


def _sibling_exchange(partials, small, name):
    n = len(partials)
    ns = 0 if small is None else 1

    def body(*refs):
        ins, outs = refs[:n], refs[n + ns:2 * n + ns]
        send_sems, recv_sems = refs[2 * (n + ns):2 * (n + ns) + 2]
        x, y, c = _place()
        me = 4 * x + 2 * y + c
        sends = [pltpu.make_async_remote_copy(
            src_ref=ins[a].at[:, _half(ins[a], 1 - c)], dst_ref=outs[a],
            send_sem=send_sems.at[a], recv_sem=recv_sems.at[a],
            device_id=(x, y, 1 - c), device_id_type=MESH) for a in range(n)]
        if ns:
            small_ref, small_all = refs[n], refs[2 * n + 1]
            s_send, s_recv, loc_sem = refs[2 * (n + ns) + 2:]
            local = pltpu.make_async_copy(small_ref, small_all.at[me], loc_sem)
            local.start()
            for d in range(1, N_DEV):
                px, py, pc = x ^ ((d >> 2) & 1), y ^ ((d >> 1) & 1), c ^ (d & 1)
                sends.append(pltpu.make_async_remote_copy(
                    src_ref=small_ref, dst_ref=small_all.at[me],
                    send_sem=s_send.at[d - 1], recv_sem=s_recv.at[d - 1],
                    device_id=(px, py, pc), device_id_type=MESH))
        for cp in sends:
            cp.start()
        if ns:
            for d in range(1, N_DEV):
                pltpu.make_async_remote_copy(
                    src_ref=small_ref, dst_ref=small_all.at[me ^ d],
                    send_sem=s_send.at[d - 1], recv_sem=s_recv.at[d - 1],
                    device_id=(x, y, c), device_id_type=MESH).wait_recv()
        for cp in sends[:n]:
            cp.wait_recv()
        for cp in sends:
            cp.wait_send()
        if ns:
            local.wait()

    out_shape = [jax.ShapeDtypeStruct((N_CHIPS, p.shape[1] // 2, p.shape[2]), F32) for p in partials]
    scratch = [pltpu.SemaphoreType.DMA((max(n, 1),)), pltpu.SemaphoreType.DMA((max(n, 1),))]
    if ns:
        out_shape.append(jax.ShapeDtypeStruct((N_DEV,) + small.shape, F32))
        scratch += [pltpu.SemaphoreType.DMA((N_DEV - 1,)), pltpu.SemaphoreType.DMA((N_DEV - 1,)),
                    pltpu.SemaphoreType.DMA]
    return pl.pallas_call(
        body, name=name,
        in_specs=[HBM_SPEC] * (n + ns), out_specs=[HBM_SPEC] * (n + ns),
        out_shape=out_shape, scratch_shapes=scratch,
    )(*partials, *([small] if ns else []))


def _chip_sum(partial, from_sibling, c, name, tr=256):
    _, hr, C = from_sibling.shape
    nb = hr // tr

    def body(c_ref, p_ref, s_ref, o_ref):
        o_ref[...] = (p_ref[...] + s_ref[...]).astype(BF16)

    return pl.pallas_call(
        body, name=name,
        grid_spec=pltpu.PrefetchScalarGridSpec(
            num_scalar_prefetch=1, grid=(N_CHIPS, nb),
            in_specs=[pl.BlockSpec((1, tr, C), lambda j, i, c_ref: (j, c_ref[0] * nb + i, 0)),
                      pl.BlockSpec((1, tr, C), lambda j, i, c_ref: (j, i, 0))],
            out_specs=pl.BlockSpec((1, tr, C), lambda j, i, c_ref: (j, i, 0))),
        out_shape=jax.ShapeDtypeStruct(from_sibling.shape, BF16),
        compiler_params=_params(2),
    )(c, partial, from_sibling)


def _reduce_half(partial, from_sibling, received, place, name, tr=256):
    _, hr, C = from_sibling.shape
    nb = hr // tr

    def body(p_ref, mine_ref, sib_ref, r_ref, o_ref):
        acc = mine_ref[0] + sib_ref[0]
        for k in range(3):
            acc = acc + r_ref[k].astype(F32)
        o_ref[...] = acc

    return pl.pallas_call(
        body, name=name,
        grid_spec=pltpu.PrefetchScalarGridSpec(
            num_scalar_prefetch=1, grid=(nb,),
            in_specs=[pl.BlockSpec((1, tr, C), lambda i, p: (p[0], p[1] * nb + i, 0)),
                      pl.BlockSpec((1, tr, C), lambda i, p: (p[0], i, 0)),
                      pl.BlockSpec((3, tr, C), lambda i, p: (0, i, 0))],
            out_specs=pl.BlockSpec((tr, C), lambda i, p: (p[1] * nb + i, 0))),
        out_shape=jax.ShapeDtypeStruct((2 * hr, C), F32),
        compiler_params=_params(1),
    )(place, partial, from_sibling, received)


def _join_halves(fulls, name):
    n = len(fulls)

    def body(*refs):
        outs = refs[n:2 * n]
        send_sems, recv_sems = refs[2 * n:]
        x, y, c = _place()

        def copy(a, half_of, to):
            rows = outs[a].at[_half(outs[a], half_of)]
            return pltpu.make_async_remote_copy(
                src_ref=rows, dst_ref=rows, send_sem=send_sems.at[a], recv_sem=recv_sems.at[a],
                device_id=to, device_id_type=MESH)

        sends = [copy(a, c, (x, y, 1 - c)) for a in range(n)]
        for cp in sends:
            cp.start()
        for a in range(n):
            copy(a, 1 - c, (x, y, c)).wait_recv()
        for cp in sends:
            cp.wait_send()

    return pl.pallas_call(
        body, name=name,
        in_specs=[HBM_SPEC] * n, out_specs=[HBM_SPEC] * n,
        out_shape=[jax.ShapeDtypeStruct(f.shape, F32) for f in fulls],
        input_output_aliases={a: a for a in range(n)},
        scratch_shapes=[pltpu.SemaphoreType.DMA((n,)), pltpu.SemaphoreType.DMA((n,))],
    )(*fulls)


def _adamw_math(w, g, m, v):
    m = ADAM_B1 * m + (1.0 - ADAM_B1) * g
    v = ADAM_B2 * v + (1.0 - ADAM_B2) * (g * g)
    m_hat = m / (1.0 - ADAM_B1 ** ADAM_STEP)
    v_hat = v / (1.0 - ADAM_B2 ** ADAM_STEP)
    delta = -ADAM_LR * (m_hat / (jnp.sqrt(v_hat) + ADAM_EPS) + ADAM_WD * w)
    return delta, m, v


def _adamw(w, g, m, v, name, tr=256):
    R, C = w.shape
    tr = min(tr, R)

    def body(w_ref, g_ref, m_ref, v_ref, d_out, m_out, v_out):
        d_out[...], m_out[...], v_out[...] = _adamw_math(w_ref[...], g_ref[...], m_ref[...], v_ref[...])

    spec = pl.BlockSpec((tr, C), lambda i: (i, 0))
    return pl.pallas_call(
        body, name=name, grid=(R // tr,),
        in_specs=[spec] * 4, out_specs=[spec] * 3,
        out_shape=[jax.ShapeDtypeStruct((R, C), F32)] * 3,
        compiler_params=_params(1),
    )(w, g, m, v)


def _adamw_small(small_all, w, m, v, name):
    def body(s_ref, w_ref, m_ref, v_ref, g_out, d_out, m_out, v_out):
        g = s_ref[0]
        for d in range(1, N_DEV):
            g = g + s_ref[d]
        g_out[...] = g
        d_out[...], m_out[...], v_out[...] = _adamw_math(w_ref[...], g, m_ref[...], v_ref[...])

    vm = pl.BlockSpec(memory_space=pltpu.VMEM)
    return pl.pallas_call(
        body, name=name, in_specs=[vm] * 4, out_specs=[vm] * 4,
        out_shape=[jax.ShapeDtypeStruct(w.shape, F32)] * 4,
    )(small_all, w, m, v)


def _pack_small(norm_g, pool_scale, norm_f, extra_row):
    return jnp.concatenate([norm_g.reshape(2, D_MODEL), pool_scale.reshape(2, D_MODEL),
                            norm_f.reshape(1, D_MODEL), extra_row,
                            jnp.zeros((2, D_MODEL), F32)], axis=0)


def kernel(x, norm_g, pool_w_in, pool_w, pool_scale, pool_w_out, sb_w_in, sb_w_out, norm_f, loss_target, m_norm_g, m_pool_w_in, m_pool_w, m_pool_scale, m_pool_w_out, m_sb_w_in, m_sb_w_out, m_norm_f, v_norm_g, v_pool_w_in, v_pool_w, v_pool_scale, v_pool_w_out, v_sb_w_in, v_sb_w_out, v_norm_f):
    nb, S, _ = x.shape
    T = nb * S
    xt = x.reshape(T, D_MODEL)
    target = loss_target.reshape(T, D_MODEL)
    cx, cy, cc = _place()

    def shard2d(w):
        return w.reshape(-1, w.shape[-1])

    names = ("pool_w_in", "pool_w", "pool_w_out", "sb_w_in", "sb_w_out")
    w_shards = [shard2d(w) for w in (pool_w_in, pool_w, pool_w_out, sb_w_in, sb_w_out)]
    m_shards = [shard2d(w) for w in (m_pool_w_in, m_pool_w, m_pool_w_out, m_sb_w_in, m_sb_w_out)]
    v_shards = [shard2d(w) for w in (v_pool_w_in, v_pool_w, v_pool_w_out, v_sb_w_in, v_sb_w_out)]

    chip = (2 * cx + cy).reshape(1).astype(jnp.int32)
    c_arr = cc.reshape(1).astype(jnp.int32)
    place = jnp.stack([2 * cx + cy, cc]).astype(jnp.int32)
    slots = [_cast_to_slot(w, chip, "cast_" + nm) for w, nm in zip(w_shards, names)]
    g0, g1, gf = norm_g[0:1], norm_g[1:2], norm_f.reshape(1, D_MODEL)

    w_pin, w_g, w_pout = _allgather_weights(slots[:3], "allgather_pool_weights")
    w_pout = w_pout.reshape(D_INNER, D_MODEL)
    sb_send, sb_recv, sb_slots, token = _exchange_start(slots[3:], w_pin, _weight_plan, 6,
                                                        "sb_weights_start")

    proj0, u0 = _rms_matmul(xt, g0 + token[0:1, 0:1], w_pin, [(2 * D_INNER, BF16)], "pool_in_proj",
                            tn=w_pin.shape[2])
    y0, pooled, mixed = _pool_fwd(proj0, w_g, pool_scale, S, "pool_mix")
    sb_slots = _exchange_wait(sb_slots, sb_send, sb_recv, y0, _weight_plan, "sb_weights_wait")
    w_sin, w_sout = _allgather_weights(sb_slots, "sb_weights_forward", landed=True)
    w_sout = w_sout.reshape(D_INNER, D_MODEL)
    h1 = _matmul_residual(y0, w_pout, xt, "pool_out_proj")
    n1 = 2 * QK_WIDTH + 2 * D_INNER
    qkvz, u1 = _rms_matmul(h1, g1, w_sin, [(n1, BF16)], "sb_in_proj", tn=w_sin.shape[2])
    o, y1, ltot = _attn_fwd(qkvz, S, "sb_attention")
    h2 = _matmul_residual(y1, w_sout, h1, "sb_out_proj")
    dh2, d_norm_f, loss_row = _loss_head(h2, gf, target, "loss_head")

    def reduce_start(partials, tag):
        from_sibling = _sibling_exchange(partials, None, "grad_sibling_exchange_" + tag)
        sums = [_chip_sum(p, s, c_arr, "grad_chip_sum_%s_%d" % (tag, i))
                for i, (p, s) in enumerate(zip(partials, from_sibling))]
        lands = [lax.empty((3,) + s.shape[1:], BF16) for s in sums]
        send, recv, bufs, token = _exchange_start(sums + lands, c_arr, _chip_sum_plan, 3 * len(sums),
                                                  "grad_chip_exchange_start_" + tag)
        return (partials, list(from_sibling), send, recv, bufs), token[0:1, 0:1]

    def reduce_finish(started, after, tag):
        partials, from_sibling, send, recv, bufs = started
        received = _exchange_wait(bufs, send, recv, after, _chip_sum_plan,
                                  "grad_chip_exchange_wait_" + tag)[len(partials):]
        return [_reduce_half(p, s, r, place, "grad_reduce_%s_%d" % (tag, i))
                for i, (p, s, r) in enumerate(zip(partials, from_sibling, received))]

    shard = lambda i, j, t: (j, 0, 0)
    gw_sout = _matmul_tn(y1, dh2, D_INNER, D_MODEL, (D_INNER, D_MODEL), (1024, 1024),
                         lambda i, j, t: (i, j), "grad_sb_w_out", bm=1024, bn=1024)
    sout_started, _ = reduce_start([gw_sout.reshape(N_CHIPS, -1, D_MODEL)], "sb_out")
    do, dproj1 = _attn_gate_bwd(dh2, w_sout, qkvz, o, "sb_gate_bwd")
    dproj1 = _attn_bwd(qkvz, do, ltot, dproj1, S, "sb_attention_bwd")
    gw_sin = _matmul_tn(u1, dproj1, D_MODEL, n1, (N_CHIPS, D_MODEL, n1 // 4), (1, D_MODEL, n1 // 4),
                        shard, "grad_sb_w_in", bm=D_MODEL, bn=n1 // 4)
    sin_started, token = reduce_start([gw_sin], "sb_in")
    dh1, d_g1 = _matmul_nt_rms_bwd(dproj1, w_sin, h1, g1 + token, dh2, "sb_in_bwd", tk=w_sin.shape[2])
    gw_pout = _matmul_tn(y0, dh1, D_INNER, D_MODEL, (D_INNER, D_MODEL), (1024, 1024),
                         lambda i, j, t: (i, j), "grad_pool_w_out", bm=1024, bn=1024)
    dmixed, dproj0, d_scale = _pool_gate_bwd(dh1, w_pout, proj0, mixed, pool_scale, "pool_gate_bwd")
    gw_g = _matmul_tn(pooled, dmixed, D_INNER, D_INNER, (N_CHIPS, GROUP_DIM, GROUP_DIM),
                      (N_CHIPS, GROUP_DIM // N_CHIPS, GROUP_DIM), lambda i, j, t: (0, i, 0),
                      "grad_pool_w", bm=GROUP_DIM, bn=GROUP_DIM, diagonal_blocks=True)
    mix_started, _ = reduce_start([gw_g, gw_pout.reshape(N_CHIPS, -1, D_MODEL)], "pool_mix")
    dproj0 = _pool_bwd(dmixed, w_g, dproj0, S, "pool_bwd")
    n0 = 2 * D_INNER
    gw_pin = _matmul_tn(u0, dproj0, D_MODEL, n0, (N_CHIPS, D_MODEL, n0 // 4), (1, D_MODEL, n0 // 4),
                        shard, "grad_pool_w_in", bm=D_MODEL, bn=n0 // 4)
    pin_started, token = reduce_start([gw_pin], "pool_in")
    dx, d_g0 = _matmul_nt_rms_bwd(dproj0, w_pin, xt, g0 + token, dh1, "pool_in_bwd", tk=w_pin.shape[2])

    small = _pack_small(jnp.concatenate([d_g0, d_g1], axis=0), d_scale, d_norm_f,
                        jnp.broadcast_to(loss_row[:, :1], (1, D_MODEL)))
    small_all, = _sibling_exchange([], small, "small_sums_exchange")
    grads = _join_halves(reduce_finish(pin_started, dx, "pool_in")
                         + reduce_finish(mix_started, dx, "pool_mix")
                         + reduce_finish(sin_started, dx, "sb_in")
                         + reduce_finish(sout_started, dx, "sb_out"), "grad_join_halves")

    deltas, new_m, new_v = [], [], []
    for w, g, m, v, nm in zip(w_shards, grads, m_shards, v_shards, names):
        d, mm, vv = _adamw(w, g, m, v, "adamw_" + nm)
        deltas.append(d)
        new_m.append(mm)
        new_v.append(vv)

    zero_row = jnp.zeros((1, D_MODEL), F32)
    g_small, d_small, m_small, v_small = _adamw_small(
        small_all, _pack_small(norm_g, pool_scale, norm_f, zero_row),
        _pack_small(m_norm_g, m_pool_scale, m_norm_f, zero_row),
        _pack_small(v_norm_g, v_pool_scale, v_norm_f, zero_row + 1.0), "adamw_small")
    loss = g_small[5, 0]

    def unpack_small(a):
        return a[0:2], a[2:4].reshape(1, D_INNER), a[4]

    def assemble(big, small3):
        ng, ps, nf = small3
        return [ng, big[0].reshape(pool_w_in.shape), big[1].reshape(pool_w.shape), ps,
                big[2].reshape(pool_w_out.shape), big[3].reshape(sb_w_in.shape),
                big[4].reshape(sb_w_out.shape), nf]

    return (loss, dx.reshape(x.shape),
            *assemble(grads, unpack_small(g_small)),
            *assemble(deltas, unpack_small(d_small)),
            *assemble(new_m, unpack_small(m_small)),
            *assemble(new_v, unpack_small(v_small)))
```

```python
import functools

import jax
import jax.numpy as jnp
from jax import lax
from jax.experimental import pallas as pl
from jax.experimental.pallas import tpu as pltpu

F32 = jnp.float32
BF16 = jnp.bfloat16
MESH = pl.DeviceIdType.MESH

D_MODEL = 1024
D_INNER = 2048
N_GROUPS = 4
GROUP_DIM = 512
HEAD_PAIR_QK = 128
HEAD_V = 128
N_HEAD_PAIRS = 8
QK_WIDTH = 1024
RMS_EPS = 1e-6
HALO = 16
N_CHIPS = 4
N_DEV = 8

ADAM_LR = 0.001
ADAM_B1 = 0.9
ADAM_B2 = 0.999
ADAM_EPS = 1e-08
ADAM_WD = 0.01
ADAM_STEP = 10

VMEM_LIMIT = 56 * 1024 * 1024

HBM_SPEC = pl.BlockSpec(memory_space=pltpu.HBM)


def _params(n_axes):
    return pltpu.CompilerParams(dimension_semantics=("arbitrary",) * n_axes,
                                vmem_limit_bytes=VMEM_LIMIT)


def _dot(a, b):
    return jnp.dot(a, b, preferred_element_type=F32)


def _dot_nt(a, b):
    return lax.dot_general(a, b, (((1,), (1,)), ((), ())), preferred_element_type=F32)


def _dot_tn(a, b):
    return lax.dot_general(a, b, (((0,), (0,)), ((), ())), preferred_element_type=F32)


def _sigmoid(z):
    return 1.0 / (1.0 + jnp.exp(-z))


def _rms_matmul(h, g_row, w4, outs, name, tm=1024, tn=512):
    T = h.shape[0]
    per_shard = w4.shape[2] // tn
    starts = [0]
    for width, _ in outs:
        starts.append(starts[-1] + width // tn)
    n_out = len(outs)

    def body(h_ref, g_ref, w_ref, *rest):
        o_refs, u_out, u_s = rest[:n_out], rest[n_out], rest[n_out + 1]
        n = pl.program_id(1)

        @pl.when(n == 0)
        def _():
            x = h_ref[...]
            inv = lax.rsqrt(jnp.mean(x * x, axis=-1, keepdims=True) + RMS_EPS)
            u = (x * inv * g_ref[...]).astype(BF16)
            u_s[...] = u
            u_out[...] = u

        res = _dot(u_s[...], w_ref[0])
        for k in range(n_out):
            @pl.when((n >= starts[k]) & (n < starts[k + 1]))
            def _():
                o_refs[k][...] = res.astype(o_refs[k].dtype)

    def out_map(k):
        return lambda m, n: (m, jnp.clip(n - starts[k], 0, starts[k + 1] - starts[k] - 1))

    return pl.pallas_call(
        body, name=name, grid=(T // tm, starts[-1]),
        in_specs=[pl.BlockSpec((tm, D_MODEL), lambda m, n: (m, 0)),
                  pl.BlockSpec((1, D_MODEL), lambda m, n: (0, 0)),
                  pl.BlockSpec((1, D_MODEL, tn), lambda m, n: (n // per_shard, 0, n % per_shard))],
        out_specs=[pl.BlockSpec((tm, tn), out_map(k)) for k in range(n_out)]
        + [pl.BlockSpec((tm, D_MODEL), lambda m, n: (m, 0))],
        out_shape=[jax.ShapeDtypeStruct((T, width), dt) for width, dt in outs]
        + [jax.ShapeDtypeStruct((T, D_MODEL), BF16)],
        scratch_shapes=[pltpu.VMEM((tm, D_MODEL), BF16)],
        compiler_params=_params(2),
    )(h, g_row, w4)


def _matmul_residual(a, w, res, name, tm=1024, tn=512):
    T, K = a.shape
    N = w.shape[1]

    def body(a_ref, w_ref, r_ref, o_ref):
        o_ref[...] = r_ref[...] + _dot(a_ref[...], w_ref[...])

    return pl.pallas_call(
        body, name=name, grid=(T // tm, N // tn),
        in_specs=[pl.BlockSpec((tm, K), lambda m, n: (m, 0)),
                  pl.BlockSpec((K, tn), lambda m, n: (0, n)),
                  pl.BlockSpec((tm, tn), lambda m, n: (m, n))],
        out_specs=pl.BlockSpec((tm, tn), lambda m, n: (m, n)),
        out_shape=jax.ShapeDtypeStruct((T, N), F32),
        compiler_params=_params(2),
    )(a, w, res)


def _matmul_tn(a, b, a_cols, b_cols, out_shape, out_block, out_map, name, bm, bn, tk=512,
               diagonal_blocks=False):
    T = a.shape[0]

    def body(a_ref, b_ref, o_ref):
        @pl.when(pl.program_id(2) == 0)
        def _():
            o_ref[...] = jnp.zeros_like(o_ref)

        part = _dot_tn(a_ref[...].astype(BF16), b_ref[...].astype(BF16))
        o_ref[...] += part.reshape(o_ref.shape)

    b_map = (lambda i, j, t: (t, i)) if diagonal_blocks else (lambda i, j, t: (t, j))
    return pl.pallas_call(
        body, name=name, grid=(a_cols // bm, 1 if diagonal_blocks else b_cols // bn, T // tk),
        in_specs=[pl.BlockSpec((tk, bm), lambda i, j, t: (t, i)),
                  pl.BlockSpec((tk, bn), b_map)],
        out_specs=pl.BlockSpec(out_block, out_map),
        out_shape=jax.ShapeDtypeStruct(out_shape, F32),
        compiler_params=_params(3),
    )(a, b)


def _matmul_nt_rms_bwd(dproj, w4, h, g_row, dres, name, tm=1024, tk=512):
    T, cols = dproj.shape
    per_shard = w4.shape[2] // tk
    nk = cols // tk

    def body(dp_ref, w_ref, h_ref, g_ref, r_ref, dx_ref, dg_ref, acc):
        m, k = pl.program_id(0), pl.program_id(1)

        @pl.when(k == 0)
        def _():
            acc[...] = jnp.zeros_like(acc)

        @pl.when((k == 0) & (m == 0))
        def _():
            dg_ref[...] = jnp.zeros_like(dg_ref)

        acc[...] += _dot_nt(dp_ref[...], w_ref[0])

        @pl.when(k == nk - 1)
        def _():
            du = acc[...]
            x = h_ref[...]
            inv = lax.rsqrt(jnp.mean(x * x, axis=-1, keepdims=True) + RMS_EPS)
            xhat = x * inv
            dg_ref[...] += jnp.sum(du * xhat, axis=0, keepdims=True)
            dxh = du * g_ref[...]
            proj = jnp.mean(dxh * xhat, axis=-1, keepdims=True)
            dx_ref[...] = r_ref[...] + inv * (dxh - xhat * proj)

    return pl.pallas_call(
        body, name=name, grid=(T // tm, nk),
        in_specs=[pl.BlockSpec((tm, tk), lambda m, k: (m, k)),
                  pl.BlockSpec((1, D_MODEL, tk), lambda m, k: (k // per_shard, 0, k % per_shard)),
                  pl.BlockSpec((tm, D_MODEL), lambda m, k: (m, 0)),
                  pl.BlockSpec((1, D_MODEL), lambda m, k: (0, 0)),
                  pl.BlockSpec((tm, D_MODEL), lambda m, k: (m, 0))],
        out_specs=[pl.BlockSpec((tm, D_MODEL), lambda m, k: (m, 0)),
                   pl.BlockSpec((1, D_MODEL), lambda m, k: (0, 0))],
        out_shape=[jax.ShapeDtypeStruct((T, D_MODEL), F32),
                   jax.ShapeDtypeStruct((1, D_MODEL), F32)],
        scratch_shapes=[pltpu.VMEM((tm, D_MODEL), F32)],
        compiler_params=_params(2),
    )(dproj, w4, h, g_row, dres)


def _window_of(g):
    return jnp.left_shift(2, g)


def _select_stage(g, stages):
    res = stages[0]
    for i in range(1, len(stages)):
        res = jnp.where(g >= i, stages[i], res)
    return res


def _pool_fwd(proj0, wg4, scale_row, S, name, tm=1024):
    T = proj0.shape[0]
    tm = min(tm, S)
    blocks_per_seq = S // tm
    hb = tm // HALO

    def body(x_ref, halo_ref, z_ref, w_ref, s_ref, y_ref, p_ref, mix_ref):
        m, g = pl.program_id(0), pl.program_id(1)
        first = (m % blocks_per_seq) == 0
        halo = jnp.where(first, 0.0, halo_ref[...].astype(F32))
        x = x_ref[...].astype(F32)
        ext = jnp.concatenate([halo, x], axis=0)
        stages = []
        cur = ext
        for sh in (1, 2, 4, 8):
            cur = cur + pltpu.roll(cur, sh, 0)
            stages.append(cur[HALO:, :])
        win_sum = _select_stage(g, stages)
        pos = (m % blocks_per_seq) * tm + lax.broadcasted_iota(jnp.int32, (tm, 1), 0)
        count = jnp.minimum(pos + 1, _window_of(g)).astype(F32)
        pooled = win_sum / count - x
        pooled_b = pooled.astype(BF16)
        mixed = _dot(pooled_b, w_ref[...].reshape(GROUP_DIM, GROUP_DIM))
        z = z_ref[...].astype(F32)
        y_ref[...] = (mixed * s_ref[...] * (z * _sigmoid(z))).astype(BF16)
        p_ref[...] = pooled_b
        mix_ref[...] = mixed.astype(BF16)

    blk = lambda m, g: (m, g)
    return pl.pallas_call(
        body, name=name, grid=(T // tm, N_GROUPS),
        in_specs=[pl.BlockSpec((tm, GROUP_DIM), blk),
                  pl.BlockSpec((HALO, GROUP_DIM), lambda m, g: (jnp.maximum(m * hb - 1, 0), g)),
                  pl.BlockSpec((tm, GROUP_DIM), lambda m, g: (m, N_GROUPS + g)),
                  pl.BlockSpec((N_CHIPS, GROUP_DIM // N_CHIPS, GROUP_DIM), lambda m, g: (0, g, 0)),
                  pl.BlockSpec((1, GROUP_DIM), lambda m, g: (0, g))],
        out_specs=[pl.BlockSpec((tm, GROUP_DIM), blk)] * 3,
        out_shape=[jax.ShapeDtypeStruct((T, D_INNER), BF16)] * 3,
        compiler_params=_params(2),
    )(proj0, proj0, proj0, wg4, scale_row)


def _pool_gate_bwd(dh, w_out, proj0, mixed, scale_row, name, tm=1024, tn=512):
    T = dh.shape[0]
    gate_b0 = D_INNER // tn

    def body(dh_ref, w_ref, z_ref, mix_ref, s_ref, dm_ref, dz_ref, ds_ref):
        @pl.when(pl.program_id(1) == 0)
        def _():
            ds_ref[...] = jnp.zeros_like(ds_ref)

        dy = _dot_nt(dh_ref[...].astype(BF16), w_ref[...])
        z = z_ref[...].astype(F32)
        sig = _sigmoid(z)
        silu = z * sig
        mixed = mix_ref[...].astype(F32)
        s = s_ref[...]
        dm_ref[...] = (dy * s * silu).astype(BF16)
        dz_ref[...] = (dy * mixed * s * (sig * (1.0 + z * (1.0 - sig)))).astype(BF16)
        ds_ref[...] += jnp.sum(dy * mixed * silu, axis=0, keepdims=True)

    return pl.pallas_call(
        body, name=name, grid=(D_INNER // tn, T // tm),
        in_specs=[pl.BlockSpec((tm, D_MODEL), lambda n, m: (m, 0)),
                  pl.BlockSpec((tn, D_MODEL), lambda n, m: (n, 0)),
                  pl.BlockSpec((tm, tn), lambda n, m: (m, gate_b0 + n)),
                  pl.BlockSpec((tm, tn), lambda n, m: (m, n)),
                  pl.BlockSpec((1, tn), lambda n, m: (0, n))],
        out_specs=[pl.BlockSpec((tm, tn), lambda n, m: (m, n)),
                   pl.BlockSpec((tm, tn), lambda n, m: (m, gate_b0 + n)),
                   pl.BlockSpec((1, tn), lambda n, m: (0, n))],
        out_shape=[jax.ShapeDtypeStruct((T, D_INNER), BF16),
                   jax.ShapeDtypeStruct((T, 2 * D_INNER), BF16),
                   jax.ShapeDtypeStruct((1, D_INNER), F32)],
        compiler_params=_params(2),
    )(dh, w_out, proj0, mixed, scale_row)


def _pool_bwd(dmixed, wg4, dproj0, after, S, name, tm=1024):
    T = dmixed.shape[0]
    tm = min(tm, S)
    blocks_per_seq = S // tm
    hb = tm // HALO
    n_halo_blocks = T // HALO

    def body(dm_ref, halo_ref, w_ref, _, __, o_ref):
        m, g = pl.program_id(0), pl.program_id(1)
        ext = jnp.concatenate([dm_ref[...], halo_ref[...]], axis=0)
        dp = _dot_nt(ext, w_ref[...].reshape(GROUP_DIM, GROUP_DIM))
        pos = (m % blocks_per_seq) * tm + lax.broadcasted_iota(jnp.int32, (tm + HALO, 1), 0)
        count = jnp.minimum(pos + 1, _window_of(g)).astype(F32)
        c = jnp.where(pos < S, dp / count, 0.0)
        n = tm + HALO
        stages = []
        cur = c
        for sh in (1, 2, 4, 8):
            cur = cur + pltpu.roll(cur, n - sh, 0)
            stages.append(cur[:tm, :])
        o_ref[...] = (_select_stage(g, stages) - dp[:tm, :]).astype(BF16)

    blk = lambda m, g: (m, g)
    return pl.pallas_call(
        body, name=name, grid=(T // tm, N_GROUPS),
        in_specs=[pl.BlockSpec((tm, GROUP_DIM), blk),
                  pl.BlockSpec((HALO, GROUP_DIM),
                               lambda m, g: (jnp.minimum((m + 1) * hb, n_halo_blocks - 1), g)),
                  pl.BlockSpec((N_CHIPS, GROUP_DIM // N_CHIPS, GROUP_DIM), lambda m, g: (0, g, 0)),
                  HBM_SPEC, ANY_SPEC],
        out_specs=pl.BlockSpec((tm, GROUP_DIM), blk),
        out_shape=jax.ShapeDtypeStruct(dproj0.shape, dproj0.dtype),
        input_output_aliases={3: 0},
        compiler_params=_params(2),
    )(dmixed, dmixed, wg4, dproj0, after)


TQ = 256


def _split_dot(x, m):
    hi = x.astype(BF16)
    lo = (x - hi.astype(F32)).astype(BF16)
    return _dot(hi, m) + _dot(lo, m)


NEG_LOG2E = -1.4426950408889634


def _log_terms(z):
    soft = jnp.log(1.0 + jnp.exp2(jnp.abs(z) * NEG_LOG2E))
    log_beta = jnp.minimum(z, 0.0) - soft
    return log_beta, log_beta - z


N_HEADS = 16
FWD_HEADS = BWD_HEADS = 4


def _masked_heads(x, heads):
    lane = lax.broadcasted_iota(jnp.int32, (1, HEAD_PAIR_QK), 1)
    out = []
    for hh in range(heads):
        slab = x[:, (hh // 2) * HEAD_PAIR_QK:(hh // 2 + 1) * HEAD_PAIR_QK]
        out.append(jnp.where((lane // 64) == hh % 2, slab, jnp.zeros_like(slab)))
    return out


def _attn_fwd(qkvz, S, name):
    T = qkvz.shape[0]
    nq = S // TQ
    HEADS, QK_W, V_W = FWD_HEADS, FWD_HEADS * 64, FWD_HEADS * HEAD_V
    k_b0 = QK_WIDTH // QK_W
    v_b0 = 2 * QK_WIDTH // V_W
    z_b0 = (2 * QK_WIDTH + D_INNER) // V_W
    hs = range(HEADS)

    def body(q_ref, k_ref, v_ref, z_ref, o_ref, y_ref, lt_ref):
        qi = pl.program_id(2)
        row = lax.broadcasted_iota(jnp.int32, (TQ, TQ), 0)
        col = lax.broadcasted_iota(jnp.int32, (TQ, TQ), 1)
        causal = col < row
        later_in_block = (row > col).astype(BF16)
        qms = [qm * 0.125 for qm in _masked_heads(q_ref[...], HEADS)]

        def step(j, carry, diagonal):
            koff = pl.multiple_of(j * TQ, TQ)
            kbs = [k_ref[pl.ds(koff, TQ), p * HEAD_PAIR_QK:(p + 1) * HEAD_PAIR_QK]
                   for p in range(HEADS // 2)]
            run, acc = [carry[2 * hh] for hh in hs], [carry[2 * hh + 1] for hh in hs]
            z = [_dot_nt(qms[hh], kbs[hh // 2]) for hh in hs]
            terms = [_log_terms(z[hh]) for hh in hs]
            log_om = [jnp.where(causal, t[1], 0.0) if diagonal else t[1] for t in terms]
            later = [_split_dot(log_om[hh], later_in_block) for hh in hs]
            a = [jnp.exp(terms[hh][0] + (run[hh] + later[hh])) for hh in hs]
            if diagonal:
                a = [jnp.where(causal, a[hh], 0.0) for hh in hs]
            out = []
            for hh in hs:
                vb = v_ref[pl.ds(koff, TQ), hh * HEAD_V:(hh + 1) * HEAD_V]
                out += [run[hh] + jnp.sum(log_om[hh], axis=1, keepdims=True),
                        acc[hh] + _dot(a[hh].astype(BF16), vb)]
            return tuple(out)

        zero = (jnp.zeros((TQ, 1), F32), jnp.zeros((TQ, HEAD_V), F32))
        carry = step(qi, zero * HEADS, True)
        carry = lax.fori_loop(0, qi, lambda i, c: step(qi - 1 - i, c, False), carry)
        for hh in hs:
            sl = slice(hh * HEAD_V, (hh + 1) * HEAD_V)
            acc = carry[2 * hh + 1]
            z = z_ref[:, sl].astype(F32)
            o_ref[:, sl] = acc.astype(BF16)
            y_ref[:, sl] = (acc * (z * _sigmoid(z))).astype(BF16)
            lt_ref[:, hh:hh + 1] = carry[2 * hh]

    qblk = lambda b, p, i: (b * nq + i, p)
    return pl.pallas_call(
        body, name=name, grid=(T // S, N_HEADS // HEADS, nq),
        in_specs=[pl.BlockSpec((TQ, QK_W), qblk),
                  pl.BlockSpec((S, QK_W), lambda b, p, i: (b, k_b0 + p)),
                  pl.BlockSpec((S, V_W), lambda b, p, i: (b, v_b0 + p)),
                  pl.BlockSpec((TQ, V_W), lambda b, p, i: (b * nq + i, z_b0 + p))],
        out_specs=[pl.BlockSpec((TQ, V_W), qblk),
                   pl.BlockSpec((TQ, V_W), qblk),
                   pl.BlockSpec((None, TQ, HEADS), lambda b, p, i: (p, b * nq + i, 0))],
        out_shape=[jax.ShapeDtypeStruct((T, D_INNER), BF16),
                   jax.ShapeDtypeStruct((T, D_INNER), BF16),
                   jax.ShapeDtypeStruct((N_HEADS // HEADS, T, HEADS), F32)],
        compiler_params=_params(3),
    )(qkvz, qkvz, qkvz, qkvz)


def _attn_gate_bwd(dh, w_out, qkvz, o, after, name, tm=1024, tn=512):
    T = dh.shape[0]
    gate_b0 = (2 * QK_WIDTH + D_INNER) // tn

    def body(dh_ref, w_ref, z_ref, o_ref, _, do_ref, dz_ref, dh_s):
        @pl.when(pl.program_id(1) == 0)
        def _():
            dh_s[...] = dh_ref[...].astype(BF16)

        dy = _dot_nt(dh_s[...], w_ref[...])
        z = z_ref[...].astype(F32)
        sig = _sigmoid(z)
        do_ref[...] = (dy * (z * sig)).astype(BF16)
        dz_ref[...] = (dy * o_ref[...].astype(F32) * (sig * (1.0 + z * (1.0 - sig)))).astype(BF16)

    return pl.pallas_call(
        body, name=name, grid=(T // tm, D_INNER // tn),
        in_specs=[pl.BlockSpec((tm, D_MODEL), lambda m, n: (m, 0)),
                  pl.BlockSpec((tn, D_MODEL), lambda m, n: (n, 0)),
                  pl.BlockSpec((tm, tn), lambda m, n: (m, gate_b0 + n)),
                  pl.BlockSpec((tm, tn), lambda m, n: (m, n)),
                  ANY_SPEC],
        out_specs=[pl.BlockSpec((tm, tn), lambda m, n: (m, n)),
                   pl.BlockSpec((tm, tn), lambda m, n: (m, gate_b0 + n))],
        out_shape=[jax.ShapeDtypeStruct((T, D_INNER), BF16),
                   jax.ShapeDtypeStruct((T, 2 * QK_WIDTH + 2 * D_INNER), BF16)],
        scratch_shapes=[pltpu.VMEM((tm, D_MODEL), BF16)],
        compiler_params=_params(2),
    )(dh, w_out, qkvz, o, after)


def _attn_bwd(qkv, do, ltot, dproj1, S, name):
    T = qkv.shape[0]
    nq = S // TQ
    HEADS, QK_W, V_W = BWD_HEADS, BWD_HEADS * 64, BWD_HEADS * HEAD_V
    k_b0 = QK_WIDTH // QK_W
    v_b0 = 2 * QK_WIDTH // V_W
    hs = range(HEADS)
    pairs = range(HEADS // 2)

    def body(q_ref, k_ref, v_ref, do_ref, lt_ref, _, out_ref, dq_s, dk_s, dv_s, dkb_s, dvb_s, sems):
        b, p = pl.program_id(0), pl.program_id(1)
        row = lax.broadcasted_iota(jnp.int32, (TQ, TQ), 0)
        col = lax.broadcasted_iota(jnp.int32, (TQ, TQ), 1)
        causal = col < row
        upto = (row <= col).astype(BF16)
        before = (row < col).astype(BF16)
        dk_s[...] = jnp.zeros_like(dk_s)
        dv_s[...] = jnp.zeros_like(dv_s)

        def q_block(qi, _):
            qoff = pl.multiple_of(qi * TQ, TQ)
            qms = [qm * 0.125 for qm in _masked_heads(q_ref[pl.ds(qoff, TQ), :], HEADS)]
            vsl = [slice(hh * HEAD_V, (hh + 1) * HEAD_V) for hh in hs]
            psl = [slice(pp * HEAD_PAIR_QK, (pp + 1) * HEAD_PAIR_QK) for pp in pairs]
            do_h = [do_ref[pl.ds(qoff, TQ), sl] for sl in vsl]
            total = [lt_ref[pl.ds(qoff, TQ), hh:hh + 1] for hh in hs]

            def k_block(j, carry, diagonal):
                koff = pl.multiple_of(j * TQ, TQ)
                kms = _masked_heads(k_ref[pl.ds(koff, TQ), :], HEADS)
                g_before = [carry[2 * hh] for hh in hs]
                lom_before = [carry[2 * hh + 1] for hh in hs]
                z = [_dot_nt(qms[hh], kms[hh]) for hh in hs]
                da = [_dot_nt(do_h[hh], v_ref[pl.ds(koff, TQ), vsl[hh]]) for hh in hs]
                terms = [_log_terms(z[hh]) for hh in hs]
                log_om = [jnp.where(causal, t[1], 0.0) if diagonal else t[1] for t in terms]
                prefix = [_split_dot(log_om[hh], upto) for hh in hs]
                a = [jnp.exp(terms[hh][0] + ((total[hh] - lom_before[hh]) - prefix[hh])) for hh in hs]
                if diagonal:
                    a = [jnp.where(causal, a[hh], 0.0) for hh in hs]
                g = [a[hh] * da[hh] for hh in hs]
                g_prefix = [_dot(g[hh].astype(BF16), before) for hh in hs]
                out, dzs = [], []
                for hh in hs:
                    beta = jnp.exp(terms[hh][0])
                    g_excl = (g_before[hh] + g_prefix[hh]) * beta
                    if diagonal:
                        g_excl = jnp.where(causal, g_excl, 0.0)
                    dzs.append((g[hh] * (1.0 - beta) - g_excl).astype(BF16))
                    out += [g_before[hh] + jnp.sum(g[hh], axis=1, keepdims=True),
                            lom_before[hh] + jnp.sum(log_om[hh], axis=1, keepdims=True)]
                for hh in hs:
                    dv_s[pl.ds(koff, TQ), vsl[hh]] += _dot_tn(a[hh].astype(BF16), do_h[hh])
                dq = []
                for pp in pairs:
                    pair = slice(2 * pp, 2 * pp + 2)
                    dq.append(carry[2 * HEADS + pp] + _dot(jnp.concatenate(dzs[pair], axis=1),
                                                           jnp.concatenate(kms[pair], axis=0)))
                    dk_s[pl.ds(koff, TQ), psl[pp]] += _dot_tn(jnp.concatenate(dzs[pair], axis=0),
                                                              jnp.concatenate(qms[pair], axis=0))
                return tuple(out) + tuple(dq)

            zero = jnp.zeros((TQ, 1), F32)
            carry = (zero,) * (2 * HEADS) + (jnp.zeros((TQ, HEAD_PAIR_QK), F32),) * (HEADS // 2)
            carry = lax.fori_loop(0, qi, lambda j, c: k_block(j, c, False), carry)
            carry = k_block(qi, carry, True)
            for pp in pairs:
                dq_s[pl.ds(qoff, TQ), psl[pp]] = (carry[2 * HEADS + pp] * 0.125).astype(BF16)
            return 0

        lax.fori_loop(0, nq, q_block, 0)
        dkb_s[...] = dk_s[...].astype(BF16)
        dvb_s[...] = dv_s[...].astype(BF16)
        rows = pl.ds(pl.multiple_of(b * S, TQ), S)
        copies = [
            pltpu.make_async_copy(
                dq_s, out_ref.at[rows, pl.ds(pl.multiple_of(p * QK_W, 128), QK_W)], sems.at[0]),
            pltpu.make_async_copy(
                dkb_s, out_ref.at[rows, pl.ds(pl.multiple_of(QK_WIDTH + p * QK_W, 128), QK_W)],
                sems.at[1]),
            pltpu.make_async_copy(
                dvb_s, out_ref.at[rows, pl.ds(pl.multiple_of(2 * QK_WIDTH + p * V_W, 128), V_W)],
                sems.at[2]),
        ]
        for cp in copies:
            cp.start()
        for cp in copies:
            cp.wait()

    return pl.pallas_call(
        body, name=name, grid=(T // S, N_HEADS // HEADS),
        in_specs=[pl.BlockSpec((S, QK_W), lambda b, p: (b, p)),
                  pl.BlockSpec((S, QK_W), lambda b, p: (b, k_b0 + p)),
                  pl.BlockSpec((S, V_W), lambda b, p: (b, v_b0 + p)),
                  pl.BlockSpec((S, V_W), lambda b, p: (b, p)),
                  pl.BlockSpec((None, S, HEADS), lambda b, p: (p, b, 0)),
                  HBM_SPEC],
        out_specs=HBM_SPEC,
        out_shape=jax.ShapeDtypeStruct(dproj1.shape, dproj1.dtype),
        input_output_aliases={5: 0},
        scratch_shapes=[pltpu.VMEM((S, QK_W), BF16),
                        pltpu.VMEM((S, QK_W), F32),
                        pltpu.VMEM((S, V_W), F32),
                        pltpu.VMEM((S, QK_W), BF16),
                        pltpu.VMEM((S, V_W), BF16),
                        pltpu.SemaphoreType.DMA((3,))],
        compiler_params=_params(2),
    )(qkv, qkv, qkv, do, ltot, dproj1)


def _loss_head(h, g_row, target, name, tm=512):
    T = h.shape[0]

    def body(h_ref, g_ref, t_ref, dh_ref, dg_ref, loss_ref):
        @pl.when(pl.program_id(0) == 0)
        def _():
            dg_ref[...] = jnp.zeros_like(dg_ref)
            loss_ref[...] = jnp.zeros_like(loss_ref)

        x = h_ref[...]
        inv = lax.rsqrt(jnp.mean(x * x, axis=-1, keepdims=True) + RMS_EPS)
        xhat = x * inv
        gain = g_ref[...]
        err = xhat * gain - t_ref[...]
        per_token = jnp.mean(err * err, axis=-1, keepdims=True)
        loss_ref[...] += 0.5 * jnp.sum(per_token, axis=0, keepdims=True)
        dy = err * (1.0 / D_MODEL)
        dg_ref[...] += jnp.sum(dy * xhat, axis=0, keepdims=True)
        dxh = dy * gain
        proj = jnp.mean(dxh * xhat, axis=-1, keepdims=True)
        dh_ref[...] = inv * (dxh - xhat * proj)

    return pl.pallas_call(
        body, name=name, grid=(T // tm,),
        in_specs=[pl.BlockSpec((tm, D_MODEL), lambda m: (m, 0)),
                  pl.BlockSpec((1, D_MODEL), lambda m: (0, 0)),
                  pl.BlockSpec((tm, D_MODEL), lambda m: (m, 0))],
        out_specs=[pl.BlockSpec((tm, D_MODEL), lambda m: (m, 0)),
                   pl.BlockSpec((1, D_MODEL), lambda m: (0, 0)),
                   pl.BlockSpec((1, 128), lambda m: (0, 0))],
        out_shape=[jax.ShapeDtypeStruct((T, D_MODEL), F32),
                   jax.ShapeDtypeStruct((1, D_MODEL), F32),
                   jax.ShapeDtypeStruct((1, 128), F32)],
        compiler_params=_params(1),
    )(h, g_row, target)


def _place():
    return lax.axis_index("x"), lax.axis_index("y"), lax.axis_index("c")


def _other_chips(x, y):
    return [(1 - x, y), (x, 1 - y), (1 - x, 1 - y)]


def _half(ref, c):
    hr = ref.shape[-2] // 2
    return pl.ds(pl.multiple_of(c * hr, 8), hr)


def _cast_to_slot(shard, chip, name, tr=256):
    R, C = shard.shape

    def body(chip_ref, w_ref, o_ref):
        o_ref[0] = w_ref[...].astype(BF16)

    return pl.pallas_call(
        body, name=name,
        grid_spec=pltpu.PrefetchScalarGridSpec(
            num_scalar_prefetch=1, grid=(R // tr,),
            in_specs=[pl.BlockSpec((tr, C), lambda i, chip_ref: (i, 0))],
            out_specs=pl.BlockSpec((1, tr, C), lambda i, chip_ref: (chip_ref[0], i, 0))),
        out_shape=jax.ShapeDtypeStruct((N_CHIPS, R, C), BF16),
        compiler_params=_params(1),
    )(chip, shard)


def _weight_plan(bufs):
    x, y, c = _place()
    plan = []
    for buf in bufs:
        mine = buf.at[2 * x + y, _half(buf, c)]
        for ox, oy in _other_chips(x, y):
            plan.append((mine, mine, (ox, oy, c), buf.at[2 * ox + oy, _half(buf, c)]))
    return plan


def _chip_sum_plan(bufs):
    x, y, c = _place()
    n = len(bufs) // 2
    plan = []
    for sums, land in zip(bufs[:n], bufs[n:]):
        for k, (ox, oy) in enumerate(_other_chips(x, y)):
            plan.append((sums.at[2 * ox + oy], land.at[k], (ox, oy, c), land.at[k]))
    return plan


SEM_SPEC = pl.BlockSpec(memory_space=pltpu.SEMAPHORE)
ANY_SPEC = pl.BlockSpec(memory_space=pl.ANY)
DATAFLOW = pltpu.SideEffectType.DATAFLOW_SIDE_EFFECTING


def _in_hbm(a):
    return pltpu.with_memory_space_constraint(a, pltpu.HBM)


def _exchange_start(bufs, after, plan, n_copies, name):
    nb = len(bufs)

    def body(*refs):
        send_sems, recv_sems = refs[nb + 1], refs[nb + 2]
        for i, (src, dst, dev, _) in enumerate(plan(refs[:nb])):
            pltpu.make_async_remote_copy(
                src_ref=src, dst_ref=dst, send_sem=send_sems.at[i], recv_sem=recv_sems.at[i],
                device_id=dev, device_id_type=MESH).start()
        token = refs[-1]
        token[...] = jnp.zeros_like(token)

    res = pl.pallas_call(
        body, name=name,
        in_specs=[HBM_SPEC] * nb + [ANY_SPEC],
        out_specs=[SEM_SPEC, SEM_SPEC] + [HBM_SPEC] * nb + [pl.BlockSpec(memory_space=pltpu.VMEM)],
        out_shape=[pltpu.SemaphoreType.DMA((n_copies,)), pltpu.SemaphoreType.DMA((n_copies,))]
        + [pltpu.HBM(b.shape, b.dtype) for b in bufs] + [jax.ShapeDtypeStruct((8, 128), F32)],
        input_output_aliases={i: 2 + i for i in range(nb)},
        compiler_params=pltpu.CompilerParams(has_side_effects=DATAFLOW),
    )(*[_in_hbm(b) for b in bufs], after)
    return res[0], res[1], list(res[2:2 + nb]), res[-1]


def _exchange_wait(bufs, send_sems, recv_sems, after, plan, name):
    nb = len(bufs)

    def body(*refs):
        sends, recvs = refs[nb], refs[nb + 1]
        for i, (src, dst, dev, landing) in enumerate(plan(refs[:nb])):
            pltpu.make_async_remote_copy(
                src_ref=src, dst_ref=landing, send_sem=sends.at[i], recv_sem=recvs.at[i],
                device_id=dev, device_id_type=MESH).wait()

    res = pl.pallas_call(
        body, name=name,
        in_specs=[HBM_SPEC] * nb + [SEM_SPEC, SEM_SPEC, ANY_SPEC],
        out_specs=[HBM_SPEC] * nb,
        out_shape=[pltpu.HBM(b.shape, b.dtype) for b in bufs],
        input_output_aliases={i: i for i in range(nb)},
        compiler_params=pltpu.CompilerParams(has_side_effects=DATAFLOW),
    )(*bufs, send_sems, recv_sems, after)
    return list(res)


def _allgather_weights(slots, name, landed=False):
    n = len(slots)

    def body(*refs):
        outs = refs[n:2 * n]
        send_sems, recv_sems, fwd_send, fwd_recv = refs[2 * n:]
        x, y, c = _place()
        chips = _other_chips(x, y)

        def landing(a, chip, half_of):
            return outs[a].at[2 * chip[0] + chip[1], _half(outs[a], half_of)]

        def ici(a, k, chip_from, to):
            return pltpu.make_async_remote_copy(
                src_ref=landing(a, chip_from, c), dst_ref=landing(a, chip_from, c),
                send_sem=send_sems.at[a, k], recv_sem=recv_sems.at[a, k],
                device_id=to, device_id_type=MESH)

        def d2d(a, k, chip_from, half_of):
            return pltpu.make_async_remote_copy(
                src_ref=landing(a, chip_from, half_of), dst_ref=landing(a, chip_from, half_of),
                send_sem=fwd_send.at[a, k], recv_sem=fwd_recv.at[a, k],
                device_id=(x, y, 1 - c), device_id_type=MESH)

        sends = []
        if not landed:
            sends = [ici(a, k, (x, y), (*chips[k], c)) for a in range(n) for k in range(3)]
        for cp in sends:
            cp.start()
        forwards = []
        for a in range(n):
            for k in range(3):
                if not landed:
                    ici(a, k, chips[k], (x, y, c)).wait_recv()
                fw = d2d(a, k, chips[k], c)
                fw.start()
                forwards.append(fw)
        for a in range(n):
            for k in range(3):
                d2d(a, k, chips[k], 1 - c).wait_recv()
        for cp in sends + forwards:
            cp.wait_send()

    return pl.pallas_call(
        body, name=name,
        in_specs=[HBM_SPEC] * n, out_specs=[HBM_SPEC] * n,
        out_shape=[jax.ShapeDtypeStruct(s.shape, s.dtype) for s in slots],
        input_output_aliases={a: a for a in range(n)},
        scratch_shapes=[pltpu.SemaphoreType.DMA((n, 3)), pltpu.SemaphoreType.DMA((n, 3)),
                        pltpu.SemaphoreType.DMA((n, 3)), pltpu.SemaphoreType.DMA((n, 3))],
    )(*slots)


def _sibling_exchange(partials, small, name):
    n = len(partials)
    ns = 0 if small is None else 1

    def body(*refs):
        ins, outs = refs[:n], refs[n + ns:2 * n + ns]
        send_sems, recv_sems = refs[2 * (n + ns):2 * (n + ns) + 2]
        x, y, c = _place()
        me = 4 * x + 2 * y + c
        sends = [pltpu.make_async_remote_copy(
            src_ref=ins[a].at[:, _half(ins[a], 1 - c)], dst_ref=outs[a],
            send_sem=send_sems.at[a], recv_sem=recv_sems.at[a],
            device_id=(x, y, 1 - c), device_id_type=MESH) for a in range(n)]
        if ns:
            small_ref, small_all = refs[n], refs[2 * n + 1]
            s_send, s_recv, loc_sem = refs[2 * (n + ns) + 2:]
            local = pltpu.make_async_copy(small_ref, small_all.at[me], loc_sem)
            local.start()
            for d in range(1, N_DEV):
                px, py, pc = x ^ ((d >> 2) & 1), y ^ ((d >> 1) & 1), c ^ (d & 1)
                sends.append(pltpu.make_async_remote_copy(
                    src_ref=small_ref, dst_ref=small_all.at[me],
                    send_sem=s_send.at[d - 1], recv_sem=s_recv.at[d - 1],
                    device_id=(px, py, pc), device_id_type=MESH))
        for cp in sends:
            cp.start()
        if ns:
            for d in range(1, N_DEV):
                pltpu.make_async_remote_copy(
                    src_ref=small_ref, dst_ref=small_all.at[me ^ d],
                    send_sem=s_send.at[d - 1], recv_sem=s_recv.at[d - 1],
                    device_id=(x, y, c), device_id_type=MESH).wait_recv()
        for cp in sends[:n]:
            cp.wait_recv()
        for cp in sends:
            cp.wait_send()
        if ns:
            local.wait()

    out_shape = [jax.ShapeDtypeStruct((N_CHIPS, p.shape[1] // 2, p.shape[2]), F32) for p in partials]
    scratch = [pltpu.SemaphoreType.DMA((max(n, 1),)), pltpu.SemaphoreType.DMA((max(n, 1),))]
    if ns:
        out_shape.append(jax.ShapeDtypeStruct((N_DEV,) + small.shape, F32))
        scratch += [pltpu.SemaphoreType.DMA((N_DEV - 1,)), pltpu.SemaphoreType.DMA((N_DEV - 1,)),
                    pltpu.SemaphoreType.DMA]
    return pl.pallas_call(
        body, name=name,
        in_specs=[HBM_SPEC] * (n + ns), out_specs=[HBM_SPEC] * (n + ns),
        out_shape=out_shape, scratch_shapes=scratch,
    )(*partials, *([small] if ns else []))


def _chip_sum(partial, from_sibling, c, name, tr=256):
    _, hr, C = from_sibling.shape
    nb = hr // tr

    def body(c_ref, p_ref, s_ref, o_ref):
        o_ref[...] = (p_ref[...] + s_ref[...]).astype(BF16)

    return pl.pallas_call(
        body, name=name,
        grid_spec=pltpu.PrefetchScalarGridSpec(
            num_scalar_prefetch=1, grid=(N_CHIPS, nb),
            in_specs=[pl.BlockSpec((1, tr, C), lambda j, i, c_ref: (j, c_ref[0] * nb + i, 0)),
                      pl.BlockSpec((1, tr, C), lambda j, i, c_ref: (j, i, 0))],
            out_specs=pl.BlockSpec((1, tr, C), lambda j, i, c_ref: (j, i, 0))),
        out_shape=jax.ShapeDtypeStruct(from_sibling.shape, BF16),
        compiler_params=_params(2),
    )(c, partial, from_sibling)


def _reduce_half(partial, from_sibling, received, place, name, tr=256):
    _, hr, C = from_sibling.shape
    nb = hr // tr

    def body(p_ref, mine_ref, sib_ref, r_ref, o_ref):
        acc = mine_ref[0] + sib_ref[0]
        for k in range(3):
            acc = acc + r_ref[k].astype(F32)
        o_ref[...] = acc

    return pl.pallas_call(
        body, name=name,
        grid_spec=pltpu.PrefetchScalarGridSpec(
            num_scalar_prefetch=1, grid=(nb,),
            in_specs=[pl.BlockSpec((1, tr, C), lambda i, p: (p[0], p[1] * nb + i, 0)),
                      pl.BlockSpec((1, tr, C), lambda i, p: (p[0], i, 0)),
                      pl.BlockSpec((3, tr, C), lambda i, p: (0, i, 0))],
            out_specs=pl.BlockSpec((tr, C), lambda i, p: (p[1] * nb + i, 0))),
        out_shape=jax.ShapeDtypeStruct((2 * hr, C), F32),
        compiler_params=_params(1),
    )(place, partial, from_sibling, received)


def _join_halves(fulls, name):
    n = len(fulls)

    def body(*refs):
        outs = refs[n:2 * n]
        send_sems, recv_sems = refs[2 * n:]
        x, y, c = _place()

        def copy(a, half_of, to):
            rows = outs[a].at[_half(outs[a], half_of)]
            return pltpu.make_async_remote_copy(
                src_ref=rows, dst_ref=rows, send_sem=send_sems.at[a], recv_sem=recv_sems.at[a],
                device_id=to, device_id_type=MESH)

        sends = [copy(a, c, (x, y, 1 - c)) for a in range(n)]
        for cp in sends:
            cp.start()
        for a in range(n):
            copy(a, 1 - c, (x, y, c)).wait_recv()
        for cp in sends:
            cp.wait_send()

    return pl.pallas_call(
        body, name=name,
        in_specs=[HBM_SPEC] * n, out_specs=[HBM_SPEC] * n,
        out_shape=[jax.ShapeDtypeStruct(f.shape, F32) for f in fulls],
        input_output_aliases={a: a for a in range(n)},
        scratch_shapes=[pltpu.SemaphoreType.DMA((n,)), pltpu.SemaphoreType.DMA((n,))],
    )(*fulls)


def _adamw_math(w, g, m, v):
    m = ADAM_B1 * m + (1.0 - ADAM_B1) * g
    v = ADAM_B2 * v + (1.0 - ADAM_B2) * (g * g)
    m_hat = m / (1.0 - ADAM_B1 ** ADAM_STEP)
    v_hat = v / (1.0 - ADAM_B2 ** ADAM_STEP)
    delta = -ADAM_LR * (m_hat / (jnp.sqrt(v_hat) + ADAM_EPS) + ADAM_WD * w)
    return delta, m, v


def _adamw(w, g, m, v, name, tr=256):
    R, C = w.shape
    tr = min(tr, R)

    def body(w_ref, g_ref, m_ref, v_ref, d_out, m_out, v_out):
        d_out[...], m_out[...], v_out[...] = _adamw_math(w_ref[...], g_ref[...], m_ref[...], v_ref[...])

    spec = pl.BlockSpec((tr, C), lambda i: (i, 0))
    return pl.pallas_call(
        body, name=name, grid=(R // tr,),
        in_specs=[spec] * 4, out_specs=[spec] * 3,
        out_shape=[jax.ShapeDtypeStruct((R, C), F32)] * 3,
        compiler_params=_params(1),
    )(w, g, m, v)


def _adamw_small(small_all, w, m, v, name):
    def body(s_ref, w_ref, m_ref, v_ref, g_out, d_out, m_out, v_out):
        g = s_ref[0]
        for d in range(1, N_DEV):
            g = g + s_ref[d]
        g_out[...] = g
        d_out[...], m_out[...], v_out[...] = _adamw_math(w_ref[...], g, m_ref[...], v_ref[...])

    vm = pl.BlockSpec(memory_space=pltpu.VMEM)
    return pl.pallas_call(
        body, name=name, in_specs=[vm] * 4, out_specs=[vm] * 4,
        out_shape=[jax.ShapeDtypeStruct(w.shape, F32)] * 4,
    )(small_all, w, m, v)


def _pack_small(norm_g, pool_scale, norm_f, extra_row):
    return jnp.concatenate([norm_g.reshape(2, D_MODEL), pool_scale.reshape(2, D_MODEL),
                            norm_f.reshape(1, D_MODEL), extra_row,
                            jnp.zeros((2, D_MODEL), F32)], axis=0)


def kernel(x, norm_g, pool_w_in, pool_w, pool_scale, pool_w_out, sb_w_in, sb_w_out, norm_f, loss_target, m_norm_g, m_pool_w_in, m_pool_w, m_pool_scale, m_pool_w_out, m_sb_w_in, m_sb_w_out, m_norm_f, v_norm_g, v_pool_w_in, v_pool_w, v_pool_scale, v_pool_w_out, v_sb_w_in, v_sb_w_out, v_norm_f):
    nb, S, _ = x.shape
    T = nb * S
    xt = x.reshape(T, D_MODEL)
    target = loss_target.reshape(T, D_MODEL)
    cx, cy, cc = _place()

    def shard2d(w):
        return w.reshape(-1, w.shape[-1])

    names = ("pool_w_in", "pool_w", "pool_w_out", "sb_w_in", "sb_w_out")
    w_shards = [shard2d(w) for w in (pool_w_in, pool_w, pool_w_out, sb_w_in, sb_w_out)]
    m_shards = [shard2d(w) for w in (m_pool_w_in, m_pool_w, m_pool_w_out, m_sb_w_in, m_sb_w_out)]
    v_shards = [shard2d(w) for w in (v_pool_w_in, v_pool_w, v_pool_w_out, v_sb_w_in, v_sb_w_out)]

    chip = (2 * cx + cy).reshape(1).astype(jnp.int32)
    c_arr = cc.reshape(1).astype(jnp.int32)
    place = jnp.stack([2 * cx + cy, cc]).astype(jnp.int32)
    slots = [_cast_to_slot(w, chip, "cast_" + nm) for w, nm in zip(w_shards, names)]
    g0, g1, gf = norm_g[0:1], norm_g[1:2], norm_f.reshape(1, D_MODEL)

    w_pin, w_g, w_pout = _allgather_weights(slots[:3], "allgather_pool_weights")
    w_pout = w_pout.reshape(D_INNER, D_MODEL)
    sb_send, sb_recv, sb_slots, token = _exchange_start(slots[3:], w_pin, _weight_plan, 6,
                                                        "sb_weights_start")

    proj0, u0 = _rms_matmul(xt, g0 + token[0:1, 0:1], w_pin, [(2 * D_INNER, BF16)], "pool_in_proj",
                            tn=w_pin.shape[2])
    y0, pooled, mixed = _pool_fwd(proj0, w_g, pool_scale, S, "pool_mix")
    sb_slots = _exchange_wait(sb_slots, sb_send, sb_recv, y0, _weight_plan, "sb_weights_wait")
    w_sin, w_sout = _allgather_weights(sb_slots, "sb_weights_forward", landed=True)
    w_sout = w_sout.reshape(D_INNER, D_MODEL)
    h1 = _matmul_residual(y0, w_pout, xt, "pool_out_proj")
    n1 = 2 * QK_WIDTH + 2 * D_INNER
    qkvz, u1 = _rms_matmul(h1, g1, w_sin, [(n1, BF16)], "sb_in_proj", tn=w_sin.shape[2])
    o, y1, ltot = _attn_fwd(qkvz, S, "sb_attention")
    h2 = _matmul_residual(y1, w_sout, h1, "sb_out_proj")
    dh2, d_norm_f, loss_row = _loss_head(h2, gf, target, "loss_head")

    def reduce_start(partials, tag):
        from_sibling = _sibling_exchange(partials, None, "grad_sibling_exchange_" + tag)
        sums = [_chip_sum(p, s, c_arr, "grad_chip_sum_%s_%d" % (tag, i))
                for i, (p, s) in enumerate(zip(partials, from_sibling))]
        lands = [lax.empty((3,) + s.shape[1:], BF16) for s in sums]
        send, recv, bufs, token = _exchange_start(sums + lands, c_arr, _chip_sum_plan, 3 * len(sums),
                                                  "grad_chip_exchange_start_" + tag)
        return (partials, list(from_sibling), send, recv, bufs), token[0:1, 0:1]

    def reduce_finish(started, after, tag):
        partials, from_sibling, send, recv, bufs = started
        received = _exchange_wait(bufs, send, recv, after, _chip_sum_plan,
                                  "grad_chip_exchange_wait_" + tag)[len(partials):]
        return [_reduce_half(p, s, r, place, "grad_reduce_%s_%d" % (tag, i))
                for i, (p, s, r) in enumerate(zip(partials, from_sibling, received))]

    shard = lambda i, j, t: (j, 0, 0)
    gw_sout = _matmul_tn(y1, dh2, D_INNER, D_MODEL, (D_INNER, D_MODEL), (1024, 1024),
                         lambda i, j, t: (i, j), "grad_sb_w_out", bm=1024, bn=1024)
    sout_started, token = reduce_start([gw_sout.reshape(N_CHIPS, -1, D_MODEL)], "sb_out")
    do, dproj1 = _attn_gate_bwd(dh2, w_sout, qkvz, o, token, "sb_gate_bwd")
    dproj1 = _attn_bwd(qkvz, do, ltot, dproj1, S, "sb_attention_bwd")
    gw_sin = _matmul_tn(u1, dproj1, D_MODEL, n1, (N_CHIPS, D_MODEL, n1 // 4), (1, D_MODEL, n1 // 4),
                        shard, "grad_sb_w_in", bm=D_MODEL, bn=n1 // 4)
    sin_started, token = reduce_start([gw_sin], "sb_in")
    dh1, d_g1 = _matmul_nt_rms_bwd(dproj1, w_sin, h1, g1 + token, dh2, "sb_in_bwd", tk=w_sin.shape[2])
    gw_pout = _matmul_tn(y0, dh1, D_INNER, D_MODEL, (D_INNER, D_MODEL), (1024, 1024),
                         lambda i, j, t: (i, j), "grad_pool_w_out", bm=1024, bn=1024)
    dmixed, dproj0, d_scale = _pool_gate_bwd(dh1, w_pout, proj0, mixed, pool_scale, "pool_gate_bwd")
    gw_g = _matmul_tn(pooled, dmixed, D_INNER, D_INNER, (N_CHIPS, GROUP_DIM, GROUP_DIM),
                      (N_CHIPS, GROUP_DIM // N_CHIPS, GROUP_DIM), lambda i, j, t: (0, i, 0),
                      "grad_pool_w", bm=GROUP_DIM, bn=GROUP_DIM, diagonal_blocks=True)
    mix_started, token = reduce_start([gw_g, gw_pout.reshape(N_CHIPS, -1, D_MODEL)], "pool_mix")
    dproj0 = _pool_bwd(dmixed, w_g, dproj0, token, S, "pool_bwd")
    n0 = 2 * D_INNER
    gw_pin = _matmul_tn(u0, dproj0, D_MODEL, n0, (N_CHIPS, D_MODEL, n0 // 4), (1, D_MODEL, n0 // 4),
                        shard, "grad_pool_w_in", bm=D_MODEL, bn=n0 // 4)
    pin_started, token = reduce_start([gw_pin], "pool_in")
    dx, d_g0 = _matmul_nt_rms_bwd(dproj0, w_pin, xt, g0 + token, dh1, "pool_in_bwd", tk=w_pin.shape[2])

    small = _pack_small(jnp.concatenate([d_g0, d_g1], axis=0), d_scale, d_norm_f,
                        jnp.broadcast_to(loss_row[:, :1], (1, D_MODEL)))
    small_all, = _sibling_exchange([], small, "small_sums_exchange")
    grads = _join_halves(reduce_finish(pin_started, dx, "pool_in")
                         + reduce_finish(mix_started, dx, "pool_mix")
                         + reduce_finish(sin_started, dx, "sb_in")
                         + reduce_finish(sout_started, dx, "sb_out"), "grad_join_halves")

    deltas, new_m, new_v = [], [], []
    for w, g, m, v, nm in zip(w_shards, grads, m_shards, v_shards, names):
        d, mm, vv = _adamw(w, g, m, v, "adamw_" + nm)
        deltas.append(d)
        new_m.append(mm)
        new_v.append(vv)

    zero_row = jnp.zeros((1, D_MODEL), F32)
    g_small, d_small, m_small, v_small = _adamw_small(
        small_all, _pack_small(norm_g, pool_scale, norm_f, zero_row),
        _pack_small(m_norm_g, m_pool_scale, m_norm_f, zero_row),
        _pack_small(v_norm_g, v_pool_scale, v_norm_f, zero_row + 1.0), "adamw_small")
    loss = g_small[5, 0]

    def unpack_small(a):
        return a[0:2], a[2:4].reshape(1, D_INNER), a[4]

    def assemble(big, small3):
        ng, ps, nf = small3
        return [ng, big[0].reshape(pool_w_in.shape), big[1].reshape(pool_w.shape), ps,
                big[2].reshape(pool_w_out.shape), big[3].reshape(sb_w_in.shape),
                big[4].reshape(sb_w_out.shape), nf]

    return (loss, dx.reshape(x.shape),
            *assemble(grads, unpack_small(g_small)),
            *assemble(deltas, unpack_small(d_small)),
            *assemble(new_m, unpack_small(m_small)),
            *assemble(new_v, unpack_small(v_small)))
```

```python
import jax
import jax.numpy as jnp
from jax import lax
from jax.experimental import pallas as pl
from jax.experimental.pallas import tpu as pltpu

F32 = jnp.float32
BF16 = jnp.bfloat16
MESH = pl.DeviceIdType.MESH

D_MODEL = 1024
D_INNER = 2048
N_GROUPS = 4
GROUP_DIM = 512
HEAD_PAIR_QK = 128
HEAD_V = 128
QK_WIDTH = 1024
RMS_EPS = 1e-6
HALO = 16
N_CHIPS = 4
N_DEV = 8

ADAM_LR = 0.001
ADAM_B1 = 0.9
ADAM_B2 = 0.999
ADAM_EPS = 1e-08
ADAM_WD = 0.01
ADAM_STEP = 10

VMEM_LIMIT = 56 * 1024 * 1024

HBM_SPEC = pl.BlockSpec(memory_space=pltpu.HBM)


def _params(n_axes):
    return pltpu.CompilerParams(dimension_semantics=("arbitrary",) * n_axes,
                                vmem_limit_bytes=VMEM_LIMIT)


def _dot(a, b):
    return jnp.dot(a, b, preferred_element_type=F32)


def _dot_nt(a, b):
    return lax.dot_general(a, b, (((1,), (1,)), ((), ())), preferred_element_type=F32)


def _dot_tn(a, b):
    return lax.dot_general(a, b, (((0,), (0,)), ((), ())), preferred_element_type=F32)


def _sigmoid(z):
    return 1.0 / (1.0 + jnp.exp(-z))


def _rms_matmul(h, g_row, w4, outs, name, tm=1024, tn=512):
    T = h.shape[0]
    per_shard = w4.shape[2] // tn
    starts = [0]
    for width, _ in outs:
        starts.append(starts[-1] + width // tn)
    n_out = len(outs)

    def body(h_ref, g_ref, w_ref, *rest):
        o_refs, u_out, u_s = rest[:n_out], rest[n_out], rest[n_out + 1]
        n = pl.program_id(1)

        @pl.when(n == 0)
        def _():
            x = h_ref[...]
            inv = lax.rsqrt(jnp.mean(x * x, axis=-1, keepdims=True) + RMS_EPS)
            u = (x * inv * g_ref[...]).astype(BF16)
            u_s[...] = u
            u_out[...] = u

        res = _dot(u_s[...], w_ref[0])
        for k in range(n_out):
            @pl.when((n >= starts[k]) & (n < starts[k + 1]))
            def _():
                o_refs[k][...] = res.astype(o_refs[k].dtype)

    def out_map(k):
        return lambda m, n: (m, jnp.clip(n - starts[k], 0, starts[k + 1] - starts[k] - 1))

    return pl.pallas_call(
        body, name=name, grid=(T // tm, starts[-1]),
        in_specs=[pl.BlockSpec((tm, D_MODEL), lambda m, n: (m, 0)),
                  pl.BlockSpec((1, D_MODEL), lambda m, n: (0, 0)),
                  pl.BlockSpec((1, D_MODEL, tn), lambda m, n: (n // per_shard, 0, n % per_shard))],
        out_specs=[pl.BlockSpec((tm, tn), out_map(k)) for k in range(n_out)]
        + [pl.BlockSpec((tm, D_MODEL), lambda m, n: (m, 0))],
        out_shape=[jax.ShapeDtypeStruct((T, width), dt) for width, dt in outs]
        + [jax.ShapeDtypeStruct((T, D_MODEL), BF16)],
        scratch_shapes=[pltpu.VMEM((tm, D_MODEL), BF16)],
        compiler_params=_params(2),
    )(h, g_row, w4)


def _matmul_residual(a, w, res, name, tm=1024, tn=512):
    T, K = a.shape
    N = w.shape[1]

    def body(a_ref, w_ref, r_ref, o_ref):
        o_ref[...] = r_ref[...] + _dot(a_ref[...], w_ref[...])

    return pl.pallas_call(
        body, name=name, grid=(T // tm, N // tn),
        in_specs=[pl.BlockSpec((tm, K), lambda m, n: (m, 0)),
                  pl.BlockSpec((K, tn), lambda m, n: (0, n)),
                  pl.BlockSpec((tm, tn), lambda m, n: (m, n))],
        out_specs=pl.BlockSpec((tm, tn), lambda m, n: (m, n)),
        out_shape=jax.ShapeDtypeStruct((T, N), F32),
        compiler_params=_params(2),
    )(a, w, res)


def _matmul_tn(a, b, a_cols, b_cols, out_shape, out_block, out_map, name, bm, bn, tk=512,
               diagonal_blocks=False):
    T = a.shape[0]

    def body(a_ref, b_ref, o_ref):
        @pl.when(pl.program_id(2) == 0)
        def _():
            o_ref[...] = jnp.zeros_like(o_ref)

        part = _dot_tn(a_ref[...].astype(BF16), b_ref[...].astype(BF16))
        o_ref[...] += part.reshape(o_ref.shape)

    b_map = (lambda i, j, t: (t, i)) if diagonal_blocks else (lambda i, j, t: (t, j))
    return pl.pallas_call(
        body, name=name, grid=(a_cols // bm, 1 if diagonal_blocks else b_cols // bn, T // tk),
        in_specs=[pl.BlockSpec((tk, bm), lambda i, j, t: (t, i)),
                  pl.BlockSpec((tk, bn), b_map)],
        out_specs=pl.BlockSpec(out_block, out_map),
        out_shape=jax.ShapeDtypeStruct(out_shape, F32),
        compiler_params=_params(3),
    )(a, b)


def _matmul_nt_rms_bwd(dproj, w4, h, g_row, dres, name, tm=1024, tk=512):
    T, cols = dproj.shape
    per_shard = w4.shape[2] // tk
    nk = cols // tk

    def body(dp_ref, w_ref, h_ref, g_ref, r_ref, dx_ref, dg_ref, acc):
        m, k = pl.program_id(0), pl.program_id(1)

        @pl.when(k == 0)
        def _():
            acc[...] = jnp.zeros_like(acc)

        @pl.when((k == 0) & (m == 0))
        def _():
            dg_ref[...] = jnp.zeros_like(dg_ref)

        acc[...] += _dot_nt(dp_ref[...], w_ref[0])

        @pl.when(k == nk - 1)
        def _():
            du = acc[...]
            x = h_ref[...]
            inv = lax.rsqrt(jnp.mean(x * x, axis=-1, keepdims=True) + RMS_EPS)
            xhat = x * inv
            dg_ref[...] += jnp.sum(du * xhat, axis=0, keepdims=True)
            dxh = du * g_ref[...]
            proj = jnp.mean(dxh * xhat, axis=-1, keepdims=True)
            dx_ref[...] = r_ref[...] + inv * (dxh - xhat * proj)

    return pl.pallas_call(
        body, name=name, grid=(T // tm, nk),
        in_specs=[pl.BlockSpec((tm, tk), lambda m, k: (m, k)),
                  pl.BlockSpec((1, D_MODEL, tk), lambda m, k: (k // per_shard, 0, k % per_shard)),
                  pl.BlockSpec((tm, D_MODEL), lambda m, k: (m, 0)),
                  pl.BlockSpec((1, D_MODEL), lambda m, k: (0, 0)),
                  pl.BlockSpec((tm, D_MODEL), lambda m, k: (m, 0))],
        out_specs=[pl.BlockSpec((tm, D_MODEL), lambda m, k: (m, 0)),
                   pl.BlockSpec((1, D_MODEL), lambda m, k: (0, 0))],
        out_shape=[jax.ShapeDtypeStruct((T, D_MODEL), F32),
                   jax.ShapeDtypeStruct((1, D_MODEL), F32)],
        scratch_shapes=[pltpu.VMEM((tm, D_MODEL), F32)],
        compiler_params=_params(2),
    )(dproj, w4, h, g_row, dres)


def _window_of(g):
    return jnp.left_shift(2, g)


def _select_stage(g, stages):
    res = stages[0]
    for i in range(1, len(stages)):
        res = jnp.where(g >= i, stages[i], res)
    return res


def _pool_fwd(proj0, wg4, scale_row, S, name, tm=1024):
    T = proj0.shape[0]
    tm = min(tm, S)
    blocks_per_seq = S // tm
    hb = tm // HALO

    def body(x_ref, halo_ref, z_ref, w_ref, s_ref, y_ref, p_ref, mix_ref):
        m, g = pl.program_id(0), pl.program_id(1)
        first = (m % blocks_per_seq) == 0
        halo = jnp.where(first, 0.0, halo_ref[...].astype(F32))
        x = x_ref[...].astype(F32)
        ext = jnp.concatenate([halo, x], axis=0)
        stages = []
        cur = ext
        for sh in (1, 2, 4, 8):
            cur = cur + pltpu.roll(cur, sh, 0)
            stages.append(cur[HALO:, :])
        win_sum = _select_stage(g, stages)
        pos = (m % blocks_per_seq) * tm + lax.broadcasted_iota(jnp.int32, (tm, 1), 0)
        count = jnp.minimum(pos + 1, _window_of(g)).astype(F32)
        pooled = win_sum / count - x
        pooled_b = pooled.astype(BF16)
        mixed = _dot(pooled_b, w_ref[...].reshape(GROUP_DIM, GROUP_DIM))
        z = z_ref[...].astype(F32)
        y_ref[...] = (mixed * s_ref[...] * (z * _sigmoid(z))).astype(BF16)
        p_ref[...] = pooled_b
        mix_ref[...] = mixed.astype(BF16)

    blk = lambda m, g: (m, g)
    return pl.pallas_call(
        body, name=name, grid=(T // tm, N_GROUPS),
        in_specs=[pl.BlockSpec((tm, GROUP_DIM), blk),
                  pl.BlockSpec((HALO, GROUP_DIM), lambda m, g: (jnp.maximum(m * hb - 1, 0), g)),
                  pl.BlockSpec((tm, GROUP_DIM), lambda m, g: (m, N_GROUPS + g)),
                  pl.BlockSpec((N_CHIPS, GROUP_DIM // N_CHIPS, GROUP_DIM), lambda m, g: (0, g, 0)),
                  pl.BlockSpec((1, GROUP_DIM), lambda m, g: (0, g))],
        out_specs=[pl.BlockSpec((tm, GROUP_DIM), blk)] * 3,
        out_shape=[jax.ShapeDtypeStruct((T, D_INNER), BF16)] * 3,
        compiler_params=_params(2),
    )(proj0, proj0, proj0, wg4, scale_row)


def _pool_gate_bwd(dh, w_out, proj0, mixed, scale_row, name, tm=1024, tn=512):
    T = dh.shape[0]
    gate_b0 = D_INNER // tn

    def body(dh_ref, w_ref, z_ref, mix_ref, s_ref, dm_ref, dz_ref, ds_ref):
        @pl.when(pl.program_id(1) == 0)
        def _():
            ds_ref[...] = jnp.zeros_like(ds_ref)

        dy = _dot_nt(dh_ref[...].astype(BF16), w_ref[...])
        z = z_ref[...].astype(F32)
        sig = _sigmoid(z)
        silu = z * sig
        mixed = mix_ref[...].astype(F32)
        s = s_ref[...]
        dm_ref[...] = (dy * s * silu).astype(BF16)
        dz_ref[...] = (dy * mixed * s * (sig * (1.0 + z * (1.0 - sig)))).astype(BF16)
        ds_ref[...] += jnp.sum(dy * mixed * silu, axis=0, keepdims=True)

    return pl.pallas_call(
        body, name=name, grid=(D_INNER // tn, T // tm),
        in_specs=[pl.BlockSpec((tm, D_MODEL), lambda n, m: (m, 0)),
                  pl.BlockSpec((tn, D_MODEL), lambda n, m: (n, 0)),
                  pl.BlockSpec((tm, tn), lambda n, m: (m, gate_b0 + n)),
                  pl.BlockSpec((tm, tn), lambda n, m: (m, n)),
                  pl.BlockSpec((1, tn), lambda n, m: (0, n))],
        out_specs=[pl.BlockSpec((tm, tn), lambda n, m: (m, n)),
                   pl.BlockSpec((tm, tn), lambda n, m: (m, gate_b0 + n)),
                   pl.BlockSpec((1, tn), lambda n, m: (0, n))],
        out_shape=[jax.ShapeDtypeStruct((T, D_INNER), BF16),
                   jax.ShapeDtypeStruct((T, 2 * D_INNER), BF16),
                   jax.ShapeDtypeStruct((1, D_INNER), F32)],
        compiler_params=_params(2),
    )(dh, w_out, proj0, mixed, scale_row)


def _pool_bwd(dmixed, wg4, dproj0, after, S, name, tm=1024):
    T = dmixed.shape[0]
    tm = min(tm, S)
    blocks_per_seq = S // tm
    hb = tm // HALO
    n_halo_blocks = T // HALO

    def body(dm_ref, halo_ref, w_ref, _, __, o_ref):
        m, g = pl.program_id(0), pl.program_id(1)
        ext = jnp.concatenate([dm_ref[...], halo_ref[...]], axis=0)
        dp = _dot_nt(ext, w_ref[...].reshape(GROUP_DIM, GROUP_DIM))
        pos = (m % blocks_per_seq) * tm + lax.broadcasted_iota(jnp.int32, (tm + HALO, 1), 0)
        count = jnp.minimum(pos + 1, _window_of(g)).astype(F32)
        c = jnp.where(pos < S, dp / count, 0.0)
        n = tm + HALO
        stages = []
        cur = c
        for sh in (1, 2, 4, 8):
            cur = cur + pltpu.roll(cur, n - sh, 0)
            stages.append(cur[:tm, :])
        o_ref[...] = (_select_stage(g, stages) - dp[:tm, :]).astype(BF16)

    blk = lambda m, g: (m, g)
    return pl.pallas_call(
        body, name=name, grid=(T // tm, N_GROUPS),
        in_specs=[pl.BlockSpec((tm, GROUP_DIM), blk),
                  pl.BlockSpec((HALO, GROUP_DIM),
                               lambda m, g: (jnp.minimum((m + 1) * hb, n_halo_blocks - 1), g)),
                  pl.BlockSpec((N_CHIPS, GROUP_DIM // N_CHIPS, GROUP_DIM), lambda m, g: (0, g, 0)),
                  HBM_SPEC, ANY_SPEC],
        out_specs=pl.BlockSpec((tm, GROUP_DIM), blk),
        out_shape=jax.ShapeDtypeStruct(dproj0.shape, dproj0.dtype),
        input_output_aliases={3: 0},
        compiler_params=_params(2),
    )(dmixed, dmixed, wg4, dproj0, after)


TQ = 256


def _split_dot(x, m):
    hi = x.astype(BF16)
    lo = (x - hi.astype(F32)).astype(BF16)
    return _dot(hi, m) + _dot(lo, m)


NEG_LOG2E = -1.4426950408889634


def _log_terms(z):
    soft = jnp.log(1.0 + jnp.exp2(jnp.abs(z) * NEG_LOG2E))
    log_beta = jnp.minimum(z, 0.0) - soft
    return log_beta, log_beta - z


N_HEADS = 16
FWD_HEADS = BWD_HEADS = 4


def _masked_heads(x, heads):
    lane = lax.broadcasted_iota(jnp.int32, (1, HEAD_PAIR_QK), 1)
    out = []
    for hh in range(heads):
        slab = x[:, (hh // 2) * HEAD_PAIR_QK:(hh // 2 + 1) * HEAD_PAIR_QK]
        out.append(jnp.where((lane // 64) == hh % 2, slab, jnp.zeros_like(slab)))
    return out


def _attn_fwd(qkvz, S, name):
    T = qkvz.shape[0]
    nq = S // TQ
    HEADS, QK_W, V_W = FWD_HEADS, FWD_HEADS * 64, FWD_HEADS * HEAD_V
    k_b0 = QK_WIDTH // QK_W
    v_b0 = 2 * QK_WIDTH // V_W
    z_b0 = (2 * QK_WIDTH + D_INNER) // V_W
    hs = range(HEADS)

    def body(q_ref, k_ref, v_ref, z_ref, o_ref, y_ref, lt_ref):
        row = lax.broadcasted_iota(jnp.int32, (TQ, TQ), 0)
        col = lax.broadcasted_iota(jnp.int32, (TQ, TQ), 1)
        causal = col < row
        later_in_block = (row > col).astype(BF16)
        lax.fori_loop(0, nq, lambda qi, _: q_block(qi, causal, later_in_block,
                                                   q_ref, k_ref, v_ref, z_ref, o_ref, y_ref, lt_ref), 0)

    def q_block(qi, causal, later_in_block, q_ref, k_ref, v_ref, z_ref, o_ref, y_ref, lt_ref):
        rows = pl.ds(pl.multiple_of(qi * TQ, TQ), TQ)
        qms = [qm * 0.125 for qm in _masked_heads(q_ref[rows, :], HEADS)]

        def step(j, carry, diagonal):
            koff = pl.multiple_of(j * TQ, TQ)
            kbs = [k_ref[pl.ds(koff, TQ), p * HEAD_PAIR_QK:(p + 1) * HEAD_PAIR_QK]
                   for p in range(HEADS // 2)]
            run, acc = [carry[2 * hh] for hh in hs], [carry[2 * hh + 1] for hh in hs]
            z = [_dot_nt(qms[hh], kbs[hh // 2]) for hh in hs]
            terms = [_log_terms(z[hh]) for hh in hs]
            log_om = [jnp.where(causal, t[1], 0.0) if diagonal else t[1] for t in terms]
            later = [_split_dot(log_om[hh], later_in_block) for hh in hs]
            a = [jnp.exp(terms[hh][0] + (run[hh] + later[hh])) for hh in hs]
            if diagonal:
                a = [jnp.where(causal, a[hh], 0.0) for hh in hs]
            out = []
            for hh in hs:
                vb = v_ref[pl.ds(koff, TQ), hh * HEAD_V:(hh + 1) * HEAD_V]
                out += [run[hh] + jnp.sum(log_om[hh], axis=1, keepdims=True),
                        acc[hh] + _dot(a[hh].astype(BF16), vb)]
            return tuple(out)

        zero = (jnp.zeros((TQ, 1), F32), jnp.zeros((TQ, HEAD_V), F32))
        carry = step(qi, zero * HEADS, True)
        carry = lax.fori_loop(0, qi, lambda i, c: step(qi - 1 - i, c, False), carry)
        for hh in hs:
            sl = slice(hh * HEAD_V, (hh + 1) * HEAD_V)
            acc = carry[2 * hh + 1]
            z = z_ref[rows, sl].astype(F32)
            o_ref[rows, sl] = acc.astype(BF16)
            y_ref[rows, sl] = (acc * (z * _sigmoid(z))).astype(BF16)
            lt_ref[rows, hh:hh + 1] = carry[2 * hh]
        return 0

    blk = lambda b, p: (b, p)
    return pl.pallas_call(
        body, name=name, grid=(T // S, N_HEADS // HEADS),
        in_specs=[pl.BlockSpec((S, QK_W), blk),
                  pl.BlockSpec((S, QK_W), lambda b, p: (b, k_b0 + p)),
                  pl.BlockSpec((S, V_W), lambda b, p: (b, v_b0 + p)),
                  pl.BlockSpec((S, V_W), lambda b, p: (b, z_b0 + p))],
        out_specs=[pl.BlockSpec((S, V_W), blk),
                   pl.BlockSpec((S, V_W), blk),
                   pl.BlockSpec((None, S, HEADS), lambda b, p: (p, b, 0))],
        out_shape=[jax.ShapeDtypeStruct((T, D_INNER), BF16),
                   jax.ShapeDtypeStruct((T, D_INNER), BF16),
                   jax.ShapeDtypeStruct((N_HEADS // HEADS, T, HEADS), F32)],
        compiler_params=_params(2),
    )(qkvz, qkvz, qkvz, qkvz)


def _attn_gate_bwd(dh, w_out, qkvz, o, after, name, tm=1024, tn=512):
    T = dh.shape[0]
    gate_b0 = (2 * QK_WIDTH + D_INNER) // tn

    def body(dh_ref, w_ref, z_ref, o_ref, _, do_ref, dz_ref, dh_s):
        @pl.when(pl.program_id(1) == 0)
        def _():
            dh_s[...] = dh_ref[...].astype(BF16)

        dy = _dot_nt(dh_s[...], w_ref[...])
        z = z_ref[...].astype(F32)
        sig = _sigmoid(z)
        do_ref[...] = (dy * (z * sig)).astype(BF16)
        dz_ref[...] = (dy * o_ref[...].astype(F32) * (sig * (1.0 + z * (1.0 - sig)))).astype(BF16)

    return pl.pallas_call(
        body, name=name, grid=(T // tm, D_INNER // tn),
        in_specs=[pl.BlockSpec((tm, D_MODEL), lambda m, n: (m, 0)),
                  pl.BlockSpec((tn, D_MODEL), lambda m, n: (n, 0)),
                  pl.BlockSpec((tm, tn), lambda m, n: (m, gate_b0 + n)),
                  pl.BlockSpec((tm, tn), lambda m, n: (m, n)),
                  ANY_SPEC],
        out_specs=[pl.BlockSpec((tm, tn), lambda m, n: (m, n)),
                   pl.BlockSpec((tm, tn), lambda m, n: (m, gate_b0 + n))],
        out_shape=[jax.ShapeDtypeStruct((T, D_INNER), BF16),
                   jax.ShapeDtypeStruct((T, 2 * QK_WIDTH + 2 * D_INNER), BF16)],
        scratch_shapes=[pltpu.VMEM((tm, D_MODEL), BF16)],
        compiler_params=_params(2),
    )(dh, w_out, qkvz, o, after)


def _attn_bwd(qkv, do, ltot, dproj1, S, name):
    T = qkv.shape[0]
    nq = S // TQ
    HEADS, QK_W, V_W = BWD_HEADS, BWD_HEADS * 64, BWD_HEADS * HEAD_V
    k_b0 = QK_WIDTH // QK_W
    v_b0 = 2 * QK_WIDTH // V_W
    hs = range(HEADS)
    pairs = range(HEADS // 2)

    def body(q_ref, k_ref, v_ref, do_ref, lt_ref, _, out_ref, dq_s, dk_s, dv_s, dkb_s, dvb_s, sems):
        b, p = pl.program_id(0), pl.program_id(1)
        row = lax.broadcasted_iota(jnp.int32, (TQ, TQ), 0)
        col = lax.broadcasted_iota(jnp.int32, (TQ, TQ), 1)
        causal = col < row
        upto = (row <= col).astype(BF16)
        before = (row < col).astype(BF16)
        dk_s[...] = jnp.zeros_like(dk_s)
        dv_s[...] = jnp.zeros_like(dv_s)

        def q_block(qi, _):
            qoff = pl.multiple_of(qi * TQ, TQ)
            qms = [qm * 0.125 for qm in _masked_heads(q_ref[pl.ds(qoff, TQ), :], HEADS)]
            vsl = [slice(hh * HEAD_V, (hh + 1) * HEAD_V) for hh in hs]
            psl = [slice(pp * HEAD_PAIR_QK, (pp + 1) * HEAD_PAIR_QK) for pp in pairs]
            do_h = [do_ref[pl.ds(qoff, TQ), sl] for sl in vsl]
            total = [lt_ref[pl.ds(qoff, TQ), hh:hh + 1] for hh in hs]

            def k_block(j, carry, diagonal):
                koff = pl.multiple_of(j * TQ, TQ)
                kms = _masked_heads(k_ref[pl.ds(koff, TQ), :], HEADS)
                g_before = [carry[2 * hh] for hh in hs]
                lom_before = [carry[2 * hh + 1] for hh in hs]
                z = [_dot_nt(qms[hh], kms[hh]) for hh in hs]
                da = [_dot_nt(do_h[hh], v_ref[pl.ds(koff, TQ), vsl[hh]]) for hh in hs]
                terms = [_log_terms(z[hh]) for hh in hs]
                log_om = [jnp.where(causal, t[1], 0.0) if diagonal else t[1] for t in terms]
                prefix = [_split_dot(log_om[hh], upto) for hh in hs]
                a = [jnp.exp(terms[hh][0] + ((total[hh] - lom_before[hh]) - prefix[hh])) for hh in hs]
                if diagonal:
                    a = [jnp.where(causal, a[hh], 0.0) for hh in hs]
                g = [a[hh] * da[hh] for hh in hs]
                g_prefix = [_dot(g[hh].astype(BF16), before) for hh in hs]
                out, dzs = [], []
                for hh in hs:
                    beta = jnp.exp(terms[hh][0])
                    g_excl = (g_before[hh] + g_prefix[hh]) * beta
                    if diagonal:
                        g_excl = jnp.where(causal, g_excl, 0.0)
                    dzs.append((g[hh] * (1.0 - beta) - g_excl).astype(BF16))
                    out += [g_before[hh] + jnp.sum(g[hh], axis=1, keepdims=True),
                            lom_before[hh] + jnp.sum(log_om[hh], axis=1, keepdims=True)]
                for hh in hs:
                    dv_s[pl.ds(koff, TQ), vsl[hh]] += _dot_tn(a[hh].astype(BF16), do_h[hh])
                dq = []
                for pp in pairs:
                    pair = slice(2 * pp, 2 * pp + 2)
                    dq.append(carry[2 * HEADS + pp] + _dot(jnp.concatenate(dzs[pair], axis=1),
                                                           jnp.concatenate(kms[pair], axis=0)))
                    dk_s[pl.ds(koff, TQ), psl[pp]] += _dot_tn(jnp.concatenate(dzs[pair], axis=0),
                                                              jnp.concatenate(qms[pair], axis=0))
                return tuple(out) + tuple(dq)

            zero = jnp.zeros((TQ, 1), F32)
            carry = (zero,) * (2 * HEADS) + (jnp.zeros((TQ, HEAD_PAIR_QK), F32),) * (HEADS // 2)
            carry = lax.fori_loop(0, qi, lambda j, c: k_block(j, c, False), carry)
            carry = k_block(qi, carry, True)
            for pp in pairs:
                dq_s[pl.ds(qoff, TQ), psl[pp]] = (carry[2 * HEADS + pp] * 0.125).astype(BF16)
            return 0

        lax.fori_loop(0, nq, q_block, 0)
        dkb_s[...] = dk_s[...].astype(BF16)
        dvb_s[...] = dv_s[...].astype(BF16)
        rows = pl.ds(pl.multiple_of(b * S, TQ), S)
        copies = [
            pltpu.make_async_copy(
                dq_s, out_ref.at[rows, pl.ds(pl.multiple_of(p * QK_W, 128), QK_W)], sems.at[0]),
            pltpu.make_async_copy(
                dkb_s, out_ref.at[rows, pl.ds(pl.multiple_of(QK_WIDTH + p * QK_W, 128), QK_W)],
                sems.at[1]),
            pltpu.make_async_copy(
                dvb_s, out_ref.at[rows, pl.ds(pl.multiple_of(2 * QK_WIDTH + p * V_W, 128), V_W)],
                sems.at[2]),
        ]
        for cp in copies:
            cp.start()
        for cp in copies:
            cp.wait()

    return pl.pallas_call(
        body, name=name, grid=(T // S, N_HEADS // HEADS),
        in_specs=[pl.BlockSpec((S, QK_W), lambda b, p: (b, p)),
                  pl.BlockSpec((S, QK_W), lambda b, p: (b, k_b0 + p)),
                  pl.BlockSpec((S, V_W), lambda b, p: (b, v_b0 + p)),
                  pl.BlockSpec((S, V_W), lambda b, p: (b, p)),
                  pl.BlockSpec((None, S, HEADS), lambda b, p: (p, b, 0)),
                  HBM_SPEC],
        out_specs=HBM_SPEC,
        out_shape=jax.ShapeDtypeStruct(dproj1.shape, dproj1.dtype),
        input_output_aliases={5: 0},
        scratch_shapes=[pltpu.VMEM((S, QK_W), BF16),
                        pltpu.VMEM((S, QK_W), F32),
                        pltpu.VMEM((S, V_W), F32),
                        pltpu.VMEM((S, QK_W), BF16),
                        pltpu.VMEM((S, V_W), BF16),
                        pltpu.SemaphoreType.DMA((3,))],
        compiler_params=_params(2),
    )(qkv, qkv, qkv, do, ltot, dproj1)


def _loss_head(h, g_row, target, name, tm=512):
    T = h.shape[0]

    def body(h_ref, g_ref, t_ref, dh_ref, dg_ref, loss_ref):
        @pl.when(pl.program_id(0) == 0)
        def _():
            dg_ref[...] = jnp.zeros_like(dg_ref)
            loss_ref[...] = jnp.zeros_like(loss_ref)

        x = h_ref[...]
        inv = lax.rsqrt(jnp.mean(x * x, axis=-1, keepdims=True) + RMS_EPS)
        xhat = x * inv
        gain = g_ref[...]
        err = xhat * gain - t_ref[...]
        per_token = jnp.mean(err * err, axis=-1, keepdims=True)
        loss_ref[...] += 0.5 * jnp.sum(per_token, axis=0, keepdims=True)
        dy = err * (1.0 / D_MODEL)
        dg_ref[...] += jnp.sum(dy * xhat, axis=0, keepdims=True)
        dxh = dy * gain
        proj = jnp.mean(dxh * xhat, axis=-1, keepdims=True)
        dh_ref[...] = inv * (dxh - xhat * proj)

    return pl.pallas_call(
        body, name=name, grid=(T // tm,),
        in_specs=[pl.BlockSpec((tm, D_MODEL), lambda m: (m, 0)),
                  pl.BlockSpec((1, D_MODEL), lambda m: (0, 0)),
                  pl.BlockSpec((tm, D_MODEL), lambda m: (m, 0))],
        out_specs=[pl.BlockSpec((tm, D_MODEL), lambda m: (m, 0)),
                   pl.BlockSpec((1, D_MODEL), lambda m: (0, 0)),
                   pl.BlockSpec((1, 128), lambda m: (0, 0))],
        out_shape=[jax.ShapeDtypeStruct((T, D_MODEL), F32),
                   jax.ShapeDtypeStruct((1, D_MODEL), F32),
                   jax.ShapeDtypeStruct((1, 128), F32)],
        compiler_params=_params(1),
    )(h, g_row, target)


def _place():
    return lax.axis_index("x"), lax.axis_index("y"), lax.axis_index("c")


def _other_chips(x, y):
    return [(1 - x, y), (x, 1 - y), (1 - x, 1 - y)]


def _half(ref, c):
    hr = ref.shape[-2] // 2
    return pl.ds(pl.multiple_of(c * hr, 8), hr)


def _cast_to_slot(shard, chip, name, tr=256):
    R, C = shard.shape

    def body(chip_ref, w_ref, o_ref):
        o_ref[0] = w_ref[...].astype(BF16)

    return pl.pallas_call(
        body, name=name,
        grid_spec=pltpu.PrefetchScalarGridSpec(
            num_scalar_prefetch=1, grid=(R // tr,),
            in_specs=[pl.BlockSpec((tr, C), lambda i, chip_ref: (i, 0))],
            out_specs=pl.BlockSpec((1, tr, C), lambda i, chip_ref: (chip_ref[0], i, 0))),
        out_shape=jax.ShapeDtypeStruct((N_CHIPS, R, C), BF16),
        compiler_params=_params(1),
    )(chip, shard)


def _weight_plan(bufs):
    x, y, c = _place()
    plan = []
    for buf in bufs:
        mine = buf.at[2 * x + y, _half(buf, c)]
        for ox, oy in _other_chips(x, y):
            plan.append((mine, mine, (ox, oy, c), buf.at[2 * ox + oy, _half(buf, c)]))
    return plan


def _chip_sum_plan(bufs):
    x, y, c = _place()
    n = len(bufs) // 2
    plan = []
    for sums, land in zip(bufs[:n], bufs[n:]):
        for k, (ox, oy) in enumerate(_other_chips(x, y)):
            plan.append((sums.at[2 * ox + oy], land.at[k], (ox, oy, c), land.at[k]))
    return plan


SEM_SPEC = pl.BlockSpec(memory_space=pltpu.SEMAPHORE)
ANY_SPEC = pl.BlockSpec(memory_space=pl.ANY)
DATAFLOW = pltpu.SideEffectType.DATAFLOW_SIDE_EFFECTING


def _in_hbm(a):
    return pltpu.with_memory_space_constraint(a, pltpu.HBM)


def _exchange_start(bufs, after, plan, n_copies, name):
    nb = len(bufs)

    def body(*refs):
        send_sems, recv_sems = refs[nb + 1], refs[nb + 2]
        for i, (src, dst, dev, _) in enumerate(plan(refs[:nb])):
            pltpu.make_async_remote_copy(
                src_ref=src, dst_ref=dst, send_sem=send_sems.at[i], recv_sem=recv_sems.at[i],
                device_id=dev, device_id_type=MESH).start()
        token = refs[-1]
        token[...] = jnp.zeros_like(token)

    res = pl.pallas_call(
        body, name=name,
        in_specs=[HBM_SPEC] * nb + [ANY_SPEC],
        out_specs=[SEM_SPEC, SEM_SPEC] + [HBM_SPEC] * nb + [pl.BlockSpec(memory_space=pltpu.VMEM)],
        out_shape=[pltpu.SemaphoreType.DMA((n_copies,)), pltpu.SemaphoreType.DMA((n_copies,))]
        + [pltpu.HBM(b.shape, b.dtype) for b in bufs] + [jax.ShapeDtypeStruct((8, 128), F32)],
        input_output_aliases={i: 2 + i for i in range(nb)},
        compiler_params=pltpu.CompilerParams(has_side_effects=DATAFLOW),
    )(*[_in_hbm(b) for b in bufs], after)
    return res[0], res[1], list(res[2:2 + nb]), res[-1]


def _exchange_wait(bufs, send_sems, recv_sems, after, plan, name):
    nb = len(bufs)

    def body(*refs):
        sends, recvs = refs[nb], refs[nb + 1]
        for i, (src, dst, dev, landing) in enumerate(plan(refs[:nb])):
            pltpu.make_async_remote_copy(
                src_ref=src, dst_ref=landing, send_sem=sends.at[i], recv_sem=recvs.at[i],
                device_id=dev, device_id_type=MESH).wait()

    res = pl.pallas_call(
        body, name=name,
        in_specs=[HBM_SPEC] * nb + [SEM_SPEC, SEM_SPEC, ANY_SPEC],
        out_specs=[HBM_SPEC] * nb,
        out_shape=[pltpu.HBM(b.shape, b.dtype) for b in bufs],
        input_output_aliases={i: i for i in range(nb)},
        compiler_params=pltpu.CompilerParams(has_side_effects=DATAFLOW),
    )(*bufs, send_sems, recv_sems, after)
    return list(res)


def _allgather_weights(slots, name, landed=False):
    n = len(slots)

    def body(*refs):
        outs = refs[n:2 * n]
        send_sems, recv_sems, fwd_send, fwd_recv = refs[2 * n:]
        x, y, c = _place()
        chips = _other_chips(x, y)

        def landing(a, chip, half_of):
            return outs[a].at[2 * chip[0] + chip[1], _half(outs[a], half_of)]

        def ici(a, k, chip_from, to):
            return pltpu.make_async_remote_copy(
                src_ref=landing(a, chip_from, c), dst_ref=landing(a, chip_from, c),
                send_sem=send_sems.at[a, k], recv_sem=recv_sems.at[a, k],
                device_id=to, device_id_type=MESH)

        def d2d(a, k, chip_from, half_of):
            return pltpu.make_async_remote_copy(
                src_ref=landing(a, chip_from, half_of), dst_ref=landing(a, chip_from, half_of),
                send_sem=fwd_send.at[a, k], recv_sem=fwd_recv.at[a, k],
                device_id=(x, y, 1 - c), device_id_type=MESH)

        sends = []
        if not landed:
            sends = [ici(a, k, (x, y), (*chips[k], c)) for a in range(n) for k in range(3)]
        for cp in sends:
            cp.start()
        forwards = []
        for a in range(n):
            for k in range(3):
                if not landed:
                    ici(a, k, chips[k], (x, y, c)).wait_recv()
                fw = d2d(a, k, chips[k], c)
                fw.start()
                forwards.append(fw)
        for a in range(n):
            for k in range(3):
                d2d(a, k, chips[k], 1 - c).wait_recv()
        for cp in sends + forwards:
            cp.wait_send()

    return pl.pallas_call(
        body, name=name,
        in_specs=[HBM_SPEC] * n, out_specs=[HBM_SPEC] * n,
        out_shape=[jax.ShapeDtypeStruct(s.shape, s.dtype) for s in slots],
        input_output_aliases={a: a for a in range(n)},
        scratch_shapes=[pltpu.SemaphoreType.DMA((n, 3)), pltpu.SemaphoreType.DMA((n, 3)),
                        pltpu.SemaphoreType.DMA((n, 3)), pltpu.SemaphoreType.DMA((n, 3))],
    )(*slots)


def _sibling_exchange(partials, small, name):
    n = len(partials)
    ns = 0 if small is None else 1

    def body(*refs):
        ins, outs = refs[:n], refs[n + ns:2 * n + ns]
        send_sems, recv_sems = refs[2 * (n + ns):2 * (n + ns) + 2]
        x, y, c = _place()
        me = 4 * x + 2 * y + c
        sends = [pltpu.make_async_remote_copy(
            src_ref=ins[a].at[:, _half(ins[a], 1 - c)], dst_ref=outs[a],
            send_sem=send_sems.at[a], recv_sem=recv_sems.at[a],
            device_id=(x, y, 1 - c), device_id_type=MESH) for a in range(n)]
        if ns:
            small_ref, small_all = refs[n], refs[2 * n + 1]
            s_send, s_recv, loc_sem = refs[2 * (n + ns) + 2:]
            local = pltpu.make_async_copy(small_ref, small_all.at[me], loc_sem)
            local.start()
            for d in range(1, N_DEV):
                px, py, pc = x ^ ((d >> 2) & 1), y ^ ((d >> 1) & 1), c ^ (d & 1)
                sends.append(pltpu.make_async_remote_copy(
                    src_ref=small_ref, dst_ref=small_all.at[me],
                    send_sem=s_send.at[d - 1], recv_sem=s_recv.at[d - 1],
                    device_id=(px, py, pc), device_id_type=MESH))
        for cp in sends:
            cp.start()
        if ns:
            for d in range(1, N_DEV):
                pltpu.make_async_remote_copy(
                    src_ref=small_ref, dst_ref=small_all.at[me ^ d],
                    send_sem=s_send.at[d - 1], recv_sem=s_recv.at[d - 1],
                    device_id=(x, y, c), device_id_type=MESH).wait_recv()
        for cp in sends[:n]:
            cp.wait_recv()
        for cp in sends:
            cp.wait_send()
        if ns:
            local.wait()

    out_shape = [jax.ShapeDtypeStruct((N_CHIPS, p.shape[1] // 2, p.shape[2]), F32) for p in partials]
    scratch = [pltpu.SemaphoreType.DMA((max(n, 1),)), pltpu.SemaphoreType.DMA((max(n, 1),))]
    if ns:
        out_shape.append(jax.ShapeDtypeStruct((N_DEV,) + small.shape, F32))
        scratch += [pltpu.SemaphoreType.DMA((N_DEV - 1,)), pltpu.SemaphoreType.DMA((N_DEV - 1,)),
                    pltpu.SemaphoreType.DMA]
    return pl.pallas_call(
        body, name=name,
        in_specs=[HBM_SPEC] * (n + ns), out_specs=[HBM_SPEC] * (n + ns),
        out_shape=out_shape, scratch_shapes=scratch,
    )(*partials, *([small] if ns else []))


def _chip_sum(partial, from_sibling, c, name, tr=256):
    _, hr, C = from_sibling.shape
    nb = hr // tr

    def body(c_ref, p_ref, s_ref, o_ref):
        o_ref[...] = (p_ref[...] + s_ref[...]).astype(BF16)

    return pl.pallas_call(
        body, name=name,
        grid_spec=pltpu.PrefetchScalarGridSpec(
            num_scalar_prefetch=1, grid=(N_CHIPS, nb),
            in_specs=[pl.BlockSpec((1, tr, C), lambda j, i, c_ref: (j, c_ref[0] * nb + i, 0)),
                      pl.BlockSpec((1, tr, C), lambda j, i, c_ref: (j, i, 0))],
            out_specs=pl.BlockSpec((1, tr, C), lambda j, i, c_ref: (j, i, 0))),
        out_shape=jax.ShapeDtypeStruct(from_sibling.shape, BF16),
        compiler_params=_params(2),
    )(c, partial, from_sibling)


def _reduce_half(partial, from_sibling, received, place, name, tr=256):
    _, hr, C = from_sibling.shape
    nb = hr // tr

    def body(p_ref, mine_ref, sib_ref, r_ref, o_ref):
        acc = mine_ref[0] + sib_ref[0]
        for k in range(3):
            acc = acc + r_ref[k].astype(F32)
        o_ref[...] = acc

    return pl.pallas_call(
        body, name=name,
        grid_spec=pltpu.PrefetchScalarGridSpec(
            num_scalar_prefetch=1, grid=(nb,),
            in_specs=[pl.BlockSpec((1, tr, C), lambda i, p: (p[0], p[1] * nb + i, 0)),
                      pl.BlockSpec((1, tr, C), lambda i, p: (p[0], i, 0)),
                      pl.BlockSpec((3, tr, C), lambda i, p: (0, i, 0))],
            out_specs=pl.BlockSpec((tr, C), lambda i, p: (p[1] * nb + i, 0))),
        out_shape=jax.ShapeDtypeStruct((2 * hr, C), F32),
        compiler_params=_params(1),
    )(place, partial, from_sibling, received)


def _join_halves(fulls, name):
    n = len(fulls)

    def body(*refs):
        outs = refs[n:2 * n]
        send_sems, recv_sems = refs[2 * n:]
        x, y, c = _place()

        def copy(a, half_of, to):
            rows = outs[a].at[_half(outs[a], half_of)]
            return pltpu.make_async_remote_copy(
                src_ref=rows, dst_ref=rows, send_sem=send_sems.at[a], recv_sem=recv_sems.at[a],
                device_id=to, device_id_type=MESH)

        sends = [copy(a, c, (x, y, 1 - c)) for a in range(n)]
        for cp in sends:
            cp.start()
        for a in range(n):
            copy(a, 1 - c, (x, y, c)).wait_recv()
        for cp in sends:
            cp.wait_send()

    return pl.pallas_call(
        body, name=name,
        in_specs=[HBM_SPEC] * n, out_specs=[HBM_SPEC] * n,
        out_shape=[jax.ShapeDtypeStruct(f.shape, F32) for f in fulls],
        input_output_aliases={a: a for a in range(n)},
        scratch_shapes=[pltpu.SemaphoreType.DMA((n,)), pltpu.SemaphoreType.DMA((n,))],
    )(*fulls)


def _adamw_math(w, g, m, v):
    m = ADAM_B1 * m + (1.0 - ADAM_B1) * g
    v = ADAM_B2 * v + (1.0 - ADAM_B2) * (g * g)
    m_hat = m / (1.0 - ADAM_B1 ** ADAM_STEP)
    v_hat = v / (1.0 - ADAM_B2 ** ADAM_STEP)
    delta = -ADAM_LR * (m_hat / (jnp.sqrt(v_hat) + ADAM_EPS) + ADAM_WD * w)
    return delta, m, v


def _adamw(w, g, m, v, name, tr=256):
    R, C = w.shape
    tr = min(tr, R)

    def body(w_ref, g_ref, m_ref, v_ref, d_out, m_out, v_out):
        d_out[...], m_out[...], v_out[...] = _adamw_math(w_ref[...], g_ref[...], m_ref[...], v_ref[...])

    spec = pl.BlockSpec((tr, C), lambda i: (i, 0))
    return pl.pallas_call(
        body, name=name, grid=(R // tr,),
        in_specs=[spec] * 4, out_specs=[spec] * 3,
        out_shape=[jax.ShapeDtypeStruct((R, C), F32)] * 3,
        compiler_params=_params(1),
    )(w, g, m, v)


def _adamw_small(small_all, w, m, v, name):
    def body(s_ref, w_ref, m_ref, v_ref, g_out, d_out, m_out, v_out):
        g = s_ref[0]
        for d in range(1, N_DEV):
            g = g + s_ref[d]
        g_out[...] = g
        d_out[...], m_out[...], v_out[...] = _adamw_math(w_ref[...], g, m_ref[...], v_ref[...])

    vm = pl.BlockSpec(memory_space=pltpu.VMEM)
    return pl.pallas_call(
        body, name=name, in_specs=[vm] * 4, out_specs=[vm] * 4,
        out_shape=[jax.ShapeDtypeStruct(w.shape, F32)] * 4,
    )(small_all, w, m, v)


def _pack_small(norm_g, pool_scale, norm_f, extra_row):
    return jnp.concatenate([norm_g.reshape(2, D_MODEL), pool_scale.reshape(2, D_MODEL),
                            norm_f.reshape(1, D_MODEL), extra_row,
                            jnp.zeros((2, D_MODEL), F32)], axis=0)


def kernel(x, norm_g, pool_w_in, pool_w, pool_scale, pool_w_out, sb_w_in, sb_w_out, norm_f, loss_target, m_norm_g, m_pool_w_in, m_pool_w, m_pool_scale, m_pool_w_out, m_sb_w_in, m_sb_w_out, m_norm_f, v_norm_g, v_pool_w_in, v_pool_w, v_pool_scale, v_pool_w_out, v_sb_w_in, v_sb_w_out, v_norm_f):
    nb, S, _ = x.shape
    T = nb * S
    xt = x.reshape(T, D_MODEL)
    target = loss_target.reshape(T, D_MODEL)
    cx, cy, cc = _place()

    def shard2d(w):
        return w.reshape(-1, w.shape[-1])

    names = ("pool_w_in", "pool_w", "pool_w_out", "sb_w_in", "sb_w_out")
    w_shards = [shard2d(w) for w in (pool_w_in, pool_w, pool_w_out, sb_w_in, sb_w_out)]
    m_shards = [shard2d(w) for w in (m_pool_w_in, m_pool_w, m_pool_w_out, m_sb_w_in, m_sb_w_out)]
    v_shards = [shard2d(w) for w in (v_pool_w_in, v_pool_w, v_pool_w_out, v_sb_w_in, v_sb_w_out)]

    chip = (2 * cx + cy).reshape(1).astype(jnp.int32)
    c_arr = cc.reshape(1).astype(jnp.int32)
    place = jnp.stack([2 * cx + cy, cc]).astype(jnp.int32)
    slots = [_cast_to_slot(w, chip, "cast_" + nm) for w, nm in zip(w_shards, names)]
    g0, g1, gf = norm_g[0:1], norm_g[1:2], norm_f.reshape(1, D_MODEL)

    w_pin, = _allgather_weights(slots[:1], "allgather_pool_in_weights")
    mix_send, mix_recv, mix_slots, token = _exchange_start(slots[1:3], w_pin, _weight_plan, 6,
                                                           "pool_weights_start")
    sb_send, sb_recv, sb_slots, token = _exchange_start(slots[3:], token, _weight_plan, 6,
                                                        "sb_weights_start")

    proj0, u0 = _rms_matmul(xt, g0 + token[0:1, 0:1], w_pin, [(2 * D_INNER, BF16)], "pool_in_proj",
                            tn=w_pin.shape[2])
    mix_slots = _exchange_wait(mix_slots, mix_send, mix_recv, proj0, _weight_plan, "pool_weights_wait")
    w_g, w_pout = _allgather_weights(mix_slots, "pool_weights_forward", landed=True)
    w_pout = w_pout.reshape(D_INNER, D_MODEL)
    y0, pooled, mixed = _pool_fwd(proj0, w_g, pool_scale, S, "pool_mix")
    sb_slots = _exchange_wait(sb_slots, sb_send, sb_recv, y0, _weight_plan, "sb_weights_wait")
    w_sin, w_sout = _allgather_weights(sb_slots, "sb_weights_forward", landed=True)
    w_sout = w_sout.reshape(D_INNER, D_MODEL)
    h1 = _matmul_residual(y0, w_pout, xt, "pool_out_proj")
    n1 = 2 * QK_WIDTH + 2 * D_INNER
    qkvz, u1 = _rms_matmul(h1, g1, w_sin, [(n1, BF16)], "sb_in_proj", tn=w_sin.shape[2])
    o, y1, ltot = _attn_fwd(qkvz, S, "sb_attention")
    h2 = _matmul_residual(y1, w_sout, h1, "sb_out_proj")
    dh2, d_norm_f, loss_row = _loss_head(h2, gf, target, "loss_head")

    def reduce_start(partials, tag):
        from_sibling = _sibling_exchange(partials, None, "grad_sibling_exchange_" + tag)
        sums = [_chip_sum(p, s, c_arr, "grad_chip_sum_%s_%d" % (tag, i))
                for i, (p, s) in enumerate(zip(partials, from_sibling))]
        lands = [lax.empty((3,) + s.shape[1:], BF16) for s in sums]
        send, recv, bufs, token = _exchange_start(sums + lands, c_arr, _chip_sum_plan, 3 * len(sums),
                                                  "grad_chip_exchange_start_" + tag)
        return (partials, list(from_sibling), send, recv, bufs), token[0:1, 0:1]

    def reduce_finish(started, after, tag):
        partials, from_sibling, send, recv, bufs = started
        received = _exchange_wait(bufs, send, recv, after, _chip_sum_plan,
                                  "grad_chip_exchange_wait_" + tag)[len(partials):]
        return [_reduce_half(p, s, r, place, "grad_reduce_%s_%d" % (tag, i))
                for i, (p, s, r) in enumerate(zip(partials, from_sibling, received))]

    shard = lambda i, j, t: (j, 0, 0)
    gw_sout = _matmul_tn(y1, dh2, D_INNER, D_MODEL, (D_INNER, D_MODEL), (1024, 1024),
                         lambda i, j, t: (i, j), "grad_sb_w_out", bm=1024, bn=1024)
    sout_started, token = reduce_start([gw_sout.reshape(N_CHIPS, -1, D_MODEL)], "sb_out")
    do, dproj1 = _attn_gate_bwd(dh2, w_sout, qkvz, o, token, "sb_gate_bwd")
    dproj1 = _attn_bwd(qkvz, do, ltot, dproj1, S, "sb_attention_bwd")
    gw_sin = _matmul_tn(u1, dproj1, D_MODEL, n1, (N_CHIPS, D_MODEL, n1 // 4), (1, D_MODEL, n1 // 4),
                        shard, "grad_sb_w_in", bm=D_MODEL, bn=n1 // 4)
    sin_started, token = reduce_start([gw_sin], "sb_in")
    dh1, d_g1 = _matmul_nt_rms_bwd(dproj1, w_sin, h1, g1 + token, dh2, "sb_in_bwd", tk=w_sin.shape[2])
    gw_pout = _matmul_tn(y0, dh1, D_INNER, D_MODEL, (D_INNER, D_MODEL), (1024, 1024),
                         lambda i, j, t: (i, j), "grad_pool_w_out", bm=1024, bn=1024)
    dmixed, dproj0, d_scale = _pool_gate_bwd(dh1, w_pout, proj0, mixed, pool_scale, "pool_gate_bwd")
    gw_g = _matmul_tn(pooled, dmixed, D_INNER, D_INNER, (N_CHIPS, GROUP_DIM, GROUP_DIM),
                      (N_CHIPS, GROUP_DIM // N_CHIPS, GROUP_DIM), lambda i, j, t: (0, i, 0),
                      "grad_pool_w", bm=GROUP_DIM, bn=GROUP_DIM, diagonal_blocks=True)
    mix_started, token = reduce_start([gw_g, gw_pout.reshape(N_CHIPS, -1, D_MODEL)], "pool_mix")
    dproj0 = _pool_bwd(dmixed, w_g, dproj0, token, S, "pool_bwd")
    n0 = 2 * D_INNER
    gw_pin = _matmul_tn(u0, dproj0, D_MODEL, n0, (N_CHIPS, D_MODEL, n0 // 4), (1, D_MODEL, n0 // 4),
                        shard, "grad_pool_w_in", bm=D_MODEL, bn=n0 // 4)
    pin_started, token = reduce_start([gw_pin], "pool_in")
    dx, d_g0 = _matmul_nt_rms_bwd(dproj0, w_pin, xt, g0 + token, dh1, "pool_in_bwd", tk=w_pin.shape[2])

    small = _pack_small(jnp.concatenate([d_g0, d_g1], axis=0), d_scale, d_norm_f,
                        jnp.broadcast_to(loss_row[:, :1], (1, D_MODEL)))
    small_all, = _sibling_exchange([], small, "small_sums_exchange")
    grads = _join_halves(reduce_finish(pin_started, dx, "pool_in")
                         + reduce_finish(mix_started, dx, "pool_mix")
                         + reduce_finish(sin_started, dx, "sb_in")
                         + reduce_finish(sout_started, dx, "sb_out"), "grad_join_halves")

    deltas, new_m, new_v = [], [], []
    for w, g, m, v, nm in zip(w_shards, grads, m_shards, v_shards, names):
        d, mm, vv = _adamw(w, g, m, v, "adamw_" + nm)
        deltas.append(d)
        new_m.append(mm)
        new_v.append(vv)

    zero_row = jnp.zeros((1, D_MODEL), F32)
    g_small, d_small, m_small, v_small = _adamw_small(
        small_all, _pack_small(norm_g, pool_scale, norm_f, zero_row),
        _pack_small(m_norm_g, m_pool_scale, m_norm_f, zero_row),
        _pack_small(v_norm_g, v_pool_scale, v_norm_f, zero_row + 1.0), "adamw_small")
    loss = g_small[5, 0]

    def unpack_small(a):
        return a[0:2], a[2:4].reshape(1, D_INNER), a[4]

    def assemble(big, small3):
        ng, ps, nf = small3
        return [ng, big[0].reshape(pool_w_in.shape), big[1].reshape(pool_w.shape), ps,
                big[2].reshape(pool_w_out.shape), big[3].reshape(sb_w_in.shape),
                big[4].reshape(sb_w_out.shape), nf]

    return (loss, dx.reshape(x.shape),
            *assemble(grads, unpack_small(g_small)),
            *assemble(deltas, unpack_small(d_small)),
            *assemble(new_m, unpack_small(m_small)),
            *assemble(new_v, unpack_small(v_small)))
```

```python
import jax
import jax.numpy as jnp
from jax import lax
from jax.experimental import pallas as pl
from jax.experimental.pallas import tpu as pltpu

F32 = jnp.float32
BF16 = jnp.bfloat16
MESH = pl.DeviceIdType.MESH

D_MODEL = 1024
D_INNER = 2048
N_GROUPS = 4
GROUP_DIM = 512
HEAD_PAIR_QK = 128
HEAD_V = 128
QK_WIDTH = 1024
RMS_EPS = 1e-6
HALO = 16
N_CHIPS = 4
N_DEV = 8

ADAM_LR = 0.001
ADAM_B1 = 0.9
ADAM_B2 = 0.999
ADAM_EPS = 1e-08
ADAM_WD = 0.01
ADAM_STEP = 10

VMEM_LIMIT = 56 * 1024 * 1024

HBM_SPEC = pl.BlockSpec(memory_space=pltpu.HBM)


def _params(n_axes):
    return pltpu.CompilerParams(dimension_semantics=("arbitrary",) * n_axes,
                                vmem_limit_bytes=VMEM_LIMIT)


def _dot(a, b):
    return jnp.dot(a, b, preferred_element_type=F32)


def _dot_nt(a, b):
    return lax.dot_general(a, b, (((1,), (1,)), ((), ())), preferred_element_type=F32)


def _dot_tn(a, b):
    return lax.dot_general(a, b, (((0,), (0,)), ((), ())), preferred_element_type=F32)


def _sigmoid(z):
    return 1.0 / (1.0 + jnp.exp(-z))


def _row_blocks(tm, rows=256):
    return [slice(r, r + rows) for r in range(0, tm, rows)]


def _rms_matmul(h, g_row, w4, outs, name, tm=1024, tn=512):
    T = h.shape[0]
    per_shard = w4.shape[2] // tn
    starts = [0]
    for width, _ in outs:
        starts.append(starts[-1] + width // tn)
    n_out = len(outs)

    def body(h_ref, g_ref, w_ref, *rest):
        o_refs, u_out, u_s = rest[:n_out], rest[n_out], rest[n_out + 1]
        n = pl.program_id(1)

        @pl.when(n == 0)
        def _():
            x = h_ref[...]
            inv = lax.rsqrt(jnp.mean(x * x, axis=-1, keepdims=True) + RMS_EPS)
            u = (x * inv * g_ref[...]).astype(BF16)
            u_s[...] = u
            u_out[...] = u

        res = _dot(u_s[...], w_ref[0])
        for k in range(n_out):
            @pl.when((n >= starts[k]) & (n < starts[k + 1]))
            def _():
                o_refs[k][...] = res.astype(o_refs[k].dtype)

    def out_map(k):
        return lambda m, n: (m, jnp.clip(n - starts[k], 0, starts[k + 1] - starts[k] - 1))

    return pl.pallas_call(
        body, name=name, grid=(T // tm, starts[-1]),
        in_specs=[pl.BlockSpec((tm, D_MODEL), lambda m, n: (m, 0)),
                  pl.BlockSpec((1, D_MODEL), lambda m, n: (0, 0)),
                  pl.BlockSpec((1, D_MODEL, tn), lambda m, n: (n // per_shard, 0, n % per_shard))],
        out_specs=[pl.BlockSpec((tm, tn), out_map(k)) for k in range(n_out)]
        + [pl.BlockSpec((tm, D_MODEL), lambda m, n: (m, 0))],
        out_shape=[jax.ShapeDtypeStruct((T, width), dt) for width, dt in outs]
        + [jax.ShapeDtypeStruct((T, D_MODEL), BF16)],
        scratch_shapes=[pltpu.VMEM((tm, D_MODEL), BF16)],
        compiler_params=_params(2),
    )(h, g_row, w4)


def _matmul_residual(a, w, res, name, tm=1024, tn=512):
    T, K = a.shape
    N = w.shape[1]

    def body(a_ref, w_ref, r_ref, o_ref):
        o_ref[...] = r_ref[...] + _dot(a_ref[...], w_ref[...])

    return pl.pallas_call(
        body, name=name, grid=(T // tm, N // tn),
        in_specs=[pl.BlockSpec((tm, K), lambda m, n: (m, 0)),
                  pl.BlockSpec((K, tn), lambda m, n: (0, n)),
                  pl.BlockSpec((tm, tn), lambda m, n: (m, n))],
        out_specs=pl.BlockSpec((tm, tn), lambda m, n: (m, n)),
        out_shape=jax.ShapeDtypeStruct((T, N), F32),
        compiler_params=_params(2),
    )(a, w, res)


def _matmul_tn(a, b, a_cols, b_cols, out_shape, out_block, out_map, name, bm, bn, tk=512,
               diagonal_blocks=False):
    T = a.shape[0]

    def body(a_ref, b_ref, o_ref):
        @pl.when(pl.program_id(2) == 0)
        def _():
            o_ref[...] = jnp.zeros_like(o_ref)

        part = _dot_tn(a_ref[...].astype(BF16), b_ref[...].astype(BF16))
        o_ref[...] += part.reshape(o_ref.shape)

    b_map = (lambda i, j, t: (t, i)) if diagonal_blocks else (lambda i, j, t: (t, j))
    return pl.pallas_call(
        body, name=name, grid=(a_cols // bm, 1 if diagonal_blocks else b_cols // bn, T // tk),
        in_specs=[pl.BlockSpec((tk, bm), lambda i, j, t: (t, i)),
                  pl.BlockSpec((tk, bn), b_map)],
        out_specs=pl.BlockSpec(out_block, out_map),
        out_shape=jax.ShapeDtypeStruct(out_shape, F32),
        compiler_params=_params(3),
    )(a, b)


def _matmul_nt_rms_bwd(dproj, w4, h, g_row, dres, name, tm=1024, tk=512):
    T, cols = dproj.shape
    per_shard = w4.shape[2] // tk
    nk = cols // tk

    def body(dp_ref, w_ref, h_ref, g_ref, r_ref, dx_ref, dg_ref, acc):
        m, k = pl.program_id(0), pl.program_id(1)

        @pl.when(k == 0)
        def _():
            acc[...] = jnp.zeros_like(acc)

        @pl.when((k == 0) & (m == 0))
        def _():
            dg_ref[...] = jnp.zeros_like(dg_ref)

        acc[...] += _dot_nt(dp_ref[...], w_ref[0])

        @pl.when(k == nk - 1)
        def _():
            du = acc[...]
            x = h_ref[...]
            inv = lax.rsqrt(jnp.mean(x * x, axis=-1, keepdims=True) + RMS_EPS)
            xhat = x * inv
            dg_ref[...] += jnp.sum(du * xhat, axis=0, keepdims=True)
            dxh = du * g_ref[...]
            proj = jnp.mean(dxh * xhat, axis=-1, keepdims=True)
            dx_ref[...] = r_ref[...] + inv * (dxh - xhat * proj)

    return pl.pallas_call(
        body, name=name, grid=(T // tm, nk),
        in_specs=[pl.BlockSpec((tm, tk), lambda m, k: (m, k)),
                  pl.BlockSpec((1, D_MODEL, tk), lambda m, k: (k // per_shard, 0, k % per_shard)),
                  pl.BlockSpec((tm, D_MODEL), lambda m, k: (m, 0)),
                  pl.BlockSpec((1, D_MODEL), lambda m, k: (0, 0)),
                  pl.BlockSpec((tm, D_MODEL), lambda m, k: (m, 0))],
        out_specs=[pl.BlockSpec((tm, D_MODEL), lambda m, k: (m, 0)),
                   pl.BlockSpec((1, D_MODEL), lambda m, k: (0, 0))],
        out_shape=[jax.ShapeDtypeStruct((T, D_MODEL), F32),
                   jax.ShapeDtypeStruct((1, D_MODEL), F32)],
        scratch_shapes=[pltpu.VMEM((tm, D_MODEL), F32)],
        compiler_params=_params(2),
    )(dproj, w4, h, g_row, dres)


def _window_of(g):
    return jnp.left_shift(2, g)


def _select_stage(g, stages):
    res = stages[0]
    for i in range(1, len(stages)):
        res = jnp.where(g >= i, stages[i], res)
    return res


def _pool_fwd(proj0, wg4, scale_row, S, name, tm=1024):
    T = proj0.shape[0]
    tm = min(tm, S)
    blocks_per_seq = S // tm
    hb = tm // HALO

    def body(x_ref, halo_ref, z_ref, w_ref, s_ref, y_ref, p_ref, mix_ref):
        m, g = pl.program_id(0), pl.program_id(1)
        first = (m % blocks_per_seq) == 0
        halo = jnp.where(first, 0.0, halo_ref[...].astype(F32))
        x = x_ref[...].astype(F32)
        ext = jnp.concatenate([halo, x], axis=0)
        stages = []
        cur = ext
        for sh in (1, 2, 4, 8):
            cur = cur + pltpu.roll(cur, sh, 0)
            stages.append(cur[HALO:, :])
        win_sum = _select_stage(g, stages)
        pos = (m % blocks_per_seq) * tm + lax.broadcasted_iota(jnp.int32, (tm, 1), 0)
        count = jnp.minimum(pos + 1, _window_of(g)).astype(F32)
        p_ref[...] = (win_sum / count - x).astype(BF16)
        w = w_ref[...].reshape(GROUP_DIM, GROUP_DIM)
        for rows in _row_blocks(tm):
            mixed = _dot(p_ref[rows, :], w)
            z = z_ref[rows, :].astype(F32)
            y_ref[rows, :] = (mixed * s_ref[...] * (z * _sigmoid(z))).astype(BF16)
            mix_ref[rows, :] = mixed.astype(BF16)

    blk = lambda m, g: (m, g)
    return pl.pallas_call(
        body, name=name, grid=(T // tm, N_GROUPS),
        in_specs=[pl.BlockSpec((tm, GROUP_DIM), blk),
                  pl.BlockSpec((HALO, GROUP_DIM), lambda m, g: (jnp.maximum(m * hb - 1, 0), g)),
                  pl.BlockSpec((tm, GROUP_DIM), lambda m, g: (m, N_GROUPS + g)),
                  pl.BlockSpec((N_CHIPS, GROUP_DIM // N_CHIPS, GROUP_DIM), lambda m, g: (0, g, 0)),
                  pl.BlockSpec((1, GROUP_DIM), lambda m, g: (0, g))],
        out_specs=[pl.BlockSpec((tm, GROUP_DIM), blk)] * 3,
        out_shape=[jax.ShapeDtypeStruct((T, D_INNER), BF16)] * 3,
        compiler_params=_params(2),
    )(proj0, proj0, proj0, wg4, scale_row)


def _pool_gate_bwd(dh, w_out, proj0, mixed, scale_row, name, tm=1024, tn=512):
    T = dh.shape[0]
    gate_b0 = D_INNER // tn

    def body(dh_ref, w_ref, z_ref, mix_ref, s_ref, dm_ref, dz_ref, ds_ref):
        @pl.when(pl.program_id(1) == 0)
        def _():
            ds_ref[...] = jnp.zeros_like(ds_ref)

        s = s_ref[...]
        ds = ds_ref[...]
        for rows in _row_blocks(tm):
            dy = _dot_nt(dh_ref[rows, :].astype(BF16), w_ref[...])
            z = z_ref[rows, :].astype(F32)
            sig = _sigmoid(z)
            silu = z * sig
            mixed = mix_ref[rows, :].astype(F32)
            dm_ref[rows, :] = (dy * s * silu).astype(BF16)
            dz_ref[rows, :] = (dy * mixed * s * (sig * (1.0 + z * (1.0 - sig)))).astype(BF16)
            ds = ds + jnp.sum(dy * mixed * silu, axis=0, keepdims=True)
        ds_ref[...] = ds

    return pl.pallas_call(
        body, name=name, grid=(D_INNER // tn, T // tm),
        in_specs=[pl.BlockSpec((tm, D_MODEL), lambda n, m: (m, 0)),
                  pl.BlockSpec((tn, D_MODEL), lambda n, m: (n, 0)),
                  pl.BlockSpec((tm, tn), lambda n, m: (m, gate_b0 + n)),
                  pl.BlockSpec((tm, tn), lambda n, m: (m, n)),
                  pl.BlockSpec((1, tn), lambda n, m: (0, n))],
        out_specs=[pl.BlockSpec((tm, tn), lambda n, m: (m, n)),
                   pl.BlockSpec((tm, tn), lambda n, m: (m, gate_b0 + n)),
                   pl.BlockSpec((1, tn), lambda n, m: (0, n))],
        out_shape=[jax.ShapeDtypeStruct((T, D_INNER), BF16),
                   jax.ShapeDtypeStruct((T, 2 * D_INNER), BF16),
                   jax.ShapeDtypeStruct((1, D_INNER), F32)],
        compiler_params=_params(2),
    )(dh, w_out, proj0, mixed, scale_row)


def _pool_bwd(dmixed, wg4, dproj0, after, S, name, tm=1024):
    T = dmixed.shape[0]
    tm = min(tm, S)
    blocks_per_seq = S // tm
    hb = tm // HALO
    n_halo_blocks = T // HALO

    def body(dm_ref, halo_ref, w_ref, _, __, o_ref):
        m, g = pl.program_id(0), pl.program_id(1)
        ext = jnp.concatenate([dm_ref[...], halo_ref[...]], axis=0)
        dp = _dot_nt(ext, w_ref[...].reshape(GROUP_DIM, GROUP_DIM))
        pos = (m % blocks_per_seq) * tm + lax.broadcasted_iota(jnp.int32, (tm + HALO, 1), 0)
        count = jnp.minimum(pos + 1, _window_of(g)).astype(F32)
        c = jnp.where(pos < S, dp / count, 0.0)
        n = tm + HALO
        stages = []
        cur = c
        for sh in (1, 2, 4, 8):
            cur = cur + pltpu.roll(cur, n - sh, 0)
            stages.append(cur[:tm, :])
        o_ref[...] = (_select_stage(g, stages) - dp[:tm, :]).astype(BF16)

    blk = lambda m, g: (m, g)
    return pl.pallas_call(
        body, name=name, grid=(T // tm, N_GROUPS),
        in_specs=[pl.BlockSpec((tm, GROUP_DIM), blk),
                  pl.BlockSpec((HALO, GROUP_DIM),
                               lambda m, g: (jnp.minimum((m + 1) * hb, n_halo_blocks - 1), g)),
                  pl.BlockSpec((N_CHIPS, GROUP_DIM // N_CHIPS, GROUP_DIM), lambda m, g: (0, g, 0)),
                  HBM_SPEC, ANY_SPEC],
        out_specs=pl.BlockSpec((tm, GROUP_DIM), blk),
        out_shape=jax.ShapeDtypeStruct(dproj0.shape, dproj0.dtype),
        input_output_aliases={3: 0},
        compiler_params=_params(2),
    )(dmixed, dmixed, wg4, dproj0, after)


TQ = 256


def _split_dot(x, m):
    hi = x.astype(BF16)
    lo = (x - hi.astype(F32)).astype(BF16)
    return _dot(hi, m) + _dot(lo, m)


NEG_LOG2E = -1.4426950408889634


def _log_terms(z):
    soft = jnp.log(1.0 + jnp.exp2(jnp.abs(z) * NEG_LOG2E))
    log_beta = jnp.minimum(z, 0.0) - soft
    return log_beta, log_beta - z


N_HEADS = 16
FWD_HEADS = BWD_HEADS = 4


def _masked_heads(x, heads):
    lane = lax.broadcasted_iota(jnp.int32, (1, HEAD_PAIR_QK), 1)
    out = []
    for hh in range(heads):
        slab = x[:, (hh // 2) * HEAD_PAIR_QK:(hh // 2 + 1) * HEAD_PAIR_QK]
        out.append(jnp.where((lane // 64) == hh % 2, slab, jnp.zeros_like(slab)))
    return out


def _attn_fwd(qkvz, S, name):
    T = qkvz.shape[0]
    nq = S // TQ
    HEADS, QK_W, V_W = FWD_HEADS, FWD_HEADS * 64, FWD_HEADS * HEAD_V
    k_b0 = QK_WIDTH // QK_W
    v_b0 = 2 * QK_WIDTH // V_W
    z_b0 = (2 * QK_WIDTH + D_INNER) // V_W
    hs = range(HEADS)

    def body(q_ref, k_ref, v_ref, z_ref, o_ref, y_ref, lt_ref):
        row = lax.broadcasted_iota(jnp.int32, (TQ, TQ), 0)
        col = lax.broadcasted_iota(jnp.int32, (TQ, TQ), 1)
        causal = col < row
        later_in_block = (row > col).astype(BF16)
        lax.fori_loop(0, nq, lambda qi, _: q_block(qi, causal, later_in_block,
                                                   q_ref, k_ref, v_ref, z_ref, o_ref, y_ref, lt_ref), 0)

    def q_block(qi, causal, later_in_block, q_ref, k_ref, v_ref, z_ref, o_ref, y_ref, lt_ref):
        rows = pl.ds(pl.multiple_of(qi * TQ, TQ), TQ)
        qms = [qm * 0.125 for qm in _masked_heads(q_ref[rows, :], HEADS)]

        def step(j, carry, diagonal):
            koff = pl.multiple_of(j * TQ, TQ)
            kbs = [k_ref[pl.ds(koff, TQ), p * HEAD_PAIR_QK:(p + 1) * HEAD_PAIR_QK]
                   for p in range(HEADS // 2)]
            run, acc = [carry[2 * hh] for hh in hs], [carry[2 * hh + 1] for hh in hs]
            z = [_dot_nt(qms[hh], kbs[hh // 2]) for hh in hs]
            terms = [_log_terms(z[hh]) for hh in hs]
            log_om = [jnp.where(causal, t[1], 0.0) if diagonal else t[1] for t in terms]
            later = [_split_dot(log_om[hh], later_in_block) for hh in hs]
            a = [jnp.exp(terms[hh][0] + (run[hh] + later[hh])) for hh in hs]
            if diagonal:
                a = [jnp.where(causal, a[hh], 0.0) for hh in hs]
            out = []
            for hh in hs:
                vb = v_ref[pl.ds(koff, TQ), hh * HEAD_V:(hh + 1) * HEAD_V]
                out += [run[hh] + jnp.sum(log_om[hh], axis=1, keepdims=True),
                        acc[hh] + _dot(a[hh].astype(BF16), vb)]
            return tuple(out)

        zero = (jnp.zeros((TQ, 1), F32), jnp.zeros((TQ, HEAD_V), F32))
        carry = step(qi, zero * HEADS, True)
        carry = lax.fori_loop(0, qi, lambda i, c: step(qi - 1 - i, c, False), carry)
        for hh in hs:
            sl = slice(hh * HEAD_V, (hh + 1) * HEAD_V)
            acc = carry[2 * hh + 1]
            z = z_ref[rows, sl].astype(F32)
            o_ref[rows, sl] = acc.astype(BF16)
            y_ref[rows, sl] = (acc * (z * _sigmoid(z))).astype(BF16)
            lt_ref[rows, hh:hh + 1] = carry[2 * hh]
        return 0

    blk = lambda b, p: (b, p)
    return pl.pallas_call(
        body, name=name, grid=(T // S, N_HEADS // HEADS),
        in_specs=[pl.BlockSpec((S, QK_W), blk),
                  pl.BlockSpec((S, QK_W), lambda b, p: (b, k_b0 + p)),
                  pl.BlockSpec((S, V_W), lambda b, p: (b, v_b0 + p)),
                  pl.BlockSpec((S, V_W), lambda b, p: (b, z_b0 + p))],
        out_specs=[pl.BlockSpec((S, V_W), blk),
                   pl.BlockSpec((S, V_W), blk),
                   pl.BlockSpec((None, S, HEADS), lambda b, p: (p, b, 0))],
        out_shape=[jax.ShapeDtypeStruct((T, D_INNER), BF16),
                   jax.ShapeDtypeStruct((T, D_INNER), BF16),
                   jax.ShapeDtypeStruct((N_HEADS // HEADS, T, HEADS), F32)],
        compiler_params=_params(2),
    )(qkvz, qkvz, qkvz, qkvz)


def _attn_gate_bwd(dh, w_out, qkvz, o, after, name, tm=1024, tn=512):
    T = dh.shape[0]
    gate_b0 = (2 * QK_WIDTH + D_INNER) // tn

    def body(dh_ref, w_ref, z_ref, o_ref, _, do_ref, dz_ref, dh_s):
        @pl.when(pl.program_id(1) == 0)
        def _():
            dh_s[...] = dh_ref[...].astype(BF16)

        for rows in _row_blocks(tm):
            dy = _dot_nt(dh_s[rows, :], w_ref[...])
            z = z_ref[rows, :].astype(F32)
            sig = _sigmoid(z)
            do_ref[rows, :] = (dy * (z * sig)).astype(BF16)
            dz_ref[rows, :] = (dy * o_ref[rows, :].astype(F32)
                               * (sig * (1.0 + z * (1.0 - sig)))).astype(BF16)

    return pl.pallas_call(
        body, name=name, grid=(T // tm, D_INNER // tn),
        in_specs=[pl.BlockSpec((tm, D_MODEL), lambda m, n: (m, 0)),
                  pl.BlockSpec((tn, D_MODEL), lambda m, n: (n, 0)),
                  pl.BlockSpec((tm, tn), lambda m, n: (m, gate_b0 + n)),
                  pl.BlockSpec((tm, tn), lambda m, n: (m, n)),
                  ANY_SPEC],
        out_specs=[pl.BlockSpec((tm, tn), lambda m, n: (m, n)),
                   pl.BlockSpec((tm, tn), lambda m, n: (m, gate_b0 + n))],
        out_shape=[jax.ShapeDtypeStruct((T, D_INNER), BF16),
                   jax.ShapeDtypeStruct((T, 2 * QK_WIDTH + 2 * D_INNER), BF16)],
        scratch_shapes=[pltpu.VMEM((tm, D_MODEL), BF16)],
        compiler_params=_params(2),
    )(dh, w_out, qkvz, o, after)


def _attn_bwd(qkv, do, ltot, dproj1, S, name):
    T = qkv.shape[0]
    nq = S // TQ
    HEADS, QK_W, V_W = BWD_HEADS, BWD_HEADS * 64, BWD_HEADS * HEAD_V
    k_b0 = QK_WIDTH // QK_W
    v_b0 = 2 * QK_WIDTH // V_W
    hs = range(HEADS)
    pairs = range(HEADS // 2)

    def body(q_ref, k_ref, v_ref, do_ref, lt_ref, _, out_ref, dq_s, dk_s, dv_s, dkb_s, dvb_s, sems):
        b, p = pl.program_id(0), pl.program_id(1)
        row = lax.broadcasted_iota(jnp.int32, (TQ, TQ), 0)
        col = lax.broadcasted_iota(jnp.int32, (TQ, TQ), 1)
        causal = col < row
        upto = (row <= col).astype(BF16)
        before = (row < col).astype(BF16)
        dk_s[...] = jnp.zeros_like(dk_s)
        dv_s[...] = jnp.zeros_like(dv_s)

        def q_block(qi, _):
            qoff = pl.multiple_of(qi * TQ, TQ)
            qms = [qm * 0.125 for qm in _masked_heads(q_ref[pl.ds(qoff, TQ), :], HEADS)]
            vsl = [slice(hh * HEAD_V, (hh + 1) * HEAD_V) for hh in hs]
            psl = [slice(pp * HEAD_PAIR_QK, (pp + 1) * HEAD_PAIR_QK) for pp in pairs]
            do_h = [do_ref[pl.ds(qoff, TQ), sl] for sl in vsl]
            total = [lt_ref[pl.ds(qoff, TQ), hh:hh + 1] for hh in hs]

            def k_block(j, carry, diagonal):
                koff = pl.multiple_of(j * TQ, TQ)
                kms = _masked_heads(k_ref[pl.ds(koff, TQ), :], HEADS)
                g_before = [carry[2 * hh] for hh in hs]
                lom_before = [carry[2 * hh + 1] for hh in hs]
                z = [_dot_nt(qms[hh], kms[hh]) for hh in hs]
                da = [_dot_nt(do_h[hh], v_ref[pl.ds(koff, TQ), vsl[hh]]) for hh in hs]
                terms = [_log_terms(z[hh]) for hh in hs]
                log_om = [jnp.where(causal, t[1], 0.0) if diagonal else t[1] for t in terms]
                prefix = [_split_dot(log_om[hh], upto) for hh in hs]
                a = [jnp.exp(terms[hh][0] + ((total[hh] - lom_before[hh]) - prefix[hh])) for hh in hs]
                if diagonal:
                    a = [jnp.where(causal, a[hh], 0.0) for hh in hs]
                g = [a[hh] * da[hh] for hh in hs]
                g_prefix = [_dot(g[hh].astype(BF16), before) for hh in hs]
                out, dzs = [], []
                for hh in hs:
                    beta = jnp.exp(terms[hh][0])
                    g_excl = (g_before[hh] + g_prefix[hh]) * beta
                    if diagonal:
                        g_excl = jnp.where(causal, g_excl, 0.0)
                    dzs.append((g[hh] * (1.0 - beta) - g_excl).astype(BF16))
                    out += [g_before[hh] + jnp.sum(g[hh], axis=1, keepdims=True),
                            lom_before[hh] + jnp.sum(log_om[hh], axis=1, keepdims=True)]
                for hh in hs:
                    dv_s[pl.ds(koff, TQ), vsl[hh]] += _dot_tn(a[hh].astype(BF16), do_h[hh])
                dq = []
                for pp in pairs:
                    pair = slice(2 * pp, 2 * pp + 2)
                    dq.append(carry[2 * HEADS + pp] + _dot(jnp.concatenate(dzs[pair], axis=1),
                                                           jnp.concatenate(kms[pair], axis=0)))
                    dk_s[pl.ds(koff, TQ), psl[pp]] += _dot_tn(jnp.concatenate(dzs[pair], axis=0),
                                                              jnp.concatenate(qms[pair], axis=0))
                return tuple(out) + tuple(dq)

            zero = jnp.zeros((TQ, 1), F32)
            carry = (zero,) * (2 * HEADS) + (jnp.zeros((TQ, HEAD_PAIR_QK), F32),) * (HEADS // 2)
            carry = lax.fori_loop(0, qi, lambda j, c: k_block(j, c, False), carry)
            carry = k_block(qi, carry, True)
            for pp in pairs:
                dq_s[pl.ds(qoff, TQ), psl[pp]] = (carry[2 * HEADS + pp] * 0.125).astype(BF16)
            return 0

        lax.fori_loop(0, nq, q_block, 0)
        dkb_s[...] = dk_s[...].astype(BF16)
        dvb_s[...] = dv_s[...].astype(BF16)
        rows = pl.ds(pl.multiple_of(b * S, TQ), S)
        copies = [
            pltpu.make_async_copy(
                dq_s, out_ref.at[rows, pl.ds(pl.multiple_of(p * QK_W, 128), QK_W)], sems.at[0]),
            pltpu.make_async_copy(
                dkb_s, out_ref.at[rows, pl.ds(pl.multiple_of(QK_WIDTH + p * QK_W, 128), QK_W)],
                sems.at[1]),
            pltpu.make_async_copy(
                dvb_s, out_ref.at[rows, pl.ds(pl.multiple_of(2 * QK_WIDTH + p * V_W, 128), V_W)],
                sems.at[2]),
        ]
        for cp in copies:
            cp.start()
        for cp in copies:
            cp.wait()

    return pl.pallas_call(
        body, name=name, grid=(T // S, N_HEADS // HEADS),
        in_specs=[pl.BlockSpec((S, QK_W), lambda b, p: (b, p)),
                  pl.BlockSpec((S, QK_W), lambda b, p: (b, k_b0 + p)),
                  pl.BlockSpec((S, V_W), lambda b, p: (b, v_b0 + p)),
                  pl.BlockSpec((S, V_W), lambda b, p: (b, p)),
                  pl.BlockSpec((None, S, HEADS), lambda b, p: (p, b, 0)),
                  HBM_SPEC],
        out_specs=HBM_SPEC,
        out_shape=jax.ShapeDtypeStruct(dproj1.shape, dproj1.dtype),
        input_output_aliases={5: 0},
        scratch_shapes=[pltpu.VMEM((S, QK_W), BF16),
                        pltpu.VMEM((S, QK_W), F32),
                        pltpu.VMEM((S, V_W), F32),
                        pltpu.VMEM((S, QK_W), BF16),
                        pltpu.VMEM((S, V_W), BF16),
                        pltpu.SemaphoreType.DMA((3,))],
        compiler_params=_params(2),
    )(qkv, qkv, qkv, do, ltot, dproj1)


def _out_proj_loss_head(a, w, res, g_row, target, name, tm=512):
    T, K = a.shape

    def body(a_ref, w_ref, r_ref, g_ref, t_ref, dh_ref, dg_ref, loss_ref):
        @pl.when(pl.program_id(0) == 0)
        def _():
            dg_ref[...] = jnp.zeros_like(dg_ref)
            loss_ref[...] = jnp.zeros_like(loss_ref)

        gain = g_ref[...]
        dg, loss = dg_ref[...], loss_ref[...]
        for rows in _row_blocks(tm):
            x = r_ref[rows, :] + _dot(a_ref[rows, :], w_ref[...])
            inv = lax.rsqrt(jnp.mean(x * x, axis=-1, keepdims=True) + RMS_EPS)
            xhat = x * inv
            err = xhat * gain - t_ref[rows, :]
            per_token = jnp.mean(err * err, axis=-1, keepdims=True)
            loss = loss + 0.5 * jnp.sum(per_token, axis=0, keepdims=True)
            dy = err * (1.0 / D_MODEL)
            dg = dg + jnp.sum(dy * xhat, axis=0, keepdims=True)
            dxh = dy * gain
            proj = jnp.mean(dxh * xhat, axis=-1, keepdims=True)
            dh_ref[rows, :] = inv * (dxh - xhat * proj)
        dg_ref[...] = dg
        loss_ref[...] = loss

    return pl.pallas_call(
        body, name=name, grid=(T // tm,),
        in_specs=[pl.BlockSpec((tm, K), lambda m: (m, 0)),
                  pl.BlockSpec((K, D_MODEL), lambda m: (0, 0)),
                  pl.BlockSpec((tm, D_MODEL), lambda m: (m, 0)),
                  pl.BlockSpec((1, D_MODEL), lambda m: (0, 0)),
                  pl.BlockSpec((tm, D_MODEL), lambda m: (m, 0))],
        out_specs=[pl.BlockSpec((tm, D_MODEL), lambda m: (m, 0)),
                   pl.BlockSpec((1, D_MODEL), lambda m: (0, 0)),
                   pl.BlockSpec((1, 128), lambda m: (0, 0))],
        out_shape=[jax.ShapeDtypeStruct((T, D_MODEL), F32),
                   jax.ShapeDtypeStruct((1, D_MODEL), F32),
                   jax.ShapeDtypeStruct((1, 128), F32)],
        compiler_params=_params(1),
    )(a, w, res, g_row, target)


def _place():
    return lax.axis_index("x"), lax.axis_index("y"), lax.axis_index("c")


def _other_chips(x, y):
    return [(1 - x, y), (x, 1 - y), (1 - x, 1 - y)]


def _half(ref, c):
    hr = ref.shape[-2] // 2
    return pl.ds(pl.multiple_of(c * hr, 8), hr)


def _cast_to_slot(shard, chip, name, tr=256):
    R, C = shard.shape

    def body(chip_ref, w_ref, o_ref):
        o_ref[0] = w_ref[...].astype(BF16)

    return pl.pallas_call(
        body, name=name,
        grid_spec=pltpu.PrefetchScalarGridSpec(
            num_scalar_prefetch=1, grid=(R // tr,),
            in_specs=[pl.BlockSpec((tr, C), lambda i, chip_ref: (i, 0))],
            out_specs=pl.BlockSpec((1, tr, C), lambda i, chip_ref: (chip_ref[0], i, 0))),
        out_shape=jax.ShapeDtypeStruct((N_CHIPS, R, C), BF16),
        compiler_params=_params(1),
    )(chip, shard)


def _weight_plan(bufs):
    x, y, c = _place()
    plan = []
    for buf in bufs:
        mine = buf.at[2 * x + y, _half(buf, c)]
        for ox, oy in _other_chips(x, y):
            plan.append((mine, mine, (ox, oy, c), buf.at[2 * ox + oy, _half(buf, c)]))
    return plan


def _chip_sum_plan(bufs):
    x, y, c = _place()
    n = len(bufs) // 2
    plan = []
    for sums, land in zip(bufs[:n], bufs[n:]):
        for k, (ox, oy) in enumerate(_other_chips(x, y)):
            plan.append((sums.at[2 * ox + oy], land.at[k], (ox, oy, c), land.at[k]))
    return plan


SEM_SPEC = pl.BlockSpec(memory_space=pltpu.SEMAPHORE)
ANY_SPEC = pl.BlockSpec(memory_space=pl.ANY)
DATAFLOW = pltpu.SideEffectType.DATAFLOW_SIDE_EFFECTING


def _in_hbm(a):
    return pltpu.with_memory_space_constraint(a, pltpu.HBM)


def _exchange_start(bufs, after, plan, n_copies, name):
    nb = len(bufs)

    def body(*refs):
        send_sems, recv_sems = refs[nb + 1], refs[nb + 2]
        for i, (src, dst, dev, _) in enumerate(plan(refs[:nb])):
            pltpu.make_async_remote_copy(
                src_ref=src, dst_ref=dst, send_sem=send_sems.at[i], recv_sem=recv_sems.at[i],
                device_id=dev, device_id_type=MESH).start()
        token = refs[-1]
        token[...] = jnp.zeros_like(token)

    res = pl.pallas_call(
        body, name=name,
        in_specs=[HBM_SPEC] * nb + [ANY_SPEC],
        out_specs=[SEM_SPEC, SEM_SPEC] + [HBM_SPEC] * nb + [pl.BlockSpec(memory_space=pltpu.VMEM)],
        out_shape=[pltpu.SemaphoreType.DMA((n_copies,)), pltpu.SemaphoreType.DMA((n_copies,))]
        + [pltpu.HBM(b.shape, b.dtype) for b in bufs] + [jax.ShapeDtypeStruct((8, 128), F32)],
        input_output_aliases={i: 2 + i for i in range(nb)},
        compiler_params=pltpu.CompilerParams(has_side_effects=DATAFLOW),
    )(*[_in_hbm(b) for b in bufs], after)
    return res[0], res[1], list(res[2:2 + nb]), res[-1]


def _exchange_wait(bufs, send_sems, recv_sems, after, plan, name):
    nb = len(bufs)

    def body(*refs):
        sends, recvs = refs[nb], refs[nb + 1]
        for i, (src, dst, dev, landing) in enumerate(plan(refs[:nb])):
            pltpu.make_async_remote_copy(
                src_ref=src, dst_ref=landing, send_sem=sends.at[i], recv_sem=recvs.at[i],
                device_id=dev, device_id_type=MESH).wait()

    res = pl.pallas_call(
        body, name=name,
        in_specs=[HBM_SPEC] * nb + [SEM_SPEC, SEM_SPEC, ANY_SPEC],
        out_specs=[HBM_SPEC] * nb,
        out_shape=[pltpu.HBM(b.shape, b.dtype) for b in bufs],
        input_output_aliases={i: i for i in range(nb)},
        compiler_params=pltpu.CompilerParams(has_side_effects=DATAFLOW),
    )(*bufs, send_sems, recv_sems, after)
    return list(res)


def _allgather_weights(slots, name, landed=False):
    n = len(slots)

    def body(*refs):
        outs = refs[n:2 * n]
        send_sems, recv_sems, fwd_send, fwd_recv = refs[2 * n:]
        x, y, c = _place()
        chips = _other_chips(x, y)

        def landing(a, chip, half_of):
            return outs[a].at[2 * chip[0] + chip[1], _half(outs[a], half_of)]

        def ici(a, k, chip_from, to):
            return pltpu.make_async_remote_copy(
                src_ref=landing(a, chip_from, c), dst_ref=landing(a, chip_from, c),
                send_sem=send_sems.at[a, k], recv_sem=recv_sems.at[a, k],
                device_id=to, device_id_type=MESH)

        def d2d(a, k, chip_from, half_of):
            return pltpu.make_async_remote_copy(
                src_ref=landing(a, chip_from, half_of), dst_ref=landing(a, chip_from, half_of),
                send_sem=fwd_send.at[a, k], recv_sem=fwd_recv.at[a, k],
                device_id=(x, y, 1 - c), device_id_type=MESH)

        sends = []
        if not landed:
            sends = [ici(a, k, (x, y), (*chips[k], c)) for a in range(n) for k in range(3)]
        for cp in sends:
            cp.start()
        forwards = []
        for a in range(n):
            for k in range(3):
                if not landed:
                    ici(a, k, chips[k], (x, y, c)).wait_recv()
                fw = d2d(a, k, chips[k], c)
                fw.start()
                forwards.append(fw)
        for a in range(n):
            for k in range(3):
                d2d(a, k, chips[k], 1 - c).wait_recv()
        for cp in sends + forwards:
            cp.wait_send()

    return pl.pallas_call(
        body, name=name,
        in_specs=[HBM_SPEC] * n, out_specs=[HBM_SPEC] * n,
        out_shape=[jax.ShapeDtypeStruct(s.shape, s.dtype) for s in slots],
        input_output_aliases={a: a for a in range(n)},
        scratch_shapes=[pltpu.SemaphoreType.DMA((n, 3)), pltpu.SemaphoreType.DMA((n, 3)),
                        pltpu.SemaphoreType.DMA((n, 3)), pltpu.SemaphoreType.DMA((n, 3))],
    )(*slots)


def _sibling_exchange(partials, small, name):
    n = len(partials)
    ns = 0 if small is None else 1

    def body(*refs):
        ins, outs = refs[:n], refs[n + ns:2 * n + ns]
        send_sems, recv_sems = refs[2 * (n + ns):2 * (n + ns) + 2]
        x, y, c = _place()
        me = 4 * x + 2 * y + c
        sends = [pltpu.make_async_remote_copy(
            src_ref=ins[a].at[:, _half(ins[a], 1 - c)], dst_ref=outs[a],
            send_sem=send_sems.at[a], recv_sem=recv_sems.at[a],
            device_id=(x, y, 1 - c), device_id_type=MESH) for a in range(n)]
        if ns:
            small_ref, small_all = refs[n], refs[2 * n + 1]
            s_send, s_recv, loc_sem = refs[2 * (n + ns) + 2:]
            local = pltpu.make_async_copy(small_ref, small_all.at[me], loc_sem)
            local.start()
            for d in range(1, N_DEV):
                px, py, pc = x ^ ((d >> 2) & 1), y ^ ((d >> 1) & 1), c ^ (d & 1)
                sends.append(pltpu.make_async_remote_copy(
                    src_ref=small_ref, dst_ref=small_all.at[me],
                    send_sem=s_send.at[d - 1], recv_sem=s_recv.at[d - 1],
                    device_id=(px, py, pc), device_id_type=MESH))
        for cp in sends:
            cp.start()
        if ns:
            for d in range(1, N_DEV):
                pltpu.make_async_remote_copy(
                    src_ref=small_ref, dst_ref=small_all.at[me ^ d],
                    send_sem=s_send.at[d - 1], recv_sem=s_recv.at[d - 1],
                    device_id=(x, y, c), device_id_type=MESH).wait_recv()
        for cp in sends[:n]:
            cp.wait_recv()
        for cp in sends:
            cp.wait_send()
        if ns:
            local.wait()

    out_shape = [jax.ShapeDtypeStruct((N_CHIPS, p.shape[1] // 2, p.shape[2]), F32) for p in partials]
    scratch = [pltpu.SemaphoreType.DMA((max(n, 1),)), pltpu.SemaphoreType.DMA((max(n, 1),))]
    if ns:
        out_shape.append(jax.ShapeDtypeStruct((N_DEV,) + small.shape, F32))
        scratch += [pltpu.SemaphoreType.DMA((N_DEV - 1,)), pltpu.SemaphoreType.DMA((N_DEV - 1,)),
                    pltpu.SemaphoreType.DMA]
    return pl.pallas_call(
        body, name=name,
        in_specs=[HBM_SPEC] * (n + ns), out_specs=[HBM_SPEC] * (n + ns),
        out_shape=out_shape, scratch_shapes=scratch,
    )(*partials, *([small] if ns else []))


def _chip_sum(partial, from_sibling, c, name, tr=256):
    _, hr, C = from_sibling.shape
    nb = hr // tr

    def body(c_ref, p_ref, s_ref, o_ref):
        o_ref[...] = (p_ref[...] + s_ref[...]).astype(BF16)

    return pl.pallas_call(
        body, name=name,
        grid_spec=pltpu.PrefetchScalarGridSpec(
            num_scalar_prefetch=1, grid=(N_CHIPS, nb),
            in_specs=[pl.BlockSpec((1, tr, C), lambda j, i, c_ref: (j, c_ref[0] * nb + i, 0)),
                      pl.BlockSpec((1, tr, C), lambda j, i, c_ref: (j, i, 0))],
            out_specs=pl.BlockSpec((1, tr, C), lambda j, i, c_ref: (j, i, 0))),
        out_shape=jax.ShapeDtypeStruct(from_sibling.shape, BF16),
        compiler_params=_params(2),
    )(c, partial, from_sibling)


def _reduce_half(partial, from_sibling, received, place, name, tr=256):
    _, hr, C = from_sibling.shape
    nb = hr // tr

    def body(p_ref, mine_ref, sib_ref, r_ref, o_ref):
        acc = mine_ref[0] + sib_ref[0]
        for k in range(3):
            acc = acc + r_ref[k].astype(F32)
        o_ref[...] = acc

    return pl.pallas_call(
        body, name=name,
        grid_spec=pltpu.PrefetchScalarGridSpec(
            num_scalar_prefetch=1, grid=(nb,),
            in_specs=[pl.BlockSpec((1, tr, C), lambda i, p: (p[0], p[1] * nb + i, 0)),
                      pl.BlockSpec((1, tr, C), lambda i, p: (p[0], i, 0)),
                      pl.BlockSpec((3, tr, C), lambda i, p: (0, i, 0))],
            out_specs=pl.BlockSpec((tr, C), lambda i, p: (p[1] * nb + i, 0))),
        out_shape=jax.ShapeDtypeStruct((2 * hr, C), F32),
        compiler_params=_params(1),
    )(place, partial, from_sibling, received)


def _join_halves(fulls, name):
    n = len(fulls)

    def body(*refs):
        outs = refs[n:2 * n]
        send_sems, recv_sems = refs[2 * n:]
        x, y, c = _place()

        def copy(a, half_of, to):
            rows = outs[a].at[_half(outs[a], half_of)]
            return pltpu.make_async_remote_copy(
                src_ref=rows, dst_ref=rows, send_sem=send_sems.at[a], recv_sem=recv_sems.at[a],
                device_id=to, device_id_type=MESH)

        sends = [copy(a, c, (x, y, 1 - c)) for a in range(n)]
        for cp in sends:
            cp.start()
        for a in range(n):
            copy(a, 1 - c, (x, y, c)).wait_recv()
        for cp in sends:
            cp.wait_send()

    return pl.pallas_call(
        body, name=name,
        in_specs=[HBM_SPEC] * n, out_specs=[HBM_SPEC] * n,
        out_shape=[jax.ShapeDtypeStruct(f.shape, F32) for f in fulls],
        input_output_aliases={a: a for a in range(n)},
        scratch_shapes=[pltpu.SemaphoreType.DMA((n,)), pltpu.SemaphoreType.DMA((n,))],
    )(*fulls)


def _adamw_math(w, g, m, v):
    m = ADAM_B1 * m + (1.0 - ADAM_B1) * g
    v = ADAM_B2 * v + (1.0 - ADAM_B2) * (g * g)
    m_hat = m / (1.0 - ADAM_B1 ** ADAM_STEP)
    v_hat = v / (1.0 - ADAM_B2 ** ADAM_STEP)
    delta = -ADAM_LR * (m_hat / (jnp.sqrt(v_hat) + ADAM_EPS) + ADAM_WD * w)
    return delta, m, v


def _adamw(w, g, m, v, name, tr=256):
    R, C = w.shape
    tr = min(tr, R)

    def body(w_ref, g_ref, m_ref, v_ref, d_out, m_out, v_out):
        d_out[...], m_out[...], v_out[...] = _adamw_math(w_ref[...], g_ref[...], m_ref[...], v_ref[...])

    spec = pl.BlockSpec((tr, C), lambda i: (i, 0))
    return pl.pallas_call(
        body, name=name, grid=(R // tr,),
        in_specs=[spec] * 4, out_specs=[spec] * 3,
        out_shape=[jax.ShapeDtypeStruct((R, C), F32)] * 3,
        compiler_params=_params(1),
    )(w, g, m, v)


def _adamw_small(small_all, w, m, v, name):
    def body(s_ref, w_ref, m_ref, v_ref, g_out, d_out, m_out, v_out):
        g = s_ref[0]
        for d in range(1, N_DEV):
            g = g + s_ref[d]
        g_out[...] = g
        d_out[...], m_out[...], v_out[...] = _adamw_math(w_ref[...], g, m_ref[...], v_ref[...])

    vm = pl.BlockSpec(memory_space=pltpu.VMEM)
    return pl.pallas_call(
        body, name=name, in_specs=[vm] * 4, out_specs=[vm] * 4,
        out_shape=[jax.ShapeDtypeStruct(w.shape, F32)] * 4,
    )(small_all, w, m, v)


def _pack_small(norm_g, pool_scale, norm_f, extra_row):
    return jnp.concatenate([norm_g.reshape(2, D_MODEL), pool_scale.reshape(2, D_MODEL),
                            norm_f.reshape(1, D_MODEL), extra_row,
                            jnp.zeros((2, D_MODEL), F32)], axis=0)


def kernel(x, norm_g, pool_w_in, pool_w, pool_scale, pool_w_out, sb_w_in, sb_w_out, norm_f, loss_target, m_norm_g, m_pool_w_in, m_pool_w, m_pool_scale, m_pool_w_out, m_sb_w_in, m_sb_w_out, m_norm_f, v_norm_g, v_pool_w_in, v_pool_w, v_pool_scale, v_pool_w_out, v_sb_w_in, v_sb_w_out, v_norm_f):
    nb, S, _ = x.shape
    T = nb * S
    xt = x.reshape(T, D_MODEL)
    target = loss_target.reshape(T, D_MODEL)
    cx, cy, cc = _place()

    def shard2d(w):
        return w.reshape(-1, w.shape[-1])

    names = ("pool_w_in", "pool_w", "pool_w_out", "sb_w_in", "sb_w_out")
    w_shards = [shard2d(w) for w in (pool_w_in, pool_w, pool_w_out, sb_w_in, sb_w_out)]
    m_shards = [shard2d(w) for w in (m_pool_w_in, m_pool_w, m_pool_w_out, m_sb_w_in, m_sb_w_out)]
    v_shards = [shard2d(w) for w in (v_pool_w_in, v_pool_w, v_pool_w_out, v_sb_w_in, v_sb_w_out)]

    chip = (2 * cx + cy).reshape(1).astype(jnp.int32)
    c_arr = cc.reshape(1).astype(jnp.int32)
    place = jnp.stack([2 * cx + cy, cc]).astype(jnp.int32)
    slots = [_cast_to_slot(w, chip, "cast_" + nm) for w, nm in zip(w_shards, names)]
    g0, g1, gf = norm_g[0:1], norm_g[1:2], norm_f.reshape(1, D_MODEL)

    w_pin, = _allgather_weights(slots[:1], "allgather_pool_in_weights")
    mix_send, mix_recv, mix_slots, token = _exchange_start(slots[1:3], w_pin, _weight_plan, 6,
                                                           "pool_weights_start")
    sb_send, sb_recv, sb_slots, token = _exchange_start(slots[3:], token, _weight_plan, 6,
                                                        "sb_weights_start")

    proj0, u0 = _rms_matmul(xt, g0 + token[0:1, 0:1], w_pin, [(2 * D_INNER, BF16)], "pool_in_proj",
                            tn=w_pin.shape[2])
    mix_slots = _exchange_wait(mix_slots, mix_send, mix_recv, proj0, _weight_plan, "pool_weights_wait")
    w_g, w_pout = _allgather_weights(mix_slots, "pool_weights_forward", landed=True)
    w_pout = w_pout.reshape(D_INNER, D_MODEL)
    y0, pooled, mixed = _pool_fwd(proj0, w_g, pool_scale, S, "pool_mix")
    sb_slots = _exchange_wait(sb_slots, sb_send, sb_recv, y0, _weight_plan, "sb_weights_wait")
    w_sin, w_sout = _allgather_weights(sb_slots, "sb_weights_forward", landed=True)
    w_sout = w_sout.reshape(D_INNER, D_MODEL)
    h1 = _matmul_residual(y0, w_pout, xt, "pool_out_proj")
    n1 = 2 * QK_WIDTH + 2 * D_INNER
    qkvz, u1 = _rms_matmul(h1, g1, w_sin, [(n1, BF16)], "sb_in_proj", tn=w_sin.shape[2])
    o, y1, ltot = _attn_fwd(qkvz, S, "sb_attention")
    dh2, d_norm_f, loss_row = _out_proj_loss_head(y1, w_sout, h1, gf, target, "sb_out_proj_loss_head")

    def reduce_start(partials, tag):
        from_sibling = _sibling_exchange(partials, None, "grad_sibling_exchange_" + tag)
        sums = [_chip_sum(p, s, c_arr, "grad_chip_sum_%s_%d" % (tag, i))
                for i, (p, s) in enumerate(zip(partials, from_sibling))]
        lands = [lax.empty((3,) + s.shape[1:], BF16) for s in sums]
        send, recv, bufs, token = _exchange_start(sums + lands, c_arr, _chip_sum_plan, 3 * len(sums),
                                                  "grad_chip_exchange_start_" + tag)
        return (partials, list(from_sibling), send, recv, bufs), token[0:1, 0:1]

    def reduce_finish(started, after, tag):
        partials, from_sibling, send, recv, bufs = started
        received = _exchange_wait(bufs, send, recv, after, _chip_sum_plan,
                                  "grad_chip_exchange_wait_" + tag)[len(partials):]
        return [_reduce_half(p, s, r, place, "grad_reduce_%s_%d" % (tag, i))
                for i, (p, s, r) in enumerate(zip(partials, from_sibling, received))]

    shard = lambda i, j, t: (j, 0, 0)
    gw_sout = _matmul_tn(y1, dh2, D_INNER, D_MODEL, (D_INNER, D_MODEL), (1024, 1024),
                         lambda i, j, t: (i, j), "grad_sb_w_out", bm=1024, bn=1024)
    sout_started, token = reduce_start([gw_sout.reshape(N_CHIPS, -1, D_MODEL)], "sb_out")
    do, dproj1 = _attn_gate_bwd(dh2, w_sout, qkvz, o, token, "sb_gate_bwd")
    dproj1 = _attn_bwd(qkvz, do, ltot, dproj1, S, "sb_attention_bwd")
    gw_sin = _matmul_tn(u1, dproj1, D_MODEL, n1, (N_CHIPS, D_MODEL, n1 // 4), (1, D_MODEL, n1 // 4),
                        shard, "grad_sb_w_in", bm=D_MODEL, bn=n1 // 4)
    sin_started, token = reduce_start([gw_sin], "sb_in")
    dh1, d_g1 = _matmul_nt_rms_bwd(dproj1, w_sin, h1, g1 + token, dh2, "sb_in_bwd", tk=w_sin.shape[2])
    gw_pout = _matmul_tn(y0, dh1, D_INNER, D_MODEL, (D_INNER, D_MODEL), (1024, 1024),
                         lambda i, j, t: (i, j), "grad_pool_w_out", bm=1024, bn=1024)
    dmixed, dproj0, d_scale = _pool_gate_bwd(dh1, w_pout, proj0, mixed, pool_scale, "pool_gate_bwd")
    gw_g = _matmul_tn(pooled, dmixed, D_INNER, D_INNER, (N_CHIPS, GROUP_DIM, GROUP_DIM),
                      (N_CHIPS, GROUP_DIM // N_CHIPS, GROUP_DIM), lambda i, j, t: (0, i, 0),
                      "grad_pool_w", bm=GROUP_DIM, bn=GROUP_DIM, diagonal_blocks=True)
    mix_started, token = reduce_start([gw_g, gw_pout.reshape(N_CHIPS, -1, D_MODEL)], "pool_mix")
    dproj0 = _pool_bwd(dmixed, w_g, dproj0, token, S, "pool_bwd")
    n0 = 2 * D_INNER
    gw_pin = _matmul_tn(u0, dproj0, D_MODEL, n0, (N_CHIPS, D_MODEL, n0 // 4), (1, D_MODEL, n0 // 4),
                        shard, "grad_pool_w_in", bm=D_MODEL, bn=n0 // 4)
    pin_started, token = reduce_start([gw_pin], "pool_in")
    dx, d_g0 = _matmul_nt_rms_bwd(dproj0, w_pin, xt, g0 + token, dh1, "pool_in_bwd", tk=w_pin.shape[2])

    small = _pack_small(jnp.concatenate([d_g0, d_g1], axis=0), d_scale, d_norm_f,
                        jnp.broadcast_to(loss_row[:, :1], (1, D_MODEL)))
    small_all, = _sibling_exchange([], small, "small_sums_exchange")
    grads = _join_halves(reduce_finish(pin_started, dx, "pool_in")
                         + reduce_finish(mix_started, dx, "pool_mix")
                         + reduce_finish(sin_started, dx, "sb_in")
                         + reduce_finish(sout_started, dx, "sb_out"), "grad_join_halves")

    deltas, new_m, new_v = [], [], []
    for w, g, m, v, nm in zip(w_shards, grads, m_shards, v_shards, names):
        d, mm, vv = _adamw(w, g, m, v, "adamw_" + nm)
        deltas.append(d)
        new_m.append(mm)
        new_v.append(vv)

    zero_row = jnp.zeros((1, D_MODEL), F32)
    g_small, d_small, m_small, v_small = _adamw_small(
        small_all, _pack_small(norm_g, pool_scale, norm_f, zero_row),
        _pack_small(m_norm_g, m_pool_scale, m_norm_f, zero_row),
        _pack_small(v_norm_g, v_pool_scale, v_norm_f, zero_row + 1.0), "adamw_small")
    loss = g_small[5, 0]

    def unpack_small(a):
        return a[0:2], a[2:4].reshape(1, D_INNER), a[4]

    def assemble(big, small3):
        ng, ps, nf = small3
        return [ng, big[0].reshape(pool_w_in.shape), big[1].reshape(pool_w.shape), ps,
                big[2].reshape(pool_w_out.shape), big[3].reshape(sb_w_in.shape),
                big[4].reshape(sb_w_out.shape), nf]

    return (loss, dx.reshape(x.shape),
            *assemble(grads, unpack_small(g_small)),
            *assemble(deltas, unpack_small(d_small)),
            *assemble(new_m, unpack_small(m_small)),
            *assemble(new_v, unpack_small(v_small)))
```

```python
import jax
import jax.numpy as jnp
from jax import lax
from jax.experimental import pallas as pl
from jax.experimental.pallas import tpu as pltpu

F32 = jnp.float32
BF16 = jnp.bfloat16
MESH = pl.DeviceIdType.MESH

D_MODEL = 1024
D_INNER = 2048
N_GROUPS = 4
GROUP_DIM = 512
HEAD_PAIR_QK = 128
HEAD_V = 128
QK_WIDTH = 1024
RMS_EPS = 1e-6
HALO = 16
N_CHIPS = 4
N_DEV = 8

ADAM_LR = 0.001
ADAM_B1 = 0.9
ADAM_B2 = 0.999
ADAM_EPS = 1e-08
ADAM_WD = 0.01
ADAM_STEP = 10

VMEM_LIMIT = 56 * 1024 * 1024

HBM_SPEC = pl.BlockSpec(memory_space=pltpu.HBM)


def _params(n_axes):
    return pltpu.CompilerParams(dimension_semantics=("arbitrary",) * n_axes,
                                vmem_limit_bytes=VMEM_LIMIT)


def _dot(a, b):
    return jnp.dot(a, b, preferred_element_type=F32)


def _dot_nt(a, b):
    return lax.dot_general(a, b, (((1,), (1,)), ((), ())), preferred_element_type=F32)


def _dot_tn(a, b):
    return lax.dot_general(a, b, (((0,), (0,)), ((), ())), preferred_element_type=F32)


def _sigmoid(z):
    return 1.0 / (1.0 + jnp.exp(-z))


def _row_blocks(tm, rows=256):
    return [slice(r, r + rows) for r in range(0, tm, rows)]


def _rms_matmul(h, g_row, w4, outs, name, tm=1024, tn=512):
    T = h.shape[0]
    per_shard = w4.shape[2] // tn
    starts = [0]
    for width, _ in outs:
        starts.append(starts[-1] + width // tn)
    n_out = len(outs)

    def body(h_ref, g_ref, w_ref, *rest):
        o_refs, u_out, u_s = rest[:n_out], rest[n_out], rest[n_out + 1]
        n = pl.program_id(1)

        @pl.when(n == 0)
        def _():
            x = h_ref[...]
            inv = lax.rsqrt(jnp.mean(x * x, axis=-1, keepdims=True) + RMS_EPS)
            u = (x * inv * g_ref[...]).astype(BF16)
            u_s[...] = u
            u_out[...] = u

        res = _dot(u_s[...], w_ref[0])
        for k in range(n_out):
            @pl.when((n >= starts[k]) & (n < starts[k + 1]))
            def _():
                o_refs[k][...] = res.astype(o_refs[k].dtype)

    def out_map(k):
        return lambda m, n: (m, jnp.clip(n - starts[k], 0, starts[k + 1] - starts[k] - 1))

    return pl.pallas_call(
        body, name=name, grid=(T // tm, starts[-1]),
        in_specs=[pl.BlockSpec((tm, D_MODEL), lambda m, n: (m, 0)),
                  pl.BlockSpec((1, D_MODEL), lambda m, n: (0, 0)),
                  pl.BlockSpec((1, D_MODEL, tn), lambda m, n: (n // per_shard, 0, n % per_shard))],
        out_specs=[pl.BlockSpec((tm, tn), out_map(k)) for k in range(n_out)]
        + [pl.BlockSpec((tm, D_MODEL), lambda m, n: (m, 0))],
        out_shape=[jax.ShapeDtypeStruct((T, width), dt) for width, dt in outs]
        + [jax.ShapeDtypeStruct((T, D_MODEL), BF16)],
        scratch_shapes=[pltpu.VMEM((tm, D_MODEL), BF16)],
        compiler_params=_params(2),
    )(h, g_row, w4)


def _matmul_residual(a, w, res, name, tm=1024, tn=512):
    T, K = a.shape
    N = w.shape[1]

    def body(a_ref, w_ref, r_ref, o_ref):
        o_ref[...] = r_ref[...] + _dot(a_ref[...], w_ref[...])

    return pl.pallas_call(
        body, name=name, grid=(T // tm, N // tn),
        in_specs=[pl.BlockSpec((tm, K), lambda m, n: (m, 0)),
                  pl.BlockSpec((K, tn), lambda m, n: (0, n)),
                  pl.BlockSpec((tm, tn), lambda m, n: (m, n))],
        out_specs=pl.BlockSpec((tm, tn), lambda m, n: (m, n)),
        out_shape=jax.ShapeDtypeStruct((T, N), F32),
        compiler_params=_params(2),
    )(a, w, res)


def _matmul_tn(a, b, a_cols, b_cols, out_shape, out_block, out_map, name, bm, bn, tk=512,
               diagonal_blocks=False):
    T = a.shape[0]

    def body(a_ref, b_ref, o_ref):
        @pl.when(pl.program_id(2) == 0)
        def _():
            o_ref[...] = jnp.zeros_like(o_ref)

        part = _dot_tn(a_ref[...].astype(BF16), b_ref[...].astype(BF16))
        o_ref[...] += part.reshape(o_ref.shape)

    b_map = (lambda i, j, t: (t, i)) if diagonal_blocks else (lambda i, j, t: (t, j))
    return pl.pallas_call(
        body, name=name, grid=(a_cols // bm, 1 if diagonal_blocks else b_cols // bn, T // tk),
        in_specs=[pl.BlockSpec((tk, bm), lambda i, j, t: (t, i)),
                  pl.BlockSpec((tk, bn), b_map)],
        out_specs=pl.BlockSpec(out_block, out_map),
        out_shape=jax.ShapeDtypeStruct(out_shape, F32),
        compiler_params=_params(3),
    )(a, b)


def _matmul_nt_rms_bwd(dproj, w4, h, g_row, dres, name, tm=1024, tk=512):
    T, cols = dproj.shape
    per_shard = w4.shape[2] // tk
    nk = cols // tk

    def body(dp_ref, w_ref, h_ref, g_ref, r_ref, dx_ref, dg_ref, acc):
        m, k = pl.program_id(0), pl.program_id(1)

        @pl.when(k == 0)
        def _():
            acc[...] = jnp.zeros_like(acc)

        @pl.when((k == 0) & (m == 0))
        def _():
            dg_ref[...] = jnp.zeros_like(dg_ref)

        acc[...] += _dot_nt(dp_ref[...], w_ref[0])

        @pl.when(k == nk - 1)
        def _():
            du = acc[...]
            x = h_ref[...]
            inv = lax.rsqrt(jnp.mean(x * x, axis=-1, keepdims=True) + RMS_EPS)
            xhat = x * inv
            dg_ref[...] += jnp.sum(du * xhat, axis=0, keepdims=True)
            dxh = du * g_ref[...]
            proj = jnp.mean(dxh * xhat, axis=-1, keepdims=True)
            dx_ref[...] = r_ref[...] + inv * (dxh - xhat * proj)

    return pl.pallas_call(
        body, name=name, grid=(T // tm, nk),
        in_specs=[pl.BlockSpec((tm, tk), lambda m, k: (m, k)),
                  pl.BlockSpec((1, D_MODEL, tk), lambda m, k: (k // per_shard, 0, k % per_shard)),
                  pl.BlockSpec((tm, D_MODEL), lambda m, k: (m, 0)),
                  pl.BlockSpec((1, D_MODEL), lambda m, k: (0, 0)),
                  pl.BlockSpec((tm, D_MODEL), lambda m, k: (m, 0))],
        out_specs=[pl.BlockSpec((tm, D_MODEL), lambda m, k: (m, 0)),
                   pl.BlockSpec((1, D_MODEL), lambda m, k: (0, 0))],
        out_shape=[jax.ShapeDtypeStruct((T, D_MODEL), F32),
                   jax.ShapeDtypeStruct((1, D_MODEL), F32)],
        scratch_shapes=[pltpu.VMEM((tm, D_MODEL), F32)],
        compiler_params=_params(2),
    )(dproj, w4, h, g_row, dres)


def _window_of(g):
    return jnp.left_shift(2, g)


def _select_stage(g, stages):
    res = stages[0]
    for i in range(1, len(stages)):
        res = jnp.where(g >= i, stages[i], res)
    return res


def _pool_fwd(proj0, wg4, scale_row, S, name, tm=1024):
    T = proj0.shape[0]
    tm = min(tm, S)
    blocks_per_seq = S // tm
    hb = tm // HALO

    def body(x_ref, halo_ref, z_ref, w_ref, s_ref, y_ref, p_ref, mix_ref):
        m, g = pl.program_id(0), pl.program_id(1)
        first = (m % blocks_per_seq) == 0
        halo = jnp.where(first, 0.0, halo_ref[...].astype(F32))
        x = x_ref[...].astype(F32)
        ext = jnp.concatenate([halo, x], axis=0)
        stages = []
        cur = ext
        for sh in (1, 2, 4, 8):
            cur = cur + pltpu.roll(cur, sh, 0)
            stages.append(cur[HALO:, :])
        win_sum = _select_stage(g, stages)
        pos = (m % blocks_per_seq) * tm + lax.broadcasted_iota(jnp.int32, (tm, 1), 0)
        count = jnp.minimum(pos + 1, _window_of(g)).astype(F32)
        p_ref[...] = (win_sum / count - x).astype(BF16)
        w = w_ref[...].reshape(GROUP_DIM, GROUP_DIM)
        for rows in _row_blocks(tm):
            mixed = _dot(p_ref[rows, :], w)
            z = z_ref[rows, :].astype(F32)
            y_ref[rows, :] = (mixed * s_ref[...] * (z * _sigmoid(z))).astype(BF16)
            mix_ref[rows, :] = mixed.astype(BF16)

    blk = lambda m, g: (m, g)
    return pl.pallas_call(
        body, name=name, grid=(T // tm, N_GROUPS),
        in_specs=[pl.BlockSpec((tm, GROUP_DIM), blk),
                  pl.BlockSpec((HALO, GROUP_DIM), lambda m, g: (jnp.maximum(m * hb - 1, 0), g)),
                  pl.BlockSpec((tm, GROUP_DIM), lambda m, g: (m, N_GROUPS + g)),
                  pl.BlockSpec((N_CHIPS, GROUP_DIM // N_CHIPS, GROUP_DIM), lambda m, g: (0, g, 0)),
                  pl.BlockSpec((1, GROUP_DIM), lambda m, g: (0, g))],
        out_specs=[pl.BlockSpec((tm, GROUP_DIM), blk)] * 3,
        out_shape=[jax.ShapeDtypeStruct((T, D_INNER), BF16)] * 3,
        compiler_params=_params(2),
    )(proj0, proj0, proj0, wg4, scale_row)


def _pool_gate_bwd(dh, w_out, proj0, mixed, scale_row, name, tm=1024, tn=512):
    T = dh.shape[0]
    gate_b0 = D_INNER // tn

    def body(dh_ref, w_ref, z_ref, mix_ref, s_ref, dm_ref, dz_ref, ds_ref, dh_s):
        m, n = pl.program_id(0), pl.program_id(1)

        @pl.when((m == 0) & (n == 0))
        def _():
            ds_ref[...] = jnp.zeros_like(ds_ref)

        @pl.when(n == 0)
        def _():
            dh_s[...] = dh_ref[...].astype(BF16)

        cols = pl.ds(pl.multiple_of(n * tn, tn), tn)
        s = s_ref[...]
        ds = ds_ref[:, cols]
        for rows in _row_blocks(tm):
            dy = _dot_nt(dh_s[rows, :], w_ref[...])
            z = z_ref[rows, :].astype(F32)
            sig = _sigmoid(z)
            silu = z * sig
            mixed = mix_ref[rows, :].astype(F32)
            dm_ref[rows, :] = (dy * s * silu).astype(BF16)
            dz_ref[rows, :] = (dy * mixed * s * (sig * (1.0 + z * (1.0 - sig)))).astype(BF16)
            ds = ds + jnp.sum(dy * mixed * silu, axis=0, keepdims=True)
        ds_ref[:, cols] = ds

    return pl.pallas_call(
        body, name=name, grid=(T // tm, D_INNER // tn),
        in_specs=[pl.BlockSpec((tm, D_MODEL), lambda m, n: (m, 0)),
                  pl.BlockSpec((tn, D_MODEL), lambda m, n: (n, 0)),
                  pl.BlockSpec((tm, tn), lambda m, n: (m, gate_b0 + n)),
                  pl.BlockSpec((tm, tn), lambda m, n: (m, n)),
                  pl.BlockSpec((1, tn), lambda m, n: (0, n))],
        out_specs=[pl.BlockSpec((tm, tn), lambda m, n: (m, n)),
                   pl.BlockSpec((tm, tn), lambda m, n: (m, gate_b0 + n)),
                   pl.BlockSpec((1, D_INNER), lambda m, n: (0, 0))],
        out_shape=[jax.ShapeDtypeStruct((T, D_INNER), BF16),
                   jax.ShapeDtypeStruct((T, 2 * D_INNER), BF16),
                   jax.ShapeDtypeStruct((1, D_INNER), F32)],
        scratch_shapes=[pltpu.VMEM((tm, D_MODEL), BF16)],
        compiler_params=_params(2),
    )(dh, w_out, proj0, mixed, scale_row)


def _pool_bwd(dmixed, wg4, dproj0, after, S, name, tm=1024):
    T = dmixed.shape[0]
    tm = min(tm, S)
    blocks_per_seq = S // tm
    hb = tm // HALO
    n_halo_blocks = T // HALO

    def body(dm_ref, halo_ref, w_ref, _, __, o_ref):
        m, g = pl.program_id(0), pl.program_id(1)
        ext = jnp.concatenate([dm_ref[...], halo_ref[...]], axis=0)
        dp = _dot_nt(ext, w_ref[...].reshape(GROUP_DIM, GROUP_DIM))
        pos = (m % blocks_per_seq) * tm + lax.broadcasted_iota(jnp.int32, (tm + HALO, 1), 0)
        count = jnp.minimum(pos + 1, _window_of(g)).astype(F32)
        c = jnp.where(pos < S, dp / count, 0.0)
        n = tm + HALO
        stages = []
        cur = c
        for sh in (1, 2, 4, 8):
            cur = cur + pltpu.roll(cur, n - sh, 0)
            stages.append(cur[:tm, :])
        o_ref[...] = (_select_stage(g, stages) - dp[:tm, :]).astype(BF16)

    blk = lambda m, g: (m, g)
    return pl.pallas_call(
        body, name=name, grid=(T // tm, N_GROUPS),
        in_specs=[pl.BlockSpec((tm, GROUP_DIM), blk),
                  pl.BlockSpec((HALO, GROUP_DIM),
                               lambda m, g: (jnp.minimum((m + 1) * hb, n_halo_blocks - 1), g)),
                  pl.BlockSpec((N_CHIPS, GROUP_DIM // N_CHIPS, GROUP_DIM), lambda m, g: (0, g, 0)),
                  HBM_SPEC, ANY_SPEC],
        out_specs=pl.BlockSpec((tm, GROUP_DIM), blk),
        out_shape=jax.ShapeDtypeStruct(dproj0.shape, dproj0.dtype),
        input_output_aliases={3: 0},
        compiler_params=_params(2),
    )(dmixed, dmixed, wg4, dproj0, after)


TQ = 256


def _split_dot(x, m):
    hi = x.astype(BF16)
    lo = (x - hi.astype(F32)).astype(BF16)
    return _dot(hi, m) + _dot(lo, m)


NEG_LOG2E = -1.4426950408889634


def _log_terms(z):
    soft = jnp.log(1.0 + jnp.exp2(jnp.abs(z) * NEG_LOG2E))
    log_beta = jnp.minimum(z, 0.0) - soft
    return log_beta, log_beta - z


N_HEADS = 16
FWD_HEADS = BWD_HEADS = 4


def _masked_heads(x, heads):
    lane = lax.broadcasted_iota(jnp.int32, (1, HEAD_PAIR_QK), 1)
    out = []
    for hh in range(heads):
        slab = x[:, (hh // 2) * HEAD_PAIR_QK:(hh // 2 + 1) * HEAD_PAIR_QK]
        out.append(jnp.where((lane // 64) == hh % 2, slab, jnp.zeros_like(slab)))
    return out


def _attn_fwd(qkvz, S, name):
    T = qkvz.shape[0]
    nq = S // TQ
    HEADS, QK_W, V_W = FWD_HEADS, FWD_HEADS * 64, FWD_HEADS * HEAD_V
    k_b0 = QK_WIDTH // QK_W
    v_b0 = 2 * QK_WIDTH // V_W
    z_b0 = (2 * QK_WIDTH + D_INNER) // V_W
    hs = range(HEADS)

    def body(q_ref, k_ref, v_ref, z_ref, o_ref, y_ref, lt_ref):
        row = lax.broadcasted_iota(jnp.int32, (TQ, TQ), 0)
        col = lax.broadcasted_iota(jnp.int32, (TQ, TQ), 1)
        causal = col < row
        later_in_block = (row > col).astype(BF16)
        lax.fori_loop(0, nq, lambda qi, _: q_block(qi, causal, later_in_block,
                                                   q_ref, k_ref, v_ref, z_ref, o_ref, y_ref, lt_ref), 0)

    def q_block(qi, causal, later_in_block, q_ref, k_ref, v_ref, z_ref, o_ref, y_ref, lt_ref):
        rows = pl.ds(pl.multiple_of(qi * TQ, TQ), TQ)
        qms = [qm * 0.125 for qm in _masked_heads(q_ref[rows, :], HEADS)]

        def step(j, carry, diagonal):
            koff = pl.multiple_of(j * TQ, TQ)
            kbs = [k_ref[pl.ds(koff, TQ), p * HEAD_PAIR_QK:(p + 1) * HEAD_PAIR_QK]
                   for p in range(HEADS // 2)]
            run, acc = [carry[2 * hh] for hh in hs], [carry[2 * hh + 1] for hh in hs]
            z = [_dot_nt(qms[hh], kbs[hh // 2]) for hh in hs]
            terms = [_log_terms(z[hh]) for hh in hs]
            log_om = [jnp.where(causal, t[1], 0.0) if diagonal else t[1] for t in terms]
            later = [_split_dot(log_om[hh], later_in_block) for hh in hs]
            a = [jnp.exp(terms[hh][0] + (run[hh] + later[hh])) for hh in hs]
            if diagonal:
                a = [jnp.where(causal, a[hh], 0.0) for hh in hs]
            out = []
            for hh in hs:
                vb = v_ref[pl.ds(koff, TQ), hh * HEAD_V:(hh + 1) * HEAD_V]
                out += [run[hh] + jnp.sum(log_om[hh], axis=1, keepdims=True),
                        acc[hh] + _dot(a[hh].astype(BF16), vb)]
            return tuple(out)

        zero = (jnp.zeros((TQ, 1), F32), jnp.zeros((TQ, HEAD_V), F32))
        carry = step(qi, zero * HEADS, True)
        carry = lax.fori_loop(0, qi, lambda i, c: step(qi - 1 - i, c, False), carry)
        for hh in hs:
            sl = slice(hh * HEAD_V, (hh + 1) * HEAD_V)
            acc = carry[2 * hh + 1]
            z = z_ref[rows, sl].astype(F32)
            o_ref[rows, sl] = acc.astype(BF16)
            y_ref[rows, sl] = (acc * (z * _sigmoid(z))).astype(BF16)
            lt_ref[rows, hh:hh + 1] = carry[2 * hh]
        return 0

    blk = lambda b, p: (b, p)
    return pl.pallas_call(
        body, name=name, grid=(T // S, N_HEADS // HEADS),
        in_specs=[pl.BlockSpec((S, QK_W), blk),
                  pl.BlockSpec((S, QK_W), lambda b, p: (b, k_b0 + p)),
                  pl.BlockSpec((S, V_W), lambda b, p: (b, v_b0 + p)),
                  pl.BlockSpec((S, V_W), lambda b, p: (b, z_b0 + p))],
        out_specs=[pl.BlockSpec((S, V_W), blk),
                   pl.BlockSpec((S, V_W), blk),
                   pl.BlockSpec((None, S, HEADS), lambda b, p: (p, b, 0))],
        out_shape=[jax.ShapeDtypeStruct((T, D_INNER), BF16),
                   jax.ShapeDtypeStruct((T, D_INNER), BF16),
                   jax.ShapeDtypeStruct((N_HEADS // HEADS, T, HEADS), F32)],
        compiler_params=_params(2),
    )(qkvz, qkvz, qkvz, qkvz)


def _attn_gate_bwd(dh, w_out, qkvz, o, after, name, tm=1024, tn=512):
    T = dh.shape[0]
    gate_b0 = (2 * QK_WIDTH + D_INNER) // tn

    def body(dh_ref, w_ref, z_ref, o_ref, _, do_ref, dz_ref, dh_s):
        @pl.when(pl.program_id(1) == 0)
        def _():
            dh_s[...] = dh_ref[...].astype(BF16)

        for rows in _row_blocks(tm):
            dy = _dot_nt(dh_s[rows, :], w_ref[...])
            z = z_ref[rows, :].astype(F32)
            sig = _sigmoid(z)
            do_ref[rows, :] = (dy * (z * sig)).astype(BF16)
            dz_ref[rows, :] = (dy * o_ref[rows, :].astype(F32)
                               * (sig * (1.0 + z * (1.0 - sig)))).astype(BF16)

    return pl.pallas_call(
        body, name=name, grid=(T // tm, D_INNER // tn),
        in_specs=[pl.BlockSpec((tm, D_MODEL), lambda m, n: (m, 0)),
                  pl.BlockSpec((tn, D_MODEL), lambda m, n: (n, 0)),
                  pl.BlockSpec((tm, tn), lambda m, n: (m, gate_b0 + n)),
                  pl.BlockSpec((tm, tn), lambda m, n: (m, n)),
                  ANY_SPEC],
        out_specs=[pl.BlockSpec((tm, tn), lambda m, n: (m, n)),
                   pl.BlockSpec((tm, tn), lambda m, n: (m, gate_b0 + n))],
        out_shape=[jax.ShapeDtypeStruct((T, D_INNER), BF16),
                   jax.ShapeDtypeStruct((T, 2 * QK_WIDTH + 2 * D_INNER), BF16)],
        scratch_shapes=[pltpu.VMEM((tm, D_MODEL), BF16)],
        compiler_params=_params(2),
    )(dh, w_out, qkvz, o, after)


def _attn_bwd(qkv, do, ltot, dproj1, S, name):
    T = qkv.shape[0]
    nq = S // TQ
    HEADS, QK_W, V_W = BWD_HEADS, BWD_HEADS * 64, BWD_HEADS * HEAD_V
    k_b0 = QK_WIDTH // QK_W
    v_b0 = 2 * QK_WIDTH // V_W
    hs = range(HEADS)
    pairs = range(HEADS // 2)

    def body(q_ref, k_ref, v_ref, do_ref, lt_ref, _, out_ref, dq_s, dk_s, dv_s, dkb_s, dvb_s, sems):
        b, p = pl.program_id(0), pl.program_id(1)
        row = lax.broadcasted_iota(jnp.int32, (TQ, TQ), 0)
        col = lax.broadcasted_iota(jnp.int32, (TQ, TQ), 1)
        causal = col < row
        upto = (row <= col).astype(BF16)
        before = (row < col).astype(BF16)
        dk_s[...] = jnp.zeros_like(dk_s)
        dv_s[...] = jnp.zeros_like(dv_s)

        def q_block(qi, _):
            qoff = pl.multiple_of(qi * TQ, TQ)
            qms = [qm * 0.125 for qm in _masked_heads(q_ref[pl.ds(qoff, TQ), :], HEADS)]
            vsl = [slice(hh * HEAD_V, (hh + 1) * HEAD_V) for hh in hs]
            psl = [slice(pp * HEAD_PAIR_QK, (pp + 1) * HEAD_PAIR_QK) for pp in pairs]
            do_h = [do_ref[pl.ds(qoff, TQ), sl] for sl in vsl]
            total = [lt_ref[pl.ds(qoff, TQ), hh:hh + 1] for hh in hs]

            def k_block(j, carry, diagonal):
                koff = pl.multiple_of(j * TQ, TQ)
                kms = _masked_heads(k_ref[pl.ds(koff, TQ), :], HEADS)
                g_before = [carry[2 * hh] for hh in hs]
                lom_before = [carry[2 * hh + 1] for hh in hs]
                z = [_dot_nt(qms[hh], kms[hh]) for hh in hs]
                da = [_dot_nt(do_h[hh], v_ref[pl.ds(koff, TQ), vsl[hh]]) for hh in hs]
                terms = [_log_terms(z[hh]) for hh in hs]
                log_om = [jnp.where(causal, t[1], 0.0) if diagonal else t[1] for t in terms]
                prefix = [_split_dot(log_om[hh], upto) for hh in hs]
                a = [jnp.exp(terms[hh][0] + ((total[hh] - lom_before[hh]) - prefix[hh])) for hh in hs]
                if diagonal:
                    a = [jnp.where(causal, a[hh], 0.0) for hh in hs]
                g = [a[hh] * da[hh] for hh in hs]
                g_prefix = [_dot(g[hh].astype(BF16), before) for hh in hs]
                out, dzs = [], []
                for hh in hs:
                    beta = jnp.exp(terms[hh][0])
                    g_excl = (g_before[hh] + g_prefix[hh]) * beta
                    if diagonal:
                        g_excl = jnp.where(causal, g_excl, 0.0)
                    dzs.append((g[hh] * (1.0 - beta) - g_excl).astype(BF16))
                    out += [g_before[hh] + jnp.sum(g[hh], axis=1, keepdims=True),
                            lom_before[hh] + jnp.sum(log_om[hh], axis=1, keepdims=True)]
                for hh in hs:
                    dv_s[pl.ds(koff, TQ), vsl[hh]] += _dot_tn(a[hh].astype(BF16), do_h[hh])
                dq = []
                for pp in pairs:
                    pair = slice(2 * pp, 2 * pp + 2)
                    dq.append(carry[2 * HEADS + pp] + _dot(jnp.concatenate(dzs[pair], axis=1),
                                                           jnp.concatenate(kms[pair], axis=0)))
                    dk_s[pl.ds(koff, TQ), psl[pp]] += _dot_tn(jnp.concatenate(dzs[pair], axis=0),
                                                              jnp.concatenate(qms[pair], axis=0))
                return tuple(out) + tuple(dq)

            zero = jnp.zeros((TQ, 1), F32)
            carry = (zero,) * (2 * HEADS) + (jnp.zeros((TQ, HEAD_PAIR_QK), F32),) * (HEADS // 2)
            carry = lax.fori_loop(0, qi, lambda j, c: k_block(j, c, False), carry)
            carry = k_block(qi, carry, True)
            for pp in pairs:
                dq_s[pl.ds(qoff, TQ), psl[pp]] = (carry[2 * HEADS + pp] * 0.125).astype(BF16)
            return 0

        lax.fori_loop(0, nq, q_block, 0)
        dkb_s[...] = dk_s[...].astype(BF16)
        dvb_s[...] = dv_s[...].astype(BF16)
        rows = pl.ds(pl.multiple_of(b * S, TQ), S)
        copies = [
            pltpu.make_async_copy(
                dq_s, out_ref.at[rows, pl.ds(pl.multiple_of(p * QK_W, 128), QK_W)], sems.at[0]),
            pltpu.make_async_copy(
                dkb_s, out_ref.at[rows, pl.ds(pl.multiple_of(QK_WIDTH + p * QK_W, 128), QK_W)],
                sems.at[1]),
            pltpu.make_async_copy(
                dvb_s, out_ref.at[rows, pl.ds(pl.multiple_of(2 * QK_WIDTH + p * V_W, 128), V_W)],
                sems.at[2]),
        ]
        for cp in copies:
            cp.start()
        for cp in copies:
            cp.wait()

    return pl.pallas_call(
        body, name=name, grid=(T // S, N_HEADS // HEADS),
        in_specs=[pl.BlockSpec((S, QK_W), lambda b, p: (b, p)),
                  pl.BlockSpec((S, QK_W), lambda b, p: (b, k_b0 + p)),
                  pl.BlockSpec((S, V_W), lambda b, p: (b, v_b0 + p)),
                  pl.BlockSpec((S, V_W), lambda b, p: (b, p)),
                  pl.BlockSpec((None, S, HEADS), lambda b, p: (p, b, 0)),
                  HBM_SPEC],
        out_specs=HBM_SPEC,
        out_shape=jax.ShapeDtypeStruct(dproj1.shape, dproj1.dtype),
        input_output_aliases={5: 0},
        scratch_shapes=[pltpu.VMEM((S, QK_W), BF16),
                        pltpu.VMEM((S, QK_W), F32),
                        pltpu.VMEM((S, V_W), F32),
                        pltpu.VMEM((S, QK_W), BF16),
                        pltpu.VMEM((S, V_W), BF16),
                        pltpu.SemaphoreType.DMA((3,))],
        compiler_params=_params(2),
    )(qkv, qkv, qkv, do, ltot, dproj1)


def _out_proj_loss_head(a, w, res, g_row, target, name, tm=512):
    T, K = a.shape

    def body(a_ref, w_ref, r_ref, g_ref, t_ref, dh_ref, dg_ref, loss_ref):
        @pl.when(pl.program_id(0) == 0)
        def _():
            dg_ref[...] = jnp.zeros_like(dg_ref)
            loss_ref[...] = jnp.zeros_like(loss_ref)

        gain = g_ref[...]
        dg, loss = dg_ref[...], loss_ref[...]
        for rows in _row_blocks(tm):
            x = r_ref[rows, :] + _dot(a_ref[rows, :], w_ref[...])
            inv = lax.rsqrt(jnp.mean(x * x, axis=-1, keepdims=True) + RMS_EPS)
            xhat = x * inv
            err = xhat * gain - t_ref[rows, :]
            per_token = jnp.mean(err * err, axis=-1, keepdims=True)
            loss = loss + 0.5 * jnp.sum(per_token, axis=0, keepdims=True)
            dy = err * (1.0 / D_MODEL)
            dg = dg + jnp.sum(dy * xhat, axis=0, keepdims=True)
            dxh = dy * gain
            proj = jnp.mean(dxh * xhat, axis=-1, keepdims=True)
            dh_ref[rows, :] = inv * (dxh - xhat * proj)
        dg_ref[...] = dg
        loss_ref[...] = loss

    return pl.pallas_call(
        body, name=name, grid=(T // tm,),
        in_specs=[pl.BlockSpec((tm, K), lambda m: (m, 0)),
                  pl.BlockSpec((K, D_MODEL), lambda m: (0, 0)),
                  pl.BlockSpec((tm, D_MODEL), lambda m: (m, 0)),
                  pl.BlockSpec((1, D_MODEL), lambda m: (0, 0)),
                  pl.BlockSpec((tm, D_MODEL), lambda m: (m, 0))],
        out_specs=[pl.BlockSpec((tm, D_MODEL), lambda m: (m, 0)),
                   pl.BlockSpec((1, D_MODEL), lambda m: (0, 0)),
                   pl.BlockSpec((1, 128), lambda m: (0, 0))],
        out_shape=[jax.ShapeDtypeStruct((T, D_MODEL), F32),
                   jax.ShapeDtypeStruct((1, D_MODEL), F32),
                   jax.ShapeDtypeStruct((1, 128), F32)],
        compiler_params=_params(1),
    )(a, w, res, g_row, target)


def _place():
    return lax.axis_index("x"), lax.axis_index("y"), lax.axis_index("c")


def _other_chips(x, y):
    return [(1 - x, y), (x, 1 - y), (1 - x, 1 - y)]


def _half(ref, c):
    hr = ref.shape[-2] // 2
    return pl.ds(pl.multiple_of(c * hr, 8), hr)


def _cast_to_slot(shard, chip, name, tr=256):
    R, C = shard.shape

    def body(chip_ref, w_ref, o_ref):
        o_ref[0] = w_ref[...].astype(BF16)

    return pl.pallas_call(
        body, name=name,
        grid_spec=pltpu.PrefetchScalarGridSpec(
            num_scalar_prefetch=1, grid=(R // tr,),
            in_specs=[pl.BlockSpec((tr, C), lambda i, chip_ref: (i, 0))],
            out_specs=pl.BlockSpec((1, tr, C), lambda i, chip_ref: (chip_ref[0], i, 0))),
        out_shape=jax.ShapeDtypeStruct((N_CHIPS, R, C), BF16),
        compiler_params=_params(1),
    )(chip, shard)


def _weight_plan(bufs):
    x, y, c = _place()
    plan = []
    for buf in bufs:
        mine = buf.at[2 * x + y, _half(buf, c)]
        for ox, oy in _other_chips(x, y):
            plan.append((mine, mine, (ox, oy, c), buf.at[2 * ox + oy, _half(buf, c)]))
    return plan


def _chip_sum_plan(bufs):
    x, y, c = _place()
    n = len(bufs) // 2
    plan = []
    for sums, land in zip(bufs[:n], bufs[n:]):
        for k, (ox, oy) in enumerate(_other_chips(x, y)):
            plan.append((sums.at[2 * ox + oy], land.at[k], (ox, oy, c), land.at[k]))
    return plan


SEM_SPEC = pl.BlockSpec(memory_space=pltpu.SEMAPHORE)
ANY_SPEC = pl.BlockSpec(memory_space=pl.ANY)
DATAFLOW = pltpu.SideEffectType.DATAFLOW_SIDE_EFFECTING


def _in_hbm(a):
    return pltpu.with_memory_space_constraint(a, pltpu.HBM)


def _exchange_start(bufs, after, plan, n_copies, name):
    nb = len(bufs)

    def body(*refs):
        send_sems, recv_sems = refs[nb + 1], refs[nb + 2]
        for i, (src, dst, dev, _) in enumerate(plan(refs[:nb])):
            pltpu.make_async_remote_copy(
                src_ref=src, dst_ref=dst, send_sem=send_sems.at[i], recv_sem=recv_sems.at[i],
                device_id=dev, device_id_type=MESH).start()
        token = refs[-1]
        token[...] = jnp.zeros_like(token)

    res = pl.pallas_call(
        body, name=name,
        in_specs=[HBM_SPEC] * nb + [ANY_SPEC],
        out_specs=[SEM_SPEC, SEM_SPEC] + [HBM_SPEC] * nb + [pl.BlockSpec(memory_space=pltpu.VMEM)],
        out_shape=[pltpu.SemaphoreType.DMA((n_copies,)), pltpu.SemaphoreType.DMA((n_copies,))]
        + [pltpu.HBM(b.shape, b.dtype) for b in bufs] + [jax.ShapeDtypeStruct((8, 128), F32)],
        input_output_aliases={i: 2 + i for i in range(nb)},
        compiler_params=pltpu.CompilerParams(has_side_effects=DATAFLOW),
    )(*[_in_hbm(b) for b in bufs], after)
    return res[0], res[1], list(res[2:2 + nb]), res[-1]


def _exchange_wait(bufs, send_sems, recv_sems, after, plan, name):
    nb = len(bufs)

    def body(*refs):
        sends, recvs = refs[nb], refs[nb + 1]
        for i, (src, dst, dev, landing) in enumerate(plan(refs[:nb])):
            pltpu.make_async_remote_copy(
                src_ref=src, dst_ref=landing, send_sem=sends.at[i], recv_sem=recvs.at[i],
                device_id=dev, device_id_type=MESH).wait()

    res = pl.pallas_call(
        body, name=name,
        in_specs=[HBM_SPEC] * nb + [SEM_SPEC, SEM_SPEC, ANY_SPEC],
        out_specs=[HBM_SPEC] * nb,
        out_shape=[pltpu.HBM(b.shape, b.dtype) for b in bufs],
        input_output_aliases={i: i for i in range(nb)},
        compiler_params=pltpu.CompilerParams(has_side_effects=DATAFLOW),
    )(*bufs, send_sems, recv_sems, after)
    return list(res)


def _allgather_weights(slots, name, landed=False):
    n = len(slots)

    def body(*refs):
        outs = refs[n:2 * n]
        send_sems, recv_sems, fwd_send, fwd_recv = refs[2 * n:]
        x, y, c = _place()
        chips = _other_chips(x, y)

        def landing(a, chip, half_of):
            return outs[a].at[2 * chip[0] + chip[1], _half(outs[a], half_of)]

        def ici(a, k, chip_from, to):
            return pltpu.make_async_remote_copy(
                src_ref=landing(a, chip_from, c), dst_ref=landing(a, chip_from, c),
                send_sem=send_sems.at[a, k], recv_sem=recv_sems.at[a, k],
                device_id=to, device_id_type=MESH)

        def d2d(a, k, chip_from, half_of):
            return pltpu.make_async_remote_copy(
                src_ref=landing(a, chip_from, half_of), dst_ref=landing(a, chip_from, half_of),
                send_sem=fwd_send.at[a, k], recv_sem=fwd_recv.at[a, k],
                device_id=(x, y, 1 - c), device_id_type=MESH)

        sends = []
        if not landed:
            sends = [ici(a, k, (x, y), (*chips[k], c)) for a in range(n) for k in range(3)]
        for cp in sends:
            cp.start()
        forwards = []
        for a in range(n):
            for k in range(3):
                if not landed:
                    ici(a, k, chips[k], (x, y, c)).wait_recv()
                fw = d2d(a, k, chips[k], c)
                fw.start()
                forwards.append(fw)
        for a in range(n):
            for k in range(3):
                d2d(a, k, chips[k], 1 - c).wait_recv()
        for cp in sends + forwards:
            cp.wait_send()

    return pl.pallas_call(
        body, name=name,
        in_specs=[HBM_SPEC] * n, out_specs=[HBM_SPEC] * n,
        out_shape=[jax.ShapeDtypeStruct(s.shape, s.dtype) for s in slots],
        input_output_aliases={a: a for a in range(n)},
        scratch_shapes=[pltpu.SemaphoreType.DMA((n, 3)), pltpu.SemaphoreType.DMA((n, 3)),
                        pltpu.SemaphoreType.DMA((n, 3)), pltpu.SemaphoreType.DMA((n, 3))],
    )(*slots)


def _sibling_exchange(partials, small, name):
    n = len(partials)
    ns = 0 if small is None else 1

    def body(*refs):
        ins, outs = refs[:n], refs[n + ns:2 * n + ns]
        send_sems, recv_sems = refs[2 * (n + ns):2 * (n + ns) + 2]
        x, y, c = _place()
        me = 4 * x + 2 * y + c
        sends = [pltpu.make_async_remote_copy(
            src_ref=ins[a].at[:, _half(ins[a], 1 - c)], dst_ref=outs[a],
            send_sem=send_sems.at[a], recv_sem=recv_sems.at[a],
            device_id=(x, y, 1 - c), device_id_type=MESH) for a in range(n)]
        if ns:
            small_ref, small_all = refs[n], refs[2 * n + 1]
            s_send, s_recv, loc_sem = refs[2 * (n + ns) + 2:]
            local = pltpu.make_async_copy(small_ref, small_all.at[me], loc_sem)
            local.start()
            for d in range(1, N_DEV):
                px, py, pc = x ^ ((d >> 2) & 1), y ^ ((d >> 1) & 1), c ^ (d & 1)
                sends.append(pltpu.make_async_remote_copy(
                    src_ref=small_ref, dst_ref=small_all.at[me],
                    send_sem=s_send.at[d - 1], recv_sem=s_recv.at[d - 1],
                    device_id=(px, py, pc), device_id_type=MESH))
        for cp in sends:
            cp.start()
        if ns:
            for d in range(1, N_DEV):
                pltpu.make_async_remote_copy(
                    src_ref=small_ref, dst_ref=small_all.at[me ^ d],
                    send_sem=s_send.at[d - 1], recv_sem=s_recv.at[d - 1],
                    device_id=(x, y, c), device_id_type=MESH).wait_recv()
        for cp in sends[:n]:
            cp.wait_recv()
        for cp in sends:
            cp.wait_send()
        if ns:
            local.wait()

    out_shape = [jax.ShapeDtypeStruct((N_CHIPS, p.shape[1] // 2, p.shape[2]), F32) for p in partials]
    scratch = [pltpu.SemaphoreType.DMA((max(n, 1),)), pltpu.SemaphoreType.DMA((max(n, 1),))]
    if ns:
        out_shape.append(jax.ShapeDtypeStruct((N_DEV,) + small.shape, F32))
        scratch += [pltpu.SemaphoreType.DMA((N_DEV - 1,)), pltpu.SemaphoreType.DMA((N_DEV - 1,)),
                    pltpu.SemaphoreType.DMA]
    return pl.pallas_call(
        body, name=name,
        in_specs=[HBM_SPEC] * (n + ns), out_specs=[HBM_SPEC] * (n + ns),
        out_shape=out_shape, scratch_shapes=scratch,
    )(*partials, *([small] if ns else []))


def _chip_sum(partial, from_sibling, c, name, tr=256):
    _, hr, C = from_sibling.shape
    nb = hr // tr

    def body(c_ref, p_ref, s_ref, o_ref):
        o_ref[...] = (p_ref[...] + s_ref[...]).astype(BF16)

    return pl.pallas_call(
        body, name=name,
        grid_spec=pltpu.PrefetchScalarGridSpec(
            num_scalar_prefetch=1, grid=(N_CHIPS, nb),
            in_specs=[pl.BlockSpec((1, tr, C), lambda j, i, c_ref: (j, c_ref[0] * nb + i, 0)),
                      pl.BlockSpec((1, tr, C), lambda j, i, c_ref: (j, i, 0))],
            out_specs=pl.BlockSpec((1, tr, C), lambda j, i, c_ref: (j, i, 0))),
        out_shape=jax.ShapeDtypeStruct(from_sibling.shape, BF16),
        compiler_params=_params(2),
    )(c, partial, from_sibling)


def _reduce_half(partial, from_sibling, received, place, name, tr=256):
    _, hr, C = from_sibling.shape
    nb = hr // tr

    def body(p_ref, mine_ref, sib_ref, r_ref, o_ref):
        acc = mine_ref[0] + sib_ref[0]
        for k in range(3):
            acc = acc + r_ref[k].astype(F32)
        o_ref[...] = acc

    return pl.pallas_call(
        body, name=name,
        grid_spec=pltpu.PrefetchScalarGridSpec(
            num_scalar_prefetch=1, grid=(nb,),
            in_specs=[pl.BlockSpec((1, tr, C), lambda i, p: (p[0], p[1] * nb + i, 0)),
                      pl.BlockSpec((1, tr, C), lambda i, p: (p[0], i, 0)),
                      pl.BlockSpec((3, tr, C), lambda i, p: (0, i, 0))],
            out_specs=pl.BlockSpec((tr, C), lambda i, p: (p[1] * nb + i, 0))),
        out_shape=jax.ShapeDtypeStruct((2 * hr, C), F32),
        compiler_params=_params(1),
    )(place, partial, from_sibling, received)


def _join_halves(fulls, name):
    n = len(fulls)

    def body(*refs):
        outs = refs[n:2 * n]
        send_sems, recv_sems = refs[2 * n:]
        x, y, c = _place()

        def copy(a, half_of, to):
            rows = outs[a].at[_half(outs[a], half_of)]
            return pltpu.make_async_remote_copy(
                src_ref=rows, dst_ref=rows, send_sem=send_sems.at[a], recv_sem=recv_sems.at[a],
                device_id=to, device_id_type=MESH)

        sends = [copy(a, c, (x, y, 1 - c)) for a in range(n)]
        for cp in sends:
            cp.start()
        for a in range(n):
            copy(a, 1 - c, (x, y, c)).wait_recv()
        for cp in sends:
            cp.wait_send()

    return pl.pallas_call(
        body, name=name,
        in_specs=[HBM_SPEC] * n, out_specs=[HBM_SPEC] * n,
        out_shape=[jax.ShapeDtypeStruct(f.shape, F32) for f in fulls],
        input_output_aliases={a: a for a in range(n)},
        scratch_shapes=[pltpu.SemaphoreType.DMA((n,)), pltpu.SemaphoreType.DMA((n,))],
    )(*fulls)


def _adamw_math(w, g, m, v):
    m = ADAM_B1 * m + (1.0 - ADAM_B1) * g
    v = ADAM_B2 * v + (1.0 - ADAM_B2) * (g * g)
    m_hat = m / (1.0 - ADAM_B1 ** ADAM_STEP)
    v_hat = v / (1.0 - ADAM_B2 ** ADAM_STEP)
    delta = -ADAM_LR * (m_hat / (jnp.sqrt(v_hat) + ADAM_EPS) + ADAM_WD * w)
    return delta, m, v


def _adamw(w, g, m, v, name, tr=256):
    R, C = w.shape
    tr = min(tr, R)

    def body(w_ref, g_ref, m_ref, v_ref, d_out, m_out, v_out):
        d_out[...], m_out[...], v_out[...] = _adamw_math(w_ref[...], g_ref[...], m_ref[...], v_ref[...])

    spec = pl.BlockSpec((tr, C), lambda i: (i, 0))
    return pl.pallas_call(
        body, name=name, grid=(R // tr,),
        in_specs=[spec] * 4, out_specs=[spec] * 3,
        out_shape=[jax.ShapeDtypeStruct((R, C), F32)] * 3,
        compiler_params=_params(1),
    )(w, g, m, v)


def _adamw_small(small_all, w, m, v, name):
    def body(s_ref, w_ref, m_ref, v_ref, g_out, d_out, m_out, v_out):
        g = s_ref[0]
        for d in range(1, N_DEV):
            g = g + s_ref[d]
        g_out[...] = g
        d_out[...], m_out[...], v_out[...] = _adamw_math(w_ref[...], g, m_ref[...], v_ref[...])

    vm = pl.BlockSpec(memory_space=pltpu.VMEM)
    return pl.pallas_call(
        body, name=name, in_specs=[vm] * 4, out_specs=[vm] * 4,
        out_shape=[jax.ShapeDtypeStruct(w.shape, F32)] * 4,
    )(small_all, w, m, v)


def _pack_small(norm_g, pool_scale, norm_f, extra_row):
    return jnp.concatenate([norm_g.reshape(2, D_MODEL), pool_scale.reshape(2, D_MODEL),
                            norm_f.reshape(1, D_MODEL), extra_row,
                            jnp.zeros((2, D_MODEL), F32)], axis=0)


def kernel(x, norm_g, pool_w_in, pool_w, pool_scale, pool_w_out, sb_w_in, sb_w_out, norm_f, loss_target, m_norm_g, m_pool_w_in, m_pool_w, m_pool_scale, m_pool_w_out, m_sb_w_in, m_sb_w_out, m_norm_f, v_norm_g, v_pool_w_in, v_pool_w, v_pool_scale, v_pool_w_out, v_sb_w_in, v_sb_w_out, v_norm_f):
    nb, S, _ = x.shape
    T = nb * S
    xt = x.reshape(T, D_MODEL)
    target = loss_target.reshape(T, D_MODEL)
    cx, cy, cc = _place()

    def shard2d(w):
        return w.reshape(-1, w.shape[-1])

    names = ("pool_w_in", "pool_w", "pool_w_out", "sb_w_in", "sb_w_out")
    w_shards = [shard2d(w) for w in (pool_w_in, pool_w, pool_w_out, sb_w_in, sb_w_out)]
    m_shards = [shard2d(w) for w in (m_pool_w_in, m_pool_w, m_pool_w_out, m_sb_w_in, m_sb_w_out)]
    v_shards = [shard2d(w) for w in (v_pool_w_in, v_pool_w, v_pool_w_out, v_sb_w_in, v_sb_w_out)]

    chip = (2 * cx + cy).reshape(1).astype(jnp.int32)
    c_arr = cc.reshape(1).astype(jnp.int32)
    place = jnp.stack([2 * cx + cy, cc]).astype(jnp.int32)
    slots = [_cast_to_slot(w, chip, "cast_" + nm) for w, nm in zip(w_shards, names)]
    g0, g1, gf = norm_g[0:1], norm_g[1:2], norm_f.reshape(1, D_MODEL)

    w_pin, = _allgather_weights(slots[:1], "allgather_pool_in_weights")
    mix_send, mix_recv, mix_slots, token = _exchange_start(slots[1:3], w_pin, _weight_plan, 6,
                                                           "pool_weights_start")
    sb_send, sb_recv, sb_slots, token = _exchange_start(slots[3:], token, _weight_plan, 6,
                                                        "sb_weights_start")

    proj0, u0 = _rms_matmul(xt, g0 + token[0:1, 0:1], w_pin, [(2 * D_INNER, BF16)], "pool_in_proj",
                            tn=w_pin.shape[2])
    mix_slots = _exchange_wait(mix_slots, mix_send, mix_recv, proj0, _weight_plan, "pool_weights_wait")
    w_g, w_pout = _allgather_weights(mix_slots, "pool_weights_forward", landed=True)
    w_pout = w_pout.reshape(D_INNER, D_MODEL)
    y0, pooled, mixed = _pool_fwd(proj0, w_g, pool_scale, S, "pool_mix")
    sb_slots = _exchange_wait(sb_slots, sb_send, sb_recv, y0, _weight_plan, "sb_weights_wait")
    w_sin, w_sout = _allgather_weights(sb_slots, "sb_weights_forward", landed=True)
    w_sout = w_sout.reshape(D_INNER, D_MODEL)
    h1 = _matmul_residual(y0, w_pout, xt, "pool_out_proj")
    n1 = 2 * QK_WIDTH + 2 * D_INNER
    qkvz, u1 = _rms_matmul(h1, g1, w_sin, [(n1, BF16)], "sb_in_proj", tn=w_sin.shape[2])
    o, y1, ltot = _attn_fwd(qkvz, S, "sb_attention")
    dh2, d_norm_f, loss_row = _out_proj_loss_head(y1, w_sout, h1, gf, target, "sb_out_proj_loss_head")

    def reduce_start(partials, tag):
        from_sibling = _sibling_exchange(partials, None, "grad_sibling_exchange_" + tag)
        sums = [_chip_sum(p, s, c_arr, "grad_chip_sum_%s_%d" % (tag, i))
                for i, (p, s) in enumerate(zip(partials, from_sibling))]
        lands = [lax.empty((3,) + s.shape[1:], BF16) for s in sums]
        send, recv, bufs, token = _exchange_start(sums + lands, c_arr, _chip_sum_plan, 3 * len(sums),
                                                  "grad_chip_exchange_start_" + tag)
        return (partials, list(from_sibling), send, recv, bufs), token[0:1, 0:1]

    def reduce_finish(started, after, tag):
        partials, from_sibling, send, recv, bufs = started
        received = _exchange_wait(bufs, send, recv, after, _chip_sum_plan,
                                  "grad_chip_exchange_wait_" + tag)[len(partials):]
        return [_reduce_half(p, s, r, place, "grad_reduce_%s_%d" % (tag, i))
                for i, (p, s, r) in enumerate(zip(partials, from_sibling, received))]

    shard = lambda i, j, t: (j, 0, 0)
    gw_sout = _matmul_tn(y1, dh2, D_INNER, D_MODEL, (D_INNER, D_MODEL), (1024, 1024),
                         lambda i, j, t: (i, j), "grad_sb_w_out", bm=1024, bn=1024)
    sout_started, token = reduce_start([gw_sout.reshape(N_CHIPS, -1, D_MODEL)], "sb_out")
    do, dproj1 = _attn_gate_bwd(dh2, w_sout, qkvz, o, token, "sb_gate_bwd")
    dproj1 = _attn_bwd(qkvz, do, ltot, dproj1, S, "sb_attention_bwd")
    gw_sin = _matmul_tn(u1, dproj1, D_MODEL, n1, (N_CHIPS, D_MODEL, n1 // 4), (1, D_MODEL, n1 // 4),
                        shard, "grad_sb_w_in", bm=D_MODEL, bn=n1 // 4)
    sin_started, token = reduce_start([gw_sin], "sb_in")
    dh1, d_g1 = _matmul_nt_rms_bwd(dproj1, w_sin, h1, g1 + token, dh2, "sb_in_bwd", tk=w_sin.shape[2])
    gw_pout = _matmul_tn(y0, dh1, D_INNER, D_MODEL, (D_INNER, D_MODEL), (1024, 1024),
                         lambda i, j, t: (i, j), "grad_pool_w_out", bm=1024, bn=1024)
    dmixed, dproj0, d_scale = _pool_gate_bwd(dh1, w_pout, proj0, mixed, pool_scale, "pool_gate_bwd")
    gw_g = _matmul_tn(pooled, dmixed, D_INNER, D_INNER, (N_CHIPS, GROUP_DIM, GROUP_DIM),
                      (N_CHIPS, GROUP_DIM // N_CHIPS, GROUP_DIM), lambda i, j, t: (0, i, 0),
                      "grad_pool_w", bm=GROUP_DIM, bn=GROUP_DIM, diagonal_blocks=True)
    mix_started, token = reduce_start([gw_g, gw_pout.reshape(N_CHIPS, -1, D_MODEL)], "pool_mix")
    dproj0 = _pool_bwd(dmixed, w_g, dproj0, token, S, "pool_bwd")
    n0 = 2 * D_INNER
    gw_pin = _matmul_tn(u0, dproj0, D_MODEL, n0, (N_CHIPS, D_MODEL, n0 // 4), (1, D_MODEL, n0 // 4),
                        shard, "grad_pool_w_in", bm=D_MODEL, bn=n0 // 4)
    pin_started, token = reduce_start([gw_pin], "pool_in")
    dx, d_g0 = _matmul_nt_rms_bwd(dproj0, w_pin, xt, g0 + token, dh1, "pool_in_bwd", tk=w_pin.shape[2])

    small = _pack_small(jnp.concatenate([d_g0, d_g1], axis=0), d_scale, d_norm_f,
                        jnp.broadcast_to(loss_row[:, :1], (1, D_MODEL)))
    small_all, = _sibling_exchange([], small, "small_sums_exchange")
    grads = _join_halves(reduce_finish(pin_started, dx, "pool_in")
                         + reduce_finish(mix_started, dx, "pool_mix")
                         + reduce_finish(sin_started, dx, "sb_in")
                         + reduce_finish(sout_started, dx, "sb_out"), "grad_join_halves")

    deltas, new_m, new_v = [], [], []
    for w, g, m, v, nm in zip(w_shards, grads, m_shards, v_shards, names):
        d, mm, vv = _adamw(w, g, m, v, "adamw_" + nm)
        deltas.append(d)
        new_m.append(mm)
        new_v.append(vv)

    zero_row = jnp.zeros((1, D_MODEL), F32)
    g_small, d_small, m_small, v_small = _adamw_small(
        small_all, _pack_small(norm_g, pool_scale, norm_f, zero_row),
        _pack_small(m_norm_g, m_pool_scale, m_norm_f, zero_row),
        _pack_small(v_norm_g, v_pool_scale, v_norm_f, zero_row + 1.0), "adamw_small")
    loss = g_small[5, 0]

    def unpack_small(a):
        return a[0:2], a[2:4].reshape(1, D_INNER), a[4]

    def assemble(big, small3):
        ng, ps, nf = small3
        return [ng, big[0].reshape(pool_w_in.shape), big[1].reshape(pool_w.shape), ps,
                big[2].reshape(pool_w_out.shape), big[3].reshape(sb_w_in.shape),
                big[4].reshape(sb_w_out.shape), nf]

    return (loss, dx.reshape(x.shape),
            *assemble(grads, unpack_small(g_small)),
            *assemble(deltas, unpack_small(d_small)),
            *assemble(new_m, unpack_small(m_small)),
            *assemble(new_v, unpack_small(v_small)))
```

```python
import jax
import jax.numpy as jnp
from jax import lax
from jax.experimental import pallas as pl
from jax.experimental.pallas import tpu as pltpu

F32 = jnp.float32
BF16 = jnp.bfloat16
MESH = pl.DeviceIdType.MESH

D_MODEL = 1024
D_INNER = 2048
N_GROUPS = 4
GROUP_DIM = 512
HEAD_PAIR_QK = 128
HEAD_V = 128
QK_WIDTH = 1024
RMS_EPS = 1e-6
HALO = 16
N_CHIPS = 4
N_DEV = 8

ADAM_LR = 0.001
ADAM_B1 = 0.9
ADAM_B2 = 0.999
ADAM_EPS = 1e-08
ADAM_WD = 0.01
ADAM_STEP = 10

VMEM_LIMIT = 56 * 1024 * 1024

HBM_SPEC = pl.BlockSpec(memory_space=pltpu.HBM)


def _params(n_axes):
    return pltpu.CompilerParams(dimension_semantics=("arbitrary",) * n_axes,
                                vmem_limit_bytes=VMEM_LIMIT)


def _dot(a, b):
    return jnp.dot(a, b, preferred_element_type=F32)


def _dot_nt(a, b):
    return lax.dot_general(a, b, (((1,), (1,)), ((), ())), preferred_element_type=F32)


def _dot_tn(a, b):
    return lax.dot_general(a, b, (((0,), (0,)), ((), ())), preferred_element_type=F32)


def _sigmoid(z):
    return 1.0 / (1.0 + jnp.exp(-z))


def _row_blocks(tm, rows=256):
    return [slice(r, r + rows) for r in range(0, tm, rows)]


def _rms_matmul(h, g_row, w4, outs, name, tm=1024, tn=512):
    T = h.shape[0]
    per_shard = w4.shape[2] // tn
    starts = [0]
    for width, _ in outs:
        starts.append(starts[-1] + width // tn)
    n_out = len(outs)

    def body(h_ref, g_ref, w_ref, *rest):
        o_refs, u_out, u_s = rest[:n_out], rest[n_out], rest[n_out + 1]
        n = pl.program_id(1)

        @pl.when(n == 0)
        def _():
            x = h_ref[...]
            inv = lax.rsqrt(jnp.mean(x * x, axis=-1, keepdims=True) + RMS_EPS)
            u = (x * inv * g_ref[...]).astype(BF16)
            u_s[...] = u
            u_out[...] = u

        res = _dot(u_s[...], w_ref[0])
        for k in range(n_out):
            @pl.when((n >= starts[k]) & (n < starts[k + 1]))
            def _():
                o_refs[k][...] = res.astype(o_refs[k].dtype)

    def out_map(k):
        return lambda m, n: (m, jnp.clip(n - starts[k], 0, starts[k + 1] - starts[k] - 1))

    return pl.pallas_call(
        body, name=name, grid=(T // tm, starts[-1]),
        in_specs=[pl.BlockSpec((tm, D_MODEL), lambda m, n: (m, 0)),
                  pl.BlockSpec((1, D_MODEL), lambda m, n: (0, 0)),
                  pl.BlockSpec((1, D_MODEL, tn), lambda m, n: (n // per_shard, 0, n % per_shard))],
        out_specs=[pl.BlockSpec((tm, tn), out_map(k)) for k in range(n_out)]
        + [pl.BlockSpec((tm, D_MODEL), lambda m, n: (m, 0))],
        out_shape=[jax.ShapeDtypeStruct((T, width), dt) for width, dt in outs]
        + [jax.ShapeDtypeStruct((T, D_MODEL), BF16)],
        scratch_shapes=[pltpu.VMEM((tm, D_MODEL), BF16)],
        compiler_params=_params(2),
    )(h, g_row, w4)


def _matmul_residual(a, w, res, name, tm=1024, tn=512):
    T, K = a.shape
    N = w.shape[1]

    def body(a_ref, w_ref, r_ref, o_ref):
        o_ref[...] = r_ref[...] + _dot(a_ref[...], w_ref[...])

    return pl.pallas_call(
        body, name=name, grid=(T // tm, N // tn),
        in_specs=[pl.BlockSpec((tm, K), lambda m, n: (m, 0)),
                  pl.BlockSpec((K, tn), lambda m, n: (0, n)),
                  pl.BlockSpec((tm, tn), lambda m, n: (m, n))],
        out_specs=pl.BlockSpec((tm, tn), lambda m, n: (m, n)),
        out_shape=jax.ShapeDtypeStruct((T, N), F32),
        compiler_params=_params(2),
    )(a, w, res)


def _matmul_tn(a, b, a_cols, b_cols, out_shape, out_block, out_map, name, bm, bn, tk=1024,
               diagonal_blocks=False):
    T = a.shape[0]

    def body(a_ref, b_ref, o_ref):
        @pl.when(pl.program_id(2) == 0)
        def _():
            o_ref[...] = jnp.zeros_like(o_ref)

        part = _dot_tn(a_ref[...].astype(BF16), b_ref[...].astype(BF16))
        o_ref[...] += part.reshape(o_ref.shape)

    b_map = (lambda i, j, t: (t, i)) if diagonal_blocks else (lambda i, j, t: (t, j))
    return pl.pallas_call(
        body, name=name, grid=(a_cols // bm, 1 if diagonal_blocks else b_cols // bn, T // tk),
        in_specs=[pl.BlockSpec((tk, bm), lambda i, j, t: (t, i)),
                  pl.BlockSpec((tk, bn), b_map)],
        out_specs=pl.BlockSpec(out_block, out_map),
        out_shape=jax.ShapeDtypeStruct(out_shape, F32),
        compiler_params=_params(3),
    )(a, b)


def _matmul_nt_rms_bwd(dproj, w4, h, g_row, dres, name, tm=1024, tk=512):
    T, cols = dproj.shape
    per_shard = w4.shape[2] // tk
    nk = cols // tk

    def body(dp_ref, w_ref, h_ref, g_ref, r_ref, dx_ref, dg_ref, acc):
        m, k = pl.program_id(0), pl.program_id(1)

        @pl.when(k == 0)
        def _():
            acc[...] = jnp.zeros_like(acc)

        @pl.when((k == 0) & (m == 0))
        def _():
            dg_ref[...] = jnp.zeros_like(dg_ref)

        acc[...] += _dot_nt(dp_ref[...], w_ref[0])

        @pl.when(k == nk - 1)
        def _():
            du = acc[...]
            x = h_ref[...]
            inv = lax.rsqrt(jnp.mean(x * x, axis=-1, keepdims=True) + RMS_EPS)
            xhat = x * inv
            dg_ref[...] += jnp.sum(du * xhat, axis=0, keepdims=True)
            dxh = du * g_ref[...]
            proj = jnp.mean(dxh * xhat, axis=-1, keepdims=True)
            dx_ref[...] = r_ref[...] + inv * (dxh - xhat * proj)

    return pl.pallas_call(
        body, name=name, grid=(T // tm, nk),
        in_specs=[pl.BlockSpec((tm, tk), lambda m, k: (m, k)),
                  pl.BlockSpec((1, D_MODEL, tk), lambda m, k: (k // per_shard, 0, k % per_shard)),
                  pl.BlockSpec((tm, D_MODEL), lambda m, k: (m, 0)),
                  pl.BlockSpec((1, D_MODEL), lambda m, k: (0, 0)),
                  pl.BlockSpec((tm, D_MODEL), lambda m, k: (m, 0))],
        out_specs=[pl.BlockSpec((tm, D_MODEL), lambda m, k: (m, 0)),
                   pl.BlockSpec((1, D_MODEL), lambda m, k: (0, 0))],
        out_shape=[jax.ShapeDtypeStruct((T, D_MODEL), F32),
                   jax.ShapeDtypeStruct((1, D_MODEL), F32)],
        scratch_shapes=[pltpu.VMEM((tm, D_MODEL), F32)],
        compiler_params=_params(2),
    )(dproj, w4, h, g_row, dres)


def _window_of(g):
    return jnp.left_shift(2, g)


def _select_stage(g, stages):
    res = stages[0]
    for i in range(1, len(stages)):
        res = jnp.where(g >= i, stages[i], res)
    return res


def _pool_fwd(proj0, wg4, scale_row, S, name, tm=1024):
    T = proj0.shape[0]
    tm = min(tm, S)
    blocks_per_seq = S // tm
    hb = tm // HALO

    def body(x_ref, halo_ref, z_ref, w_ref, s_ref, y_ref, p_ref, mix_ref):
        m, g = pl.program_id(0), pl.program_id(1)
        first = (m % blocks_per_seq) == 0
        halo = jnp.where(first, 0.0, halo_ref[...].astype(F32))
        x = x_ref[...].astype(F32)
        ext = jnp.concatenate([halo, x], axis=0)
        stages = []
        cur = ext
        for sh in (1, 2, 4, 8):
            cur = cur + pltpu.roll(cur, sh, 0)
            stages.append(cur[HALO:, :])
        win_sum = _select_stage(g, stages)
        pos = (m % blocks_per_seq) * tm + lax.broadcasted_iota(jnp.int32, (tm, 1), 0)
        count = jnp.minimum(pos + 1, _window_of(g)).astype(F32)
        p_ref[...] = (win_sum / count - x).astype(BF16)
        w = w_ref[...].reshape(GROUP_DIM, GROUP_DIM)
        for rows in _row_blocks(tm):
            mixed = _dot(p_ref[rows, :], w)
            z = z_ref[rows, :].astype(F32)
            y_ref[rows, :] = (mixed * s_ref[...] * (z * _sigmoid(z))).astype(BF16)
            mix_ref[rows, :] = mixed.astype(BF16)

    blk = lambda m, g: (m, g)
    return pl.pallas_call(
        body, name=name, grid=(T // tm, N_GROUPS),
        in_specs=[pl.BlockSpec((tm, GROUP_DIM), blk),
                  pl.BlockSpec((HALO, GROUP_DIM), lambda m, g: (jnp.maximum(m * hb - 1, 0), g)),
                  pl.BlockSpec((tm, GROUP_DIM), lambda m, g: (m, N_GROUPS + g)),
                  pl.BlockSpec((N_CHIPS, GROUP_DIM // N_CHIPS, GROUP_DIM), lambda m, g: (0, g, 0)),
                  pl.BlockSpec((1, GROUP_DIM), lambda m, g: (0, g))],
        out_specs=[pl.BlockSpec((tm, GROUP_DIM), blk)] * 3,
        out_shape=[jax.ShapeDtypeStruct((T, D_INNER), BF16)] * 3,
        compiler_params=_params(2),
    )(proj0, proj0, proj0, wg4, scale_row)


def _pool_gate_bwd(dh, w_out, proj0, mixed, scale_row, name, tm=1024, tn=512):
    T = dh.shape[0]
    gate_b0 = D_INNER // tn

    def body(dh_ref, w_ref, z_ref, mix_ref, s_ref, dm_ref, dz_ref, ds_ref, dh_s):
        m, n = pl.program_id(0), pl.program_id(1)

        @pl.when((m == 0) & (n == 0))
        def _():
            ds_ref[...] = jnp.zeros_like(ds_ref)

        @pl.when(n == 0)
        def _():
            dh_s[...] = dh_ref[...].astype(BF16)

        cols = pl.ds(pl.multiple_of(n * tn, tn), tn)
        s = s_ref[...]
        ds = ds_ref[:, cols]
        for rows in _row_blocks(tm):
            dy = _dot_nt(dh_s[rows, :], w_ref[...])
            z = z_ref[rows, :].astype(F32)
            sig = _sigmoid(z)
            silu = z * sig
            mixed = mix_ref[rows, :].astype(F32)
            dm_ref[rows, :] = (dy * s * silu).astype(BF16)
            dz_ref[rows, :] = (dy * mixed * s * (sig * (1.0 + z * (1.0 - sig)))).astype(BF16)
            ds = ds + jnp.sum(dy * mixed * silu, axis=0, keepdims=True)
        ds_ref[:, cols] = ds

    return pl.pallas_call(
        body, name=name, grid=(T // tm, D_INNER // tn),
        in_specs=[pl.BlockSpec((tm, D_MODEL), lambda m, n: (m, 0)),
                  pl.BlockSpec((tn, D_MODEL), lambda m, n: (n, 0)),
                  pl.BlockSpec((tm, tn), lambda m, n: (m, gate_b0 + n)),
                  pl.BlockSpec((tm, tn), lambda m, n: (m, n)),
                  pl.BlockSpec((1, tn), lambda m, n: (0, n))],
        out_specs=[pl.BlockSpec((tm, tn), lambda m, n: (m, n)),
                   pl.BlockSpec((tm, tn), lambda m, n: (m, gate_b0 + n)),
                   pl.BlockSpec((1, D_INNER), lambda m, n: (0, 0))],
        out_shape=[jax.ShapeDtypeStruct((T, D_INNER), BF16),
                   jax.ShapeDtypeStruct((T, 2 * D_INNER), BF16),
                   jax.ShapeDtypeStruct((1, D_INNER), F32)],
        scratch_shapes=[pltpu.VMEM((tm, D_MODEL), BF16)],
        compiler_params=_params(2),
    )(dh, w_out, proj0, mixed, scale_row)


def _pool_bwd(dmixed, wg4, dproj0, after, S, name, tm=1024):
    T = dmixed.shape[0]
    tm = min(tm, S)
    blocks_per_seq = S // tm
    hb = tm // HALO
    n_halo_blocks = T // HALO

    def body(dm_ref, halo_ref, w_ref, _, __, o_ref):
        m, g = pl.program_id(0), pl.program_id(1)
        ext = jnp.concatenate([dm_ref[...], halo_ref[...]], axis=0)
        dp = _dot_nt(ext, w_ref[...].reshape(GROUP_DIM, GROUP_DIM))
        pos = (m % blocks_per_seq) * tm + lax.broadcasted_iota(jnp.int32, (tm + HALO, 1), 0)
        count = jnp.minimum(pos + 1, _window_of(g)).astype(F32)
        c = jnp.where(pos < S, dp / count, 0.0)
        n = tm + HALO
        stages = []
        cur = c
        for sh in (1, 2, 4, 8):
            cur = cur + pltpu.roll(cur, n - sh, 0)
            stages.append(cur[:tm, :])
        o_ref[...] = (_select_stage(g, stages) - dp[:tm, :]).astype(BF16)

    blk = lambda m, g: (m, g)
    return pl.pallas_call(
        body, name=name, grid=(T // tm, N_GROUPS),
        in_specs=[pl.BlockSpec((tm, GROUP_DIM), blk),
                  pl.BlockSpec((HALO, GROUP_DIM),
                               lambda m, g: (jnp.minimum((m + 1) * hb, n_halo_blocks - 1), g)),
                  pl.BlockSpec((N_CHIPS, GROUP_DIM // N_CHIPS, GROUP_DIM), lambda m, g: (0, g, 0)),
                  HBM_SPEC, ANY_SPEC],
        out_specs=pl.BlockSpec((tm, GROUP_DIM), blk),
        out_shape=jax.ShapeDtypeStruct(dproj0.shape, dproj0.dtype),
        input_output_aliases={3: 0},
        compiler_params=_params(2),
    )(dmixed, dmixed, wg4, dproj0, after)


TQ = 256


def _split_dot(x, m):
    hi = x.astype(BF16)
    lo = (x - hi.astype(F32)).astype(BF16)
    return _dot(hi, m) + _dot(lo, m)


NEG_LOG2E = -1.4426950408889634


def _log_terms(z):
    soft = jnp.log(1.0 + jnp.exp2(jnp.abs(z) * NEG_LOG2E))
    log_beta = jnp.minimum(z, 0.0) - soft
    return log_beta, log_beta - z


N_HEADS = 16
FWD_HEADS = BWD_HEADS = 4


def _masked_heads(x, heads):
    lane = lax.broadcasted_iota(jnp.int32, (1, HEAD_PAIR_QK), 1)
    out = []
    for hh in range(heads):
        slab = x[:, (hh // 2) * HEAD_PAIR_QK:(hh // 2 + 1) * HEAD_PAIR_QK]
        out.append(jnp.where((lane // 64) == hh % 2, slab, jnp.zeros_like(slab)))
    return out


def _attn_fwd(qkvz, S, name):
    T = qkvz.shape[0]
    nq = S // TQ
    HEADS, QK_W, V_W = FWD_HEADS, FWD_HEADS * 64, FWD_HEADS * HEAD_V
    k_b0 = QK_WIDTH // QK_W
    v_b0 = 2 * QK_WIDTH // V_W
    z_b0 = (2 * QK_WIDTH + D_INNER) // V_W
    hs = range(HEADS)

    def body(q_ref, k_ref, v_ref, z_ref, o_ref, y_ref, lt_ref):
        row = lax.broadcasted_iota(jnp.int32, (TQ, TQ), 0)
        col = lax.broadcasted_iota(jnp.int32, (TQ, TQ), 1)
        causal = col < row
        later_in_block = (row > col).astype(BF16)
        lax.fori_loop(0, nq, lambda qi, _: q_block(qi, causal, later_in_block,
                                                   q_ref, k_ref, v_ref, z_ref, o_ref, y_ref, lt_ref), 0)

    def q_block(qi, causal, later_in_block, q_ref, k_ref, v_ref, z_ref, o_ref, y_ref, lt_ref):
        rows = pl.ds(pl.multiple_of(qi * TQ, TQ), TQ)
        qms = [qm * 0.125 for qm in _masked_heads(q_ref[rows, :], HEADS)]

        def step(j, carry, diagonal):
            koff = pl.multiple_of(j * TQ, TQ)
            kbs = [k_ref[pl.ds(koff, TQ), p * HEAD_PAIR_QK:(p + 1) * HEAD_PAIR_QK]
                   for p in range(HEADS // 2)]
            run, acc = [carry[2 * hh] for hh in hs], [carry[2 * hh + 1] for hh in hs]
            z = [_dot_nt(qms[hh], kbs[hh // 2]) for hh in hs]
            terms = [_log_terms(z[hh]) for hh in hs]
            log_om = [jnp.where(causal, t[1], 0.0) if diagonal else t[1] for t in terms]
            later = [_split_dot(log_om[hh], later_in_block) for hh in hs]
            a = [jnp.exp(terms[hh][0] + (run[hh] + later[hh])) for hh in hs]
            if diagonal:
                a = [jnp.where(causal, a[hh], 0.0) for hh in hs]
            out = []
            for hh in hs:
                vb = v_ref[pl.ds(koff, TQ), hh * HEAD_V:(hh + 1) * HEAD_V]
                out += [run[hh] + jnp.sum(log_om[hh], axis=1, keepdims=True),
                        acc[hh] + _dot(a[hh].astype(BF16), vb)]
            return tuple(out)

        zero = (jnp.zeros((TQ, 1), F32), jnp.zeros((TQ, HEAD_V), F32))
        carry = step(qi, zero * HEADS, True)
        carry = lax.fori_loop(0, qi, lambda i, c: step(qi - 1 - i, c, False), carry)
        for hh in hs:
            sl = slice(hh * HEAD_V, (hh + 1) * HEAD_V)
            acc = carry[2 * hh + 1]
            z = z_ref[rows, sl].astype(F32)
            o_ref[rows, sl] = acc.astype(BF16)
            y_ref[rows, sl] = (acc * (z * _sigmoid(z))).astype(BF16)
            lt_ref[rows, hh:hh + 1] = carry[2 * hh]
        return 0

    blk = lambda b, p: (b, p)
    return pl.pallas_call(
        body, name=name, grid=(T // S, N_HEADS // HEADS),
        in_specs=[pl.BlockSpec((S, QK_W), blk),
                  pl.BlockSpec((S, QK_W), lambda b, p: (b, k_b0 + p)),
                  pl.BlockSpec((S, V_W), lambda b, p: (b, v_b0 + p)),
                  pl.BlockSpec((S, V_W), lambda b, p: (b, z_b0 + p))],
        out_specs=[pl.BlockSpec((S, V_W), blk),
                   pl.BlockSpec((S, V_W), blk),
                   pl.BlockSpec((None, S, HEADS), lambda b, p: (p, b, 0))],
        out_shape=[jax.ShapeDtypeStruct((T, D_INNER), BF16),
                   jax.ShapeDtypeStruct((T, D_INNER), BF16),
                   jax.ShapeDtypeStruct((N_HEADS // HEADS, T, HEADS), F32)],
        compiler_params=_params(2),
    )(qkvz, qkvz, qkvz, qkvz)


def _attn_gate_bwd(dh, w_out, qkvz, o, after, name, tm=1024, tn=512):
    T = dh.shape[0]
    gate_b0 = (2 * QK_WIDTH + D_INNER) // tn

    def body(dh_ref, w_ref, z_ref, o_ref, _, do_ref, dz_ref, dh_s):
        @pl.when(pl.program_id(1) == 0)
        def _():
            dh_s[...] = dh_ref[...].astype(BF16)

        for rows in _row_blocks(tm):
            dy = _dot_nt(dh_s[rows, :], w_ref[...])
            z = z_ref[rows, :].astype(F32)
            sig = _sigmoid(z)
            do_ref[rows, :] = (dy * (z * sig)).astype(BF16)
            dz_ref[rows, :] = (dy * o_ref[rows, :].astype(F32)
                               * (sig * (1.0 + z * (1.0 - sig)))).astype(BF16)

    return pl.pallas_call(
        body, name=name, grid=(T // tm, D_INNER // tn),
        in_specs=[pl.BlockSpec((tm, D_MODEL), lambda m, n: (m, 0)),
                  pl.BlockSpec((tn, D_MODEL), lambda m, n: (n, 0)),
                  pl.BlockSpec((tm, tn), lambda m, n: (m, gate_b0 + n)),
                  pl.BlockSpec((tm, tn), lambda m, n: (m, n)),
                  ANY_SPEC],
        out_specs=[pl.BlockSpec((tm, tn), lambda m, n: (m, n)),
                   pl.BlockSpec((tm, tn), lambda m, n: (m, gate_b0 + n))],
        out_shape=[jax.ShapeDtypeStruct((T, D_INNER), BF16),
                   jax.ShapeDtypeStruct((T, 2 * QK_WIDTH + 2 * D_INNER), BF16)],
        scratch_shapes=[pltpu.VMEM((tm, D_MODEL), BF16)],
        compiler_params=_params(2),
    )(dh, w_out, qkvz, o, after)


def _attn_bwd(qkv, do, ltot, dproj1, S, name):
    T = qkv.shape[0]
    nq = S // TQ
    HEADS, QK_W, V_W = BWD_HEADS, BWD_HEADS * 64, BWD_HEADS * HEAD_V
    k_b0 = QK_WIDTH // QK_W
    v_b0 = 2 * QK_WIDTH // V_W
    hs = range(HEADS)
    pairs = range(HEADS // 2)

    def body(q_ref, k_ref, v_ref, do_ref, lt_ref, _, out_ref, dq_s, dk_s, dv_s, dkb_s, dvb_s, sems):
        b, p = pl.program_id(0), pl.program_id(1)
        row = lax.broadcasted_iota(jnp.int32, (TQ, TQ), 0)
        col = lax.broadcasted_iota(jnp.int32, (TQ, TQ), 1)
        causal = col < row
        upto = (row <= col).astype(BF16)
        before = (row < col).astype(BF16)
        dk_s[...] = jnp.zeros_like(dk_s)
        dv_s[...] = jnp.zeros_like(dv_s)

        def q_block(qi, _):
            qoff = pl.multiple_of(qi * TQ, TQ)
            qms = [qm * 0.125 for qm in _masked_heads(q_ref[pl.ds(qoff, TQ), :], HEADS)]
            vsl = [slice(hh * HEAD_V, (hh + 1) * HEAD_V) for hh in hs]
            psl = [slice(pp * HEAD_PAIR_QK, (pp + 1) * HEAD_PAIR_QK) for pp in pairs]
            do_h = [do_ref[pl.ds(qoff, TQ), sl] for sl in vsl]
            total = [lt_ref[pl.ds(qoff, TQ), hh:hh + 1] for hh in hs]

            def k_block(j, carry, diagonal):
                koff = pl.multiple_of(j * TQ, TQ)
                kms = _masked_heads(k_ref[pl.ds(koff, TQ), :], HEADS)
                g_before = [carry[2 * hh] for hh in hs]
                lom_before = [carry[2 * hh + 1] for hh in hs]
                z = [_dot_nt(qms[hh], kms[hh]) for hh in hs]
                da = [_dot_nt(do_h[hh], v_ref[pl.ds(koff, TQ), vsl[hh]]) for hh in hs]
                terms = [_log_terms(z[hh]) for hh in hs]
                log_om = [jnp.where(causal, t[1], 0.0) if diagonal else t[1] for t in terms]
                prefix = [_split_dot(log_om[hh], upto) for hh in hs]
                a = [jnp.exp(terms[hh][0] + ((total[hh] - lom_before[hh]) - prefix[hh])) for hh in hs]
                if diagonal:
                    a = [jnp.where(causal, a[hh], 0.0) for hh in hs]
                g = [a[hh] * da[hh] for hh in hs]
                g_prefix = [_dot(g[hh].astype(BF16), before) for hh in hs]
                out, dzs = [], []
                for hh in hs:
                    beta = jnp.exp(terms[hh][0])
                    g_excl = (g_before[hh] + g_prefix[hh]) * beta
                    if diagonal:
                        g_excl = jnp.where(causal, g_excl, 0.0)
                    dzs.append((g[hh] * (1.0 - beta) - g_excl).astype(BF16))
                    out += [g_before[hh] + jnp.sum(g[hh], axis=1, keepdims=True),
                            lom_before[hh] + jnp.sum(log_om[hh], axis=1, keepdims=True)]
                for hh in hs:
                    dv_s[pl.ds(koff, TQ), vsl[hh]] += _dot_tn(a[hh].astype(BF16), do_h[hh])
                dq = []
                for pp in pairs:
                    pair = slice(2 * pp, 2 * pp + 2)
                    dq.append(carry[2 * HEADS + pp] + _dot(jnp.concatenate(dzs[pair], axis=1),
                                                           jnp.concatenate(kms[pair], axis=0)))
                    dk_s[pl.ds(koff, TQ), psl[pp]] += _dot_tn(jnp.concatenate(dzs[pair], axis=0),
                                                              jnp.concatenate(qms[pair], axis=0))
                return tuple(out) + tuple(dq)

            zero = jnp.zeros((TQ, 1), F32)
            carry = (zero,) * (2 * HEADS) + (jnp.zeros((TQ, HEAD_PAIR_QK), F32),) * (HEADS // 2)
            carry = lax.fori_loop(0, qi, lambda j, c: k_block(j, c, False), carry)
            carry = k_block(qi, carry, True)
            for pp in pairs:
                dq_s[pl.ds(qoff, TQ), psl[pp]] = (carry[2 * HEADS + pp] * 0.125).astype(BF16)
            return 0

        lax.fori_loop(0, nq, q_block, 0)
        dkb_s[...] = dk_s[...].astype(BF16)
        dvb_s[...] = dv_s[...].astype(BF16)
        rows = pl.ds(pl.multiple_of(b * S, TQ), S)
        copies = [
            pltpu.make_async_copy(
                dq_s, out_ref.at[rows, pl.ds(pl.multiple_of(p * QK_W, 128), QK_W)], sems.at[0]),
            pltpu.make_async_copy(
                dkb_s, out_ref.at[rows, pl.ds(pl.multiple_of(QK_WIDTH + p * QK_W, 128), QK_W)],
                sems.at[1]),
            pltpu.make_async_copy(
                dvb_s, out_ref.at[rows, pl.ds(pl.multiple_of(2 * QK_WIDTH + p * V_W, 128), V_W)],
                sems.at[2]),
        ]
        for cp in copies:
            cp.start()
        for cp in copies:
            cp.wait()

    return pl.pallas_call(
        body, name=name, grid=(T // S, N_HEADS // HEADS),
        in_specs=[pl.BlockSpec((S, QK_W), lambda b, p: (b, p)),
                  pl.BlockSpec((S, QK_W), lambda b, p: (b, k_b0 + p)),
                  pl.BlockSpec((S, V_W), lambda b, p: (b, v_b0 + p)),
                  pl.BlockSpec((S, V_W), lambda b, p: (b, p)),
                  pl.BlockSpec((None, S, HEADS), lambda b, p: (p, b, 0)),
                  HBM_SPEC],
        out_specs=HBM_SPEC,
        out_shape=jax.ShapeDtypeStruct(dproj1.shape, dproj1.dtype),
        input_output_aliases={5: 0},
        scratch_shapes=[pltpu.VMEM((S, QK_W), BF16),
                        pltpu.VMEM((S, QK_W), F32),
                        pltpu.VMEM((S, V_W), F32),
                        pltpu.VMEM((S, QK_W), BF16),
                        pltpu.VMEM((S, V_W), BF16),
                        pltpu.SemaphoreType.DMA((3,))],
        compiler_params=_params(2),
    )(qkv, qkv, qkv, do, ltot, dproj1)


def _out_proj_loss_head(a, w, res, g_row, target, name, tm=512):
    T, K = a.shape

    def body(a_ref, w_ref, r_ref, g_ref, t_ref, dh_ref, dg_ref, loss_ref):
        @pl.when(pl.program_id(0) == 0)
        def _():
            dg_ref[...] = jnp.zeros_like(dg_ref)
            loss_ref[...] = jnp.zeros_like(loss_ref)

        gain = g_ref[...]
        dg, loss = dg_ref[...], loss_ref[...]
        for rows in _row_blocks(tm):
            x = r_ref[rows, :] + _dot(a_ref[rows, :], w_ref[...])
            inv = lax.rsqrt(jnp.mean(x * x, axis=-1, keepdims=True) + RMS_EPS)
            xhat = x * inv
            err = xhat * gain - t_ref[rows, :]
            per_token = jnp.mean(err * err, axis=-1, keepdims=True)
            loss = loss + 0.5 * jnp.sum(per_token, axis=0, keepdims=True)
            dy = err * (1.0 / D_MODEL)
            dg = dg + jnp.sum(dy * xhat, axis=0, keepdims=True)
            dxh = dy * gain
            proj = jnp.mean(dxh * xhat, axis=-1, keepdims=True)
            dh_ref[rows, :] = inv * (dxh - xhat * proj)
        dg_ref[...] = dg
        loss_ref[...] = loss

    return pl.pallas_call(
        body, name=name, grid=(T // tm,),
        in_specs=[pl.BlockSpec((tm, K), lambda m: (m, 0)),
                  pl.BlockSpec((K, D_MODEL), lambda m: (0, 0)),
                  pl.BlockSpec((tm, D_MODEL), lambda m: (m, 0)),
                  pl.BlockSpec((1, D_MODEL), lambda m: (0, 0)),
                  pl.BlockSpec((tm, D_MODEL), lambda m: (m, 0))],
        out_specs=[pl.BlockSpec((tm, D_MODEL), lambda m: (m, 0)),
                   pl.BlockSpec((1, D_MODEL), lambda m: (0, 0)),
                   pl.BlockSpec((1, 128), lambda m: (0, 0))],
        out_shape=[jax.ShapeDtypeStruct((T, D_MODEL), F32),
                   jax.ShapeDtypeStruct((1, D_MODEL), F32),
                   jax.ShapeDtypeStruct((1, 128), F32)],
        compiler_params=_params(1),
    )(a, w, res, g_row, target)


def _place():
    return lax.axis_index("x"), lax.axis_index("y"), lax.axis_index("c")


def _other_chips(x, y):
    return [(1 - x, y), (x, 1 - y), (1 - x, 1 - y)]


def _half(ref, c):
    hr = ref.shape[-2] // 2
    return pl.ds(pl.multiple_of(c * hr, 8), hr)


def _cast_to_slot(shard, chip, name, tr=256):
    R, C = shard.shape

    def body(chip_ref, w_ref, o_ref):
        o_ref[0] = w_ref[...].astype(BF16)

    return pl.pallas_call(
        body, name=name,
        grid_spec=pltpu.PrefetchScalarGridSpec(
            num_scalar_prefetch=1, grid=(R // tr,),
            in_specs=[pl.BlockSpec((tr, C), lambda i, chip_ref: (i, 0))],
            out_specs=pl.BlockSpec((1, tr, C), lambda i, chip_ref: (chip_ref[0], i, 0))),
        out_shape=jax.ShapeDtypeStruct((N_CHIPS, R, C), BF16),
        compiler_params=_params(1),
    )(chip, shard)


def _weight_plan(bufs):
    x, y, c = _place()
    plan = []
    for buf in bufs:
        mine = buf.at[2 * x + y, _half(buf, c)]
        for ox, oy in _other_chips(x, y):
            plan.append((mine, mine, (ox, oy, c), buf.at[2 * ox + oy, _half(buf, c)]))
    return plan


def _chip_sum_plan(bufs):
    x, y, c = _place()
    n = len(bufs) // 2
    plan = []
    for sums, land in zip(bufs[:n], bufs[n:]):
        for k, (ox, oy) in enumerate(_other_chips(x, y)):
            plan.append((sums.at[2 * ox + oy], land.at[k], (ox, oy, c), land.at[k]))
    return plan


SEM_SPEC = pl.BlockSpec(memory_space=pltpu.SEMAPHORE)
ANY_SPEC = pl.BlockSpec(memory_space=pl.ANY)
DATAFLOW = pltpu.SideEffectType.DATAFLOW_SIDE_EFFECTING


def _in_hbm(a):
    return pltpu.with_memory_space_constraint(a, pltpu.HBM)


def _exchange_start(bufs, after, plan, n_copies, name):
    nb = len(bufs)

    def body(*refs):
        send_sems, recv_sems = refs[nb + 1], refs[nb + 2]
        for i, (src, dst, dev, _) in enumerate(plan(refs[:nb])):
            pltpu.make_async_remote_copy(
                src_ref=src, dst_ref=dst, send_sem=send_sems.at[i], recv_sem=recv_sems.at[i],
                device_id=dev, device_id_type=MESH).start()
        token = refs[-1]
        token[...] = jnp.zeros_like(token)

    res = pl.pallas_call(
        body, name=name,
        in_specs=[HBM_SPEC] * nb + [ANY_SPEC],
        out_specs=[SEM_SPEC, SEM_SPEC] + [HBM_SPEC] * nb + [pl.BlockSpec(memory_space=pltpu.VMEM)],
        out_shape=[pltpu.SemaphoreType.DMA((n_copies,)), pltpu.SemaphoreType.DMA((n_copies,))]
        + [pltpu.HBM(b.shape, b.dtype) for b in bufs] + [jax.ShapeDtypeStruct((8, 128), F32)],
        input_output_aliases={i: 2 + i for i in range(nb)},
        compiler_params=pltpu.CompilerParams(has_side_effects=DATAFLOW),
    )(*[_in_hbm(b) for b in bufs], after)
    return res[0], res[1], list(res[2:2 + nb]), res[-1]


def _exchange_wait(bufs, send_sems, recv_sems, after, plan, name):
    nb = len(bufs)

    def body(*refs):
        sends, recvs = refs[nb], refs[nb + 1]
        for i, (src, dst, dev, landing) in enumerate(plan(refs[:nb])):
            pltpu.make_async_remote_copy(
                src_ref=src, dst_ref=landing, send_sem=sends.at[i], recv_sem=recvs.at[i],
                device_id=dev, device_id_type=MESH).wait()

    res = pl.pallas_call(
        body, name=name,
        in_specs=[HBM_SPEC] * nb + [SEM_SPEC, SEM_SPEC, ANY_SPEC],
        out_specs=[HBM_SPEC] * nb,
        out_shape=[pltpu.HBM(b.shape, b.dtype) for b in bufs],
        input_output_aliases={i: i for i in range(nb)},
        compiler_params=pltpu.CompilerParams(has_side_effects=DATAFLOW),
    )(*bufs, send_sems, recv_sems, after)
    return list(res)


def _allgather_weights(slots, name, landed=False):
    n = len(slots)

    def body(*refs):
        outs = refs[n:2 * n]
        send_sems, recv_sems, fwd_send, fwd_recv = refs[2 * n:]
        x, y, c = _place()
        chips = _other_chips(x, y)

        def landing(a, chip, half_of):
            return outs[a].at[2 * chip[0] + chip[1], _half(outs[a], half_of)]

        def ici(a, k, chip_from, to):
            return pltpu.make_async_remote_copy(
                src_ref=landing(a, chip_from, c), dst_ref=landing(a, chip_from, c),
                send_sem=send_sems.at[a, k], recv_sem=recv_sems.at[a, k],
                device_id=to, device_id_type=MESH)

        def d2d(a, k, chip_from, half_of):
            return pltpu.make_async_remote_copy(
                src_ref=landing(a, chip_from, half_of), dst_ref=landing(a, chip_from, half_of),
                send_sem=fwd_send.at[a, k], recv_sem=fwd_recv.at[a, k],
                device_id=(x, y, 1 - c), device_id_type=MESH)

        sends = []
        if not landed:
            sends = [ici(a, k, (x, y), (*chips[k], c)) for a in range(n) for k in range(3)]
        for cp in sends:
            cp.start()
        forwards = []
        for a in range(n):
            for k in range(3):
                if not landed:
                    ici(a, k, chips[k], (x, y, c)).wait_recv()
                fw = d2d(a, k, chips[k], c)
                fw.start()
                forwards.append(fw)
        for a in range(n):
            for k in range(3):
                d2d(a, k, chips[k], 1 - c).wait_recv()
        for cp in sends + forwards:
            cp.wait_send()

    return pl.pallas_call(
        body, name=name,
        in_specs=[HBM_SPEC] * n, out_specs=[HBM_SPEC] * n,
        out_shape=[jax.ShapeDtypeStruct(s.shape, s.dtype) for s in slots],
        input_output_aliases={a: a for a in range(n)},
        scratch_shapes=[pltpu.SemaphoreType.DMA((n, 3)), pltpu.SemaphoreType.DMA((n, 3)),
                        pltpu.SemaphoreType.DMA((n, 3)), pltpu.SemaphoreType.DMA((n, 3))],
    )(*slots)


def _sibling_exchange(partials, small, name):
    n = len(partials)
    ns = 0 if small is None else 1

    def body(*refs):
        ins, outs = refs[:n], refs[n + ns:2 * n + ns]
        send_sems, recv_sems = refs[2 * (n + ns):2 * (n + ns) + 2]
        x, y, c = _place()
        me = 4 * x + 2 * y + c
        sends = [pltpu.make_async_remote_copy(
            src_ref=ins[a].at[:, _half(ins[a], 1 - c)], dst_ref=outs[a],
            send_sem=send_sems.at[a], recv_sem=recv_sems.at[a],
            device_id=(x, y, 1 - c), device_id_type=MESH) for a in range(n)]
        if ns:
            small_ref, small_all = refs[n], refs[2 * n + 1]
            s_send, s_recv, loc_sem = refs[2 * (n + ns) + 2:]
            local = pltpu.make_async_copy(small_ref, small_all.at[me], loc_sem)
            local.start()
            for d in range(1, N_DEV):
                px, py, pc = x ^ ((d >> 2) & 1), y ^ ((d >> 1) & 1), c ^ (d & 1)
                sends.append(pltpu.make_async_remote_copy(
                    src_ref=small_ref, dst_ref=small_all.at[me],
                    send_sem=s_send.at[d - 1], recv_sem=s_recv.at[d - 1],
                    device_id=(px, py, pc), device_id_type=MESH))
        for cp in sends:
            cp.start()
        if ns:
            for d in range(1, N_DEV):
                pltpu.make_async_remote_copy(
                    src_ref=small_ref, dst_ref=small_all.at[me ^ d],
                    send_sem=s_send.at[d - 1], recv_sem=s_recv.at[d - 1],
                    device_id=(x, y, c), device_id_type=MESH).wait_recv()
        for cp in sends[:n]:
            cp.wait_recv()
        for cp in sends:
            cp.wait_send()
        if ns:
            local.wait()

    out_shape = [jax.ShapeDtypeStruct((N_CHIPS, p.shape[1] // 2, p.shape[2]), F32) for p in partials]
    scratch = [pltpu.SemaphoreType.DMA((max(n, 1),)), pltpu.SemaphoreType.DMA((max(n, 1),))]
    if ns:
        out_shape.append(jax.ShapeDtypeStruct((N_DEV,) + small.shape, F32))
        scratch += [pltpu.SemaphoreType.DMA((N_DEV - 1,)), pltpu.SemaphoreType.DMA((N_DEV - 1,)),
                    pltpu.SemaphoreType.DMA]
    return pl.pallas_call(
        body, name=name,
        in_specs=[HBM_SPEC] * (n + ns), out_specs=[HBM_SPEC] * (n + ns),
        out_shape=out_shape, scratch_shapes=scratch,
    )(*partials, *([small] if ns else []))


def _chip_sum(partial, from_sibling, c, name, tr=256):
    _, hr, C = from_sibling.shape
    nb = hr // tr

    def body(c_ref, p_ref, s_ref, o_ref):
        o_ref[...] = (p_ref[...] + s_ref[...]).astype(BF16)

    return pl.pallas_call(
        body, name=name,
        grid_spec=pltpu.PrefetchScalarGridSpec(
            num_scalar_prefetch=1, grid=(N_CHIPS, nb),
            in_specs=[pl.BlockSpec((1, tr, C), lambda j, i, c_ref: (j, c_ref[0] * nb + i, 0)),
                      pl.BlockSpec((1, tr, C), lambda j, i, c_ref: (j, i, 0))],
            out_specs=pl.BlockSpec((1, tr, C), lambda j, i, c_ref: (j, i, 0))),
        out_shape=jax.ShapeDtypeStruct(from_sibling.shape, BF16),
        compiler_params=_params(2),
    )(c, partial, from_sibling)


def _reduce_half(partial, from_sibling, received, place, name, tr=256):
    _, hr, C = from_sibling.shape
    nb = hr // tr

    def body(p_ref, mine_ref, sib_ref, r_ref, o_ref):
        acc = mine_ref[0] + sib_ref[0]
        for k in range(3):
            acc = acc + r_ref[k].astype(F32)
        o_ref[...] = acc

    return pl.pallas_call(
        body, name=name,
        grid_spec=pltpu.PrefetchScalarGridSpec(
            num_scalar_prefetch=1, grid=(nb,),
            in_specs=[pl.BlockSpec((1, tr, C), lambda i, p: (p[0], p[1] * nb + i, 0)),
                      pl.BlockSpec((1, tr, C), lambda i, p: (p[0], i, 0)),
                      pl.BlockSpec((3, tr, C), lambda i, p: (0, i, 0))],
            out_specs=pl.BlockSpec((tr, C), lambda i, p: (p[1] * nb + i, 0))),
        out_shape=jax.ShapeDtypeStruct((2 * hr, C), F32),
        compiler_params=_params(1),
    )(place, partial, from_sibling, received)


def _join_halves(fulls, name):
    n = len(fulls)

    def body(*refs):
        outs = refs[n:2 * n]
        send_sems, recv_sems = refs[2 * n:]
        x, y, c = _place()

        def copy(a, half_of, to):
            rows = outs[a].at[_half(outs[a], half_of)]
            return pltpu.make_async_remote_copy(
                src_ref=rows, dst_ref=rows, send_sem=send_sems.at[a], recv_sem=recv_sems.at[a],
                device_id=to, device_id_type=MESH)

        sends = [copy(a, c, (x, y, 1 - c)) for a in range(n)]
        for cp in sends:
            cp.start()
        for a in range(n):
            copy(a, 1 - c, (x, y, c)).wait_recv()
        for cp in sends:
            cp.wait_send()

    return pl.pallas_call(
        body, name=name,
        in_specs=[HBM_SPEC] * n, out_specs=[HBM_SPEC] * n,
        out_shape=[jax.ShapeDtypeStruct(f.shape, F32) for f in fulls],
        input_output_aliases={a: a for a in range(n)},
        scratch_shapes=[pltpu.SemaphoreType.DMA((n,)), pltpu.SemaphoreType.DMA((n,))],
    )(*fulls)


def _adamw_math(w, g, m, v):
    m = ADAM_B1 * m + (1.0 - ADAM_B1) * g
    v = ADAM_B2 * v + (1.0 - ADAM_B2) * (g * g)
    m_hat = m / (1.0 - ADAM_B1 ** ADAM_STEP)
    v_hat = v / (1.0 - ADAM_B2 ** ADAM_STEP)
    delta = -ADAM_LR * (m_hat / (jnp.sqrt(v_hat) + ADAM_EPS) + ADAM_WD * w)
    return delta, m, v


def _adamw(w, g, m, v, name, tr=256):
    R, C = w.shape
    tr = min(tr, R)

    def body(w_ref, g_ref, m_ref, v_ref, d_out, m_out, v_out):
        d_out[...], m_out[...], v_out[...] = _adamw_math(w_ref[...], g_ref[...], m_ref[...], v_ref[...])

    spec = pl.BlockSpec((tr, C), lambda i: (i, 0))
    return pl.pallas_call(
        body, name=name, grid=(R // tr,),
        in_specs=[spec] * 4, out_specs=[spec] * 3,
        out_shape=[jax.ShapeDtypeStruct((R, C), F32)] * 3,
        compiler_params=_params(1),
    )(w, g, m, v)


def _adamw_small(small_all, w, m, v, name):
    def body(s_ref, w_ref, m_ref, v_ref, g_out, d_out, m_out, v_out):
        g = s_ref[0]
        for d in range(1, N_DEV):
            g = g + s_ref[d]
        g_out[...] = g
        d_out[...], m_out[...], v_out[...] = _adamw_math(w_ref[...], g, m_ref[...], v_ref[...])

    vm = pl.BlockSpec(memory_space=pltpu.VMEM)
    return pl.pallas_call(
        body, name=name, in_specs=[vm] * 4, out_specs=[vm] * 4,
        out_shape=[jax.ShapeDtypeStruct(w.shape, F32)] * 4,
    )(small_all, w, m, v)


def _pack_small(norm_g, pool_scale, norm_f, extra_row):
    return jnp.concatenate([norm_g.reshape(2, D_MODEL), pool_scale.reshape(2, D_MODEL),
                            norm_f.reshape(1, D_MODEL), extra_row,
                            jnp.zeros((2, D_MODEL), F32)], axis=0)


def kernel(x, norm_g, pool_w_in, pool_w, pool_scale, pool_w_out, sb_w_in, sb_w_out, norm_f, loss_target, m_norm_g, m_pool_w_in, m_pool_w, m_pool_scale, m_pool_w_out, m_sb_w_in, m_sb_w_out, m_norm_f, v_norm_g, v_pool_w_in, v_pool_w, v_pool_scale, v_pool_w_out, v_sb_w_in, v_sb_w_out, v_norm_f):
    nb, S, _ = x.shape
    T = nb * S
    xt = x.reshape(T, D_MODEL)
    target = loss_target.reshape(T, D_MODEL)
    cx, cy, cc = _place()

    def shard2d(w):
        return w.reshape(-1, w.shape[-1])

    names = ("pool_w_in", "pool_w", "pool_w_out", "sb_w_in", "sb_w_out")
    w_shards = [shard2d(w) for w in (pool_w_in, pool_w, pool_w_out, sb_w_in, sb_w_out)]
    m_shards = [shard2d(w) for w in (m_pool_w_in, m_pool_w, m_pool_w_out, m_sb_w_in, m_sb_w_out)]
    v_shards = [shard2d(w) for w in (v_pool_w_in, v_pool_w, v_pool_w_out, v_sb_w_in, v_sb_w_out)]

    chip = (2 * cx + cy).reshape(1).astype(jnp.int32)
    c_arr = cc.reshape(1).astype(jnp.int32)
    place = jnp.stack([2 * cx + cy, cc]).astype(jnp.int32)
    slots = [_cast_to_slot(w, chip, "cast_" + nm) for w, nm in zip(w_shards, names)]
    g0, g1, gf = norm_g[0:1], norm_g[1:2], norm_f.reshape(1, D_MODEL)

    w_pin, = _allgather_weights(slots[:1], "allgather_pool_in_weights")
    mix_send, mix_recv, mix_slots, token = _exchange_start(slots[1:3], w_pin, _weight_plan, 6,
                                                           "pool_weights_start")
    sb_send, sb_recv, sb_slots, token = _exchange_start(slots[3:], token, _weight_plan, 6,
                                                        "sb_weights_start")

    proj0, u0 = _rms_matmul(xt, g0 + token[0:1, 0:1], w_pin, [(2 * D_INNER, BF16)], "pool_in_proj",
                            tn=w_pin.shape[2])
    mix_slots = _exchange_wait(mix_slots, mix_send, mix_recv, proj0, _weight_plan, "pool_weights_wait")
    w_g, w_pout = _allgather_weights(mix_slots, "pool_weights_forward", landed=True)
    w_pout = w_pout.reshape(D_INNER, D_MODEL)
    y0, pooled, mixed = _pool_fwd(proj0, w_g, pool_scale, S, "pool_mix")
    sb_slots = _exchange_wait(sb_slots, sb_send, sb_recv, y0, _weight_plan, "sb_weights_wait")
    w_sin, w_sout = _allgather_weights(sb_slots, "sb_weights_forward", landed=True)
    w_sout = w_sout.reshape(D_INNER, D_MODEL)
    h1 = _matmul_residual(y0, w_pout, xt, "pool_out_proj")
    n1 = 2 * QK_WIDTH + 2 * D_INNER
    qkvz, u1 = _rms_matmul(h1, g1, w_sin, [(n1, BF16)], "sb_in_proj", tn=w_sin.shape[2])
    o, y1, ltot = _attn_fwd(qkvz, S, "sb_attention")
    dh2, d_norm_f, loss_row = _out_proj_loss_head(y1, w_sout, h1, gf, target, "sb_out_proj_loss_head")

    def reduce_start(partials, tag):
        from_sibling = _sibling_exchange(partials, None, "grad_sibling_exchange_" + tag)
        sums = [_chip_sum(p, s, c_arr, "grad_chip_sum_%s_%d" % (tag, i))
                for i, (p, s) in enumerate(zip(partials, from_sibling))]
        lands = [lax.empty((3,) + s.shape[1:], BF16) for s in sums]
        send, recv, bufs, token = _exchange_start(sums + lands, c_arr, _chip_sum_plan, 3 * len(sums),
                                                  "grad_chip_exchange_start_" + tag)
        return (partials, list(from_sibling), send, recv, bufs), token[0:1, 0:1]

    def reduce_finish(started, after, tag):
        partials, from_sibling, send, recv, bufs = started
        received = _exchange_wait(bufs, send, recv, after, _chip_sum_plan,
                                  "grad_chip_exchange_wait_" + tag)[len(partials):]
        return [_reduce_half(p, s, r, place, "grad_reduce_%s_%d" % (tag, i))
                for i, (p, s, r) in enumerate(zip(partials, from_sibling, received))]

    shard = lambda i, j, t: (j, 0, 0)
    gw_sout = _matmul_tn(y1, dh2, D_INNER, D_MODEL, (D_INNER, D_MODEL), (1024, 1024),
                         lambda i, j, t: (i, j), "grad_sb_w_out", bm=1024, bn=1024)
    sout_started, token = reduce_start([gw_sout.reshape(N_CHIPS, -1, D_MODEL)], "sb_out")
    do, dproj1 = _attn_gate_bwd(dh2, w_sout, qkvz, o, token, "sb_gate_bwd")
    dproj1 = _attn_bwd(qkvz, do, ltot, dproj1, S, "sb_attention_bwd")
    gw_sin = _matmul_tn(u1, dproj1, D_MODEL, n1, (N_CHIPS, D_MODEL, n1 // 4), (1, D_MODEL, n1 // 4),
                        shard, "grad_sb_w_in", bm=D_MODEL, bn=n1 // 4)
    sin_started, token = reduce_start([gw_sin], "sb_in")
    dh1, d_g1 = _matmul_nt_rms_bwd(dproj1, w_sin, h1, g1 + token, dh2, "sb_in_bwd", tk=w_sin.shape[2])
    gw_pout = _matmul_tn(y0, dh1, D_INNER, D_MODEL, (D_INNER, D_MODEL), (1024, 1024),
                         lambda i, j, t: (i, j), "grad_pool_w_out", bm=1024, bn=1024)
    dmixed, dproj0, d_scale = _pool_gate_bwd(dh1, w_pout, proj0, mixed, pool_scale, "pool_gate_bwd")
    gw_g = _matmul_tn(pooled, dmixed, D_INNER, D_INNER, (N_CHIPS, GROUP_DIM, GROUP_DIM),
                      (N_CHIPS, GROUP_DIM // N_CHIPS, GROUP_DIM), lambda i, j, t: (0, i, 0),
                      "grad_pool_w", bm=GROUP_DIM, bn=GROUP_DIM, diagonal_blocks=True)
    mix_started, token = reduce_start([gw_g, gw_pout.reshape(N_CHIPS, -1, D_MODEL)], "pool_mix")
    dproj0 = _pool_bwd(dmixed, w_g, dproj0, token, S, "pool_bwd")
    n0 = 2 * D_INNER
    gw_pin = _matmul_tn(u0, dproj0, D_MODEL, n0, (N_CHIPS, D_MODEL, n0 // 4), (1, D_MODEL, n0 // 4),
                        shard, "grad_pool_w_in", bm=D_MODEL, bn=n0 // 4)
    pin_started, token = reduce_start([gw_pin], "pool_in")
    dx, d_g0 = _matmul_nt_rms_bwd(dproj0, w_pin, xt, g0 + token, dh1, "pool_in_bwd", tk=w_pin.shape[2])

    small = _pack_small(jnp.concatenate([d_g0, d_g1], axis=0), d_scale, d_norm_f,
                        jnp.broadcast_to(loss_row[:, :1], (1, D_MODEL)))
    small_all, = _sibling_exchange([], small, "small_sums_exchange")
    grads = _join_halves(reduce_finish(pin_started, dx, "pool_in")
                         + reduce_finish(mix_started, dx, "pool_mix")
                         + reduce_finish(sin_started, dx, "sb_in")
                         + reduce_finish(sout_started, dx, "sb_out"), "grad_join_halves")

    deltas, new_m, new_v = [], [], []
    for w, g, m, v, nm in zip(w_shards, grads, m_shards, v_shards, names):
        d, mm, vv = _adamw(w, g, m, v, "adamw_" + nm)
        deltas.append(d)
        new_m.append(mm)
        new_v.append(vv)

    zero_row = jnp.zeros((1, D_MODEL), F32)
    g_small, d_small, m_small, v_small = _adamw_small(
        small_all, _pack_small(norm_g, pool_scale, norm_f, zero_row),
        _pack_small(m_norm_g, m_pool_scale, m_norm_f, zero_row),
        _pack_small(v_norm_g, v_pool_scale, v_norm_f, zero_row + 1.0), "adamw_small")
    loss = g_small[5, 0]

    def unpack_small(a):
        return a[0:2], a[2:4].reshape(1, D_INNER), a[4]

    def assemble(big, small3):
        ng, ps, nf = small3
        return [ng, big[0].reshape(pool_w_in.shape), big[1].reshape(pool_w.shape), ps,
                big[2].reshape(pool_w_out.shape), big[3].reshape(sb_w_in.shape),
                big[4].reshape(sb_w_out.shape), nf]

    return (loss, dx.reshape(x.shape),
            *assemble(grads, unpack_small(g_small)),
            *assemble(deltas, unpack_small(d_small)),
            *assemble(new_m, unpack_small(m_small)),
            *assemble(new_v, unpack_small(v_small)))
```

```python
import jax
import jax.numpy as jnp
from jax import lax
from jax.experimental import pallas as pl
from jax.experimental.pallas import tpu as pltpu

F32 = jnp.float32
BF16 = jnp.bfloat16
MESH = pl.DeviceIdType.MESH

D_MODEL = 1024
D_INNER = 2048
N_GROUPS = 4
GROUP_DIM = 512
HEAD_PAIR_QK = 128
HEAD_V = 128
QK_WIDTH = 1024
RMS_EPS = 1e-6
HALO = 16
N_CHIPS = 4
N_DEV = 8

ADAM_LR = 0.001
ADAM_B1 = 0.9
ADAM_B2 = 0.999
ADAM_EPS = 1e-08
ADAM_WD = 0.01
ADAM_STEP = 10

VMEM_LIMIT = 56 * 1024 * 1024

HBM_SPEC = pl.BlockSpec(memory_space=pltpu.HBM)


def _params(n_axes):
    return pltpu.CompilerParams(dimension_semantics=("arbitrary",) * n_axes,
                                vmem_limit_bytes=VMEM_LIMIT)


def _dot(a, b):
    return jnp.dot(a, b, preferred_element_type=F32)


def _dot_nt(a, b):
    return lax.dot_general(a, b, (((1,), (1,)), ((), ())), preferred_element_type=F32)


def _dot_tn(a, b):
    return lax.dot_general(a, b, (((0,), (0,)), ((), ())), preferred_element_type=F32)


def _sigmoid(z):
    return 1.0 / (1.0 + jnp.exp(-z))


def _row_blocks(tm, rows=256):
    return [slice(r, r + rows) for r in range(0, tm, rows)]


def _rms_matmul(h, g_row, w4, outs, name, tm=1024, tn=512):
    T = h.shape[0]
    per_shard = w4.shape[2] // tn
    starts = [0]
    for width, _ in outs:
        starts.append(starts[-1] + width // tn)
    n_out = len(outs)

    def body(h_ref, g_ref, w_ref, *rest):
        o_refs, u_out, u_s = rest[:n_out], rest[n_out], rest[n_out + 1]
        n = pl.program_id(1)

        @pl.when(n == 0)
        def _():
            x = h_ref[...]
            inv = lax.rsqrt(jnp.mean(x * x, axis=-1, keepdims=True) + RMS_EPS)
            u = (x * inv * g_ref[...]).astype(BF16)
            u_s[...] = u
            u_out[...] = u

        res = _dot(u_s[...], w_ref[0])
        for k in range(n_out):
            @pl.when((n >= starts[k]) & (n < starts[k + 1]))
            def _():
                o_refs[k][...] = res.astype(o_refs[k].dtype)

    def out_map(k):
        return lambda m, n: (m, jnp.clip(n - starts[k], 0, starts[k + 1] - starts[k] - 1))

    return pl.pallas_call(
        body, name=name, grid=(T // tm, starts[-1]),
        in_specs=[pl.BlockSpec((tm, D_MODEL), lambda m, n: (m, 0)),
                  pl.BlockSpec((1, D_MODEL), lambda m, n: (0, 0)),
                  pl.BlockSpec((1, D_MODEL, tn), lambda m, n: (n // per_shard, 0, n % per_shard))],
        out_specs=[pl.BlockSpec((tm, tn), out_map(k)) for k in range(n_out)]
        + [pl.BlockSpec((tm, D_MODEL), lambda m, n: (m, 0))],
        out_shape=[jax.ShapeDtypeStruct((T, width), dt) for width, dt in outs]
        + [jax.ShapeDtypeStruct((T, D_MODEL), BF16)],
        scratch_shapes=[pltpu.VMEM((tm, D_MODEL), BF16)],
        compiler_params=_params(2),
    )(h, g_row, w4)


def _matmul_residual(a, w, res, name, tm=1024, tn=1024):
    T, K = a.shape
    N = w.shape[1]

    def body(a_ref, w_ref, r_ref, o_ref):
        o_ref[...] = r_ref[...] + _dot(a_ref[...], w_ref[...])

    return pl.pallas_call(
        body, name=name, grid=(T // tm, N // tn),
        in_specs=[pl.BlockSpec((tm, K), lambda m, n: (m, 0)),
                  pl.BlockSpec((K, tn), lambda m, n: (0, n)),
                  pl.BlockSpec((tm, tn), lambda m, n: (m, n))],
        out_specs=pl.BlockSpec((tm, tn), lambda m, n: (m, n)),
        out_shape=jax.ShapeDtypeStruct((T, N), F32),
        compiler_params=_params(2),
    )(a, w, res)


def _matmul_tn(a, b, a_cols, b_cols, out_shape, out_block, out_map, name, bm, bn, tk=2048,
               diagonal_blocks=False):
    T = a.shape[0]
    tk = min(tk, T)

    def body(a_ref, b_ref, o_ref):
        @pl.when(pl.program_id(2) == 0)
        def _():
            o_ref[...] = jnp.zeros_like(o_ref)

        part = _dot_tn(a_ref[...].astype(BF16), b_ref[...].astype(BF16))
        o_ref[...] += part.reshape(o_ref.shape)

    b_map = (lambda i, j, t: (t, i)) if diagonal_blocks else (lambda i, j, t: (t, j))
    return pl.pallas_call(
        body, name=name, grid=(a_cols // bm, 1 if diagonal_blocks else b_cols // bn, T // tk),
        in_specs=[pl.BlockSpec((tk, bm), lambda i, j, t: (t, i)),
                  pl.BlockSpec((tk, bn), b_map)],
        out_specs=pl.BlockSpec(out_block, out_map),
        out_shape=jax.ShapeDtypeStruct(out_shape, F32),
        compiler_params=_params(3),
    )(a, b)


def _matmul_nt_rms_bwd(dproj, w4, h, g_row, dres, name, tm=1024, tk=512):
    T, cols = dproj.shape
    per_shard = w4.shape[2] // tk
    nk = cols // tk

    def body(dp_ref, w_ref, h_ref, g_ref, r_ref, dx_ref, dg_ref, acc):
        m, k = pl.program_id(0), pl.program_id(1)

        @pl.when(k == 0)
        def _():
            acc[...] = jnp.zeros_like(acc)

        @pl.when((k == 0) & (m == 0))
        def _():
            dg_ref[...] = jnp.zeros_like(dg_ref)

        acc[...] += _dot_nt(dp_ref[...], w_ref[0])

        @pl.when(k == nk - 1)
        def _():
            du = acc[...]
            x = h_ref[...]
            inv = lax.rsqrt(jnp.mean(x * x, axis=-1, keepdims=True) + RMS_EPS)
            xhat = x * inv
            dg_ref[...] += jnp.sum(du * xhat, axis=0, keepdims=True)
            dxh = du * g_ref[...]
            proj = jnp.mean(dxh * xhat, axis=-1, keepdims=True)
            dx_ref[...] = r_ref[...] + inv * (dxh - xhat * proj)

    return pl.pallas_call(
        body, name=name, grid=(T // tm, nk),
        in_specs=[pl.BlockSpec((tm, tk), lambda m, k: (m, k)),
                  pl.BlockSpec((1, D_MODEL, tk), lambda m, k: (k // per_shard, 0, k % per_shard)),
                  pl.BlockSpec((tm, D_MODEL), lambda m, k: (m, 0)),
                  pl.BlockSpec((1, D_MODEL), lambda m, k: (0, 0)),
                  pl.BlockSpec((tm, D_MODEL), lambda m, k: (m, 0))],
        out_specs=[pl.BlockSpec((tm, D_MODEL), lambda m, k: (m, 0)),
                   pl.BlockSpec((1, D_MODEL), lambda m, k: (0, 0))],
        out_shape=[jax.ShapeDtypeStruct((T, D_MODEL), F32),
                   jax.ShapeDtypeStruct((1, D_MODEL), F32)],
        scratch_shapes=[pltpu.VMEM((tm, D_MODEL), F32)],
        compiler_params=_params(2),
    )(dproj, w4, h, g_row, dres)


def _window_of(g):
    return jnp.left_shift(2, g)


def _select_stage(g, stages):
    res = stages[0]
    for i in range(1, len(stages)):
        res = jnp.where(g >= i, stages[i], res)
    return res


def _pool_fwd(proj0, wg4, scale_row, S, name, tm=1024):
    T = proj0.shape[0]
    tm = min(tm, S)
    blocks_per_seq = S // tm
    hb = tm // HALO

    def body(x_ref, halo_ref, z_ref, w_ref, s_ref, y_ref, p_ref, mix_ref):
        m, g = pl.program_id(0), pl.program_id(1)
        first = (m % blocks_per_seq) == 0
        halo = jnp.where(first, 0.0, halo_ref[...].astype(F32))
        x = x_ref[...].astype(F32)
        ext = jnp.concatenate([halo, x], axis=0)
        stages = []
        cur = ext
        for sh in (1, 2, 4, 8):
            cur = cur + pltpu.roll(cur, sh, 0)
            stages.append(cur[HALO:, :])
        win_sum = _select_stage(g, stages)
        pos = (m % blocks_per_seq) * tm + lax.broadcasted_iota(jnp.int32, (tm, 1), 0)
        count = jnp.minimum(pos + 1, _window_of(g)).astype(F32)
        p_ref[...] = (win_sum / count - x).astype(BF16)
        w = w_ref[...].reshape(GROUP_DIM, GROUP_DIM)
        for rows in _row_blocks(tm):
            mixed = _dot(p_ref[rows, :], w)
            z = z_ref[rows, :].astype(F32)
            y_ref[rows, :] = (mixed * s_ref[...] * (z * _sigmoid(z))).astype(BF16)
            mix_ref[rows, :] = mixed.astype(BF16)

    blk = lambda m, g: (m, g)
    return pl.pallas_call(
        body, name=name, grid=(T // tm, N_GROUPS),
        in_specs=[pl.BlockSpec((tm, GROUP_DIM), blk),
                  pl.BlockSpec((HALO, GROUP_DIM), lambda m, g: (jnp.maximum(m * hb - 1, 0), g)),
                  pl.BlockSpec((tm, GROUP_DIM), lambda m, g: (m, N_GROUPS + g)),
                  pl.BlockSpec((N_CHIPS, GROUP_DIM // N_CHIPS, GROUP_DIM), lambda m, g: (0, g, 0)),
                  pl.BlockSpec((1, GROUP_DIM), lambda m, g: (0, g))],
        out_specs=[pl.BlockSpec((tm, GROUP_DIM), blk)] * 3,
        out_shape=[jax.ShapeDtypeStruct((T, D_INNER), BF16)] * 3,
        compiler_params=_params(2),
    )(proj0, proj0, proj0, wg4, scale_row)


def _pool_gate_bwd(dh, w_out, proj0, mixed, scale_row, name, tm=1024, tn=512):
    T = dh.shape[0]
    gate_b0 = D_INNER // tn

    def body(dh_ref, w_ref, z_ref, mix_ref, s_ref, dm_ref, dz_ref, ds_ref, dh_s):
        m, n = pl.program_id(0), pl.program_id(1)

        @pl.when((m == 0) & (n == 0))
        def _():
            ds_ref[...] = jnp.zeros_like(ds_ref)

        @pl.when(n == 0)
        def _():
            dh_s[...] = dh_ref[...].astype(BF16)

        cols = pl.ds(pl.multiple_of(n * tn, tn), tn)
        s = s_ref[...]
        ds = ds_ref[:, cols]
        for rows in _row_blocks(tm):
            dy = _dot_nt(dh_s[rows, :], w_ref[...])
            z = z_ref[rows, :].astype(F32)
            sig = _sigmoid(z)
            silu = z * sig
            mixed = mix_ref[rows, :].astype(F32)
            dm_ref[rows, :] = (dy * s * silu).astype(BF16)
            dz_ref[rows, :] = (dy * mixed * s * (sig * (1.0 + z * (1.0 - sig)))).astype(BF16)
            ds = ds + jnp.sum(dy * mixed * silu, axis=0, keepdims=True)
        ds_ref[:, cols] = ds

    return pl.pallas_call(
        body, name=name, grid=(T // tm, D_INNER // tn),
        in_specs=[pl.BlockSpec((tm, D_MODEL), lambda m, n: (m, 0)),
                  pl.BlockSpec((tn, D_MODEL), lambda m, n: (n, 0)),
                  pl.BlockSpec((tm, tn), lambda m, n: (m, gate_b0 + n)),
                  pl.BlockSpec((tm, tn), lambda m, n: (m, n)),
                  pl.BlockSpec((1, tn), lambda m, n: (0, n))],
        out_specs=[pl.BlockSpec((tm, tn), lambda m, n: (m, n)),
                   pl.BlockSpec((tm, tn), lambda m, n: (m, gate_b0 + n)),
                   pl.BlockSpec((1, D_INNER), lambda m, n: (0, 0))],
        out_shape=[jax.ShapeDtypeStruct((T, D_INNER), BF16),
                   jax.ShapeDtypeStruct((T, 2 * D_INNER), BF16),
                   jax.ShapeDtypeStruct((1, D_INNER), F32)],
        scratch_shapes=[pltpu.VMEM((tm, D_MODEL), BF16)],
        compiler_params=_params(2),
    )(dh, w_out, proj0, mixed, scale_row)


def _pool_bwd(dmixed, wg4, dproj0, after, S, name, tm=1024):
    T = dmixed.shape[0]
    tm = min(tm, S)
    blocks_per_seq = S // tm
    hb = tm // HALO
    n_halo_blocks = T // HALO

    def body(dm_ref, halo_ref, w_ref, _, __, o_ref):
        m, g = pl.program_id(0), pl.program_id(1)
        ext = jnp.concatenate([dm_ref[...], halo_ref[...]], axis=0)
        dp = _dot_nt(ext, w_ref[...].reshape(GROUP_DIM, GROUP_DIM))
        pos = (m % blocks_per_seq) * tm + lax.broadcasted_iota(jnp.int32, (tm + HALO, 1), 0)
        count = jnp.minimum(pos + 1, _window_of(g)).astype(F32)
        c = jnp.where(pos < S, dp / count, 0.0)
        n = tm + HALO
        stages = []
        cur = c
        for sh in (1, 2, 4, 8):
            cur = cur + pltpu.roll(cur, n - sh, 0)
            stages.append(cur[:tm, :])
        o_ref[...] = (_select_stage(g, stages) - dp[:tm, :]).astype(BF16)

    blk = lambda m, g: (m, g)
    return pl.pallas_call(
        body, name=name, grid=(T // tm, N_GROUPS),
        in_specs=[pl.BlockSpec((tm, GROUP_DIM), blk),
                  pl.BlockSpec((HALO, GROUP_DIM),
                               lambda m, g: (jnp.minimum((m + 1) * hb, n_halo_blocks - 1), g)),
                  pl.BlockSpec((N_CHIPS, GROUP_DIM // N_CHIPS, GROUP_DIM), lambda m, g: (0, g, 0)),
                  HBM_SPEC, ANY_SPEC],
        out_specs=pl.BlockSpec((tm, GROUP_DIM), blk),
        out_shape=jax.ShapeDtypeStruct(dproj0.shape, dproj0.dtype),
        input_output_aliases={3: 0},
        compiler_params=_params(2),
    )(dmixed, dmixed, wg4, dproj0, after)


TQ = 256


def _split_dot(x, m):
    hi = x.astype(BF16)
    lo = (x - hi.astype(F32)).astype(BF16)
    return _dot(hi, m) + _dot(lo, m)


NEG_LOG2E = -1.4426950408889634


def _log_terms(z):
    soft = jnp.log(1.0 + jnp.exp2(jnp.abs(z) * NEG_LOG2E))
    log_beta = jnp.minimum(z, 0.0) - soft
    return log_beta, log_beta - z


N_HEADS = 16
FWD_HEADS = BWD_HEADS = 4


def _masked_heads(x, heads):
    lane = lax.broadcasted_iota(jnp.int32, (1, HEAD_PAIR_QK), 1)
    out = []
    for hh in range(heads):
        slab = x[:, (hh // 2) * HEAD_PAIR_QK:(hh // 2 + 1) * HEAD_PAIR_QK]
        out.append(jnp.where((lane // 64) == hh % 2, slab, jnp.zeros_like(slab)))
    return out


def _attn_fwd(qkvz, S, name):
    T = qkvz.shape[0]
    nq = S // TQ
    HEADS, QK_W, V_W = FWD_HEADS, FWD_HEADS * 64, FWD_HEADS * HEAD_V
    k_b0 = QK_WIDTH // QK_W
    v_b0 = 2 * QK_WIDTH // V_W
    z_b0 = (2 * QK_WIDTH + D_INNER) // V_W
    hs = range(HEADS)

    def body(q_ref, k_ref, v_ref, z_ref, o_ref, y_ref, lt_ref):
        row = lax.broadcasted_iota(jnp.int32, (TQ, TQ), 0)
        col = lax.broadcasted_iota(jnp.int32, (TQ, TQ), 1)
        causal = col < row
        later_in_block = (row > col).astype(BF16)
        lax.fori_loop(0, nq, lambda qi, _: q_block(qi, causal, later_in_block,
                                                   q_ref, k_ref, v_ref, z_ref, o_ref, y_ref, lt_ref), 0)

    def q_block(qi, causal, later_in_block, q_ref, k_ref, v_ref, z_ref, o_ref, y_ref, lt_ref):
        rows = pl.ds(pl.multiple_of(qi * TQ, TQ), TQ)
        qms = [qm * 0.125 for qm in _masked_heads(q_ref[rows, :], HEADS)]

        def step(j, carry, diagonal):
            koff = pl.multiple_of(j * TQ, TQ)
            kbs = [k_ref[pl.ds(koff, TQ), p * HEAD_PAIR_QK:(p + 1) * HEAD_PAIR_QK]
                   for p in range(HEADS // 2)]
            run, acc = [carry[2 * hh] for hh in hs], [carry[2 * hh + 1] for hh in hs]
            z = [_dot_nt(qms[hh], kbs[hh // 2]) for hh in hs]
            terms = [_log_terms(z[hh]) for hh in hs]
            log_om = [jnp.where(causal, t[1], 0.0) if diagonal else t[1] for t in terms]
            later = [_split_dot(log_om[hh], later_in_block) for hh in hs]
            a = [jnp.exp(terms[hh][0] + (run[hh] + later[hh])) for hh in hs]
            if diagonal:
                a = [jnp.where(causal, a[hh], 0.0) for hh in hs]
            out = []
            for hh in hs:
                vb = v_ref[pl.ds(koff, TQ), hh * HEAD_V:(hh + 1) * HEAD_V]
                out += [run[hh] + jnp.sum(log_om[hh], axis=1, keepdims=True),
                        acc[hh] + _dot(a[hh].astype(BF16), vb)]
            return tuple(out)

        zero = (jnp.zeros((TQ, 1), F32), jnp.zeros((TQ, HEAD_V), F32))
        carry = step(qi, zero * HEADS, True)
        carry = lax.fori_loop(0, qi, lambda i, c: step(qi - 1 - i, c, False), carry)
        for hh in hs:
            sl = slice(hh * HEAD_V, (hh + 1) * HEAD_V)
            acc = carry[2 * hh + 1]
            z = z_ref[rows, sl].astype(F32)
            o_ref[rows, sl] = acc.astype(BF16)
            y_ref[rows, sl] = (acc * (z * _sigmoid(z))).astype(BF16)
            lt_ref[rows, hh:hh + 1] = carry[2 * hh]
        return 0

    blk = lambda b, p: (b, p)
    return pl.pallas_call(
        body, name=name, grid=(T // S, N_HEADS // HEADS),
        in_specs=[pl.BlockSpec((S, QK_W), blk),
                  pl.BlockSpec((S, QK_W), lambda b, p: (b, k_b0 + p)),
                  pl.BlockSpec((S, V_W), lambda b, p: (b, v_b0 + p)),
                  pl.BlockSpec((S, V_W), lambda b, p: (b, z_b0 + p))],
        out_specs=[pl.BlockSpec((S, V_W), blk),
                   pl.BlockSpec((S, V_W), blk),
                   pl.BlockSpec((None, S, HEADS), lambda b, p: (p, b, 0))],
        out_shape=[jax.ShapeDtypeStruct((T, D_INNER), BF16),
                   jax.ShapeDtypeStruct((T, D_INNER), BF16),
                   jax.ShapeDtypeStruct((N_HEADS // HEADS, T, HEADS), F32)],
        compiler_params=_params(2),
    )(qkvz, qkvz, qkvz, qkvz)


def _attn_gate_bwd(dh, w_out, qkvz, o, after, name, tm=1024, tn=512):
    T = dh.shape[0]
    gate_b0 = (2 * QK_WIDTH + D_INNER) // tn

    def body(dh_ref, w_ref, z_ref, o_ref, _, do_ref, dz_ref, dh_s):
        @pl.when(pl.program_id(1) == 0)
        def _():
            dh_s[...] = dh_ref[...].astype(BF16)

        for rows in _row_blocks(tm):
            dy = _dot_nt(dh_s[rows, :], w_ref[...])
            z = z_ref[rows, :].astype(F32)
            sig = _sigmoid(z)
            do_ref[rows, :] = (dy * (z * sig)).astype(BF16)
            dz_ref[rows, :] = (dy * o_ref[rows, :].astype(F32)
                               * (sig * (1.0 + z * (1.0 - sig)))).astype(BF16)

    return pl.pallas_call(
        body, name=name, grid=(T // tm, D_INNER // tn),
        in_specs=[pl.BlockSpec((tm, D_MODEL), lambda m, n: (m, 0)),
                  pl.BlockSpec((tn, D_MODEL), lambda m, n: (n, 0)),
                  pl.BlockSpec((tm, tn), lambda m, n: (m, gate_b0 + n)),
                  pl.BlockSpec((tm, tn), lambda m, n: (m, n)),
                  ANY_SPEC],
        out_specs=[pl.BlockSpec((tm, tn), lambda m, n: (m, n)),
                   pl.BlockSpec((tm, tn), lambda m, n: (m, gate_b0 + n))],
        out_shape=[jax.ShapeDtypeStruct((T, D_INNER), BF16),
                   jax.ShapeDtypeStruct((T, 2 * QK_WIDTH + 2 * D_INNER), BF16)],
        scratch_shapes=[pltpu.VMEM((tm, D_MODEL), BF16)],
        compiler_params=_params(2),
    )(dh, w_out, qkvz, o, after)


def _attn_bwd(qkv, do, ltot, dproj1, S, name):
    T = qkv.shape[0]
    nq = S // TQ
    HEADS, QK_W, V_W = BWD_HEADS, BWD_HEADS * 64, BWD_HEADS * HEAD_V
    k_b0 = QK_WIDTH // QK_W
    v_b0 = 2 * QK_WIDTH // V_W
    hs = range(HEADS)
    pairs = range(HEADS // 2)

    def body(q_ref, k_ref, v_ref, do_ref, lt_ref, _, out_ref, dq_s, dk_s, dv_s, dkb_s, dvb_s, sems):
        b, p = pl.program_id(0), pl.program_id(1)
        row = lax.broadcasted_iota(jnp.int32, (TQ, TQ), 0)
        col = lax.broadcasted_iota(jnp.int32, (TQ, TQ), 1)
        causal = col < row
        upto = (row <= col).astype(BF16)
        before = (row < col).astype(BF16)
        dk_s[...] = jnp.zeros_like(dk_s)
        dv_s[...] = jnp.zeros_like(dv_s)

        def q_block(qi, _):
            qoff = pl.multiple_of(qi * TQ, TQ)
            qms = [qm * 0.125 for qm in _masked_heads(q_ref[pl.ds(qoff, TQ), :], HEADS)]
            vsl = [slice(hh * HEAD_V, (hh + 1) * HEAD_V) for hh in hs]
            psl = [slice(pp * HEAD_PAIR_QK, (pp + 1) * HEAD_PAIR_QK) for pp in pairs]
            do_h = [do_ref[pl.ds(qoff, TQ), sl] for sl in vsl]
            total = [lt_ref[pl.ds(qoff, TQ), hh:hh + 1] for hh in hs]

            def k_block(j, carry, diagonal):
                koff = pl.multiple_of(j * TQ, TQ)
                kms = _masked_heads(k_ref[pl.ds(koff, TQ), :], HEADS)
                g_before = [carry[2 * hh] for hh in hs]
                lom_before = [carry[2 * hh + 1] for hh in hs]
                z = [_dot_nt(qms[hh], kms[hh]) for hh in hs]
                da = [_dot_nt(do_h[hh], v_ref[pl.ds(koff, TQ), vsl[hh]]) for hh in hs]
                terms = [_log_terms(z[hh]) for hh in hs]
                log_om = [jnp.where(causal, t[1], 0.0) if diagonal else t[1] for t in terms]
                prefix = [_split_dot(log_om[hh], upto) for hh in hs]
                a = [jnp.exp(terms[hh][0] + ((total[hh] - lom_before[hh]) - prefix[hh])) for hh in hs]
                if diagonal:
                    a = [jnp.where(causal, a[hh], 0.0) for hh in hs]
                g = [a[hh] * da[hh] for hh in hs]
                g_prefix = [_dot(g[hh].astype(BF16), before) for hh in hs]
                out, dzs = [], []
                for hh in hs:
                    beta = jnp.exp(terms[hh][0])
                    g_excl = (g_before[hh] + g_prefix[hh]) * beta
                    if diagonal:
                        g_excl = jnp.where(causal, g_excl, 0.0)
                    dzs.append((g[hh] * (1.0 - beta) - g_excl).astype(BF16))
                    out += [g_before[hh] + jnp.sum(g[hh], axis=1, keepdims=True),
                            lom_before[hh] + jnp.sum(log_om[hh], axis=1, keepdims=True)]
                for hh in hs:
                    dv_s[pl.ds(koff, TQ), vsl[hh]] += _dot_tn(a[hh].astype(BF16), do_h[hh])
                dq = []
                for pp in pairs:
                    pair = slice(2 * pp, 2 * pp + 2)
                    dq.append(carry[2 * HEADS + pp] + _dot(jnp.concatenate(dzs[pair], axis=1),
                                                           jnp.concatenate(kms[pair], axis=0)))
                    dk_s[pl.ds(koff, TQ), psl[pp]] += _dot_tn(jnp.concatenate(dzs[pair], axis=0),
                                                              jnp.concatenate(qms[pair], axis=0))
                return tuple(out) + tuple(dq)

            zero = jnp.zeros((TQ, 1), F32)
            carry = (zero,) * (2 * HEADS) + (jnp.zeros((TQ, HEAD_PAIR_QK), F32),) * (HEADS // 2)
            carry = lax.fori_loop(0, qi, lambda j, c: k_block(j, c, False), carry)
            carry = k_block(qi, carry, True)
            for pp in pairs:
                dq_s[pl.ds(qoff, TQ), psl[pp]] = (carry[2 * HEADS + pp] * 0.125).astype(BF16)
            return 0

        lax.fori_loop(0, nq, q_block, 0)
        dkb_s[...] = dk_s[...].astype(BF16)
        dvb_s[...] = dv_s[...].astype(BF16)
        rows = pl.ds(pl.multiple_of(b * S, TQ), S)
        copies = [
            pltpu.make_async_copy(
                dq_s, out_ref.at[rows, pl.ds(pl.multiple_of(p * QK_W, 128), QK_W)], sems.at[0]),
            pltpu.make_async_copy(
                dkb_s, out_ref.at[rows, pl.ds(pl.multiple_of(QK_WIDTH + p * QK_W, 128), QK_W)],
                sems.at[1]),
            pltpu.make_async_copy(
                dvb_s, out_ref.at[rows, pl.ds(pl.multiple_of(2 * QK_WIDTH + p * V_W, 128), V_W)],
                sems.at[2]),
        ]
        for cp in copies:
            cp.start()
        for cp in copies:
            cp.wait()

    return pl.pallas_call(
        body, name=name, grid=(T // S, N_HEADS // HEADS),
        in_specs=[pl.BlockSpec((S, QK_W), lambda b, p: (b, p)),
                  pl.BlockSpec((S, QK_W), lambda b, p: (b, k_b0 + p)),
                  pl.BlockSpec((S, V_W), lambda b, p: (b, v_b0 + p)),
                  pl.BlockSpec((S, V_W), lambda b, p: (b, p)),
                  pl.BlockSpec((None, S, HEADS), lambda b, p: (p, b, 0)),
                  HBM_SPEC],
        out_specs=HBM_SPEC,
        out_shape=jax.ShapeDtypeStruct(dproj1.shape, dproj1.dtype),
        input_output_aliases={5: 0},
        scratch_shapes=[pltpu.VMEM((S, QK_W), BF16),
                        pltpu.VMEM((S, QK_W), F32),
                        pltpu.VMEM((S, V_W), F32),
                        pltpu.VMEM((S, QK_W), BF16),
                        pltpu.VMEM((S, V_W), BF16),
                        pltpu.SemaphoreType.DMA((3,))],
        compiler_params=_params(2),
    )(qkv, qkv, qkv, do, ltot, dproj1)


def _out_proj_loss_head(a, w, res, g_row, target, name, tm=512):
    T, K = a.shape

    def body(a_ref, w_ref, r_ref, g_ref, t_ref, dh_ref, dg_ref, loss_ref):
        @pl.when(pl.program_id(0) == 0)
        def _():
            dg_ref[...] = jnp.zeros_like(dg_ref)
            loss_ref[...] = jnp.zeros_like(loss_ref)

        gain = g_ref[...]
        dg, loss = dg_ref[...], loss_ref[...]
        for rows in _row_blocks(tm):
            x = r_ref[rows, :] + _dot(a_ref[rows, :], w_ref[...])
            inv = lax.rsqrt(jnp.mean(x * x, axis=-1, keepdims=True) + RMS_EPS)
            xhat = x * inv
            err = xhat * gain - t_ref[rows, :]
            per_token = jnp.mean(err * err, axis=-1, keepdims=True)
            loss = loss + 0.5 * jnp.sum(per_token, axis=0, keepdims=True)
            dy = err * (1.0 / D_MODEL)
            dg = dg + jnp.sum(dy * xhat, axis=0, keepdims=True)
            dxh = dy * gain
            proj = jnp.mean(dxh * xhat, axis=-1, keepdims=True)
            dh_ref[rows, :] = inv * (dxh - xhat * proj)
        dg_ref[...] = dg
        loss_ref[...] = loss

    return pl.pallas_call(
        body, name=name, grid=(T // tm,),
        in_specs=[pl.BlockSpec((tm, K), lambda m: (m, 0)),
                  pl.BlockSpec((K, D_MODEL), lambda m: (0, 0)),
                  pl.BlockSpec((tm, D_MODEL), lambda m: (m, 0)),
                  pl.BlockSpec((1, D_MODEL), lambda m: (0, 0)),
                  pl.BlockSpec((tm, D_MODEL), lambda m: (m, 0))],
        out_specs=[pl.BlockSpec((tm, D_MODEL), lambda m: (m, 0)),
                   pl.BlockSpec((1, D_MODEL), lambda m: (0, 0)),
                   pl.BlockSpec((1, 128), lambda m: (0, 0))],
        out_shape=[jax.ShapeDtypeStruct((T, D_MODEL), F32),
                   jax.ShapeDtypeStruct((1, D_MODEL), F32),
                   jax.ShapeDtypeStruct((1, 128), F32)],
        compiler_params=_params(1),
    )(a, w, res, g_row, target)


def _place():
    return lax.axis_index("x"), lax.axis_index("y"), lax.axis_index("c")


def _other_chips(x, y):
    return [(1 - x, y), (x, 1 - y), (1 - x, 1 - y)]


def _half(ref, c):
    hr = ref.shape[-2] // 2
    return pl.ds(pl.multiple_of(c * hr, 8), hr)


def _cast_to_slot(shard, chip, name, tr=256):
    R, C = shard.shape

    def body(chip_ref, w_ref, o_ref):
        o_ref[0] = w_ref[...].astype(BF16)

    return pl.pallas_call(
        body, name=name,
        grid_spec=pltpu.PrefetchScalarGridSpec(
            num_scalar_prefetch=1, grid=(R // tr,),
            in_specs=[pl.BlockSpec((tr, C), lambda i, chip_ref: (i, 0))],
            out_specs=pl.BlockSpec((1, tr, C), lambda i, chip_ref: (chip_ref[0], i, 0))),
        out_shape=jax.ShapeDtypeStruct((N_CHIPS, R, C), BF16),
        compiler_params=_params(1),
    )(chip, shard)


def _weight_plan(bufs):
    x, y, c = _place()
    plan = []
    for buf in bufs:
        mine = buf.at[2 * x + y, _half(buf, c)]
        for ox, oy in _other_chips(x, y):
            plan.append((mine, mine, (ox, oy, c), buf.at[2 * ox + oy, _half(buf, c)]))
    return plan


def _chip_sum_plan(bufs):
    x, y, c = _place()
    n = len(bufs) // 2
    plan = []
    for sums, land in zip(bufs[:n], bufs[n:]):
        for k, (ox, oy) in enumerate(_other_chips(x, y)):
            plan.append((sums.at[2 * ox + oy], land.at[k], (ox, oy, c), land.at[k]))
    return plan


SEM_SPEC = pl.BlockSpec(memory_space=pltpu.SEMAPHORE)
ANY_SPEC = pl.BlockSpec(memory_space=pl.ANY)
DATAFLOW = pltpu.SideEffectType.DATAFLOW_SIDE_EFFECTING


def _in_hbm(a):
    return pltpu.with_memory_space_constraint(a, pltpu.HBM)


def _exchange_start(bufs, after, plan, n_copies, name):
    nb = len(bufs)

    def body(*refs):
        send_sems, recv_sems = refs[nb + 1], refs[nb + 2]
        for i, (src, dst, dev, _) in enumerate(plan(refs[:nb])):
            pltpu.make_async_remote_copy(
                src_ref=src, dst_ref=dst, send_sem=send_sems.at[i], recv_sem=recv_sems.at[i],
                device_id=dev, device_id_type=MESH).start()
        token = refs[-1]
        token[...] = jnp.zeros_like(token)

    res = pl.pallas_call(
        body, name=name,
        in_specs=[HBM_SPEC] * nb + [ANY_SPEC],
        out_specs=[SEM_SPEC, SEM_SPEC] + [HBM_SPEC] * nb + [pl.BlockSpec(memory_space=pltpu.VMEM)],
        out_shape=[pltpu.SemaphoreType.DMA((n_copies,)), pltpu.SemaphoreType.DMA((n_copies,))]
        + [pltpu.HBM(b.shape, b.dtype) for b in bufs] + [jax.ShapeDtypeStruct((8, 128), F32)],
        input_output_aliases={i: 2 + i for i in range(nb)},
        compiler_params=pltpu.CompilerParams(has_side_effects=DATAFLOW),
    )(*[_in_hbm(b) for b in bufs], after)
    return res[0], res[1], list(res[2:2 + nb]), res[-1]


def _exchange_wait(bufs, send_sems, recv_sems, after, plan, name):
    nb = len(bufs)

    def body(*refs):
        sends, recvs = refs[nb], refs[nb + 1]
        for i, (src, dst, dev, landing) in enumerate(plan(refs[:nb])):
            pltpu.make_async_remote_copy(
                src_ref=src, dst_ref=landing, send_sem=sends.at[i], recv_sem=recvs.at[i],
                device_id=dev, device_id_type=MESH).wait()

    res = pl.pallas_call(
        body, name=name,
        in_specs=[HBM_SPEC] * nb + [SEM_SPEC, SEM_SPEC, ANY_SPEC],
        out_specs=[HBM_SPEC] * nb,
        out_shape=[pltpu.HBM(b.shape, b.dtype) for b in bufs],
        input_output_aliases={i: i for i in range(nb)},
        compiler_params=pltpu.CompilerParams(has_side_effects=DATAFLOW),
    )(*bufs, send_sems, recv_sems, after)
    return list(res)


def _allgather_weights(slots, name, landed=False):
    n = len(slots)

    def body(*refs):
        outs = refs[n:2 * n]
        send_sems, recv_sems, fwd_send, fwd_recv = refs[2 * n:]
        x, y, c = _place()
        chips = _other_chips(x, y)

        def landing(a, chip, half_of):
            return outs[a].at[2 * chip[0] + chip[1], _half(outs[a], half_of)]

        def ici(a, k, chip_from, to):
            return pltpu.make_async_remote_copy(
                src_ref=landing(a, chip_from, c), dst_ref=landing(a, chip_from, c),
                send_sem=send_sems.at[a, k], recv_sem=recv_sems.at[a, k],
                device_id=to, device_id_type=MESH)

        def d2d(a, k, chip_from, half_of):
            return pltpu.make_async_remote_copy(
                src_ref=landing(a, chip_from, half_of), dst_ref=landing(a, chip_from, half_of),
                send_sem=fwd_send.at[a, k], recv_sem=fwd_recv.at[a, k],
                device_id=(x, y, 1 - c), device_id_type=MESH)

        sends = []
        if not landed:
            sends = [ici(a, k, (x, y), (*chips[k], c)) for a in range(n) for k in range(3)]
        for cp in sends:
            cp.start()
        forwards = []
        for a in range(n):
            for k in range(3):
                if not landed:
                    ici(a, k, chips[k], (x, y, c)).wait_recv()
                fw = d2d(a, k, chips[k], c)
                fw.start()
                forwards.append(fw)
        for a in range(n):
            for k in range(3):
                d2d(a, k, chips[k], 1 - c).wait_recv()
        for cp in sends + forwards:
            cp.wait_send()

    return pl.pallas_call(
        body, name=name,
        in_specs=[HBM_SPEC] * n, out_specs=[HBM_SPEC] * n,
        out_shape=[jax.ShapeDtypeStruct(s.shape, s.dtype) for s in slots],
        input_output_aliases={a: a for a in range(n)},
        scratch_shapes=[pltpu.SemaphoreType.DMA((n, 3)), pltpu.SemaphoreType.DMA((n, 3)),
                        pltpu.SemaphoreType.DMA((n, 3)), pltpu.SemaphoreType.DMA((n, 3))],
    )(*slots)


def _sibling_exchange(partials, small, name):
    n = len(partials)
    ns = 0 if small is None else 1

    def body(*refs):
        ins, outs = refs[:n], refs[n + ns:2 * n + ns]
        send_sems, recv_sems = refs[2 * (n + ns):2 * (n + ns) + 2]
        x, y, c = _place()
        me = 4 * x + 2 * y + c
        sends = [pltpu.make_async_remote_copy(
            src_ref=ins[a].at[:, _half(ins[a], 1 - c)], dst_ref=outs[a],
            send_sem=send_sems.at[a], recv_sem=recv_sems.at[a],
            device_id=(x, y, 1 - c), device_id_type=MESH) for a in range(n)]
        if ns:
            small_ref, small_all = refs[n], refs[2 * n + 1]
            s_send, s_recv, loc_sem = refs[2 * (n + ns) + 2:]
            local = pltpu.make_async_copy(small_ref, small_all.at[me], loc_sem)
            local.start()
            for d in range(1, N_DEV):
                px, py, pc = x ^ ((d >> 2) & 1), y ^ ((d >> 1) & 1), c ^ (d & 1)
                sends.append(pltpu.make_async_remote_copy(
                    src_ref=small_ref, dst_ref=small_all.at[me],
                    send_sem=s_send.at[d - 1], recv_sem=s_recv.at[d - 1],
                    device_id=(px, py, pc), device_id_type=MESH))
        for cp in sends:
            cp.start()
        if ns:
            for d in range(1, N_DEV):
                pltpu.make_async_remote_copy(
                    src_ref=small_ref, dst_ref=small_all.at[me ^ d],
                    send_sem=s_send.at[d - 1], recv_sem=s_recv.at[d - 1],
                    device_id=(x, y, c), device_id_type=MESH).wait_recv()
        for cp in sends[:n]:
            cp.wait_recv()
        for cp in sends:
            cp.wait_send()
        if ns:
            local.wait()

    out_shape = [jax.ShapeDtypeStruct((N_CHIPS, p.shape[1] // 2, p.shape[2]), F32) for p in partials]
    scratch = [pltpu.SemaphoreType.DMA((max(n, 1),)), pltpu.SemaphoreType.DMA((max(n, 1),))]
    if ns:
        out_shape.append(jax.ShapeDtypeStruct((N_DEV,) + small.shape, F32))
        scratch += [pltpu.SemaphoreType.DMA((N_DEV - 1,)), pltpu.SemaphoreType.DMA((N_DEV - 1,)),
                    pltpu.SemaphoreType.DMA]
    return pl.pallas_call(
        body, name=name,
        in_specs=[HBM_SPEC] * (n + ns), out_specs=[HBM_SPEC] * (n + ns),
        out_shape=out_shape, scratch_shapes=scratch,
    )(*partials, *([small] if ns else []))


def _chip_sum(partial, from_sibling, c, name, tr=256):
    _, hr, C = from_sibling.shape
    nb = hr // tr

    def body(c_ref, p_ref, s_ref, o_ref):
        o_ref[...] = (p_ref[...] + s_ref[...]).astype(BF16)

    return pl.pallas_call(
        body, name=name,
        grid_spec=pltpu.PrefetchScalarGridSpec(
            num_scalar_prefetch=1, grid=(N_CHIPS, nb),
            in_specs=[pl.BlockSpec((1, tr, C), lambda j, i, c_ref: (j, c_ref[0] * nb + i, 0)),
                      pl.BlockSpec((1, tr, C), lambda j, i, c_ref: (j, i, 0))],
            out_specs=pl.BlockSpec((1, tr, C), lambda j, i, c_ref: (j, i, 0))),
        out_shape=jax.ShapeDtypeStruct(from_sibling.shape, BF16),
        compiler_params=_params(2),
    )(c, partial, from_sibling)


def _reduce_half(partial, from_sibling, received, place, name, tr=256):
    _, hr, C = from_sibling.shape
    nb = hr // tr

    def body(p_ref, mine_ref, sib_ref, r_ref, o_ref):
        acc = mine_ref[0] + sib_ref[0]
        for k in range(3):
            acc = acc + r_ref[k].astype(F32)
        o_ref[...] = acc

    return pl.pallas_call(
        body, name=name,
        grid_spec=pltpu.PrefetchScalarGridSpec(
            num_scalar_prefetch=1, grid=(nb,),
            in_specs=[pl.BlockSpec((1, tr, C), lambda i, p: (p[0], p[1] * nb + i, 0)),
                      pl.BlockSpec((1, tr, C), lambda i, p: (p[0], i, 0)),
                      pl.BlockSpec((3, tr, C), lambda i, p: (0, i, 0))],
            out_specs=pl.BlockSpec((tr, C), lambda i, p: (p[1] * nb + i, 0))),
        out_shape=jax.ShapeDtypeStruct((2 * hr, C), F32),
        compiler_params=_params(1),
    )(place, partial, from_sibling, received)


def _join_halves(fulls, name):
    n = len(fulls)

    def body(*refs):
        outs = refs[n:2 * n]
        send_sems, recv_sems = refs[2 * n:]
        x, y, c = _place()

        def copy(a, half_of, to):
            rows = outs[a].at[_half(outs[a], half_of)]
            return pltpu.make_async_remote_copy(
                src_ref=rows, dst_ref=rows, send_sem=send_sems.at[a], recv_sem=recv_sems.at[a],
                device_id=to, device_id_type=MESH)

        sends = [copy(a, c, (x, y, 1 - c)) for a in range(n)]
        for cp in sends:
            cp.start()
        for a in range(n):
            copy(a, 1 - c, (x, y, c)).wait_recv()
        for cp in sends:
            cp.wait_send()

    return pl.pallas_call(
        body, name=name,
        in_specs=[HBM_SPEC] * n, out_specs=[HBM_SPEC] * n,
        out_shape=[jax.ShapeDtypeStruct(f.shape, F32) for f in fulls],
        input_output_aliases={a: a for a in range(n)},
        scratch_shapes=[pltpu.SemaphoreType.DMA((n,)), pltpu.SemaphoreType.DMA((n,))],
    )(*fulls)


def _adamw_math(w, g, m, v):
    m = ADAM_B1 * m + (1.0 - ADAM_B1) * g
    v = ADAM_B2 * v + (1.0 - ADAM_B2) * (g * g)
    m_hat = m / (1.0 - ADAM_B1 ** ADAM_STEP)
    v_hat = v / (1.0 - ADAM_B2 ** ADAM_STEP)
    delta = -ADAM_LR * (m_hat / (jnp.sqrt(v_hat) + ADAM_EPS) + ADAM_WD * w)
    return delta, m, v


def _adamw(w, g, m, v, name, tr=256):
    R, C = w.shape
    tr = min(tr, R)

    def body(w_ref, g_ref, m_ref, v_ref, d_out, m_out, v_out):
        d_out[...], m_out[...], v_out[...] = _adamw_math(w_ref[...], g_ref[...], m_ref[...], v_ref[...])

    spec = pl.BlockSpec((tr, C), lambda i: (i, 0))
    return pl.pallas_call(
        body, name=name, grid=(R // tr,),
        in_specs=[spec] * 4, out_specs=[spec] * 3,
        out_shape=[jax.ShapeDtypeStruct((R, C), F32)] * 3,
        compiler_params=_params(1),
    )(w, g, m, v)


def _adamw_small(small_all, w, m, v, name):
    def body(s_ref, w_ref, m_ref, v_ref, g_out, d_out, m_out, v_out):
        g = s_ref[0]
        for d in range(1, N_DEV):
            g = g + s_ref[d]
        g_out[...] = g
        d_out[...], m_out[...], v_out[...] = _adamw_math(w_ref[...], g, m_ref[...], v_ref[...])

    vm = pl.BlockSpec(memory_space=pltpu.VMEM)
    return pl.pallas_call(
        body, name=name, in_specs=[vm] * 4, out_specs=[vm] * 4,
        out_shape=[jax.ShapeDtypeStruct(w.shape, F32)] * 4,
    )(small_all, w, m, v)


def _pack_small(norm_g, pool_scale, norm_f, extra_row):
    return jnp.concatenate([norm_g.reshape(2, D_MODEL), pool_scale.reshape(2, D_MODEL),
                            norm_f.reshape(1, D_MODEL), extra_row,
                            jnp.zeros((2, D_MODEL), F32)], axis=0)


def kernel(x, norm_g, pool_w_in, pool_w, pool_scale, pool_w_out, sb_w_in, sb_w_out, norm_f, loss_target, m_norm_g, m_pool_w_in, m_pool_w, m_pool_scale, m_pool_w_out, m_sb_w_in, m_sb_w_out, m_norm_f, v_norm_g, v_pool_w_in, v_pool_w, v_pool_scale, v_pool_w_out, v_sb_w_in, v_sb_w_out, v_norm_f):
    nb, S, _ = x.shape
    T = nb * S
    xt = x.reshape(T, D_MODEL)
    target = loss_target.reshape(T, D_MODEL)
    cx, cy, cc = _place()

    def shard2d(w):
        return w.reshape(-1, w.shape[-1])

    names = ("pool_w_in", "pool_w", "pool_w_out", "sb_w_in", "sb_w_out")
    w_shards = [shard2d(w) for w in (pool_w_in, pool_w, pool_w_out, sb_w_in, sb_w_out)]
    m_shards = [shard2d(w) for w in (m_pool_w_in, m_pool_w, m_pool_w_out, m_sb_w_in, m_sb_w_out)]
    v_shards = [shard2d(w) for w in (v_pool_w_in, v_pool_w, v_pool_w_out, v_sb_w_in, v_sb_w_out)]

    chip = (2 * cx + cy).reshape(1).astype(jnp.int32)
    c_arr = cc.reshape(1).astype(jnp.int32)
    place = jnp.stack([2 * cx + cy, cc]).astype(jnp.int32)
    slots = [_cast_to_slot(w, chip, "cast_" + nm) for w, nm in zip(w_shards, names)]
    g0, g1, gf = norm_g[0:1], norm_g[1:2], norm_f.reshape(1, D_MODEL)

    w_pin, = _allgather_weights(slots[:1], "allgather_pool_in_weights")
    mix_send, mix_recv, mix_slots, token = _exchange_start(slots[1:3], w_pin, _weight_plan, 6,
                                                           "pool_weights_start")
    sb_send, sb_recv, sb_slots, token = _exchange_start(slots[3:], token, _weight_plan, 6,
                                                        "sb_weights_start")

    proj0, u0 = _rms_matmul(xt, g0 + token[0:1, 0:1], w_pin, [(2 * D_INNER, BF16)], "pool_in_proj",
                            tn=w_pin.shape[2])
    mix_slots = _exchange_wait(mix_slots, mix_send, mix_recv, proj0, _weight_plan, "pool_weights_wait")
    w_g, w_pout = _allgather_weights(mix_slots, "pool_weights_forward", landed=True)
    w_pout = w_pout.reshape(D_INNER, D_MODEL)
    y0, pooled, mixed = _pool_fwd(proj0, w_g, pool_scale, S, "pool_mix")
    sb_slots = _exchange_wait(sb_slots, sb_send, sb_recv, y0, _weight_plan, "sb_weights_wait")
    w_sin, w_sout = _allgather_weights(sb_slots, "sb_weights_forward", landed=True)
    w_sout = w_sout.reshape(D_INNER, D_MODEL)
    h1 = _matmul_residual(y0, w_pout, xt, "pool_out_proj")
    n1 = 2 * QK_WIDTH + 2 * D_INNER
    qkvz, u1 = _rms_matmul(h1, g1, w_sin, [(n1, BF16)], "sb_in_proj", tn=w_sin.shape[2])
    o, y1, ltot = _attn_fwd(qkvz, S, "sb_attention")
    dh2, d_norm_f, loss_row = _out_proj_loss_head(y1, w_sout, h1, gf, target, "sb_out_proj_loss_head")

    def reduce_start(partials, tag):
        from_sibling = _sibling_exchange(partials, None, "grad_sibling_exchange_" + tag)
        sums = [_chip_sum(p, s, c_arr, "grad_chip_sum_%s_%d" % (tag, i))
                for i, (p, s) in enumerate(zip(partials, from_sibling))]
        lands = [lax.empty((3,) + s.shape[1:], BF16) for s in sums]
        send, recv, bufs, token = _exchange_start(sums + lands, c_arr, _chip_sum_plan, 3 * len(sums),
                                                  "grad_chip_exchange_start_" + tag)
        return (partials, list(from_sibling), send, recv, bufs), token[0:1, 0:1]

    def reduce_finish(started, after, tag):
        partials, from_sibling, send, recv, bufs = started
        received = _exchange_wait(bufs, send, recv, after, _chip_sum_plan,
                                  "grad_chip_exchange_wait_" + tag)[len(partials):]
        return [_reduce_half(p, s, r, place, "grad_reduce_%s_%d" % (tag, i))
                for i, (p, s, r) in enumerate(zip(partials, from_sibling, received))]

    shard = lambda i, j, t: (j, 0, 0)
    gw_sout = _matmul_tn(y1, dh2, D_INNER, D_MODEL, (D_INNER, D_MODEL), (1024, 1024),
                         lambda i, j, t: (i, j), "grad_sb_w_out", bm=1024, bn=1024)
    sout_started, token = reduce_start([gw_sout.reshape(N_CHIPS, -1, D_MODEL)], "sb_out")
    do, dproj1 = _attn_gate_bwd(dh2, w_sout, qkvz, o, token, "sb_gate_bwd")
    dproj1 = _attn_bwd(qkvz, do, ltot, dproj1, S, "sb_attention_bwd")
    gw_sin = _matmul_tn(u1, dproj1, D_MODEL, n1, (N_CHIPS, D_MODEL, n1 // 4), (1, D_MODEL, n1 // 4),
                        shard, "grad_sb_w_in", bm=D_MODEL, bn=n1 // 4)
    sin_started, token = reduce_start([gw_sin], "sb_in")
    dh1, d_g1 = _matmul_nt_rms_bwd(dproj1, w_sin, h1, g1 + token, dh2, "sb_in_bwd", tk=w_sin.shape[2])
    gw_pout = _matmul_tn(y0, dh1, D_INNER, D_MODEL, (D_INNER, D_MODEL), (1024, 1024),
                         lambda i, j, t: (i, j), "grad_pool_w_out", bm=1024, bn=1024)
    dmixed, dproj0, d_scale = _pool_gate_bwd(dh1, w_pout, proj0, mixed, pool_scale, "pool_gate_bwd")
    gw_g = _matmul_tn(pooled, dmixed, D_INNER, D_INNER, (N_CHIPS, GROUP_DIM, GROUP_DIM),
                      (N_CHIPS, GROUP_DIM // N_CHIPS, GROUP_DIM), lambda i, j, t: (0, i, 0),
                      "grad_pool_w", bm=GROUP_DIM, bn=GROUP_DIM, diagonal_blocks=True)
    mix_started, token = reduce_start([gw_g, gw_pout.reshape(N_CHIPS, -1, D_MODEL)], "pool_mix")
    dproj0 = _pool_bwd(dmixed, w_g, dproj0, token, S, "pool_bwd")
    n0 = 2 * D_INNER
    gw_pin = _matmul_tn(u0, dproj0, D_MODEL, n0, (N_CHIPS, D_MODEL, n0 // 4), (1, D_MODEL, n0 // 4),
                        shard, "grad_pool_w_in", bm=D_MODEL, bn=n0 // 4)
    pin_started, token = reduce_start([gw_pin], "pool_in")
    dx, d_g0 = _matmul_nt_rms_bwd(dproj0, w_pin, xt, g0 + token, dh1, "pool_in_bwd", tk=w_pin.shape[2])

    small = _pack_small(jnp.concatenate([d_g0, d_g1], axis=0), d_scale, d_norm_f,
                        jnp.broadcast_to(loss_row[:, :1], (1, D_MODEL)))
    small_all, = _sibling_exchange([], small, "small_sums_exchange")
    grads = _join_halves(reduce_finish(pin_started, dx, "pool_in")
                         + reduce_finish(mix_started, dx, "pool_mix")
                         + reduce_finish(sin_started, dx, "sb_in")
                         + reduce_finish(sout_started, dx, "sb_out"), "grad_join_halves")

    deltas, new_m, new_v = [], [], []
    for w, g, m, v, nm in zip(w_shards, grads, m_shards, v_shards, names):
        d, mm, vv = _adamw(w, g, m, v, "adamw_" + nm)
        deltas.append(d)
        new_m.append(mm)
        new_v.append(vv)

    zero_row = jnp.zeros((1, D_MODEL), F32)
    g_small, d_small, m_small, v_small = _adamw_small(
        small_all, _pack_small(norm_g, pool_scale, norm_f, zero_row),
        _pack_small(m_norm_g, m_pool_scale, m_norm_f, zero_row),
        _pack_small(v_norm_g, v_pool_scale, v_norm_f, zero_row + 1.0), "adamw_small")
    loss = g_small[5, 0]

    def unpack_small(a):
        return a[0:2], a[2:4].reshape(1, D_INNER), a[4]

    def assemble(big, small3):
        ng, ps, nf = small3
        return [ng, big[0].reshape(pool_w_in.shape), big[1].reshape(pool_w.shape), ps,
                big[2].reshape(pool_w_out.shape), big[3].reshape(sb_w_in.shape),
                big[4].reshape(sb_w_out.shape), nf]

    return (loss, dx.reshape(x.shape),
            *assemble(grads, unpack_small(g_small)),
            *assemble(deltas, unpack_small(d_small)),
            *assemble(new_m, unpack_small(m_small)),
            *assemble(new_v, unpack_small(v_small)))
```

```python
import jax
import jax.numpy as jnp
from jax import lax
from jax.experimental import pallas as pl
from jax.experimental.pallas import tpu as pltpu

F32 = jnp.float32
BF16 = jnp.bfloat16
MESH = pl.DeviceIdType.MESH

D_MODEL = 1024
D_INNER = 2048
N_GROUPS = 4
GROUP_DIM = 512
HEAD_PAIR_QK = 128
HEAD_V = 128
QK_WIDTH = 1024
RMS_EPS = 1e-6
HALO = 16
N_CHIPS = 4
N_DEV = 8

ADAM_LR = 0.001
ADAM_B1 = 0.9
ADAM_B2 = 0.999
ADAM_EPS = 1e-08
ADAM_WD = 0.01
ADAM_STEP = 10

VMEM_LIMIT = 56 * 1024 * 1024

HBM_SPEC = pl.BlockSpec(memory_space=pltpu.HBM)


def _params(n_axes):
    return pltpu.CompilerParams(dimension_semantics=("arbitrary",) * n_axes,
                                vmem_limit_bytes=VMEM_LIMIT)


def _dot(a, b):
    return jnp.dot(a, b, preferred_element_type=F32)


def _dot_nt(a, b):
    return lax.dot_general(a, b, (((1,), (1,)), ((), ())), preferred_element_type=F32)


def _dot_tn(a, b):
    return lax.dot_general(a, b, (((0,), (0,)), ((), ())), preferred_element_type=F32)


def _sigmoid(z):
    return 1.0 / (1.0 + jnp.exp(-z))


def _row_blocks(tm, rows=256):
    return [slice(r, r + rows) for r in range(0, tm, rows)]


def _rms_matmul(h, g_row, w4, outs, name, tm=1024, tn=512):
    T = h.shape[0]
    per_shard = w4.shape[2] // tn
    starts = [0]
    for width, _ in outs:
        starts.append(starts[-1] + width // tn)
    n_out = len(outs)

    def body(h_ref, g_ref, w_ref, *rest):
        o_refs, u_out, u_s = rest[:n_out], rest[n_out], rest[n_out + 1]
        n = pl.program_id(1)

        @pl.when(n == 0)
        def _():
            x = h_ref[...]
            inv = lax.rsqrt(jnp.mean(x * x, axis=-1, keepdims=True) + RMS_EPS)
            u = (x * inv * g_ref[...]).astype(BF16)
            u_s[...] = u
            u_out[...] = u

        res = _dot(u_s[...], w_ref[0])
        for k in range(n_out):
            @pl.when((n >= starts[k]) & (n < starts[k + 1]))
            def _():
                o_refs[k][...] = res.astype(o_refs[k].dtype)

    def out_map(k):
        return lambda m, n: (m, jnp.clip(n - starts[k], 0, starts[k + 1] - starts[k] - 1))

    return pl.pallas_call(
        body, name=name, grid=(T // tm, starts[-1]),
        in_specs=[pl.BlockSpec((tm, D_MODEL), lambda m, n: (m, 0)),
                  pl.BlockSpec((1, D_MODEL), lambda m, n: (0, 0)),
                  pl.BlockSpec((1, D_MODEL, tn), lambda m, n: (n // per_shard, 0, n % per_shard))],
        out_specs=[pl.BlockSpec((tm, tn), out_map(k)) for k in range(n_out)]
        + [pl.BlockSpec((tm, D_MODEL), lambda m, n: (m, 0))],
        out_shape=[jax.ShapeDtypeStruct((T, width), dt) for width, dt in outs]
        + [jax.ShapeDtypeStruct((T, D_MODEL), BF16)],
        scratch_shapes=[pltpu.VMEM((tm, D_MODEL), BF16)],
        compiler_params=_params(2),
    )(h, g_row, w4)


def _matmul_residual(a, w, res, name, tm=1024, tn=1024):
    T, K = a.shape
    N = w.shape[1]

    def body(a_ref, w_ref, r_ref, o_ref):
        o_ref[...] = r_ref[...] + _dot(a_ref[...], w_ref[...])

    return pl.pallas_call(
        body, name=name, grid=(T // tm, N // tn),
        in_specs=[pl.BlockSpec((tm, K), lambda m, n: (m, 0)),
                  pl.BlockSpec((K, tn), lambda m, n: (0, n)),
                  pl.BlockSpec((tm, tn), lambda m, n: (m, n))],
        out_specs=pl.BlockSpec((tm, tn), lambda m, n: (m, n)),
        out_shape=jax.ShapeDtypeStruct((T, N), F32),
        compiler_params=_params(2),
    )(a, w, res)


def _matmul_tn(a, b, a_cols, b_cols, out_shape, out_block, out_map, name, bm, bn, tk=2048,
               diagonal_blocks=False):
    T = a.shape[0]
    tk = min(tk, T)

    def body(a_ref, b_ref, o_ref):
        @pl.when(pl.program_id(2) == 0)
        def _():
            o_ref[...] = jnp.zeros_like(o_ref)

        part = _dot_tn(a_ref[...].astype(BF16), b_ref[...].astype(BF16))
        o_ref[...] += part.reshape(o_ref.shape)

    b_map = (lambda i, j, t: (t, i)) if diagonal_blocks else (lambda i, j, t: (t, j))
    return pl.pallas_call(
        body, name=name, grid=(a_cols // bm, 1 if diagonal_blocks else b_cols // bn, T // tk),
        in_specs=[pl.BlockSpec((tk, bm), lambda i, j, t: (t, i)),
                  pl.BlockSpec((tk, bn), b_map)],
        out_specs=pl.BlockSpec(out_block, out_map),
        out_shape=jax.ShapeDtypeStruct(out_shape, F32),
        compiler_params=_params(3),
    )(a, b)


def _matmul_nt_rms_bwd(dproj, w4, h, g_row, dres, name, tm=1024, tk=512):
    T, cols = dproj.shape
    per_shard = w4.shape[2] // tk
    nk = cols // tk

    def body(dp_ref, w_ref, h_ref, g_ref, r_ref, dx_ref, dg_ref, acc):
        m, k = pl.program_id(0), pl.program_id(1)

        @pl.when(k == 0)
        def _():
            acc[...] = jnp.zeros_like(acc)

        @pl.when((k == 0) & (m == 0))
        def _():
            dg_ref[...] = jnp.zeros_like(dg_ref)

        acc[...] += _dot_nt(dp_ref[...], w_ref[0])

        @pl.when(k == nk - 1)
        def _():
            du = acc[...]
            x = h_ref[...]
            inv = lax.rsqrt(jnp.mean(x * x, axis=-1, keepdims=True) + RMS_EPS)
            xhat = x * inv
            dg_ref[...] += jnp.sum(du * xhat, axis=0, keepdims=True)
            dxh = du * g_ref[...]
            proj = jnp.mean(dxh * xhat, axis=-1, keepdims=True)
            dx_ref[...] = r_ref[...] + inv * (dxh - xhat * proj)

    return pl.pallas_call(
        body, name=name, grid=(T // tm, nk),
        in_specs=[pl.BlockSpec((tm, tk), lambda m, k: (m, k)),
                  pl.BlockSpec((1, D_MODEL, tk), lambda m, k: (k // per_shard, 0, k % per_shard)),
                  pl.BlockSpec((tm, D_MODEL), lambda m, k: (m, 0)),
                  pl.BlockSpec((1, D_MODEL), lambda m, k: (0, 0)),
                  pl.BlockSpec((tm, D_MODEL), lambda m, k: (m, 0))],
        out_specs=[pl.BlockSpec((tm, D_MODEL), lambda m, k: (m, 0)),
                   pl.BlockSpec((1, D_MODEL), lambda m, k: (0, 0))],
        out_shape=[jax.ShapeDtypeStruct((T, D_MODEL), F32),
                   jax.ShapeDtypeStruct((1, D_MODEL), F32)],
        scratch_shapes=[pltpu.VMEM((tm, D_MODEL), F32)],
        compiler_params=_params(2),
    )(dproj, w4, h, g_row, dres)


def _window_of(g):
    return jnp.left_shift(2, g)


def _select_stage(g, stages):
    res = stages[0]
    for i in range(1, len(stages)):
        res = jnp.where(g >= i, stages[i], res)
    return res


def _pool_fwd(proj0, wg4, scale_row, S, name, tm=1024):
    T = proj0.shape[0]
    tm = min(tm, S)
    blocks_per_seq = S // tm
    hb = tm // HALO

    def body(x_ref, halo_ref, z_ref, w_ref, s_ref, y_ref, p_ref, mix_ref):
        m, g = pl.program_id(0), pl.program_id(1)
        first = (m % blocks_per_seq) == 0
        halo = jnp.where(first, 0.0, halo_ref[...].astype(F32))
        x = x_ref[...].astype(F32)
        ext = jnp.concatenate([halo, x], axis=0)
        stages = []
        cur = ext
        for sh in (1, 2, 4, 8):
            cur = cur + pltpu.roll(cur, sh, 0)
            stages.append(cur[HALO:, :])
        win_sum = _select_stage(g, stages)
        pos = (m % blocks_per_seq) * tm + lax.broadcasted_iota(jnp.int32, (tm, 1), 0)
        count = jnp.minimum(pos + 1, _window_of(g)).astype(F32)
        p_ref[...] = (win_sum / count - x).astype(BF16)
        w = w_ref[...].reshape(GROUP_DIM, GROUP_DIM)
        for rows in _row_blocks(tm):
            mixed = _dot(p_ref[rows, :], w)
            z = z_ref[rows, :].astype(F32)
            y_ref[rows, :] = (mixed * s_ref[...] * (z * _sigmoid(z))).astype(BF16)
            mix_ref[rows, :] = mixed.astype(BF16)

    blk = lambda m, g: (m, g)
    return pl.pallas_call(
        body, name=name, grid=(T // tm, N_GROUPS),
        in_specs=[pl.BlockSpec((tm, GROUP_DIM), blk),
                  pl.BlockSpec((HALO, GROUP_DIM), lambda m, g: (jnp.maximum(m * hb - 1, 0), g)),
                  pl.BlockSpec((tm, GROUP_DIM), lambda m, g: (m, N_GROUPS + g)),
                  pl.BlockSpec((N_CHIPS, GROUP_DIM // N_CHIPS, GROUP_DIM), lambda m, g: (0, g, 0)),
                  pl.BlockSpec((1, GROUP_DIM), lambda m, g: (0, g))],
        out_specs=[pl.BlockSpec((tm, GROUP_DIM), blk)] * 3,
        out_shape=[jax.ShapeDtypeStruct((T, D_INNER), BF16)] * 3,
        compiler_params=_params(2),
    )(proj0, proj0, proj0, wg4, scale_row)


def _pool_gate_bwd(dh, w_out, proj0, mixed, scale_row, name, tm=1024, tn=512):
    T = dh.shape[0]
    gate_b0 = D_INNER // tn

    def body(dh_ref, w_ref, z_ref, mix_ref, s_ref, dm_ref, dz_ref, ds_ref, dh_s):
        m, n = pl.program_id(0), pl.program_id(1)

        @pl.when((m == 0) & (n == 0))
        def _():
            ds_ref[...] = jnp.zeros_like(ds_ref)

        @pl.when(n == 0)
        def _():
            dh_s[...] = dh_ref[...].astype(BF16)

        cols = pl.ds(pl.multiple_of(n * tn, tn), tn)
        s = s_ref[...]
        ds = ds_ref[:, cols]
        for rows in _row_blocks(tm):
            dy = _dot_nt(dh_s[rows, :], w_ref[...])
            z = z_ref[rows, :].astype(F32)
            sig = _sigmoid(z)
            silu = z * sig
            mixed = mix_ref[rows, :].astype(F32)
            dm_ref[rows, :] = (dy * s * silu).astype(BF16)
            dz_ref[rows, :] = (dy * mixed * s * (sig * (1.0 + z * (1.0 - sig)))).astype(BF16)
            ds = ds + jnp.sum(dy * mixed * silu, axis=0, keepdims=True)
        ds_ref[:, cols] = ds

    return pl.pallas_call(
        body, name=name, grid=(T // tm, D_INNER // tn),
        in_specs=[pl.BlockSpec((tm, D_MODEL), lambda m, n: (m, 0)),
                  pl.BlockSpec((tn, D_MODEL), lambda m, n: (n, 0)),
                  pl.BlockSpec((tm, tn), lambda m, n: (m, gate_b0 + n)),
                  pl.BlockSpec((tm, tn), lambda m, n: (m, n)),
                  pl.BlockSpec((1, tn), lambda m, n: (0, n))],
        out_specs=[pl.BlockSpec((tm, tn), lambda m, n: (m, n)),
                   pl.BlockSpec((tm, tn), lambda m, n: (m, gate_b0 + n)),
                   pl.BlockSpec((1, D_INNER), lambda m, n: (0, 0))],
        out_shape=[jax.ShapeDtypeStruct((T, D_INNER), BF16),
                   jax.ShapeDtypeStruct((T, 2 * D_INNER), BF16),
                   jax.ShapeDtypeStruct((1, D_INNER), F32)],
        scratch_shapes=[pltpu.VMEM((tm, D_MODEL), BF16)],
        compiler_params=_params(2),
    )(dh, w_out, proj0, mixed, scale_row)


def _pool_bwd(dmixed, wg4, dproj0, after, S, name, tm=1024):
    T = dmixed.shape[0]
    tm = min(tm, S)
    blocks_per_seq = S // tm
    hb = tm // HALO
    n_halo_blocks = T // HALO

    def body(dm_ref, halo_ref, w_ref, _, __, o_ref):
        m, g = pl.program_id(0), pl.program_id(1)
        ext = jnp.concatenate([dm_ref[...], halo_ref[...]], axis=0)
        dp = _dot_nt(ext, w_ref[...].reshape(GROUP_DIM, GROUP_DIM))
        pos = (m % blocks_per_seq) * tm + lax.broadcasted_iota(jnp.int32, (tm + HALO, 1), 0)
        count = jnp.minimum(pos + 1, _window_of(g)).astype(F32)
        c = jnp.where(pos < S, dp / count, 0.0)
        n = tm + HALO
        stages = []
        cur = c
        for sh in (1, 2, 4, 8):
            cur = cur + pltpu.roll(cur, n - sh, 0)
            stages.append(cur[:tm, :])
        o_ref[...] = (_select_stage(g, stages) - dp[:tm, :]).astype(BF16)

    blk = lambda m, g: (m, g)
    return pl.pallas_call(
        body, name=name, grid=(T // tm, N_GROUPS),
        in_specs=[pl.BlockSpec((tm, GROUP_DIM), blk),
                  pl.BlockSpec((HALO, GROUP_DIM),
                               lambda m, g: (jnp.minimum((m + 1) * hb, n_halo_blocks - 1), g)),
                  pl.BlockSpec((N_CHIPS, GROUP_DIM // N_CHIPS, GROUP_DIM), lambda m, g: (0, g, 0)),
                  HBM_SPEC, ANY_SPEC],
        out_specs=pl.BlockSpec((tm, GROUP_DIM), blk),
        out_shape=jax.ShapeDtypeStruct(dproj0.shape, dproj0.dtype),
        input_output_aliases={3: 0},
        compiler_params=_params(2),
    )(dmixed, dmixed, wg4, dproj0, after)


TQ = 256


def _split_dot(x, m):
    hi = x.astype(BF16)
    lo = (x - hi.astype(F32)).astype(BF16)
    return _dot(hi, m) + _dot(lo, m)


NEG_LOG2E = -1.4426950408889634


def _log_terms(z):
    soft = jnp.log(1.0 + jnp.exp2(jnp.abs(z) * NEG_LOG2E))
    log_beta = jnp.minimum(z, 0.0) - soft
    return log_beta, log_beta - z


N_HEADS = 16
FWD_HEADS = BWD_HEADS = 4


def _masked_heads(x, heads):
    lane = lax.broadcasted_iota(jnp.int32, (1, HEAD_PAIR_QK), 1)
    out = []
    for hh in range(heads):
        slab = x[:, (hh // 2) * HEAD_PAIR_QK:(hh // 2 + 1) * HEAD_PAIR_QK]
        out.append(jnp.where((lane // 64) == hh % 2, slab, jnp.zeros_like(slab)))
    return out


def _attn_fwd(qkvz, S, name):
    T = qkvz.shape[0]
    nq = S // TQ
    HEADS, QK_W, V_W = FWD_HEADS, FWD_HEADS * 64, FWD_HEADS * HEAD_V
    k_b0 = QK_WIDTH // QK_W
    v_b0 = 2 * QK_WIDTH // V_W
    z_b0 = (2 * QK_WIDTH + D_INNER) // V_W
    hs = range(HEADS)

    def body(q_ref, k_ref, v_ref, z_ref, o_ref, y_ref, lt_ref):
        row = lax.broadcasted_iota(jnp.int32, (TQ, TQ), 0)
        col = lax.broadcasted_iota(jnp.int32, (TQ, TQ), 1)
        causal = col < row
        later_in_block = (row > col).astype(BF16)
        lax.fori_loop(0, nq, lambda qi, _: q_block(qi, causal, later_in_block,
                                                   q_ref, k_ref, v_ref, z_ref, o_ref, y_ref, lt_ref), 0)

    def q_block(qi, causal, later_in_block, q_ref, k_ref, v_ref, z_ref, o_ref, y_ref, lt_ref):
        rows = pl.ds(pl.multiple_of(qi * TQ, TQ), TQ)
        qms = [qm * 0.125 for qm in _masked_heads(q_ref[rows, :], HEADS)]

        def step(j, carry, diagonal):
            koff = pl.multiple_of(j * TQ, TQ)
            kbs = [k_ref[pl.ds(koff, TQ), p * HEAD_PAIR_QK:(p + 1) * HEAD_PAIR_QK]
                   for p in range(HEADS // 2)]
            run, acc = [carry[2 * hh] for hh in hs], [carry[2 * hh + 1] for hh in hs]
            z = [_dot_nt(qms[hh], kbs[hh // 2]) for hh in hs]
            terms = [_log_terms(z[hh]) for hh in hs]
            log_om = [jnp.where(causal, t[1], 0.0) if diagonal else t[1] for t in terms]
            later = [_split_dot(log_om[hh], later_in_block) for hh in hs]
            a = [jnp.exp(terms[hh][0] + (run[hh] + later[hh])) for hh in hs]
            if diagonal:
                a = [jnp.where(causal, a[hh], 0.0) for hh in hs]
            out = []
            for hh in hs:
                vb = v_ref[pl.ds(koff, TQ), hh * HEAD_V:(hh + 1) * HEAD_V]
                out += [run[hh] + jnp.sum(log_om[hh], axis=1, keepdims=True),
                        acc[hh] + _dot(a[hh].astype(BF16), vb)]
            return tuple(out)

        zero = (jnp.zeros((TQ, 1), F32), jnp.zeros((TQ, HEAD_V), F32))
        carry = step(qi, zero * HEADS, True)
        carry = lax.fori_loop(0, qi, lambda i, c: step(qi - 1 - i, c, False), carry)
        for hh in hs:
            sl = slice(hh * HEAD_V, (hh + 1) * HEAD_V)
            acc = carry[2 * hh + 1]
            z = z_ref[rows, sl].astype(F32)
            o_ref[rows, sl] = acc.astype(BF16)
            y_ref[rows, sl] = (acc * (z * _sigmoid(z))).astype(BF16)
            lt_ref[rows, hh:hh + 1] = carry[2 * hh]
        return 0

    blk = lambda b, p: (b, p)
    return pl.pallas_call(
        body, name=name, grid=(T // S, N_HEADS // HEADS),
        in_specs=[pl.BlockSpec((S, QK_W), blk),
                  pl.BlockSpec((S, QK_W), lambda b, p: (b, k_b0 + p)),
                  pl.BlockSpec((S, V_W), lambda b, p: (b, v_b0 + p)),
                  pl.BlockSpec((S, V_W), lambda b, p: (b, z_b0 + p))],
        out_specs=[pl.BlockSpec((S, V_W), blk),
                   pl.BlockSpec((S, V_W), blk),
                   pl.BlockSpec((None, S, HEADS), lambda b, p: (p, b, 0))],
        out_shape=[jax.ShapeDtypeStruct((T, D_INNER), BF16),
                   jax.ShapeDtypeStruct((T, D_INNER), BF16),
                   jax.ShapeDtypeStruct((N_HEADS // HEADS, T, HEADS), F32)],
        compiler_params=_params(2),
    )(qkvz, qkvz, qkvz, qkvz)


def _attn_gate_bwd(dh, w_out, qkvz, o, after, name, tm=1024, tn=512):
    T = dh.shape[0]
    gate_b0 = (2 * QK_WIDTH + D_INNER) // tn

    def body(dh_ref, w_ref, z_ref, o_ref, _, do_ref, dz_ref, dh_s):
        @pl.when(pl.program_id(1) == 0)
        def _():
            dh_s[...] = dh_ref[...].astype(BF16)

        for rows in _row_blocks(tm):
            dy = _dot_nt(dh_s[rows, :], w_ref[...])
            z = z_ref[rows, :].astype(F32)
            sig = _sigmoid(z)
            do_ref[rows, :] = (dy * (z * sig)).astype(BF16)
            dz_ref[rows, :] = (dy * o_ref[rows, :].astype(F32)
                               * (sig * (1.0 + z * (1.0 - sig)))).astype(BF16)

    return pl.pallas_call(
        body, name=name, grid=(T // tm, D_INNER // tn),
        in_specs=[pl.BlockSpec((tm, D_MODEL), lambda m, n: (m, 0)),
                  pl.BlockSpec((tn, D_MODEL), lambda m, n: (n, 0)),
                  pl.BlockSpec((tm, tn), lambda m, n: (m, gate_b0 + n)),
                  pl.BlockSpec((tm, tn), lambda m, n: (m, n)),
                  ANY_SPEC],
        out_specs=[pl.BlockSpec((tm, tn), lambda m, n: (m, n)),
                   pl.BlockSpec((tm, tn), lambda m, n: (m, gate_b0 + n))],
        out_shape=[jax.ShapeDtypeStruct((T, D_INNER), BF16),
                   jax.ShapeDtypeStruct((T, 2 * QK_WIDTH + 2 * D_INNER), BF16)],
        scratch_shapes=[pltpu.VMEM((tm, D_MODEL), BF16)],
        compiler_params=_params(2),
    )(dh, w_out, qkvz, o, after)


def _attn_bwd(qkv, do, ltot, dproj1, S, name):
    T = qkv.shape[0]
    nq = S // TQ
    HEADS, QK_W, V_W = BWD_HEADS, BWD_HEADS * 64, BWD_HEADS * HEAD_V
    k_b0 = QK_WIDTH // QK_W
    v_b0 = 2 * QK_WIDTH // V_W
    hs = range(HEADS)
    pairs = range(HEADS // 2)

    def body(q_ref, k_ref, v_ref, do_ref, lt_ref, _, out_ref, dq_s, dk_s, dv_s, dkb_s, dvb_s, sems):
        b, p = pl.program_id(0), pl.program_id(1)
        row = lax.broadcasted_iota(jnp.int32, (TQ, TQ), 0)
        col = lax.broadcasted_iota(jnp.int32, (TQ, TQ), 1)
        causal = col < row
        upto = (row <= col).astype(BF16)
        before = (row < col).astype(BF16)
        dk_s[...] = jnp.zeros_like(dk_s)
        dv_s[...] = jnp.zeros_like(dv_s)

        def q_block(qi, _):
            qoff = pl.multiple_of(qi * TQ, TQ)
            qms = [qm * 0.125 for qm in _masked_heads(q_ref[pl.ds(qoff, TQ), :], HEADS)]
            vsl = [slice(hh * HEAD_V, (hh + 1) * HEAD_V) for hh in hs]
            psl = [slice(pp * HEAD_PAIR_QK, (pp + 1) * HEAD_PAIR_QK) for pp in pairs]
            do_h = [do_ref[pl.ds(qoff, TQ), sl] for sl in vsl]
            total = [lt_ref[pl.ds(qoff, TQ), hh:hh + 1] for hh in hs]

            def k_block(j, carry, diagonal):
                koff = pl.multiple_of(j * TQ, TQ)
                kms = _masked_heads(k_ref[pl.ds(koff, TQ), :], HEADS)
                g_before = [carry[2 * hh] for hh in hs]
                lom_before = [carry[2 * hh + 1] for hh in hs]
                z = [_dot_nt(qms[hh], kms[hh]) for hh in hs]
                da = [_dot_nt(do_h[hh], v_ref[pl.ds(koff, TQ), vsl[hh]]) for hh in hs]
                terms = [_log_terms(z[hh]) for hh in hs]
                log_om = [jnp.where(causal, t[1], 0.0) if diagonal else t[1] for t in terms]
                prefix = [_split_dot(log_om[hh], upto) for hh in hs]
                a = [jnp.exp(terms[hh][0] + ((total[hh] - lom_before[hh]) - prefix[hh])) for hh in hs]
                if diagonal:
                    a = [jnp.where(causal, a[hh], 0.0) for hh in hs]
                g = [a[hh] * da[hh] for hh in hs]
                g_prefix = [_dot(g[hh].astype(BF16), before) for hh in hs]
                out, dzs = [], []
                for hh in hs:
                    beta = jnp.exp(terms[hh][0])
                    g_excl = (g_before[hh] + g_prefix[hh]) * beta
                    if diagonal:
                        g_excl = jnp.where(causal, g_excl, 0.0)
                    dzs.append((g[hh] * (1.0 - beta) - g_excl).astype(BF16))
                    out += [g_before[hh] + jnp.sum(g[hh], axis=1, keepdims=True),
                            lom_before[hh] + jnp.sum(log_om[hh], axis=1, keepdims=True)]
                for hh in hs:
                    dv_s[pl.ds(koff, TQ), vsl[hh]] += _dot_tn(a[hh].astype(BF16), do_h[hh])
                dq = []
                for pp in pairs:
                    pair = slice(2 * pp, 2 * pp + 2)
                    dq.append(carry[2 * HEADS + pp] + _dot(jnp.concatenate(dzs[pair], axis=1),
                                                           jnp.concatenate(kms[pair], axis=0)))
                    dk_s[pl.ds(koff, TQ), psl[pp]] += _dot_tn(jnp.concatenate(dzs[pair], axis=0),
                                                              jnp.concatenate(qms[pair], axis=0))
                return tuple(out) + tuple(dq)

            zero = jnp.zeros((TQ, 1), F32)
            carry = (zero,) * (2 * HEADS) + (jnp.zeros((TQ, HEAD_PAIR_QK), F32),) * (HEADS // 2)
            carry = lax.fori_loop(0, qi, lambda j, c: k_block(j, c, False), carry)
            carry = k_block(qi, carry, True)
            for pp in pairs:
                dq_s[pl.ds(qoff, TQ), psl[pp]] = (carry[2 * HEADS + pp] * 0.125).astype(BF16)
            return 0

        lax.fori_loop(0, nq, q_block, 0)
        dkb_s[...] = dk_s[...].astype(BF16)
        dvb_s[...] = dv_s[...].astype(BF16)
        rows = pl.ds(pl.multiple_of(b * S, TQ), S)
        copies = [
            pltpu.make_async_copy(
                dq_s, out_ref.at[rows, pl.ds(pl.multiple_of(p * QK_W, 128), QK_W)], sems.at[0]),
            pltpu.make_async_copy(
                dkb_s, out_ref.at[rows, pl.ds(pl.multiple_of(QK_WIDTH + p * QK_W, 128), QK_W)],
                sems.at[1]),
            pltpu.make_async_copy(
                dvb_s, out_ref.at[rows, pl.ds(pl.multiple_of(2 * QK_WIDTH + p * V_W, 128), V_W)],
                sems.at[2]),
        ]
        for cp in copies:
            cp.start()
        for cp in copies:
            cp.wait()

    return pl.pallas_call(
        body, name=name, grid=(T // S, N_HEADS // HEADS),
        in_specs=[pl.BlockSpec((S, QK_W), lambda b, p: (b, p)),
                  pl.BlockSpec((S, QK_W), lambda b, p: (b, k_b0 + p)),
                  pl.BlockSpec((S, V_W), lambda b, p: (b, v_b0 + p)),
                  pl.BlockSpec((S, V_W), lambda b, p: (b, p)),
                  pl.BlockSpec((None, S, HEADS), lambda b, p: (p, b, 0)),
                  HBM_SPEC],
        out_specs=HBM_SPEC,
        out_shape=jax.ShapeDtypeStruct(dproj1.shape, dproj1.dtype),
        input_output_aliases={5: 0},
        scratch_shapes=[pltpu.VMEM((S, QK_W), BF16),
                        pltpu.VMEM((S, QK_W), F32),
                        pltpu.VMEM((S, V_W), F32),
                        pltpu.VMEM((S, QK_W), BF16),
                        pltpu.VMEM((S, V_W), BF16),
                        pltpu.SemaphoreType.DMA((3,))],
        compiler_params=_params(2),
    )(qkv, qkv, qkv, do, ltot, dproj1)


def _out_proj_loss_head(a, w, res, g_row, target, name, tm=512):
    T, K = a.shape

    def body(a_ref, w_ref, r_ref, g_ref, t_ref, dh_ref, dg_ref, loss_ref):
        @pl.when(pl.program_id(0) == 0)
        def _():
            dg_ref[...] = jnp.zeros_like(dg_ref)
            loss_ref[...] = jnp.zeros_like(loss_ref)

        gain = g_ref[...]
        dg, loss = dg_ref[...], loss_ref[...]
        for rows in _row_blocks(tm):
            x = r_ref[rows, :] + _dot(a_ref[rows, :], w_ref[...])
            inv = lax.rsqrt(jnp.mean(x * x, axis=-1, keepdims=True) + RMS_EPS)
            xhat = x * inv
            err = xhat * gain - t_ref[rows, :]
            per_token = jnp.mean(err * err, axis=-1, keepdims=True)
            loss = loss + 0.5 * jnp.sum(per_token, axis=0, keepdims=True)
            dy = err * (1.0 / D_MODEL)
            dg = dg + jnp.sum(dy * xhat, axis=0, keepdims=True)
            dxh = dy * gain
            proj = jnp.mean(dxh * xhat, axis=-1, keepdims=True)
            dh_ref[rows, :] = inv * (dxh - xhat * proj)
        dg_ref[...] = dg
        loss_ref[...] = loss

    return pl.pallas_call(
        body, name=name, grid=(T // tm,),
        in_specs=[pl.BlockSpec((tm, K), lambda m: (m, 0)),
                  pl.BlockSpec((K, D_MODEL), lambda m: (0, 0)),
                  pl.BlockSpec((tm, D_MODEL), lambda m: (m, 0)),
                  pl.BlockSpec((1, D_MODEL), lambda m: (0, 0)),
                  pl.BlockSpec((tm, D_MODEL), lambda m: (m, 0))],
        out_specs=[pl.BlockSpec((tm, D_MODEL), lambda m: (m, 0)),
                   pl.BlockSpec((1, D_MODEL), lambda m: (0, 0)),
                   pl.BlockSpec((1, 128), lambda m: (0, 0))],
        out_shape=[jax.ShapeDtypeStruct((T, D_MODEL), F32),
                   jax.ShapeDtypeStruct((1, D_MODEL), F32),
                   jax.ShapeDtypeStruct((1, 128), F32)],
        compiler_params=_params(1),
    )(a, w, res, g_row, target)


def _place():
    return lax.axis_index("x"), lax.axis_index("y"), lax.axis_index("c")


def _other_chips(x, y):
    return [(1 - x, y), (x, 1 - y), (1 - x, 1 - y)]


def _half(ref, c):
    hr = ref.shape[-2] // 2
    return pl.ds(pl.multiple_of(c * hr, 8), hr)


def _cast_to_slot(shard, chip, name, tr=256):
    R, C = shard.shape

    def body(chip_ref, w_ref, o_ref):
        o_ref[0] = w_ref[...].astype(BF16)

    return pl.pallas_call(
        body, name=name,
        grid_spec=pltpu.PrefetchScalarGridSpec(
            num_scalar_prefetch=1, grid=(R // tr,),
            in_specs=[pl.BlockSpec((tr, C), lambda i, chip_ref: (i, 0))],
            out_specs=pl.BlockSpec((1, tr, C), lambda i, chip_ref: (chip_ref[0], i, 0))),
        out_shape=jax.ShapeDtypeStruct((N_CHIPS, R, C), BF16),
        compiler_params=_params(1),
    )(chip, shard)


def _weight_plan(bufs):
    x, y, c = _place()
    plan = []
    for buf in bufs:
        mine = buf.at[2 * x + y, _half(buf, c)]
        for ox, oy in _other_chips(x, y):
            plan.append((mine, mine, (ox, oy, c), buf.at[2 * ox + oy, _half(buf, c)]))
    return plan


def _chip_sum_plan(bufs):
    x, y, c = _place()
    n = len(bufs) // 2
    plan = []
    for sums, land in zip(bufs[:n], bufs[n:]):
        for k, (ox, oy) in enumerate(_other_chips(x, y)):
            plan.append((sums.at[2 * ox + oy], land.at[k], (ox, oy, c), land.at[k]))
    return plan


def _sibling_plan(bufs):
    x, y, c = _place()
    n = len(bufs) // 2
    return [(p.at[:, _half(p, 1 - c)], land, (x, y, 1 - c), land)
            for p, land in zip(bufs[:n], bufs[n:])]


SEM_SPEC = pl.BlockSpec(memory_space=pltpu.SEMAPHORE)
ANY_SPEC = pl.BlockSpec(memory_space=pl.ANY)
DATAFLOW = pltpu.SideEffectType.DATAFLOW_SIDE_EFFECTING


def _in_hbm(a):
    return pltpu.with_memory_space_constraint(a, pltpu.HBM)


def _exchange_start(bufs, after, plan, n_copies, name):
    nb = len(bufs)

    def body(*refs):
        send_sems, recv_sems = refs[nb + 1], refs[nb + 2]
        for i, (src, dst, dev, _) in enumerate(plan(refs[:nb])):
            pltpu.make_async_remote_copy(
                src_ref=src, dst_ref=dst, send_sem=send_sems.at[i], recv_sem=recv_sems.at[i],
                device_id=dev, device_id_type=MESH).start()
        token = refs[-1]
        token[...] = jnp.zeros_like(token)

    res = pl.pallas_call(
        body, name=name,
        in_specs=[HBM_SPEC] * nb + [ANY_SPEC],
        out_specs=[SEM_SPEC, SEM_SPEC] + [HBM_SPEC] * nb + [pl.BlockSpec(memory_space=pltpu.VMEM)],
        out_shape=[pltpu.SemaphoreType.DMA((n_copies,)), pltpu.SemaphoreType.DMA((n_copies,))]
        + [pltpu.HBM(b.shape, b.dtype) for b in bufs] + [jax.ShapeDtypeStruct((8, 128), F32)],
        input_output_aliases={i: 2 + i for i in range(nb)},
        compiler_params=pltpu.CompilerParams(has_side_effects=DATAFLOW),
    )(*[_in_hbm(b) for b in bufs], after)
    return res[0], res[1], list(res[2:2 + nb]), res[-1]


def _exchange_wait(bufs, send_sems, recv_sems, after, plan, name):
    nb = len(bufs)

    def body(*refs):
        sends, recvs = refs[nb], refs[nb + 1]
        for i, (src, dst, dev, landing) in enumerate(plan(refs[:nb])):
            pltpu.make_async_remote_copy(
                src_ref=src, dst_ref=landing, send_sem=sends.at[i], recv_sem=recvs.at[i],
                device_id=dev, device_id_type=MESH).wait()

    res = pl.pallas_call(
        body, name=name,
        in_specs=[HBM_SPEC] * nb + [SEM_SPEC, SEM_SPEC, ANY_SPEC],
        out_specs=[HBM_SPEC] * nb,
        out_shape=[pltpu.HBM(b.shape, b.dtype) for b in bufs],
        input_output_aliases={i: i for i in range(nb)},
        compiler_params=pltpu.CompilerParams(has_side_effects=DATAFLOW),
    )(*bufs, send_sems, recv_sems, after)
    return list(res)


def _allgather_weights(slots, name, landed=False):
    n = len(slots)

    def body(*refs):
        outs = refs[n:2 * n]
        send_sems, recv_sems, fwd_send, fwd_recv = refs[2 * n:]
        x, y, c = _place()
        chips = _other_chips(x, y)

        def landing(a, chip, half_of):
            return outs[a].at[2 * chip[0] + chip[1], _half(outs[a], half_of)]

        def ici(a, k, chip_from, to):
            return pltpu.make_async_remote_copy(
                src_ref=landing(a, chip_from, c), dst_ref=landing(a, chip_from, c),
                send_sem=send_sems.at[a, k], recv_sem=recv_sems.at[a, k],
                device_id=to, device_id_type=MESH)

        def d2d(a, k, chip_from, half_of):
            return pltpu.make_async_remote_copy(
                src_ref=landing(a, chip_from, half_of), dst_ref=landing(a, chip_from, half_of),
                send_sem=fwd_send.at[a, k], recv_sem=fwd_recv.at[a, k],
                device_id=(x, y, 1 - c), device_id_type=MESH)

        sends = []
        if not landed:
            sends = [ici(a, k, (x, y), (*chips[k], c)) for a in range(n) for k in range(3)]
        for cp in sends:
            cp.start()
        forwards = []
        for a in range(n):
            for k in range(3):
                if not landed:
                    ici(a, k, chips[k], (x, y, c)).wait_recv()
                fw = d2d(a, k, chips[k], c)
                fw.start()
                forwards.append(fw)
        for a in range(n):
            for k in range(3):
                d2d(a, k, chips[k], 1 - c).wait_recv()
        for cp in sends + forwards:
            cp.wait_send()

    return pl.pallas_call(
        body, name=name,
        in_specs=[HBM_SPEC] * n, out_specs=[HBM_SPEC] * n,
        out_shape=[jax.ShapeDtypeStruct(s.shape, s.dtype) for s in slots],
        input_output_aliases={a: a for a in range(n)},
        scratch_shapes=[pltpu.SemaphoreType.DMA((n, 3)), pltpu.SemaphoreType.DMA((n, 3)),
                        pltpu.SemaphoreType.DMA((n, 3)), pltpu.SemaphoreType.DMA((n, 3))],
    )(*slots)


def _sibling_exchange(partials, small, name):
    n = len(partials)
    ns = 0 if small is None else 1

    def body(*refs):
        ins, outs = refs[:n], refs[n + ns:2 * n + ns]
        send_sems, recv_sems = refs[2 * (n + ns):2 * (n + ns) + 2]
        x, y, c = _place()
        me = 4 * x + 2 * y + c
        sends = [pltpu.make_async_remote_copy(
            src_ref=ins[a].at[:, _half(ins[a], 1 - c)], dst_ref=outs[a],
            send_sem=send_sems.at[a], recv_sem=recv_sems.at[a],
            device_id=(x, y, 1 - c), device_id_type=MESH) for a in range(n)]
        if ns:
            small_ref, small_all = refs[n], refs[2 * n + 1]
            s_send, s_recv, loc_sem = refs[2 * (n + ns) + 2:]
            local = pltpu.make_async_copy(small_ref, small_all.at[me], loc_sem)
            local.start()
            for d in range(1, N_DEV):
                px, py, pc = x ^ ((d >> 2) & 1), y ^ ((d >> 1) & 1), c ^ (d & 1)
                sends.append(pltpu.make_async_remote_copy(
                    src_ref=small_ref, dst_ref=small_all.at[me],
                    send_sem=s_send.at[d - 1], recv_sem=s_recv.at[d - 1],
                    device_id=(px, py, pc), device_id_type=MESH))
        for cp in sends:
            cp.start()
        if ns:
            for d in range(1, N_DEV):
                pltpu.make_async_remote_copy(
                    src_ref=small_ref, dst_ref=small_all.at[me ^ d],
                    send_sem=s_send.at[d - 1], recv_sem=s_recv.at[d - 1],
                    device_id=(x, y, c), device_id_type=MESH).wait_recv()
        for cp in sends[:n]:
            cp.wait_recv()
        for cp in sends:
            cp.wait_send()
        if ns:
            local.wait()

    out_shape = [jax.ShapeDtypeStruct((N_CHIPS, p.shape[1] // 2, p.shape[2]), F32) for p in partials]
    scratch = [pltpu.SemaphoreType.DMA((max(n, 1),)), pltpu.SemaphoreType.DMA((max(n, 1),))]
    if ns:
        out_shape.append(jax.ShapeDtypeStruct((N_DEV,) + small.shape, F32))
        scratch += [pltpu.SemaphoreType.DMA((N_DEV - 1,)), pltpu.SemaphoreType.DMA((N_DEV - 1,)),
                    pltpu.SemaphoreType.DMA]
    return pl.pallas_call(
        body, name=name,
        in_specs=[HBM_SPEC] * (n + ns), out_specs=[HBM_SPEC] * (n + ns),
        out_shape=out_shape, scratch_shapes=scratch,
    )(*partials, *([small] if ns else []))


def _chip_sum(partial, from_sibling, c, name, tr=256):
    _, hr, C = from_sibling.shape
    nb = hr // tr

    def body(c_ref, p_ref, s_ref, o_ref):
        o_ref[...] = (p_ref[...] + s_ref[...]).astype(BF16)

    return pl.pallas_call(
        body, name=name,
        grid_spec=pltpu.PrefetchScalarGridSpec(
            num_scalar_prefetch=1, grid=(N_CHIPS, nb),
            in_specs=[pl.BlockSpec((1, tr, C), lambda j, i, c_ref: (j, c_ref[0] * nb + i, 0)),
                      pl.BlockSpec((1, tr, C), lambda j, i, c_ref: (j, i, 0))],
            out_specs=pl.BlockSpec((1, tr, C), lambda j, i, c_ref: (j, i, 0))),
        out_shape=jax.ShapeDtypeStruct(from_sibling.shape, BF16),
        compiler_params=_params(2),
    )(c, partial, from_sibling)


def _reduce_half(partial, from_sibling, received, place, name, tr=256):
    _, hr, C = from_sibling.shape
    nb = hr // tr

    def body(p_ref, mine_ref, sib_ref, r_ref, o_ref):
        acc = mine_ref[0] + sib_ref[0]
        for k in range(3):
            acc = acc + r_ref[k].astype(F32)
        o_ref[...] = acc

    return pl.pallas_call(
        body, name=name,
        grid_spec=pltpu.PrefetchScalarGridSpec(
            num_scalar_prefetch=1, grid=(nb,),
            in_specs=[pl.BlockSpec((1, tr, C), lambda i, p: (p[0], p[1] * nb + i, 0)),
                      pl.BlockSpec((1, tr, C), lambda i, p: (p[0], i, 0)),
                      pl.BlockSpec((3, tr, C), lambda i, p: (0, i, 0))],
            out_specs=pl.BlockSpec((tr, C), lambda i, p: (p[1] * nb + i, 0))),
        out_shape=jax.ShapeDtypeStruct((2 * hr, C), F32),
        compiler_params=_params(1),
    )(place, partial, from_sibling, received)


def _join_halves(fulls, name):
    n = len(fulls)

    def body(*refs):
        outs = refs[n:2 * n]
        send_sems, recv_sems = refs[2 * n:]
        x, y, c = _place()

        def copy(a, half_of, to):
            rows = outs[a].at[_half(outs[a], half_of)]
            return pltpu.make_async_remote_copy(
                src_ref=rows, dst_ref=rows, send_sem=send_sems.at[a], recv_sem=recv_sems.at[a],
                device_id=to, device_id_type=MESH)

        sends = [copy(a, c, (x, y, 1 - c)) for a in range(n)]
        for cp in sends:
            cp.start()
        for a in range(n):
            copy(a, 1 - c, (x, y, c)).wait_recv()
        for cp in sends:
            cp.wait_send()

    return pl.pallas_call(
        body, name=name,
        in_specs=[HBM_SPEC] * n, out_specs=[HBM_SPEC] * n,
        out_shape=[jax.ShapeDtypeStruct(f.shape, F32) for f in fulls],
        input_output_aliases={a: a for a in range(n)},
        scratch_shapes=[pltpu.SemaphoreType.DMA((n,)), pltpu.SemaphoreType.DMA((n,))],
    )(*fulls)


def _adamw_math(w, g, m, v):
    m = ADAM_B1 * m + (1.0 - ADAM_B1) * g
    v = ADAM_B2 * v + (1.0 - ADAM_B2) * (g * g)
    m_hat = m / (1.0 - ADAM_B1 ** ADAM_STEP)
    v_hat = v / (1.0 - ADAM_B2 ** ADAM_STEP)
    delta = -ADAM_LR * (m_hat / (jnp.sqrt(v_hat) + ADAM_EPS) + ADAM_WD * w)
    return delta, m, v


def _adamw(w, g, m, v, name, tr=256):
    R, C = w.shape
    tr = min(tr, R)

    def body(w_ref, g_ref, m_ref, v_ref, d_out, m_out, v_out):
        d_out[...], m_out[...], v_out[...] = _adamw_math(w_ref[...], g_ref[...], m_ref[...], v_ref[...])

    spec = pl.BlockSpec((tr, C), lambda i: (i, 0))
    return pl.pallas_call(
        body, name=name, grid=(R // tr,),
        in_specs=[spec] * 4, out_specs=[spec] * 3,
        out_shape=[jax.ShapeDtypeStruct((R, C), F32)] * 3,
        compiler_params=_params(1),
    )(w, g, m, v)


def _adamw_small(small_all, w, m, v, name):
    def body(s_ref, w_ref, m_ref, v_ref, g_out, d_out, m_out, v_out):
        g = s_ref[0]
        for d in range(1, N_DEV):
            g = g + s_ref[d]
        g_out[...] = g
        d_out[...], m_out[...], v_out[...] = _adamw_math(w_ref[...], g, m_ref[...], v_ref[...])

    vm = pl.BlockSpec(memory_space=pltpu.VMEM)
    return pl.pallas_call(
        body, name=name, in_specs=[vm] * 4, out_specs=[vm] * 4,
        out_shape=[jax.ShapeDtypeStruct(w.shape, F32)] * 4,
    )(small_all, w, m, v)


def _pack_small(norm_g, pool_scale, norm_f, extra_row):
    return jnp.concatenate([norm_g.reshape(2, D_MODEL), pool_scale.reshape(2, D_MODEL),
                            norm_f.reshape(1, D_MODEL), extra_row,
                            jnp.zeros((2, D_MODEL), F32)], axis=0)


def kernel(x, norm_g, pool_w_in, pool_w, pool_scale, pool_w_out, sb_w_in, sb_w_out, norm_f, loss_target, m_norm_g, m_pool_w_in, m_pool_w, m_pool_scale, m_pool_w_out, m_sb_w_in, m_sb_w_out, m_norm_f, v_norm_g, v_pool_w_in, v_pool_w, v_pool_scale, v_pool_w_out, v_sb_w_in, v_sb_w_out, v_norm_f):
    nb, S, _ = x.shape
    T = nb * S
    xt = x.reshape(T, D_MODEL)
    target = loss_target.reshape(T, D_MODEL)
    cx, cy, cc = _place()

    def shard2d(w):
        return w.reshape(-1, w.shape[-1])

    names = ("pool_w_in", "pool_w", "pool_w_out", "sb_w_in", "sb_w_out")
    w_shards = [shard2d(w) for w in (pool_w_in, pool_w, pool_w_out, sb_w_in, sb_w_out)]
    m_shards = [shard2d(w) for w in (m_pool_w_in, m_pool_w, m_pool_w_out, m_sb_w_in, m_sb_w_out)]
    v_shards = [shard2d(w) for w in (v_pool_w_in, v_pool_w, v_pool_w_out, v_sb_w_in, v_sb_w_out)]

    chip = (2 * cx + cy).reshape(1).astype(jnp.int32)
    c_arr = cc.reshape(1).astype(jnp.int32)
    place = jnp.stack([2 * cx + cy, cc]).astype(jnp.int32)
    slots = [_cast_to_slot(w, chip, "cast_" + nm) for w, nm in zip(w_shards, names)]
    g0, g1, gf = norm_g[0:1], norm_g[1:2], norm_f.reshape(1, D_MODEL)

    w_pin, = _allgather_weights(slots[:1], "allgather_pool_in_weights")
    mix_send, mix_recv, mix_slots, token = _exchange_start(slots[1:3], w_pin, _weight_plan, 6,
                                                           "pool_weights_start")
    sb_send, sb_recv, sb_slots, token = _exchange_start(slots[3:], token, _weight_plan, 6,
                                                        "sb_weights_start")

    proj0, u0 = _rms_matmul(xt, g0 + token[0:1, 0:1], w_pin, [(2 * D_INNER, BF16)], "pool_in_proj",
                            tn=w_pin.shape[2])
    mix_slots = _exchange_wait(mix_slots, mix_send, mix_recv, proj0, _weight_plan, "pool_weights_wait")
    w_g, w_pout = _allgather_weights(mix_slots, "pool_weights_forward", landed=True)
    w_pout = w_pout.reshape(D_INNER, D_MODEL)
    y0, pooled, mixed = _pool_fwd(proj0, w_g, pool_scale, S, "pool_mix")
    sb_slots = _exchange_wait(sb_slots, sb_send, sb_recv, y0, _weight_plan, "sb_weights_wait")
    w_sin, w_sout = _allgather_weights(sb_slots, "sb_weights_forward", landed=True)
    w_sout = w_sout.reshape(D_INNER, D_MODEL)
    h1 = _matmul_residual(y0, w_pout, xt, "pool_out_proj")
    n1 = 2 * QK_WIDTH + 2 * D_INNER
    qkvz, u1 = _rms_matmul(h1, g1, w_sin, [(n1, BF16)], "sb_in_proj", tn=w_sin.shape[2])
    o, y1, ltot = _attn_fwd(qkvz, S, "sb_attention")
    dh2, d_norm_f, loss_row = _out_proj_loss_head(y1, w_sout, h1, gf, target, "sb_out_proj_loss_head")

    def reduce_start(partials, tag, after=None, behind=None):
        n, done = len(partials), None
        after = c_arr if after is None else after
        if behind is None:
            from_sibling = list(_sibling_exchange(partials, None, "grad_sibling_exchange_" + tag))
        else:
            lands = [lax.empty((N_CHIPS, p.shape[1] // 2, p.shape[2]), F32) for p in partials]
            send, recv, bufs, token = _exchange_start(partials + lands, after, _sibling_plan, n,
                                                      "grad_sibling_start_" + tag)
            done = behind(token[0:1, 0:1])
            bufs = _exchange_wait(bufs, send, recv, done[0], _sibling_plan, "grad_sibling_wait_" + tag)
            partials, from_sibling, after = bufs[:n], bufs[n:], c_arr
        sums = [_chip_sum(p, s, c_arr, "grad_chip_sum_%s_%d" % (tag, i))
                for i, (p, s) in enumerate(zip(partials, from_sibling))]
        lands = [lax.empty((3,) + s.shape[1:], BF16) for s in sums]
        send, recv, bufs, token = _exchange_start(sums + lands, after, _chip_sum_plan, 3 * n,
                                                  "grad_chip_exchange_start_" + tag)
        return (partials, from_sibling, send, recv, bufs), token[0:1, 0:1], done

    def reduce_finish(started, after, tag):
        partials, from_sibling, send, recv, bufs = started
        received = _exchange_wait(bufs, send, recv, after, _chip_sum_plan,
                                  "grad_chip_exchange_wait_" + tag)[len(partials):]
        return [_reduce_half(p, s, r, place, "grad_reduce_%s_%d" % (tag, i))
                for i, (p, s, r) in enumerate(zip(partials, from_sibling, received))]

    shard = lambda i, j, t: (j, 0, 0)
    gw_sout = _matmul_tn(y1, dh2, D_INNER, D_MODEL, (D_INNER, D_MODEL), (1024, 1024),
                         lambda i, j, t: (i, j), "grad_sb_w_out", bm=1024, bn=1024)
    sout_started, token, (do, dproj1) = reduce_start(
        [gw_sout.reshape(N_CHIPS, -1, D_MODEL)], "sb_out",
        behind=lambda tok: _attn_gate_bwd(dh2, w_sout, qkvz, o, tok, "sb_gate_bwd"))
    dproj1 = _attn_bwd(qkvz, do, ltot, dproj1, S, "sb_attention_bwd")
    gw_sin = _matmul_tn(u1, dproj1, D_MODEL, n1, (N_CHIPS, D_MODEL, n1 // 4), (1, D_MODEL, n1 // 4),
                        shard, "grad_sb_w_in", bm=D_MODEL, bn=n1 // 4)
    sin_started, token, (dh1, d_g1) = reduce_start(
        [gw_sin], "sb_in", after=token,
        behind=lambda tok: _matmul_nt_rms_bwd(dproj1, w_sin, h1, g1 + tok, dh2, "sb_in_bwd",
                                              tk=w_sin.shape[2]))
    gw_pout = _matmul_tn(y0, dh1, D_INNER, D_MODEL, (D_INNER, D_MODEL), (1024, 1024),
                         lambda i, j, t: (i, j), "grad_pool_w_out", bm=1024, bn=1024)
    dmixed, dproj0, d_scale = _pool_gate_bwd(dh1, w_pout, proj0, mixed, pool_scale + token,
                                             "pool_gate_bwd")
    gw_g = _matmul_tn(pooled, dmixed, D_INNER, D_INNER, (N_CHIPS, GROUP_DIM, GROUP_DIM),
                      (N_CHIPS, GROUP_DIM // N_CHIPS, GROUP_DIM), lambda i, j, t: (0, i, 0),
                      "grad_pool_w", bm=GROUP_DIM, bn=GROUP_DIM, diagonal_blocks=True)
    mix_started, token, (dproj0,) = reduce_start(
        [gw_g, gw_pout.reshape(N_CHIPS, -1, D_MODEL)], "pool_mix",
        behind=lambda tok: (_pool_bwd(dmixed, w_g, dproj0, tok, S, "pool_bwd"),))
    n0 = 2 * D_INNER
    gw_pin = _matmul_tn(u0, dproj0, D_MODEL, n0, (N_CHIPS, D_MODEL, n0 // 4), (1, D_MODEL, n0 // 4),
                        shard, "grad_pool_w_in", bm=D_MODEL, bn=n0 // 4)
    pin_started, token, _ = reduce_start([gw_pin], "pool_in", after=token)
    dx, d_g0 = _matmul_nt_rms_bwd(dproj0, w_pin, xt, g0 + token, dh1, "pool_in_bwd", tk=w_pin.shape[2])

    small = _pack_small(jnp.concatenate([d_g0, d_g1], axis=0), d_scale, d_norm_f,
                        jnp.broadcast_to(loss_row[:, :1], (1, D_MODEL)))
    small_all, = _sibling_exchange([], small, "small_sums_exchange")
    grads = _join_halves(reduce_finish(pin_started, dx, "pool_in")
                         + reduce_finish(mix_started, dx, "pool_mix")
                         + reduce_finish(sin_started, dx, "sb_in")
                         + reduce_finish(sout_started, dx, "sb_out"), "grad_join_halves")

    deltas, new_m, new_v = [], [], []
    for w, g, m, v, nm in zip(w_shards, grads, m_shards, v_shards, names):
        d, mm, vv = _adamw(w, g, m, v, "adamw_" + nm)
        deltas.append(d)
        new_m.append(mm)
        new_v.append(vv)

    zero_row = jnp.zeros((1, D_MODEL), F32)
    g_small, d_small, m_small, v_small = _adamw_small(
        small_all, _pack_small(norm_g, pool_scale, norm_f, zero_row),
        _pack_small(m_norm_g, m_pool_scale, m_norm_f, zero_row),
        _pack_small(v_norm_g, v_pool_scale, v_norm_f, zero_row + 1.0), "adamw_small")
    loss = g_small[5, 0]

    def unpack_small(a):
        return a[0:2], a[2:4].reshape(1, D_INNER), a[4]

    def assemble(big, small3):
        ng, ps, nf = small3
        return [ng, big[0].reshape(pool_w_in.shape), big[1].reshape(pool_w.shape), ps,
                big[2].reshape(pool_w_out.shape), big[3].reshape(sb_w_in.shape),
                big[4].reshape(sb_w_out.shape), nf]

    return (loss, dx.reshape(x.shape),
            *assemble(grads, unpack_small(g_small)),
            *assemble(deltas, unpack_small(d_small)),
            *assemble(new_m, unpack_small(m_small)),
            *assemble(new_v, unpack_small(v_small)))
```

```python
import jax
import jax.numpy as jnp
from jax import lax
from jax.experimental import pallas as pl
from jax.experimental.pallas import tpu as pltpu

F32 = jnp.float32
BF16 = jnp.bfloat16
MESH = pl.DeviceIdType.MESH

D_MODEL = 1024
D_INNER = 2048
N_GROUPS = 4
GROUP_DIM = 512
HEAD_PAIR_QK = 128
HEAD_V = 128
QK_WIDTH = 1024
RMS_EPS = 1e-6
HALO = 16
N_CHIPS = 4
N_DEV = 8

ADAM_LR = 0.001
ADAM_B1 = 0.9
ADAM_B2 = 0.999
ADAM_EPS = 1e-08
ADAM_WD = 0.01
ADAM_STEP = 10

VMEM_LIMIT = 56 * 1024 * 1024

HBM_SPEC = pl.BlockSpec(memory_space=pltpu.HBM)


def _params(n_axes):
    return pltpu.CompilerParams(dimension_semantics=("arbitrary",) * n_axes,
                                vmem_limit_bytes=VMEM_LIMIT)


def _dot(a, b):
    return jnp.dot(a, b, preferred_element_type=F32)


def _dot_nt(a, b):
    return lax.dot_general(a, b, (((1,), (1,)), ((), ())), preferred_element_type=F32)


def _dot_tn(a, b):
    return lax.dot_general(a, b, (((0,), (0,)), ((), ())), preferred_element_type=F32)


def _sigmoid(z):
    return 1.0 / (1.0 + jnp.exp(-z))


def _row_blocks(tm, rows=256):
    return [slice(r, r + rows) for r in range(0, tm, rows)]


def _rms_matmul(h, g_row, w4, outs, name, tm=1024, tn=512):
    T = h.shape[0]
    per_shard = w4.shape[2] // tn
    starts = [0]
    for width, _ in outs:
        starts.append(starts[-1] + width // tn)
    n_out = len(outs)

    def body(h_ref, g_ref, w_ref, *rest):
        o_refs, u_out, u_s = rest[:n_out], rest[n_out], rest[n_out + 1]
        n = pl.program_id(1)

        @pl.when(n == 0)
        def _():
            x = h_ref[...]
            inv = lax.rsqrt(jnp.mean(x * x, axis=-1, keepdims=True) + RMS_EPS)
            u = (x * inv * g_ref[...]).astype(BF16)
            u_s[...] = u
            u_out[...] = u

        res = _dot(u_s[...], w_ref[0])
        for k in range(n_out):
            @pl.when((n >= starts[k]) & (n < starts[k + 1]))
            def _():
                o_refs[k][...] = res.astype(o_refs[k].dtype)

    def out_map(k):
        return lambda m, n: (m, jnp.clip(n - starts[k], 0, starts[k + 1] - starts[k] - 1))

    return pl.pallas_call(
        body, name=name, grid=(T // tm, starts[-1]),
        in_specs=[pl.BlockSpec((tm, D_MODEL), lambda m, n: (m, 0)),
                  pl.BlockSpec((1, D_MODEL), lambda m, n: (0, 0)),
                  pl.BlockSpec((1, D_MODEL, tn), lambda m, n: (n // per_shard, 0, n % per_shard))],
        out_specs=[pl.BlockSpec((tm, tn), out_map(k)) for k in range(n_out)]
        + [pl.BlockSpec((tm, D_MODEL), lambda m, n: (m, 0))],
        out_shape=[jax.ShapeDtypeStruct((T, width), dt) for width, dt in outs]
        + [jax.ShapeDtypeStruct((T, D_MODEL), BF16)],
        scratch_shapes=[pltpu.VMEM((tm, D_MODEL), BF16)],
        compiler_params=_params(2),
    )(h, g_row, w4)


def _matmul_residual(a, w, res, after, name, tm=1024, tn=1024):
    T, K = a.shape
    N = w.shape[1]

    def body(a_ref, w_ref, r_ref, _, o_ref):
        o_ref[...] = r_ref[...] + _dot(a_ref[...], w_ref[...])

    return pl.pallas_call(
        body, name=name, grid=(T // tm, N // tn),
        in_specs=[pl.BlockSpec((tm, K), lambda m, n: (m, 0)),
                  pl.BlockSpec((K, tn), lambda m, n: (0, n)),
                  pl.BlockSpec((tm, tn), lambda m, n: (m, n)),
                  ANY_SPEC],
        out_specs=pl.BlockSpec((tm, tn), lambda m, n: (m, n)),
        out_shape=jax.ShapeDtypeStruct((T, N), F32),
        compiler_params=_params(2),
    )(a, w, res, after)


def _matmul_tn(a, b, a_cols, b_cols, out_shape, out_block, out_map, name, bm, bn, tk=2048,
               diagonal_blocks=False):
    T = a.shape[0]
    tk = min(tk, T)

    def body(a_ref, b_ref, o_ref):
        @pl.when(pl.program_id(2) == 0)
        def _():
            o_ref[...] = jnp.zeros_like(o_ref)

        part = _dot_tn(a_ref[...].astype(BF16), b_ref[...].astype(BF16))
        o_ref[...] += part.reshape(o_ref.shape)

    b_map = (lambda i, j, t: (t, i)) if diagonal_blocks else (lambda i, j, t: (t, j))
    return pl.pallas_call(
        body, name=name, grid=(a_cols // bm, 1 if diagonal_blocks else b_cols // bn, T // tk),
        in_specs=[pl.BlockSpec((tk, bm), lambda i, j, t: (t, i)),
                  pl.BlockSpec((tk, bn), b_map)],
        out_specs=pl.BlockSpec(out_block, out_map),
        out_shape=jax.ShapeDtypeStruct(out_shape, F32),
        compiler_params=_params(3),
    )(a, b)


def _matmul_nt_rms_bwd(dproj, w4, h, g_row, dres, name, tm=1024, tk=512):
    T, cols = dproj.shape
    per_shard = w4.shape[2] // tk
    nk = cols // tk

    def body(dp_ref, w_ref, h_ref, g_ref, r_ref, dx_ref, dg_ref, acc):
        m, k = pl.program_id(0), pl.program_id(1)

        @pl.when(k == 0)
        def _():
            acc[...] = jnp.zeros_like(acc)

        @pl.when((k == 0) & (m == 0))
        def _():
            dg_ref[...] = jnp.zeros_like(dg_ref)

        acc[...] += _dot_nt(dp_ref[...], w_ref[0])

        @pl.when(k == nk - 1)
        def _():
            du = acc[...]
            x = h_ref[...]
            inv = lax.rsqrt(jnp.mean(x * x, axis=-1, keepdims=True) + RMS_EPS)
            xhat = x * inv
            dg_ref[...] += jnp.sum(du * xhat, axis=0, keepdims=True)
            dxh = du * g_ref[...]
            proj = jnp.mean(dxh * xhat, axis=-1, keepdims=True)
            dx_ref[...] = r_ref[...] + inv * (dxh - xhat * proj)

    return pl.pallas_call(
        body, name=name, grid=(T // tm, nk),
        in_specs=[pl.BlockSpec((tm, tk), lambda m, k: (m, k)),
                  pl.BlockSpec((1, D_MODEL, tk), lambda m, k: (k // per_shard, 0, k % per_shard)),
                  pl.BlockSpec((tm, D_MODEL), lambda m, k: (m, 0)),
                  pl.BlockSpec((1, D_MODEL), lambda m, k: (0, 0)),
                  pl.BlockSpec((tm, D_MODEL), lambda m, k: (m, 0))],
        out_specs=[pl.BlockSpec((tm, D_MODEL), lambda m, k: (m, 0)),
                   pl.BlockSpec((1, D_MODEL), lambda m, k: (0, 0))],
        out_shape=[jax.ShapeDtypeStruct((T, D_MODEL), F32),
                   jax.ShapeDtypeStruct((1, D_MODEL), F32)],
        scratch_shapes=[pltpu.VMEM((tm, D_MODEL), F32)],
        compiler_params=_params(2),
    )(dproj, w4, h, g_row, dres)


def _window_of(g):
    return jnp.left_shift(2, g)


def _select_stage(g, stages):
    res = stages[0]
    for i in range(1, len(stages)):
        res = jnp.where(g >= i, stages[i], res)
    return res


def _pool_fwd(proj0, wg4, scale_row, S, name, tm=1024):
    T = proj0.shape[0]
    tm = min(tm, S)
    blocks_per_seq = S // tm
    hb = tm // HALO

    def body(x_ref, halo_ref, z_ref, w_ref, s_ref, y_ref, p_ref, mix_ref):
        m, g = pl.program_id(0), pl.program_id(1)
        first = (m % blocks_per_seq) == 0
        halo = jnp.where(first, 0.0, halo_ref[...].astype(F32))
        x = x_ref[...].astype(F32)
        ext = jnp.concatenate([halo, x], axis=0)
        stages = []
        cur = ext
        for sh in (1, 2, 4, 8):
            cur = cur + pltpu.roll(cur, sh, 0)
            stages.append(cur[HALO:, :])
        win_sum = _select_stage(g, stages)
        pos = (m % blocks_per_seq) * tm + lax.broadcasted_iota(jnp.int32, (tm, 1), 0)
        count = jnp.minimum(pos + 1, _window_of(g)).astype(F32)
        p_ref[...] = (win_sum / count - x).astype(BF16)
        w = w_ref[...].reshape(GROUP_DIM, GROUP_DIM)
        for rows in _row_blocks(tm):
            mixed = _dot(p_ref[rows, :], w)
            z = z_ref[rows, :].astype(F32)
            y_ref[rows, :] = (mixed * s_ref[...] * (z * _sigmoid(z))).astype(BF16)
            mix_ref[rows, :] = mixed.astype(BF16)

    blk = lambda m, g: (m, g)
    return pl.pallas_call(
        body, name=name, grid=(T // tm, N_GROUPS),
        in_specs=[pl.BlockSpec((tm, GROUP_DIM), blk),
                  pl.BlockSpec((HALO, GROUP_DIM), lambda m, g: (jnp.maximum(m * hb - 1, 0), g)),
                  pl.BlockSpec((tm, GROUP_DIM), lambda m, g: (m, N_GROUPS + g)),
                  pl.BlockSpec((N_CHIPS, GROUP_DIM // N_CHIPS, GROUP_DIM), lambda m, g: (0, g, 0)),
                  pl.BlockSpec((1, GROUP_DIM), lambda m, g: (0, g))],
        out_specs=[pl.BlockSpec((tm, GROUP_DIM), blk)] * 3,
        out_shape=[jax.ShapeDtypeStruct((T, D_INNER), BF16)] * 3,
        compiler_params=_params(2),
    )(proj0, proj0, proj0, wg4, scale_row)


def _pool_gate_bwd(dh, w_out, proj0, mixed, scale_row, name, tm=1024, tn=512):
    T = dh.shape[0]
    gate_b0 = D_INNER // tn

    def body(dh_ref, w_ref, z_ref, mix_ref, s_ref, dm_ref, dz_ref, ds_ref, dh_s):
        m, n = pl.program_id(0), pl.program_id(1)

        @pl.when((m == 0) & (n == 0))
        def _():
            ds_ref[...] = jnp.zeros_like(ds_ref)

        @pl.when(n == 0)
        def _():
            dh_s[...] = dh_ref[...].astype(BF16)

        cols = pl.ds(pl.multiple_of(n * tn, tn), tn)
        s = s_ref[...]
        ds = ds_ref[:, cols]
        for rows in _row_blocks(tm):
            dy = _dot_nt(dh_s[rows, :], w_ref[...])
            z = z_ref[rows, :].astype(F32)
            sig = _sigmoid(z)
            silu = z * sig
            mixed = mix_ref[rows, :].astype(F32)
            dm_ref[rows, :] = (dy * s * silu).astype(BF16)
            dz_ref[rows, :] = (dy * mixed * s * (sig * (1.0 + z * (1.0 - sig)))).astype(BF16)
            ds = ds + jnp.sum(dy * mixed * silu, axis=0, keepdims=True)
        ds_ref[:, cols] = ds

    return pl.pallas_call(
        body, name=name, grid=(T // tm, D_INNER // tn),
        in_specs=[pl.BlockSpec((tm, D_MODEL), lambda m, n: (m, 0)),
                  pl.BlockSpec((tn, D_MODEL), lambda m, n: (n, 0)),
                  pl.BlockSpec((tm, tn), lambda m, n: (m, gate_b0 + n)),
                  pl.BlockSpec((tm, tn), lambda m, n: (m, n)),
                  pl.BlockSpec((1, tn), lambda m, n: (0, n))],
        out_specs=[pl.BlockSpec((tm, tn), lambda m, n: (m, n)),
                   pl.BlockSpec((tm, tn), lambda m, n: (m, gate_b0 + n)),
                   pl.BlockSpec((1, D_INNER), lambda m, n: (0, 0))],
        out_shape=[jax.ShapeDtypeStruct((T, D_INNER), BF16),
                   jax.ShapeDtypeStruct((T, 2 * D_INNER), BF16),
                   jax.ShapeDtypeStruct((1, D_INNER), F32)],
        scratch_shapes=[pltpu.VMEM((tm, D_MODEL), BF16)],
        compiler_params=_params(2),
    )(dh, w_out, proj0, mixed, scale_row)


def _pool_bwd(dmixed, wg4, dproj0, after, S, name, tm=1024):
    T = dmixed.shape[0]
    tm = min(tm, S)
    blocks_per_seq = S // tm
    hb = tm // HALO
    n_halo_blocks = T // HALO

    def body(dm_ref, halo_ref, w_ref, _, __, o_ref):
        m, g = pl.program_id(0), pl.program_id(1)
        ext = jnp.concatenate([dm_ref[...], halo_ref[...]], axis=0)
        dp = _dot_nt(ext, w_ref[...].reshape(GROUP_DIM, GROUP_DIM))
        pos = (m % blocks_per_seq) * tm + lax.broadcasted_iota(jnp.int32, (tm + HALO, 1), 0)
        count = jnp.minimum(pos + 1, _window_of(g)).astype(F32)
        c = jnp.where(pos < S, dp / count, 0.0)
        n = tm + HALO
        stages = []
        cur = c
        for sh in (1, 2, 4, 8):
            cur = cur + pltpu.roll(cur, n - sh, 0)
            stages.append(cur[:tm, :])
        o_ref[...] = (_select_stage(g, stages) - dp[:tm, :]).astype(BF16)

    blk = lambda m, g: (m, g)
    return pl.pallas_call(
        body, name=name, grid=(T // tm, N_GROUPS),
        in_specs=[pl.BlockSpec((tm, GROUP_DIM), blk),
                  pl.BlockSpec((HALO, GROUP_DIM),
                               lambda m, g: (jnp.minimum((m + 1) * hb, n_halo_blocks - 1), g)),
                  pl.BlockSpec((N_CHIPS, GROUP_DIM // N_CHIPS, GROUP_DIM), lambda m, g: (0, g, 0)),
                  HBM_SPEC, ANY_SPEC],
        out_specs=pl.BlockSpec((tm, GROUP_DIM), blk),
        out_shape=jax.ShapeDtypeStruct(dproj0.shape, dproj0.dtype),
        input_output_aliases={3: 0},
        compiler_params=_params(2),
    )(dmixed, dmixed, wg4, dproj0, after)


TQ = 256


def _split_dot(x, m):
    hi = x.astype(BF16)
    lo = (x - hi.astype(F32)).astype(BF16)
    return _dot(hi, m) + _dot(lo, m)


NEG_LOG2E = -1.4426950408889634


def _log_terms(z):
    soft = jnp.log(1.0 + jnp.exp2(jnp.abs(z) * NEG_LOG2E))
    log_beta = jnp.minimum(z, 0.0) - soft
    return log_beta, log_beta - z


N_HEADS = 16
FWD_HEADS = BWD_HEADS = 4


def _masked_heads(x, heads):
    lane = lax.broadcasted_iota(jnp.int32, (1, HEAD_PAIR_QK), 1)
    out = []
    for hh in range(heads):
        slab = x[:, (hh // 2) * HEAD_PAIR_QK:(hh // 2 + 1) * HEAD_PAIR_QK]
        out.append(jnp.where((lane // 64) == hh % 2, slab, jnp.zeros_like(slab)))
    return out


def _attn_fwd(qkvz, S, name):
    T = qkvz.shape[0]
    nq = S // TQ
    HEADS, QK_W, V_W = FWD_HEADS, FWD_HEADS * 64, FWD_HEADS * HEAD_V
    k_b0 = QK_WIDTH // QK_W
    v_b0 = 2 * QK_WIDTH // V_W
    z_b0 = (2 * QK_WIDTH + D_INNER) // V_W
    hs = range(HEADS)

    def body(q_ref, k_ref, v_ref, z_ref, o_ref, y_ref, lt_ref):
        row = lax.broadcasted_iota(jnp.int32, (TQ, TQ), 0)
        col = lax.broadcasted_iota(jnp.int32, (TQ, TQ), 1)
        causal = col < row
        later_in_block = (row > col).astype(BF16)
        lax.fori_loop(0, nq, lambda qi, _: q_block(qi, causal, later_in_block,
                                                   q_ref, k_ref, v_ref, z_ref, o_ref, y_ref, lt_ref), 0)

    def q_block(qi, causal, later_in_block, q_ref, k_ref, v_ref, z_ref, o_ref, y_ref, lt_ref):
        rows = pl.ds(pl.multiple_of(qi * TQ, TQ), TQ)
        qms = [qm * 0.125 for qm in _masked_heads(q_ref[rows, :], HEADS)]

        def step(j, carry, diagonal):
            koff = pl.multiple_of(j * TQ, TQ)
            kbs = [k_ref[pl.ds(koff, TQ), p * HEAD_PAIR_QK:(p + 1) * HEAD_PAIR_QK]
                   for p in range(HEADS // 2)]
            run, acc = [carry[2 * hh] for hh in hs], [carry[2 * hh + 1] for hh in hs]
            z = [_dot_nt(qms[hh], kbs[hh // 2]) for hh in hs]
            terms = [_log_terms(z[hh]) for hh in hs]
            log_om = [jnp.where(causal, t[1], 0.0) if diagonal else t[1] for t in terms]
            later = [_split_dot(log_om[hh], later_in_block) for hh in hs]
            a = [jnp.exp(terms[hh][0] + (run[hh] + later[hh])) for hh in hs]
            if diagonal:
                a = [jnp.where(causal, a[hh], 0.0) for hh in hs]
            out = []
            for hh in hs:
                vb = v_ref[pl.ds(koff, TQ), hh * HEAD_V:(hh + 1) * HEAD_V]
                out += [run[hh] + jnp.sum(log_om[hh], axis=1, keepdims=True),
                        acc[hh] + _dot(a[hh].astype(BF16), vb)]
            return tuple(out)

        zero = (jnp.zeros((TQ, 1), F32), jnp.zeros((TQ, HEAD_V), F32))
        carry = step(qi, zero * HEADS, True)
        carry = lax.fori_loop(0, qi, lambda i, c: step(qi - 1 - i, c, False), carry)
        for hh in hs:
            sl = slice(hh * HEAD_V, (hh + 1) * HEAD_V)
            acc = carry[2 * hh + 1]
            z = z_ref[rows, sl].astype(F32)
            o_ref[rows, sl] = acc.astype(BF16)
            y_ref[rows, sl] = (acc * (z * _sigmoid(z))).astype(BF16)
            lt_ref[rows, hh:hh + 1] = carry[2 * hh]
        return 0

    blk = lambda b, p: (b, p)
    return pl.pallas_call(
        body, name=name, grid=(T // S, N_HEADS // HEADS),
        in_specs=[pl.BlockSpec((S, QK_W), blk),
                  pl.BlockSpec((S, QK_W), lambda b, p: (b, k_b0 + p)),
                  pl.BlockSpec((S, V_W), lambda b, p: (b, v_b0 + p)),
                  pl.BlockSpec((S, V_W), lambda b, p: (b, z_b0 + p))],
        out_specs=[pl.BlockSpec((S, V_W), blk),
                   pl.BlockSpec((S, V_W), blk),
                   pl.BlockSpec((None, S, HEADS), lambda b, p: (p, b, 0))],
        out_shape=[jax.ShapeDtypeStruct((T, D_INNER), BF16),
                   jax.ShapeDtypeStruct((T, D_INNER), BF16),
                   jax.ShapeDtypeStruct((N_HEADS // HEADS, T, HEADS), F32)],
        compiler_params=_params(2),
    )(qkvz, qkvz, qkvz, qkvz)


def _attn_gate_bwd(dh, w_out, qkvz, o, after, name, tm=1024, tn=512):
    T = dh.shape[0]
    gate_b0 = (2 * QK_WIDTH + D_INNER) // tn

    def body(dh_ref, w_ref, z_ref, o_ref, _, do_ref, dz_ref, dh_s):
        @pl.when(pl.program_id(1) == 0)
        def _():
            dh_s[...] = dh_ref[...].astype(BF16)

        for rows in _row_blocks(tm):
            dy = _dot_nt(dh_s[rows, :], w_ref[...])
            z = z_ref[rows, :].astype(F32)
            sig = _sigmoid(z)
            do_ref[rows, :] = (dy * (z * sig)).astype(BF16)
            dz_ref[rows, :] = (dy * o_ref[rows, :].astype(F32)
                               * (sig * (1.0 + z * (1.0 - sig)))).astype(BF16)

    return pl.pallas_call(
        body, name=name, grid=(T // tm, D_INNER // tn),
        in_specs=[pl.BlockSpec((tm, D_MODEL), lambda m, n: (m, 0)),
                  pl.BlockSpec((tn, D_MODEL), lambda m, n: (n, 0)),
                  pl.BlockSpec((tm, tn), lambda m, n: (m, gate_b0 + n)),
                  pl.BlockSpec((tm, tn), lambda m, n: (m, n)),
                  ANY_SPEC],
        out_specs=[pl.BlockSpec((tm, tn), lambda m, n: (m, n)),
                   pl.BlockSpec((tm, tn), lambda m, n: (m, gate_b0 + n))],
        out_shape=[jax.ShapeDtypeStruct((T, D_INNER), BF16),
                   jax.ShapeDtypeStruct((T, 2 * QK_WIDTH + 2 * D_INNER), BF16)],
        scratch_shapes=[pltpu.VMEM((tm, D_MODEL), BF16)],
        compiler_params=_params(2),
    )(dh, w_out, qkvz, o, after)


def _attn_bwd(qkv, do, ltot, dproj1, S, name):
    T = qkv.shape[0]
    nq = S // TQ
    HEADS, QK_W, V_W = BWD_HEADS, BWD_HEADS * 64, BWD_HEADS * HEAD_V
    k_b0 = QK_WIDTH // QK_W
    v_b0 = 2 * QK_WIDTH // V_W
    hs = range(HEADS)
    pairs = range(HEADS // 2)

    def body(q_ref, k_ref, v_ref, do_ref, lt_ref, _, out_ref, dq_s, dk_s, dv_s, dkb_s, dvb_s, sems):
        b, p = pl.program_id(0), pl.program_id(1)
        row = lax.broadcasted_iota(jnp.int32, (TQ, TQ), 0)
        col = lax.broadcasted_iota(jnp.int32, (TQ, TQ), 1)
        causal = col < row
        upto = (row <= col).astype(BF16)
        before = (row < col).astype(BF16)
        dk_s[...] = jnp.zeros_like(dk_s)
        dv_s[...] = jnp.zeros_like(dv_s)

        def q_block(qi, _):
            qoff = pl.multiple_of(qi * TQ, TQ)
            qms = [qm * 0.125 for qm in _masked_heads(q_ref[pl.ds(qoff, TQ), :], HEADS)]
            vsl = [slice(hh * HEAD_V, (hh + 1) * HEAD_V) for hh in hs]
            psl = [slice(pp * HEAD_PAIR_QK, (pp + 1) * HEAD_PAIR_QK) for pp in pairs]
            do_h = [do_ref[pl.ds(qoff, TQ), sl] for sl in vsl]
            total = [lt_ref[pl.ds(qoff, TQ), hh:hh + 1] for hh in hs]

            def k_block(j, carry, diagonal):
                koff = pl.multiple_of(j * TQ, TQ)
                kms = _masked_heads(k_ref[pl.ds(koff, TQ), :], HEADS)
                g_before = [carry[2 * hh] for hh in hs]
                lom_before = [carry[2 * hh + 1] for hh in hs]
                z = [_dot_nt(qms[hh], kms[hh]) for hh in hs]
                da = [_dot_nt(do_h[hh], v_ref[pl.ds(koff, TQ), vsl[hh]]) for hh in hs]
                terms = [_log_terms(z[hh]) for hh in hs]
                log_om = [jnp.where(causal, t[1], 0.0) if diagonal else t[1] for t in terms]
                prefix = [_split_dot(log_om[hh], upto) for hh in hs]
                a = [jnp.exp(terms[hh][0] + ((total[hh] - lom_before[hh]) - prefix[hh])) for hh in hs]
                if diagonal:
                    a = [jnp.where(causal, a[hh], 0.0) for hh in hs]
                g = [a[hh] * da[hh] for hh in hs]
                g_prefix = [_dot(g[hh].astype(BF16), before) for hh in hs]
                out, dzs = [], []
                for hh in hs:
                    beta = jnp.exp(terms[hh][0])
                    g_excl = (g_before[hh] + g_prefix[hh]) * beta
                    if diagonal:
                        g_excl = jnp.where(causal, g_excl, 0.0)
                    dzs.append((g[hh] * (1.0 - beta) - g_excl).astype(BF16))
                    out += [g_before[hh] + jnp.sum(g[hh], axis=1, keepdims=True),
                            lom_before[hh] + jnp.sum(log_om[hh], axis=1, keepdims=True)]
                for hh in hs:
                    dv_s[pl.ds(koff, TQ), vsl[hh]] += _dot_tn(a[hh].astype(BF16), do_h[hh])
                dq = []
                for pp in pairs:
                    pair = slice(2 * pp, 2 * pp + 2)
                    dq.append(carry[2 * HEADS + pp] + _dot(jnp.concatenate(dzs[pair], axis=1),
                                                           jnp.concatenate(kms[pair], axis=0)))
                    dk_s[pl.ds(koff, TQ), psl[pp]] += _dot_tn(jnp.concatenate(dzs[pair], axis=0),
                                                              jnp.concatenate(qms[pair], axis=0))
                return tuple(out) + tuple(dq)

            zero = jnp.zeros((TQ, 1), F32)
            carry = (zero,) * (2 * HEADS) + (jnp.zeros((TQ, HEAD_PAIR_QK), F32),) * (HEADS // 2)
            carry = lax.fori_loop(0, qi, lambda j, c: k_block(j, c, False), carry)
            carry = k_block(qi, carry, True)
            for pp in pairs:
                dq_s[pl.ds(qoff, TQ), psl[pp]] = (carry[2 * HEADS + pp] * 0.125).astype(BF16)
            return 0

        lax.fori_loop(0, nq, q_block, 0)
        dkb_s[...] = dk_s[...].astype(BF16)
        dvb_s[...] = dv_s[...].astype(BF16)
        rows = pl.ds(pl.multiple_of(b * S, TQ), S)
        copies = [
            pltpu.make_async_copy(
                dq_s, out_ref.at[rows, pl.ds(pl.multiple_of(p * QK_W, 128), QK_W)], sems.at[0]),
            pltpu.make_async_copy(
                dkb_s, out_ref.at[rows, pl.ds(pl.multiple_of(QK_WIDTH + p * QK_W, 128), QK_W)],
                sems.at[1]),
            pltpu.make_async_copy(
                dvb_s, out_ref.at[rows, pl.ds(pl.multiple_of(2 * QK_WIDTH + p * V_W, 128), V_W)],
                sems.at[2]),
        ]
        for cp in copies:
            cp.start()
        for cp in copies:
            cp.wait()

    return pl.pallas_call(
        body, name=name, grid=(T // S, N_HEADS // HEADS),
        in_specs=[pl.BlockSpec((S, QK_W), lambda b, p: (b, p)),
                  pl.BlockSpec((S, QK_W), lambda b, p: (b, k_b0 + p)),
                  pl.BlockSpec((S, V_W), lambda b, p: (b, v_b0 + p)),
                  pl.BlockSpec((S, V_W), lambda b, p: (b, p)),
                  pl.BlockSpec((None, S, HEADS), lambda b, p: (p, b, 0)),
                  HBM_SPEC],
        out_specs=HBM_SPEC,
        out_shape=jax.ShapeDtypeStruct(dproj1.shape, dproj1.dtype),
        input_output_aliases={5: 0},
        scratch_shapes=[pltpu.VMEM((S, QK_W), BF16),
                        pltpu.VMEM((S, QK_W), F32),
                        pltpu.VMEM((S, V_W), F32),
                        pltpu.VMEM((S, QK_W), BF16),
                        pltpu.VMEM((S, V_W), BF16),
                        pltpu.SemaphoreType.DMA((3,))],
        compiler_params=_params(2),
    )(qkv, qkv, qkv, do, ltot, dproj1)


def _out_proj_loss_head(a, w, res, g_row, target, name, tm=512):
    T, K = a.shape

    def body(a_ref, w_ref, r_ref, g_ref, t_ref, dh_ref, dg_ref, loss_ref):
        @pl.when(pl.program_id(0) == 0)
        def _():
            dg_ref[...] = jnp.zeros_like(dg_ref)
            loss_ref[...] = jnp.zeros_like(loss_ref)

        gain = g_ref[...]
        dg, loss = dg_ref[...], loss_ref[...]
        for rows in _row_blocks(tm):
            x = r_ref[rows, :] + _dot(a_ref[rows, :], w_ref[...])
            inv = lax.rsqrt(jnp.mean(x * x, axis=-1, keepdims=True) + RMS_EPS)
            xhat = x * inv
            err = xhat * gain - t_ref[rows, :]
            per_token = jnp.mean(err * err, axis=-1, keepdims=True)
            loss = loss + 0.5 * jnp.sum(per_token, axis=0, keepdims=True)
            dy = err * (1.0 / D_MODEL)
            dg = dg + jnp.sum(dy * xhat, axis=0, keepdims=True)
            dxh = dy * gain
            proj = jnp.mean(dxh * xhat, axis=-1, keepdims=True)
            dh_ref[rows, :] = inv * (dxh - xhat * proj)
        dg_ref[...] = dg
        loss_ref[...] = loss

    return pl.pallas_call(
        body, name=name, grid=(T // tm,),
        in_specs=[pl.BlockSpec((tm, K), lambda m: (m, 0)),
                  pl.BlockSpec((K, D_MODEL), lambda m: (0, 0)),
                  pl.BlockSpec((tm, D_MODEL), lambda m: (m, 0)),
                  pl.BlockSpec((1, D_MODEL), lambda m: (0, 0)),
                  pl.BlockSpec((tm, D_MODEL), lambda m: (m, 0))],
        out_specs=[pl.BlockSpec((tm, D_MODEL), lambda m: (m, 0)),
                   pl.BlockSpec((1, D_MODEL), lambda m: (0, 0)),
                   pl.BlockSpec((1, 128), lambda m: (0, 0))],
        out_shape=[jax.ShapeDtypeStruct((T, D_MODEL), F32),
                   jax.ShapeDtypeStruct((1, D_MODEL), F32),
                   jax.ShapeDtypeStruct((1, 128), F32)],
        compiler_params=_params(1),
    )(a, w, res, g_row, target)


def _place():
    return lax.axis_index("x"), lax.axis_index("y"), lax.axis_index("c")


def _other_chips(x, y):
    return [(1 - x, y), (x, 1 - y), (1 - x, 1 - y)]


def _half(ref, c):
    hr = ref.shape[-2] // 2
    return pl.ds(pl.multiple_of(c * hr, 8), hr)


def _cast_to_slot(shard, chip, name, tr=256):
    R, C = shard.shape

    def body(chip_ref, w_ref, o_ref):
        o_ref[0] = w_ref[...].astype(BF16)

    return pl.pallas_call(
        body, name=name,
        grid_spec=pltpu.PrefetchScalarGridSpec(
            num_scalar_prefetch=1, grid=(R // tr,),
            in_specs=[pl.BlockSpec((tr, C), lambda i, chip_ref: (i, 0))],
            out_specs=pl.BlockSpec((1, tr, C), lambda i, chip_ref: (chip_ref[0], i, 0))),
        out_shape=jax.ShapeDtypeStruct((N_CHIPS, R, C), BF16),
        compiler_params=_params(1),
    )(chip, shard)


def _weight_plan(bufs):
    x, y, c = _place()
    plan = []
    for buf in bufs:
        mine = buf.at[2 * x + y, _half(buf, c)]
        for ox, oy in _other_chips(x, y):
            plan.append((mine, mine, (ox, oy, c), buf.at[2 * ox + oy, _half(buf, c)]))
    return plan


def _forward_plan(bufs):
    x, y, c = _place()
    plan = []
    for buf in bufs:
        for ox, oy in _other_chips(x, y):
            here = buf.at[2 * ox + oy, _half(buf, c)]
            plan.append((here, here, (x, y, 1 - c), buf.at[2 * ox + oy, _half(buf, 1 - c)]))
    return plan


def _chip_sum_plan(bufs):
    x, y, c = _place()
    n = len(bufs) // 2
    plan = []
    for sums, land in zip(bufs[:n], bufs[n:]):
        for k, (ox, oy) in enumerate(_other_chips(x, y)):
            plan.append((sums.at[2 * ox + oy], land.at[k], (ox, oy, c), land.at[k]))
    return plan


def _sibling_plan(bufs):
    x, y, c = _place()
    n = len(bufs) // 2
    return [(p.at[:, _half(p, 1 - c)], land, (x, y, 1 - c), land)
            for p, land in zip(bufs[:n], bufs[n:])]


SEM_SPEC = pl.BlockSpec(memory_space=pltpu.SEMAPHORE)
ANY_SPEC = pl.BlockSpec(memory_space=pl.ANY)
DATAFLOW = pltpu.SideEffectType.DATAFLOW_SIDE_EFFECTING


def _in_hbm(a):
    return pltpu.with_memory_space_constraint(a, pltpu.HBM)


def _exchange_start(bufs, after, plan, n_copies, name):
    nb = len(bufs)

    def body(*refs):
        send_sems, recv_sems = refs[nb + 1], refs[nb + 2]
        for i, (src, dst, dev, _) in enumerate(plan(refs[:nb])):
            pltpu.make_async_remote_copy(
                src_ref=src, dst_ref=dst, send_sem=send_sems.at[i], recv_sem=recv_sems.at[i],
                device_id=dev, device_id_type=MESH).start()
        token = refs[-1]
        token[...] = jnp.zeros_like(token)

    res = pl.pallas_call(
        body, name=name,
        in_specs=[HBM_SPEC] * nb + [ANY_SPEC],
        out_specs=[SEM_SPEC, SEM_SPEC] + [HBM_SPEC] * nb + [pl.BlockSpec(memory_space=pltpu.VMEM)],
        out_shape=[pltpu.SemaphoreType.DMA((n_copies,)), pltpu.SemaphoreType.DMA((n_copies,))]
        + [pltpu.HBM(b.shape, b.dtype) for b in bufs] + [jax.ShapeDtypeStruct((8, 128), F32)],
        input_output_aliases={i: 2 + i for i in range(nb)},
        compiler_params=pltpu.CompilerParams(has_side_effects=DATAFLOW),
    )(*[_in_hbm(b) for b in bufs], after)
    return res[0], res[1], list(res[2:2 + nb]), res[-1]


def _exchange_wait(bufs, send_sems, recv_sems, after, plan, name):
    nb = len(bufs)

    def body(*refs):
        sends, recvs = refs[nb], refs[nb + 1]
        for i, (src, dst, dev, landing) in enumerate(plan(refs[:nb])):
            pltpu.make_async_remote_copy(
                src_ref=src, dst_ref=landing, send_sem=sends.at[i], recv_sem=recvs.at[i],
                device_id=dev, device_id_type=MESH).wait()

    res = pl.pallas_call(
        body, name=name,
        in_specs=[HBM_SPEC] * nb + [SEM_SPEC, SEM_SPEC, ANY_SPEC],
        out_specs=[HBM_SPEC] * nb,
        out_shape=[pltpu.HBM(b.shape, b.dtype) for b in bufs],
        input_output_aliases={i: i for i in range(nb)},
        compiler_params=pltpu.CompilerParams(has_side_effects=DATAFLOW),
    )(*bufs, send_sems, recv_sems, after)
    return list(res)


def _allgather_weights(slots, name, landed=False):
    n = len(slots)

    def body(*refs):
        outs = refs[n:2 * n]
        send_sems, recv_sems, fwd_send, fwd_recv = refs[2 * n:]
        x, y, c = _place()
        chips = _other_chips(x, y)

        def landing(a, chip, half_of):
            return outs[a].at[2 * chip[0] + chip[1], _half(outs[a], half_of)]

        def ici(a, k, chip_from, to):
            return pltpu.make_async_remote_copy(
                src_ref=landing(a, chip_from, c), dst_ref=landing(a, chip_from, c),
                send_sem=send_sems.at[a, k], recv_sem=recv_sems.at[a, k],
                device_id=to, device_id_type=MESH)

        def d2d(a, k, chip_from, half_of):
            return pltpu.make_async_remote_copy(
                src_ref=landing(a, chip_from, half_of), dst_ref=landing(a, chip_from, half_of),
                send_sem=fwd_send.at[a, k], recv_sem=fwd_recv.at[a, k],
                device_id=(x, y, 1 - c), device_id_type=MESH)

        sends = []
        if not landed:
            sends = [ici(a, k, (x, y), (*chips[k], c)) for a in range(n) for k in range(3)]
        for cp in sends:
            cp.start()
        forwards = []
        for a in range(n):
            for k in range(3):
                if not landed:
                    ici(a, k, chips[k], (x, y, c)).wait_recv()
                fw = d2d(a, k, chips[k], c)
                fw.start()
                forwards.append(fw)
        for a in range(n):
            for k in range(3):
                d2d(a, k, chips[k], 1 - c).wait_recv()
        for cp in sends + forwards:
            cp.wait_send()

    return pl.pallas_call(
        body, name=name,
        in_specs=[HBM_SPEC] * n, out_specs=[HBM_SPEC] * n,
        out_shape=[jax.ShapeDtypeStruct(s.shape, s.dtype) for s in slots],
        input_output_aliases={a: a for a in range(n)},
        scratch_shapes=[pltpu.SemaphoreType.DMA((n, 3)), pltpu.SemaphoreType.DMA((n, 3)),
                        pltpu.SemaphoreType.DMA((n, 3)), pltpu.SemaphoreType.DMA((n, 3))],
    )(*slots)


def _sibling_exchange(partials, small, name):
    n = len(partials)
    ns = 0 if small is None else 1

    def body(*refs):
        ins, outs = refs[:n], refs[n + ns:2 * n + ns]
        send_sems, recv_sems = refs[2 * (n + ns):2 * (n + ns) + 2]
        x, y, c = _place()
        me = 4 * x + 2 * y + c
        sends = [pltpu.make_async_remote_copy(
            src_ref=ins[a].at[:, _half(ins[a], 1 - c)], dst_ref=outs[a],
            send_sem=send_sems.at[a], recv_sem=recv_sems.at[a],
            device_id=(x, y, 1 - c), device_id_type=MESH) for a in range(n)]
        if ns:
            small_ref, small_all = refs[n], refs[2 * n + 1]
            s_send, s_recv, loc_sem = refs[2 * (n + ns) + 2:]
            local = pltpu.make_async_copy(small_ref, small_all.at[me], loc_sem)
            local.start()
            for d in range(1, N_DEV):
                px, py, pc = x ^ ((d >> 2) & 1), y ^ ((d >> 1) & 1), c ^ (d & 1)
                sends.append(pltpu.make_async_remote_copy(
                    src_ref=small_ref, dst_ref=small_all.at[me],
                    send_sem=s_send.at[d - 1], recv_sem=s_recv.at[d - 1],
                    device_id=(px, py, pc), device_id_type=MESH))
        for cp in sends:
            cp.start()
        if ns:
            for d in range(1, N_DEV):
                pltpu.make_async_remote_copy(
                    src_ref=small_ref, dst_ref=small_all.at[me ^ d],
                    send_sem=s_send.at[d - 1], recv_sem=s_recv.at[d - 1],
                    device_id=(x, y, c), device_id_type=MESH).wait_recv()
        for cp in sends[:n]:
            cp.wait_recv()
        for cp in sends:
            cp.wait_send()
        if ns:
            local.wait()

    out_shape = [jax.ShapeDtypeStruct((N_CHIPS, p.shape[1] // 2, p.shape[2]), F32) for p in partials]
    scratch = [pltpu.SemaphoreType.DMA((max(n, 1),)), pltpu.SemaphoreType.DMA((max(n, 1),))]
    if ns:
        out_shape.append(jax.ShapeDtypeStruct((N_DEV,) + small.shape, F32))
        scratch += [pltpu.SemaphoreType.DMA((N_DEV - 1,)), pltpu.SemaphoreType.DMA((N_DEV - 1,)),
                    pltpu.SemaphoreType.DMA]
    return pl.pallas_call(
        body, name=name,
        in_specs=[HBM_SPEC] * (n + ns), out_specs=[HBM_SPEC] * (n + ns),
        out_shape=out_shape, scratch_shapes=scratch,
    )(*partials, *([small] if ns else []))


def _chip_sum(partial, from_sibling, c, name, tr=256):
    _, hr, C = from_sibling.shape
    nb = hr // tr

    def body(c_ref, p_ref, s_ref, o_ref):
        o_ref[...] = (p_ref[...] + s_ref[...]).astype(BF16)

    return pl.pallas_call(
        body, name=name,
        grid_spec=pltpu.PrefetchScalarGridSpec(
            num_scalar_prefetch=1, grid=(N_CHIPS, nb),
            in_specs=[pl.BlockSpec((1, tr, C), lambda j, i, c_ref: (j, c_ref[0] * nb + i, 0)),
                      pl.BlockSpec((1, tr, C), lambda j, i, c_ref: (j, i, 0))],
            out_specs=pl.BlockSpec((1, tr, C), lambda j, i, c_ref: (j, i, 0))),
        out_shape=jax.ShapeDtypeStruct(from_sibling.shape, BF16),
        compiler_params=_params(2),
    )(c, partial, from_sibling)


def _reduce_half(partial, from_sibling, received, place, name, tr=256):
    _, hr, C = from_sibling.shape
    nb = hr // tr

    def body(p_ref, mine_ref, sib_ref, r_ref, o_ref):
        acc = mine_ref[0] + sib_ref[0]
        for k in range(3):
            acc = acc + r_ref[k].astype(F32)
        o_ref[...] = acc

    return pl.pallas_call(
        body, name=name,
        grid_spec=pltpu.PrefetchScalarGridSpec(
            num_scalar_prefetch=1, grid=(nb,),
            in_specs=[pl.BlockSpec((1, tr, C), lambda i, p: (p[0], p[1] * nb + i, 0)),
                      pl.BlockSpec((1, tr, C), lambda i, p: (p[0], i, 0)),
                      pl.BlockSpec((3, tr, C), lambda i, p: (0, i, 0))],
            out_specs=pl.BlockSpec((tr, C), lambda i, p: (p[1] * nb + i, 0))),
        out_shape=jax.ShapeDtypeStruct((2 * hr, C), F32),
        compiler_params=_params(1),
    )(place, partial, from_sibling, received)


def _join_halves(fulls, name):
    n = len(fulls)

    def body(*refs):
        outs = refs[n:2 * n]
        send_sems, recv_sems = refs[2 * n:]
        x, y, c = _place()

        def copy(a, half_of, to):
            rows = outs[a].at[_half(outs[a], half_of)]
            return pltpu.make_async_remote_copy(
                src_ref=rows, dst_ref=rows, send_sem=send_sems.at[a], recv_sem=recv_sems.at[a],
                device_id=to, device_id_type=MESH)

        sends = [copy(a, c, (x, y, 1 - c)) for a in range(n)]
        for cp in sends:
            cp.start()
        for a in range(n):
            copy(a, 1 - c, (x, y, c)).wait_recv()
        for cp in sends:
            cp.wait_send()

    return pl.pallas_call(
        body, name=name,
        in_specs=[HBM_SPEC] * n, out_specs=[HBM_SPEC] * n,
        out_shape=[jax.ShapeDtypeStruct(f.shape, F32) for f in fulls],
        input_output_aliases={a: a for a in range(n)},
        scratch_shapes=[pltpu.SemaphoreType.DMA((n,)), pltpu.SemaphoreType.DMA((n,))],
    )(*fulls)


def _adamw_math(w, g, m, v):
    m = ADAM_B1 * m + (1.0 - ADAM_B1) * g
    v = ADAM_B2 * v + (1.0 - ADAM_B2) * (g * g)
    m_hat = m / (1.0 - ADAM_B1 ** ADAM_STEP)
    v_hat = v / (1.0 - ADAM_B2 ** ADAM_STEP)
    delta = -ADAM_LR * (m_hat / (jnp.sqrt(v_hat) + ADAM_EPS) + ADAM_WD * w)
    return delta, m, v


def _adamw(w, g, m, v, name, tr=256):
    R, C = w.shape
    tr = min(tr, R)

    def body(w_ref, g_ref, m_ref, v_ref, d_out, m_out, v_out):
        d_out[...], m_out[...], v_out[...] = _adamw_math(w_ref[...], g_ref[...], m_ref[...], v_ref[...])

    spec = pl.BlockSpec((tr, C), lambda i: (i, 0))
    return pl.pallas_call(
        body, name=name, grid=(R // tr,),
        in_specs=[spec] * 4, out_specs=[spec] * 3,
        out_shape=[jax.ShapeDtypeStruct((R, C), F32)] * 3,
        compiler_params=_params(1),
    )(w, g, m, v)


def _adamw_small(small_all, w, m, v, name):
    def body(s_ref, w_ref, m_ref, v_ref, g_out, d_out, m_out, v_out):
        g = s_ref[0]
        for d in range(1, N_DEV):
            g = g + s_ref[d]
        g_out[...] = g
        d_out[...], m_out[...], v_out[...] = _adamw_math(w_ref[...], g, m_ref[...], v_ref[...])

    vm = pl.BlockSpec(memory_space=pltpu.VMEM)
    return pl.pallas_call(
        body, name=name, in_specs=[vm] * 4, out_specs=[vm] * 4,
        out_shape=[jax.ShapeDtypeStruct(w.shape, F32)] * 4,
    )(small_all, w, m, v)


def _pack_small(norm_g, pool_scale, norm_f, extra_row):
    return jnp.concatenate([norm_g.reshape(2, D_MODEL), pool_scale.reshape(2, D_MODEL),
                            norm_f.reshape(1, D_MODEL), extra_row,
                            jnp.zeros((2, D_MODEL), F32)], axis=0)


def kernel(x, norm_g, pool_w_in, pool_w, pool_scale, pool_w_out, sb_w_in, sb_w_out, norm_f, loss_target, m_norm_g, m_pool_w_in, m_pool_w, m_pool_scale, m_pool_w_out, m_sb_w_in, m_sb_w_out, m_norm_f, v_norm_g, v_pool_w_in, v_pool_w, v_pool_scale, v_pool_w_out, v_sb_w_in, v_sb_w_out, v_norm_f):
    nb, S, _ = x.shape
    T = nb * S
    xt = x.reshape(T, D_MODEL)
    target = loss_target.reshape(T, D_MODEL)
    cx, cy, cc = _place()

    def shard2d(w):
        return w.reshape(-1, w.shape[-1])

    names = ("pool_w_in", "pool_w", "pool_w_out", "sb_w_in", "sb_w_out")
    w_shards = [shard2d(w) for w in (pool_w_in, pool_w, pool_w_out, sb_w_in, sb_w_out)]
    m_shards = [shard2d(w) for w in (m_pool_w_in, m_pool_w, m_pool_w_out, m_sb_w_in, m_sb_w_out)]
    v_shards = [shard2d(w) for w in (v_pool_w_in, v_pool_w, v_pool_w_out, v_sb_w_in, v_sb_w_out)]

    chip = (2 * cx + cy).reshape(1).astype(jnp.int32)
    c_arr = cc.reshape(1).astype(jnp.int32)
    place = jnp.stack([2 * cx + cy, cc]).astype(jnp.int32)
    slots = [_cast_to_slot(w, chip, "cast_" + nm) for w, nm in zip(w_shards, names)]
    g0, g1, gf = norm_g[0:1], norm_g[1:2], norm_f.reshape(1, D_MODEL)

    w_pin, = _allgather_weights(slots[:1], "allgather_pool_in_weights")
    mix_send, mix_recv, mix_slots, token = _exchange_start(slots[1:3], w_pin, _weight_plan, 6,
                                                           "pool_weights_start")
    sb_send, sb_recv, sb_slots, token = _exchange_start(slots[3:], token, _weight_plan, 6,
                                                        "sb_weights_start")

    proj0, u0 = _rms_matmul(xt, g0 + token[0:1, 0:1], w_pin, [(2 * D_INNER, BF16)], "pool_in_proj",
                            tn=w_pin.shape[2])
    mix_slots = _exchange_wait(mix_slots, mix_send, mix_recv, proj0, _weight_plan, "pool_weights_wait")
    w_g, w_pout = _allgather_weights(mix_slots, "pool_weights_forward", landed=True)
    w_pout = w_pout.reshape(D_INNER, D_MODEL)
    y0, pooled, mixed = _pool_fwd(proj0, w_g, pool_scale, S, "pool_mix")
    sb_slots = _exchange_wait(sb_slots, sb_send, sb_recv, y0, _weight_plan, "sb_weights_wait")
    f_send, f_recv, sb_slots, token = _exchange_start(sb_slots, y0, _forward_plan, 6,
                                                      "sb_weights_forward_start")
    h1 = _matmul_residual(y0, w_pout, xt, token, "pool_out_proj")
    w_sin, w_sout = _exchange_wait(sb_slots, f_send, f_recv, h1, _forward_plan,
                                   "sb_weights_forward_wait")
    w_sout = w_sout.reshape(D_INNER, D_MODEL)
    n1 = 2 * QK_WIDTH + 2 * D_INNER
    qkvz, u1 = _rms_matmul(h1, g1, w_sin, [(n1, BF16)], "sb_in_proj", tn=w_sin.shape[2])
    o, y1, ltot = _attn_fwd(qkvz, S, "sb_attention")
    dh2, d_norm_f, loss_row = _out_proj_loss_head(y1, w_sout, h1, gf, target, "sb_out_proj_loss_head")

    def reduce_start(partials, tag, after=None, behind=None):
        n, done = len(partials), None
        after = c_arr if after is None else after
        if behind is None:
            from_sibling = list(_sibling_exchange(partials, None, "grad_sibling_exchange_" + tag))
        else:
            lands = [lax.empty((N_CHIPS, p.shape[1] // 2, p.shape[2]), F32) for p in partials]
            send, recv, bufs, token = _exchange_start(partials + lands, after, _sibling_plan, n,
                                                      "grad_sibling_start_" + tag)
            done = behind(token[0:1, 0:1])
            bufs = _exchange_wait(bufs, send, recv, done[0], _sibling_plan, "grad_sibling_wait_" + tag)
            partials, from_sibling, after = bufs[:n], bufs[n:], c_arr
        sums = [_chip_sum(p, s, c_arr, "grad_chip_sum_%s_%d" % (tag, i))
                for i, (p, s) in enumerate(zip(partials, from_sibling))]
        lands = [lax.empty((3,) + s.shape[1:], BF16) for s in sums]
        send, recv, bufs, token = _exchange_start(sums + lands, after, _chip_sum_plan, 3 * n,
                                                  "grad_chip_exchange_start_" + tag)
        return (partials, from_sibling, send, recv, bufs), token[0:1, 0:1], done

    def reduce_finish(started, after, tag):
        partials, from_sibling, send, recv, bufs = started
        received = _exchange_wait(bufs, send, recv, after, _chip_sum_plan,
                                  "grad_chip_exchange_wait_" + tag)[len(partials):]
        return [_reduce_half(p, s, r, place, "grad_reduce_%s_%d" % (tag, i))
                for i, (p, s, r) in enumerate(zip(partials, from_sibling, received))]

    shard = lambda i, j, t: (j, 0, 0)
    gw_sout = _matmul_tn(y1, dh2, D_INNER, D_MODEL, (D_INNER, D_MODEL), (1024, 1024),
                         lambda i, j, t: (i, j), "grad_sb_w_out", bm=1024, bn=1024)
    sout_started, token, (do, dproj1) = reduce_start(
        [gw_sout.reshape(N_CHIPS, -1, D_MODEL)], "sb_out",
        behind=lambda tok: _attn_gate_bwd(dh2, w_sout, qkvz, o, tok, "sb_gate_bwd"))
    dproj1 = _attn_bwd(qkvz, do, ltot, dproj1, S, "sb_attention_bwd")
    gw_sin = _matmul_tn(u1, dproj1, D_MODEL, n1, (N_CHIPS, D_MODEL, n1 // 4), (1, D_MODEL, n1 // 4),
                        shard, "grad_sb_w_in", bm=D_MODEL, bn=n1 // 4)
    sin_started, token, (dh1, d_g1) = reduce_start(
        [gw_sin], "sb_in", after=token,
        behind=lambda tok: _matmul_nt_rms_bwd(dproj1, w_sin, h1, g1 + tok, dh2, "sb_in_bwd",
                                              tk=w_sin.shape[2]))
    gw_pout = _matmul_tn(y0, dh1, D_INNER, D_MODEL, (D_INNER, D_MODEL), (1024, 1024),
                         lambda i, j, t: (i, j), "grad_pool_w_out", bm=1024, bn=1024)
    dmixed, dproj0, d_scale = _pool_gate_bwd(dh1, w_pout, proj0, mixed, pool_scale + token,
                                             "pool_gate_bwd")
    gw_g = _matmul_tn(pooled, dmixed, D_INNER, D_INNER, (N_CHIPS, GROUP_DIM, GROUP_DIM),
                      (N_CHIPS, GROUP_DIM // N_CHIPS, GROUP_DIM), lambda i, j, t: (0, i, 0),
                      "grad_pool_w", bm=GROUP_DIM, bn=GROUP_DIM, diagonal_blocks=True)
    mix_started, token, (dproj0,) = reduce_start(
        [gw_g, gw_pout.reshape(N_CHIPS, -1, D_MODEL)], "pool_mix",
        behind=lambda tok: (_pool_bwd(dmixed, w_g, dproj0, tok, S, "pool_bwd"),))
    n0 = 2 * D_INNER
    gw_pin = _matmul_tn(u0, dproj0, D_MODEL, n0, (N_CHIPS, D_MODEL, n0 // 4), (1, D_MODEL, n0 // 4),
                        shard, "grad_pool_w_in", bm=D_MODEL, bn=n0 // 4)
    pin_started, token, _ = reduce_start([gw_pin], "pool_in", after=token)
    dx, d_g0 = _matmul_nt_rms_bwd(dproj0, w_pin, xt, g0 + token, dh1, "pool_in_bwd", tk=w_pin.shape[2])

    small = _pack_small(jnp.concatenate([d_g0, d_g1], axis=0), d_scale, d_norm_f,
                        jnp.broadcast_to(loss_row[:, :1], (1, D_MODEL)))
    small_all, = _sibling_exchange([], small, "small_sums_exchange")
    grads = _join_halves(reduce_finish(pin_started, dx, "pool_in")
                         + reduce_finish(mix_started, dx, "pool_mix")
                         + reduce_finish(sin_started, dx, "sb_in")
                         + reduce_finish(sout_started, dx, "sb_out"), "grad_join_halves")

    deltas, new_m, new_v = [], [], []
    for w, g, m, v, nm in zip(w_shards, grads, m_shards, v_shards, names):
        d, mm, vv = _adamw(w, g, m, v, "adamw_" + nm)
        deltas.append(d)
        new_m.append(mm)
        new_v.append(vv)

    zero_row = jnp.zeros((1, D_MODEL), F32)
    g_small, d_small, m_small, v_small = _adamw_small(
        small_all, _pack_small(norm_g, pool_scale, norm_f, zero_row),
        _pack_small(m_norm_g, m_pool_scale, m_norm_f, zero_row),
        _pack_small(v_norm_g, v_pool_scale, v_norm_f, zero_row + 1.0), "adamw_small")
    loss = g_small[5, 0]

    def unpack_small(a):
        return a[0:2], a[2:4].reshape(1, D_INNER), a[4]

    def assemble(big, small3):
        ng, ps, nf = small3
        return [ng, big[0].reshape(pool_w_in.shape), big[1].reshape(pool_w.shape), ps,
                big[2].reshape(pool_w_out.shape), big[3].reshape(sb_w_in.shape),
                big[4].reshape(sb_w_out.shape), nf]

    return (loss, dx.reshape(x.shape),
            *assemble(grads, unpack_small(g_small)),
            *assemble(deltas, unpack_small(d_small)),
            *assemble(new_m, unpack_small(m_small)),
            *assemble(new_v, unpack_small(v_small)))
```

```python
import jax
import jax.numpy as jnp
from jax import lax
from jax.experimental import pallas as pl
from jax.experimental.pallas import tpu as pltpu

F32 = jnp.float32
BF16 = jnp.bfloat16
MESH = pl.DeviceIdType.MESH

D_MODEL = 1024
D_INNER = 2048
N_GROUPS = 4
GROUP_DIM = 512
HEAD_PAIR_QK = 128
HEAD_V = 128
QK_WIDTH = 1024
RMS_EPS = 1e-6
HALO = 16
N_CHIPS = 4
N_DEV = 8

ADAM_LR = 0.001
ADAM_B1 = 0.9
ADAM_B2 = 0.999
ADAM_EPS = 1e-08
ADAM_WD = 0.01
ADAM_STEP = 10

VMEM_LIMIT = 56 * 1024 * 1024

HBM_SPEC = pl.BlockSpec(memory_space=pltpu.HBM)


def _params(n_axes):
    return pltpu.CompilerParams(dimension_semantics=("arbitrary",) * n_axes,
                                vmem_limit_bytes=VMEM_LIMIT)


def _dot(a, b):
    return jnp.dot(a, b, preferred_element_type=F32)


def _dot_nt(a, b):
    return lax.dot_general(a, b, (((1,), (1,)), ((), ())), preferred_element_type=F32)


def _dot_tn(a, b):
    return lax.dot_general(a, b, (((0,), (0,)), ((), ())), preferred_element_type=F32)


def _sigmoid(z):
    return 1.0 / (1.0 + jnp.exp(-z))


def _row_blocks(tm, rows=256):
    return [slice(r, r + rows) for r in range(0, tm, rows)]


def _rms_matmul(h, g_row, w4, outs, name, tm=1024, tn=512):
    T = h.shape[0]
    per_shard = w4.shape[2] // tn
    starts = [0]
    for width, _ in outs:
        starts.append(starts[-1] + width // tn)
    n_out = len(outs)

    def body(h_ref, g_ref, w_ref, *rest):
        o_refs, u_out, u_s = rest[:n_out], rest[n_out], rest[n_out + 1]
        n = pl.program_id(1)

        @pl.when(n == 0)
        def _():
            x = h_ref[...]
            inv = lax.rsqrt(jnp.mean(x * x, axis=-1, keepdims=True) + RMS_EPS)
            u = (x * inv * g_ref[...]).astype(BF16)
            u_s[...] = u
            u_out[...] = u

        res = _dot(u_s[...], w_ref[0])
        for k in range(n_out):
            @pl.when((n >= starts[k]) & (n < starts[k + 1]))
            def _():
                o_refs[k][...] = res.astype(o_refs[k].dtype)

    def out_map(k):
        return lambda m, n: (m, jnp.clip(n - starts[k], 0, starts[k + 1] - starts[k] - 1))

    return pl.pallas_call(
        body, name=name, grid=(T // tm, starts[-1]),
        in_specs=[pl.BlockSpec((tm, D_MODEL), lambda m, n: (m, 0)),
                  pl.BlockSpec((1, D_MODEL), lambda m, n: (0, 0)),
                  pl.BlockSpec((1, D_MODEL, tn), lambda m, n: (n // per_shard, 0, n % per_shard))],
        out_specs=[pl.BlockSpec((tm, tn), out_map(k)) for k in range(n_out)]
        + [pl.BlockSpec((tm, D_MODEL), lambda m, n: (m, 0))],
        out_shape=[jax.ShapeDtypeStruct((T, width), dt) for width, dt in outs]
        + [jax.ShapeDtypeStruct((T, D_MODEL), BF16)],
        scratch_shapes=[pltpu.VMEM((tm, D_MODEL), BF16)],
        compiler_params=_params(2),
    )(h, g_row, w4)


def _matmul_residual(a, w, res, name, tm=1024, tn=1024):
    T, K = a.shape
    N = w.shape[1]

    def body(a_ref, w_ref, r_ref, o_ref):
        o_ref[...] = r_ref[...] + _dot(a_ref[...], w_ref[...])

    return pl.pallas_call(
        body, name=name, grid=(T // tm, N // tn),
        in_specs=[pl.BlockSpec((tm, K), lambda m, n: (m, 0)),
                  pl.BlockSpec((K, tn), lambda m, n: (0, n)),
                  pl.BlockSpec((tm, tn), lambda m, n: (m, n))],
        out_specs=pl.BlockSpec((tm, tn), lambda m, n: (m, n)),
        out_shape=jax.ShapeDtypeStruct((T, N), F32),
        compiler_params=_params(2),
    )(a, w, res)


def _matmul_tn(a, b, a_cols, b_cols, out_shape, out_block, out_map, name, bm, bn, tk=2048,
               diagonal_blocks=False):
    T = a.shape[0]
    tk = min(tk, T)

    def body(a_ref, b_ref, o_ref):
        @pl.when(pl.program_id(2) == 0)
        def _():
            o_ref[...] = jnp.zeros_like(o_ref)

        part = _dot_tn(a_ref[...].astype(BF16), b_ref[...].astype(BF16))
        o_ref[...] += part.reshape(o_ref.shape)

    b_map = (lambda i, j, t: (t, i)) if diagonal_blocks else (lambda i, j, t: (t, j))
    return pl.pallas_call(
        body, name=name, grid=(a_cols // bm, 1 if diagonal_blocks else b_cols // bn, T // tk),
        in_specs=[pl.BlockSpec((tk, bm), lambda i, j, t: (t, i)),
                  pl.BlockSpec((tk, bn), b_map)],
        out_specs=pl.BlockSpec(out_block, out_map),
        out_shape=jax.ShapeDtypeStruct(out_shape, F32),
        compiler_params=_params(3),
    )(a, b)


def _matmul_nt_rms_bwd(dproj, w4, h, g_row, dres, name, tm=1024, tk=512):
    T, cols = dproj.shape
    per_shard = w4.shape[2] // tk
    nk = cols // tk

    def body(dp_ref, w_ref, h_ref, g_ref, r_ref, dx_ref, dg_ref, acc):
        m, k = pl.program_id(0), pl.program_id(1)

        @pl.when(k == 0)
        def _():
            acc[...] = jnp.zeros_like(acc)

        @pl.when((k == 0) & (m == 0))
        def _():
            dg_ref[...] = jnp.zeros_like(dg_ref)

        acc[...] += _dot_nt(dp_ref[...], w_ref[0])

        @pl.when(k == nk - 1)
        def _():
            du = acc[...]
            x = h_ref[...]
            inv = lax.rsqrt(jnp.mean(x * x, axis=-1, keepdims=True) + RMS_EPS)
            xhat = x * inv
            dg_ref[...] += jnp.sum(du * xhat, axis=0, keepdims=True)
            dxh = du * g_ref[...]
            proj = jnp.mean(dxh * xhat, axis=-1, keepdims=True)
            dx_ref[...] = r_ref[...] + inv * (dxh - xhat * proj)

    return pl.pallas_call(
        body, name=name, grid=(T // tm, nk),
        in_specs=[pl.BlockSpec((tm, tk), lambda m, k: (m, k)),
                  pl.BlockSpec((1, D_MODEL, tk), lambda m, k: (k // per_shard, 0, k % per_shard)),
                  pl.BlockSpec((tm, D_MODEL), lambda m, k: (m, 0)),
                  pl.BlockSpec((1, D_MODEL), lambda m, k: (0, 0)),
                  pl.BlockSpec((tm, D_MODEL), lambda m, k: (m, 0))],
        out_specs=[pl.BlockSpec((tm, D_MODEL), lambda m, k: (m, 0)),
                   pl.BlockSpec((1, D_MODEL), lambda m, k: (0, 0))],
        out_shape=[jax.ShapeDtypeStruct((T, D_MODEL), F32),
                   jax.ShapeDtypeStruct((1, D_MODEL), F32)],
        scratch_shapes=[pltpu.VMEM((tm, D_MODEL), F32)],
        compiler_params=_params(2),
    )(dproj, w4, h, g_row, dres)


def _window_of(g):
    return jnp.left_shift(2, g)


def _select_stage(g, stages):
    res = stages[0]
    for i in range(1, len(stages)):
        res = jnp.where(g >= i, stages[i], res)
    return res


def _pool_fwd(proj0, wg4, scale_row, S, name, tm=1024):
    T = proj0.shape[0]
    tm = min(tm, S)
    blocks_per_seq = S // tm
    hb = tm // HALO

    def body(x_ref, halo_ref, z_ref, w_ref, s_ref, y_ref, p_ref, mix_ref):
        m, g = pl.program_id(0), pl.program_id(1)
        first = (m % blocks_per_seq) == 0
        halo = jnp.where(first, 0.0, halo_ref[...].astype(F32))
        x = x_ref[...].astype(F32)
        ext = jnp.concatenate([halo, x], axis=0)
        stages = []
        cur = ext
        for sh in (1, 2, 4, 8):
            cur = cur + pltpu.roll(cur, sh, 0)
            stages.append(cur[HALO:, :])
        win_sum = _select_stage(g, stages)
        pos = (m % blocks_per_seq) * tm + lax.broadcasted_iota(jnp.int32, (tm, 1), 0)
        count = jnp.minimum(pos + 1, _window_of(g)).astype(F32)
        p_ref[...] = (win_sum / count - x).astype(BF16)
        w = w_ref[...].reshape(GROUP_DIM, GROUP_DIM)
        for rows in _row_blocks(tm):
            mixed = _dot(p_ref[rows, :], w)
            z = z_ref[rows, :].astype(F32)
            y_ref[rows, :] = (mixed * s_ref[...] * (z * _sigmoid(z))).astype(BF16)
            mix_ref[rows, :] = mixed.astype(BF16)

    blk = lambda m, g: (m, g)
    return pl.pallas_call(
        body, name=name, grid=(T // tm, N_GROUPS),
        in_specs=[pl.BlockSpec((tm, GROUP_DIM), blk),
                  pl.BlockSpec((HALO, GROUP_DIM), lambda m, g: (jnp.maximum(m * hb - 1, 0), g)),
                  pl.BlockSpec((tm, GROUP_DIM), lambda m, g: (m, N_GROUPS + g)),
                  pl.BlockSpec((N_CHIPS, GROUP_DIM // N_CHIPS, GROUP_DIM), lambda m, g: (0, g, 0)),
                  pl.BlockSpec((1, GROUP_DIM), lambda m, g: (0, g))],
        out_specs=[pl.BlockSpec((tm, GROUP_DIM), blk)] * 3,
        out_shape=[jax.ShapeDtypeStruct((T, D_INNER), BF16)] * 3,
        compiler_params=_params(2),
    )(proj0, proj0, proj0, wg4, scale_row)


def _pool_gate_bwd(dh, w_out, proj0, mixed, scale_row, name, tm=1024, tn=512):
    T = dh.shape[0]
    gate_b0 = D_INNER // tn

    def body(dh_ref, w_ref, z_ref, mix_ref, s_ref, dm_ref, dz_ref, ds_ref, dh_s):
        m, n = pl.program_id(0), pl.program_id(1)

        @pl.when((m == 0) & (n == 0))
        def _():
            ds_ref[...] = jnp.zeros_like(ds_ref)

        @pl.when(n == 0)
        def _():
            dh_s[...] = dh_ref[...].astype(BF16)

        cols = pl.ds(pl.multiple_of(n * tn, tn), tn)
        s = s_ref[...]
        ds = ds_ref[:, cols]
        for rows in _row_blocks(tm):
            dy = _dot_nt(dh_s[rows, :], w_ref[...])
            z = z_ref[rows, :].astype(F32)
            sig = _sigmoid(z)
            silu = z * sig
            mixed = mix_ref[rows, :].astype(F32)
            dm_ref[rows, :] = (dy * s * silu).astype(BF16)
            dz_ref[rows, :] = (dy * mixed * s * (sig * (1.0 + z * (1.0 - sig)))).astype(BF16)
            ds = ds + jnp.sum(dy * mixed * silu, axis=0, keepdims=True)
        ds_ref[:, cols] = ds

    return pl.pallas_call(
        body, name=name, grid=(T // tm, D_INNER // tn),
        in_specs=[pl.BlockSpec((tm, D_MODEL), lambda m, n: (m, 0)),
                  pl.BlockSpec((tn, D_MODEL), lambda m, n: (n, 0)),
                  pl.BlockSpec((tm, tn), lambda m, n: (m, gate_b0 + n)),
                  pl.BlockSpec((tm, tn), lambda m, n: (m, n)),
                  pl.BlockSpec((1, tn), lambda m, n: (0, n))],
        out_specs=[pl.BlockSpec((tm, tn), lambda m, n: (m, n)),
                   pl.BlockSpec((tm, tn), lambda m, n: (m, gate_b0 + n)),
                   pl.BlockSpec((1, D_INNER), lambda m, n: (0, 0))],
        out_shape=[jax.ShapeDtypeStruct((T, D_INNER), BF16),
                   jax.ShapeDtypeStruct((T, 2 * D_INNER), BF16),
                   jax.ShapeDtypeStruct((1, D_INNER), F32)],
        scratch_shapes=[pltpu.VMEM((tm, D_MODEL), BF16)],
        compiler_params=_params(2),
    )(dh, w_out, proj0, mixed, scale_row)


def _pool_bwd(dmixed, wg4, dproj0, after, S, name, tm=1024):
    T = dmixed.shape[0]
    tm = min(tm, S)
    blocks_per_seq = S // tm
    hb = tm // HALO
    n_halo_blocks = T // HALO

    def body(dm_ref, halo_ref, w_ref, _, __, o_ref):
        m, g = pl.program_id(0), pl.program_id(1)
        ext = jnp.concatenate([dm_ref[...], halo_ref[...]], axis=0)
        dp = _dot_nt(ext, w_ref[...].reshape(GROUP_DIM, GROUP_DIM))
        pos = (m % blocks_per_seq) * tm + lax.broadcasted_iota(jnp.int32, (tm + HALO, 1), 0)
        count = jnp.minimum(pos + 1, _window_of(g)).astype(F32)
        c = jnp.where(pos < S, dp / count, 0.0)
        n = tm + HALO
        stages = []
        cur = c
        for sh in (1, 2, 4, 8):
            cur = cur + pltpu.roll(cur, n - sh, 0)
            stages.append(cur[:tm, :])
        o_ref[...] = (_select_stage(g, stages) - dp[:tm, :]).astype(BF16)

    blk = lambda m, g: (m, g)
    return pl.pallas_call(
        body, name=name, grid=(T // tm, N_GROUPS),
        in_specs=[pl.BlockSpec((tm, GROUP_DIM), blk),
                  pl.BlockSpec((HALO, GROUP_DIM),
                               lambda m, g: (jnp.minimum((m + 1) * hb, n_halo_blocks - 1), g)),
                  pl.BlockSpec((N_CHIPS, GROUP_DIM // N_CHIPS, GROUP_DIM), lambda m, g: (0, g, 0)),
                  HBM_SPEC, ANY_SPEC],
        out_specs=pl.BlockSpec((tm, GROUP_DIM), blk),
        out_shape=jax.ShapeDtypeStruct(dproj0.shape, dproj0.dtype),
        input_output_aliases={3: 0},
        compiler_params=_params(2),
    )(dmixed, dmixed, wg4, dproj0, after)


TQ = 256


def _split_dot(x, m):
    hi = x.astype(BF16)
    lo = (x - hi.astype(F32)).astype(BF16)
    return _dot(hi, m) + _dot(lo, m)


NEG_LOG2E = -1.4426950408889634


def _log_terms(z):
    soft = jnp.log(1.0 + jnp.exp2(jnp.abs(z) * NEG_LOG2E))
    log_beta = jnp.minimum(z, 0.0) - soft
    return log_beta, log_beta - z


N_HEADS = 16
FWD_HEADS = BWD_HEADS = 4


def _masked_heads(x, heads):
    lane = lax.broadcasted_iota(jnp.int32, (1, HEAD_PAIR_QK), 1)
    out = []
    for hh in range(heads):
        slab = x[:, (hh // 2) * HEAD_PAIR_QK:(hh // 2 + 1) * HEAD_PAIR_QK]
        out.append(jnp.where((lane // 64) == hh % 2, slab, jnp.zeros_like(slab)))
    return out


def _attn_fwd(qkvz, S, name):
    T = qkvz.shape[0]
    nq = S // TQ
    HEADS, QK_W, V_W = FWD_HEADS, FWD_HEADS * 64, FWD_HEADS * HEAD_V
    k_b0 = QK_WIDTH // QK_W
    v_b0 = 2 * QK_WIDTH // V_W
    z_b0 = (2 * QK_WIDTH + D_INNER) // V_W
    hs = range(HEADS)

    def body(q_ref, k_ref, v_ref, z_ref, o_ref, y_ref, lt_ref):
        row = lax.broadcasted_iota(jnp.int32, (TQ, TQ), 0)
        col = lax.broadcasted_iota(jnp.int32, (TQ, TQ), 1)
        causal = col < row
        later_in_block = (row > col).astype(BF16)
        lax.fori_loop(0, nq, lambda qi, _: q_block(qi, causal, later_in_block,
                                                   q_ref, k_ref, v_ref, z_ref, o_ref, y_ref, lt_ref), 0)

    def q_block(qi, causal, later_in_block, q_ref, k_ref, v_ref, z_ref, o_ref, y_ref, lt_ref):
        rows = pl.ds(pl.multiple_of(qi * TQ, TQ), TQ)
        qms = [qm * 0.125 for qm in _masked_heads(q_ref[rows, :], HEADS)]

        def step(j, carry, diagonal):
            koff = pl.multiple_of(j * TQ, TQ)
            kbs = [k_ref[pl.ds(koff, TQ), p * HEAD_PAIR_QK:(p + 1) * HEAD_PAIR_QK]
                   for p in range(HEADS // 2)]
            run, acc = [carry[2 * hh] for hh in hs], [carry[2 * hh + 1] for hh in hs]
            z = [_dot_nt(qms[hh], kbs[hh // 2]) for hh in hs]
            terms = [_log_terms(z[hh]) for hh in hs]
            log_om = [jnp.where(causal, t[1], 0.0) if diagonal else t[1] for t in terms]
            later = [_split_dot(log_om[hh], later_in_block) for hh in hs]
            a = [jnp.exp(terms[hh][0] + (run[hh] + later[hh])) for hh in hs]
            if diagonal:
                a = [jnp.where(causal, a[hh], 0.0) for hh in hs]
            out = []
            for hh in hs:
                vb = v_ref[pl.ds(koff, TQ), hh * HEAD_V:(hh + 1) * HEAD_V]
                out += [run[hh] + jnp.sum(log_om[hh], axis=1, keepdims=True),
                        acc[hh] + _dot(a[hh].astype(BF16), vb)]
            return tuple(out)

        zero = (jnp.zeros((TQ, 1), F32), jnp.zeros((TQ, HEAD_V), F32))
        carry = step(qi, zero * HEADS, True)
        carry = lax.fori_loop(0, qi, lambda i, c: step(qi - 1 - i, c, False), carry)
        for hh in hs:
            sl = slice(hh * HEAD_V, (hh + 1) * HEAD_V)
            acc = carry[2 * hh + 1]
            z = z_ref[rows, sl].astype(F32)
            o_ref[rows, sl] = acc.astype(BF16)
            y_ref[rows, sl] = (acc * (z * _sigmoid(z))).astype(BF16)
            lt_ref[rows, hh:hh + 1] = carry[2 * hh]
        return 0

    blk = lambda b, p: (b, p)
    return pl.pallas_call(
        body, name=name, grid=(T // S, N_HEADS // HEADS),
        in_specs=[pl.BlockSpec((S, QK_W), blk),
                  pl.BlockSpec((S, QK_W), lambda b, p: (b, k_b0 + p)),
                  pl.BlockSpec((S, V_W), lambda b, p: (b, v_b0 + p)),
                  pl.BlockSpec((S, V_W), lambda b, p: (b, z_b0 + p))],
        out_specs=[pl.BlockSpec((S, V_W), blk),
                   pl.BlockSpec((S, V_W), blk),
                   pl.BlockSpec((None, S, HEADS), lambda b, p: (p, b, 0))],
        out_shape=[jax.ShapeDtypeStruct((T, D_INNER), BF16),
                   jax.ShapeDtypeStruct((T, D_INNER), BF16),
                   jax.ShapeDtypeStruct((N_HEADS // HEADS, T, HEADS), F32)],
        compiler_params=_params(2),
    )(qkvz, qkvz, qkvz, qkvz)


def _attn_gate_bwd(dh, w_out, qkvz, o, after, name, tm=1024, tn=512):
    T = dh.shape[0]
    gate_b0 = (2 * QK_WIDTH + D_INNER) // tn

    def body(dh_ref, w_ref, z_ref, o_ref, _, do_ref, dz_ref, dh_s):
        @pl.when(pl.program_id(1) == 0)
        def _():
            dh_s[...] = dh_ref[...].astype(BF16)

        for rows in _row_blocks(tm):
            dy = _dot_nt(dh_s[rows, :], w_ref[...])
            z = z_ref[rows, :].astype(F32)
            sig = _sigmoid(z)
            do_ref[rows, :] = (dy * (z * sig)).astype(BF16)
            dz_ref[rows, :] = (dy * o_ref[rows, :].astype(F32)
                               * (sig * (1.0 + z * (1.0 - sig)))).astype(BF16)

    return pl.pallas_call(
        body, name=name, grid=(T // tm, D_INNER // tn),
        in_specs=[pl.BlockSpec((tm, D_MODEL), lambda m, n: (m, 0)),
                  pl.BlockSpec((tn, D_MODEL), lambda m, n: (n, 0)),
                  pl.BlockSpec((tm, tn), lambda m, n: (m, gate_b0 + n)),
                  pl.BlockSpec((tm, tn), lambda m, n: (m, n)),
                  ANY_SPEC],
        out_specs=[pl.BlockSpec((tm, tn), lambda m, n: (m, n)),
                   pl.BlockSpec((tm, tn), lambda m, n: (m, gate_b0 + n))],
        out_shape=[jax.ShapeDtypeStruct((T, D_INNER), BF16),
                   jax.ShapeDtypeStruct((T, 2 * QK_WIDTH + 2 * D_INNER), BF16)],
        scratch_shapes=[pltpu.VMEM((tm, D_MODEL), BF16)],
        compiler_params=_params(2),
    )(dh, w_out, qkvz, o, after)


def _attn_bwd(qkv, do, ltot, dproj1, S, name):
    T = qkv.shape[0]
    nq = S // TQ
    HEADS, QK_W, V_W = BWD_HEADS, BWD_HEADS * 64, BWD_HEADS * HEAD_V
    k_b0 = QK_WIDTH // QK_W
    v_b0 = 2 * QK_WIDTH // V_W
    hs = range(HEADS)
    pairs = range(HEADS // 2)
    n_groups = N_HEADS // HEADS
    n_steps = (T // S) * n_groups

    def body(q_ref, k_ref, v_ref, do_ref, lt_ref, _, out_ref, dq_s, dk_s, dv_s, dkb_s, dvb_s, sems):
        b, p = pl.program_id(0), pl.program_id(1)
        row = lax.broadcasted_iota(jnp.int32, (TQ, TQ), 0)
        col = lax.broadcasted_iota(jnp.int32, (TQ, TQ), 1)
        causal = col < row
        upto = (row <= col).astype(BF16)
        before = (row < col).astype(BF16)
        dk_s[...] = jnp.zeros_like(dk_s)
        dv_s[...] = jnp.zeros_like(dv_s)
        rows = pl.ds(pl.multiple_of(b * S, TQ), S)
        copies = [
            pltpu.make_async_copy(
                dq_s, out_ref.at[rows, pl.ds(pl.multiple_of(p * QK_W, 128), QK_W)], sems.at[0]),
            pltpu.make_async_copy(
                dkb_s, out_ref.at[rows, pl.ds(pl.multiple_of(QK_WIDTH + p * QK_W, 128), QK_W)],
                sems.at[1]),
            pltpu.make_async_copy(
                dvb_s, out_ref.at[rows, pl.ds(pl.multiple_of(2 * QK_WIDTH + p * V_W, 128), V_W)],
                sems.at[2]),
        ]
        step = b * n_groups + p

        @pl.when(step > 0)
        def _():
            for cp in copies:
                cp.wait()

        def q_block(qi, _):
            qoff = pl.multiple_of(qi * TQ, TQ)
            qms = [qm * 0.125 for qm in _masked_heads(q_ref[pl.ds(qoff, TQ), :], HEADS)]
            vsl = [slice(hh * HEAD_V, (hh + 1) * HEAD_V) for hh in hs]
            psl = [slice(pp * HEAD_PAIR_QK, (pp + 1) * HEAD_PAIR_QK) for pp in pairs]
            do_h = [do_ref[pl.ds(qoff, TQ), sl] for sl in vsl]
            total = [lt_ref[pl.ds(qoff, TQ), hh:hh + 1] for hh in hs]

            def k_block(j, carry, diagonal):
                koff = pl.multiple_of(j * TQ, TQ)
                kms = _masked_heads(k_ref[pl.ds(koff, TQ), :], HEADS)
                g_before = [carry[2 * hh] for hh in hs]
                lom_before = [carry[2 * hh + 1] for hh in hs]
                z = [_dot_nt(qms[hh], kms[hh]) for hh in hs]
                da = [_dot_nt(do_h[hh], v_ref[pl.ds(koff, TQ), vsl[hh]]) for hh in hs]
                terms = [_log_terms(z[hh]) for hh in hs]
                log_om = [jnp.where(causal, t[1], 0.0) if diagonal else t[1] for t in terms]
                prefix = [_split_dot(log_om[hh], upto) for hh in hs]
                a = [jnp.exp(terms[hh][0] + ((total[hh] - lom_before[hh]) - prefix[hh])) for hh in hs]
                if diagonal:
                    a = [jnp.where(causal, a[hh], 0.0) for hh in hs]
                g = [a[hh] * da[hh] for hh in hs]
                g_prefix = [_dot(g[hh].astype(BF16), before) for hh in hs]
                out, dzs = [], []
                for hh in hs:
                    beta = jnp.exp(terms[hh][0])
                    g_excl = (g_before[hh] + g_prefix[hh]) * beta
                    if diagonal:
                        g_excl = jnp.where(causal, g_excl, 0.0)
                    dzs.append((g[hh] * (1.0 - beta) - g_excl).astype(BF16))
                    out += [g_before[hh] + jnp.sum(g[hh], axis=1, keepdims=True),
                            lom_before[hh] + jnp.sum(log_om[hh], axis=1, keepdims=True)]
                for hh in hs:
                    dv_s[pl.ds(koff, TQ), vsl[hh]] += _dot_tn(a[hh].astype(BF16), do_h[hh])
                dq = []
                for pp in pairs:
                    pair = slice(2 * pp, 2 * pp + 2)
                    dq.append(carry[2 * HEADS + pp] + _dot(jnp.concatenate(dzs[pair], axis=1),
                                                           jnp.concatenate(kms[pair], axis=0)))
                    dk_s[pl.ds(koff, TQ), psl[pp]] += _dot_tn(jnp.concatenate(dzs[pair], axis=0),
                                                              jnp.concatenate(qms[pair], axis=0))
                return tuple(out) + tuple(dq)

            zero = jnp.zeros((TQ, 1), F32)
            carry = (zero,) * (2 * HEADS) + (jnp.zeros((TQ, HEAD_PAIR_QK), F32),) * (HEADS // 2)
            carry = lax.fori_loop(0, qi, lambda j, c: k_block(j, c, False), carry)
            carry = k_block(qi, carry, True)
            for pp in pairs:
                dq_s[pl.ds(qoff, TQ), psl[pp]] = (carry[2 * HEADS + pp] * 0.125).astype(BF16)
            return 0

        lax.fori_loop(0, nq, q_block, 0)
        dkb_s[...] = dk_s[...].astype(BF16)
        dvb_s[...] = dv_s[...].astype(BF16)
        for cp in copies:
            cp.start()

        @pl.when(step == n_steps - 1)
        def _():
            for cp in copies:
                cp.wait()

    return pl.pallas_call(
        body, name=name, grid=(T // S, N_HEADS // HEADS),
        in_specs=[pl.BlockSpec((S, QK_W), lambda b, p: (b, p)),
                  pl.BlockSpec((S, QK_W), lambda b, p: (b, k_b0 + p)),
                  pl.BlockSpec((S, V_W), lambda b, p: (b, v_b0 + p)),
                  pl.BlockSpec((S, V_W), lambda b, p: (b, p)),
                  pl.BlockSpec((None, S, HEADS), lambda b, p: (p, b, 0)),
                  HBM_SPEC],
        out_specs=HBM_SPEC,
        out_shape=jax.ShapeDtypeStruct(dproj1.shape, dproj1.dtype),
        input_output_aliases={5: 0},
        scratch_shapes=[pltpu.VMEM((S, QK_W), BF16),
                        pltpu.VMEM((S, QK_W), F32),
                        pltpu.VMEM((S, V_W), F32),
                        pltpu.VMEM((S, QK_W), BF16),
                        pltpu.VMEM((S, V_W), BF16),
                        pltpu.SemaphoreType.DMA((3,))],
        compiler_params=_params(2),
    )(qkv, qkv, qkv, do, ltot, dproj1)


def _out_proj_loss_head(a, w, res, g_row, target, name, tm=512):
    T, K = a.shape

    def body(a_ref, w_ref, r_ref, g_ref, t_ref, dh_ref, dg_ref, loss_ref):
        @pl.when(pl.program_id(0) == 0)
        def _():
            dg_ref[...] = jnp.zeros_like(dg_ref)
            loss_ref[...] = jnp.zeros_like(loss_ref)

        gain = g_ref[...]
        dg, loss = dg_ref[...], loss_ref[...]
        for rows in _row_blocks(tm):
            x = r_ref[rows, :] + _dot(a_ref[rows, :], w_ref[...])
            inv = lax.rsqrt(jnp.mean(x * x, axis=-1, keepdims=True) + RMS_EPS)
            xhat = x * inv
            err = xhat * gain - t_ref[rows, :]
            per_token = jnp.mean(err * err, axis=-1, keepdims=True)
            loss = loss + 0.5 * jnp.sum(per_token, axis=0, keepdims=True)
            dy = err * (1.0 / D_MODEL)
            dg = dg + jnp.sum(dy * xhat, axis=0, keepdims=True)
            dxh = dy * gain
            proj = jnp.mean(dxh * xhat, axis=-1, keepdims=True)
            dh_ref[rows, :] = inv * (dxh - xhat * proj)
        dg_ref[...] = dg
        loss_ref[...] = loss

    return pl.pallas_call(
        body, name=name, grid=(T // tm,),
        in_specs=[pl.BlockSpec((tm, K), lambda m: (m, 0)),
                  pl.BlockSpec((K, D_MODEL), lambda m: (0, 0)),
                  pl.BlockSpec((tm, D_MODEL), lambda m: (m, 0)),
                  pl.BlockSpec((1, D_MODEL), lambda m: (0, 0)),
                  pl.BlockSpec((tm, D_MODEL), lambda m: (m, 0))],
        out_specs=[pl.BlockSpec((tm, D_MODEL), lambda m: (m, 0)),
                   pl.BlockSpec((1, D_MODEL), lambda m: (0, 0)),
                   pl.BlockSpec((1, 128), lambda m: (0, 0))],
        out_shape=[jax.ShapeDtypeStruct((T, D_MODEL), F32),
                   jax.ShapeDtypeStruct((1, D_MODEL), F32),
                   jax.ShapeDtypeStruct((1, 128), F32)],
        compiler_params=_params(1),
    )(a, w, res, g_row, target)


def _place():
    return lax.axis_index("x"), lax.axis_index("y"), lax.axis_index("c")


def _other_chips(x, y):
    return [(1 - x, y), (x, 1 - y), (1 - x, 1 - y)]


def _half(ref, c):
    hr = ref.shape[-2] // 2
    return pl.ds(pl.multiple_of(c * hr, 8), hr)


def _cast_to_slot(shard, chip, name, tr=256):
    R, C = shard.shape

    def body(chip_ref, w_ref, o_ref):
        o_ref[0] = w_ref[...].astype(BF16)

    return pl.pallas_call(
        body, name=name,
        grid_spec=pltpu.PrefetchScalarGridSpec(
            num_scalar_prefetch=1, grid=(R // tr,),
            in_specs=[pl.BlockSpec((tr, C), lambda i, chip_ref: (i, 0))],
            out_specs=pl.BlockSpec((1, tr, C), lambda i, chip_ref: (chip_ref[0], i, 0))),
        out_shape=jax.ShapeDtypeStruct((N_CHIPS, R, C), BF16),
        compiler_params=_params(1),
    )(chip, shard)


def _weight_plan(bufs):
    x, y, c = _place()
    plan = []
    for buf in bufs:
        mine = buf.at[2 * x + y, _half(buf, c)]
        for ox, oy in _other_chips(x, y):
            plan.append((mine, mine, (ox, oy, c), buf.at[2 * ox + oy, _half(buf, c)]))
    return plan


def _chip_sum_plan(bufs):
    x, y, c = _place()
    n = len(bufs) // 2
    plan = []
    for sums, land in zip(bufs[:n], bufs[n:]):
        for k, (ox, oy) in enumerate(_other_chips(x, y)):
            plan.append((sums.at[2 * ox + oy], land.at[k], (ox, oy, c), land.at[k]))
    return plan


def _sibling_plan(bufs):
    x, y, c = _place()
    n = len(bufs) // 2
    return [(p.at[:, _half(p, 1 - c)], land, (x, y, 1 - c), land)
            for p, land in zip(bufs[:n], bufs[n:])]


SEM_SPEC = pl.BlockSpec(memory_space=pltpu.SEMAPHORE)
ANY_SPEC = pl.BlockSpec(memory_space=pl.ANY)
DATAFLOW = pltpu.SideEffectType.DATAFLOW_SIDE_EFFECTING


def _in_hbm(a):
    return pltpu.with_memory_space_constraint(a, pltpu.HBM)


def _exchange_start(bufs, after, plan, n_copies, name):
    nb = len(bufs)

    def body(*refs):
        send_sems, recv_sems = refs[nb + 1], refs[nb + 2]
        for i, (src, dst, dev, _) in enumerate(plan(refs[:nb])):
            pltpu.make_async_remote_copy(
                src_ref=src, dst_ref=dst, send_sem=send_sems.at[i], recv_sem=recv_sems.at[i],
                device_id=dev, device_id_type=MESH).start()
        token = refs[-1]
        token[...] = jnp.zeros_like(token)

    res = pl.pallas_call(
        body, name=name,
        in_specs=[HBM_SPEC] * nb + [ANY_SPEC],
        out_specs=[SEM_SPEC, SEM_SPEC] + [HBM_SPEC] * nb + [pl.BlockSpec(memory_space=pltpu.VMEM)],
        out_shape=[pltpu.SemaphoreType.DMA((n_copies,)), pltpu.SemaphoreType.DMA((n_copies,))]
        + [pltpu.HBM(b.shape, b.dtype) for b in bufs] + [jax.ShapeDtypeStruct((8, 128), F32)],
        input_output_aliases={i: 2 + i for i in range(nb)},
        compiler_params=pltpu.CompilerParams(has_side_effects=DATAFLOW),
    )(*[_in_hbm(b) for b in bufs], after)
    return res[0], res[1], list(res[2:2 + nb]), res[-1]


def _exchange_wait(bufs, send_sems, recv_sems, after, plan, name):
    nb = len(bufs)

    def body(*refs):
        sends, recvs = refs[nb], refs[nb + 1]
        for i, (src, dst, dev, landing) in enumerate(plan(refs[:nb])):
            pltpu.make_async_remote_copy(
                src_ref=src, dst_ref=landing, send_sem=sends.at[i], recv_sem=recvs.at[i],
                device_id=dev, device_id_type=MESH).wait()

    res = pl.pallas_call(
        body, name=name,
        in_specs=[HBM_SPEC] * nb + [SEM_SPEC, SEM_SPEC, ANY_SPEC],
        out_specs=[HBM_SPEC] * nb,
        out_shape=[pltpu.HBM(b.shape, b.dtype) for b in bufs],
        input_output_aliases={i: i for i in range(nb)},
        compiler_params=pltpu.CompilerParams(has_side_effects=DATAFLOW),
    )(*bufs, send_sems, recv_sems, after)
    return list(res)


def _allgather_weights(slots, name, landed=False):
    n = len(slots)

    def body(*refs):
        outs = refs[n:2 * n]
        send_sems, recv_sems, fwd_send, fwd_recv = refs[2 * n:]
        x, y, c = _place()
        chips = _other_chips(x, y)

        def landing(a, chip, half_of):
            return outs[a].at[2 * chip[0] + chip[1], _half(outs[a], half_of)]

        def ici(a, k, chip_from, to):
            return pltpu.make_async_remote_copy(
                src_ref=landing(a, chip_from, c), dst_ref=landing(a, chip_from, c),
                send_sem=send_sems.at[a, k], recv_sem=recv_sems.at[a, k],
                device_id=to, device_id_type=MESH)

        def d2d(a, k, chip_from, half_of):
            return pltpu.make_async_remote_copy(
                src_ref=landing(a, chip_from, half_of), dst_ref=landing(a, chip_from, half_of),
                send_sem=fwd_send.at[a, k], recv_sem=fwd_recv.at[a, k],
                device_id=(x, y, 1 - c), device_id_type=MESH)

        sends = []
        if not landed:
            sends = [ici(a, k, (x, y), (*chips[k], c)) for a in range(n) for k in range(3)]
        for cp in sends:
            cp.start()
        forwards = []
        for a in range(n):
            for k in range(3):
                if not landed:
                    ici(a, k, chips[k], (x, y, c)).wait_recv()
                fw = d2d(a, k, chips[k], c)
                fw.start()
                forwards.append(fw)
        for a in range(n):
            for k in range(3):
                d2d(a, k, chips[k], 1 - c).wait_recv()
        for cp in sends + forwards:
            cp.wait_send()

    return pl.pallas_call(
        body, name=name,
        in_specs=[HBM_SPEC] * n, out_specs=[HBM_SPEC] * n,
        out_shape=[jax.ShapeDtypeStruct(s.shape, s.dtype) for s in slots],
        input_output_aliases={a: a for a in range(n)},
        scratch_shapes=[pltpu.SemaphoreType.DMA((n, 3)), pltpu.SemaphoreType.DMA((n, 3)),
                        pltpu.SemaphoreType.DMA((n, 3)), pltpu.SemaphoreType.DMA((n, 3))],
    )(*slots)


def _sibling_exchange(partials, small, name):
    n = len(partials)
    ns = 0 if small is None else 1

    def body(*refs):
        ins, outs = refs[:n], refs[n + ns:2 * n + ns]
        send_sems, recv_sems = refs[2 * (n + ns):2 * (n + ns) + 2]
        x, y, c = _place()
        me = 4 * x + 2 * y + c
        sends = [pltpu.make_async_remote_copy(
            src_ref=ins[a].at[:, _half(ins[a], 1 - c)], dst_ref=outs[a],
            send_sem=send_sems.at[a], recv_sem=recv_sems.at[a],
            device_id=(x, y, 1 - c), device_id_type=MESH) for a in range(n)]
        if ns:
            small_ref, small_all = refs[n], refs[2 * n + 1]
            s_send, s_recv, loc_sem = refs[2 * (n + ns) + 2:]
            local = pltpu.make_async_copy(small_ref, small_all.at[me], loc_sem)
            local.start()
            for d in range(1, N_DEV):
                px, py, pc = x ^ ((d >> 2) & 1), y ^ ((d >> 1) & 1), c ^ (d & 1)
                sends.append(pltpu.make_async_remote_copy(
                    src_ref=small_ref, dst_ref=small_all.at[me],
                    send_sem=s_send.at[d - 1], recv_sem=s_recv.at[d - 1],
                    device_id=(px, py, pc), device_id_type=MESH))
        for cp in sends:
            cp.start()
        if ns:
            for d in range(1, N_DEV):
                pltpu.make_async_remote_copy(
                    src_ref=small_ref, dst_ref=small_all.at[me ^ d],
                    send_sem=s_send.at[d - 1], recv_sem=s_recv.at[d - 1],
                    device_id=(x, y, c), device_id_type=MESH).wait_recv()
        for cp in sends[:n]:
            cp.wait_recv()
        for cp in sends:
            cp.wait_send()
        if ns:
            local.wait()

    out_shape = [jax.ShapeDtypeStruct((N_CHIPS, p.shape[1] // 2, p.shape[2]), F32) for p in partials]
    scratch = [pltpu.SemaphoreType.DMA((max(n, 1),)), pltpu.SemaphoreType.DMA((max(n, 1),))]
    if ns:
        out_shape.append(jax.ShapeDtypeStruct((N_DEV,) + small.shape, F32))
        scratch += [pltpu.SemaphoreType.DMA((N_DEV - 1,)), pltpu.SemaphoreType.DMA((N_DEV - 1,)),
                    pltpu.SemaphoreType.DMA]
    return pl.pallas_call(
        body, name=name,
        in_specs=[HBM_SPEC] * (n + ns), out_specs=[HBM_SPEC] * (n + ns),
        out_shape=out_shape, scratch_shapes=scratch,
    )(*partials, *([small] if ns else []))


def _chip_sum(partial, from_sibling, c, name, tr=256):
    _, hr, C = from_sibling.shape
    nb = hr // tr

    def body(c_ref, p_ref, s_ref, o_ref):
        o_ref[...] = (p_ref[...] + s_ref[...]).astype(BF16)

    return pl.pallas_call(
        body, name=name,
        grid_spec=pltpu.PrefetchScalarGridSpec(
            num_scalar_prefetch=1, grid=(N_CHIPS, nb),
            in_specs=[pl.BlockSpec((1, tr, C), lambda j, i, c_ref: (j, c_ref[0] * nb + i, 0)),
                      pl.BlockSpec((1, tr, C), lambda j, i, c_ref: (j, i, 0))],
            out_specs=pl.BlockSpec((1, tr, C), lambda j, i, c_ref: (j, i, 0))),
        out_shape=jax.ShapeDtypeStruct(from_sibling.shape, BF16),
        compiler_params=_params(2),
    )(c, partial, from_sibling)


def _reduce_half(partial, from_sibling, received, place, name, tr=256):
    _, hr, C = from_sibling.shape
    nb = hr // tr

    def body(p_ref, mine_ref, sib_ref, r_ref, o_ref):
        acc = mine_ref[0] + sib_ref[0]
        for k in range(3):
            acc = acc + r_ref[k].astype(F32)
        o_ref[...] = acc

    return pl.pallas_call(
        body, name=name,
        grid_spec=pltpu.PrefetchScalarGridSpec(
            num_scalar_prefetch=1, grid=(nb,),
            in_specs=[pl.BlockSpec((1, tr, C), lambda i, p: (p[0], p[1] * nb + i, 0)),
                      pl.BlockSpec((1, tr, C), lambda i, p: (p[0], i, 0)),
                      pl.BlockSpec((3, tr, C), lambda i, p: (0, i, 0))],
            out_specs=pl.BlockSpec((tr, C), lambda i, p: (p[1] * nb + i, 0))),
        out_shape=jax.ShapeDtypeStruct((2 * hr, C), F32),
        compiler_params=_params(1),
    )(place, partial, from_sibling, received)


def _join_halves(fulls, name):
    n = len(fulls)

    def body(*refs):
        outs = refs[n:2 * n]
        send_sems, recv_sems = refs[2 * n:]
        x, y, c = _place()

        def copy(a, half_of, to):
            rows = outs[a].at[_half(outs[a], half_of)]
            return pltpu.make_async_remote_copy(
                src_ref=rows, dst_ref=rows, send_sem=send_sems.at[a], recv_sem=recv_sems.at[a],
                device_id=to, device_id_type=MESH)

        sends = [copy(a, c, (x, y, 1 - c)) for a in range(n)]
        for cp in sends:
            cp.start()
        for a in range(n):
            copy(a, 1 - c, (x, y, c)).wait_recv()
        for cp in sends:
            cp.wait_send()

    return pl.pallas_call(
        body, name=name,
        in_specs=[HBM_SPEC] * n, out_specs=[HBM_SPEC] * n,
        out_shape=[jax.ShapeDtypeStruct(f.shape, F32) for f in fulls],
        input_output_aliases={a: a for a in range(n)},
        scratch_shapes=[pltpu.SemaphoreType.DMA((n,)), pltpu.SemaphoreType.DMA((n,))],
    )(*fulls)


def _adamw_math(w, g, m, v):
    m = ADAM_B1 * m + (1.0 - ADAM_B1) * g
    v = ADAM_B2 * v + (1.0 - ADAM_B2) * (g * g)
    m_hat = m / (1.0 - ADAM_B1 ** ADAM_STEP)
    v_hat = v / (1.0 - ADAM_B2 ** ADAM_STEP)
    delta = -ADAM_LR * (m_hat / (jnp.sqrt(v_hat) + ADAM_EPS) + ADAM_WD * w)
    return delta, m, v


def _adamw(w, g, m, v, name, tr=256):
    R, C = w.shape
    tr = min(tr, R)

    def body(w_ref, g_ref, m_ref, v_ref, d_out, m_out, v_out):
        d_out[...], m_out[...], v_out[...] = _adamw_math(w_ref[...], g_ref[...], m_ref[...], v_ref[...])

    spec = pl.BlockSpec((tr, C), lambda i: (i, 0))
    return pl.pallas_call(
        body, name=name, grid=(R // tr,),
        in_specs=[spec] * 4, out_specs=[spec] * 3,
        out_shape=[jax.ShapeDtypeStruct((R, C), F32)] * 3,
        compiler_params=_params(1),
    )(w, g, m, v)


def _adamw_small(small_all, w, m, v, name):
    def body(s_ref, w_ref, m_ref, v_ref, g_out, d_out, m_out, v_out):
        g = s_ref[0]
        for d in range(1, N_DEV):
            g = g + s_ref[d]
        g_out[...] = g
        d_out[...], m_out[...], v_out[...] = _adamw_math(w_ref[...], g, m_ref[...], v_ref[...])

    vm = pl.BlockSpec(memory_space=pltpu.VMEM)
    return pl.pallas_call(
        body, name=name, in_specs=[vm] * 4, out_specs=[vm] * 4,
        out_shape=[jax.ShapeDtypeStruct(w.shape, F32)] * 4,
    )(small_all, w, m, v)


def _pack_small(norm_g, pool_scale, norm_f, extra_row):
    return jnp.concatenate([norm_g.reshape(2, D_MODEL), pool_scale.reshape(2, D_MODEL),
                            norm_f.reshape(1, D_MODEL), extra_row,
                            jnp.zeros((2, D_MODEL), F32)], axis=0)


def kernel(x, norm_g, pool_w_in, pool_w, pool_scale, pool_w_out, sb_w_in, sb_w_out, norm_f, loss_target, m_norm_g, m_pool_w_in, m_pool_w, m_pool_scale, m_pool_w_out, m_sb_w_in, m_sb_w_out, m_norm_f, v_norm_g, v_pool_w_in, v_pool_w, v_pool_scale, v_pool_w_out, v_sb_w_in, v_sb_w_out, v_norm_f):
    nb, S, _ = x.shape
    T = nb * S
    xt = x.reshape(T, D_MODEL)
    target = loss_target.reshape(T, D_MODEL)
    cx, cy, cc = _place()

    def shard2d(w):
        return w.reshape(-1, w.shape[-1])

    names = ("pool_w_in", "pool_w", "pool_w_out", "sb_w_in", "sb_w_out")
    w_shards = [shard2d(w) for w in (pool_w_in, pool_w, pool_w_out, sb_w_in, sb_w_out)]
    m_shards = [shard2d(w) for w in (m_pool_w_in, m_pool_w, m_pool_w_out, m_sb_w_in, m_sb_w_out)]
    v_shards = [shard2d(w) for w in (v_pool_w_in, v_pool_w, v_pool_w_out, v_sb_w_in, v_sb_w_out)]

    chip = (2 * cx + cy).reshape(1).astype(jnp.int32)
    c_arr = cc.reshape(1).astype(jnp.int32)
    place = jnp.stack([2 * cx + cy, cc]).astype(jnp.int32)
    slots = [_cast_to_slot(w, chip, "cast_" + nm) for w, nm in zip(w_shards, names)]
    g0, g1, gf = norm_g[0:1], norm_g[1:2], norm_f.reshape(1, D_MODEL)

    w_pin, = _allgather_weights(slots[:1], "allgather_pool_in_weights")
    mix_send, mix_recv, mix_slots, token = _exchange_start(slots[1:3], w_pin, _weight_plan, 6,
                                                           "pool_weights_start")
    sb_send, sb_recv, sb_slots, token = _exchange_start(slots[3:], token, _weight_plan, 6,
                                                        "sb_weights_start")

    proj0, u0 = _rms_matmul(xt, g0 + token[0:1, 0:1], w_pin, [(2 * D_INNER, BF16)], "pool_in_proj",
                            tn=w_pin.shape[2])
    mix_slots = _exchange_wait(mix_slots, mix_send, mix_recv, proj0, _weight_plan, "pool_weights_wait")
    w_g, w_pout = _allgather_weights(mix_slots, "pool_weights_forward", landed=True)
    w_pout = w_pout.reshape(D_INNER, D_MODEL)
    y0, pooled, mixed = _pool_fwd(proj0, w_g, pool_scale, S, "pool_mix")
    sb_slots = _exchange_wait(sb_slots, sb_send, sb_recv, y0, _weight_plan, "sb_weights_wait")
    w_sin, w_sout = _allgather_weights(sb_slots, "sb_weights_forward", landed=True)
    w_sout = w_sout.reshape(D_INNER, D_MODEL)
    h1 = _matmul_residual(y0, w_pout, xt, "pool_out_proj")
    n1 = 2 * QK_WIDTH + 2 * D_INNER
    qkvz, u1 = _rms_matmul(h1, g1, w_sin, [(n1, BF16)], "sb_in_proj", tn=w_sin.shape[2])
    o, y1, ltot = _attn_fwd(qkvz, S, "sb_attention")
    dh2, d_norm_f, loss_row = _out_proj_loss_head(y1, w_sout, h1, gf, target, "sb_out_proj_loss_head")

    def reduce_start(partials, tag, after=None, behind=None):
        n, done = len(partials), None
        after = c_arr if after is None else after
        if behind is None:
            from_sibling = list(_sibling_exchange(partials, None, "grad_sibling_exchange_" + tag))
        else:
            lands = [lax.empty((N_CHIPS, p.shape[1] // 2, p.shape[2]), F32) for p in partials]
            send, recv, bufs, token = _exchange_start(partials + lands, after, _sibling_plan, n,
                                                      "grad_sibling_start_" + tag)
            done = behind(token[0:1, 0:1])
            bufs = _exchange_wait(bufs, send, recv, done[0], _sibling_plan, "grad_sibling_wait_" + tag)
            partials, from_sibling, after = bufs[:n], bufs[n:], c_arr
        sums = [_chip_sum(p, s, c_arr, "grad_chip_sum_%s_%d" % (tag, i))
                for i, (p, s) in enumerate(zip(partials, from_sibling))]
        lands = [lax.empty((3,) + s.shape[1:], BF16) for s in sums]
        send, recv, bufs, token = _exchange_start(sums + lands, after, _chip_sum_plan, 3 * n,
                                                  "grad_chip_exchange_start_" + tag)
        return (partials, from_sibling, send, recv, bufs), token[0:1, 0:1], done

    def reduce_finish(started, after, tag):
        partials, from_sibling, send, recv, bufs = started
        received = _exchange_wait(bufs, send, recv, after, _chip_sum_plan,
                                  "grad_chip_exchange_wait_" + tag)[len(partials):]
        return [_reduce_half(p, s, r, place, "grad_reduce_%s_%d" % (tag, i))
                for i, (p, s, r) in enumerate(zip(partials, from_sibling, received))]

    shard = lambda i, j, t: (j, 0, 0)
    gw_sout = _matmul_tn(y1, dh2, D_INNER, D_MODEL, (D_INNER, D_MODEL), (1024, 1024),
                         lambda i, j, t: (i, j), "grad_sb_w_out", bm=1024, bn=1024)
    sout_started, token, (do, dproj1) = reduce_start(
        [gw_sout.reshape(N_CHIPS, -1, D_MODEL)], "sb_out",
        behind=lambda tok: _attn_gate_bwd(dh2, w_sout, qkvz, o, tok, "sb_gate_bwd"))
    dproj1 = _attn_bwd(qkvz, do, ltot, dproj1, S, "sb_attention_bwd")
    gw_sin = _matmul_tn(u1, dproj1, D_MODEL, n1, (N_CHIPS, D_MODEL, n1 // 4), (1, D_MODEL, n1 // 4),
                        shard, "grad_sb_w_in", bm=D_MODEL, bn=n1 // 4)
    sin_started, token, (dh1, d_g1) = reduce_start(
        [gw_sin], "sb_in", after=token,
        behind=lambda tok: _matmul_nt_rms_bwd(dproj1, w_sin, h1, g1 + tok, dh2, "sb_in_bwd",
                                              tk=w_sin.shape[2]))
    gw_pout = _matmul_tn(y0, dh1, D_INNER, D_MODEL, (D_INNER, D_MODEL), (1024, 1024),
                         lambda i, j, t: (i, j), "grad_pool_w_out", bm=1024, bn=1024)
    dmixed, dproj0, d_scale = _pool_gate_bwd(dh1, w_pout, proj0, mixed, pool_scale + token,
                                             "pool_gate_bwd")
    gw_g = _matmul_tn(pooled, dmixed, D_INNER, D_INNER, (N_CHIPS, GROUP_DIM, GROUP_DIM),
                      (N_CHIPS, GROUP_DIM // N_CHIPS, GROUP_DIM), lambda i, j, t: (0, i, 0),
                      "grad_pool_w", bm=GROUP_DIM, bn=GROUP_DIM, diagonal_blocks=True)
    mix_started, token, (dproj0,) = reduce_start(
        [gw_g, gw_pout.reshape(N_CHIPS, -1, D_MODEL)], "pool_mix",
        behind=lambda tok: (_pool_bwd(dmixed, w_g, dproj0, tok, S, "pool_bwd"),))
    n0 = 2 * D_INNER
    gw_pin = _matmul_tn(u0, dproj0, D_MODEL, n0, (N_CHIPS, D_MODEL, n0 // 4), (1, D_MODEL, n0 // 4),
                        shard, "grad_pool_w_in", bm=D_MODEL, bn=n0 // 4)
    pin_started, token, _ = reduce_start([gw_pin], "pool_in", after=token)
    dx, d_g0 = _matmul_nt_rms_bwd(dproj0, w_pin, xt, g0 + token, dh1, "pool_in_bwd", tk=w_pin.shape[2])

    small = _pack_small(jnp.concatenate([d_g0, d_g1], axis=0), d_scale, d_norm_f,
                        jnp.broadcast_to(loss_row[:, :1], (1, D_MODEL)))
    small_all, = _sibling_exchange([], small, "small_sums_exchange")
    grads = _join_halves(reduce_finish(pin_started, dx, "pool_in")
                         + reduce_finish(mix_started, dx, "pool_mix")
                         + reduce_finish(sin_started, dx, "sb_in")
                         + reduce_finish(sout_started, dx, "sb_out"), "grad_join_halves")

    deltas, new_m, new_v = [], [], []
    for w, g, m, v, nm in zip(w_shards, grads, m_shards, v_shards, names):
        d, mm, vv = _adamw(w, g, m, v, "adamw_" + nm)
        deltas.append(d)
        new_m.append(mm)
        new_v.append(vv)

    zero_row = jnp.zeros((1, D_MODEL), F32)
    g_small, d_small, m_small, v_small = _adamw_small(
        small_all, _pack_small(norm_g, pool_scale, norm_f, zero_row),
        _pack_small(m_norm_g, m_pool_scale, m_norm_f, zero_row),
        _pack_small(v_norm_g, v_pool_scale, v_norm_f, zero_row + 1.0), "adamw_small")
    loss = g_small[5, 0]

    def unpack_small(a):
        return a[0:2], a[2:4].reshape(1, D_INNER), a[4]

    def assemble(big, small3):
        ng, ps, nf = small3
        return [ng, big[0].reshape(pool_w_in.shape), big[1].reshape(pool_w.shape), ps,
                big[2].reshape(pool_w_out.shape), big[3].reshape(sb_w_in.shape),
                big[4].reshape(sb_w_out.shape), nf]

    return (loss, dx.reshape(x.shape),
            *assemble(grads, unpack_small(g_small)),
            *assemble(deltas, unpack_small(d_small)),
            *assemble(new_m, unpack_small(m_small)),
            *assemble(new_v, unpack_small(v_small)))
```

```python
import jax
import jax.numpy as jnp
from jax import lax
from jax.experimental import pallas as pl
from jax.experimental.pallas import tpu as pltpu

F32 = jnp.float32
BF16 = jnp.bfloat16
MESH = pl.DeviceIdType.MESH

D_MODEL = 1024
D_INNER = 2048
N_GROUPS = 4
GROUP_DIM = 512
HEAD_PAIR_QK = 128
HEAD_V = 128
QK_WIDTH = 1024
RMS_EPS = 1e-6
HALO = 16
N_CHIPS = 4
N_DEV = 8

ADAM_LR = 0.001
ADAM_B1 = 0.9
ADAM_B2 = 0.999
ADAM_EPS = 1e-08
ADAM_WD = 0.01
ADAM_STEP = 10

VMEM_LIMIT = 56 * 1024 * 1024

HBM_SPEC = pl.BlockSpec(memory_space=pltpu.HBM)


def _params(n_axes):
    return pltpu.CompilerParams(dimension_semantics=("arbitrary",) * n_axes,
                                vmem_limit_bytes=VMEM_LIMIT)


def _dot(a, b):
    return jnp.dot(a, b, preferred_element_type=F32)


def _dot_nt(a, b):
    return lax.dot_general(a, b, (((1,), (1,)), ((), ())), preferred_element_type=F32)


def _dot_tn(a, b):
    return lax.dot_general(a, b, (((0,), (0,)), ((), ())), preferred_element_type=F32)


def _sigmoid(z):
    return 1.0 / (1.0 + jnp.exp(-z))


def _row_blocks(tm, rows=256):
    return [slice(r, r + rows) for r in range(0, tm, rows)]


def _rms_matmul(h, g_row, w4, outs, name, tm=1024, tn=512):
    T = h.shape[0]
    per_shard = w4.shape[2] // tn
    starts = [0]
    for width, _ in outs:
        starts.append(starts[-1] + width // tn)
    n_out = len(outs)

    def body(h_ref, g_ref, w_ref, *rest):
        o_refs, u_out, u_s = rest[:n_out], rest[n_out], rest[n_out + 1]
        n = pl.program_id(1)

        @pl.when(n == 0)
        def _():
            x = h_ref[...]
            inv = lax.rsqrt(jnp.mean(x * x, axis=-1, keepdims=True) + RMS_EPS)
            u = (x * inv * g_ref[...]).astype(BF16)
            u_s[...] = u
            u_out[...] = u

        res = _dot(u_s[...], w_ref[0])
        for k in range(n_out):
            @pl.when((n >= starts[k]) & (n < starts[k + 1]))
            def _():
                o_refs[k][...] = res.astype(o_refs[k].dtype)

    def out_map(k):
        return lambda m, n: (m, jnp.clip(n - starts[k], 0, starts[k + 1] - starts[k] - 1))

    return pl.pallas_call(
        body, name=name, grid=(T // tm, starts[-1]),
        in_specs=[pl.BlockSpec((tm, D_MODEL), lambda m, n: (m, 0)),
                  pl.BlockSpec((1, D_MODEL), lambda m, n: (0, 0)),
                  pl.BlockSpec((1, D_MODEL, tn), lambda m, n: (n // per_shard, 0, n % per_shard))],
        out_specs=[pl.BlockSpec((tm, tn), out_map(k)) for k in range(n_out)]
        + [pl.BlockSpec((tm, D_MODEL), lambda m, n: (m, 0))],
        out_shape=[jax.ShapeDtypeStruct((T, width), dt) for width, dt in outs]
        + [jax.ShapeDtypeStruct((T, D_MODEL), BF16)],
        scratch_shapes=[pltpu.VMEM((tm, D_MODEL), BF16)],
        compiler_params=_params(2),
    )(h, g_row, w4)


def _matmul_residual(a, w, res, name, tm=1024, tn=1024):
    T, K = a.shape
    N = w.shape[1]

    def body(a_ref, w_ref, r_ref, o_ref):
        o_ref[...] = r_ref[...] + _dot(a_ref[...], w_ref[...])

    return pl.pallas_call(
        body, name=name, grid=(T // tm, N // tn),
        in_specs=[pl.BlockSpec((tm, K), lambda m, n: (m, 0)),
                  pl.BlockSpec((K, tn), lambda m, n: (0, n)),
                  pl.BlockSpec((tm, tn), lambda m, n: (m, n))],
        out_specs=pl.BlockSpec((tm, tn), lambda m, n: (m, n)),
        out_shape=jax.ShapeDtypeStruct((T, N), F32),
        compiler_params=_params(2),
    )(a, w, res)


def _matmul_tn(a, b, a_cols, b_cols, out_shape, out_block, out_map, name, bm, bn, tk=2048,
               diagonal_blocks=False):
    T = a.shape[0]
    tk = min(tk, T)

    def body(a_ref, b_ref, o_ref):
        @pl.when(pl.program_id(2) == 0)
        def _():
            o_ref[...] = jnp.zeros_like(o_ref)

        part = _dot_tn(a_ref[...].astype(BF16), b_ref[...].astype(BF16))
        o_ref[...] += part.reshape(o_ref.shape)

    b_map = (lambda i, j, t: (t, i)) if diagonal_blocks else (lambda i, j, t: (t, j))
    return pl.pallas_call(
        body, name=name, grid=(a_cols // bm, 1 if diagonal_blocks else b_cols // bn, T // tk),
        in_specs=[pl.BlockSpec((tk, bm), lambda i, j, t: (t, i)),
                  pl.BlockSpec((tk, bn), b_map)],
        out_specs=pl.BlockSpec(out_block, out_map),
        out_shape=jax.ShapeDtypeStruct(out_shape, F32),
        compiler_params=_params(3),
    )(a, b)


def _matmul_nt_rms_bwd(dproj, w4, h, g_row, dres, name, with_bf16, tm=1024, tk=512):
    T, cols = dproj.shape
    per_shard = w4.shape[2] // tk
    nk = cols // tk

    def body(dp_ref, w_ref, h_ref, g_ref, r_ref, dx_ref, dg_ref, *rest):
        acc = rest[-1]
        m, k = pl.program_id(0), pl.program_id(1)

        @pl.when(k == 0)
        def _():
            acc[...] = jnp.zeros_like(acc)

        @pl.when((k == 0) & (m == 0))
        def _():
            dg_ref[...] = jnp.zeros_like(dg_ref)

        acc[...] += _dot_nt(dp_ref[...], w_ref[0])

        @pl.when(k == nk - 1)
        def _():
            du = acc[...]
            x = h_ref[...]
            inv = lax.rsqrt(jnp.mean(x * x, axis=-1, keepdims=True) + RMS_EPS)
            xhat = x * inv
            dg_ref[...] += jnp.sum(du * xhat, axis=0, keepdims=True)
            dxh = du * g_ref[...]
            proj = jnp.mean(dxh * xhat, axis=-1, keepdims=True)
            dx = r_ref[...] + inv * (dxh - xhat * proj)
            dx_ref[...] = dx
            if with_bf16:
                rest[0][...] = dx.astype(BF16)

    rows = pl.BlockSpec((tm, D_MODEL), lambda m, k: (m, 0))
    return pl.pallas_call(
        body, name=name, grid=(T // tm, nk),
        in_specs=[pl.BlockSpec((tm, tk), lambda m, k: (m, k)),
                  pl.BlockSpec((1, D_MODEL, tk), lambda m, k: (k // per_shard, 0, k % per_shard)),
                  rows, pl.BlockSpec((1, D_MODEL), lambda m, k: (0, 0)), rows],
        out_specs=[rows, pl.BlockSpec((1, D_MODEL), lambda m, k: (0, 0))] + [rows] * with_bf16,
        out_shape=[jax.ShapeDtypeStruct((T, D_MODEL), F32), jax.ShapeDtypeStruct((1, D_MODEL), F32)]
        + [jax.ShapeDtypeStruct((T, D_MODEL), BF16)] * with_bf16,
        scratch_shapes=[pltpu.VMEM((tm, D_MODEL), F32)],
        compiler_params=_params(2),
    )(dproj, w4, h, g_row, dres)


def _window_of(g):
    return jnp.left_shift(2, g)


def _select_stage(g, stages):
    res = stages[0]
    for i in range(1, len(stages)):
        res = jnp.where(g >= i, stages[i], res)
    return res


def _pool_fwd(proj0, wg4, scale_row, S, name, tm=1024):
    T = proj0.shape[0]
    tm = min(tm, S)
    blocks_per_seq = S // tm
    hb = tm // HALO

    def body(x_ref, halo_ref, z_ref, w_ref, s_ref, y_ref, p_ref, mix_ref):
        m, g = pl.program_id(0), pl.program_id(1)
        first = (m % blocks_per_seq) == 0
        halo = jnp.where(first, 0.0, halo_ref[...].astype(F32))
        x = x_ref[...].astype(F32)
        ext = jnp.concatenate([halo, x], axis=0)
        stages = []
        cur = ext
        for sh in (1, 2, 4, 8):
            cur = cur + pltpu.roll(cur, sh, 0)
            stages.append(cur[HALO:, :])
        win_sum = _select_stage(g, stages)
        pos = (m % blocks_per_seq) * tm + lax.broadcasted_iota(jnp.int32, (tm, 1), 0)
        count = jnp.minimum(pos + 1, _window_of(g)).astype(F32)
        p_ref[...] = (win_sum / count - x).astype(BF16)
        w = w_ref[...].reshape(GROUP_DIM, GROUP_DIM)
        for rows in _row_blocks(tm):
            mixed = _dot(p_ref[rows, :], w)
            z = z_ref[rows, :].astype(F32)
            y_ref[rows, :] = (mixed * s_ref[...] * (z * _sigmoid(z))).astype(BF16)
            mix_ref[rows, :] = mixed.astype(BF16)

    blk = lambda m, g: (m, g)
    return pl.pallas_call(
        body, name=name, grid=(T // tm, N_GROUPS),
        in_specs=[pl.BlockSpec((tm, GROUP_DIM), blk),
                  pl.BlockSpec((HALO, GROUP_DIM), lambda m, g: (jnp.maximum(m * hb - 1, 0), g)),
                  pl.BlockSpec((tm, GROUP_DIM), lambda m, g: (m, N_GROUPS + g)),
                  pl.BlockSpec((N_CHIPS, GROUP_DIM // N_CHIPS, GROUP_DIM), lambda m, g: (0, g, 0)),
                  pl.BlockSpec((1, GROUP_DIM), lambda m, g: (0, g))],
        out_specs=[pl.BlockSpec((tm, GROUP_DIM), blk)] * 3,
        out_shape=[jax.ShapeDtypeStruct((T, D_INNER), BF16)] * 3,
        compiler_params=_params(2),
    )(proj0, proj0, proj0, wg4, scale_row)


def _pool_gate_bwd(dh, w_out, proj0, mixed, scale_row, name, tm=1024, tn=512):
    T = dh.shape[0]
    gate_b0 = D_INNER // tn

    def body(dh_ref, w_ref, z_ref, mix_ref, s_ref, dm_ref, dz_ref, ds_ref):
        m, n = pl.program_id(0), pl.program_id(1)

        @pl.when((m == 0) & (n == 0))
        def _():
            ds_ref[...] = jnp.zeros_like(ds_ref)

        cols = pl.ds(pl.multiple_of(n * tn, tn), tn)
        s = s_ref[...]
        ds = ds_ref[:, cols]
        for rows in _row_blocks(tm):
            dy = _dot_nt(dh_ref[rows, :], w_ref[...])
            z = z_ref[rows, :].astype(F32)
            sig = _sigmoid(z)
            silu = z * sig
            mixed = mix_ref[rows, :].astype(F32)
            dm_ref[rows, :] = (dy * s * silu).astype(BF16)
            dz_ref[rows, :] = (dy * mixed * s * (sig * (1.0 + z * (1.0 - sig)))).astype(BF16)
            ds = ds + jnp.sum(dy * mixed * silu, axis=0, keepdims=True)
        ds_ref[:, cols] = ds

    return pl.pallas_call(
        body, name=name, grid=(T // tm, D_INNER // tn),
        in_specs=[pl.BlockSpec((tm, D_MODEL), lambda m, n: (m, 0)),
                  pl.BlockSpec((tn, D_MODEL), lambda m, n: (n, 0)),
                  pl.BlockSpec((tm, tn), lambda m, n: (m, gate_b0 + n)),
                  pl.BlockSpec((tm, tn), lambda m, n: (m, n)),
                  pl.BlockSpec((1, tn), lambda m, n: (0, n))],
        out_specs=[pl.BlockSpec((tm, tn), lambda m, n: (m, n)),
                   pl.BlockSpec((tm, tn), lambda m, n: (m, gate_b0 + n)),
                   pl.BlockSpec((1, D_INNER), lambda m, n: (0, 0))],
        out_shape=[jax.ShapeDtypeStruct((T, D_INNER), BF16),
                   jax.ShapeDtypeStruct((T, 2 * D_INNER), BF16),
                   jax.ShapeDtypeStruct((1, D_INNER), F32)],
        compiler_params=_params(2),
    )(dh, w_out, proj0, mixed, scale_row)


def _pool_bwd(dmixed, wg4, dproj0, after, S, name, tm=1024):
    T = dmixed.shape[0]
    tm = min(tm, S)
    blocks_per_seq = S // tm
    hb = tm // HALO
    n_halo_blocks = T // HALO

    def body(dm_ref, halo_ref, w_ref, _, __, o_ref):
        m, g = pl.program_id(0), pl.program_id(1)
        ext = jnp.concatenate([dm_ref[...], halo_ref[...]], axis=0)
        dp = _dot_nt(ext, w_ref[...].reshape(GROUP_DIM, GROUP_DIM))
        pos = (m % blocks_per_seq) * tm + lax.broadcasted_iota(jnp.int32, (tm + HALO, 1), 0)
        count = jnp.minimum(pos + 1, _window_of(g)).astype(F32)
        c = jnp.where(pos < S, dp / count, 0.0)
        n = tm + HALO
        stages = []
        cur = c
        for sh in (1, 2, 4, 8):
            cur = cur + pltpu.roll(cur, n - sh, 0)
            stages.append(cur[:tm, :])
        o_ref[...] = (_select_stage(g, stages) - dp[:tm, :]).astype(BF16)

    blk = lambda m, g: (m, g)
    return pl.pallas_call(
        body, name=name, grid=(T // tm, N_GROUPS),
        in_specs=[pl.BlockSpec((tm, GROUP_DIM), blk),
                  pl.BlockSpec((HALO, GROUP_DIM),
                               lambda m, g: (jnp.minimum((m + 1) * hb, n_halo_blocks - 1), g)),
                  pl.BlockSpec((N_CHIPS, GROUP_DIM // N_CHIPS, GROUP_DIM), lambda m, g: (0, g, 0)),
                  HBM_SPEC, ANY_SPEC],
        out_specs=pl.BlockSpec((tm, GROUP_DIM), blk),
        out_shape=jax.ShapeDtypeStruct(dproj0.shape, dproj0.dtype),
        input_output_aliases={3: 0},
        compiler_params=_params(2),
    )(dmixed, dmixed, wg4, dproj0, after)


TQ = 256


def _split_dot(x, m):
    hi = x.astype(BF16)
    lo = (x - hi.astype(F32)).astype(BF16)
    return _dot(hi, m) + _dot(lo, m)


NEG_LOG2E = -1.4426950408889634


def _log_terms(z):
    soft = jnp.log(1.0 + jnp.exp2(jnp.abs(z) * NEG_LOG2E))
    log_beta = jnp.minimum(z, 0.0) - soft
    return log_beta, log_beta - z


N_HEADS = 16
FWD_HEADS = BWD_HEADS = 4


def _masked_heads(x, heads):
    lane = lax.broadcasted_iota(jnp.int32, (1, HEAD_PAIR_QK), 1)
    out = []
    for hh in range(heads):
        slab = x[:, (hh // 2) * HEAD_PAIR_QK:(hh // 2 + 1) * HEAD_PAIR_QK]
        out.append(jnp.where((lane // 64) == hh % 2, slab, jnp.zeros_like(slab)))
    return out


def _attn_fwd(qkvz, S, name):
    T = qkvz.shape[0]
    nq = S // TQ
    HEADS, QK_W, V_W = FWD_HEADS, FWD_HEADS * 64, FWD_HEADS * HEAD_V
    k_b0 = QK_WIDTH // QK_W
    v_b0 = 2 * QK_WIDTH // V_W
    z_b0 = (2 * QK_WIDTH + D_INNER) // V_W
    hs = range(HEADS)

    def body(q_ref, k_ref, v_ref, z_ref, o_ref, y_ref, lt_ref):
        row = lax.broadcasted_iota(jnp.int32, (TQ, TQ), 0)
        col = lax.broadcasted_iota(jnp.int32, (TQ, TQ), 1)
        causal = col < row
        later_in_block = (row > col).astype(BF16)
        lax.fori_loop(0, nq, lambda qi, _: q_block(qi, causal, later_in_block,
                                                   q_ref, k_ref, v_ref, z_ref, o_ref, y_ref, lt_ref), 0)

    def q_block(qi, causal, later_in_block, q_ref, k_ref, v_ref, z_ref, o_ref, y_ref, lt_ref):
        rows = pl.ds(pl.multiple_of(qi * TQ, TQ), TQ)
        qms = [qm * 0.125 for qm in _masked_heads(q_ref[rows, :], HEADS)]

        def step(j, carry, diagonal):
            koff = pl.multiple_of(j * TQ, TQ)
            kbs = [k_ref[pl.ds(koff, TQ), p * HEAD_PAIR_QK:(p + 1) * HEAD_PAIR_QK]
                   for p in range(HEADS // 2)]
            run, acc = [carry[2 * hh] for hh in hs], [carry[2 * hh + 1] for hh in hs]
            z = [_dot_nt(qms[hh], kbs[hh // 2]) for hh in hs]
            terms = [_log_terms(z[hh]) for hh in hs]
            log_om = [jnp.where(causal, t[1], 0.0) if diagonal else t[1] for t in terms]
            later = [_split_dot(log_om[hh], later_in_block) for hh in hs]
            a = [jnp.exp(terms[hh][0] + (run[hh] + later[hh])) for hh in hs]
            if diagonal:
                a = [jnp.where(causal, a[hh], 0.0) for hh in hs]
            out = []
            for hh in hs:
                vb = v_ref[pl.ds(koff, TQ), hh * HEAD_V:(hh + 1) * HEAD_V]
                out += [run[hh] + jnp.sum(log_om[hh], axis=1, keepdims=True),
                        acc[hh] + _dot(a[hh].astype(BF16), vb)]
            return tuple(out)

        zero = (jnp.zeros((TQ, 1), F32), jnp.zeros((TQ, HEAD_V), F32))
        carry = step(qi, zero * HEADS, True)
        carry = lax.fori_loop(0, qi, lambda i, c: step(qi - 1 - i, c, False), carry)
        for hh in hs:
            sl = slice(hh * HEAD_V, (hh + 1) * HEAD_V)
            acc = carry[2 * hh + 1]
            z = z_ref[rows, sl].astype(F32)
            o_ref[rows, sl] = acc.astype(BF16)
            y_ref[rows, sl] = (acc * (z * _sigmoid(z))).astype(BF16)
            lt_ref[rows, hh:hh + 1] = carry[2 * hh]
        return 0

    blk = lambda b, p: (b, p)
    return pl.pallas_call(
        body, name=name, grid=(T // S, N_HEADS // HEADS),
        in_specs=[pl.BlockSpec((S, QK_W), blk),
                  pl.BlockSpec((S, QK_W), lambda b, p: (b, k_b0 + p)),
                  pl.BlockSpec((S, V_W), lambda b, p: (b, v_b0 + p)),
                  pl.BlockSpec((S, V_W), lambda b, p: (b, z_b0 + p))],
        out_specs=[pl.BlockSpec((S, V_W), blk),
                   pl.BlockSpec((S, V_W), blk),
                   pl.BlockSpec((None, S, HEADS), lambda b, p: (p, b, 0))],
        out_shape=[jax.ShapeDtypeStruct((T, D_INNER), BF16),
                   jax.ShapeDtypeStruct((T, D_INNER), BF16),
                   jax.ShapeDtypeStruct((N_HEADS // HEADS, T, HEADS), F32)],
        compiler_params=_params(2),
    )(qkvz, qkvz, qkvz, qkvz)


def _attn_gate_bwd(dh, w_out, qkvz, o, after, name, tm=1024, tn=512):
    T = dh.shape[0]
    gate_b0 = (2 * QK_WIDTH + D_INNER) // tn

    def body(dh_ref, w_ref, z_ref, o_ref, _, do_ref, dz_ref):
        for rows in _row_blocks(tm):
            dy = _dot_nt(dh_ref[rows, :], w_ref[...])
            z = z_ref[rows, :].astype(F32)
            sig = _sigmoid(z)
            do_ref[rows, :] = (dy * (z * sig)).astype(BF16)
            dz_ref[rows, :] = (dy * o_ref[rows, :].astype(F32)
                               * (sig * (1.0 + z * (1.0 - sig)))).astype(BF16)

    return pl.pallas_call(
        body, name=name, grid=(T // tm, D_INNER // tn),
        in_specs=[pl.BlockSpec((tm, D_MODEL), lambda m, n: (m, 0)),
                  pl.BlockSpec((tn, D_MODEL), lambda m, n: (n, 0)),
                  pl.BlockSpec((tm, tn), lambda m, n: (m, gate_b0 + n)),
                  pl.BlockSpec((tm, tn), lambda m, n: (m, n)),
                  ANY_SPEC],
        out_specs=[pl.BlockSpec((tm, tn), lambda m, n: (m, n)),
                   pl.BlockSpec((tm, tn), lambda m, n: (m, gate_b0 + n))],
        out_shape=[jax.ShapeDtypeStruct((T, D_INNER), BF16),
                   jax.ShapeDtypeStruct((T, 2 * QK_WIDTH + 2 * D_INNER), BF16)],
        compiler_params=_params(2),
    )(dh, w_out, qkvz, o, after)


def _attn_bwd(qkv, do, ltot, dproj1, S, name):
    T = qkv.shape[0]
    nq = S // TQ
    HEADS, QK_W, V_W = BWD_HEADS, BWD_HEADS * 64, BWD_HEADS * HEAD_V
    k_b0 = QK_WIDTH // QK_W
    v_b0 = 2 * QK_WIDTH // V_W
    hs = range(HEADS)
    pairs = range(HEADS // 2)
    n_groups = N_HEADS // HEADS
    n_steps = (T // S) * n_groups

    def body(q_ref, k_ref, v_ref, do_ref, lt_ref, _, out_ref, dq_s, dk_s, dv_s, dkb_s, dvb_s, sems):
        b, p = pl.program_id(0), pl.program_id(1)
        row = lax.broadcasted_iota(jnp.int32, (TQ, TQ), 0)
        col = lax.broadcasted_iota(jnp.int32, (TQ, TQ), 1)
        causal = col < row
        upto = (row <= col).astype(BF16)
        before = (row < col).astype(BF16)
        dk_s[...] = jnp.zeros_like(dk_s)
        dv_s[...] = jnp.zeros_like(dv_s)
        rows = pl.ds(pl.multiple_of(b * S, TQ), S)
        copies = [
            pltpu.make_async_copy(
                dq_s, out_ref.at[rows, pl.ds(pl.multiple_of(p * QK_W, 128), QK_W)], sems.at[0]),
            pltpu.make_async_copy(
                dkb_s, out_ref.at[rows, pl.ds(pl.multiple_of(QK_WIDTH + p * QK_W, 128), QK_W)],
                sems.at[1]),
            pltpu.make_async_copy(
                dvb_s, out_ref.at[rows, pl.ds(pl.multiple_of(2 * QK_WIDTH + p * V_W, 128), V_W)],
                sems.at[2]),
        ]
        step = b * n_groups + p

        @pl.when(step > 0)
        def _():
            for cp in copies:
                cp.wait()

        def q_block(qi, _):
            qoff = pl.multiple_of(qi * TQ, TQ)
            qms = [qm * 0.125 for qm in _masked_heads(q_ref[pl.ds(qoff, TQ), :], HEADS)]
            vsl = [slice(hh * HEAD_V, (hh + 1) * HEAD_V) for hh in hs]
            psl = [slice(pp * HEAD_PAIR_QK, (pp + 1) * HEAD_PAIR_QK) for pp in pairs]
            do_h = [do_ref[pl.ds(qoff, TQ), sl] for sl in vsl]
            total = [lt_ref[pl.ds(qoff, TQ), hh:hh + 1] for hh in hs]

            def k_block(j, carry, diagonal):
                koff = pl.multiple_of(j * TQ, TQ)
                kms = _masked_heads(k_ref[pl.ds(koff, TQ), :], HEADS)
                g_before = [carry[2 * hh] for hh in hs]
                lom_before = [carry[2 * hh + 1] for hh in hs]
                z = [_dot_nt(qms[hh], kms[hh]) for hh in hs]
                da = [_dot_nt(do_h[hh], v_ref[pl.ds(koff, TQ), vsl[hh]]) for hh in hs]
                terms = [_log_terms(z[hh]) for hh in hs]
                log_om = [jnp.where(causal, t[1], 0.0) if diagonal else t[1] for t in terms]
                prefix = [_split_dot(log_om[hh], upto) for hh in hs]
                a = [jnp.exp(terms[hh][0] + ((total[hh] - lom_before[hh]) - prefix[hh])) for hh in hs]
                if diagonal:
                    a = [jnp.where(causal, a[hh], 0.0) for hh in hs]
                g = [a[hh] * da[hh] for hh in hs]
                g_prefix = [_dot(g[hh].astype(BF16), before) for hh in hs]
                out, dzs = [], []
                for hh in hs:
                    beta = jnp.exp(terms[hh][0])
                    g_excl = (g_before[hh] + g_prefix[hh]) * beta
                    if diagonal:
                        g_excl = jnp.where(causal, g_excl, 0.0)
                    dzs.append((g[hh] * (1.0 - beta) - g_excl).astype(BF16))
                    out += [g_before[hh] + jnp.sum(g[hh], axis=1, keepdims=True),
                            lom_before[hh] + jnp.sum(log_om[hh], axis=1, keepdims=True)]
                for hh in hs:
                    dv_s[pl.ds(koff, TQ), vsl[hh]] += _dot_tn(a[hh].astype(BF16), do_h[hh])
                dq = []
                for pp in pairs:
                    pair = slice(2 * pp, 2 * pp + 2)
                    dq.append(carry[2 * HEADS + pp] + _dot(jnp.concatenate(dzs[pair], axis=1),
                                                           jnp.concatenate(kms[pair], axis=0)))
                    dk_s[pl.ds(koff, TQ), psl[pp]] += _dot_tn(jnp.concatenate(dzs[pair], axis=0),
                                                              jnp.concatenate(qms[pair], axis=0))
                return tuple(out) + tuple(dq)

            zero = jnp.zeros((TQ, 1), F32)
            carry = (zero,) * (2 * HEADS) + (jnp.zeros((TQ, HEAD_PAIR_QK), F32),) * (HEADS // 2)
            carry = lax.fori_loop(0, qi, lambda j, c: k_block(j, c, False), carry)
            carry = k_block(qi, carry, True)
            for pp in pairs:
                dq_s[pl.ds(qoff, TQ), psl[pp]] = (carry[2 * HEADS + pp] * 0.125).astype(BF16)
            return 0

        lax.fori_loop(0, nq, q_block, 0)
        dkb_s[...] = dk_s[...].astype(BF16)
        dvb_s[...] = dv_s[...].astype(BF16)
        for cp in copies:
            cp.start()

        @pl.when(step == n_steps - 1)
        def _():
            for cp in copies:
                cp.wait()

    return pl.pallas_call(
        body, name=name, grid=(T // S, N_HEADS // HEADS),
        in_specs=[pl.BlockSpec((S, QK_W), lambda b, p: (b, p)),
                  pl.BlockSpec((S, QK_W), lambda b, p: (b, k_b0 + p)),
                  pl.BlockSpec((S, V_W), lambda b, p: (b, v_b0 + p)),
                  pl.BlockSpec((S, V_W), lambda b, p: (b, p)),
                  pl.BlockSpec((None, S, HEADS), lambda b, p: (p, b, 0)),
                  HBM_SPEC],
        out_specs=HBM_SPEC,
        out_shape=jax.ShapeDtypeStruct(dproj1.shape, dproj1.dtype),
        input_output_aliases={5: 0},
        scratch_shapes=[pltpu.VMEM((S, QK_W), BF16),
                        pltpu.VMEM((S, QK_W), F32),
                        pltpu.VMEM((S, V_W), F32),
                        pltpu.VMEM((S, QK_W), BF16),
                        pltpu.VMEM((S, V_W), BF16),
                        pltpu.SemaphoreType.DMA((3,))],
        compiler_params=_params(2),
    )(qkv, qkv, qkv, do, ltot, dproj1)


def _out_proj_loss_head(a, w, res, g_row, target, name, tm=512):
    T, K = a.shape

    def body(a_ref, w_ref, r_ref, g_ref, t_ref, dh_ref, dg_ref, loss_ref, dhb_ref):
        @pl.when(pl.program_id(0) == 0)
        def _():
            dg_ref[...] = jnp.zeros_like(dg_ref)
            loss_ref[...] = jnp.zeros_like(loss_ref)

        gain = g_ref[...]
        dg, loss = dg_ref[...], loss_ref[...]
        for rows in _row_blocks(tm):
            x = r_ref[rows, :] + _dot(a_ref[rows, :], w_ref[...])
            inv = lax.rsqrt(jnp.mean(x * x, axis=-1, keepdims=True) + RMS_EPS)
            xhat = x * inv
            err = xhat * gain - t_ref[rows, :]
            per_token = jnp.mean(err * err, axis=-1, keepdims=True)
            loss = loss + 0.5 * jnp.sum(per_token, axis=0, keepdims=True)
            dy = err * (1.0 / D_MODEL)
            dg = dg + jnp.sum(dy * xhat, axis=0, keepdims=True)
            dxh = dy * gain
            proj = jnp.mean(dxh * xhat, axis=-1, keepdims=True)
            dh = inv * (dxh - xhat * proj)
            dh_ref[rows, :] = dh
            dhb_ref[rows, :] = dh.astype(BF16)
        dg_ref[...] = dg
        loss_ref[...] = loss

    return pl.pallas_call(
        body, name=name, grid=(T // tm,),
        in_specs=[pl.BlockSpec((tm, K), lambda m: (m, 0)),
                  pl.BlockSpec((K, D_MODEL), lambda m: (0, 0)),
                  pl.BlockSpec((tm, D_MODEL), lambda m: (m, 0)),
                  pl.BlockSpec((1, D_MODEL), lambda m: (0, 0)),
                  pl.BlockSpec((tm, D_MODEL), lambda m: (m, 0))],
        out_specs=[pl.BlockSpec((tm, D_MODEL), lambda m: (m, 0)),
                   pl.BlockSpec((1, D_MODEL), lambda m: (0, 0)),
                   pl.BlockSpec((1, 128), lambda m: (0, 0)),
                   pl.BlockSpec((tm, D_MODEL), lambda m: (m, 0))],
        out_shape=[jax.ShapeDtypeStruct((T, D_MODEL), F32),
                   jax.ShapeDtypeStruct((1, D_MODEL), F32),
                   jax.ShapeDtypeStruct((1, 128), F32),
                   jax.ShapeDtypeStruct((T, D_MODEL), BF16)],
        compiler_params=_params(1),
    )(a, w, res, g_row, target)


def _place():
    return lax.axis_index("x"), lax.axis_index("y"), lax.axis_index("c")


def _other_chips(x, y):
    return [(1 - x, y), (x, 1 - y), (1 - x, 1 - y)]


def _half(ref, c):
    hr = ref.shape[-2] // 2
    return pl.ds(pl.multiple_of(c * hr, 8), hr)


def _cast_to_slot(shard, chip, name, tr=256):
    R, C = shard.shape

    def body(chip_ref, w_ref, o_ref):
        o_ref[0] = w_ref[...].astype(BF16)

    return pl.pallas_call(
        body, name=name,
        grid_spec=pltpu.PrefetchScalarGridSpec(
            num_scalar_prefetch=1, grid=(R // tr,),
            in_specs=[pl.BlockSpec((tr, C), lambda i, chip_ref: (i, 0))],
            out_specs=pl.BlockSpec((1, tr, C), lambda i, chip_ref: (chip_ref[0], i, 0))),
        out_shape=jax.ShapeDtypeStruct((N_CHIPS, R, C), BF16),
        compiler_params=_params(1),
    )(chip, shard)


def _weight_plan(bufs):
    x, y, c = _place()
    plan = []
    for buf in bufs:
        mine = buf.at[2 * x + y, _half(buf, c)]
        for ox, oy in _other_chips(x, y):
            plan.append((mine, mine, (ox, oy, c), buf.at[2 * ox + oy, _half(buf, c)]))
    return plan


def _chip_sum_plan(bufs):
    x, y, c = _place()
    n = len(bufs) // 2
    plan = []
    for sums, land in zip(bufs[:n], bufs[n:]):
        for k, (ox, oy) in enumerate(_other_chips(x, y)):
            plan.append((sums.at[2 * ox + oy], land.at[k], (ox, oy, c), land.at[k]))
    return plan


def _sibling_plan(bufs):
    x, y, c = _place()
    n = len(bufs) // 2
    return [(p.at[:, _half(p, 1 - c)], land, (x, y, 1 - c), land)
            for p, land in zip(bufs[:n], bufs[n:])]


SEM_SPEC = pl.BlockSpec(memory_space=pltpu.SEMAPHORE)
ANY_SPEC = pl.BlockSpec(memory_space=pl.ANY)
DATAFLOW = pltpu.SideEffectType.DATAFLOW_SIDE_EFFECTING


def _in_hbm(a):
    return pltpu.with_memory_space_constraint(a, pltpu.HBM)


def _exchange_start(bufs, after, plan, n_copies, name):
    nb = len(bufs)

    def body(*refs):
        send_sems, recv_sems = refs[nb + 1], refs[nb + 2]
        for i, (src, dst, dev, _) in enumerate(plan(refs[:nb])):
            pltpu.make_async_remote_copy(
                src_ref=src, dst_ref=dst, send_sem=send_sems.at[i], recv_sem=recv_sems.at[i],
                device_id=dev, device_id_type=MESH).start()
        token = refs[-1]
        token[...] = jnp.zeros_like(token)

    res = pl.pallas_call(
        body, name=name,
        in_specs=[HBM_SPEC] * nb + [ANY_SPEC],
        out_specs=[SEM_SPEC, SEM_SPEC] + [HBM_SPEC] * nb + [pl.BlockSpec(memory_space=pltpu.VMEM)],
        out_shape=[pltpu.SemaphoreType.DMA((n_copies,)), pltpu.SemaphoreType.DMA((n_copies,))]
        + [pltpu.HBM(b.shape, b.dtype) for b in bufs] + [jax.ShapeDtypeStruct((8, 128), F32)],
        input_output_aliases={i: 2 + i for i in range(nb)},
        compiler_params=pltpu.CompilerParams(has_side_effects=DATAFLOW),
    )(*[_in_hbm(b) for b in bufs], after)
    return res[0], res[1], list(res[2:2 + nb]), res[-1]


def _exchange_wait(bufs, send_sems, recv_sems, after, plan, name):
    nb = len(bufs)

    def body(*refs):
        sends, recvs = refs[nb], refs[nb + 1]
        for i, (src, dst, dev, landing) in enumerate(plan(refs[:nb])):
            pltpu.make_async_remote_copy(
                src_ref=src, dst_ref=landing, send_sem=sends.at[i], recv_sem=recvs.at[i],
                device_id=dev, device_id_type=MESH).wait()

    res = pl.pallas_call(
        body, name=name,
        in_specs=[HBM_SPEC] * nb + [SEM_SPEC, SEM_SPEC, ANY_SPEC],
        out_specs=[HBM_SPEC] * nb,
        out_shape=[pltpu.HBM(b.shape, b.dtype) for b in bufs],
        input_output_aliases={i: i for i in range(nb)},
        compiler_params=pltpu.CompilerParams(has_side_effects=DATAFLOW),
    )(*bufs, send_sems, recv_sems, after)
    return list(res)


def _allgather_weights(slots, name, landed=False):
    n = len(slots)

    def body(*refs):
        outs = refs[n:2 * n]
        send_sems, recv_sems, fwd_send, fwd_recv = refs[2 * n:]
        x, y, c = _place()
        chips = _other_chips(x, y)

        def landing(a, chip, half_of):
            return outs[a].at[2 * chip[0] + chip[1], _half(outs[a], half_of)]

        def ici(a, k, chip_from, to):
            return pltpu.make_async_remote_copy(
                src_ref=landing(a, chip_from, c), dst_ref=landing(a, chip_from, c),
                send_sem=send_sems.at[a, k], recv_sem=recv_sems.at[a, k],
                device_id=to, device_id_type=MESH)

        def d2d(a, k, chip_from, half_of):
            return pltpu.make_async_remote_copy(
                src_ref=landing(a, chip_from, half_of), dst_ref=landing(a, chip_from, half_of),
                send_sem=fwd_send.at[a, k], recv_sem=fwd_recv.at[a, k],
                device_id=(x, y, 1 - c), device_id_type=MESH)

        sends = []
        if not landed:
            sends = [ici(a, k, (x, y), (*chips[k], c)) for a in range(n) for k in range(3)]
        for cp in sends:
            cp.start()
        forwards = []
        for a in range(n):
            for k in range(3):
                if not landed:
                    ici(a, k, chips[k], (x, y, c)).wait_recv()
                fw = d2d(a, k, chips[k], c)
                fw.start()
                forwards.append(fw)
        for a in range(n):
            for k in range(3):
                d2d(a, k, chips[k], 1 - c).wait_recv()
        for cp in sends + forwards:
            cp.wait_send()

    return pl.pallas_call(
        body, name=name,
        in_specs=[HBM_SPEC] * n, out_specs=[HBM_SPEC] * n,
        out_shape=[jax.ShapeDtypeStruct(s.shape, s.dtype) for s in slots],
        input_output_aliases={a: a for a in range(n)},
        scratch_shapes=[pltpu.SemaphoreType.DMA((n, 3)), pltpu.SemaphoreType.DMA((n, 3)),
                        pltpu.SemaphoreType.DMA((n, 3)), pltpu.SemaphoreType.DMA((n, 3))],
    )(*slots)


def _sibling_exchange(partials, small, name):
    n = len(partials)
    ns = 0 if small is None else 1

    def body(*refs):
        ins, outs = refs[:n], refs[n + ns:2 * n + ns]
        send_sems, recv_sems = refs[2 * (n + ns):2 * (n + ns) + 2]
        x, y, c = _place()
        me = 4 * x + 2 * y + c
        sends = [pltpu.make_async_remote_copy(
            src_ref=ins[a].at[:, _half(ins[a], 1 - c)], dst_ref=outs[a],
            send_sem=send_sems.at[a], recv_sem=recv_sems.at[a],
            device_id=(x, y, 1 - c), device_id_type=MESH) for a in range(n)]
        if ns:
            small_ref, small_all = refs[n], refs[2 * n + 1]
            s_send, s_recv, loc_sem = refs[2 * (n + ns) + 2:]
            local = pltpu.make_async_copy(small_ref, small_all.at[me], loc_sem)
            local.start()
            for d in range(1, N_DEV):
                px, py, pc = x ^ ((d >> 2) & 1), y ^ ((d >> 1) & 1), c ^ (d & 1)
                sends.append(pltpu.make_async_remote_copy(
                    src_ref=small_ref, dst_ref=small_all.at[me],
                    send_sem=s_send.at[d - 1], recv_sem=s_recv.at[d - 1],
                    device_id=(px, py, pc), device_id_type=MESH))
        for cp in sends:
            cp.start()
        if ns:
            for d in range(1, N_DEV):
                pltpu.make_async_remote_copy(
                    src_ref=small_ref, dst_ref=small_all.at[me ^ d],
                    send_sem=s_send.at[d - 1], recv_sem=s_recv.at[d - 1],
                    device_id=(x, y, c), device_id_type=MESH).wait_recv()
        for cp in sends[:n]:
            cp.wait_recv()
        for cp in sends:
            cp.wait_send()
        if ns:
            local.wait()

    out_shape = [jax.ShapeDtypeStruct((N_CHIPS, p.shape[1] // 2, p.shape[2]), F32) for p in partials]
    scratch = [pltpu.SemaphoreType.DMA((max(n, 1),)), pltpu.SemaphoreType.DMA((max(n, 1),))]
    if ns:
        out_shape.append(jax.ShapeDtypeStruct((N_DEV,) + small.shape, F32))
        scratch += [pltpu.SemaphoreType.DMA((N_DEV - 1,)), pltpu.SemaphoreType.DMA((N_DEV - 1,)),
                    pltpu.SemaphoreType.DMA]
    return pl.pallas_call(
        body, name=name,
        in_specs=[HBM_SPEC] * (n + ns), out_specs=[HBM_SPEC] * (n + ns),
        out_shape=out_shape, scratch_shapes=scratch,
    )(*partials, *([small] if ns else []))


def _chip_sum(partial, from_sibling, c, name, tr=256):
    _, hr, C = from_sibling.shape
    nb = hr // tr

    def body(c_ref, p_ref, s_ref, o_ref):
        o_ref[...] = (p_ref[...] + s_ref[...]).astype(BF16)

    return pl.pallas_call(
        body, name=name,
        grid_spec=pltpu.PrefetchScalarGridSpec(
            num_scalar_prefetch=1, grid=(N_CHIPS, nb),
            in_specs=[pl.BlockSpec((1, tr, C), lambda j, i, c_ref: (j, c_ref[0] * nb + i, 0)),
                      pl.BlockSpec((1, tr, C), lambda j, i, c_ref: (j, i, 0))],
            out_specs=pl.BlockSpec((1, tr, C), lambda j, i, c_ref: (j, i, 0))),
        out_shape=jax.ShapeDtypeStruct(from_sibling.shape, BF16),
        compiler_params=_params(2),
    )(c, partial, from_sibling)


def _reduce_half(partial, from_sibling, received, place, name, tr=256):
    _, hr, C = from_sibling.shape
    nb = hr // tr

    def body(p_ref, mine_ref, sib_ref, r_ref, o_ref):
        acc = mine_ref[0] + sib_ref[0]
        for k in range(3):
            acc = acc + r_ref[k].astype(F32)
        o_ref[...] = acc

    return pl.pallas_call(
        body, name=name,
        grid_spec=pltpu.PrefetchScalarGridSpec(
            num_scalar_prefetch=1, grid=(nb,),
            in_specs=[pl.BlockSpec((1, tr, C), lambda i, p: (p[0], p[1] * nb + i, 0)),
                      pl.BlockSpec((1, tr, C), lambda i, p: (p[0], i, 0)),
                      pl.BlockSpec((3, tr, C), lambda i, p: (0, i, 0))],
            out_specs=pl.BlockSpec((tr, C), lambda i, p: (p[1] * nb + i, 0))),
        out_shape=jax.ShapeDtypeStruct((2 * hr, C), F32),
        compiler_params=_params(1),
    )(place, partial, from_sibling, received)


def _join_halves(fulls, name):
    n = len(fulls)

    def body(*refs):
        outs = refs[n:2 * n]
        send_sems, recv_sems = refs[2 * n:]
        x, y, c = _place()

        def copy(a, half_of, to):
            rows = outs[a].at[_half(outs[a], half_of)]
            return pltpu.make_async_remote_copy(
                src_ref=rows, dst_ref=rows, send_sem=send_sems.at[a], recv_sem=recv_sems.at[a],
                device_id=to, device_id_type=MESH)

        sends = [copy(a, c, (x, y, 1 - c)) for a in range(n)]
        for cp in sends:
            cp.start()
        for a in range(n):
            copy(a, 1 - c, (x, y, c)).wait_recv()
        for cp in sends:
            cp.wait_send()

    return pl.pallas_call(
        body, name=name,
        in_specs=[HBM_SPEC] * n, out_specs=[HBM_SPEC] * n,
        out_shape=[jax.ShapeDtypeStruct(f.shape, F32) for f in fulls],
        input_output_aliases={a: a for a in range(n)},
        scratch_shapes=[pltpu.SemaphoreType.DMA((n,)), pltpu.SemaphoreType.DMA((n,))],
    )(*fulls)


def _adamw_math(w, g, m, v):
    m = ADAM_B1 * m + (1.0 - ADAM_B1) * g
    v = ADAM_B2 * v + (1.0 - ADAM_B2) * (g * g)
    m_hat = m / (1.0 - ADAM_B1 ** ADAM_STEP)
    v_hat = v / (1.0 - ADAM_B2 ** ADAM_STEP)
    delta = -ADAM_LR * (m_hat / (jnp.sqrt(v_hat) + ADAM_EPS) + ADAM_WD * w)
    return delta, m, v


def _adamw(w, g, m, v, name, tr=256):
    R, C = w.shape
    tr = min(tr, R)

    def body(w_ref, g_ref, m_ref, v_ref, d_out, m_out, v_out):
        d_out[...], m_out[...], v_out[...] = _adamw_math(w_ref[...], g_ref[...], m_ref[...], v_ref[...])

    spec = pl.BlockSpec((tr, C), lambda i: (i, 0))
    return pl.pallas_call(
        body, name=name, grid=(R // tr,),
        in_specs=[spec] * 4, out_specs=[spec] * 3,
        out_shape=[jax.ShapeDtypeStruct((R, C), F32)] * 3,
        compiler_params=_params(1),
    )(w, g, m, v)


def _adamw_small(small_all, w, m, v, name):
    def body(s_ref, w_ref, m_ref, v_ref, g_out, d_out, m_out, v_out):
        g = s_ref[0]
        for d in range(1, N_DEV):
            g = g + s_ref[d]
        g_out[...] = g
        d_out[...], m_out[...], v_out[...] = _adamw_math(w_ref[...], g, m_ref[...], v_ref[...])

    vm = pl.BlockSpec(memory_space=pltpu.VMEM)
    return pl.pallas_call(
        body, name=name, in_specs=[vm] * 4, out_specs=[vm] * 4,
        out_shape=[jax.ShapeDtypeStruct(w.shape, F32)] * 4,
    )(small_all, w, m, v)


def _pack_small(norm_g, pool_scale, norm_f, extra_row):
    return jnp.concatenate([norm_g.reshape(2, D_MODEL), pool_scale.reshape(2, D_MODEL),
                            norm_f.reshape(1, D_MODEL), extra_row,
                            jnp.zeros((2, D_MODEL), F32)], axis=0)


def kernel(x, norm_g, pool_w_in, pool_w, pool_scale, pool_w_out, sb_w_in, sb_w_out, norm_f, loss_target, m_norm_g, m_pool_w_in, m_pool_w, m_pool_scale, m_pool_w_out, m_sb_w_in, m_sb_w_out, m_norm_f, v_norm_g, v_pool_w_in, v_pool_w, v_pool_scale, v_pool_w_out, v_sb_w_in, v_sb_w_out, v_norm_f):
    nb, S, _ = x.shape
    T = nb * S
    xt = x.reshape(T, D_MODEL)
    target = loss_target.reshape(T, D_MODEL)
    cx, cy, cc = _place()

    def shard2d(w):
        return w.reshape(-1, w.shape[-1])

    names = ("pool_w_in", "pool_w", "pool_w_out", "sb_w_in", "sb_w_out")
    w_shards = [shard2d(w) for w in (pool_w_in, pool_w, pool_w_out, sb_w_in, sb_w_out)]
    m_shards = [shard2d(w) for w in (m_pool_w_in, m_pool_w, m_pool_w_out, m_sb_w_in, m_sb_w_out)]
    v_shards = [shard2d(w) for w in (v_pool_w_in, v_pool_w, v_pool_w_out, v_sb_w_in, v_sb_w_out)]

    chip = (2 * cx + cy).reshape(1).astype(jnp.int32)
    c_arr = cc.reshape(1).astype(jnp.int32)
    place = jnp.stack([2 * cx + cy, cc]).astype(jnp.int32)
    slots = [_cast_to_slot(w, chip, "cast_" + nm) for w, nm in zip(w_shards, names)]
    g0, g1, gf = norm_g[0:1], norm_g[1:2], norm_f.reshape(1, D_MODEL)

    w_pin, = _allgather_weights(slots[:1], "allgather_pool_in_weights")
    mix_send, mix_recv, mix_slots, token = _exchange_start(slots[1:3], w_pin, _weight_plan, 6,
                                                           "pool_weights_start")
    sb_send, sb_recv, sb_slots, token = _exchange_start(slots[3:], token, _weight_plan, 6,
                                                        "sb_weights_start")

    proj0, u0 = _rms_matmul(xt, g0 + token[0:1, 0:1], w_pin, [(2 * D_INNER, BF16)], "pool_in_proj",
                            tn=w_pin.shape[2])
    mix_slots = _exchange_wait(mix_slots, mix_send, mix_recv, proj0, _weight_plan, "pool_weights_wait")
    w_g, w_pout = _allgather_weights(mix_slots, "pool_weights_forward", landed=True)
    w_pout = w_pout.reshape(D_INNER, D_MODEL)
    y0, pooled, mixed = _pool_fwd(proj0, w_g, pool_scale, S, "pool_mix")
    sb_slots = _exchange_wait(sb_slots, sb_send, sb_recv, y0, _weight_plan, "sb_weights_wait")
    w_sin, w_sout = _allgather_weights(sb_slots, "sb_weights_forward", landed=True)
    w_sout = w_sout.reshape(D_INNER, D_MODEL)
    h1 = _matmul_residual(y0, w_pout, xt, "pool_out_proj")
    n1 = 2 * QK_WIDTH + 2 * D_INNER
    qkvz, u1 = _rms_matmul(h1, g1, w_sin, [(n1, BF16)], "sb_in_proj", tn=w_sin.shape[2])
    o, y1, ltot = _attn_fwd(qkvz, S, "sb_attention")
    dh2, d_norm_f, loss_row, dh2_b = _out_proj_loss_head(y1, w_sout, h1, gf, target,
                                                         "sb_out_proj_loss_head")

    def reduce_start(partials, tag, after=None, behind=None):
        n, done = len(partials), None
        after = c_arr if after is None else after
        if behind is None:
            from_sibling = list(_sibling_exchange(partials, None, "grad_sibling_exchange_" + tag))
        else:
            lands = [lax.empty((N_CHIPS, p.shape[1] // 2, p.shape[2]), F32) for p in partials]
            send, recv, bufs, token = _exchange_start(partials + lands, after, _sibling_plan, n,
                                                      "grad_sibling_start_" + tag)
            done = behind(token[0:1, 0:1])
            bufs = _exchange_wait(bufs, send, recv, done[0], _sibling_plan, "grad_sibling_wait_" + tag)
            partials, from_sibling, after = bufs[:n], bufs[n:], c_arr
        sums = [_chip_sum(p, s, c_arr, "grad_chip_sum_%s_%d" % (tag, i))
                for i, (p, s) in enumerate(zip(partials, from_sibling))]
        lands = [lax.empty((3,) + s.shape[1:], BF16) for s in sums]
        send, recv, bufs, token = _exchange_start(sums + lands, after, _chip_sum_plan, 3 * n,
                                                  "grad_chip_exchange_start_" + tag)
        return (partials, from_sibling, send, recv, bufs), token[0:1, 0:1], done

    def reduce_finish(started, after, tag):
        partials, from_sibling, send, recv, bufs = started
        received = _exchange_wait(bufs, send, recv, after, _chip_sum_plan,
                                  "grad_chip_exchange_wait_" + tag)[len(partials):]
        return [_reduce_half(p, s, r, place, "grad_reduce_%s_%d" % (tag, i))
                for i, (p, s, r) in enumerate(zip(partials, from_sibling, received))]

    shard = lambda i, j, t: (j, 0, 0)
    gw_sout = _matmul_tn(y1, dh2_b, D_INNER, D_MODEL, (D_INNER, D_MODEL), (1024, 1024),
                         lambda i, j, t: (i, j), "grad_sb_w_out", bm=1024, bn=1024)
    sout_started, token, (do, dproj1) = reduce_start(
        [gw_sout.reshape(N_CHIPS, -1, D_MODEL)], "sb_out",
        behind=lambda tok: _attn_gate_bwd(dh2_b, w_sout, qkvz, o, tok, "sb_gate_bwd"))
    dproj1 = _attn_bwd(qkvz, do, ltot, dproj1, S, "sb_attention_bwd")
    gw_sin = _matmul_tn(u1, dproj1, D_MODEL, n1, (N_CHIPS, D_MODEL, n1 // 4), (1, D_MODEL, n1 // 4),
                        shard, "grad_sb_w_in", bm=D_MODEL, bn=n1 // 4)
    sin_started, token, (dh1, d_g1, dh1_b) = reduce_start(
        [gw_sin], "sb_in", after=token,
        behind=lambda tok: _matmul_nt_rms_bwd(dproj1, w_sin, h1, g1 + tok, dh2, "sb_in_bwd", True,
                                              tk=w_sin.shape[2]))
    gw_pout = _matmul_tn(y0, dh1_b, D_INNER, D_MODEL, (D_INNER, D_MODEL), (1024, 1024),
                         lambda i, j, t: (i, j), "grad_pool_w_out", bm=1024, bn=1024)
    dmixed, dproj0, d_scale = _pool_gate_bwd(dh1_b, w_pout, proj0, mixed, pool_scale + token,
                                             "pool_gate_bwd")
    gw_g = _matmul_tn(pooled, dmixed, D_INNER, D_INNER, (N_CHIPS, GROUP_DIM, GROUP_DIM),
                      (N_CHIPS, GROUP_DIM // N_CHIPS, GROUP_DIM), lambda i, j, t: (0, i, 0),
                      "grad_pool_w", bm=GROUP_DIM, bn=GROUP_DIM, diagonal_blocks=True)
    mix_started, token, (dproj0,) = reduce_start(
        [gw_g, gw_pout.reshape(N_CHIPS, -1, D_MODEL)], "pool_mix",
        behind=lambda tok: (_pool_bwd(dmixed, w_g, dproj0, tok, S, "pool_bwd"),))
    n0 = 2 * D_INNER
    gw_pin = _matmul_tn(u0, dproj0, D_MODEL, n0, (N_CHIPS, D_MODEL, n0 // 4), (1, D_MODEL, n0 // 4),
                        shard, "grad_pool_w_in", bm=D_MODEL, bn=n0 // 4)
    pin_started, token, _ = reduce_start([gw_pin], "pool_in", after=token)
    dx, d_g0 = _matmul_nt_rms_bwd(dproj0, w_pin, xt, g0 + token, dh1, "pool_in_bwd", False,
                                  tk=w_pin.shape[2])

    small = _pack_small(jnp.concatenate([d_g0, d_g1], axis=0), d_scale, d_norm_f,
                        jnp.broadcast_to(loss_row[:, :1], (1, D_MODEL)))
    small_all, = _sibling_exchange([], small, "small_sums_exchange")
    grads = _join_halves(reduce_finish(pin_started, dx, "pool_in")
                         + reduce_finish(mix_started, dx, "pool_mix")
                         + reduce_finish(sin_started, dx, "sb_in")
                         + reduce_finish(sout_started, dx, "sb_out"), "grad_join_halves")

    deltas, new_m, new_v = [], [], []
    for w, g, m, v, nm in zip(w_shards, grads, m_shards, v_shards, names):
        d, mm, vv = _adamw(w, g, m, v, "adamw_" + nm)
        deltas.append(d)
        new_m.append(mm)
        new_v.append(vv)

    zero_row = jnp.zeros((1, D_MODEL), F32)
    g_small, d_small, m_small, v_small = _adamw_small(
        small_all, _pack_small(norm_g, pool_scale, norm_f, zero_row),
        _pack_small(m_norm_g, m_pool_scale, m_norm_f, zero_row),
        _pack_small(v_norm_g, v_pool_scale, v_norm_f, zero_row + 1.0), "adamw_small")
    loss = g_small[5, 0]

    def unpack_small(a):
        return a[0:2], a[2:4].reshape(1, D_INNER), a[4]

    def assemble(big, small3):
        ng, ps, nf = small3
        return [ng, big[0].reshape(pool_w_in.shape), big[1].reshape(pool_w.shape), ps,
                big[2].reshape(pool_w_out.shape), big[3].reshape(sb_w_in.shape),
                big[4].reshape(sb_w_out.shape), nf]

    return (loss, dx.reshape(x.shape),
            *assemble(grads, unpack_small(g_small)),
            *assemble(deltas, unpack_small(d_small)),
            *assemble(new_m, unpack_small(m_small)),
            *assemble(new_v, unpack_small(v_small)))
```

```python
import jax
import jax.numpy as jnp
from jax import lax
from jax.experimental import pallas as pl
from jax.experimental.pallas import tpu as pltpu

F32 = jnp.float32
BF16 = jnp.bfloat16
MESH = pl.DeviceIdType.MESH

D_MODEL = 1024
D_INNER = 2048
N_GROUPS = 4
GROUP_DIM = 512
HEAD_PAIR_QK = 128
HEAD_V = 128
QK_WIDTH = 1024
RMS_EPS = 1e-6
HALO = 16
N_CHIPS = 4
N_DEV = 8

ADAM_LR = 0.001
ADAM_B1 = 0.9
ADAM_B2 = 0.999
ADAM_EPS = 1e-08
ADAM_WD = 0.01
ADAM_STEP = 10

VMEM_LIMIT = 56 * 1024 * 1024

HBM_SPEC = pl.BlockSpec(memory_space=pltpu.HBM)


def _params(n_axes):
    return pltpu.CompilerParams(dimension_semantics=("arbitrary",) * n_axes,
                                vmem_limit_bytes=VMEM_LIMIT)


def _dot(a, b):
    return jnp.dot(a, b, preferred_element_type=F32)


def _dot_nt(a, b):
    return lax.dot_general(a, b, (((1,), (1,)), ((), ())), preferred_element_type=F32)


def _dot_tn(a, b):
    return lax.dot_general(a, b, (((0,), (0,)), ((), ())), preferred_element_type=F32)


def _sigmoid(z):
    return 1.0 / (1.0 + jnp.exp(-z))


def _row_blocks(tm, rows=256):
    return [slice(r, r + rows) for r in range(0, tm, rows)]


def _rms_matmul(h, g_row, w4, name, tm=1024):
    T = h.shape[0]
    n_shards, _, tn = w4.shape
    nm = T // tm

    def body(h_ref, g_ref, w_ref, o_ref, u_out, u_all):
        n, m = pl.program_id(0), pl.program_id(1)
        rows = pl.ds(pl.multiple_of(m * tm, tm), tm)

        @pl.when(n == 0)
        def _():
            x = h_ref[...]
            inv = lax.rsqrt(jnp.mean(x * x, axis=-1, keepdims=True) + RMS_EPS)
            u = (x * inv * g_ref[...]).astype(BF16)
            u_all[rows, :] = u
            u_out[...] = u

        o_ref[...] = _dot(u_all[rows, :], w_ref[0]).astype(BF16)

    return pl.pallas_call(
        body, name=name, grid=(n_shards, nm),
        in_specs=[pl.BlockSpec((tm, D_MODEL), lambda n, m: (jnp.where(n == 0, m, nm - 1), 0)),
                  pl.BlockSpec((1, D_MODEL), lambda n, m: (0, 0)),
                  pl.BlockSpec((1, D_MODEL, tn), lambda n, m: (n, 0, 0))],
        out_specs=[pl.BlockSpec((tm, tn), lambda n, m: (m, n)),
                   pl.BlockSpec((tm, D_MODEL), lambda n, m: (jnp.where(n == 0, m, nm - 1), 0))],
        out_shape=[jax.ShapeDtypeStruct((T, n_shards * tn), BF16),
                   jax.ShapeDtypeStruct((T, D_MODEL), BF16)],
        scratch_shapes=[pltpu.VMEM((T, D_MODEL), BF16)],
        compiler_params=_params(2),
    )(h, g_row, w4)


def _matmul_residual(a, w, res, name, tm=1024, tn=1024):
    T, K = a.shape
    N = w.shape[1]

    def body(a_ref, w_ref, r_ref, o_ref):
        o_ref[...] = r_ref[...] + _dot(a_ref[...], w_ref[...])

    return pl.pallas_call(
        body, name=name, grid=(T // tm, N // tn),
        in_specs=[pl.BlockSpec((tm, K), lambda m, n: (m, 0)),
                  pl.BlockSpec((K, tn), lambda m, n: (0, n)),
                  pl.BlockSpec((tm, tn), lambda m, n: (m, n))],
        out_specs=pl.BlockSpec((tm, tn), lambda m, n: (m, n)),
        out_shape=jax.ShapeDtypeStruct((T, N), F32),
        compiler_params=_params(2),
    )(a, w, res)


def _matmul_tn(a, b, a_cols, b_cols, out_shape, out_block, out_map, name, bm, bn, tk=2048,
               diagonal_blocks=False):
    T = a.shape[0]
    tk = min(tk, T)

    def body(a_ref, b_ref, o_ref):
        @pl.when(pl.program_id(2) == 0)
        def _():
            o_ref[...] = jnp.zeros_like(o_ref)

        part = _dot_tn(a_ref[...].astype(BF16), b_ref[...].astype(BF16))
        o_ref[...] += part.reshape(o_ref.shape)

    b_map = (lambda i, j, t: (t, i)) if diagonal_blocks else (lambda i, j, t: (t, j))
    return pl.pallas_call(
        body, name=name, grid=(a_cols // bm, 1 if diagonal_blocks else b_cols // bn, T // tk),
        in_specs=[pl.BlockSpec((tk, bm), lambda i, j, t: (t, i)),
                  pl.BlockSpec((tk, bn), b_map)],
        out_specs=pl.BlockSpec(out_block, out_map),
        out_shape=jax.ShapeDtypeStruct(out_shape, F32),
        compiler_params=_params(3),
    )(a, b)


def _matmul_nt_rms_bwd(dproj, w4, h, g_row, dres, name, with_bf16, tm=1024, tk=512):
    T, cols = dproj.shape
    per_shard = w4.shape[2] // tk
    nk = cols // tk

    def body(dp_ref, w_ref, h_ref, g_ref, r_ref, dx_ref, dg_ref, *rest):
        acc = rest[-1]
        m, k = pl.program_id(0), pl.program_id(1)

        @pl.when(k == 0)
        def _():
            acc[...] = jnp.zeros_like(acc)

        @pl.when((k == 0) & (m == 0))
        def _():
            dg_ref[...] = jnp.zeros_like(dg_ref)

        acc[...] += _dot_nt(dp_ref[...], w_ref[0])

        @pl.when(k == nk - 1)
        def _():
            du = acc[...]
            x = h_ref[...]
            inv = lax.rsqrt(jnp.mean(x * x, axis=-1, keepdims=True) + RMS_EPS)
            xhat = x * inv
            dg_ref[...] += jnp.sum(du * xhat, axis=0, keepdims=True)
            dxh = du * g_ref[...]
            proj = jnp.mean(dxh * xhat, axis=-1, keepdims=True)
            dx = r_ref[...] + inv * (dxh - xhat * proj)
            dx_ref[...] = dx
            if with_bf16:
                rest[0][...] = dx.astype(BF16)

    rows = pl.BlockSpec((tm, D_MODEL), lambda m, k: (m, 0))
    return pl.pallas_call(
        body, name=name, grid=(T // tm, nk),
        in_specs=[pl.BlockSpec((tm, tk), lambda m, k: (m, k)),
                  pl.BlockSpec((1, D_MODEL, tk), lambda m, k: (k // per_shard, 0, k % per_shard)),
                  rows, pl.BlockSpec((1, D_MODEL), lambda m, k: (0, 0)), rows],
        out_specs=[rows, pl.BlockSpec((1, D_MODEL), lambda m, k: (0, 0))] + [rows] * with_bf16,
        out_shape=[jax.ShapeDtypeStruct((T, D_MODEL), F32), jax.ShapeDtypeStruct((1, D_MODEL), F32)]
        + [jax.ShapeDtypeStruct((T, D_MODEL), BF16)] * with_bf16,
        scratch_shapes=[pltpu.VMEM((tm, D_MODEL), F32)],
        compiler_params=_params(2),
    )(dproj, w4, h, g_row, dres)


def _window_of(g):
    return jnp.left_shift(2, g)


def _select_stage(g, stages):
    res = stages[0]
    for i in range(1, len(stages)):
        res = jnp.where(g >= i, stages[i], res)
    return res


def _pool_fwd(proj0, wg4, scale_row, S, name, tm=1024):
    T = proj0.shape[0]
    tm = min(tm, S)
    blocks_per_seq = S // tm
    hb = tm // HALO

    def body(x_ref, halo_ref, z_ref, w_ref, s_ref, y_ref, p_ref, mix_ref):
        m, g = pl.program_id(0), pl.program_id(1)
        first = (m % blocks_per_seq) == 0
        halo = jnp.where(first, 0.0, halo_ref[...].astype(F32))
        x = x_ref[...].astype(F32)
        ext = jnp.concatenate([halo, x], axis=0)
        stages = []
        cur = ext
        for sh in (1, 2, 4, 8):
            cur = cur + pltpu.roll(cur, sh, 0)
            stages.append(cur[HALO:, :])
        win_sum = _select_stage(g, stages)
        pos = (m % blocks_per_seq) * tm + lax.broadcasted_iota(jnp.int32, (tm, 1), 0)
        count = jnp.minimum(pos + 1, _window_of(g)).astype(F32)
        p_ref[...] = (win_sum / count - x).astype(BF16)
        w = w_ref[...].reshape(GROUP_DIM, GROUP_DIM)
        for rows in _row_blocks(tm):
            mixed = _dot(p_ref[rows, :], w)
            z = z_ref[rows, :].astype(F32)
            y_ref[rows, :] = (mixed * s_ref[...] * (z * _sigmoid(z))).astype(BF16)
            mix_ref[rows, :] = mixed.astype(BF16)

    blk = lambda m, g: (m, g)
    return pl.pallas_call(
        body, name=name, grid=(T // tm, N_GROUPS),
        in_specs=[pl.BlockSpec((tm, GROUP_DIM), blk),
                  pl.BlockSpec((HALO, GROUP_DIM), lambda m, g: (jnp.maximum(m * hb - 1, 0), g)),
                  pl.BlockSpec((tm, GROUP_DIM), lambda m, g: (m, N_GROUPS + g)),
                  pl.BlockSpec((N_CHIPS, GROUP_DIM // N_CHIPS, GROUP_DIM), lambda m, g: (0, g, 0)),
                  pl.BlockSpec((1, GROUP_DIM), lambda m, g: (0, g))],
        out_specs=[pl.BlockSpec((tm, GROUP_DIM), blk)] * 3,
        out_shape=[jax.ShapeDtypeStruct((T, D_INNER), BF16)] * 3,
        compiler_params=_params(2),
    )(proj0, proj0, proj0, wg4, scale_row)


def _pool_gate_bwd(dh, w_out, proj0, mixed, scale_row, name, tm=1024, tn=512):
    T = dh.shape[0]
    gate_b0 = D_INNER // tn

    def body(dh_ref, w_ref, z_ref, mix_ref, s_ref, dm_ref, dz_ref, ds_ref):
        m, n = pl.program_id(0), pl.program_id(1)

        @pl.when((m == 0) & (n == 0))
        def _():
            ds_ref[...] = jnp.zeros_like(ds_ref)

        cols = pl.ds(pl.multiple_of(n * tn, tn), tn)
        s = s_ref[...]
        ds = ds_ref[:, cols]
        for rows in _row_blocks(tm):
            dy = _dot_nt(dh_ref[rows, :], w_ref[...])
            z = z_ref[rows, :].astype(F32)
            sig = _sigmoid(z)
            silu = z * sig
            mixed = mix_ref[rows, :].astype(F32)
            dm_ref[rows, :] = (dy * s * silu).astype(BF16)
            dz_ref[rows, :] = (dy * mixed * s * (sig * (1.0 + z * (1.0 - sig)))).astype(BF16)
            ds = ds + jnp.sum(dy * mixed * silu, axis=0, keepdims=True)
        ds_ref[:, cols] = ds

    return pl.pallas_call(
        body, name=name, grid=(T // tm, D_INNER // tn),
        in_specs=[pl.BlockSpec((tm, D_MODEL), lambda m, n: (m, 0)),
                  pl.BlockSpec((tn, D_MODEL), lambda m, n: (n, 0)),
                  pl.BlockSpec((tm, tn), lambda m, n: (m, gate_b0 + n)),
                  pl.BlockSpec((tm, tn), lambda m, n: (m, n)),
                  pl.BlockSpec((1, tn), lambda m, n: (0, n))],
        out_specs=[pl.BlockSpec((tm, tn), lambda m, n: (m, n)),
                   pl.BlockSpec((tm, tn), lambda m, n: (m, gate_b0 + n)),
                   pl.BlockSpec((1, D_INNER), lambda m, n: (0, 0))],
        out_shape=[jax.ShapeDtypeStruct((T, D_INNER), BF16),
                   jax.ShapeDtypeStruct((T, 2 * D_INNER), BF16),
                   jax.ShapeDtypeStruct((1, D_INNER), F32)],
        compiler_params=_params(2),
    )(dh, w_out, proj0, mixed, scale_row)


def _pool_bwd(dmixed, wg4, dproj0, after, S, name, tm=1024):
    T = dmixed.shape[0]
    tm = min(tm, S)
    blocks_per_seq = S // tm
    hb = tm // HALO
    n_halo_blocks = T // HALO

    def body(dm_ref, halo_ref, w_ref, _, __, o_ref):
        m, g = pl.program_id(0), pl.program_id(1)
        ext = jnp.concatenate([dm_ref[...], halo_ref[...]], axis=0)
        dp = _dot_nt(ext, w_ref[...].reshape(GROUP_DIM, GROUP_DIM))
        pos = (m % blocks_per_seq) * tm + lax.broadcasted_iota(jnp.int32, (tm + HALO, 1), 0)
        count = jnp.minimum(pos + 1, _window_of(g)).astype(F32)
        c = jnp.where(pos < S, dp / count, 0.0)
        n = tm + HALO
        stages = []
        cur = c
        for sh in (1, 2, 4, 8):
            cur = cur + pltpu.roll(cur, n - sh, 0)
            stages.append(cur[:tm, :])
        o_ref[...] = (_select_stage(g, stages) - dp[:tm, :]).astype(BF16)

    blk = lambda m, g: (m, g)
    return pl.pallas_call(
        body, name=name, grid=(T // tm, N_GROUPS),
        in_specs=[pl.BlockSpec((tm, GROUP_DIM), blk),
                  pl.BlockSpec((HALO, GROUP_DIM),
                               lambda m, g: (jnp.minimum((m + 1) * hb, n_halo_blocks - 1), g)),
                  pl.BlockSpec((N_CHIPS, GROUP_DIM // N_CHIPS, GROUP_DIM), lambda m, g: (0, g, 0)),
                  HBM_SPEC, ANY_SPEC],
        out_specs=pl.BlockSpec((tm, GROUP_DIM), blk),
        out_shape=jax.ShapeDtypeStruct(dproj0.shape, dproj0.dtype),
        input_output_aliases={3: 0},
        compiler_params=_params(2),
    )(dmixed, dmixed, wg4, dproj0, after)


TQ = 256


def _split_dot(x, m):
    hi = x.astype(BF16)
    lo = (x - hi.astype(F32)).astype(BF16)
    return _dot(hi, m) + _dot(lo, m)


NEG_LOG2E = -1.4426950408889634


def _log_terms(z):
    soft = jnp.log(1.0 + jnp.exp2(jnp.abs(z) * NEG_LOG2E))
    log_beta = jnp.minimum(z, 0.0) - soft
    return log_beta, log_beta - z


N_HEADS = 16
FWD_HEADS = BWD_HEADS = 4


def _masked_heads(x, heads):
    lane = lax.broadcasted_iota(jnp.int32, (1, HEAD_PAIR_QK), 1)
    out = []
    for hh in range(heads):
        slab = x[:, (hh // 2) * HEAD_PAIR_QK:(hh // 2 + 1) * HEAD_PAIR_QK]
        out.append(jnp.where((lane // 64) == hh % 2, slab, jnp.zeros_like(slab)))
    return out


def _attn_fwd(qkvz, S, name):
    T = qkvz.shape[0]
    nq = S // TQ
    HEADS, QK_W, V_W = FWD_HEADS, FWD_HEADS * 64, FWD_HEADS * HEAD_V
    k_b0 = QK_WIDTH // QK_W
    v_b0 = 2 * QK_WIDTH // V_W
    z_b0 = (2 * QK_WIDTH + D_INNER) // V_W
    hs = range(HEADS)

    def body(q_ref, k_ref, v_ref, z_ref, o_ref, y_ref, lt_ref):
        row = lax.broadcasted_iota(jnp.int32, (TQ, TQ), 0)
        col = lax.broadcasted_iota(jnp.int32, (TQ, TQ), 1)
        causal = col < row
        later_in_block = (row > col).astype(BF16)
        lax.fori_loop(0, nq, lambda qi, _: q_block(qi, causal, later_in_block,
                                                   q_ref, k_ref, v_ref, z_ref, o_ref, y_ref, lt_ref), 0)

    def q_block(qi, causal, later_in_block, q_ref, k_ref, v_ref, z_ref, o_ref, y_ref, lt_ref):
        rows = pl.ds(pl.multiple_of(qi * TQ, TQ), TQ)
        qms = [qm * 0.125 for qm in _masked_heads(q_ref[rows, :], HEADS)]

        def step(j, carry, diagonal):
            koff = pl.multiple_of(j * TQ, TQ)
            kbs = [k_ref[pl.ds(koff, TQ), p * HEAD_PAIR_QK:(p + 1) * HEAD_PAIR_QK]
                   for p in range(HEADS // 2)]
            run, acc = [carry[2 * hh] for hh in hs], [carry[2 * hh + 1] for hh in hs]
            z = [_dot_nt(qms[hh], kbs[hh // 2]) for hh in hs]
            terms = [_log_terms(z[hh]) for hh in hs]
            log_om = [jnp.where(causal, t[1], 0.0) if diagonal else t[1] for t in terms]
            later = [_split_dot(log_om[hh], later_in_block) for hh in hs]
            a = [jnp.exp(terms[hh][0] + (run[hh] + later[hh])) for hh in hs]
            if diagonal:
                a = [jnp.where(causal, a[hh], 0.0) for hh in hs]
            out = []
            for hh in hs:
                vb = v_ref[pl.ds(koff, TQ), hh * HEAD_V:(hh + 1) * HEAD_V]
                out += [run[hh] + jnp.sum(log_om[hh], axis=1, keepdims=True),
                        acc[hh] + _dot(a[hh].astype(BF16), vb)]
            return tuple(out)

        zero = (jnp.zeros((TQ, 1), F32), jnp.zeros((TQ, HEAD_V), F32))
        carry = step(qi, zero * HEADS, True)
        carry = lax.fori_loop(0, qi, lambda i, c: step(qi - 1 - i, c, False), carry)
        for hh in hs:
            sl = slice(hh * HEAD_V, (hh + 1) * HEAD_V)
            acc = carry[2 * hh + 1]
            z = z_ref[rows, sl].astype(F32)
            o_ref[rows, sl] = acc.astype(BF16)
            y_ref[rows, sl] = (acc * (z * _sigmoid(z))).astype(BF16)
            lt_ref[rows, hh:hh + 1] = carry[2 * hh]
        return 0

    blk = lambda b, p: (b, p)
    return pl.pallas_call(
        body, name=name, grid=(T // S, N_HEADS // HEADS),
        in_specs=[pl.BlockSpec((S, QK_W), blk),
                  pl.BlockSpec((S, QK_W), lambda b, p: (b, k_b0 + p)),
                  pl.BlockSpec((S, V_W), lambda b, p: (b, v_b0 + p)),
                  pl.BlockSpec((S, V_W), lambda b, p: (b, z_b0 + p))],
        out_specs=[pl.BlockSpec((S, V_W), blk),
                   pl.BlockSpec((S, V_W), blk),
                   pl.BlockSpec((None, S, HEADS), lambda b, p: (p, b, 0))],
        out_shape=[jax.ShapeDtypeStruct((T, D_INNER), BF16),
                   jax.ShapeDtypeStruct((T, D_INNER), BF16),
                   jax.ShapeDtypeStruct((N_HEADS // HEADS, T, HEADS), F32)],
        compiler_params=_params(2),
    )(qkvz, qkvz, qkvz, qkvz)


def _attn_gate_bwd(dh, w_out, qkvz, o, after, name, tm=1024, tn=512):
    T = dh.shape[0]
    gate_b0 = (2 * QK_WIDTH + D_INNER) // tn

    def body(dh_ref, w_ref, z_ref, o_ref, _, do_ref, dz_ref):
        for rows in _row_blocks(tm):
            dy = _dot_nt(dh_ref[rows, :], w_ref[...])
            z = z_ref[rows, :].astype(F32)
            sig = _sigmoid(z)
            do_ref[rows, :] = (dy * (z * sig)).astype(BF16)
            dz_ref[rows, :] = (dy * o_ref[rows, :].astype(F32)
                               * (sig * (1.0 + z * (1.0 - sig)))).astype(BF16)

    return pl.pallas_call(
        body, name=name, grid=(T // tm, D_INNER // tn),
        in_specs=[pl.BlockSpec((tm, D_MODEL), lambda m, n: (m, 0)),
                  pl.BlockSpec((tn, D_MODEL), lambda m, n: (n, 0)),
                  pl.BlockSpec((tm, tn), lambda m, n: (m, gate_b0 + n)),
                  pl.BlockSpec((tm, tn), lambda m, n: (m, n)),
                  ANY_SPEC],
        out_specs=[pl.BlockSpec((tm, tn), lambda m, n: (m, n)),
                   pl.BlockSpec((tm, tn), lambda m, n: (m, gate_b0 + n))],
        out_shape=[jax.ShapeDtypeStruct((T, D_INNER), BF16),
                   jax.ShapeDtypeStruct((T, 2 * QK_WIDTH + 2 * D_INNER), BF16)],
        compiler_params=_params(2),
    )(dh, w_out, qkvz, o, after)


def _attn_bwd(qkv, do, ltot, dproj1, S, name):
    T = qkv.shape[0]
    nq = S // TQ
    HEADS, QK_W, V_W = BWD_HEADS, BWD_HEADS * 64, BWD_HEADS * HEAD_V
    k_b0 = QK_WIDTH // QK_W
    v_b0 = 2 * QK_WIDTH // V_W
    hs = range(HEADS)
    pairs = range(HEADS // 2)
    n_groups = N_HEADS // HEADS
    n_steps = (T // S) * n_groups

    def body(q_ref, k_ref, v_ref, do_ref, lt_ref, _, out_ref, dq_s, dk_s, dv_s, dkb_s, dvb_s, sems):
        b, p = pl.program_id(0), pl.program_id(1)
        row = lax.broadcasted_iota(jnp.int32, (TQ, TQ), 0)
        col = lax.broadcasted_iota(jnp.int32, (TQ, TQ), 1)
        causal = col < row
        upto = (row <= col).astype(BF16)
        before = (row < col).astype(BF16)
        dk_s[...] = jnp.zeros_like(dk_s)
        dv_s[...] = jnp.zeros_like(dv_s)
        rows = pl.ds(pl.multiple_of(b * S, TQ), S)
        copies = [
            pltpu.make_async_copy(
                dq_s, out_ref.at[rows, pl.ds(pl.multiple_of(p * QK_W, 128), QK_W)], sems.at[0]),
            pltpu.make_async_copy(
                dkb_s, out_ref.at[rows, pl.ds(pl.multiple_of(QK_WIDTH + p * QK_W, 128), QK_W)],
                sems.at[1]),
            pltpu.make_async_copy(
                dvb_s, out_ref.at[rows, pl.ds(pl.multiple_of(2 * QK_WIDTH + p * V_W, 128), V_W)],
                sems.at[2]),
        ]
        step = b * n_groups + p

        @pl.when(step > 0)
        def _():
            for cp in copies:
                cp.wait()

        def q_block(qi, _):
            qoff = pl.multiple_of(qi * TQ, TQ)
            qms = [qm * 0.125 for qm in _masked_heads(q_ref[pl.ds(qoff, TQ), :], HEADS)]
            vsl = [slice(hh * HEAD_V, (hh + 1) * HEAD_V) for hh in hs]
            psl = [slice(pp * HEAD_PAIR_QK, (pp + 1) * HEAD_PAIR_QK) for pp in pairs]
            do_h = [do_ref[pl.ds(qoff, TQ), sl] for sl in vsl]
            total = [lt_ref[pl.ds(qoff, TQ), hh:hh + 1] for hh in hs]

            def k_block(j, carry, diagonal):
                koff = pl.multiple_of(j * TQ, TQ)
                kms = _masked_heads(k_ref[pl.ds(koff, TQ), :], HEADS)
                g_before = [carry[2 * hh] for hh in hs]
                lom_before = [carry[2 * hh + 1] for hh in hs]
                z = [_dot_nt(qms[hh], kms[hh]) for hh in hs]
                da = [_dot_nt(do_h[hh], v_ref[pl.ds(koff, TQ), vsl[hh]]) for hh in hs]
                terms = [_log_terms(z[hh]) for hh in hs]
                log_om = [jnp.where(causal, t[1], 0.0) if diagonal else t[1] for t in terms]
                prefix = [_split_dot(log_om[hh], upto) for hh in hs]
                a = [jnp.exp(terms[hh][0] + ((total[hh] - lom_before[hh]) - prefix[hh])) for hh in hs]
                if diagonal:
                    a = [jnp.where(causal, a[hh], 0.0) for hh in hs]
                g = [a[hh] * da[hh] for hh in hs]
                g_prefix = [_dot(g[hh].astype(BF16), before) for hh in hs]
                out, dzs = [], []
                for hh in hs:
                    beta = jnp.exp(terms[hh][0])
                    g_excl = (g_before[hh] + g_prefix[hh]) * beta
                    if diagonal:
                        g_excl = jnp.where(causal, g_excl, 0.0)
                    dzs.append((g[hh] * (1.0 - beta) - g_excl).astype(BF16))
                    out += [g_before[hh] + jnp.sum(g[hh], axis=1, keepdims=True),
                            lom_before[hh] + jnp.sum(log_om[hh], axis=1, keepdims=True)]
                for hh in hs:
                    dv_s[pl.ds(koff, TQ), vsl[hh]] += _dot_tn(a[hh].astype(BF16), do_h[hh])
                dq = []
                for pp in pairs:
                    pair = slice(2 * pp, 2 * pp + 2)
                    dq.append(carry[2 * HEADS + pp] + _dot(jnp.concatenate(dzs[pair], axis=1),
                                                           jnp.concatenate(kms[pair], axis=0)))
                    dk_s[pl.ds(koff, TQ), psl[pp]] += _dot_tn(jnp.concatenate(dzs[pair], axis=0),
                                                              jnp.concatenate(qms[pair], axis=0))
                return tuple(out) + tuple(dq)

            zero = jnp.zeros((TQ, 1), F32)
            carry = (zero,) * (2 * HEADS) + (jnp.zeros((TQ, HEAD_PAIR_QK), F32),) * (HEADS // 2)
            carry = lax.fori_loop(0, qi, lambda j, c: k_block(j, c, False), carry)
            carry = k_block(qi, carry, True)
            for pp in pairs:
                dq_s[pl.ds(qoff, TQ), psl[pp]] = (carry[2 * HEADS + pp] * 0.125).astype(BF16)
            return 0

        lax.fori_loop(0, nq, q_block, 0)
        dkb_s[...] = dk_s[...].astype(BF16)
        dvb_s[...] = dv_s[...].astype(BF16)
        for cp in copies:
            cp.start()

        @pl.when(step == n_steps - 1)
        def _():
            for cp in copies:
                cp.wait()

    return pl.pallas_call(
        body, name=name, grid=(T // S, N_HEADS // HEADS),
        in_specs=[pl.BlockSpec((S, QK_W), lambda b, p: (b, p)),
                  pl.BlockSpec((S, QK_W), lambda b, p: (b, k_b0 + p)),
                  pl.BlockSpec((S, V_W), lambda b, p: (b, v_b0 + p)),
                  pl.BlockSpec((S, V_W), lambda b, p: (b, p)),
                  pl.BlockSpec((None, S, HEADS), lambda b, p: (p, b, 0)),
                  HBM_SPEC],
        out_specs=HBM_SPEC,
        out_shape=jax.ShapeDtypeStruct(dproj1.shape, dproj1.dtype),
        input_output_aliases={5: 0},
        scratch_shapes=[pltpu.VMEM((S, QK_W), BF16),
                        pltpu.VMEM((S, QK_W), F32),
                        pltpu.VMEM((S, V_W), F32),
                        pltpu.VMEM((S, QK_W), BF16),
                        pltpu.VMEM((S, V_W), BF16),
                        pltpu.SemaphoreType.DMA((3,))],
        compiler_params=_params(2),
    )(qkv, qkv, qkv, do, ltot, dproj1)


def _out_proj_loss_head(a, w, res, g_row, target, name, tm=512):
    T, K = a.shape

    def body(a_ref, w_ref, r_ref, g_ref, t_ref, dh_ref, dg_ref, loss_ref, dhb_ref):
        @pl.when(pl.program_id(0) == 0)
        def _():
            dg_ref[...] = jnp.zeros_like(dg_ref)
            loss_ref[...] = jnp.zeros_like(loss_ref)

        gain = g_ref[...]
        dg, loss = dg_ref[...], loss_ref[...]
        for rows in _row_blocks(tm):
            x = r_ref[rows, :] + _dot(a_ref[rows, :], w_ref[...])
            inv = lax.rsqrt(jnp.mean(x * x, axis=-1, keepdims=True) + RMS_EPS)
            xhat = x * inv
            err = xhat * gain - t_ref[rows, :]
            per_token = jnp.mean(err * err, axis=-1, keepdims=True)
            loss = loss + 0.5 * jnp.sum(per_token, axis=0, keepdims=True)
            dy = err * (1.0 / D_MODEL)
            dg = dg + jnp.sum(dy * xhat, axis=0, keepdims=True)
            dxh = dy * gain
            proj = jnp.mean(dxh * xhat, axis=-1, keepdims=True)
            dh = inv * (dxh - xhat * proj)
            dh_ref[rows, :] = dh
            dhb_ref[rows, :] = dh.astype(BF16)
        dg_ref[...] = dg
        loss_ref[...] = loss

    return pl.pallas_call(
        body, name=name, grid=(T // tm,),
        in_specs=[pl.BlockSpec((tm, K), lambda m: (m, 0)),
                  pl.BlockSpec((K, D_MODEL), lambda m: (0, 0)),
                  pl.BlockSpec((tm, D_MODEL), lambda m: (m, 0)),
                  pl.BlockSpec((1, D_MODEL), lambda m: (0, 0)),
                  pl.BlockSpec((tm, D_MODEL), lambda m: (m, 0))],
        out_specs=[pl.BlockSpec((tm, D_MODEL), lambda m: (m, 0)),
                   pl.BlockSpec((1, D_MODEL), lambda m: (0, 0)),
                   pl.BlockSpec((1, 128), lambda m: (0, 0)),
                   pl.BlockSpec((tm, D_MODEL), lambda m: (m, 0))],
        out_shape=[jax.ShapeDtypeStruct((T, D_MODEL), F32),
                   jax.ShapeDtypeStruct((1, D_MODEL), F32),
                   jax.ShapeDtypeStruct((1, 128), F32),
                   jax.ShapeDtypeStruct((T, D_MODEL), BF16)],
        compiler_params=_params(1),
    )(a, w, res, g_row, target)


def _place():
    return lax.axis_index("x"), lax.axis_index("y"), lax.axis_index("c")


def _other_chips(x, y):
    return [(1 - x, y), (x, 1 - y), (1 - x, 1 - y)]


def _half(ref, c):
    hr = ref.shape[-2] // 2
    return pl.ds(pl.multiple_of(c * hr, 8), hr)


def _cast_to_slot(shard, chip, name, tr=256):
    R, C = shard.shape

    def body(chip_ref, w_ref, o_ref):
        o_ref[0] = w_ref[...].astype(BF16)

    return pl.pallas_call(
        body, name=name,
        grid_spec=pltpu.PrefetchScalarGridSpec(
            num_scalar_prefetch=1, grid=(R // tr,),
            in_specs=[pl.BlockSpec((tr, C), lambda i, chip_ref: (i, 0))],
            out_specs=pl.BlockSpec((1, tr, C), lambda i, chip_ref: (chip_ref[0], i, 0))),
        out_shape=jax.ShapeDtypeStruct((N_CHIPS, R, C), BF16),
        compiler_params=_params(1),
    )(chip, shard)


def _weight_plan(bufs):
    x, y, c = _place()
    plan = []
    for buf in bufs:
        mine = buf.at[2 * x + y, _half(buf, c)]
        for ox, oy in _other_chips(x, y):
            plan.append((mine, mine, (ox, oy, c), buf.at[2 * ox + oy, _half(buf, c)]))
    return plan


def _chip_sum_plan(bufs):
    x, y, c = _place()
    n = len(bufs) // 2
    plan = []
    for sums, land in zip(bufs[:n], bufs[n:]):
        for k, (ox, oy) in enumerate(_other_chips(x, y)):
            plan.append((sums.at[2 * ox + oy], land.at[k], (ox, oy, c), land.at[k]))
    return plan


def _sibling_plan(bufs):
    x, y, c = _place()
    n = len(bufs) // 2
    return [(p.at[:, _half(p, 1 - c)], land, (x, y, 1 - c), land)
            for p, land in zip(bufs[:n], bufs[n:])]


SEM_SPEC = pl.BlockSpec(memory_space=pltpu.SEMAPHORE)
ANY_SPEC = pl.BlockSpec(memory_space=pl.ANY)
DATAFLOW = pltpu.SideEffectType.DATAFLOW_SIDE_EFFECTING


def _in_hbm(a):
    return pltpu.with_memory_space_constraint(a, pltpu.HBM)


def _exchange_start(bufs, after, plan, n_copies, name):
    nb = len(bufs)

    def body(*refs):
        send_sems, recv_sems = refs[nb + 1], refs[nb + 2]
        for i, (src, dst, dev, _) in enumerate(plan(refs[:nb])):
            pltpu.make_async_remote_copy(
                src_ref=src, dst_ref=dst, send_sem=send_sems.at[i], recv_sem=recv_sems.at[i],
                device_id=dev, device_id_type=MESH).start()
        token = refs[-1]
        token[...] = jnp.zeros_like(token)

    res = pl.pallas_call(
        body, name=name,
        in_specs=[HBM_SPEC] * nb + [ANY_SPEC],
        out_specs=[SEM_SPEC, SEM_SPEC] + [HBM_SPEC] * nb + [pl.BlockSpec(memory_space=pltpu.VMEM)],
        out_shape=[pltpu.SemaphoreType.DMA((n_copies,)), pltpu.SemaphoreType.DMA((n_copies,))]
        + [pltpu.HBM(b.shape, b.dtype) for b in bufs] + [jax.ShapeDtypeStruct((8, 128), F32)],
        input_output_aliases={i: 2 + i for i in range(nb)},
        compiler_params=pltpu.CompilerParams(has_side_effects=DATAFLOW),
    )(*[_in_hbm(b) for b in bufs], after)
    return res[0], res[1], list(res[2:2 + nb]), res[-1]


def _exchange_wait(bufs, send_sems, recv_sems, after, plan, name):
    nb = len(bufs)

    def body(*refs):
        sends, recvs = refs[nb], refs[nb + 1]
        for i, (src, dst, dev, landing) in enumerate(plan(refs[:nb])):
            pltpu.make_async_remote_copy(
                src_ref=src, dst_ref=landing, send_sem=sends.at[i], recv_sem=recvs.at[i],
                device_id=dev, device_id_type=MESH).wait()

    res = pl.pallas_call(
        body, name=name,
        in_specs=[HBM_SPEC] * nb + [SEM_SPEC, SEM_SPEC, ANY_SPEC],
        out_specs=[HBM_SPEC] * nb,
        out_shape=[pltpu.HBM(b.shape, b.dtype) for b in bufs],
        input_output_aliases={i: i for i in range(nb)},
        compiler_params=pltpu.CompilerParams(has_side_effects=DATAFLOW),
    )(*bufs, send_sems, recv_sems, after)
    return list(res)


def _allgather_weights(slots, name, landed=False):
    n = len(slots)

    def body(*refs):
        outs = refs[n:2 * n]
        send_sems, recv_sems, fwd_send, fwd_recv = refs[2 * n:]
        x, y, c = _place()
        chips = _other_chips(x, y)

        def landing(a, chip, half_of):
            return outs[a].at[2 * chip[0] + chip[1], _half(outs[a], half_of)]

        def ici(a, k, chip_from, to):
            return pltpu.make_async_remote_copy(
                src_ref=landing(a, chip_from, c), dst_ref=landing(a, chip_from, c),
                send_sem=send_sems.at[a, k], recv_sem=recv_sems.at[a, k],
                device_id=to, device_id_type=MESH)

        def d2d(a, k, chip_from, half_of):
            return pltpu.make_async_remote_copy(
                src_ref=landing(a, chip_from, half_of), dst_ref=landing(a, chip_from, half_of),
                send_sem=fwd_send.at[a, k], recv_sem=fwd_recv.at[a, k],
                device_id=(x, y, 1 - c), device_id_type=MESH)

        sends = []
        if not landed:
            sends = [ici(a, k, (x, y), (*chips[k], c)) for a in range(n) for k in range(3)]
        for cp in sends:
            cp.start()
        forwards = []
        for a in range(n):
            for k in range(3):
                if not landed:
                    ici(a, k, chips[k], (x, y, c)).wait_recv()
                fw = d2d(a, k, chips[k], c)
                fw.start()
                forwards.append(fw)
        for a in range(n):
            for k in range(3):
                d2d(a, k, chips[k], 1 - c).wait_recv()
        for cp in sends + forwards:
            cp.wait_send()

    return pl.pallas_call(
        body, name=name,
        in_specs=[HBM_SPEC] * n, out_specs=[HBM_SPEC] * n,
        out_shape=[jax.ShapeDtypeStruct(s.shape, s.dtype) for s in slots],
        input_output_aliases={a: a for a in range(n)},
        scratch_shapes=[pltpu.SemaphoreType.DMA((n, 3)), pltpu.SemaphoreType.DMA((n, 3)),
                        pltpu.SemaphoreType.DMA((n, 3)), pltpu.SemaphoreType.DMA((n, 3))],
    )(*slots)


def _sibling_exchange(partials, small, name):
    n = len(partials)
    ns = 0 if small is None else 1

    def body(*refs):
        ins, outs = refs[:n], refs[n + ns:2 * n + ns]
        send_sems, recv_sems = refs[2 * (n + ns):2 * (n + ns) + 2]
        x, y, c = _place()
        me = 4 * x + 2 * y + c
        sends = [pltpu.make_async_remote_copy(
            src_ref=ins[a].at[:, _half(ins[a], 1 - c)], dst_ref=outs[a],
            send_sem=send_sems.at[a], recv_sem=recv_sems.at[a],
            device_id=(x, y, 1 - c), device_id_type=MESH) for a in range(n)]
        if ns:
            small_ref, small_all = refs[n], refs[2 * n + 1]
            s_send, s_recv, loc_sem = refs[2 * (n + ns) + 2:]
            local = pltpu.make_async_copy(small_ref, small_all.at[me], loc_sem)
            local.start()
            for d in range(1, N_DEV):
                px, py, pc = x ^ ((d >> 2) & 1), y ^ ((d >> 1) & 1), c ^ (d & 1)
                sends.append(pltpu.make_async_remote_copy(
                    src_ref=small_ref, dst_ref=small_all.at[me],
                    send_sem=s_send.at[d - 1], recv_sem=s_recv.at[d - 1],
                    device_id=(px, py, pc), device_id_type=MESH))
        for cp in sends:
            cp.start()
        if ns:
            for d in range(1, N_DEV):
                pltpu.make_async_remote_copy(
                    src_ref=small_ref, dst_ref=small_all.at[me ^ d],
                    send_sem=s_send.at[d - 1], recv_sem=s_recv.at[d - 1],
                    device_id=(x, y, c), device_id_type=MESH).wait_recv()
        for cp in sends[:n]:
            cp.wait_recv()
        for cp in sends:
            cp.wait_send()
        if ns:
            local.wait()

    out_shape = [jax.ShapeDtypeStruct((N_CHIPS, p.shape[1] // 2, p.shape[2]), F32) for p in partials]
    scratch = [pltpu.SemaphoreType.DMA((max(n, 1),)), pltpu.SemaphoreType.DMA((max(n, 1),))]
    if ns:
        out_shape.append(jax.ShapeDtypeStruct((N_DEV,) + small.shape, F32))
        scratch += [pltpu.SemaphoreType.DMA((N_DEV - 1,)), pltpu.SemaphoreType.DMA((N_DEV - 1,)),
                    pltpu.SemaphoreType.DMA]
    return pl.pallas_call(
        body, name=name,
        in_specs=[HBM_SPEC] * (n + ns), out_specs=[HBM_SPEC] * (n + ns),
        out_shape=out_shape, scratch_shapes=scratch,
    )(*partials, *([small] if ns else []))


def _chip_sum(partial, from_sibling, c, name, tr=256):
    _, hr, C = from_sibling.shape
    nb = hr // tr

    def body(c_ref, p_ref, s_ref, o_ref):
        o_ref[...] = (p_ref[...] + s_ref[...]).astype(BF16)

    return pl.pallas_call(
        body, name=name,
        grid_spec=pltpu.PrefetchScalarGridSpec(
            num_scalar_prefetch=1, grid=(N_CHIPS, nb),
            in_specs=[pl.BlockSpec((1, tr, C), lambda j, i, c_ref: (j, c_ref[0] * nb + i, 0)),
                      pl.BlockSpec((1, tr, C), lambda j, i, c_ref: (j, i, 0))],
            out_specs=pl.BlockSpec((1, tr, C), lambda j, i, c_ref: (j, i, 0))),
        out_shape=jax.ShapeDtypeStruct(from_sibling.shape, BF16),
        compiler_params=_params(2),
    )(c, partial, from_sibling)


def _reduce_half(partial, from_sibling, received, place, name, tr=256):
    _, hr, C = from_sibling.shape
    nb = hr // tr

    def body(p_ref, mine_ref, sib_ref, r_ref, o_ref):
        acc = mine_ref[0] + sib_ref[0]
        for k in range(3):
            acc = acc + r_ref[k].astype(F32)
        o_ref[...] = acc

    return pl.pallas_call(
        body, name=name,
        grid_spec=pltpu.PrefetchScalarGridSpec(
            num_scalar_prefetch=1, grid=(nb,),
            in_specs=[pl.BlockSpec((1, tr, C), lambda i, p: (p[0], p[1] * nb + i, 0)),
                      pl.BlockSpec((1, tr, C), lambda i, p: (p[0], i, 0)),
                      pl.BlockSpec((3, tr, C), lambda i, p: (0, i, 0))],
            out_specs=pl.BlockSpec((tr, C), lambda i, p: (p[1] * nb + i, 0))),
        out_shape=jax.ShapeDtypeStruct((2 * hr, C), F32),
        compiler_params=_params(1),
    )(place, partial, from_sibling, received)


def _join_halves(fulls, name):
    n = len(fulls)

    def body(*refs):
        outs = refs[n:2 * n]
        send_sems, recv_sems = refs[2 * n:]
        x, y, c = _place()

        def copy(a, half_of, to):
            rows = outs[a].at[_half(outs[a], half_of)]
            return pltpu.make_async_remote_copy(
                src_ref=rows, dst_ref=rows, send_sem=send_sems.at[a], recv_sem=recv_sems.at[a],
                device_id=to, device_id_type=MESH)

        sends = [copy(a, c, (x, y, 1 - c)) for a in range(n)]
        for cp in sends:
            cp.start()
        for a in range(n):
            copy(a, 1 - c, (x, y, c)).wait_recv()
        for cp in sends:
            cp.wait_send()

    return pl.pallas_call(
        body, name=name,
        in_specs=[HBM_SPEC] * n, out_specs=[HBM_SPEC] * n,
        out_shape=[jax.ShapeDtypeStruct(f.shape, F32) for f in fulls],
        input_output_aliases={a: a for a in range(n)},
        scratch_shapes=[pltpu.SemaphoreType.DMA((n,)), pltpu.SemaphoreType.DMA((n,))],
    )(*fulls)


def _adamw_math(w, g, m, v):
    m = ADAM_B1 * m + (1.0 - ADAM_B1) * g
    v = ADAM_B2 * v + (1.0 - ADAM_B2) * (g * g)
    m_hat = m / (1.0 - ADAM_B1 ** ADAM_STEP)
    v_hat = v / (1.0 - ADAM_B2 ** ADAM_STEP)
    delta = -ADAM_LR * (m_hat / (jnp.sqrt(v_hat) + ADAM_EPS) + ADAM_WD * w)
    return delta, m, v


def _adamw(w, g, m, v, name, tr=256):
    R, C = w.shape
    tr = min(tr, R)

    def body(w_ref, g_ref, m_ref, v_ref, d_out, m_out, v_out):
        d_out[...], m_out[...], v_out[...] = _adamw_math(w_ref[...], g_ref[...], m_ref[...], v_ref[...])

    spec = pl.BlockSpec((tr, C), lambda i: (i, 0))
    return pl.pallas_call(
        body, name=name, grid=(R // tr,),
        in_specs=[spec] * 4, out_specs=[spec] * 3,
        out_shape=[jax.ShapeDtypeStruct((R, C), F32)] * 3,
        compiler_params=_params(1),
    )(w, g, m, v)


def _adamw_small(small_all, w, m, v, name):
    def body(s_ref, w_ref, m_ref, v_ref, g_out, d_out, m_out, v_out):
        g = s_ref[0]
        for d in range(1, N_DEV):
            g = g + s_ref[d]
        g_out[...] = g
        d_out[...], m_out[...], v_out[...] = _adamw_math(w_ref[...], g, m_ref[...], v_ref[...])

    vm = pl.BlockSpec(memory_space=pltpu.VMEM)
    return pl.pallas_call(
        body, name=name, in_specs=[vm] * 4, out_specs=[vm] * 4,
        out_shape=[jax.ShapeDtypeStruct(w.shape, F32)] * 4,
    )(small_all, w, m, v)


def _pack_small(norm_g, pool_scale, norm_f, extra_row):
    return jnp.concatenate([norm_g.reshape(2, D_MODEL), pool_scale.reshape(2, D_MODEL),
                            norm_f.reshape(1, D_MODEL), extra_row,
                            jnp.zeros((2, D_MODEL), F32)], axis=0)


def kernel(x, norm_g, pool_w_in, pool_w, pool_scale, pool_w_out, sb_w_in, sb_w_out, norm_f, loss_target, m_norm_g, m_pool_w_in, m_pool_w, m_pool_scale, m_pool_w_out, m_sb_w_in, m_sb_w_out, m_norm_f, v_norm_g, v_pool_w_in, v_pool_w, v_pool_scale, v_pool_w_out, v_sb_w_in, v_sb_w_out, v_norm_f):
    nb, S, _ = x.shape
    T = nb * S
    xt = x.reshape(T, D_MODEL)
    target = loss_target.reshape(T, D_MODEL)
    cx, cy, cc = _place()

    def shard2d(w):
        return w.reshape(-1, w.shape[-1])

    names = ("pool_w_in", "pool_w", "pool_w_out", "sb_w_in", "sb_w_out")
    w_shards = [shard2d(w) for w in (pool_w_in, pool_w, pool_w_out, sb_w_in, sb_w_out)]
    m_shards = [shard2d(w) for w in (m_pool_w_in, m_pool_w, m_pool_w_out, m_sb_w_in, m_sb_w_out)]
    v_shards = [shard2d(w) for w in (v_pool_w_in, v_pool_w, v_pool_w_out, v_sb_w_in, v_sb_w_out)]

    chip = (2 * cx + cy).reshape(1).astype(jnp.int32)
    c_arr = cc.reshape(1).astype(jnp.int32)
    place = jnp.stack([2 * cx + cy, cc]).astype(jnp.int32)
    slots = [_cast_to_slot(w, chip, "cast_" + nm) for w, nm in zip(w_shards, names)]
    g0, g1, gf = norm_g[0:1], norm_g[1:2], norm_f.reshape(1, D_MODEL)

    w_pin, = _allgather_weights(slots[:1], "allgather_pool_in_weights")
    mix_send, mix_recv, mix_slots, token = _exchange_start(slots[1:3], w_pin, _weight_plan, 6,
                                                           "pool_weights_start")
    sb_send, sb_recv, sb_slots, token = _exchange_start(slots[3:], token, _weight_plan, 6,
                                                        "sb_weights_start")

    proj0, u0 = _rms_matmul(xt, g0 + token[0:1, 0:1], w_pin, "pool_in_proj")
    mix_slots = _exchange_wait(mix_slots, mix_send, mix_recv, proj0, _weight_plan, "pool_weights_wait")
    w_g, w_pout = _allgather_weights(mix_slots, "pool_weights_forward", landed=True)
    w_pout = w_pout.reshape(D_INNER, D_MODEL)
    y0, pooled, mixed = _pool_fwd(proj0, w_g, pool_scale, S, "pool_mix")
    sb_slots = _exchange_wait(sb_slots, sb_send, sb_recv, y0, _weight_plan, "sb_weights_wait")
    w_sin, w_sout = _allgather_weights(sb_slots, "sb_weights_forward", landed=True)
    w_sout = w_sout.reshape(D_INNER, D_MODEL)
    h1 = _matmul_residual(y0, w_pout, xt, "pool_out_proj")
    n1 = 2 * QK_WIDTH + 2 * D_INNER
    qkvz, u1 = _rms_matmul(h1, g1, w_sin, "sb_in_proj")
    o, y1, ltot = _attn_fwd(qkvz, S, "sb_attention")
    dh2, d_norm_f, loss_row, dh2_b = _out_proj_loss_head(y1, w_sout, h1, gf, target,
                                                         "sb_out_proj_loss_head")

    def reduce_start(partials, tag, after=None, behind=None):
        n, done = len(partials), None
        after = c_arr if after is None else after
        if behind is None:
            from_sibling = list(_sibling_exchange(partials, None, "grad_sibling_exchange_" + tag))
        else:
            lands = [lax.empty((N_CHIPS, p.shape[1] // 2, p.shape[2]), F32) for p in partials]
            send, recv, bufs, token = _exchange_start(partials + lands, after, _sibling_plan, n,
                                                      "grad_sibling_start_" + tag)
            done = behind(token[0:1, 0:1])
            bufs = _exchange_wait(bufs, send, recv, done[0], _sibling_plan, "grad_sibling_wait_" + tag)
            partials, from_sibling, after = bufs[:n], bufs[n:], c_arr
        sums = [_chip_sum(p, s, c_arr, "grad_chip_sum_%s_%d" % (tag, i))
                for i, (p, s) in enumerate(zip(partials, from_sibling))]
        lands = [lax.empty((3,) + s.shape[1:], BF16) for s in sums]
        send, recv, bufs, token = _exchange_start(sums + lands, after, _chip_sum_plan, 3 * n,
                                                  "grad_chip_exchange_start_" + tag)
        return (partials, from_sibling, send, recv, bufs), token[0:1, 0:1], done

    def reduce_finish(started, after, tag):
        partials, from_sibling, send, recv, bufs = started
        received = _exchange_wait(bufs, send, recv, after, _chip_sum_plan,
                                  "grad_chip_exchange_wait_" + tag)[len(partials):]
        return [_reduce_half(p, s, r, place, "grad_reduce_%s_%d" % (tag, i))
                for i, (p, s, r) in enumerate(zip(partials, from_sibling, received))]

    shard = lambda i, j, t: (j, 0, 0)
    gw_sout = _matmul_tn(y1, dh2_b, D_INNER, D_MODEL, (D_INNER, D_MODEL), (1024, 1024),
                         lambda i, j, t: (i, j), "grad_sb_w_out", bm=1024, bn=1024)
    sout_started, token, (do, dproj1) = reduce_start(
        [gw_sout.reshape(N_CHIPS, -1, D_MODEL)], "sb_out",
        behind=lambda tok: _attn_gate_bwd(dh2_b, w_sout, qkvz, o, tok, "sb_gate_bwd"))
    dproj1 = _attn_bwd(qkvz, do, ltot, dproj1, S, "sb_attention_bwd")
    gw_sin = _matmul_tn(u1, dproj1, D_MODEL, n1, (N_CHIPS, D_MODEL, n1 // 4), (1, D_MODEL, n1 // 4),
                        shard, "grad_sb_w_in", bm=D_MODEL, bn=n1 // 4)
    sin_started, token, (dh1, d_g1, dh1_b) = reduce_start(
        [gw_sin], "sb_in", after=token,
        behind=lambda tok: _matmul_nt_rms_bwd(dproj1, w_sin, h1, g1 + tok, dh2, "sb_in_bwd", True,
                                              tk=w_sin.shape[2]))
    gw_pout = _matmul_tn(y0, dh1_b, D_INNER, D_MODEL, (D_INNER, D_MODEL), (1024, 1024),
                         lambda i, j, t: (i, j), "grad_pool_w_out", bm=1024, bn=1024)
    dmixed, dproj0, d_scale = _pool_gate_bwd(dh1_b, w_pout, proj0, mixed, pool_scale + token,
                                             "pool_gate_bwd")
    gw_g = _matmul_tn(pooled, dmixed, D_INNER, D_INNER, (N_CHIPS, GROUP_DIM, GROUP_DIM),
                      (N_CHIPS, GROUP_DIM // N_CHIPS, GROUP_DIM), lambda i, j, t: (0, i, 0),
                      "grad_pool_w", bm=GROUP_DIM, bn=GROUP_DIM, diagonal_blocks=True)
    mix_started, token, (dproj0,) = reduce_start(
        [gw_g, gw_pout.reshape(N_CHIPS, -1, D_MODEL)], "pool_mix",
        behind=lambda tok: (_pool_bwd(dmixed, w_g, dproj0, tok, S, "pool_bwd"),))
    n0 = 2 * D_INNER
    gw_pin = _matmul_tn(u0, dproj0, D_MODEL, n0, (N_CHIPS, D_MODEL, n0 // 4), (1, D_MODEL, n0 // 4),
                        shard, "grad_pool_w_in", bm=D_MODEL, bn=n0 // 4)
    pin_started, token, _ = reduce_start([gw_pin], "pool_in", after=token)
    dx, d_g0 = _matmul_nt_rms_bwd(dproj0, w_pin, xt, g0 + token, dh1, "pool_in_bwd", False,
                                  tk=w_pin.shape[2])

    small = _pack_small(jnp.concatenate([d_g0, d_g1], axis=0), d_scale, d_norm_f,
                        jnp.broadcast_to(loss_row[:, :1], (1, D_MODEL)))
    small_all, = _sibling_exchange([], small, "small_sums_exchange")
    grads = _join_halves(reduce_finish(pin_started, dx, "pool_in")
                         + reduce_finish(mix_started, dx, "pool_mix")
                         + reduce_finish(sin_started, dx, "sb_in")
                         + reduce_finish(sout_started, dx, "sb_out"), "grad_join_halves")

    deltas, new_m, new_v = [], [], []
    for w, g, m, v, nm in zip(w_shards, grads, m_shards, v_shards, names):
        d, mm, vv = _adamw(w, g, m, v, "adamw_" + nm)
        deltas.append(d)
        new_m.append(mm)
        new_v.append(vv)

    zero_row = jnp.zeros((1, D_MODEL), F32)
    g_small, d_small, m_small, v_small = _adamw_small(
        small_all, _pack_small(norm_g, pool_scale, norm_f, zero_row),
        _pack_small(m_norm_g, m_pool_scale, m_norm_f, zero_row),
        _pack_small(v_norm_g, v_pool_scale, v_norm_f, zero_row + 1.0), "adamw_small")
    loss = g_small[5, 0]

    def unpack_small(a):
        return a[0:2], a[2:4].reshape(1, D_INNER), a[4]

    def assemble(big, small3):
        ng, ps, nf = small3
        return [ng, big[0].reshape(pool_w_in.shape), big[1].reshape(pool_w.shape), ps,
                big[2].reshape(pool_w_out.shape), big[3].reshape(sb_w_in.shape),
                big[4].reshape(sb_w_out.shape), nf]

    return (loss, dx.reshape(x.shape),
            *assemble(grads, unpack_small(g_small)),
            *assemble(deltas, unpack_small(d_small)),
            *assemble(new_m, unpack_small(m_small)),
            *assemble(new_v, unpack_small(v_small)))
```

```python
import jax
import jax.numpy as jnp
from jax import lax
from jax.experimental import pallas as pl
from jax.experimental.pallas import tpu as pltpu

F32 = jnp.float32
BF16 = jnp.bfloat16
MESH = pl.DeviceIdType.MESH

D_MODEL = 1024
D_INNER = 2048
N_GROUPS = 4
GROUP_DIM = 512
HEAD_PAIR_QK = 128
HEAD_V = 128
QK_WIDTH = 1024
RMS_EPS = 1e-6
HALO = 16
N_CHIPS = 4
N_DEV = 8

ADAM_LR = 0.001
ADAM_B1 = 0.9
ADAM_B2 = 0.999
ADAM_EPS = 1e-08
ADAM_WD = 0.01
ADAM_STEP = 10

VMEM_LIMIT = 56 * 1024 * 1024

HBM_SPEC = pl.BlockSpec(memory_space=pltpu.HBM)


def _params(n_axes):
    return pltpu.CompilerParams(dimension_semantics=("arbitrary",) * n_axes,
                                vmem_limit_bytes=VMEM_LIMIT)


def _dot(a, b):
    return jnp.dot(a, b, preferred_element_type=F32)


def _dot_nt(a, b):
    return lax.dot_general(a, b, (((1,), (1,)), ((), ())), preferred_element_type=F32)


def _dot_tn(a, b):
    return lax.dot_general(a, b, (((0,), (0,)), ((), ())), preferred_element_type=F32)


def _sigmoid(z):
    return 1.0 / (1.0 + jnp.exp(-z))


def _row_blocks(tm, rows=256):
    return [slice(r, r + rows) for r in range(0, tm, rows)]


def _rms_matmul(h, g_row, w4, name, tm=1024):
    T = h.shape[0]
    n_shards, _, tn = w4.shape
    nm = T // tm

    def body(h_ref, g_ref, w_ref, o_ref, u_out, u_all):
        n, m = pl.program_id(0), pl.program_id(1)
        rows = pl.ds(pl.multiple_of(m * tm, tm), tm)

        @pl.when(n == 0)
        def _():
            x = h_ref[...]
            inv = lax.rsqrt(jnp.mean(x * x, axis=-1, keepdims=True) + RMS_EPS)
            u = (x * inv * g_ref[...]).astype(BF16)
            u_all[rows, :] = u
            u_out[...] = u

        o_ref[...] = _dot(u_all[rows, :], w_ref[0]).astype(BF16)

    return pl.pallas_call(
        body, name=name, grid=(n_shards, nm),
        in_specs=[pl.BlockSpec((tm, D_MODEL), lambda n, m: (jnp.where(n == 0, m, nm - 1), 0)),
                  pl.BlockSpec((1, D_MODEL), lambda n, m: (0, 0)),
                  pl.BlockSpec((1, D_MODEL, tn), lambda n, m: (n, 0, 0))],
        out_specs=[pl.BlockSpec((tm, tn), lambda n, m: (m, n)),
                   pl.BlockSpec((tm, D_MODEL), lambda n, m: (jnp.where(n == 0, m, nm - 1), 0))],
        out_shape=[jax.ShapeDtypeStruct((T, n_shards * tn), BF16),
                   jax.ShapeDtypeStruct((T, D_MODEL), BF16)],
        scratch_shapes=[pltpu.VMEM((T, D_MODEL), BF16)],
        compiler_params=_params(2),
    )(h, g_row, w4)


def _matmul_residual(a, w, res, name, tm=1024, tn=1024):
    T, K = a.shape
    N = w.shape[1]

    def body(a_ref, w_ref, r_ref, o_ref):
        o_ref[...] = r_ref[...] + _dot(a_ref[...], w_ref[...])

    return pl.pallas_call(
        body, name=name, grid=(T // tm, N // tn),
        in_specs=[pl.BlockSpec((tm, K), lambda m, n: (m, 0)),
                  pl.BlockSpec((K, tn), lambda m, n: (0, n)),
                  pl.BlockSpec((tm, tn), lambda m, n: (m, n))],
        out_specs=pl.BlockSpec((tm, tn), lambda m, n: (m, n)),
        out_shape=jax.ShapeDtypeStruct((T, N), F32),
        compiler_params=_params(2),
    )(a, w, res)


def _matmul_tn(a, b, a_cols, b_cols, out_shape, out_block, out_map, name, bm, bn, tk=2048,
               diagonal_blocks=False):
    T = a.shape[0]
    tk = min(tk, T)

    def body(a_ref, b_ref, o_ref):
        @pl.when(pl.program_id(2) == 0)
        def _():
            o_ref[...] = jnp.zeros_like(o_ref)

        part = _dot_tn(a_ref[...].astype(BF16), b_ref[...].astype(BF16))
        o_ref[...] += part.reshape(o_ref.shape)

    b_map = (lambda i, j, t: (t, i)) if diagonal_blocks else (lambda i, j, t: (t, j))
    return pl.pallas_call(
        body, name=name, grid=(a_cols // bm, 1 if diagonal_blocks else b_cols // bn, T // tk),
        in_specs=[pl.BlockSpec((tk, bm), lambda i, j, t: (t, i)),
                  pl.BlockSpec((tk, bn), b_map)],
        out_specs=pl.BlockSpec(out_block, out_map),
        out_shape=jax.ShapeDtypeStruct(out_shape, F32),
        compiler_params=_params(3),
    )(a, b)


def _matmul_nt_rms_bwd(dproj, w4, h, g_row, dres, name, with_bf16, tm=1024, tk=512):
    T, cols = dproj.shape
    per_shard = w4.shape[2] // tk
    nk = cols // tk

    def body(dp_ref, w_ref, h_ref, g_ref, r_ref, dx_ref, dg_ref, *rest):
        acc = rest[-1]
        m, k = pl.program_id(0), pl.program_id(1)

        @pl.when(k == 0)
        def _():
            acc[...] = jnp.zeros_like(acc)

        @pl.when((k == 0) & (m == 0))
        def _():
            dg_ref[...] = jnp.zeros_like(dg_ref)

        acc[...] += _dot_nt(dp_ref[...], w_ref[0])

        @pl.when(k == nk - 1)
        def _():
            du = acc[...]
            x = h_ref[...]
            inv = lax.rsqrt(jnp.mean(x * x, axis=-1, keepdims=True) + RMS_EPS)
            xhat = x * inv
            dg_ref[...] += jnp.sum(du * xhat, axis=0, keepdims=True)
            dxh = du * g_ref[...]
            proj = jnp.mean(dxh * xhat, axis=-1, keepdims=True)
            dx = r_ref[...] + inv * (dxh - xhat * proj)
            dx_ref[...] = dx
            if with_bf16:
                rest[0][...] = dx.astype(BF16)

    rows = pl.BlockSpec((tm, D_MODEL), lambda m, k: (m, 0))
    return pl.pallas_call(
        body, name=name, grid=(T // tm, nk),
        in_specs=[pl.BlockSpec((tm, tk), lambda m, k: (m, k)),
                  pl.BlockSpec((1, D_MODEL, tk), lambda m, k: (k // per_shard, 0, k % per_shard)),
                  rows, pl.BlockSpec((1, D_MODEL), lambda m, k: (0, 0)), rows],
        out_specs=[rows, pl.BlockSpec((1, D_MODEL), lambda m, k: (0, 0))] + [rows] * with_bf16,
        out_shape=[jax.ShapeDtypeStruct((T, D_MODEL), F32), jax.ShapeDtypeStruct((1, D_MODEL), F32)]
        + [jax.ShapeDtypeStruct((T, D_MODEL), BF16)] * with_bf16,
        scratch_shapes=[pltpu.VMEM((tm, D_MODEL), F32)],
        compiler_params=_params(2),
    )(dproj, w4, h, g_row, dres)


def _window_of(g):
    return jnp.left_shift(2, g)


def _select_stage(g, stages):
    res = stages[0]
    for i in range(1, len(stages)):
        res = jnp.where(g >= i, stages[i], res)
    return res


def _pool_fwd(proj0, wg4, scale_row, S, name, tm=1024):
    T = proj0.shape[0]
    tm = min(tm, S)
    blocks_per_seq = S // tm
    hb = tm // HALO

    def body(x_ref, halo_ref, z_ref, w_ref, s_ref, y_ref, p_ref, mix_ref):
        m, g = pl.program_id(0), pl.program_id(1)
        first = (m % blocks_per_seq) == 0
        halo = jnp.where(first, 0.0, halo_ref[...].astype(F32))
        x = x_ref[...].astype(F32)
        ext = jnp.concatenate([halo, x], axis=0)
        stages = []
        cur = ext
        for sh in (1, 2, 4, 8):
            cur = cur + pltpu.roll(cur, sh, 0)
            stages.append(cur[HALO:, :])
        win_sum = _select_stage(g, stages)
        pos = (m % blocks_per_seq) * tm + lax.broadcasted_iota(jnp.int32, (tm, 1), 0)
        count = jnp.minimum(pos + 1, _window_of(g)).astype(F32)
        p_ref[...] = (win_sum / count - x).astype(BF16)
        w = w_ref[...].reshape(GROUP_DIM, GROUP_DIM)
        for rows in _row_blocks(tm):
            mixed = _dot(p_ref[rows, :], w)
            z = z_ref[rows, :].astype(F32)
            y_ref[rows, :] = (mixed * s_ref[...] * (z * _sigmoid(z))).astype(BF16)
            mix_ref[rows, :] = mixed.astype(BF16)

    blk = lambda m, g: (m, g)
    return pl.pallas_call(
        body, name=name, grid=(T // tm, N_GROUPS),
        in_specs=[pl.BlockSpec((tm, GROUP_DIM), blk),
                  pl.BlockSpec((HALO, GROUP_DIM), lambda m, g: (jnp.maximum(m * hb - 1, 0), g)),
                  pl.BlockSpec((tm, GROUP_DIM), lambda m, g: (m, N_GROUPS + g)),
                  pl.BlockSpec((N_CHIPS, GROUP_DIM // N_CHIPS, GROUP_DIM), lambda m, g: (0, g, 0)),
                  pl.BlockSpec((1, GROUP_DIM), lambda m, g: (0, g))],
        out_specs=[pl.BlockSpec((tm, GROUP_DIM), blk)] * 3,
        out_shape=[jax.ShapeDtypeStruct((T, D_INNER), BF16)] * 3,
        compiler_params=_params(2),
    )(proj0, proj0, proj0, wg4, scale_row)


def _pool_gate_bwd(dh, w_out, proj0, mixed, scale_row, name, tm=1024, tn=512):
    T = dh.shape[0]
    gate_b0 = D_INNER // tn

    def body(dh_ref, w_ref, z_ref, mix_ref, s_ref, dm_ref, dz_ref, ds_ref):
        m, n = pl.program_id(0), pl.program_id(1)

        @pl.when((m == 0) & (n == 0))
        def _():
            ds_ref[...] = jnp.zeros_like(ds_ref)

        cols = pl.ds(pl.multiple_of(n * tn, tn), tn)
        s = s_ref[...]
        ds = ds_ref[:, cols]
        for rows in _row_blocks(tm):
            dy = _dot_nt(dh_ref[rows, :], w_ref[...])
            z = z_ref[rows, :].astype(F32)
            sig = _sigmoid(z)
            silu = z * sig
            mixed = mix_ref[rows, :].astype(F32)
            dm_ref[rows, :] = (dy * s * silu).astype(BF16)
            dz_ref[rows, :] = (dy * mixed * s * (sig * (1.0 + z * (1.0 - sig)))).astype(BF16)
            ds = ds + jnp.sum(dy * mixed * silu, axis=0, keepdims=True)
        ds_ref[:, cols] = ds

    return pl.pallas_call(
        body, name=name, grid=(T // tm, D_INNER // tn),
        in_specs=[pl.BlockSpec((tm, D_MODEL), lambda m, n: (m, 0)),
                  pl.BlockSpec((tn, D_MODEL), lambda m, n: (n, 0)),
                  pl.BlockSpec((tm, tn), lambda m, n: (m, gate_b0 + n)),
                  pl.BlockSpec((tm, tn), lambda m, n: (m, n)),
                  pl.BlockSpec((1, tn), lambda m, n: (0, n))],
        out_specs=[pl.BlockSpec((tm, tn), lambda m, n: (m, n)),
                   pl.BlockSpec((tm, tn), lambda m, n: (m, gate_b0 + n)),
                   pl.BlockSpec((1, D_INNER), lambda m, n: (0, 0))],
        out_shape=[jax.ShapeDtypeStruct((T, D_INNER), BF16),
                   jax.ShapeDtypeStruct((T, 2 * D_INNER), BF16),
                   jax.ShapeDtypeStruct((1, D_INNER), F32)],
        compiler_params=_params(2),
    )(dh, w_out, proj0, mixed, scale_row)


def _pool_bwd(dmixed, wg4, dproj0, after, S, name, tm=1024):
    T = dmixed.shape[0]
    tm = min(tm, S)
    blocks_per_seq = S // tm
    hb = tm // HALO
    n_halo_blocks = T // HALO

    def body(dm_ref, halo_ref, w_ref, _, __, o_ref):
        m, g = pl.program_id(0), pl.program_id(1)
        ext = jnp.concatenate([dm_ref[...], halo_ref[...]], axis=0)
        dp = _dot_nt(ext, w_ref[...].reshape(GROUP_DIM, GROUP_DIM))
        pos = (m % blocks_per_seq) * tm + lax.broadcasted_iota(jnp.int32, (tm + HALO, 1), 0)
        count = jnp.minimum(pos + 1, _window_of(g)).astype(F32)
        c = jnp.where(pos < S, dp / count, 0.0)
        n = tm + HALO
        stages = []
        cur = c
        for sh in (1, 2, 4, 8):
            cur = cur + pltpu.roll(cur, n - sh, 0)
            stages.append(cur[:tm, :])
        o_ref[...] = (_select_stage(g, stages) - dp[:tm, :]).astype(BF16)

    blk = lambda m, g: (m, g)
    return pl.pallas_call(
        body, name=name, grid=(T // tm, N_GROUPS),
        in_specs=[pl.BlockSpec((tm, GROUP_DIM), blk),
                  pl.BlockSpec((HALO, GROUP_DIM),
                               lambda m, g: (jnp.minimum((m + 1) * hb, n_halo_blocks - 1), g)),
                  pl.BlockSpec((N_CHIPS, GROUP_DIM // N_CHIPS, GROUP_DIM), lambda m, g: (0, g, 0)),
                  HBM_SPEC, ANY_SPEC],
        out_specs=pl.BlockSpec((tm, GROUP_DIM), blk),
        out_shape=jax.ShapeDtypeStruct(dproj0.shape, dproj0.dtype),
        input_output_aliases={3: 0},
        compiler_params=_params(2),
    )(dmixed, dmixed, wg4, dproj0, after)


TQ = 256


def _split_dot(x, m):
    hi = x.astype(BF16)
    lo = (x - hi.astype(F32)).astype(BF16)
    return _dot(hi, m) + _dot(lo, m)


NEG_LOG2E = -1.4426950408889634


def _log_terms(z):
    soft = jnp.log(1.0 + jnp.exp2(jnp.abs(z) * NEG_LOG2E))
    log_beta = jnp.minimum(z, 0.0) - soft
    return log_beta, log_beta - z


N_HEADS = 16
FWD_HEADS = BWD_HEADS = 4


def _masked_heads(x, heads):
    lane = lax.broadcasted_iota(jnp.int32, (1, HEAD_PAIR_QK), 1)
    out = []
    for hh in range(heads):
        slab = x[:, (hh // 2) * HEAD_PAIR_QK:(hh // 2 + 1) * HEAD_PAIR_QK]
        out.append(jnp.where((lane // 64) == hh % 2, slab, jnp.zeros_like(slab)))
    return out


def _attn_fwd(qkvz, S, name):
    T = qkvz.shape[0]
    nq = S // TQ
    HEADS, QK_W, V_W = FWD_HEADS, FWD_HEADS * 64, FWD_HEADS * HEAD_V
    k_b0 = QK_WIDTH // QK_W
    v_b0 = 2 * QK_WIDTH // V_W
    z_b0 = (2 * QK_WIDTH + D_INNER) // V_W
    hs = range(HEADS)

    def body(q_ref, k_ref, v_ref, z_ref, o_ref, y_ref, lt_ref):
        row = lax.broadcasted_iota(jnp.int32, (TQ, TQ), 0)
        col = lax.broadcasted_iota(jnp.int32, (TQ, TQ), 1)
        causal = col < row
        later_in_block = (row > col).astype(BF16)
        lax.fori_loop(0, nq, lambda qi, _: q_block(qi, causal, later_in_block,
                                                   q_ref, k_ref, v_ref, z_ref, o_ref, y_ref, lt_ref), 0)

    def q_block(qi, causal, later_in_block, q_ref, k_ref, v_ref, z_ref, o_ref, y_ref, lt_ref):
        rows = pl.ds(pl.multiple_of(qi * TQ, TQ), TQ)
        qms = [qm * 0.125 for qm in _masked_heads(q_ref[rows, :], HEADS)]

        def step(j, carry, diagonal):
            koff = pl.multiple_of(j * TQ, TQ)
            kbs = [k_ref[pl.ds(koff, TQ), p * HEAD_PAIR_QK:(p + 1) * HEAD_PAIR_QK]
                   for p in range(HEADS // 2)]
            run, acc = [carry[2 * hh] for hh in hs], [carry[2 * hh + 1] for hh in hs]
            z = [_dot_nt(qms[hh], kbs[hh // 2]) for hh in hs]
            terms = [_log_terms(z[hh]) for hh in hs]
            log_om = [jnp.where(causal, t[1], 0.0) if diagonal else t[1] for t in terms]
            later = [_split_dot(log_om[hh], later_in_block) for hh in hs]
            a = [jnp.exp(terms[hh][0] + (run[hh] + later[hh])) for hh in hs]
            if diagonal:
                a = [jnp.where(causal, a[hh], 0.0) for hh in hs]
            out = []
            for hh in hs:
                vb = v_ref[pl.ds(koff, TQ), hh * HEAD_V:(hh + 1) * HEAD_V]
                out += [run[hh] + jnp.sum(log_om[hh], axis=1, keepdims=True),
                        acc[hh] + _dot(a[hh].astype(BF16), vb)]
            return tuple(out)

        zero = (jnp.zeros((TQ, 1), F32), jnp.zeros((TQ, HEAD_V), F32))
        carry = step(qi, zero * HEADS, True)
        carry = lax.fori_loop(0, qi, lambda i, c: step(qi - 1 - i, c, False), carry)
        for hh in hs:
            sl = slice(hh * HEAD_V, (hh + 1) * HEAD_V)
            acc = carry[2 * hh + 1]
            z = z_ref[rows, sl].astype(F32)
            o_ref[rows, sl] = acc.astype(BF16)
            y_ref[rows, sl] = (acc * (z * _sigmoid(z))).astype(BF16)
            lt_ref[rows, hh:hh + 1] = carry[2 * hh]
        return 0

    blk = lambda b, p: (b, p)
    return pl.pallas_call(
        body, name=name, grid=(T // S, N_HEADS // HEADS),
        in_specs=[pl.BlockSpec((S, QK_W), blk),
                  pl.BlockSpec((S, QK_W), lambda b, p: (b, k_b0 + p)),
                  pl.BlockSpec((S, V_W), lambda b, p: (b, v_b0 + p)),
                  pl.BlockSpec((S, V_W), lambda b, p: (b, z_b0 + p))],
        out_specs=[pl.BlockSpec((S, V_W), blk),
                   pl.BlockSpec((S, V_W), blk),
                   pl.BlockSpec((None, S, HEADS), lambda b, p: (p, b, 0))],
        out_shape=[jax.ShapeDtypeStruct((T, D_INNER), BF16),
                   jax.ShapeDtypeStruct((T, D_INNER), BF16),
                   jax.ShapeDtypeStruct((N_HEADS // HEADS, T, HEADS), F32)],
        compiler_params=_params(2),
    )(qkvz, qkvz, qkvz, qkvz)


def _attn_gate_bwd(dh, w_out, qkvz, o, name, tm=1024, tn=512):
    T = dh.shape[0]
    gate_b0 = (2 * QK_WIDTH + D_INNER) // tn

    def body(dh_ref, w_ref, z_ref, o_ref, do_ref, dz_ref):
        for rows in _row_blocks(tm):
            dy = _dot_nt(dh_ref[rows, :], w_ref[...])
            z = z_ref[rows, :].astype(F32)
            sig = _sigmoid(z)
            do_ref[rows, :] = (dy * (z * sig)).astype(BF16)
            dz_ref[rows, :] = (dy * o_ref[rows, :].astype(F32)
                               * (sig * (1.0 + z * (1.0 - sig)))).astype(BF16)

    return pl.pallas_call(
        body, name=name, grid=(T // tm, D_INNER // tn),
        in_specs=[pl.BlockSpec((tm, D_MODEL), lambda m, n: (m, 0)),
                  pl.BlockSpec((tn, D_MODEL), lambda m, n: (n, 0)),
                  pl.BlockSpec((tm, tn), lambda m, n: (m, gate_b0 + n)),
                  pl.BlockSpec((tm, tn), lambda m, n: (m, n))],
        out_specs=[pl.BlockSpec((tm, tn), lambda m, n: (m, n)),
                   pl.BlockSpec((tm, tn), lambda m, n: (m, gate_b0 + n))],
        out_shape=[jax.ShapeDtypeStruct((T, D_INNER), BF16),
                   jax.ShapeDtypeStruct((T, 2 * QK_WIDTH + 2 * D_INNER), BF16)],
        compiler_params=_params(2),
    )(dh, w_out, qkvz, o)


def _attn_bwd(qkv, do, ltot, dproj1, S, name):
    T = qkv.shape[0]
    nq = S // TQ
    HEADS, QK_W, V_W = BWD_HEADS, BWD_HEADS * 64, BWD_HEADS * HEAD_V
    k_b0 = QK_WIDTH // QK_W
    v_b0 = 2 * QK_WIDTH // V_W
    hs = range(HEADS)
    pairs = range(HEADS // 2)
    n_groups = N_HEADS // HEADS
    n_steps = (T // S) * n_groups

    def body(q_ref, k_ref, v_ref, do_ref, lt_ref, _, out_ref, dq_s, dk_s, dv_s, dkb_s, dvb_s, sems):
        b, p = pl.program_id(0), pl.program_id(1)
        row = lax.broadcasted_iota(jnp.int32, (TQ, TQ), 0)
        col = lax.broadcasted_iota(jnp.int32, (TQ, TQ), 1)
        causal = col < row
        upto = (row <= col).astype(BF16)
        before = (row < col).astype(BF16)
        dk_s[...] = jnp.zeros_like(dk_s)
        dv_s[...] = jnp.zeros_like(dv_s)
        rows = pl.ds(pl.multiple_of(b * S, TQ), S)
        copies = [
            pltpu.make_async_copy(
                dq_s, out_ref.at[rows, pl.ds(pl.multiple_of(p * QK_W, 128), QK_W)], sems.at[0]),
            pltpu.make_async_copy(
                dkb_s, out_ref.at[rows, pl.ds(pl.multiple_of(QK_WIDTH + p * QK_W, 128), QK_W)],
                sems.at[1]),
            pltpu.make_async_copy(
                dvb_s, out_ref.at[rows, pl.ds(pl.multiple_of(2 * QK_WIDTH + p * V_W, 128), V_W)],
                sems.at[2]),
        ]
        step = b * n_groups + p

        @pl.when(step > 0)
        def _():
            for cp in copies:
                cp.wait()

        def q_block(qi, _):
            qoff = pl.multiple_of(qi * TQ, TQ)
            qms = [qm * 0.125 for qm in _masked_heads(q_ref[pl.ds(qoff, TQ), :], HEADS)]
            vsl = [slice(hh * HEAD_V, (hh + 1) * HEAD_V) for hh in hs]
            psl = [slice(pp * HEAD_PAIR_QK, (pp + 1) * HEAD_PAIR_QK) for pp in pairs]
            do_h = [do_ref[pl.ds(qoff, TQ), sl] for sl in vsl]
            total = [lt_ref[pl.ds(qoff, TQ), hh:hh + 1] for hh in hs]

            def k_block(j, carry, diagonal):
                koff = pl.multiple_of(j * TQ, TQ)
                kms = _masked_heads(k_ref[pl.ds(koff, TQ), :], HEADS)
                g_before = [carry[2 * hh] for hh in hs]
                lom_before = [carry[2 * hh + 1] for hh in hs]
                z = [_dot_nt(qms[hh], kms[hh]) for hh in hs]
                da = [_dot_nt(do_h[hh], v_ref[pl.ds(koff, TQ), vsl[hh]]) for hh in hs]
                terms = [_log_terms(z[hh]) for hh in hs]
                log_om = [jnp.where(causal, t[1], 0.0) if diagonal else t[1] for t in terms]
                prefix = [_split_dot(log_om[hh], upto) for hh in hs]
                a = [jnp.exp(terms[hh][0] + ((total[hh] - lom_before[hh]) - prefix[hh])) for hh in hs]
                if diagonal:
                    a = [jnp.where(causal, a[hh], 0.0) for hh in hs]
                g = [a[hh] * da[hh] for hh in hs]
                g_prefix = [_dot(g[hh].astype(BF16), before) for hh in hs]
                out, dzs = [], []
                for hh in hs:
                    beta = jnp.exp(terms[hh][0])
                    g_excl = (g_before[hh] + g_prefix[hh]) * beta
                    if diagonal:
                        g_excl = jnp.where(causal, g_excl, 0.0)
                    dzs.append((g[hh] * (1.0 - beta) - g_excl).astype(BF16))
                    out += [g_before[hh] + jnp.sum(g[hh], axis=1, keepdims=True),
                            lom_before[hh] + jnp.sum(log_om[hh], axis=1, keepdims=True)]
                for hh in hs:
                    dv_s[pl.ds(koff, TQ), vsl[hh]] += _dot_tn(a[hh].astype(BF16), do_h[hh])
                dq = []
                for pp in pairs:
                    pair = slice(2 * pp, 2 * pp + 2)
                    dq.append(carry[2 * HEADS + pp] + _dot(jnp.concatenate(dzs[pair], axis=1),
                                                           jnp.concatenate(kms[pair], axis=0)))
                    dk_s[pl.ds(koff, TQ), psl[pp]] += _dot_tn(jnp.concatenate(dzs[pair], axis=0),
                                                              jnp.concatenate(qms[pair], axis=0))
                return tuple(out) + tuple(dq)

            zero = jnp.zeros((TQ, 1), F32)
            carry = (zero,) * (2 * HEADS) + (jnp.zeros((TQ, HEAD_PAIR_QK), F32),) * (HEADS // 2)
            carry = lax.fori_loop(0, qi, lambda j, c: k_block(j, c, False), carry)
            carry = k_block(qi, carry, True)
            for pp in pairs:
                dq_s[pl.ds(qoff, TQ), psl[pp]] = (carry[2 * HEADS + pp] * 0.125).astype(BF16)
            return 0

        lax.fori_loop(0, nq, q_block, 0)
        dkb_s[...] = dk_s[...].astype(BF16)
        dvb_s[...] = dv_s[...].astype(BF16)
        for cp in copies:
            cp.start()

        @pl.when(step == n_steps - 1)
        def _():
            for cp in copies:
                cp.wait()

    return pl.pallas_call(
        body, name=name, grid=(T // S, N_HEADS // HEADS),
        in_specs=[pl.BlockSpec((S, QK_W), lambda b, p: (b, p)),
                  pl.BlockSpec((S, QK_W), lambda b, p: (b, k_b0 + p)),
                  pl.BlockSpec((S, V_W), lambda b, p: (b, v_b0 + p)),
                  pl.BlockSpec((S, V_W), lambda b, p: (b, p)),
                  pl.BlockSpec((None, S, HEADS), lambda b, p: (p, b, 0)),
                  HBM_SPEC],
        out_specs=HBM_SPEC,
        out_shape=jax.ShapeDtypeStruct(dproj1.shape, dproj1.dtype),
        input_output_aliases={5: 0},
        scratch_shapes=[pltpu.VMEM((S, QK_W), BF16),
                        pltpu.VMEM((S, QK_W), F32),
                        pltpu.VMEM((S, V_W), F32),
                        pltpu.VMEM((S, QK_W), BF16),
                        pltpu.VMEM((S, V_W), BF16),
                        pltpu.SemaphoreType.DMA((3,))],
        compiler_params=_params(2),
    )(qkv, qkv, qkv, do, ltot, dproj1)


def _out_proj_loss_head(a, w, res, g_row, target, name, tm=512):
    T, K = a.shape

    def body(a_ref, w_ref, r_ref, g_ref, t_ref, dh_ref, dg_ref, loss_ref, dhb_ref):
        @pl.when(pl.program_id(0) == 0)
        def _():
            dg_ref[...] = jnp.zeros_like(dg_ref)
            loss_ref[...] = jnp.zeros_like(loss_ref)

        gain = g_ref[...]
        dg, loss = dg_ref[...], loss_ref[...]
        for rows in _row_blocks(tm):
            x = r_ref[rows, :] + _dot(a_ref[rows, :], w_ref[...])
            inv = lax.rsqrt(jnp.mean(x * x, axis=-1, keepdims=True) + RMS_EPS)
            xhat = x * inv
            err = xhat * gain - t_ref[rows, :]
            per_token = jnp.mean(err * err, axis=-1, keepdims=True)
            loss = loss + 0.5 * jnp.sum(per_token, axis=0, keepdims=True)
            dy = err * (1.0 / D_MODEL)
            dg = dg + jnp.sum(dy * xhat, axis=0, keepdims=True)
            dxh = dy * gain
            proj = jnp.mean(dxh * xhat, axis=-1, keepdims=True)
            dh = inv * (dxh - xhat * proj)
            dh_ref[rows, :] = dh
            dhb_ref[rows, :] = dh.astype(BF16)
        dg_ref[...] = dg
        loss_ref[...] = loss

    return pl.pallas_call(
        body, name=name, grid=(T // tm,),
        in_specs=[pl.BlockSpec((tm, K), lambda m: (m, 0)),
                  pl.BlockSpec((K, D_MODEL), lambda m: (0, 0)),
                  pl.BlockSpec((tm, D_MODEL), lambda m: (m, 0)),
                  pl.BlockSpec((1, D_MODEL), lambda m: (0, 0)),
                  pl.BlockSpec((tm, D_MODEL), lambda m: (m, 0))],
        out_specs=[pl.BlockSpec((tm, D_MODEL), lambda m: (m, 0)),
                   pl.BlockSpec((1, D_MODEL), lambda m: (0, 0)),
                   pl.BlockSpec((1, 128), lambda m: (0, 0)),
                   pl.BlockSpec((tm, D_MODEL), lambda m: (m, 0))],
        out_shape=[jax.ShapeDtypeStruct((T, D_MODEL), F32),
                   jax.ShapeDtypeStruct((1, D_MODEL), F32),
                   jax.ShapeDtypeStruct((1, 128), F32),
                   jax.ShapeDtypeStruct((T, D_MODEL), BF16)],
        compiler_params=_params(1),
    )(a, w, res, g_row, target)


def _place():
    return lax.axis_index("x"), lax.axis_index("y"), lax.axis_index("c")


def _other_chips(x, y):
    return [(1 - x, y), (x, 1 - y), (1 - x, 1 - y)]


def _half(ref, c):
    hr = ref.shape[-2] // 2
    return pl.ds(pl.multiple_of(c * hr, 8), hr)


def _cast_to_slot(shard, chip, name, tr=256):
    R, C = shard.shape

    def body(chip_ref, w_ref, o_ref):
        o_ref[0] = w_ref[...].astype(BF16)

    return pl.pallas_call(
        body, name=name,
        grid_spec=pltpu.PrefetchScalarGridSpec(
            num_scalar_prefetch=1, grid=(R // tr,),
            in_specs=[pl.BlockSpec((tr, C), lambda i, chip_ref: (i, 0))],
            out_specs=pl.BlockSpec((1, tr, C), lambda i, chip_ref: (chip_ref[0], i, 0))),
        out_shape=jax.ShapeDtypeStruct((N_CHIPS, R, C), BF16),
        compiler_params=_params(1),
    )(chip, shard)


def _weight_plan(bufs):
    x, y, c = _place()
    plan = []
    for buf in bufs:
        mine = buf.at[2 * x + y, _half(buf, c)]
        for ox, oy in _other_chips(x, y):
            plan.append((mine, mine, (ox, oy, c), buf.at[2 * ox + oy, _half(buf, c)]))
    return plan


def _chip_sum_plan(bufs):
    x, y, c = _place()
    n = len(bufs) // 2
    plan = []
    for sums, land in zip(bufs[:n], bufs[n:]):
        for k, (ox, oy) in enumerate(_other_chips(x, y)):
            plan.append((sums.at[2 * ox + oy], land.at[k], (ox, oy, c), land.at[k]))
    return plan


def _sibling_plan(bufs):
    x, y, c = _place()
    n = len(bufs) // 2
    return [(p.at[:, _half(p, 1 - c)], land, (x, y, 1 - c), land)
            for p, land in zip(bufs[:n], bufs[n:])]


SEM_SPEC = pl.BlockSpec(memory_space=pltpu.SEMAPHORE)
ANY_SPEC = pl.BlockSpec(memory_space=pl.ANY)
DATAFLOW = pltpu.SideEffectType.DATAFLOW_SIDE_EFFECTING


def _in_hbm(a):
    return pltpu.with_memory_space_constraint(a, pltpu.HBM)


def _exchange_start(bufs, after, plan, n_copies, name):
    nb = len(bufs)

    def body(*refs):
        send_sems, recv_sems = refs[nb + 1], refs[nb + 2]
        for i, (src, dst, dev, _) in enumerate(plan(refs[:nb])):
            pltpu.make_async_remote_copy(
                src_ref=src, dst_ref=dst, send_sem=send_sems.at[i], recv_sem=recv_sems.at[i],
                device_id=dev, device_id_type=MESH).start()
        token = refs[-1]
        token[...] = jnp.zeros_like(token)

    res = pl.pallas_call(
        body, name=name,
        in_specs=[HBM_SPEC] * nb + [ANY_SPEC],
        out_specs=[SEM_SPEC, SEM_SPEC] + [HBM_SPEC] * nb + [pl.BlockSpec(memory_space=pltpu.VMEM)],
        out_shape=[pltpu.SemaphoreType.DMA((n_copies,)), pltpu.SemaphoreType.DMA((n_copies,))]
        + [pltpu.HBM(b.shape, b.dtype) for b in bufs] + [jax.ShapeDtypeStruct((8, 128), F32)],
        input_output_aliases={i: 2 + i for i in range(nb)},
        compiler_params=pltpu.CompilerParams(has_side_effects=DATAFLOW),
    )(*[_in_hbm(b) for b in bufs], after)
    return res[0], res[1], list(res[2:2 + nb]), res[-1]


def _exchange_wait(bufs, send_sems, recv_sems, after, plan, name):
    nb = len(bufs)

    def body(*refs):
        sends, recvs = refs[nb], refs[nb + 1]
        for i, (src, dst, dev, landing) in enumerate(plan(refs[:nb])):
            pltpu.make_async_remote_copy(
                src_ref=src, dst_ref=landing, send_sem=sends.at[i], recv_sem=recvs.at[i],
                device_id=dev, device_id_type=MESH).wait()

    res = pl.pallas_call(
        body, name=name,
        in_specs=[HBM_SPEC] * nb + [SEM_SPEC, SEM_SPEC, ANY_SPEC],
        out_specs=[HBM_SPEC] * nb,
        out_shape=[pltpu.HBM(b.shape, b.dtype) for b in bufs],
        input_output_aliases={i: i for i in range(nb)},
        compiler_params=pltpu.CompilerParams(has_side_effects=DATAFLOW),
    )(*bufs, send_sems, recv_sems, after)
    return list(res)


def _allgather_weights(slots, name, landed=False):
    n = len(slots)

    def body(*refs):
        outs = refs[n:2 * n]
        send_sems, recv_sems, fwd_send, fwd_recv = refs[2 * n:]
        x, y, c = _place()
        chips = _other_chips(x, y)

        def landing(a, chip, half_of):
            return outs[a].at[2 * chip[0] + chip[1], _half(outs[a], half_of)]

        def ici(a, k, chip_from, to):
            return pltpu.make_async_remote_copy(
                src_ref=landing(a, chip_from, c), dst_ref=landing(a, chip_from, c),
                send_sem=send_sems.at[a, k], recv_sem=recv_sems.at[a, k],
                device_id=to, device_id_type=MESH)

        def d2d(a, k, chip_from, half_of):
            return pltpu.make_async_remote_copy(
                src_ref=landing(a, chip_from, half_of), dst_ref=landing(a, chip_from, half_of),
                send_sem=fwd_send.at[a, k], recv_sem=fwd_recv.at[a, k],
                device_id=(x, y, 1 - c), device_id_type=MESH)

        sends = []
        if not landed:
            sends = [ici(a, k, (x, y), (*chips[k], c)) for a in range(n) for k in range(3)]
        for cp in sends:
            cp.start()
        forwards = []
        for a in range(n):
            for k in range(3):
                if not landed:
                    ici(a, k, chips[k], (x, y, c)).wait_recv()
                fw = d2d(a, k, chips[k], c)
                fw.start()
                forwards.append(fw)
        for a in range(n):
            for k in range(3):
                d2d(a, k, chips[k], 1 - c).wait_recv()
        for cp in sends + forwards:
            cp.wait_send()

    return pl.pallas_call(
        body, name=name,
        in_specs=[HBM_SPEC] * n, out_specs=[HBM_SPEC] * n,
        out_shape=[jax.ShapeDtypeStruct(s.shape, s.dtype) for s in slots],
        input_output_aliases={a: a for a in range(n)},
        scratch_shapes=[pltpu.SemaphoreType.DMA((n, 3)), pltpu.SemaphoreType.DMA((n, 3)),
                        pltpu.SemaphoreType.DMA((n, 3)), pltpu.SemaphoreType.DMA((n, 3))],
    )(*slots)


def _sibling_exchange(partials, small, name):
    n = len(partials)
    ns = 0 if small is None else 1

    def body(*refs):
        ins, outs = refs[:n], refs[n + ns:2 * n + ns]
        send_sems, recv_sems = refs[2 * (n + ns):2 * (n + ns) + 2]
        x, y, c = _place()
        me = 4 * x + 2 * y + c
        sends = [pltpu.make_async_remote_copy(
            src_ref=ins[a].at[:, _half(ins[a], 1 - c)], dst_ref=outs[a],
            send_sem=send_sems.at[a], recv_sem=recv_sems.at[a],
            device_id=(x, y, 1 - c), device_id_type=MESH) for a in range(n)]
        if ns:
            small_ref, small_all = refs[n], refs[2 * n + 1]
            s_send, s_recv, loc_sem = refs[2 * (n + ns) + 2:]
            local = pltpu.make_async_copy(small_ref, small_all.at[me], loc_sem)
            local.start()
            for d in range(1, N_DEV):
                px, py, pc = x ^ ((d >> 2) & 1), y ^ ((d >> 1) & 1), c ^ (d & 1)
                sends.append(pltpu.make_async_remote_copy(
                    src_ref=small_ref, dst_ref=small_all.at[me],
                    send_sem=s_send.at[d - 1], recv_sem=s_recv.at[d - 1],
                    device_id=(px, py, pc), device_id_type=MESH))
        for cp in sends:
            cp.start()
        if ns:
            for d in range(1, N_DEV):
                pltpu.make_async_remote_copy(
                    src_ref=small_ref, dst_ref=small_all.at[me ^ d],
                    send_sem=s_send.at[d - 1], recv_sem=s_recv.at[d - 1],
                    device_id=(x, y, c), device_id_type=MESH).wait_recv()
        for cp in sends[:n]:
            cp.wait_recv()
        for cp in sends:
            cp.wait_send()
        if ns:
            local.wait()

    out_shape = [jax.ShapeDtypeStruct((N_CHIPS, p.shape[1] // 2, p.shape[2]), F32) for p in partials]
    scratch = [pltpu.SemaphoreType.DMA((max(n, 1),)), pltpu.SemaphoreType.DMA((max(n, 1),))]
    if ns:
        out_shape.append(jax.ShapeDtypeStruct((N_DEV,) + small.shape, F32))
        scratch += [pltpu.SemaphoreType.DMA((N_DEV - 1,)), pltpu.SemaphoreType.DMA((N_DEV - 1,)),
                    pltpu.SemaphoreType.DMA]
    return pl.pallas_call(
        body, name=name,
        in_specs=[HBM_SPEC] * (n + ns), out_specs=[HBM_SPEC] * (n + ns),
        out_shape=out_shape, scratch_shapes=scratch,
    )(*partials, *([small] if ns else []))


def _chip_sum(partial, from_sibling, c, name, tr=256):
    _, hr, C = from_sibling.shape
    nb = hr // tr

    def body(c_ref, p_ref, s_ref, o_ref):
        o_ref[...] = (p_ref[...] + s_ref[...]).astype(BF16)

    return pl.pallas_call(
        body, name=name,
        grid_spec=pltpu.PrefetchScalarGridSpec(
            num_scalar_prefetch=1, grid=(N_CHIPS, nb),
            in_specs=[pl.BlockSpec((1, tr, C), lambda j, i, c_ref: (j, c_ref[0] * nb + i, 0)),
                      pl.BlockSpec((1, tr, C), lambda j, i, c_ref: (j, i, 0))],
            out_specs=pl.BlockSpec((1, tr, C), lambda j, i, c_ref: (j, i, 0))),
        out_shape=jax.ShapeDtypeStruct(from_sibling.shape, BF16),
        compiler_params=_params(2),
    )(c, partial, from_sibling)


def _reduce_half(partial, from_sibling, received, place, name, tr=256):
    _, hr, C = from_sibling.shape
    nb = hr // tr

    def body(p_ref, mine_ref, sib_ref, r_ref, o_ref):
        acc = mine_ref[0] + sib_ref[0]
        for k in range(3):
            acc = acc + r_ref[k].astype(F32)
        o_ref[...] = acc

    return pl.pallas_call(
        body, name=name,
        grid_spec=pltpu.PrefetchScalarGridSpec(
            num_scalar_prefetch=1, grid=(nb,),
            in_specs=[pl.BlockSpec((1, tr, C), lambda i, p: (p[0], p[1] * nb + i, 0)),
                      pl.BlockSpec((1, tr, C), lambda i, p: (p[0], i, 0)),
                      pl.BlockSpec((3, tr, C), lambda i, p: (0, i, 0))],
            out_specs=pl.BlockSpec((tr, C), lambda i, p: (p[1] * nb + i, 0))),
        out_shape=jax.ShapeDtypeStruct((2 * hr, C), F32),
        compiler_params=_params(1),
    )(place, partial, from_sibling, received)


def _join_halves(fulls, name):
    n = len(fulls)

    def body(*refs):
        outs = refs[n:2 * n]
        send_sems, recv_sems = refs[2 * n:]
        x, y, c = _place()

        def copy(a, half_of, to):
            rows = outs[a].at[_half(outs[a], half_of)]
            return pltpu.make_async_remote_copy(
                src_ref=rows, dst_ref=rows, send_sem=send_sems.at[a], recv_sem=recv_sems.at[a],
                device_id=to, device_id_type=MESH)

        sends = [copy(a, c, (x, y, 1 - c)) for a in range(n)]
        for cp in sends:
            cp.start()
        for a in range(n):
            copy(a, 1 - c, (x, y, c)).wait_recv()
        for cp in sends:
            cp.wait_send()

    return pl.pallas_call(
        body, name=name,
        in_specs=[HBM_SPEC] * n, out_specs=[HBM_SPEC] * n,
        out_shape=[jax.ShapeDtypeStruct(f.shape, F32) for f in fulls],
        input_output_aliases={a: a for a in range(n)},
        scratch_shapes=[pltpu.SemaphoreType.DMA((n,)), pltpu.SemaphoreType.DMA((n,))],
    )(*fulls)


def _adamw_math(w, g, m, v):
    m = ADAM_B1 * m + (1.0 - ADAM_B1) * g
    v = ADAM_B2 * v + (1.0 - ADAM_B2) * (g * g)
    m_hat = m / (1.0 - ADAM_B1 ** ADAM_STEP)
    v_hat = v / (1.0 - ADAM_B2 ** ADAM_STEP)
    delta = -ADAM_LR * (m_hat / (jnp.sqrt(v_hat) + ADAM_EPS) + ADAM_WD * w)
    return delta, m, v


def _adamw(w, g, m, v, name, tr=256):
    R, C = w.shape
    tr = min(tr, R)

    def body(w_ref, g_ref, m_ref, v_ref, d_out, m_out, v_out):
        d_out[...], m_out[...], v_out[...] = _adamw_math(w_ref[...], g_ref[...], m_ref[...], v_ref[...])

    spec = pl.BlockSpec((tr, C), lambda i: (i, 0))
    return pl.pallas_call(
        body, name=name, grid=(R // tr,),
        in_specs=[spec] * 4, out_specs=[spec] * 3,
        out_shape=[jax.ShapeDtypeStruct((R, C), F32)] * 3,
        compiler_params=_params(1),
    )(w, g, m, v)


def _adamw_small(small_all, w, m, v, name):
    def body(s_ref, w_ref, m_ref, v_ref, g_out, d_out, m_out, v_out):
        g = s_ref[0]
        for d in range(1, N_DEV):
            g = g + s_ref[d]
        g_out[...] = g
        d_out[...], m_out[...], v_out[...] = _adamw_math(w_ref[...], g, m_ref[...], v_ref[...])

    vm = pl.BlockSpec(memory_space=pltpu.VMEM)
    return pl.pallas_call(
        body, name=name, in_specs=[vm] * 4, out_specs=[vm] * 4,
        out_shape=[jax.ShapeDtypeStruct(w.shape, F32)] * 4,
    )(small_all, w, m, v)


def _pack_small(norm_g, pool_scale, norm_f, extra_row):
    return jnp.concatenate([norm_g.reshape(2, D_MODEL), pool_scale.reshape(2, D_MODEL),
                            norm_f.reshape(1, D_MODEL), extra_row,
                            jnp.zeros((2, D_MODEL), F32)], axis=0)


def kernel(x, norm_g, pool_w_in, pool_w, pool_scale, pool_w_out, sb_w_in, sb_w_out, norm_f, loss_target, m_norm_g, m_pool_w_in, m_pool_w, m_pool_scale, m_pool_w_out, m_sb_w_in, m_sb_w_out, m_norm_f, v_norm_g, v_pool_w_in, v_pool_w, v_pool_scale, v_pool_w_out, v_sb_w_in, v_sb_w_out, v_norm_f):
    nb, S, _ = x.shape
    T = nb * S
    xt = x.reshape(T, D_MODEL)
    target = loss_target.reshape(T, D_MODEL)
    cx, cy, cc = _place()

    def shard2d(w):
        return w.reshape(-1, w.shape[-1])

    names = ("pool_w_in", "pool_w", "pool_w_out", "sb_w_in", "sb_w_out")
    w_shards = [shard2d(w) for w in (pool_w_in, pool_w, pool_w_out, sb_w_in, sb_w_out)]
    m_shards = [shard2d(w) for w in (m_pool_w_in, m_pool_w, m_pool_w_out, m_sb_w_in, m_sb_w_out)]
    v_shards = [shard2d(w) for w in (v_pool_w_in, v_pool_w, v_pool_w_out, v_sb_w_in, v_sb_w_out)]

    chip = (2 * cx + cy).reshape(1).astype(jnp.int32)
    c_arr = cc.reshape(1).astype(jnp.int32)
    place = jnp.stack([2 * cx + cy, cc]).astype(jnp.int32)
    slots = [_cast_to_slot(w, chip, "cast_" + nm) for w, nm in zip(w_shards, names)]
    g0, g1, gf = norm_g[0:1], norm_g[1:2], norm_f.reshape(1, D_MODEL)

    w_pin, = _allgather_weights(slots[:1], "allgather_pool_in_weights")
    mix_send, mix_recv, mix_slots, token = _exchange_start(slots[1:3], w_pin, _weight_plan, 6,
                                                           "pool_weights_start")
    sb_send, sb_recv, sb_slots, token = _exchange_start(slots[3:], token, _weight_plan, 6,
                                                        "sb_weights_start")

    proj0, u0 = _rms_matmul(xt, g0 + token[0:1, 0:1], w_pin, "pool_in_proj")
    mix_slots = _exchange_wait(mix_slots, mix_send, mix_recv, proj0, _weight_plan, "pool_weights_wait")
    w_g, w_pout = _allgather_weights(mix_slots, "pool_weights_forward", landed=True)
    w_pout = w_pout.reshape(D_INNER, D_MODEL)
    y0, pooled, mixed = _pool_fwd(proj0, w_g, pool_scale, S, "pool_mix")
    sb_slots = _exchange_wait(sb_slots, sb_send, sb_recv, y0, _weight_plan, "sb_weights_wait")
    w_sin, w_sout = _allgather_weights(sb_slots, "sb_weights_forward", landed=True)
    w_sout = w_sout.reshape(D_INNER, D_MODEL)
    h1 = _matmul_residual(y0, w_pout, xt, "pool_out_proj")
    n1 = 2 * QK_WIDTH + 2 * D_INNER
    qkvz, u1 = _rms_matmul(h1, g1, w_sin, "sb_in_proj")
    o, y1, ltot = _attn_fwd(qkvz, S, "sb_attention")
    dh2, d_norm_f, loss_row, dh2_b = _out_proj_loss_head(y1, w_sout, h1, gf, target,
                                                         "sb_out_proj_loss_head")

    def reduce_start(partials, tag, after=None, behind=None):
        n, done = len(partials), None
        after = c_arr if after is None else after
        if behind is None:
            from_sibling = list(_sibling_exchange(partials, None, "grad_sibling_exchange_" + tag))
        else:
            lands = [lax.empty((N_CHIPS, p.shape[1] // 2, p.shape[2]), F32) for p in partials]
            send, recv, bufs, token = _exchange_start(partials + lands, after, _sibling_plan, n,
                                                      "grad_sibling_start_" + tag)
            done = behind(token[0:1, 0:1])
            bufs = _exchange_wait(bufs, send, recv, done[0], _sibling_plan, "grad_sibling_wait_" + tag)
            partials, from_sibling, after = bufs[:n], bufs[n:], c_arr
        sums = [_chip_sum(p, s, c_arr, "grad_chip_sum_%s_%d" % (tag, i))
                for i, (p, s) in enumerate(zip(partials, from_sibling))]
        lands = [lax.empty((3,) + s.shape[1:], BF16) for s in sums]
        send, recv, bufs, token = _exchange_start(sums + lands, after, _chip_sum_plan, 3 * n,
                                                  "grad_chip_exchange_start_" + tag)
        return (partials, from_sibling, send, recv, bufs), token[0:1, 0:1], done

    def reduce_finish(started, after, tag):
        partials, from_sibling, send, recv, bufs = started
        received = _exchange_wait(bufs, send, recv, after, _chip_sum_plan,
                                  "grad_chip_exchange_wait_" + tag)[len(partials):]
        return [_reduce_half(p, s, r, place, "grad_reduce_%s_%d" % (tag, i))
                for i, (p, s, r) in enumerate(zip(partials, from_sibling, received))]

    shard = lambda i, j, t: (j, 0, 0)
    gw_sout = _matmul_tn(y1, dh2_b, D_INNER, D_MODEL, (D_INNER, D_MODEL), (1024, 1024),
                         lambda i, j, t: (i, j), "grad_sb_w_out", bm=1024, bn=1024)
    do, dproj1 = _attn_gate_bwd(dh2_b, w_sout, qkvz, o, "sb_gate_bwd")
    dproj1 = _attn_bwd(qkvz, do, ltot, dproj1, S, "sb_attention_bwd")
    gw_sin = _matmul_tn(u1, dproj1, D_MODEL, n1, (N_CHIPS, D_MODEL, n1 // 4), (1, D_MODEL, n1 // 4),
                        shard, "grad_sb_w_in", bm=D_MODEL, bn=n1 // 4)
    sb_started, token, (dh1, d_g1, dh1_b) = reduce_start(
        [gw_sin, gw_sout.reshape(N_CHIPS, -1, D_MODEL)], "sb",
        behind=lambda tok: _matmul_nt_rms_bwd(dproj1, w_sin, h1, g1 + tok, dh2, "sb_in_bwd", True,
                                              tk=w_sin.shape[2]))
    gw_pout = _matmul_tn(y0, dh1_b, D_INNER, D_MODEL, (D_INNER, D_MODEL), (1024, 1024),
                         lambda i, j, t: (i, j), "grad_pool_w_out", bm=1024, bn=1024)
    dmixed, dproj0, d_scale = _pool_gate_bwd(dh1_b, w_pout, proj0, mixed, pool_scale + token,
                                             "pool_gate_bwd")
    gw_g = _matmul_tn(pooled, dmixed, D_INNER, D_INNER, (N_CHIPS, GROUP_DIM, GROUP_DIM),
                      (N_CHIPS, GROUP_DIM // N_CHIPS, GROUP_DIM), lambda i, j, t: (0, i, 0),
                      "grad_pool_w", bm=GROUP_DIM, bn=GROUP_DIM, diagonal_blocks=True)
    mix_started, token, (dproj0,) = reduce_start(
        [gw_g, gw_pout.reshape(N_CHIPS, -1, D_MODEL)], "pool_mix",
        behind=lambda tok: (_pool_bwd(dmixed, w_g, dproj0, tok, S, "pool_bwd"),))
    n0 = 2 * D_INNER
    gw_pin = _matmul_tn(u0, dproj0, D_MODEL, n0, (N_CHIPS, D_MODEL, n0 // 4), (1, D_MODEL, n0 // 4),
                        shard, "grad_pool_w_in", bm=D_MODEL, bn=n0 // 4)
    pin_started, token, _ = reduce_start([gw_pin], "pool_in", after=token)
    dx, d_g0 = _matmul_nt_rms_bwd(dproj0, w_pin, xt, g0 + token, dh1, "pool_in_bwd", False,
                                  tk=w_pin.shape[2])

    small = _pack_small(jnp.concatenate([d_g0, d_g1], axis=0), d_scale, d_norm_f,
                        jnp.broadcast_to(loss_row[:, :1], (1, D_MODEL)))
    small_all, = _sibling_exchange([], small, "small_sums_exchange")
    grads = _join_halves(reduce_finish(pin_started, dx, "pool_in")
                         + reduce_finish(mix_started, dx, "pool_mix")
                         + reduce_finish(sb_started, dx, "sb"), "grad_join_halves")

    deltas, new_m, new_v = [], [], []
    for w, g, m, v, nm in zip(w_shards, grads, m_shards, v_shards, names):
        d, mm, vv = _adamw(w, g, m, v, "adamw_" + nm)
        deltas.append(d)
        new_m.append(mm)
        new_v.append(vv)

    zero_row = jnp.zeros((1, D_MODEL), F32)
    g_small, d_small, m_small, v_small = _adamw_small(
        small_all, _pack_small(norm_g, pool_scale, norm_f, zero_row),
        _pack_small(m_norm_g, m_pool_scale, m_norm_f, zero_row),
        _pack_small(v_norm_g, v_pool_scale, v_norm_f, zero_row + 1.0), "adamw_small")
    loss = g_small[5, 0]

    def unpack_small(a):
        return a[0:2], a[2:4].reshape(1, D_INNER), a[4]

    def assemble(big, small3):
        ng, ps, nf = small3
        return [ng, big[0].reshape(pool_w_in.shape), big[1].reshape(pool_w.shape), ps,
                big[2].reshape(pool_w_out.shape), big[3].reshape(sb_w_in.shape),
                big[4].reshape(sb_w_out.shape), nf]

    return (loss, dx.reshape(x.shape),
            *assemble(grads, unpack_small(g_small)),
            *assemble(deltas, unpack_small(d_small)),
            *assemble(new_m, unpack_small(m_small)),
            *assemble(new_v, unpack_small(v_small)))
```

```python
import jax
import jax.numpy as jnp
from jax import lax
from jax.experimental import pallas as pl
from jax.experimental.pallas import tpu as pltpu

F32 = jnp.float32
BF16 = jnp.bfloat16
MESH = pl.DeviceIdType.MESH

D_MODEL = 1024
D_INNER = 2048
N_GROUPS = 4
GROUP_DIM = 512
HEAD_PAIR_QK = 128
HEAD_V = 128
QK_WIDTH = 1024
RMS_EPS = 1e-6
HALO = 16
N_CHIPS = 4
N_DEV = 8

ADAM_LR = 0.001
ADAM_B1 = 0.9
ADAM_B2 = 0.999
ADAM_EPS = 1e-08
ADAM_WD = 0.01
ADAM_STEP = 10

VMEM_LIMIT = 56 * 1024 * 1024

HBM_SPEC = pl.BlockSpec(memory_space=pltpu.HBM)


def _params(n_axes):
    return pltpu.CompilerParams(dimension_semantics=("arbitrary",) * n_axes,
                                vmem_limit_bytes=VMEM_LIMIT)


def _dot(a, b):
    return jnp.dot(a, b, preferred_element_type=F32)


def _dot_nt(a, b):
    return lax.dot_general(a, b, (((1,), (1,)), ((), ())), preferred_element_type=F32)


def _dot_tn(a, b):
    return lax.dot_general(a, b, (((0,), (0,)), ((), ())), preferred_element_type=F32)


def _sigmoid(z):
    return 1.0 / (1.0 + jnp.exp(-z))


def _row_blocks(tm, rows=256):
    return [slice(r, r + rows) for r in range(0, tm, rows)]


def _rms_matmul(h, g_row, w4, name, tm=1024):
    T = h.shape[0]
    n_shards, _, tn = w4.shape
    nm = T // tm

    def body(h_ref, g_ref, w_ref, o_ref, u_out, u_all):
        n, m = pl.program_id(0), pl.program_id(1)
        rows = pl.ds(pl.multiple_of(m * tm, tm), tm)

        @pl.when(n == 0)
        def _():
            x = h_ref[...]
            inv = lax.rsqrt(jnp.mean(x * x, axis=-1, keepdims=True) + RMS_EPS)
            u = (x * inv * g_ref[...]).astype(BF16)
            u_all[rows, :] = u
            u_out[...] = u

        o_ref[...] = _dot(u_all[rows, :], w_ref[0]).astype(BF16)

    return pl.pallas_call(
        body, name=name, grid=(n_shards, nm),
        in_specs=[pl.BlockSpec((tm, D_MODEL), lambda n, m: (jnp.where(n == 0, m, nm - 1), 0)),
                  pl.BlockSpec((1, D_MODEL), lambda n, m: (0, 0)),
                  pl.BlockSpec((1, D_MODEL, tn), lambda n, m: (n, 0, 0))],
        out_specs=[pl.BlockSpec((tm, tn), lambda n, m: (m, n)),
                   pl.BlockSpec((tm, D_MODEL), lambda n, m: (jnp.where(n == 0, m, nm - 1), 0))],
        out_shape=[jax.ShapeDtypeStruct((T, n_shards * tn), BF16),
                   jax.ShapeDtypeStruct((T, D_MODEL), BF16)],
        scratch_shapes=[pltpu.VMEM((T, D_MODEL), BF16)],
        compiler_params=_params(2),
    )(h, g_row, w4)


def _matmul_residual(a, w, res, name, tm=1024, tn=1024):
    T, K = a.shape
    N = w.shape[1]

    def body(a_ref, w_ref, r_ref, o_ref):
        o_ref[...] = r_ref[...] + _dot(a_ref[...], w_ref[...])

    return pl.pallas_call(
        body, name=name, grid=(T // tm, N // tn),
        in_specs=[pl.BlockSpec((tm, K), lambda m, n: (m, 0)),
                  pl.BlockSpec((K, tn), lambda m, n: (0, n)),
                  pl.BlockSpec((tm, tn), lambda m, n: (m, n))],
        out_specs=pl.BlockSpec((tm, tn), lambda m, n: (m, n)),
        out_shape=jax.ShapeDtypeStruct((T, N), F32),
        compiler_params=_params(2),
    )(a, w, res)


def _matmul_tn(a, b, a_cols, b_cols, out_shape, out_block, out_map, name, bm, bn, tk=2048,
               diagonal_blocks=False):
    T = a.shape[0]
    tk = min(tk, T)

    def body(a_ref, b_ref, o_ref):
        @pl.when(pl.program_id(2) == 0)
        def _():
            o_ref[...] = jnp.zeros_like(o_ref)

        part = _dot_tn(a_ref[...].astype(BF16), b_ref[...].astype(BF16))
        o_ref[...] += part.reshape(o_ref.shape)

    b_map = (lambda i, j, t: (t, i)) if diagonal_blocks else (lambda i, j, t: (t, j))
    return pl.pallas_call(
        body, name=name, grid=(a_cols // bm, 1 if diagonal_blocks else b_cols // bn, T // tk),
        in_specs=[pl.BlockSpec((tk, bm), lambda i, j, t: (t, i)),
                  pl.BlockSpec((tk, bn), b_map)],
        out_specs=pl.BlockSpec(out_block, out_map),
        out_shape=jax.ShapeDtypeStruct(out_shape, F32),
        compiler_params=_params(3),
    )(a, b)


def _matmul_nt_rms_bwd(dproj, w4, h, g_row, dres, name, with_bf16, tm=1024, tk=512):
    T, cols = dproj.shape
    per_shard = w4.shape[2] // tk
    nk = cols // tk

    def body(dp_ref, w_ref, h_ref, g_ref, r_ref, dx_ref, dg_ref, *rest):
        acc = rest[-1]
        m, k = pl.program_id(0), pl.program_id(1)

        @pl.when(k == 0)
        def _():
            acc[...] = jnp.zeros_like(acc)

        @pl.when((k == 0) & (m == 0))
        def _():
            dg_ref[...] = jnp.zeros_like(dg_ref)

        acc[...] += _dot_nt(dp_ref[...], w_ref[0])

        @pl.when(k == nk - 1)
        def _():
            du = acc[...]
            x = h_ref[...]
            inv = lax.rsqrt(jnp.mean(x * x, axis=-1, keepdims=True) + RMS_EPS)
            xhat = x * inv
            dg_ref[...] += jnp.sum(du * xhat, axis=0, keepdims=True)
            dxh = du * g_ref[...]
            proj = jnp.mean(dxh * xhat, axis=-1, keepdims=True)
            dx = r_ref[...] + inv * (dxh - xhat * proj)
            dx_ref[...] = dx
            if with_bf16:
                rest[0][...] = dx.astype(BF16)

    rows = pl.BlockSpec((tm, D_MODEL), lambda m, k: (m, 0))
    return pl.pallas_call(
        body, name=name, grid=(T // tm, nk),
        in_specs=[pl.BlockSpec((tm, tk), lambda m, k: (m, k)),
                  pl.BlockSpec((1, D_MODEL, tk), lambda m, k: (k // per_shard, 0, k % per_shard)),
                  rows, pl.BlockSpec((1, D_MODEL), lambda m, k: (0, 0)), rows],
        out_specs=[rows, pl.BlockSpec((1, D_MODEL), lambda m, k: (0, 0))] + [rows] * with_bf16,
        out_shape=[jax.ShapeDtypeStruct((T, D_MODEL), F32), jax.ShapeDtypeStruct((1, D_MODEL), F32)]
        + [jax.ShapeDtypeStruct((T, D_MODEL), BF16)] * with_bf16,
        scratch_shapes=[pltpu.VMEM((tm, D_MODEL), F32)],
        compiler_params=_params(2),
    )(dproj, w4, h, g_row, dres)


def _window_of(g):
    return jnp.left_shift(2, g)


def _select_stage(g, stages):
    res = stages[0]
    for i in range(1, len(stages)):
        res = jnp.where(g >= i, stages[i], res)
    return res


def _pool_fwd(proj0, wg4, scale_row, S, name, tm=1024):
    T = proj0.shape[0]
    tm = min(tm, S)
    blocks_per_seq = S // tm
    hb = tm // HALO

    def body(x_ref, halo_ref, z_ref, w_ref, s_ref, y_ref, p_ref, mix_ref):
        m, g = pl.program_id(0), pl.program_id(1)
        first = (m % blocks_per_seq) == 0
        halo = jnp.where(first, 0.0, halo_ref[...].astype(F32))
        x = x_ref[...].astype(F32)
        ext = jnp.concatenate([halo, x], axis=0)
        stages = []
        cur = ext
        for sh in (1, 2, 4, 8):
            cur = cur + pltpu.roll(cur, sh, 0)
            stages.append(cur[HALO:, :])
        win_sum = _select_stage(g, stages)
        pos = (m % blocks_per_seq) * tm + lax.broadcasted_iota(jnp.int32, (tm, 1), 0)
        count = jnp.minimum(pos + 1, _window_of(g)).astype(F32)
        p_ref[...] = (win_sum / count - x).astype(BF16)
        w = w_ref[...].reshape(GROUP_DIM, GROUP_DIM)
        for rows in _row_blocks(tm):
            mixed = _dot(p_ref[rows, :], w)
            z = z_ref[rows, :].astype(F32)
            y_ref[rows, :] = (mixed * s_ref[...] * (z * _sigmoid(z))).astype(BF16)
            mix_ref[rows, :] = mixed.astype(BF16)

    blk = lambda m, g: (m, g)
    return pl.pallas_call(
        body, name=name, grid=(T // tm, N_GROUPS),
        in_specs=[pl.BlockSpec((tm, GROUP_DIM), blk),
                  pl.BlockSpec((HALO, GROUP_DIM), lambda m, g: (jnp.maximum(m * hb - 1, 0), g)),
                  pl.BlockSpec((tm, GROUP_DIM), lambda m, g: (m, N_GROUPS + g)),
                  pl.BlockSpec((N_CHIPS, GROUP_DIM // N_CHIPS, GROUP_DIM), lambda m, g: (0, g, 0)),
                  pl.BlockSpec((1, GROUP_DIM), lambda m, g: (0, g))],
        out_specs=[pl.BlockSpec((tm, GROUP_DIM), blk)] * 3,
        out_shape=[jax.ShapeDtypeStruct((T, D_INNER), BF16)] * 3,
        compiler_params=_params(2),
    )(proj0, proj0, proj0, wg4, scale_row)


def _pool_gate_bwd(dh, w_out, proj0, mixed, scale_row, name, tm=1024, tn=512):
    T = dh.shape[0]
    gate_b0 = D_INNER // tn

    def body(dh_ref, w_ref, z_ref, mix_ref, s_ref, dm_ref, dz_ref, ds_ref):
        m, n = pl.program_id(0), pl.program_id(1)

        @pl.when((m == 0) & (n == 0))
        def _():
            ds_ref[...] = jnp.zeros_like(ds_ref)

        cols = pl.ds(pl.multiple_of(n * tn, tn), tn)
        s = s_ref[...]
        ds = ds_ref[:, cols]
        for rows in _row_blocks(tm):
            dy = _dot_nt(dh_ref[rows, :], w_ref[...])
            z = z_ref[rows, :].astype(F32)
            sig = _sigmoid(z)
            silu = z * sig
            mixed = mix_ref[rows, :].astype(F32)
            dm_ref[rows, :] = (dy * s * silu).astype(BF16)
            dz_ref[rows, :] = (dy * mixed * s * (sig * (1.0 + z * (1.0 - sig)))).astype(BF16)
            ds = ds + jnp.sum(dy * mixed * silu, axis=0, keepdims=True)
        ds_ref[:, cols] = ds

    return pl.pallas_call(
        body, name=name, grid=(T // tm, D_INNER // tn),
        in_specs=[pl.BlockSpec((tm, D_MODEL), lambda m, n: (m, 0)),
                  pl.BlockSpec((tn, D_MODEL), lambda m, n: (n, 0)),
                  pl.BlockSpec((tm, tn), lambda m, n: (m, gate_b0 + n)),
                  pl.BlockSpec((tm, tn), lambda m, n: (m, n)),
                  pl.BlockSpec((1, tn), lambda m, n: (0, n))],
        out_specs=[pl.BlockSpec((tm, tn), lambda m, n: (m, n)),
                   pl.BlockSpec((tm, tn), lambda m, n: (m, gate_b0 + n)),
                   pl.BlockSpec((1, D_INNER), lambda m, n: (0, 0))],
        out_shape=[jax.ShapeDtypeStruct((T, D_INNER), BF16),
                   jax.ShapeDtypeStruct((T, 2 * D_INNER), BF16),
                   jax.ShapeDtypeStruct((1, D_INNER), F32)],
        compiler_params=_params(2),
    )(dh, w_out, proj0, mixed, scale_row)


def _pool_bwd(dmixed, wg4, dproj0, after, S, name, tm=1024):
    T = dmixed.shape[0]
    tm = min(tm, S)
    blocks_per_seq = S // tm
    hb = tm // HALO
    n_halo_blocks = T // HALO

    def body(dm_ref, halo_ref, w_ref, _, __, o_ref):
        m, g = pl.program_id(0), pl.program_id(1)
        ext = jnp.concatenate([dm_ref[...], halo_ref[...]], axis=0)
        dp = _dot_nt(ext, w_ref[...].reshape(GROUP_DIM, GROUP_DIM))
        pos = (m % blocks_per_seq) * tm + lax.broadcasted_iota(jnp.int32, (tm + HALO, 1), 0)
        count = jnp.minimum(pos + 1, _window_of(g)).astype(F32)
        c = jnp.where(pos < S, dp / count, 0.0)
        n = tm + HALO
        stages = []
        cur = c
        for sh in (1, 2, 4, 8):
            cur = cur + pltpu.roll(cur, n - sh, 0)
            stages.append(cur[:tm, :])
        o_ref[...] = (_select_stage(g, stages) - dp[:tm, :]).astype(BF16)

    blk = lambda m, g: (m, g)
    return pl.pallas_call(
        body, name=name, grid=(T // tm, N_GROUPS),
        in_specs=[pl.BlockSpec((tm, GROUP_DIM), blk),
                  pl.BlockSpec((HALO, GROUP_DIM),
                               lambda m, g: (jnp.minimum((m + 1) * hb, n_halo_blocks - 1), g)),
                  pl.BlockSpec((N_CHIPS, GROUP_DIM // N_CHIPS, GROUP_DIM), lambda m, g: (0, g, 0)),
                  HBM_SPEC, ANY_SPEC],
        out_specs=pl.BlockSpec((tm, GROUP_DIM), blk),
        out_shape=jax.ShapeDtypeStruct(dproj0.shape, dproj0.dtype),
        input_output_aliases={3: 0},
        compiler_params=_params(2),
    )(dmixed, dmixed, wg4, dproj0, after)


TQ = 256


def _split_dot(x, m):
    hi = x.astype(BF16)
    lo = (x - hi.astype(F32)).astype(BF16)
    return _dot(hi, m) + _dot(lo, m)


NEG_LOG2E = -1.4426950408889634


def _log_terms(z):
    soft = jnp.log(1.0 + jnp.exp2(jnp.abs(z) * NEG_LOG2E))
    log_beta = jnp.minimum(z, 0.0) - soft
    return log_beta, log_beta - z


N_HEADS = 16
FWD_HEADS = BWD_HEADS = 4


def _masked_heads(x, heads):
    lane = lax.broadcasted_iota(jnp.int32, (1, HEAD_PAIR_QK), 1)
    out = []
    for hh in range(heads):
        slab = x[:, (hh // 2) * HEAD_PAIR_QK:(hh // 2 + 1) * HEAD_PAIR_QK]
        out.append(jnp.where((lane // 64) == hh % 2, slab, jnp.zeros_like(slab)))
    return out


def _attn_fwd(qkvz, S, name):
    T = qkvz.shape[0]
    nq = S // TQ
    HEADS, QK_W, V_W = FWD_HEADS, FWD_HEADS * 64, FWD_HEADS * HEAD_V
    k_b0 = QK_WIDTH // QK_W
    v_b0 = 2 * QK_WIDTH // V_W
    z_b0 = (2 * QK_WIDTH + D_INNER) // V_W
    hs = range(HEADS)

    def body(q_ref, k_ref, v_ref, z_ref, o_ref, y_ref, lt_ref):
        row = lax.broadcasted_iota(jnp.int32, (TQ, TQ), 0)
        col = lax.broadcasted_iota(jnp.int32, (TQ, TQ), 1)
        causal = col < row
        later_in_block = (row > col).astype(BF16)
        lax.fori_loop(0, nq, lambda qi, _: q_block(qi, causal, later_in_block,
                                                   q_ref, k_ref, v_ref, z_ref, o_ref, y_ref, lt_ref), 0)

    def q_block(qi, causal, later_in_block, q_ref, k_ref, v_ref, z_ref, o_ref, y_ref, lt_ref):
        rows = pl.ds(pl.multiple_of(qi * TQ, TQ), TQ)
        qms = [qm * 0.125 for qm in _masked_heads(q_ref[rows, :], HEADS)]

        def step(j, carry, diagonal):
            koff = pl.multiple_of(j * TQ, TQ)
            kbs = [k_ref[pl.ds(koff, TQ), p * HEAD_PAIR_QK:(p + 1) * HEAD_PAIR_QK]
                   for p in range(HEADS // 2)]
            run, acc = [carry[2 * hh] for hh in hs], [carry[2 * hh + 1] for hh in hs]
            z = [_dot_nt(qms[hh], kbs[hh // 2]) for hh in hs]
            terms = [_log_terms(z[hh]) for hh in hs]
            log_om = [jnp.where(causal, t[1], 0.0) if diagonal else t[1] for t in terms]
            later = [_split_dot(log_om[hh], later_in_block) for hh in hs]
            a = [jnp.exp(terms[hh][0] + (run[hh] + later[hh])) for hh in hs]
            if diagonal:
                a = [jnp.where(causal, a[hh], 0.0) for hh in hs]
            out = []
            for hh in hs:
                vb = v_ref[pl.ds(koff, TQ), hh * HEAD_V:(hh + 1) * HEAD_V]
                out += [run[hh] + jnp.sum(log_om[hh], axis=1, keepdims=True),
                        acc[hh] + _dot(a[hh].astype(BF16), vb)]
            return tuple(out)

        zero = (jnp.zeros((TQ, 1), F32), jnp.zeros((TQ, HEAD_V), F32))
        carry = step(qi, zero * HEADS, True)
        carry = lax.fori_loop(0, qi, lambda i, c: step(qi - 1 - i, c, False), carry)
        for hh in hs:
            sl = slice(hh * HEAD_V, (hh + 1) * HEAD_V)
            acc = carry[2 * hh + 1]
            z = z_ref[rows, sl].astype(F32)
            o_ref[rows, sl] = acc.astype(BF16)
            y_ref[rows, sl] = (acc * (z * _sigmoid(z))).astype(BF16)
            lt_ref[rows, hh:hh + 1] = carry[2 * hh]
        return 0

    blk = lambda b, p: (b, p)
    return pl.pallas_call(
        body, name=name, grid=(T // S, N_HEADS // HEADS),
        in_specs=[pl.BlockSpec((S, QK_W), blk),
                  pl.BlockSpec((S, QK_W), lambda b, p: (b, k_b0 + p)),
                  pl.BlockSpec((S, V_W), lambda b, p: (b, v_b0 + p)),
                  pl.BlockSpec((S, V_W), lambda b, p: (b, z_b0 + p))],
        out_specs=[pl.BlockSpec((S, V_W), blk),
                   pl.BlockSpec((S, V_W), blk),
                   pl.BlockSpec((None, S, HEADS), lambda b, p: (p, b, 0))],
        out_shape=[jax.ShapeDtypeStruct((T, D_INNER), BF16),
                   jax.ShapeDtypeStruct((T, D_INNER), BF16),
                   jax.ShapeDtypeStruct((N_HEADS // HEADS, T, HEADS), F32)],
        compiler_params=_params(2),
    )(qkvz, qkvz, qkvz, qkvz)


def _attn_gate_bwd(dh, w_out, qkvz, o, name, tm=1024, tn=512):
    T = dh.shape[0]
    gate_b0 = (2 * QK_WIDTH + D_INNER) // tn

    def body(dh_ref, w_ref, z_ref, o_ref, do_ref, dz_ref):
        for rows in _row_blocks(tm):
            dy = _dot_nt(dh_ref[rows, :], w_ref[...])
            z = z_ref[rows, :].astype(F32)
            sig = _sigmoid(z)
            do_ref[rows, :] = (dy * (z * sig)).astype(BF16)
            dz_ref[rows, :] = (dy * o_ref[rows, :].astype(F32)
                               * (sig * (1.0 + z * (1.0 - sig)))).astype(BF16)

    return pl.pallas_call(
        body, name=name, grid=(T // tm, D_INNER // tn),
        in_specs=[pl.BlockSpec((tm, D_MODEL), lambda m, n: (m, 0)),
                  pl.BlockSpec((tn, D_MODEL), lambda m, n: (n, 0)),
                  pl.BlockSpec((tm, tn), lambda m, n: (m, gate_b0 + n)),
                  pl.BlockSpec((tm, tn), lambda m, n: (m, n))],
        out_specs=[pl.BlockSpec((tm, tn), lambda m, n: (m, n)),
                   pl.BlockSpec((tm, tn), lambda m, n: (m, gate_b0 + n))],
        out_shape=[jax.ShapeDtypeStruct((T, D_INNER), BF16),
                   jax.ShapeDtypeStruct((T, 2 * QK_WIDTH + 2 * D_INNER), BF16)],
        compiler_params=_params(2),
    )(dh, w_out, qkvz, o)


def _attn_bwd(qkv, do, ltot, dproj1, S, name):
    T = qkv.shape[0]
    nq = S // TQ
    HEADS, QK_W, V_W = BWD_HEADS, BWD_HEADS * 64, BWD_HEADS * HEAD_V
    k_b0 = QK_WIDTH // QK_W
    v_b0 = 2 * QK_WIDTH // V_W
    hs = range(HEADS)
    pairs = range(HEADS // 2)
    n_groups = N_HEADS // HEADS
    n_steps = (T // S) * n_groups

    def body(q_ref, k_ref, v_ref, do_ref, lt_ref, _, out_ref, dq_s, dk_s, dv_s, dkb_s, dvb_s, sems):
        b, p = pl.program_id(0), pl.program_id(1)
        row = lax.broadcasted_iota(jnp.int32, (TQ, TQ), 0)
        col = lax.broadcasted_iota(jnp.int32, (TQ, TQ), 1)
        causal = col < row
        upto = (row <= col).astype(BF16)
        before = (row < col).astype(BF16)
        dk_s[...] = jnp.zeros_like(dk_s)
        dv_s[...] = jnp.zeros_like(dv_s)
        rows = pl.ds(pl.multiple_of(b * S, TQ), S)
        copies = [
            pltpu.make_async_copy(
                dq_s, out_ref.at[rows, pl.ds(pl.multiple_of(p * QK_W, 128), QK_W)], sems.at[0]),
            pltpu.make_async_copy(
                dkb_s, out_ref.at[rows, pl.ds(pl.multiple_of(QK_WIDTH + p * QK_W, 128), QK_W)],
                sems.at[1]),
            pltpu.make_async_copy(
                dvb_s, out_ref.at[rows, pl.ds(pl.multiple_of(2 * QK_WIDTH + p * V_W, 128), V_W)],
                sems.at[2]),
        ]
        step = b * n_groups + p

        @pl.when(step > 0)
        def _():
            for cp in copies:
                cp.wait()

        def q_block(qi, _):
            qoff = pl.multiple_of(qi * TQ, TQ)
            qms = [qm * 0.125 for qm in _masked_heads(q_ref[pl.ds(qoff, TQ), :], HEADS)]
            vsl = [slice(hh * HEAD_V, (hh + 1) * HEAD_V) for hh in hs]
            psl = [slice(pp * HEAD_PAIR_QK, (pp + 1) * HEAD_PAIR_QK) for pp in pairs]
            do_h = [do_ref[pl.ds(qoff, TQ), sl] for sl in vsl]
            total = [lt_ref[pl.ds(qoff, TQ), hh:hh + 1] for hh in hs]

            def k_block(j, carry, diagonal):
                koff = pl.multiple_of(j * TQ, TQ)
                kms = _masked_heads(k_ref[pl.ds(koff, TQ), :], HEADS)
                g_before = [carry[2 * hh] for hh in hs]
                lom_before = [carry[2 * hh + 1] for hh in hs]
                z = [_dot_nt(qms[hh], kms[hh]) for hh in hs]
                da = [_dot_nt(do_h[hh], v_ref[pl.ds(koff, TQ), vsl[hh]]) for hh in hs]
                terms = [_log_terms(z[hh]) for hh in hs]
                log_om = [jnp.where(causal, t[1], 0.0) if diagonal else t[1] for t in terms]
                prefix = [_split_dot(log_om[hh], upto) for hh in hs]
                a = [jnp.exp(terms[hh][0] + ((total[hh] - lom_before[hh]) - prefix[hh])) for hh in hs]
                if diagonal:
                    a = [jnp.where(causal, a[hh], 0.0) for hh in hs]
                g = [a[hh] * da[hh] for hh in hs]
                g_prefix = [_dot(g[hh].astype(BF16), before) for hh in hs]
                out, dzs = [], []
                for hh in hs:
                    beta = jnp.exp(terms[hh][0])
                    g_excl = (g_before[hh] + g_prefix[hh]) * beta
                    if diagonal:
                        g_excl = jnp.where(causal, g_excl, 0.0)
                    dzs.append((g[hh] * (1.0 - beta) - g_excl).astype(BF16))
                    out += [g_before[hh] + jnp.sum(g[hh], axis=1, keepdims=True),
                            lom_before[hh] + jnp.sum(log_om[hh], axis=1, keepdims=True)]
                for hh in hs:
                    dv_s[pl.ds(koff, TQ), vsl[hh]] += _dot_tn(a[hh].astype(BF16), do_h[hh])
                dq = []
                for pp in pairs:
                    pair = slice(2 * pp, 2 * pp + 2)
                    dq.append(carry[2 * HEADS + pp] + _dot(jnp.concatenate(dzs[pair], axis=1),
                                                           jnp.concatenate(kms[pair], axis=0)))
                    dk_s[pl.ds(koff, TQ), psl[pp]] += _dot_tn(jnp.concatenate(dzs[pair], axis=0),
                                                              jnp.concatenate(qms[pair], axis=0))
                return tuple(out) + tuple(dq)

            zero = jnp.zeros((TQ, 1), F32)
            carry = (zero,) * (2 * HEADS) + (jnp.zeros((TQ, HEAD_PAIR_QK), F32),) * (HEADS // 2)
            carry = lax.fori_loop(0, qi, lambda j, c: k_block(j, c, False), carry)
            carry = k_block(qi, carry, True)
            for pp in pairs:
                dq_s[pl.ds(qoff, TQ), psl[pp]] = (carry[2 * HEADS + pp] * 0.125).astype(BF16)
            return 0

        lax.fori_loop(0, nq, q_block, 0)
        dkb_s[...] = dk_s[...].astype(BF16)
        dvb_s[...] = dv_s[...].astype(BF16)
        for cp in copies:
            cp.start()

        @pl.when(step == n_steps - 1)
        def _():
            for cp in copies:
                cp.wait()

    return pl.pallas_call(
        body, name=name, grid=(T // S, N_HEADS // HEADS),
        in_specs=[pl.BlockSpec((S, QK_W), lambda b, p: (b, p)),
                  pl.BlockSpec((S, QK_W), lambda b, p: (b, k_b0 + p)),
                  pl.BlockSpec((S, V_W), lambda b, p: (b, v_b0 + p)),
                  pl.BlockSpec((S, V_W), lambda b, p: (b, p)),
                  pl.BlockSpec((None, S, HEADS), lambda b, p: (p, b, 0)),
                  HBM_SPEC],
        out_specs=HBM_SPEC,
        out_shape=jax.ShapeDtypeStruct(dproj1.shape, dproj1.dtype),
        input_output_aliases={5: 0},
        scratch_shapes=[pltpu.VMEM((S, QK_W), BF16),
                        pltpu.VMEM((S, QK_W), F32),
                        pltpu.VMEM((S, V_W), F32),
                        pltpu.VMEM((S, QK_W), BF16),
                        pltpu.VMEM((S, V_W), BF16),
                        pltpu.SemaphoreType.DMA((3,))],
        compiler_params=_params(2),
    )(qkv, qkv, qkv, do, ltot, dproj1)


def _out_proj_loss_head(a, w, res, g_row, target, name, tm=512):
    T, K = a.shape

    def body(a_ref, w_ref, r_ref, g_ref, t_ref, dh_ref, dg_ref, loss_ref, dhb_ref):
        @pl.when(pl.program_id(0) == 0)
        def _():
            dg_ref[...] = jnp.zeros_like(dg_ref)
            loss_ref[...] = jnp.zeros_like(loss_ref)

        gain = g_ref[...]
        dg, loss = dg_ref[...], loss_ref[...]
        for rows in _row_blocks(tm):
            x = r_ref[rows, :] + _dot(a_ref[rows, :], w_ref[...])
            inv = lax.rsqrt(jnp.mean(x * x, axis=-1, keepdims=True) + RMS_EPS)
            xhat = x * inv
            err = xhat * gain - t_ref[rows, :]
            per_token = jnp.mean(err * err, axis=-1, keepdims=True)
            loss = loss + 0.5 * jnp.sum(per_token, axis=0, keepdims=True)
            dy = err * (1.0 / D_MODEL)
            dg = dg + jnp.sum(dy * xhat, axis=0, keepdims=True)
            dxh = dy * gain
            proj = jnp.mean(dxh * xhat, axis=-1, keepdims=True)
            dh = inv * (dxh - xhat * proj)
            dh_ref[rows, :] = dh
            dhb_ref[rows, :] = dh.astype(BF16)
        dg_ref[...] = dg
        loss_ref[...] = loss

    return pl.pallas_call(
        body, name=name, grid=(T // tm,),
        in_specs=[pl.BlockSpec((tm, K), lambda m: (m, 0)),
                  pl.BlockSpec((K, D_MODEL), lambda m: (0, 0)),
                  pl.BlockSpec((tm, D_MODEL), lambda m: (m, 0)),
                  pl.BlockSpec((1, D_MODEL), lambda m: (0, 0)),
                  pl.BlockSpec((tm, D_MODEL), lambda m: (m, 0))],
        out_specs=[pl.BlockSpec((tm, D_MODEL), lambda m: (m, 0)),
                   pl.BlockSpec((1, D_MODEL), lambda m: (0, 0)),
                   pl.BlockSpec((1, 128), lambda m: (0, 0)),
                   pl.BlockSpec((tm, D_MODEL), lambda m: (m, 0))],
        out_shape=[jax.ShapeDtypeStruct((T, D_MODEL), F32),
                   jax.ShapeDtypeStruct((1, D_MODEL), F32),
                   jax.ShapeDtypeStruct((1, 128), F32),
                   jax.ShapeDtypeStruct((T, D_MODEL), BF16)],
        compiler_params=_params(1),
    )(a, w, res, g_row, target)


def _place():
    return lax.axis_index("x"), lax.axis_index("y"), lax.axis_index("c")


def _other_chips(x, y):
    return [(1 - x, y), (x, 1 - y), (1 - x, 1 - y)]


def _half(ref, c):
    hr = ref.shape[-2] // 2
    return pl.ds(pl.multiple_of(c * hr, 8), hr)


def _cast_to_slot(shard, chip, name, tr=256):
    R, C = shard.shape

    def body(chip_ref, w_ref, o_ref):
        o_ref[0] = w_ref[...].astype(BF16)

    return pl.pallas_call(
        body, name=name,
        grid_spec=pltpu.PrefetchScalarGridSpec(
            num_scalar_prefetch=1, grid=(R // tr,),
            in_specs=[pl.BlockSpec((tr, C), lambda i, chip_ref: (i, 0))],
            out_specs=pl.BlockSpec((1, tr, C), lambda i, chip_ref: (chip_ref[0], i, 0))),
        out_shape=jax.ShapeDtypeStruct((N_CHIPS, R, C), BF16),
        compiler_params=_params(1),
    )(chip, shard)


def _weight_plan(bufs):
    x, y, c = _place()
    plan = []
    for buf in bufs:
        mine = buf.at[2 * x + y, _half(buf, c)]
        for ox, oy in _other_chips(x, y):
            plan.append((mine, mine, (ox, oy, c), buf.at[2 * ox + oy, _half(buf, c)]))
    return plan


def _chip_sum_plan(bufs):
    x, y, c = _place()
    n = len(bufs) // 2
    plan = []
    for sums, land in zip(bufs[:n], bufs[n:]):
        for k, (ox, oy) in enumerate(_other_chips(x, y)):
            plan.append((sums.at[2 * ox + oy], land.at[k], (ox, oy, c), land.at[k]))
    return plan


def _sibling_plan(bufs):
    x, y, c = _place()
    n = len(bufs) // 2
    return [(p.at[:, _half(p, 1 - c)], land, (x, y, 1 - c), land)
            for p, land in zip(bufs[:n], bufs[n:])]


SEM_SPEC = pl.BlockSpec(memory_space=pltpu.SEMAPHORE)
ANY_SPEC = pl.BlockSpec(memory_space=pl.ANY)
DATAFLOW = pltpu.SideEffectType.DATAFLOW_SIDE_EFFECTING


def _in_hbm(a):
    return pltpu.with_memory_space_constraint(a, pltpu.HBM)


def _exchange_start(bufs, after, plan, n_copies, name):
    nb = len(bufs)

    def body(*refs):
        send_sems, recv_sems = refs[nb + 1], refs[nb + 2]
        for i, (src, dst, dev, _) in enumerate(plan(refs[:nb])):
            pltpu.make_async_remote_copy(
                src_ref=src, dst_ref=dst, send_sem=send_sems.at[i], recv_sem=recv_sems.at[i],
                device_id=dev, device_id_type=MESH).start()
        token = refs[-1]
        token[...] = jnp.zeros_like(token)

    res = pl.pallas_call(
        body, name=name,
        in_specs=[HBM_SPEC] * nb + [ANY_SPEC],
        out_specs=[SEM_SPEC, SEM_SPEC] + [HBM_SPEC] * nb + [pl.BlockSpec(memory_space=pltpu.VMEM)],
        out_shape=[pltpu.SemaphoreType.DMA((n_copies,)), pltpu.SemaphoreType.DMA((n_copies,))]
        + [pltpu.HBM(b.shape, b.dtype) for b in bufs] + [jax.ShapeDtypeStruct((8, 128), F32)],
        input_output_aliases={i: 2 + i for i in range(nb)},
        compiler_params=pltpu.CompilerParams(has_side_effects=DATAFLOW),
    )(*[_in_hbm(b) for b in bufs], after)
    return res[0], res[1], list(res[2:2 + nb]), res[-1]


def _exchange_wait(bufs, send_sems, recv_sems, after, plan, name):
    nb = len(bufs)

    def body(*refs):
        sends, recvs = refs[nb], refs[nb + 1]
        for i, (src, dst, dev, landing) in enumerate(plan(refs[:nb])):
            pltpu.make_async_remote_copy(
                src_ref=src, dst_ref=landing, send_sem=sends.at[i], recv_sem=recvs.at[i],
                device_id=dev, device_id_type=MESH).wait()

    res = pl.pallas_call(
        body, name=name,
        in_specs=[HBM_SPEC] * nb + [SEM_SPEC, SEM_SPEC, ANY_SPEC],
        out_specs=[HBM_SPEC] * nb,
        out_shape=[pltpu.HBM(b.shape, b.dtype) for b in bufs],
        input_output_aliases={i: i for i in range(nb)},
        compiler_params=pltpu.CompilerParams(has_side_effects=DATAFLOW),
    )(*bufs, send_sems, recv_sems, after)
    return list(res)


def _allgather_weights(slots, name, landed=False):
    n = len(slots)

    def body(*refs):
        outs = refs[n:2 * n]
        send_sems, recv_sems, fwd_send, fwd_recv = refs[2 * n:]
        x, y, c = _place()
        chips = _other_chips(x, y)

        def landing(a, chip, half_of):
            return outs[a].at[2 * chip[0] + chip[1], _half(outs[a], half_of)]

        def ici(a, k, chip_from, to):
            return pltpu.make_async_remote_copy(
                src_ref=landing(a, chip_from, c), dst_ref=landing(a, chip_from, c),
                send_sem=send_sems.at[a, k], recv_sem=recv_sems.at[a, k],
                device_id=to, device_id_type=MESH)

        def d2d(a, k, chip_from, half_of):
            return pltpu.make_async_remote_copy(
                src_ref=landing(a, chip_from, half_of), dst_ref=landing(a, chip_from, half_of),
                send_sem=fwd_send.at[a, k], recv_sem=fwd_recv.at[a, k],
                device_id=(x, y, 1 - c), device_id_type=MESH)

        sends = []
        if not landed:
            sends = [ici(a, k, (x, y), (*chips[k], c)) for a in range(n) for k in range(3)]
        for cp in sends:
            cp.start()
        forwards = []
        for a in range(n):
            for k in range(3):
                if not landed:
                    ici(a, k, chips[k], (x, y, c)).wait_recv()
                fw = d2d(a, k, chips[k], c)
                fw.start()
                forwards.append(fw)
        for a in range(n):
            for k in range(3):
                d2d(a, k, chips[k], 1 - c).wait_recv()
        for cp in sends + forwards:
            cp.wait_send()

    return pl.pallas_call(
        body, name=name,
        in_specs=[HBM_SPEC] * n, out_specs=[HBM_SPEC] * n,
        out_shape=[jax.ShapeDtypeStruct(s.shape, s.dtype) for s in slots],
        input_output_aliases={a: a for a in range(n)},
        scratch_shapes=[pltpu.SemaphoreType.DMA((n, 3)), pltpu.SemaphoreType.DMA((n, 3)),
                        pltpu.SemaphoreType.DMA((n, 3)), pltpu.SemaphoreType.DMA((n, 3))],
    )(*slots)


def _sibling_exchange(partials, name):
    n = len(partials)

    def body(*refs):
        ins, outs = refs[:n], refs[n:2 * n]
        send_sems, recv_sems = refs[2 * n:]
        x, y, c = _place()
        sends = [pltpu.make_async_remote_copy(
            src_ref=ins[a].at[:, _half(ins[a], 1 - c)], dst_ref=outs[a],
            send_sem=send_sems.at[a], recv_sem=recv_sems.at[a],
            device_id=(x, y, 1 - c), device_id_type=MESH) for a in range(n)]
        for cp in sends:
            cp.start()
        for cp in sends:
            cp.wait()

    return pl.pallas_call(
        body, name=name,
        in_specs=[HBM_SPEC] * n, out_specs=[HBM_SPEC] * n,
        out_shape=[jax.ShapeDtypeStruct((N_CHIPS, p.shape[1] // 2, p.shape[2]), F32)
                   for p in partials],
        scratch_shapes=[pltpu.SemaphoreType.DMA((n,)), pltpu.SemaphoreType.DMA((n,))],
    )(*partials)


def _chip_sum(partial, from_sibling, c, name, tr=256):
    _, hr, C = from_sibling.shape
    nb = hr // tr

    def body(c_ref, p_ref, s_ref, o_ref):
        o_ref[...] = (p_ref[...] + s_ref[...]).astype(BF16)

    return pl.pallas_call(
        body, name=name,
        grid_spec=pltpu.PrefetchScalarGridSpec(
            num_scalar_prefetch=1, grid=(N_CHIPS, nb),
            in_specs=[pl.BlockSpec((1, tr, C), lambda j, i, c_ref: (j, c_ref[0] * nb + i, 0)),
                      pl.BlockSpec((1, tr, C), lambda j, i, c_ref: (j, i, 0))],
            out_specs=pl.BlockSpec((1, tr, C), lambda j, i, c_ref: (j, i, 0))),
        out_shape=jax.ShapeDtypeStruct(from_sibling.shape, BF16),
        compiler_params=_params(2),
    )(c, partial, from_sibling)


def _reduce_half(partial, from_sibling, received, place, name, tr=256):
    _, hr, C = from_sibling.shape
    nb = hr // tr

    def body(p_ref, mine_ref, sib_ref, r_ref, o_ref):
        acc = mine_ref[0] + sib_ref[0]
        for k in range(3):
            acc = acc + r_ref[k].astype(F32)
        o_ref[...] = acc

    return pl.pallas_call(
        body, name=name,
        grid_spec=pltpu.PrefetchScalarGridSpec(
            num_scalar_prefetch=1, grid=(nb,),
            in_specs=[pl.BlockSpec((1, tr, C), lambda i, p: (p[0], p[1] * nb + i, 0)),
                      pl.BlockSpec((1, tr, C), lambda i, p: (p[0], i, 0)),
                      pl.BlockSpec((3, tr, C), lambda i, p: (0, i, 0))],
            out_specs=pl.BlockSpec((tr, C), lambda i, p: (p[1] * nb + i, 0))),
        out_shape=jax.ShapeDtypeStruct((2 * hr, C), F32),
        compiler_params=_params(1),
    )(place, partial, from_sibling, received)


def _join_halves(fulls, small, name):
    n = len(fulls)

    def body(*refs):
        small_ref, outs, small_all = refs[n], refs[n + 1:2 * n + 1], refs[2 * n + 1]
        send_sems, recv_sems, s_send, s_recv, loc_sem = refs[2 * n + 2:]
        x, y, c = _place()
        me = 4 * x + 2 * y + c

        def copy(a, half_of, to):
            rows = outs[a].at[_half(outs[a], half_of)]
            return pltpu.make_async_remote_copy(
                src_ref=rows, dst_ref=rows, send_sem=send_sems.at[a], recv_sem=recv_sems.at[a],
                device_id=to, device_id_type=MESH)

        def small_copy(d, landing, to):
            return pltpu.make_async_remote_copy(
                src_ref=small_ref, dst_ref=small_all.at[landing],
                send_sem=s_send.at[d - 1], recv_sem=s_recv.at[d - 1],
                device_id=to, device_id_type=MESH)

        local = pltpu.make_async_copy(small_ref, small_all.at[me], loc_sem)
        local.start()
        sends = [copy(a, c, (x, y, 1 - c)) for a in range(n)]
        sends += [small_copy(d, me, (x ^ ((d >> 2) & 1), y ^ ((d >> 1) & 1), c ^ (d & 1)))
                  for d in range(1, N_DEV)]
        for cp in sends:
            cp.start()
        for d in range(1, N_DEV):
            small_copy(d, me ^ d, (x, y, c)).wait_recv()
        for a in range(n):
            copy(a, 1 - c, (x, y, c)).wait_recv()
        for cp in sends:
            cp.wait_send()
        local.wait()

    res = pl.pallas_call(
        body, name=name,
        in_specs=[HBM_SPEC] * (n + 1), out_specs=[HBM_SPEC] * (n + 1),
        out_shape=[jax.ShapeDtypeStruct(f.shape, F32) for f in fulls]
        + [jax.ShapeDtypeStruct((N_DEV,) + small.shape, F32)],
        input_output_aliases={a: a for a in range(n)},
        scratch_shapes=[pltpu.SemaphoreType.DMA((n,)), pltpu.SemaphoreType.DMA((n,)),
                        pltpu.SemaphoreType.DMA((N_DEV - 1,)), pltpu.SemaphoreType.DMA((N_DEV - 1,)),
                        pltpu.SemaphoreType.DMA],
    )(*fulls, small)
    return list(res[:n]), res[n]


def _adamw_math(w, g, m, v):
    m = ADAM_B1 * m + (1.0 - ADAM_B1) * g
    v = ADAM_B2 * v + (1.0 - ADAM_B2) * (g * g)
    m_hat = m / (1.0 - ADAM_B1 ** ADAM_STEP)
    v_hat = v / (1.0 - ADAM_B2 ** ADAM_STEP)
    delta = -ADAM_LR * (m_hat / (jnp.sqrt(v_hat) + ADAM_EPS) + ADAM_WD * w)
    return delta, m, v


def _adamw(w, g, m, v, name, tr=256):
    R, C = w.shape
    tr = min(tr, R)

    def body(w_ref, g_ref, m_ref, v_ref, d_out, m_out, v_out):
        d_out[...], m_out[...], v_out[...] = _adamw_math(w_ref[...], g_ref[...], m_ref[...], v_ref[...])

    spec = pl.BlockSpec((tr, C), lambda i: (i, 0))
    return pl.pallas_call(
        body, name=name, grid=(R // tr,),
        in_specs=[spec] * 4, out_specs=[spec] * 3,
        out_shape=[jax.ShapeDtypeStruct((R, C), F32)] * 3,
        compiler_params=_params(1),
    )(w, g, m, v)


def _adamw_small(small_all, w, m, v, name):
    def body(s_ref, w_ref, m_ref, v_ref, g_out, d_out, m_out, v_out):
        g = s_ref[0]
        for d in range(1, N_DEV):
            g = g + s_ref[d]
        g_out[...] = g
        d_out[...], m_out[...], v_out[...] = _adamw_math(w_ref[...], g, m_ref[...], v_ref[...])

    vm = pl.BlockSpec(memory_space=pltpu.VMEM)
    return pl.pallas_call(
        body, name=name, in_specs=[vm] * 4, out_specs=[vm] * 4,
        out_shape=[jax.ShapeDtypeStruct(w.shape, F32)] * 4,
    )(small_all, w, m, v)


def _pack_small(norm_g, pool_scale, norm_f, extra_row):
    return jnp.concatenate([norm_g.reshape(2, D_MODEL), pool_scale.reshape(2, D_MODEL),
                            norm_f.reshape(1, D_MODEL), extra_row,
                            jnp.zeros((2, D_MODEL), F32)], axis=0)


def kernel(x, norm_g, pool_w_in, pool_w, pool_scale, pool_w_out, sb_w_in, sb_w_out, norm_f, loss_target, m_norm_g, m_pool_w_in, m_pool_w, m_pool_scale, m_pool_w_out, m_sb_w_in, m_sb_w_out, m_norm_f, v_norm_g, v_pool_w_in, v_pool_w, v_pool_scale, v_pool_w_out, v_sb_w_in, v_sb_w_out, v_norm_f):
    nb, S, _ = x.shape
    T = nb * S
    xt = x.reshape(T, D_MODEL)
    target = loss_target.reshape(T, D_MODEL)
    cx, cy, cc = _place()

    def shard2d(w):
        return w.reshape(-1, w.shape[-1])

    names = ("pool_w_in", "pool_w", "pool_w_out", "sb_w_in", "sb_w_out")
    w_shards = [shard2d(w) for w in (pool_w_in, pool_w, pool_w_out, sb_w_in, sb_w_out)]
    m_shards = [shard2d(w) for w in (m_pool_w_in, m_pool_w, m_pool_w_out, m_sb_w_in, m_sb_w_out)]
    v_shards = [shard2d(w) for w in (v_pool_w_in, v_pool_w, v_pool_w_out, v_sb_w_in, v_sb_w_out)]

    chip = (2 * cx + cy).reshape(1).astype(jnp.int32)
    c_arr = cc.reshape(1).astype(jnp.int32)
    place = jnp.stack([2 * cx + cy, cc]).astype(jnp.int32)
    slots = [_cast_to_slot(w, chip, "cast_" + nm) for w, nm in zip(w_shards, names)]
    g0, g1, gf = norm_g[0:1], norm_g[1:2], norm_f.reshape(1, D_MODEL)

    w_pin, = _allgather_weights(slots[:1], "allgather_pool_in_weights")
    mix_send, mix_recv, mix_slots, token = _exchange_start(slots[1:3], w_pin, _weight_plan, 6,
                                                           "pool_weights_start")
    sb_send, sb_recv, sb_slots, token = _exchange_start(slots[3:], token, _weight_plan, 6,
                                                        "sb_weights_start")

    proj0, u0 = _rms_matmul(xt, g0 + token[0:1, 0:1], w_pin, "pool_in_proj")
    mix_slots = _exchange_wait(mix_slots, mix_send, mix_recv, proj0, _weight_plan, "pool_weights_wait")
    w_g, w_pout = _allgather_weights(mix_slots, "pool_weights_forward", landed=True)
    w_pout = w_pout.reshape(D_INNER, D_MODEL)
    y0, pooled, mixed = _pool_fwd(proj0, w_g, pool_scale, S, "pool_mix")
    sb_slots = _exchange_wait(sb_slots, sb_send, sb_recv, y0, _weight_plan, "sb_weights_wait")
    w_sin, w_sout = _allgather_weights(sb_slots, "sb_weights_forward", landed=True)
    w_sout = w_sout.reshape(D_INNER, D_MODEL)
    h1 = _matmul_residual(y0, w_pout, xt, "pool_out_proj")
    n1 = 2 * QK_WIDTH + 2 * D_INNER
    qkvz, u1 = _rms_matmul(h1, g1, w_sin, "sb_in_proj")
    o, y1, ltot = _attn_fwd(qkvz, S, "sb_attention")
    dh2, d_norm_f, loss_row, dh2_b = _out_proj_loss_head(y1, w_sout, h1, gf, target,
                                                         "sb_out_proj_loss_head")

    def reduce_start(partials, tag, after=None, behind=None):
        n, done = len(partials), None
        after = c_arr if after is None else after
        if behind is None:
            from_sibling = list(_sibling_exchange(partials, "grad_sibling_exchange_" + tag))
        else:
            lands = [lax.empty((N_CHIPS, p.shape[1] // 2, p.shape[2]), F32) for p in partials]
            send, recv, bufs, token = _exchange_start(partials + lands, after, _sibling_plan, n,
                                                      "grad_sibling_start_" + tag)
            done = behind(token[0:1, 0:1])
            bufs = _exchange_wait(bufs, send, recv, done[0], _sibling_plan, "grad_sibling_wait_" + tag)
            partials, from_sibling, after = bufs[:n], bufs[n:], c_arr
        sums = [_chip_sum(p, s, c_arr, "grad_chip_sum_%s_%d" % (tag, i))
                for i, (p, s) in enumerate(zip(partials, from_sibling))]
        lands = [lax.empty((3,) + s.shape[1:], BF16) for s in sums]
        send, recv, bufs, token = _exchange_start(sums + lands, after, _chip_sum_plan, 3 * n,
                                                  "grad_chip_exchange_start_" + tag)
        return (partials, from_sibling, send, recv, bufs), token[0:1, 0:1], done

    def reduce_finish(started, after, tag):
        partials, from_sibling, send, recv, bufs = started
        received = _exchange_wait(bufs, send, recv, after, _chip_sum_plan,
                                  "grad_chip_exchange_wait_" + tag)[len(partials):]
        return [_reduce_half(p, s, r, place, "grad_reduce_%s_%d" % (tag, i))
                for i, (p, s, r) in enumerate(zip(partials, from_sibling, received))]

    shard = lambda i, j, t: (j, 0, 0)
    gw_sout = _matmul_tn(y1, dh2_b, D_INNER, D_MODEL, (D_INNER, D_MODEL), (1024, 1024),
                         lambda i, j, t: (i, j), "grad_sb_w_out", bm=1024, bn=1024)
    do, dproj1 = _attn_gate_bwd(dh2_b, w_sout, qkvz, o, "sb_gate_bwd")
    dproj1 = _attn_bwd(qkvz, do, ltot, dproj1, S, "sb_attention_bwd")
    gw_sin = _matmul_tn(u1, dproj1, D_MODEL, n1, (N_CHIPS, D_MODEL, n1 // 4), (1, D_MODEL, n1 // 4),
                        shard, "grad_sb_w_in", bm=D_MODEL, bn=n1 // 4)
    sb_started, token, (dh1, d_g1, dh1_b) = reduce_start(
        [gw_sin, gw_sout.reshape(N_CHIPS, -1, D_MODEL)], "sb",
        behind=lambda tok: _matmul_nt_rms_bwd(dproj1, w_sin, h1, g1 + tok, dh2, "sb_in_bwd", True,
                                              tk=w_sin.shape[2]))
    gw_pout = _matmul_tn(y0, dh1_b, D_INNER, D_MODEL, (D_INNER, D_MODEL), (1024, 1024),
                         lambda i, j, t: (i, j), "grad_pool_w_out", bm=1024, bn=1024)
    dmixed, dproj0, d_scale = _pool_gate_bwd(dh1_b, w_pout, proj0, mixed, pool_scale + token,
                                             "pool_gate_bwd")
    gw_g = _matmul_tn(pooled, dmixed, D_INNER, D_INNER, (N_CHIPS, GROUP_DIM, GROUP_DIM),
                      (N_CHIPS, GROUP_DIM // N_CHIPS, GROUP_DIM), lambda i, j, t: (0, i, 0),
                      "grad_pool_w", bm=GROUP_DIM, bn=GROUP_DIM, diagonal_blocks=True)
    mix_started, token, (dproj0,) = reduce_start(
        [gw_g, gw_pout.reshape(N_CHIPS, -1, D_MODEL)], "pool_mix",
        behind=lambda tok: (_pool_bwd(dmixed, w_g, dproj0, tok, S, "pool_bwd"),))
    n0 = 2 * D_INNER
    gw_pin = _matmul_tn(u0, dproj0, D_MODEL, n0, (N_CHIPS, D_MODEL, n0 // 4), (1, D_MODEL, n0 // 4),
                        shard, "grad_pool_w_in", bm=D_MODEL, bn=n0 // 4)
    pin_started, token, _ = reduce_start([gw_pin], "pool_in", after=token)
    dx, d_g0 = _matmul_nt_rms_bwd(dproj0, w_pin, xt, g0 + token, dh1, "pool_in_bwd", False,
                                  tk=w_pin.shape[2])

    small = _pack_small(jnp.concatenate([d_g0, d_g1], axis=0), d_scale, d_norm_f,
                        jnp.broadcast_to(loss_row[:, :1], (1, D_MODEL)))
    grads, small_all = _join_halves(reduce_finish(pin_started, dx, "pool_in")
                                    + reduce_finish(mix_started, dx, "pool_mix")
                                    + reduce_finish(sb_started, dx, "sb"), small, "grad_join_halves")

    deltas, new_m, new_v = [], [], []
    for w, g, m, v, nm in zip(w_shards, grads, m_shards, v_shards, names):
        d, mm, vv = _adamw(w, g, m, v, "adamw_" + nm)
        deltas.append(d)
        new_m.append(mm)
        new_v.append(vv)

    zero_row = jnp.zeros((1, D_MODEL), F32)
    g_small, d_small, m_small, v_small = _adamw_small(
        small_all, _pack_small(norm_g, pool_scale, norm_f, zero_row),
        _pack_small(m_norm_g, m_pool_scale, m_norm_f, zero_row),
        _pack_small(v_norm_g, v_pool_scale, v_norm_f, zero_row + 1.0), "adamw_small")
    loss = g_small[5, 0]

    def unpack_small(a):
        return a[0:2], a[2:4].reshape(1, D_INNER), a[4]

    def assemble(big, small3):
        ng, ps, nf = small3
        return [ng, big[0].reshape(pool_w_in.shape), big[1].reshape(pool_w.shape), ps,
                big[2].reshape(pool_w_out.shape), big[3].reshape(sb_w_in.shape),
                big[4].reshape(sb_w_out.shape), nf]

    return (loss, dx.reshape(x.shape),
            *assemble(grads, unpack_small(g_small)),
            *assemble(deltas, unpack_small(d_small)),
            *assemble(new_m, unpack_small(m_small)),
            *assemble(new_v, unpack_small(v_small)))
```

```python
import jax
import jax.numpy as jnp
from jax import lax
from jax.experimental import pallas as pl
from jax.experimental.pallas import tpu as pltpu

F32 = jnp.float32
BF16 = jnp.bfloat16
MESH = pl.DeviceIdType.MESH

D_MODEL = 1024
D_INNER = 2048
N_GROUPS = 4
GROUP_DIM = 512
HEAD_PAIR_QK = 128
HEAD_V = 128
QK_WIDTH = 1024
RMS_EPS = 1e-6
HALO = 16
N_CHIPS = 4
N_DEV = 8

ADAM_LR = 0.001
ADAM_B1 = 0.9
ADAM_B2 = 0.999
ADAM_EPS = 1e-08
ADAM_WD = 0.01
ADAM_STEP = 10

VMEM_LIMIT = 56 * 1024 * 1024

HBM_SPEC = pl.BlockSpec(memory_space=pltpu.HBM)


def _params(n_axes):
    return pltpu.CompilerParams(dimension_semantics=("arbitrary",) * n_axes,
                                vmem_limit_bytes=VMEM_LIMIT)


def _dot(a, b):
    return jnp.dot(a, b, preferred_element_type=F32)


def _dot_nt(a, b):
    return lax.dot_general(a, b, (((1,), (1,)), ((), ())), preferred_element_type=F32)


def _dot_tn(a, b):
    return lax.dot_general(a, b, (((0,), (0,)), ((), ())), preferred_element_type=F32)


def _sigmoid(z):
    return 1.0 / (1.0 + jnp.exp(-z))


def _row_blocks(tm, rows=256):
    return [slice(r, r + rows) for r in range(0, tm, rows)]


def _rms_matmul(h, g_row, w4, name, tm=1024):
    T = h.shape[0]
    n_shards, _, tn = w4.shape
    nm = T // tm

    def body(h_ref, g_ref, w_ref, o_ref, u_out, u_all):
        n, m = pl.program_id(0), pl.program_id(1)
        rows = pl.ds(pl.multiple_of(m * tm, tm), tm)

        @pl.when(n == 0)
        def _():
            x = h_ref[...]
            inv = lax.rsqrt(jnp.mean(x * x, axis=-1, keepdims=True) + RMS_EPS)
            u = (x * inv * g_ref[...]).astype(BF16)
            u_all[rows, :] = u
            u_out[...] = u

        o_ref[...] = _dot(u_all[rows, :], w_ref[0]).astype(BF16)

    return pl.pallas_call(
        body, name=name, grid=(n_shards, nm),
        in_specs=[pl.BlockSpec((tm, D_MODEL), lambda n, m: (jnp.where(n == 0, m, nm - 1), 0)),
                  pl.BlockSpec((1, D_MODEL), lambda n, m: (0, 0)),
                  pl.BlockSpec((1, D_MODEL, tn), lambda n, m: (n, 0, 0))],
        out_specs=[pl.BlockSpec((tm, tn), lambda n, m: (m, n)),
                   pl.BlockSpec((tm, D_MODEL), lambda n, m: (jnp.where(n == 0, m, nm - 1), 0))],
        out_shape=[jax.ShapeDtypeStruct((T, n_shards * tn), BF16),
                   jax.ShapeDtypeStruct((T, D_MODEL), BF16)],
        scratch_shapes=[pltpu.VMEM((T, D_MODEL), BF16)],
        compiler_params=_params(2),
    )(h, g_row, w4)


def _matmul_residual(a, w, res, name, tm=1024, tn=1024):
    T, K = a.shape
    N = w.shape[1]

    def body(a_ref, w_ref, r_ref, o_ref):
        o_ref[...] = r_ref[...] + _dot(a_ref[...], w_ref[...])

    return pl.pallas_call(
        body, name=name, grid=(T // tm, N // tn),
        in_specs=[pl.BlockSpec((tm, K), lambda m, n: (m, 0)),
                  pl.BlockSpec((K, tn), lambda m, n: (0, n)),
                  pl.BlockSpec((tm, tn), lambda m, n: (m, n))],
        out_specs=pl.BlockSpec((tm, tn), lambda m, n: (m, n)),
        out_shape=jax.ShapeDtypeStruct((T, N), F32),
        compiler_params=_params(2),
    )(a, w, res)


def _matmul_tn(a, b, a_cols, b_cols, out_shape, out_block, out_map, name, bm, bn, tk=2048,
               diagonal_blocks=False):
    T = a.shape[0]
    tk = min(tk, T)

    def body(a_ref, b_ref, o_ref):
        @pl.when(pl.program_id(2) == 0)
        def _():
            o_ref[...] = jnp.zeros_like(o_ref)

        part = _dot_tn(a_ref[...].astype(BF16), b_ref[...].astype(BF16))
        o_ref[...] += part.reshape(o_ref.shape)

    b_map = (lambda i, j, t: (t, i)) if diagonal_blocks else (lambda i, j, t: (t, j))
    return pl.pallas_call(
        body, name=name, grid=(a_cols // bm, 1 if diagonal_blocks else b_cols // bn, T // tk),
        in_specs=[pl.BlockSpec((tk, bm), lambda i, j, t: (t, i)),
                  pl.BlockSpec((tk, bn), b_map)],
        out_specs=pl.BlockSpec(out_block, out_map),
        out_shape=jax.ShapeDtypeStruct(out_shape, F32),
        compiler_params=_params(3),
    )(a, b)


def _matmul_nt_rms_bwd(dproj, w4, h, g_row, dres, name, with_bf16, tm=512):
    T, cols = dproj.shape
    nk, _, tk = w4.shape

    def body(dp_ref, w_ref, h_ref, g_ref, r_ref, dx_ref, dg_ref, *rest):
        acc_all = rest[-1]
        k, m = pl.program_id(0), pl.program_id(1)
        acc = acc_all.at[pl.ds(pl.multiple_of(m * tm, tm), tm), :]

        @pl.when(k == 0)
        def _():
            acc[...] = jnp.zeros_like(acc)

        @pl.when((k == 0) & (m == 0))
        def _():
            dg_ref[...] = jnp.zeros_like(dg_ref)

        acc[...] += _dot_nt(dp_ref[...], w_ref[0])

        @pl.when(k == nk - 1)
        def _():
            du = acc[...]
            x = h_ref[...]
            inv = lax.rsqrt(jnp.mean(x * x, axis=-1, keepdims=True) + RMS_EPS)
            xhat = x * inv
            dg_ref[...] += jnp.sum(du * xhat, axis=0, keepdims=True)
            dxh = du * g_ref[...]
            proj = jnp.mean(dxh * xhat, axis=-1, keepdims=True)
            dx = r_ref[...] + inv * (dxh - xhat * proj)
            dx_ref[...] = dx
            if with_bf16:
                rest[0][...] = dx.astype(BF16)

    rows = pl.BlockSpec((tm, D_MODEL), lambda k, m: (jnp.where(k == nk - 1, m, 0), 0))
    gain = pl.BlockSpec((1, D_MODEL), lambda k, m: (0, 0))
    return pl.pallas_call(
        body, name=name, grid=(nk, T // tm),
        in_specs=[pl.BlockSpec((tm, tk), lambda k, m: (m, k)),
                  pl.BlockSpec((1, D_MODEL, tk), lambda k, m: (k, 0, 0)),
                  rows, gain, rows],
        out_specs=[rows, gain] + [rows] * with_bf16,
        out_shape=[jax.ShapeDtypeStruct((T, D_MODEL), F32), jax.ShapeDtypeStruct((1, D_MODEL), F32)]
        + [jax.ShapeDtypeStruct((T, D_MODEL), BF16)] * with_bf16,
        scratch_shapes=[pltpu.VMEM((T, D_MODEL), F32)],
        compiler_params=_params(2),
    )(dproj, w4, h, g_row, dres)


def _window_of(g):
    return jnp.left_shift(2, g)


def _select_stage(g, stages):
    res = stages[0]
    for i in range(1, len(stages)):
        res = jnp.where(g >= i, stages[i], res)
    return res


def _pool_fwd(proj0, wg4, scale_row, S, name, tm=1024):
    T = proj0.shape[0]
    tm = min(tm, S)
    blocks_per_seq = S // tm
    hb = tm // HALO

    def body(x_ref, halo_ref, z_ref, w_ref, s_ref, y_ref, p_ref, mix_ref):
        m, g = pl.program_id(0), pl.program_id(1)
        first = (m % blocks_per_seq) == 0
        halo = jnp.where(first, 0.0, halo_ref[...].astype(F32))
        x = x_ref[...].astype(F32)
        ext = jnp.concatenate([halo, x], axis=0)
        stages = []
        cur = ext
        for sh in (1, 2, 4, 8):
            cur = cur + pltpu.roll(cur, sh, 0)
            stages.append(cur[HALO:, :])
        win_sum = _select_stage(g, stages)
        pos = (m % blocks_per_seq) * tm + lax.broadcasted_iota(jnp.int32, (tm, 1), 0)
        count = jnp.minimum(pos + 1, _window_of(g)).astype(F32)
        p_ref[...] = (win_sum / count - x).astype(BF16)
        w = w_ref[...].reshape(GROUP_DIM, GROUP_DIM)
        for rows in _row_blocks(tm):
            mixed = _dot(p_ref[rows, :], w)
            z = z_ref[rows, :].astype(F32)
            y_ref[rows, :] = (mixed * s_ref[...] * (z * _sigmoid(z))).astype(BF16)
            mix_ref[rows, :] = mixed.astype(BF16)

    blk = lambda m, g: (m, g)
    return pl.pallas_call(
        body, name=name, grid=(T // tm, N_GROUPS),
        in_specs=[pl.BlockSpec((tm, GROUP_DIM), blk),
                  pl.BlockSpec((HALO, GROUP_DIM), lambda m, g: (jnp.maximum(m * hb - 1, 0), g)),
                  pl.BlockSpec((tm, GROUP_DIM), lambda m, g: (m, N_GROUPS + g)),
                  pl.BlockSpec((N_CHIPS, GROUP_DIM // N_CHIPS, GROUP_DIM), lambda m, g: (0, g, 0)),
                  pl.BlockSpec((1, GROUP_DIM), lambda m, g: (0, g))],
        out_specs=[pl.BlockSpec((tm, GROUP_DIM), blk)] * 3,
        out_shape=[jax.ShapeDtypeStruct((T, D_INNER), BF16)] * 3,
        compiler_params=_params(2),
    )(proj0, proj0, proj0, wg4, scale_row)


def _pool_gate_bwd(dh, w_out, proj0, mixed, scale_row, name, tm=1024, tn=512):
    T = dh.shape[0]
    gate_b0 = D_INNER // tn

    def body(dh_ref, w_ref, z_ref, mix_ref, s_ref, dm_ref, dz_ref, ds_ref):
        m, n = pl.program_id(0), pl.program_id(1)

        @pl.when((m == 0) & (n == 0))
        def _():
            ds_ref[...] = jnp.zeros_like(ds_ref)

        cols = pl.ds(pl.multiple_of(n * tn, tn), tn)
        s = s_ref[...]
        ds = ds_ref[:, cols]
        for rows in _row_blocks(tm):
            dy = _dot_nt(dh_ref[rows, :], w_ref[...])
            z = z_ref[rows, :].astype(F32)
            sig = _sigmoid(z)
            silu = z * sig
            mixed = mix_ref[rows, :].astype(F32)
            dm_ref[rows, :] = (dy * s * silu).astype(BF16)
            dz_ref[rows, :] = (dy * mixed * s * (sig * (1.0 + z * (1.0 - sig)))).astype(BF16)
            ds = ds + jnp.sum(dy * mixed * silu, axis=0, keepdims=True)
        ds_ref[:, cols] = ds

    return pl.pallas_call(
        body, name=name, grid=(T // tm, D_INNER // tn),
        in_specs=[pl.BlockSpec((tm, D_MODEL), lambda m, n: (m, 0)),
                  pl.BlockSpec((tn, D_MODEL), lambda m, n: (n, 0)),
                  pl.BlockSpec((tm, tn), lambda m, n: (m, gate_b0 + n)),
                  pl.BlockSpec((tm, tn), lambda m, n: (m, n)),
                  pl.BlockSpec((1, tn), lambda m, n: (0, n))],
        out_specs=[pl.BlockSpec((tm, tn), lambda m, n: (m, n)),
                   pl.BlockSpec((tm, tn), lambda m, n: (m, gate_b0 + n)),
                   pl.BlockSpec((1, D_INNER), lambda m, n: (0, 0))],
        out_shape=[jax.ShapeDtypeStruct((T, D_INNER), BF16),
                   jax.ShapeDtypeStruct((T, 2 * D_INNER), BF16),
                   jax.ShapeDtypeStruct((1, D_INNER), F32)],
        compiler_params=_params(2),
    )(dh, w_out, proj0, mixed, scale_row)


def _pool_bwd(dmixed, wg4, dproj0, after, S, name, tm=1024):
    T = dmixed.shape[0]
    tm = min(tm, S)
    blocks_per_seq = S // tm
    hb = tm // HALO
    n_halo_blocks = T // HALO

    def body(dm_ref, halo_ref, w_ref, _, __, o_ref):
        m, g = pl.program_id(0), pl.program_id(1)
        ext = jnp.concatenate([dm_ref[...], halo_ref[...]], axis=0)
        dp = _dot_nt(ext, w_ref[...].reshape(GROUP_DIM, GROUP_DIM))
        pos = (m % blocks_per_seq) * tm + lax.broadcasted_iota(jnp.int32, (tm + HALO, 1), 0)
        count = jnp.minimum(pos + 1, _window_of(g)).astype(F32)
        c = jnp.where(pos < S, dp / count, 0.0)
        n = tm + HALO
        stages = []
        cur = c
        for sh in (1, 2, 4, 8):
            cur = cur + pltpu.roll(cur, n - sh, 0)
            stages.append(cur[:tm, :])
        o_ref[...] = (_select_stage(g, stages) - dp[:tm, :]).astype(BF16)

    blk = lambda m, g: (m, g)
    return pl.pallas_call(
        body, name=name, grid=(T // tm, N_GROUPS),
        in_specs=[pl.BlockSpec((tm, GROUP_DIM), blk),
                  pl.BlockSpec((HALO, GROUP_DIM),
                               lambda m, g: (jnp.minimum((m + 1) * hb, n_halo_blocks - 1), g)),
                  pl.BlockSpec((N_CHIPS, GROUP_DIM // N_CHIPS, GROUP_DIM), lambda m, g: (0, g, 0)),
                  HBM_SPEC, ANY_SPEC],
        out_specs=pl.BlockSpec((tm, GROUP_DIM), blk),
        out_shape=jax.ShapeDtypeStruct(dproj0.shape, dproj0.dtype),
        input_output_aliases={3: 0},
        compiler_params=_params(2),
    )(dmixed, dmixed, wg4, dproj0, after)


TQ = 256


def _split_dot(x, m):
    hi = x.astype(BF16)
    lo = (x - hi.astype(F32)).astype(BF16)
    return _dot(hi, m) + _dot(lo, m)


NEG_LOG2E = -1.4426950408889634


def _log_terms(z):
    soft = jnp.log(1.0 + jnp.exp2(jnp.abs(z) * NEG_LOG2E))
    log_beta = jnp.minimum(z, 0.0) - soft
    return log_beta, log_beta - z


N_HEADS = 16
FWD_HEADS = BWD_HEADS = 4


def _masked_heads(x, heads):
    lane = lax.broadcasted_iota(jnp.int32, (1, HEAD_PAIR_QK), 1)
    out = []
    for hh in range(heads):
        slab = x[:, (hh // 2) * HEAD_PAIR_QK:(hh // 2 + 1) * HEAD_PAIR_QK]
        out.append(jnp.where((lane // 64) == hh % 2, slab, jnp.zeros_like(slab)))
    return out


def _attn_fwd(qkvz, S, name):
    T = qkvz.shape[0]
    nq = S // TQ
    HEADS, QK_W, V_W = FWD_HEADS, FWD_HEADS * 64, FWD_HEADS * HEAD_V
    k_b0 = QK_WIDTH // QK_W
    v_b0 = 2 * QK_WIDTH // V_W
    z_b0 = (2 * QK_WIDTH + D_INNER) // V_W
    hs = range(HEADS)

    def body(q_ref, k_ref, v_ref, z_ref, o_ref, y_ref, lt_ref):
        row = lax.broadcasted_iota(jnp.int32, (TQ, TQ), 0)
        col = lax.broadcasted_iota(jnp.int32, (TQ, TQ), 1)
        causal = col < row
        later_in_block = (row > col).astype(BF16)
        lax.fori_loop(0, nq, lambda qi, _: q_block(qi, causal, later_in_block,
                                                   q_ref, k_ref, v_ref, z_ref, o_ref, y_ref, lt_ref), 0)

    def q_block(qi, causal, later_in_block, q_ref, k_ref, v_ref, z_ref, o_ref, y_ref, lt_ref):
        rows = pl.ds(pl.multiple_of(qi * TQ, TQ), TQ)
        qms = [qm * 0.125 for qm in _masked_heads(q_ref[rows, :], HEADS)]

        def step(j, carry, diagonal):
            koff = pl.multiple_of(j * TQ, TQ)
            kbs = [k_ref[pl.ds(koff, TQ), p * HEAD_PAIR_QK:(p + 1) * HEAD_PAIR_QK]
                   for p in range(HEADS // 2)]
            run, acc = [carry[2 * hh] for hh in hs], [carry[2 * hh + 1] for hh in hs]
            z = [_dot_nt(qms[hh], kbs[hh // 2]) for hh in hs]
            terms = [_log_terms(z[hh]) for hh in hs]
            log_om = [jnp.where(causal, t[1], 0.0) if diagonal else t[1] for t in terms]
            later = [_split_dot(log_om[hh], later_in_block) for hh in hs]
            a = [jnp.exp(terms[hh][0] + (run[hh] + later[hh])) for hh in hs]
            if diagonal:
                a = [jnp.where(causal, a[hh], 0.0) for hh in hs]
            out = []
            for hh in hs:
                vb = v_ref[pl.ds(koff, TQ), hh * HEAD_V:(hh + 1) * HEAD_V]
                out += [run[hh] + jnp.sum(log_om[hh], axis=1, keepdims=True),
                        acc[hh] + _dot(a[hh].astype(BF16), vb)]
            return tuple(out)

        zero = (jnp.zeros((TQ, 1), F32), jnp.zeros((TQ, HEAD_V), F32))
        carry = step(qi, zero * HEADS, True)
        carry = lax.fori_loop(0, qi, lambda i, c: step(qi - 1 - i, c, False), carry)
        for hh in hs:
            sl = slice(hh * HEAD_V, (hh + 1) * HEAD_V)
            acc = carry[2 * hh + 1]
            z = z_ref[rows, sl].astype(F32)
            o_ref[rows, sl] = acc.astype(BF16)
            y_ref[rows, sl] = (acc * (z * _sigmoid(z))).astype(BF16)
            lt_ref[rows, hh:hh + 1] = carry[2 * hh]
        return 0

    blk = lambda b, p: (b, p)
    return pl.pallas_call(
        body, name=name, grid=(T // S, N_HEADS // HEADS),
        in_specs=[pl.BlockSpec((S, QK_W), blk),
                  pl.BlockSpec((S, QK_W), lambda b, p: (b, k_b0 + p)),
                  pl.BlockSpec((S, V_W), lambda b, p: (b, v_b0 + p)),
                  pl.BlockSpec((S, V_W), lambda b, p: (b, z_b0 + p))],
        out_specs=[pl.BlockSpec((S, V_W), blk),
                   pl.BlockSpec((S, V_W), blk),
                   pl.BlockSpec((None, S, HEADS), lambda b, p: (p, b, 0))],
        out_shape=[jax.ShapeDtypeStruct((T, D_INNER), BF16),
                   jax.ShapeDtypeStruct((T, D_INNER), BF16),
                   jax.ShapeDtypeStruct((N_HEADS // HEADS, T, HEADS), F32)],
        compiler_params=_params(2),
    )(qkvz, qkvz, qkvz, qkvz)


def _attn_gate_bwd(dh, w_out, qkvz, o, name, tm=1024, tn=512):
    T = dh.shape[0]
    gate_b0 = (2 * QK_WIDTH + D_INNER) // tn

    def body(dh_ref, w_ref, z_ref, o_ref, do_ref, dz_ref):
        for rows in _row_blocks(tm):
            dy = _dot_nt(dh_ref[rows, :], w_ref[...])
            z = z_ref[rows, :].astype(F32)
            sig = _sigmoid(z)
            do_ref[rows, :] = (dy * (z * sig)).astype(BF16)
            dz_ref[rows, :] = (dy * o_ref[rows, :].astype(F32)
                               * (sig * (1.0 + z * (1.0 - sig)))).astype(BF16)

    return pl.pallas_call(
        body, name=name, grid=(T // tm, D_INNER // tn),
        in_specs=[pl.BlockSpec((tm, D_MODEL), lambda m, n: (m, 0)),
                  pl.BlockSpec((tn, D_MODEL), lambda m, n: (n, 0)),
                  pl.BlockSpec((tm, tn), lambda m, n: (m, gate_b0 + n)),
                  pl.BlockSpec((tm, tn), lambda m, n: (m, n))],
        out_specs=[pl.BlockSpec((tm, tn), lambda m, n: (m, n)),
                   pl.BlockSpec((tm, tn), lambda m, n: (m, gate_b0 + n))],
        out_shape=[jax.ShapeDtypeStruct((T, D_INNER), BF16),
                   jax.ShapeDtypeStruct((T, 2 * QK_WIDTH + 2 * D_INNER), BF16)],
        compiler_params=_params(2),
    )(dh, w_out, qkvz, o)


def _attn_bwd(qkv, do, ltot, dproj1, S, name):
    T = qkv.shape[0]
    nq = S // TQ
    HEADS, QK_W, V_W = BWD_HEADS, BWD_HEADS * 64, BWD_HEADS * HEAD_V
    k_b0 = QK_WIDTH // QK_W
    v_b0 = 2 * QK_WIDTH // V_W
    hs = range(HEADS)
    pairs = range(HEADS // 2)
    n_groups = N_HEADS // HEADS
    n_steps = (T // S) * n_groups

    def body(q_ref, k_ref, v_ref, do_ref, lt_ref, _, out_ref, dq_s, dk_s, dv_s, dkb_s, dvb_s, sems):
        b, p = pl.program_id(0), pl.program_id(1)
        row = lax.broadcasted_iota(jnp.int32, (TQ, TQ), 0)
        col = lax.broadcasted_iota(jnp.int32, (TQ, TQ), 1)
        causal = col < row
        upto = (row <= col).astype(BF16)
        before = (row < col).astype(BF16)
        dk_s[...] = jnp.zeros_like(dk_s)
        dv_s[...] = jnp.zeros_like(dv_s)
        rows = pl.ds(pl.multiple_of(b * S, TQ), S)
        copies = [
            pltpu.make_async_copy(
                dq_s, out_ref.at[rows, pl.ds(pl.multiple_of(p * QK_W, 128), QK_W)], sems.at[0]),
            pltpu.make_async_copy(
                dkb_s, out_ref.at[rows, pl.ds(pl.multiple_of(QK_WIDTH + p * QK_W, 128), QK_W)],
                sems.at[1]),
            pltpu.make_async_copy(
                dvb_s, out_ref.at[rows, pl.ds(pl.multiple_of(2 * QK_WIDTH + p * V_W, 128), V_W)],
                sems.at[2]),
        ]
        step = b * n_groups + p

        @pl.when(step > 0)
        def _():
            for cp in copies:
                cp.wait()

        def q_block(qi, _):
            qoff = pl.multiple_of(qi * TQ, TQ)
            qms = [qm * 0.125 for qm in _masked_heads(q_ref[pl.ds(qoff, TQ), :], HEADS)]
            vsl = [slice(hh * HEAD_V, (hh + 1) * HEAD_V) for hh in hs]
            psl = [slice(pp * HEAD_PAIR_QK, (pp + 1) * HEAD_PAIR_QK) for pp in pairs]
            do_h = [do_ref[pl.ds(qoff, TQ), sl] for sl in vsl]
            total = [lt_ref[pl.ds(qoff, TQ), hh:hh + 1] for hh in hs]

            def k_block(j, carry, diagonal):
                koff = pl.multiple_of(j * TQ, TQ)
                kms = _masked_heads(k_ref[pl.ds(koff, TQ), :], HEADS)
                g_before = [carry[2 * hh] for hh in hs]
                lom_before = [carry[2 * hh + 1] for hh in hs]
                z = [_dot_nt(qms[hh], kms[hh]) for hh in hs]
                da = [_dot_nt(do_h[hh], v_ref[pl.ds(koff, TQ), vsl[hh]]) for hh in hs]
                terms = [_log_terms(z[hh]) for hh in hs]
                log_om = [jnp.where(causal, t[1], 0.0) if diagonal else t[1] for t in terms]
                prefix = [_split_dot(log_om[hh], upto) for hh in hs]
                a = [jnp.exp(terms[hh][0] + ((total[hh] - lom_before[hh]) - prefix[hh])) for hh in hs]
                if diagonal:
                    a = [jnp.where(causal, a[hh], 0.0) for hh in hs]
                g = [a[hh] * da[hh] for hh in hs]
                g_prefix = [_dot(g[hh].astype(BF16), before) for hh in hs]
                out, dzs = [], []
                for hh in hs:
                    beta = jnp.exp(terms[hh][0])
                    g_excl = (g_before[hh] + g_prefix[hh]) * beta
                    if diagonal:
                        g_excl = jnp.where(causal, g_excl, 0.0)
                    dzs.append((g[hh] * (1.0 - beta) - g_excl).astype(BF16))
                    out += [g_before[hh] + jnp.sum(g[hh], axis=1, keepdims=True),
                            lom_before[hh] + jnp.sum(log_om[hh], axis=1, keepdims=True)]
                for hh in hs:
                    dv_s[pl.ds(koff, TQ), vsl[hh]] += _dot_tn(a[hh].astype(BF16), do_h[hh])
                dq = []
                for pp in pairs:
                    pair = slice(2 * pp, 2 * pp + 2)
                    dq.append(carry[2 * HEADS + pp] + _dot(jnp.concatenate(dzs[pair], axis=1),
                                                           jnp.concatenate(kms[pair], axis=0)))
                    dk_s[pl.ds(koff, TQ), psl[pp]] += _dot_tn(jnp.concatenate(dzs[pair], axis=0),
                                                              jnp.concatenate(qms[pair], axis=0))
                return tuple(out) + tuple(dq)

            zero = jnp.zeros((TQ, 1), F32)
            carry = (zero,) * (2 * HEADS) + (jnp.zeros((TQ, HEAD_PAIR_QK), F32),) * (HEADS // 2)
            carry = lax.fori_loop(0, qi, lambda j, c: k_block(j, c, False), carry)
            carry = k_block(qi, carry, True)
            for pp in pairs:
                dq_s[pl.ds(qoff, TQ), psl[pp]] = (carry[2 * HEADS + pp] * 0.125).astype(BF16)
            return 0

        lax.fori_loop(0, nq, q_block, 0)
        dkb_s[...] = dk_s[...].astype(BF16)
        dvb_s[...] = dv_s[...].astype(BF16)
        for cp in copies:
            cp.start()

        @pl.when(step == n_steps - 1)
        def _():
            for cp in copies:
                cp.wait()

    return pl.pallas_call(
        body, name=name, grid=(T // S, N_HEADS // HEADS),
        in_specs=[pl.BlockSpec((S, QK_W), lambda b, p: (b, p)),
                  pl.BlockSpec((S, QK_W), lambda b, p: (b, k_b0 + p)),
                  pl.BlockSpec((S, V_W), lambda b, p: (b, v_b0 + p)),
                  pl.BlockSpec((S, V_W), lambda b, p: (b, p)),
                  pl.BlockSpec((None, S, HEADS), lambda b, p: (p, b, 0)),
                  HBM_SPEC],
        out_specs=HBM_SPEC,
        out_shape=jax.ShapeDtypeStruct(dproj1.shape, dproj1.dtype),
        input_output_aliases={5: 0},
        scratch_shapes=[pltpu.VMEM((S, QK_W), BF16),
                        pltpu.VMEM((S, QK_W), F32),
                        pltpu.VMEM((S, V_W), F32),
                        pltpu.VMEM((S, QK_W), BF16),
                        pltpu.VMEM((S, V_W), BF16),
                        pltpu.SemaphoreType.DMA((3,))],
        compiler_params=_params(2),
    )(qkv, qkv, qkv, do, ltot, dproj1)


def _out_proj_loss_head(a, w, res, g_row, target, name, tm=512):
    T, K = a.shape

    def body(a_ref, w_ref, r_ref, g_ref, t_ref, dh_ref, dg_ref, loss_ref, dhb_ref):
        @pl.when(pl.program_id(0) == 0)
        def _():
            dg_ref[...] = jnp.zeros_like(dg_ref)
            loss_ref[...] = jnp.zeros_like(loss_ref)

        gain = g_ref[...]
        dg, loss = dg_ref[...], loss_ref[...]
        for rows in _row_blocks(tm):
            x = r_ref[rows, :] + _dot(a_ref[rows, :], w_ref[...])
            inv = lax.rsqrt(jnp.mean(x * x, axis=-1, keepdims=True) + RMS_EPS)
            xhat = x * inv
            err = xhat * gain - t_ref[rows, :]
            per_token = jnp.mean(err * err, axis=-1, keepdims=True)
            loss = loss + 0.5 * jnp.sum(per_token, axis=0, keepdims=True)
            dy = err * (1.0 / D_MODEL)
            dg = dg + jnp.sum(dy * xhat, axis=0, keepdims=True)
            dxh = dy * gain
            proj = jnp.mean(dxh * xhat, axis=-1, keepdims=True)
            dh = inv * (dxh - xhat * proj)
            dh_ref[rows, :] = dh
            dhb_ref[rows, :] = dh.astype(BF16)
        dg_ref[...] = dg
        loss_ref[...] = loss

    return pl.pallas_call(
        body, name=name, grid=(T // tm,),
        in_specs=[pl.BlockSpec((tm, K), lambda m: (m, 0)),
                  pl.BlockSpec((K, D_MODEL), lambda m: (0, 0)),
                  pl.BlockSpec((tm, D_MODEL), lambda m: (m, 0)),
                  pl.BlockSpec((1, D_MODEL), lambda m: (0, 0)),
                  pl.BlockSpec((tm, D_MODEL), lambda m: (m, 0))],
        out_specs=[pl.BlockSpec((tm, D_MODEL), lambda m: (m, 0)),
                   pl.BlockSpec((1, D_MODEL), lambda m: (0, 0)),
                   pl.BlockSpec((1, 128), lambda m: (0, 0)),
                   pl.BlockSpec((tm, D_MODEL), lambda m: (m, 0))],
        out_shape=[jax.ShapeDtypeStruct((T, D_MODEL), F32),
                   jax.ShapeDtypeStruct((1, D_MODEL), F32),
                   jax.ShapeDtypeStruct((1, 128), F32),
                   jax.ShapeDtypeStruct((T, D_MODEL), BF16)],
        compiler_params=_params(1),
    )(a, w, res, g_row, target)


def _place():
    return lax.axis_index("x"), lax.axis_index("y"), lax.axis_index("c")


def _other_chips(x, y):
    return [(1 - x, y), (x, 1 - y), (1 - x, 1 - y)]


def _half(ref, c):
    hr = ref.shape[-2] // 2
    return pl.ds(pl.multiple_of(c * hr, 8), hr)


def _cast_to_slot(shard, chip, name, tr=256):
    R, C = shard.shape

    def body(chip_ref, w_ref, o_ref):
        o_ref[0] = w_ref[...].astype(BF16)

    return pl.pallas_call(
        body, name=name,
        grid_spec=pltpu.PrefetchScalarGridSpec(
            num_scalar_prefetch=1, grid=(R // tr,),
            in_specs=[pl.BlockSpec((tr, C), lambda i, chip_ref: (i, 0))],
            out_specs=pl.BlockSpec((1, tr, C), lambda i, chip_ref: (chip_ref[0], i, 0))),
        out_shape=jax.ShapeDtypeStruct((N_CHIPS, R, C), BF16),
        compiler_params=_params(1),
    )(chip, shard)


def _weight_plan(bufs):
    x, y, c = _place()
    plan = []
    for buf in bufs:
        mine = buf.at[2 * x + y, _half(buf, c)]
        for ox, oy in _other_chips(x, y):
            plan.append((mine, mine, (ox, oy, c), buf.at[2 * ox + oy, _half(buf, c)]))
    return plan


def _chip_sum_plan(bufs):
    x, y, c = _place()
    n = len(bufs) // 2
    plan = []
    for sums, land in zip(bufs[:n], bufs[n:]):
        for k, (ox, oy) in enumerate(_other_chips(x, y)):
            plan.append((sums.at[2 * ox + oy], land.at[k], (ox, oy, c), land.at[k]))
    return plan


def _sibling_plan(bufs):
    x, y, c = _place()
    n = len(bufs) // 2
    return [(p.at[:, _half(p, 1 - c)], land, (x, y, 1 - c), land)
            for p, land in zip(bufs[:n], bufs[n:])]


SEM_SPEC = pl.BlockSpec(memory_space=pltpu.SEMAPHORE)
ANY_SPEC = pl.BlockSpec(memory_space=pl.ANY)
DATAFLOW = pltpu.SideEffectType.DATAFLOW_SIDE_EFFECTING


def _in_hbm(a):
    return pltpu.with_memory_space_constraint(a, pltpu.HBM)


def _exchange_start(bufs, after, plan, n_copies, name):
    nb = len(bufs)

    def body(*refs):
        send_sems, recv_sems = refs[nb + 1], refs[nb + 2]
        for i, (src, dst, dev, _) in enumerate(plan(refs[:nb])):
            pltpu.make_async_remote_copy(
                src_ref=src, dst_ref=dst, send_sem=send_sems.at[i], recv_sem=recv_sems.at[i],
                device_id=dev, device_id_type=MESH).start()
        token = refs[-1]
        token[...] = jnp.zeros_like(token)

    res = pl.pallas_call(
        body, name=name,
        in_specs=[HBM_SPEC] * nb + [ANY_SPEC],
        out_specs=[SEM_SPEC, SEM_SPEC] + [HBM_SPEC] * nb + [pl.BlockSpec(memory_space=pltpu.VMEM)],
        out_shape=[pltpu.SemaphoreType.DMA((n_copies,)), pltpu.SemaphoreType.DMA((n_copies,))]
        + [pltpu.HBM(b.shape, b.dtype) for b in bufs] + [jax.ShapeDtypeStruct((8, 128), F32)],
        input_output_aliases={i: 2 + i for i in range(nb)},
        compiler_params=pltpu.CompilerParams(has_side_effects=DATAFLOW),
    )(*[_in_hbm(b) for b in bufs], after)
    return res[0], res[1], list(res[2:2 + nb]), res[-1]


def _exchange_wait(bufs, send_sems, recv_sems, after, plan, name):
    nb = len(bufs)

    def body(*refs):
        sends, recvs = refs[nb], refs[nb + 1]
        for i, (src, dst, dev, landing) in enumerate(plan(refs[:nb])):
            pltpu.make_async_remote_copy(
                src_ref=src, dst_ref=landing, send_sem=sends.at[i], recv_sem=recvs.at[i],
                device_id=dev, device_id_type=MESH).wait()

    res = pl.pallas_call(
        body, name=name,
        in_specs=[HBM_SPEC] * nb + [SEM_SPEC, SEM_SPEC, ANY_SPEC],
        out_specs=[HBM_SPEC] * nb,
        out_shape=[pltpu.HBM(b.shape, b.dtype) for b in bufs],
        input_output_aliases={i: i for i in range(nb)},
        compiler_params=pltpu.CompilerParams(has_side_effects=DATAFLOW),
    )(*bufs, send_sems, recv_sems, after)
    return list(res)


def _allgather_weights(slots, name, landed=False):
    n = len(slots)

    def body(*refs):
        outs = refs[n:2 * n]
        send_sems, recv_sems, fwd_send, fwd_recv = refs[2 * n:]
        x, y, c = _place()
        chips = _other_chips(x, y)

        def landing(a, chip, half_of):
            return outs[a].at[2 * chip[0] + chip[1], _half(outs[a], half_of)]

        def ici(a, k, chip_from, to):
            return pltpu.make_async_remote_copy(
                src_ref=landing(a, chip_from, c), dst_ref=landing(a, chip_from, c),
                send_sem=send_sems.at[a, k], recv_sem=recv_sems.at[a, k],
                device_id=to, device_id_type=MESH)

        def d2d(a, k, chip_from, half_of):
            return pltpu.make_async_remote_copy(
                src_ref=landing(a, chip_from, half_of), dst_ref=landing(a, chip_from, half_of),
                send_sem=fwd_send.at[a, k], recv_sem=fwd_recv.at[a, k],
                device_id=(x, y, 1 - c), device_id_type=MESH)

        sends = []
        if not landed:
            sends = [ici(a, k, (x, y), (*chips[k], c)) for a in range(n) for k in range(3)]
        for cp in sends:
            cp.start()
        forwards = []
        for a in range(n):
            for k in range(3):
                if not landed:
                    ici(a, k, chips[k], (x, y, c)).wait_recv()
                fw = d2d(a, k, chips[k], c)
                fw.start()
                forwards.append(fw)
        for a in range(n):
            for k in range(3):
                d2d(a, k, chips[k], 1 - c).wait_recv()
        for cp in sends + forwards:
            cp.wait_send()

    return pl.pallas_call(
        body, name=name,
        in_specs=[HBM_SPEC] * n, out_specs=[HBM_SPEC] * n,
        out_shape=[jax.ShapeDtypeStruct(s.shape, s.dtype) for s in slots],
        input_output_aliases={a: a for a in range(n)},
        scratch_shapes=[pltpu.SemaphoreType.DMA((n, 3)), pltpu.SemaphoreType.DMA((n, 3)),
                        pltpu.SemaphoreType.DMA((n, 3)), pltpu.SemaphoreType.DMA((n, 3))],
    )(*slots)


def _sibling_exchange(partials, name):
    n = len(partials)

    def body(*refs):
        ins, outs = refs[:n], refs[n:2 * n]
        send_sems, recv_sems = refs[2 * n:]
        x, y, c = _place()
        sends = [pltpu.make_async_remote_copy(
            src_ref=ins[a].at[:, _half(ins[a], 1 - c)], dst_ref=outs[a],
            send_sem=send_sems.at[a], recv_sem=recv_sems.at[a],
            device_id=(x, y, 1 - c), device_id_type=MESH) for a in range(n)]
        for cp in sends:
            cp.start()
        for cp in sends:
            cp.wait()

    return pl.pallas_call(
        body, name=name,
        in_specs=[HBM_SPEC] * n, out_specs=[HBM_SPEC] * n,
        out_shape=[jax.ShapeDtypeStruct((N_CHIPS, p.shape[1] // 2, p.shape[2]), F32)
                   for p in partials],
        scratch_shapes=[pltpu.SemaphoreType.DMA((n,)), pltpu.SemaphoreType.DMA((n,))],
    )(*partials)


def _chip_sum(partial, from_sibling, c, name, tr=256):
    _, hr, C = from_sibling.shape
    nb = hr // tr

    def body(c_ref, p_ref, s_ref, o_ref):
        o_ref[...] = (p_ref[...] + s_ref[...]).astype(BF16)

    return pl.pallas_call(
        body, name=name,
        grid_spec=pltpu.PrefetchScalarGridSpec(
            num_scalar_prefetch=1, grid=(N_CHIPS, nb),
            in_specs=[pl.BlockSpec((1, tr, C), lambda j, i, c_ref: (j, c_ref[0] * nb + i, 0)),
                      pl.BlockSpec((1, tr, C), lambda j, i, c_ref: (j, i, 0))],
            out_specs=pl.BlockSpec((1, tr, C), lambda j, i, c_ref: (j, i, 0))),
        out_shape=jax.ShapeDtypeStruct(from_sibling.shape, BF16),
        compiler_params=_params(2),
    )(c, partial, from_sibling)


def _reduce_half(partial, from_sibling, received, place, name, tr=256):
    _, hr, C = from_sibling.shape
    nb = hr // tr

    def body(p_ref, mine_ref, sib_ref, r_ref, o_ref):
        acc = mine_ref[0] + sib_ref[0]
        for k in range(3):
            acc = acc + r_ref[k].astype(F32)
        o_ref[...] = acc

    return pl.pallas_call(
        body, name=name,
        grid_spec=pltpu.PrefetchScalarGridSpec(
            num_scalar_prefetch=1, grid=(nb,),
            in_specs=[pl.BlockSpec((1, tr, C), lambda i, p: (p[0], p[1] * nb + i, 0)),
                      pl.BlockSpec((1, tr, C), lambda i, p: (p[0], i, 0)),
                      pl.BlockSpec((3, tr, C), lambda i, p: (0, i, 0))],
            out_specs=pl.BlockSpec((tr, C), lambda i, p: (p[1] * nb + i, 0))),
        out_shape=jax.ShapeDtypeStruct((2 * hr, C), F32),
        compiler_params=_params(1),
    )(place, partial, from_sibling, received)


def _join_halves(fulls, small, name):
    n = len(fulls)

    def body(*refs):
        small_ref, outs, small_all = refs[n], refs[n + 1:2 * n + 1], refs[2 * n + 1]
        send_sems, recv_sems, s_send, s_recv, loc_sem = refs[2 * n + 2:]
        x, y, c = _place()
        me = 4 * x + 2 * y + c

        def copy(a, half_of, to):
            rows = outs[a].at[_half(outs[a], half_of)]
            return pltpu.make_async_remote_copy(
                src_ref=rows, dst_ref=rows, send_sem=send_sems.at[a], recv_sem=recv_sems.at[a],
                device_id=to, device_id_type=MESH)

        def small_copy(d, landing, to):
            return pltpu.make_async_remote_copy(
                src_ref=small_ref, dst_ref=small_all.at[landing],
                send_sem=s_send.at[d - 1], recv_sem=s_recv.at[d - 1],
                device_id=to, device_id_type=MESH)

        local = pltpu.make_async_copy(small_ref, small_all.at[me], loc_sem)
        local.start()
        sends = [copy(a, c, (x, y, 1 - c)) for a in range(n)]
        sends += [small_copy(d, me, (x ^ ((d >> 2) & 1), y ^ ((d >> 1) & 1), c ^ (d & 1)))
                  for d in range(1, N_DEV)]
        for cp in sends:
            cp.start()
        for d in range(1, N_DEV):
            small_copy(d, me ^ d, (x, y, c)).wait_recv()
        for a in range(n):
            copy(a, 1 - c, (x, y, c)).wait_recv()
        for cp in sends:
            cp.wait_send()
        local.wait()

    res = pl.pallas_call(
        body, name=name,
        in_specs=[HBM_SPEC] * (n + 1), out_specs=[HBM_SPEC] * (n + 1),
        out_shape=[jax.ShapeDtypeStruct(f.shape, F32) for f in fulls]
        + [jax.ShapeDtypeStruct((N_DEV,) + small.shape, F32)],
        input_output_aliases={a: a for a in range(n)},
        scratch_shapes=[pltpu.SemaphoreType.DMA((n,)), pltpu.SemaphoreType.DMA((n,)),
                        pltpu.SemaphoreType.DMA((N_DEV - 1,)), pltpu.SemaphoreType.DMA((N_DEV - 1,)),
                        pltpu.SemaphoreType.DMA],
    )(*fulls, small)
    return list(res[:n]), res[n]


def _adamw_math(w, g, m, v):
    m = ADAM_B1 * m + (1.0 - ADAM_B1) * g
    v = ADAM_B2 * v + (1.0 - ADAM_B2) * (g * g)
    m_hat = m / (1.0 - ADAM_B1 ** ADAM_STEP)
    v_hat = v / (1.0 - ADAM_B2 ** ADAM_STEP)
    delta = -ADAM_LR * (m_hat / (jnp.sqrt(v_hat) + ADAM_EPS) + ADAM_WD * w)
    return delta, m, v


def _adamw(w, g, m, v, name, tr=256):
    R, C = w.shape
    tr = min(tr, R)

    def body(w_ref, g_ref, m_ref, v_ref, d_out, m_out, v_out):
        d_out[...], m_out[...], v_out[...] = _adamw_math(w_ref[...], g_ref[...], m_ref[...], v_ref[...])

    spec = pl.BlockSpec((tr, C), lambda i: (i, 0))
    return pl.pallas_call(
        body, name=name, grid=(R // tr,),
        in_specs=[spec] * 4, out_specs=[spec] * 3,
        out_shape=[jax.ShapeDtypeStruct((R, C), F32)] * 3,
        compiler_params=_params(1),
    )(w, g, m, v)


def _adamw_small(small_all, w, m, v, name):
    def body(s_ref, w_ref, m_ref, v_ref, g_out, d_out, m_out, v_out):
        g = s_ref[0]
        for d in range(1, N_DEV):
            g = g + s_ref[d]
        g_out[...] = g
        d_out[...], m_out[...], v_out[...] = _adamw_math(w_ref[...], g, m_ref[...], v_ref[...])

    vm = pl.BlockSpec(memory_space=pltpu.VMEM)
    return pl.pallas_call(
        body, name=name, in_specs=[vm] * 4, out_specs=[vm] * 4,
        out_shape=[jax.ShapeDtypeStruct(w.shape, F32)] * 4,
    )(small_all, w, m, v)


def _pack_small(norm_g, pool_scale, norm_f, extra_row):
    return jnp.concatenate([norm_g.reshape(2, D_MODEL), pool_scale.reshape(2, D_MODEL),
                            norm_f.reshape(1, D_MODEL), extra_row,
                            jnp.zeros((2, D_MODEL), F32)], axis=0)


def kernel(x, norm_g, pool_w_in, pool_w, pool_scale, pool_w_out, sb_w_in, sb_w_out, norm_f, loss_target, m_norm_g, m_pool_w_in, m_pool_w, m_pool_scale, m_pool_w_out, m_sb_w_in, m_sb_w_out, m_norm_f, v_norm_g, v_pool_w_in, v_pool_w, v_pool_scale, v_pool_w_out, v_sb_w_in, v_sb_w_out, v_norm_f):
    nb, S, _ = x.shape
    T = nb * S
    xt = x.reshape(T, D_MODEL)
    target = loss_target.reshape(T, D_MODEL)
    cx, cy, cc = _place()

    def shard2d(w):
        return w.reshape(-1, w.shape[-1])

    names = ("pool_w_in", "pool_w", "pool_w_out", "sb_w_in", "sb_w_out")
    w_shards = [shard2d(w) for w in (pool_w_in, pool_w, pool_w_out, sb_w_in, sb_w_out)]
    m_shards = [shard2d(w) for w in (m_pool_w_in, m_pool_w, m_pool_w_out, m_sb_w_in, m_sb_w_out)]
    v_shards = [shard2d(w) for w in (v_pool_w_in, v_pool_w, v_pool_w_out, v_sb_w_in, v_sb_w_out)]

    chip = (2 * cx + cy).reshape(1).astype(jnp.int32)
    c_arr = cc.reshape(1).astype(jnp.int32)
    place = jnp.stack([2 * cx + cy, cc]).astype(jnp.int32)
    slots = [_cast_to_slot(w, chip, "cast_" + nm) for w, nm in zip(w_shards, names)]
    g0, g1, gf = norm_g[0:1], norm_g[1:2], norm_f.reshape(1, D_MODEL)

    w_pin, = _allgather_weights(slots[:1], "allgather_pool_in_weights")
    mix_send, mix_recv, mix_slots, token = _exchange_start(slots[1:3], w_pin, _weight_plan, 6,
                                                           "pool_weights_start")
    sb_send, sb_recv, sb_slots, token = _exchange_start(slots[3:], token, _weight_plan, 6,
                                                        "sb_weights_start")

    proj0, u0 = _rms_matmul(xt, g0 + token[0:1, 0:1], w_pin, "pool_in_proj")
    mix_slots = _exchange_wait(mix_slots, mix_send, mix_recv, proj0, _weight_plan, "pool_weights_wait")
    w_g, w_pout = _allgather_weights(mix_slots, "pool_weights_forward", landed=True)
    w_pout = w_pout.reshape(D_INNER, D_MODEL)
    y0, pooled, mixed = _pool_fwd(proj0, w_g, pool_scale, S, "pool_mix")
    sb_slots = _exchange_wait(sb_slots, sb_send, sb_recv, y0, _weight_plan, "sb_weights_wait")
    w_sin, w_sout = _allgather_weights(sb_slots, "sb_weights_forward", landed=True)
    w_sout = w_sout.reshape(D_INNER, D_MODEL)
    h1 = _matmul_residual(y0, w_pout, xt, "pool_out_proj")
    n1 = 2 * QK_WIDTH + 2 * D_INNER
    qkvz, u1 = _rms_matmul(h1, g1, w_sin, "sb_in_proj")
    o, y1, ltot = _attn_fwd(qkvz, S, "sb_attention")
    dh2, d_norm_f, loss_row, dh2_b = _out_proj_loss_head(y1, w_sout, h1, gf, target,
                                                         "sb_out_proj_loss_head")

    def reduce_start(partials, tag, after=None, behind=None):
        n, done = len(partials), None
        after = c_arr if after is None else after
        if behind is None:
            from_sibling = list(_sibling_exchange(partials, "grad_sibling_exchange_" + tag))
        else:
            lands = [lax.empty((N_CHIPS, p.shape[1] // 2, p.shape[2]), F32) for p in partials]
            send, recv, bufs, token = _exchange_start(partials + lands, after, _sibling_plan, n,
                                                      "grad_sibling_start_" + tag)
            done = behind(token[0:1, 0:1])
            bufs = _exchange_wait(bufs, send, recv, done[0], _sibling_plan, "grad_sibling_wait_" + tag)
            partials, from_sibling, after = bufs[:n], bufs[n:], c_arr
        sums = [_chip_sum(p, s, c_arr, "grad_chip_sum_%s_%d" % (tag, i))
                for i, (p, s) in enumerate(zip(partials, from_sibling))]
        lands = [lax.empty((3,) + s.shape[1:], BF16) for s in sums]
        send, recv, bufs, token = _exchange_start(sums + lands, after, _chip_sum_plan, 3 * n,
                                                  "grad_chip_exchange_start_" + tag)
        return (partials, from_sibling, send, recv, bufs), token[0:1, 0:1], done

    def reduce_finish(started, after, tag):
        partials, from_sibling, send, recv, bufs = started
        received = _exchange_wait(bufs, send, recv, after, _chip_sum_plan,
                                  "grad_chip_exchange_wait_" + tag)[len(partials):]
        return [_reduce_half(p, s, r, place, "grad_reduce_%s_%d" % (tag, i))
                for i, (p, s, r) in enumerate(zip(partials, from_sibling, received))]

    shard = lambda i, j, t: (j, 0, 0)
    gw_sout = _matmul_tn(y1, dh2_b, D_INNER, D_MODEL, (D_INNER, D_MODEL), (1024, 1024),
                         lambda i, j, t: (i, j), "grad_sb_w_out", bm=1024, bn=1024)
    do, dproj1 = _attn_gate_bwd(dh2_b, w_sout, qkvz, o, "sb_gate_bwd")
    dproj1 = _attn_bwd(qkvz, do, ltot, dproj1, S, "sb_attention_bwd")
    gw_sin = _matmul_tn(u1, dproj1, D_MODEL, n1, (N_CHIPS, D_MODEL, n1 // 4), (1, D_MODEL, n1 // 4),
                        shard, "grad_sb_w_in", bm=D_MODEL, bn=n1 // 4)
    sb_started, token, (dh1, d_g1, dh1_b) = reduce_start(
        [gw_sin, gw_sout.reshape(N_CHIPS, -1, D_MODEL)], "sb",
        behind=lambda tok: _matmul_nt_rms_bwd(dproj1, w_sin, h1, g1 + tok, dh2, "sb_in_bwd", True))
    gw_pout = _matmul_tn(y0, dh1_b, D_INNER, D_MODEL, (D_INNER, D_MODEL), (1024, 1024),
                         lambda i, j, t: (i, j), "grad_pool_w_out", bm=1024, bn=1024)
    dmixed, dproj0, d_scale = _pool_gate_bwd(dh1_b, w_pout, proj0, mixed, pool_scale + token,
                                             "pool_gate_bwd")
    gw_g = _matmul_tn(pooled, dmixed, D_INNER, D_INNER, (N_CHIPS, GROUP_DIM, GROUP_DIM),
                      (N_CHIPS, GROUP_DIM // N_CHIPS, GROUP_DIM), lambda i, j, t: (0, i, 0),
                      "grad_pool_w", bm=GROUP_DIM, bn=GROUP_DIM, diagonal_blocks=True)
    mix_started, token, (dproj0,) = reduce_start(
        [gw_g, gw_pout.reshape(N_CHIPS, -1, D_MODEL)], "pool_mix",
        behind=lambda tok: (_pool_bwd(dmixed, w_g, dproj0, tok, S, "pool_bwd"),))
    n0 = 2 * D_INNER
    gw_pin = _matmul_tn(u0, dproj0, D_MODEL, n0, (N_CHIPS, D_MODEL, n0 // 4), (1, D_MODEL, n0 // 4),
                        shard, "grad_pool_w_in", bm=D_MODEL, bn=n0 // 4)
    pin_started, token, _ = reduce_start([gw_pin], "pool_in", after=token)
    dx, d_g0 = _matmul_nt_rms_bwd(dproj0, w_pin, xt, g0 + token, dh1, "pool_in_bwd", False)

    small = _pack_small(jnp.concatenate([d_g0, d_g1], axis=0), d_scale, d_norm_f,
                        jnp.broadcast_to(loss_row[:, :1], (1, D_MODEL)))
    grads, small_all = _join_halves(reduce_finish(pin_started, dx, "pool_in")
                                    + reduce_finish(mix_started, dx, "pool_mix")
                                    + reduce_finish(sb_started, dx, "sb"), small, "grad_join_halves")

    deltas, new_m, new_v = [], [], []
    for w, g, m, v, nm in zip(w_shards, grads, m_shards, v_shards, names):
        d, mm, vv = _adamw(w, g, m, v, "adamw_" + nm)
        deltas.append(d)
        new_m.append(mm)
        new_v.append(vv)

    zero_row = jnp.zeros((1, D_MODEL), F32)
    g_small, d_small, m_small, v_small = _adamw_small(
        small_all, _pack_small(norm_g, pool_scale, norm_f, zero_row),
        _pack_small(m_norm_g, m_pool_scale, m_norm_f, zero_row),
        _pack_small(v_norm_g, v_pool_scale, v_norm_f, zero_row + 1.0), "adamw_small")
    loss = g_small[5, 0]

    def unpack_small(a):
        return a[0:2], a[2:4].reshape(1, D_INNER), a[4]

    def assemble(big, small3):
        ng, ps, nf = small3
        return [ng, big[0].reshape(pool_w_in.shape), big[1].reshape(pool_w.shape), ps,
                big[2].reshape(pool_w_out.shape), big[3].reshape(sb_w_in.shape),
                big[4].reshape(sb_w_out.shape), nf]

    return (loss, dx.reshape(x.shape),
            *assemble(grads, unpack_small(g_small)),
            *assemble(deltas, unpack_small(d_small)),
            *assemble(new_m, unpack_small(m_small)),
            *assemble(new_v, unpack_small(v_small)))
```

```python
import jax
import jax.numpy as jnp
from jax import lax
from jax.experimental import pallas as pl
from jax.experimental.pallas import tpu as pltpu

F32 = jnp.float32
BF16 = jnp.bfloat16
MESH = pl.DeviceIdType.MESH

D_MODEL = 1024
D_INNER = 2048
N_GROUPS = 4
GROUP_DIM = 512
HEAD_PAIR_QK = 128
HEAD_V = 128
QK_WIDTH = 1024
RMS_EPS = 1e-6
HALO = 16
N_CHIPS = 4
N_DEV = 8

ADAM_LR = 0.001
ADAM_B1 = 0.9
ADAM_B2 = 0.999
ADAM_EPS = 1e-08
ADAM_WD = 0.01
ADAM_STEP = 10

VMEM_LIMIT = 56 * 1024 * 1024

HBM_SPEC = pl.BlockSpec(memory_space=pltpu.HBM)


def _params(n_axes):
    return pltpu.CompilerParams(dimension_semantics=("arbitrary",) * n_axes,
                                vmem_limit_bytes=VMEM_LIMIT)


def _dot(a, b):
    return jnp.dot(a, b, preferred_element_type=F32)


def _dot_nt(a, b):
    return lax.dot_general(a, b, (((1,), (1,)), ((), ())), preferred_element_type=F32)


def _dot_tn(a, b):
    return lax.dot_general(a, b, (((0,), (0,)), ((), ())), preferred_element_type=F32)


def _sigmoid(z):
    return 1.0 / (1.0 + jnp.exp(-z))


def _row_blocks(tm, rows=256):
    return [slice(r, r + rows) for r in range(0, tm, rows)]


def _rms_matmul(h, g_row, w4, name, tm=1024):
    T = h.shape[0]
    n_shards, _, tn = w4.shape
    nm = T // tm

    def body(h_ref, g_ref, w_ref, o_ref, u_out, u_all):
        n, m = pl.program_id(0), pl.program_id(1)
        rows = pl.ds(pl.multiple_of(m * tm, tm), tm)

        @pl.when(n == 0)
        def _():
            x = h_ref[...]
            inv = lax.rsqrt(jnp.mean(x * x, axis=-1, keepdims=True) + RMS_EPS)
            u = (x * inv * g_ref[...]).astype(BF16)
            u_all[rows, :] = u
            u_out[...] = u

        o_ref[...] = _dot(u_all[rows, :], w_ref[0]).astype(BF16)

    return pl.pallas_call(
        body, name=name, grid=(n_shards, nm),
        in_specs=[pl.BlockSpec((tm, D_MODEL), lambda n, m: (jnp.where(n == 0, m, nm - 1), 0)),
                  pl.BlockSpec((1, D_MODEL), lambda n, m: (0, 0)),
                  pl.BlockSpec((1, D_MODEL, tn), lambda n, m: (n, 0, 0))],
        out_specs=[pl.BlockSpec((tm, tn), lambda n, m: (m, n)),
                   pl.BlockSpec((tm, D_MODEL), lambda n, m: (jnp.where(n == 0, m, nm - 1), 0))],
        out_shape=[jax.ShapeDtypeStruct((T, n_shards * tn), BF16),
                   jax.ShapeDtypeStruct((T, D_MODEL), BF16)],
        scratch_shapes=[pltpu.VMEM((T, D_MODEL), BF16)],
        compiler_params=_params(2),
    )(h, g_row, w4)


def _matmul_residual(a, w, res, name, tm=1024, tn=1024):
    T, K = a.shape
    N = w.shape[1]

    def body(a_ref, w_ref, r_ref, o_ref):
        o_ref[...] = r_ref[...] + _dot(a_ref[...], w_ref[...])

    return pl.pallas_call(
        body, name=name, grid=(T // tm, N // tn),
        in_specs=[pl.BlockSpec((tm, K), lambda m, n: (m, 0)),
                  pl.BlockSpec((K, tn), lambda m, n: (0, n)),
                  pl.BlockSpec((tm, tn), lambda m, n: (m, n))],
        out_specs=pl.BlockSpec((tm, tn), lambda m, n: (m, n)),
        out_shape=jax.ShapeDtypeStruct((T, N), F32),
        compiler_params=_params(2),
    )(a, w, res)


def _matmul_tn(a, b, a_cols, b_cols, out_shape, out_block, out_map, name, bm, bn, tk=2048,
               diagonal_blocks=False):
    T = a.shape[0]
    tk = min(tk, T)

    def body(a_ref, b_ref, o_ref):
        @pl.when(pl.program_id(2) == 0)
        def _():
            o_ref[...] = jnp.zeros_like(o_ref)

        part = _dot_tn(a_ref[...].astype(BF16), b_ref[...].astype(BF16))
        o_ref[...] += part.reshape(o_ref.shape)

    b_map = (lambda i, j, t: (t, i)) if diagonal_blocks else (lambda i, j, t: (t, j))
    return pl.pallas_call(
        body, name=name, grid=(a_cols // bm, 1 if diagonal_blocks else b_cols // bn, T // tk),
        in_specs=[pl.BlockSpec((tk, bm), lambda i, j, t: (t, i)),
                  pl.BlockSpec((tk, bn), b_map)],
        out_specs=pl.BlockSpec(out_block, out_map),
        out_shape=jax.ShapeDtypeStruct(out_shape, F32),
        compiler_params=_params(3),
    )(a, b)


def _matmul_nt_rms_bwd(dproj, w4, h, g_row, dres, name, with_bf16, tm=512):
    T, cols = dproj.shape
    nk, _, tk = w4.shape

    def body(dp_ref, w_ref, h_ref, g_ref, r_ref, dx_ref, dg_ref, *rest):
        acc_all = rest[-1]
        k, m = pl.program_id(0), pl.program_id(1)
        acc = acc_all.at[pl.ds(pl.multiple_of(m * tm, tm), tm), :]

        @pl.when(k == 0)
        def _():
            acc[...] = jnp.zeros_like(acc)

        @pl.when((k == 0) & (m == 0))
        def _():
            dg_ref[...] = jnp.zeros_like(dg_ref)

        acc[...] += _dot_nt(dp_ref[...], w_ref[0])

        @pl.when(k == nk - 1)
        def _():
            du = acc[...]
            x = h_ref[...]
            inv = lax.rsqrt(jnp.mean(x * x, axis=-1, keepdims=True) + RMS_EPS)
            xhat = x * inv
            dg_ref[...] += jnp.sum(du * xhat, axis=0, keepdims=True)
            dxh = du * g_ref[...]
            proj = jnp.mean(dxh * xhat, axis=-1, keepdims=True)
            dx = r_ref[...] + inv * (dxh - xhat * proj)
            dx_ref[...] = dx
            if with_bf16:
                rest[0][...] = dx.astype(BF16)

    rows = pl.BlockSpec((tm, D_MODEL), lambda k, m: (jnp.where(k == nk - 1, m, 0), 0))
    gain = pl.BlockSpec((1, D_MODEL), lambda k, m: (0, 0))
    return pl.pallas_call(
        body, name=name, grid=(nk, T // tm),
        in_specs=[pl.BlockSpec((tm, tk), lambda k, m: (m, k)),
                  pl.BlockSpec((1, D_MODEL, tk), lambda k, m: (k, 0, 0)),
                  rows, gain, rows],
        out_specs=[rows, gain] + [rows] * with_bf16,
        out_shape=[jax.ShapeDtypeStruct((T, D_MODEL), F32), jax.ShapeDtypeStruct((1, D_MODEL), F32)]
        + [jax.ShapeDtypeStruct((T, D_MODEL), BF16)] * with_bf16,
        scratch_shapes=[pltpu.VMEM((T, D_MODEL), F32)],
        compiler_params=_params(2),
    )(dproj, w4, h, g_row, dres)


def _window_of(g):
    return jnp.left_shift(2, g)


def _select_stage(g, stages):
    res = stages[0]
    for i in range(1, len(stages)):
        res = jnp.where(g >= i, stages[i], res)
    return res


def _pool_fwd(proj0, wg4, scale_row, S, name, tm=2048):
    T = proj0.shape[0]
    tm = min(tm, S)
    blocks_per_seq = S // tm
    hb = tm // HALO

    def body(x_ref, halo_ref, z_ref, w_ref, s_ref, y_ref, p_ref, mix_ref):
        m, g = pl.program_id(0), pl.program_id(1)
        first = (m % blocks_per_seq) == 0
        halo = jnp.where(first, 0.0, halo_ref[...].astype(F32))
        x = x_ref[...].astype(F32)
        ext = jnp.concatenate([halo, x], axis=0)
        stages = []
        cur = ext
        for sh in (1, 2, 4, 8):
            cur = cur + pltpu.roll(cur, sh, 0)
            stages.append(cur[HALO:, :])
        win_sum = _select_stage(g, stages)
        pos = (m % blocks_per_seq) * tm + lax.broadcasted_iota(jnp.int32, (tm, 1), 0)
        count = jnp.minimum(pos + 1, _window_of(g)).astype(F32)
        p_ref[...] = (win_sum / count - x).astype(BF16)
        w = w_ref[...].reshape(GROUP_DIM, GROUP_DIM)
        for rows in _row_blocks(tm):
            mixed = _dot(p_ref[rows, :], w)
            z = z_ref[rows, :].astype(F32)
            y_ref[rows, :] = (mixed * s_ref[...] * (z * _sigmoid(z))).astype(BF16)
            mix_ref[rows, :] = mixed.astype(BF16)

    blk = lambda m, g: (m, g)
    return pl.pallas_call(
        body, name=name, grid=(T // tm, N_GROUPS),
        in_specs=[pl.BlockSpec((tm, GROUP_DIM), blk),
                  pl.BlockSpec((HALO, GROUP_DIM), lambda m, g: (jnp.maximum(m * hb - 1, 0), g)),
                  pl.BlockSpec((tm, GROUP_DIM), lambda m, g: (m, N_GROUPS + g)),
                  pl.BlockSpec((N_CHIPS, GROUP_DIM // N_CHIPS, GROUP_DIM), lambda m, g: (0, g, 0)),
                  pl.BlockSpec((1, GROUP_DIM), lambda m, g: (0, g))],
        out_specs=[pl.BlockSpec((tm, GROUP_DIM), blk)] * 3,
        out_shape=[jax.ShapeDtypeStruct((T, D_INNER), BF16)] * 3,
        compiler_params=_params(2),
    )(proj0, proj0, proj0, wg4, scale_row)


def _pool_gate_bwd(dh, w_out, proj0, mixed, scale_row, name, tm=1024, tn=512):
    T = dh.shape[0]
    gate_b0 = D_INNER // tn

    def body(dh_ref, w_ref, z_ref, mix_ref, s_ref, dm_ref, dz_ref, ds_ref):
        m, n = pl.program_id(0), pl.program_id(1)

        @pl.when((m == 0) & (n == 0))
        def _():
            ds_ref[...] = jnp.zeros_like(ds_ref)

        cols = pl.ds(pl.multiple_of(n * tn, tn), tn)
        s = s_ref[...]
        ds = ds_ref[:, cols]
        for rows in _row_blocks(tm):
            dy = _dot_nt(dh_ref[rows, :], w_ref[...])
            z = z_ref[rows, :].astype(F32)
            sig = _sigmoid(z)
            silu = z * sig
            mixed = mix_ref[rows, :].astype(F32)
            dm_ref[rows, :] = (dy * s * silu).astype(BF16)
            dz_ref[rows, :] = (dy * mixed * s * (sig * (1.0 + z * (1.0 - sig)))).astype(BF16)
            ds = ds + jnp.sum(dy * mixed * silu, axis=0, keepdims=True)
        ds_ref[:, cols] = ds

    return pl.pallas_call(
        body, name=name, grid=(T // tm, D_INNER // tn),
        in_specs=[pl.BlockSpec((tm, D_MODEL), lambda m, n: (m, 0)),
                  pl.BlockSpec((tn, D_MODEL), lambda m, n: (n, 0)),
                  pl.BlockSpec((tm, tn), lambda m, n: (m, gate_b0 + n)),
                  pl.BlockSpec((tm, tn), lambda m, n: (m, n)),
                  pl.BlockSpec((1, tn), lambda m, n: (0, n))],
        out_specs=[pl.BlockSpec((tm, tn), lambda m, n: (m, n)),
                   pl.BlockSpec((tm, tn), lambda m, n: (m, gate_b0 + n)),
                   pl.BlockSpec((1, D_INNER), lambda m, n: (0, 0))],
        out_shape=[jax.ShapeDtypeStruct((T, D_INNER), BF16),
                   jax.ShapeDtypeStruct((T, 2 * D_INNER), BF16),
                   jax.ShapeDtypeStruct((1, D_INNER), F32)],
        compiler_params=_params(2),
    )(dh, w_out, proj0, mixed, scale_row)


def _pool_bwd(dmixed, wg4, dproj0, after, S, name, tm=2048):
    T = dmixed.shape[0]
    tm = min(tm, S)
    blocks_per_seq = S // tm
    hb = tm // HALO
    n_halo_blocks = T // HALO

    def body(dm_ref, halo_ref, w_ref, _, __, o_ref):
        m, g = pl.program_id(0), pl.program_id(1)
        ext = jnp.concatenate([dm_ref[...], halo_ref[...]], axis=0)
        dp = _dot_nt(ext, w_ref[...].reshape(GROUP_DIM, GROUP_DIM))
        pos = (m % blocks_per_seq) * tm + lax.broadcasted_iota(jnp.int32, (tm + HALO, 1), 0)
        count = jnp.minimum(pos + 1, _window_of(g)).astype(F32)
        c = jnp.where(pos < S, dp / count, 0.0)
        n = tm + HALO
        stages = []
        cur = c
        for sh in (1, 2, 4, 8):
            cur = cur + pltpu.roll(cur, n - sh, 0)
            stages.append(cur[:tm, :])
        o_ref[...] = (_select_stage(g, stages) - dp[:tm, :]).astype(BF16)

    blk = lambda m, g: (m, g)
    return pl.pallas_call(
        body, name=name, grid=(T // tm, N_GROUPS),
        in_specs=[pl.BlockSpec((tm, GROUP_DIM), blk),
                  pl.BlockSpec((HALO, GROUP_DIM),
                               lambda m, g: (jnp.minimum((m + 1) * hb, n_halo_blocks - 1), g)),
                  pl.BlockSpec((N_CHIPS, GROUP_DIM // N_CHIPS, GROUP_DIM), lambda m, g: (0, g, 0)),
                  HBM_SPEC, ANY_SPEC],
        out_specs=pl.BlockSpec((tm, GROUP_DIM), blk),
        out_shape=jax.ShapeDtypeStruct(dproj0.shape, dproj0.dtype),
        input_output_aliases={3: 0},
        compiler_params=_params(2),
    )(dmixed, dmixed, wg4, dproj0, after)


TQ = 256


def _split_dot(x, m):
    hi = x.astype(BF16)
    lo = (x - hi.astype(F32)).astype(BF16)
    return _dot(hi, m) + _dot(lo, m)


NEG_LOG2E = -1.4426950408889634


def _log_terms(z):
    soft = jnp.log(1.0 + jnp.exp2(jnp.abs(z) * NEG_LOG2E))
    log_beta = jnp.minimum(z, 0.0) - soft
    return log_beta, log_beta - z


N_HEADS = 16
FWD_HEADS = BWD_HEADS = 4


def _masked_heads(x, heads):
    lane = lax.broadcasted_iota(jnp.int32, (1, HEAD_PAIR_QK), 1)
    out = []
    for hh in range(heads):
        slab = x[:, (hh // 2) * HEAD_PAIR_QK:(hh // 2 + 1) * HEAD_PAIR_QK]
        out.append(jnp.where((lane // 64) == hh % 2, slab, jnp.zeros_like(slab)))
    return out


def _attn_fwd(qkvz, S, name):
    T = qkvz.shape[0]
    nq = S // TQ
    HEADS, QK_W, V_W = FWD_HEADS, FWD_HEADS * 64, FWD_HEADS * HEAD_V
    k_b0 = QK_WIDTH // QK_W
    v_b0 = 2 * QK_WIDTH // V_W
    z_b0 = (2 * QK_WIDTH + D_INNER) // V_W
    hs = range(HEADS)

    def body(q_ref, k_ref, v_ref, z_ref, o_ref, y_ref, lt_ref):
        row = lax.broadcasted_iota(jnp.int32, (TQ, TQ), 0)
        col = lax.broadcasted_iota(jnp.int32, (TQ, TQ), 1)
        causal = col < row
        later_in_block = (row > col).astype(BF16)
        lax.fori_loop(0, nq, lambda qi, _: q_block(qi, causal, later_in_block,
                                                   q_ref, k_ref, v_ref, z_ref, o_ref, y_ref, lt_ref), 0)

    def q_block(qi, causal, later_in_block, q_ref, k_ref, v_ref, z_ref, o_ref, y_ref, lt_ref):
        rows = pl.ds(pl.multiple_of(qi * TQ, TQ), TQ)
        qms = [qm * 0.125 for qm in _masked_heads(q_ref[rows, :], HEADS)]

        def step(j, carry, diagonal):
            koff = pl.multiple_of(j * TQ, TQ)
            kbs = [k_ref[pl.ds(koff, TQ), p * HEAD_PAIR_QK:(p + 1) * HEAD_PAIR_QK]
                   for p in range(HEADS // 2)]
            run, acc = [carry[2 * hh] for hh in hs], [carry[2 * hh + 1] for hh in hs]
            z = [_dot_nt(qms[hh], kbs[hh // 2]) for hh in hs]
            terms = [_log_terms(z[hh]) for hh in hs]
            log_om = [jnp.where(causal, t[1], 0.0) if diagonal else t[1] for t in terms]
            later = [_split_dot(log_om[hh], later_in_block) for hh in hs]
            a = [jnp.exp(terms[hh][0] + (run[hh] + later[hh])) for hh in hs]
            if diagonal:
                a = [jnp.where(causal, a[hh], 0.0) for hh in hs]
            out = []
            for hh in hs:
                vb = v_ref[pl.ds(koff, TQ), hh * HEAD_V:(hh + 1) * HEAD_V]
                out += [run[hh] + jnp.sum(log_om[hh], axis=1, keepdims=True),
                        acc[hh] + _dot(a[hh].astype(BF16), vb)]
            return tuple(out)

        zero = (jnp.zeros((TQ, 1), F32), jnp.zeros((TQ, HEAD_V), F32))
        carry = step(qi, zero * HEADS, True)
        carry = lax.fori_loop(0, qi, lambda i, c: step(qi - 1 - i, c, False), carry)
        for hh in hs:
            sl = slice(hh * HEAD_V, (hh + 1) * HEAD_V)
            acc = carry[2 * hh + 1]
            z = z_ref[rows, sl].astype(F32)
            o_ref[rows, sl] = acc.astype(BF16)
            y_ref[rows, sl] = (acc * (z * _sigmoid(z))).astype(BF16)
            lt_ref[rows, hh:hh + 1] = carry[2 * hh]
        return 0

    blk = lambda b, p: (b, p)
    return pl.pallas_call(
        body, name=name, grid=(T // S, N_HEADS // HEADS),
        in_specs=[pl.BlockSpec((S, QK_W), blk),
                  pl.BlockSpec((S, QK_W), lambda b, p: (b, k_b0 + p)),
                  pl.BlockSpec((S, V_W), lambda b, p: (b, v_b0 + p)),
                  pl.BlockSpec((S, V_W), lambda b, p: (b, z_b0 + p))],
        out_specs=[pl.BlockSpec((S, V_W), blk),
                   pl.BlockSpec((S, V_W), blk),
                   pl.BlockSpec((None, S, HEADS), lambda b, p: (p, b, 0))],
        out_shape=[jax.ShapeDtypeStruct((T, D_INNER), BF16),
                   jax.ShapeDtypeStruct((T, D_INNER), BF16),
                   jax.ShapeDtypeStruct((N_HEADS // HEADS, T, HEADS), F32)],
        compiler_params=_params(2),
    )(qkvz, qkvz, qkvz, qkvz)


def _attn_gate_bwd(dh, w_out, qkvz, o, name, tm=1024, tn=512):
    T = dh.shape[0]
    gate_b0 = (2 * QK_WIDTH + D_INNER) // tn

    def body(dh_ref, w_ref, z_ref, o_ref, do_ref, dz_ref):
        for rows in _row_blocks(tm):
            dy = _dot_nt(dh_ref[rows, :], w_ref[...])
            z = z_ref[rows, :].astype(F32)
            sig = _sigmoid(z)
            do_ref[rows, :] = (dy * (z * sig)).astype(BF16)
            dz_ref[rows, :] = (dy * o_ref[rows, :].astype(F32)
                               * (sig * (1.0 + z * (1.0 - sig)))).astype(BF16)

    return pl.pallas_call(
        body, name=name, grid=(T // tm, D_INNER // tn),
        in_specs=[pl.BlockSpec((tm, D_MODEL), lambda m, n: (m, 0)),
                  pl.BlockSpec((tn, D_MODEL), lambda m, n: (n, 0)),
                  pl.BlockSpec((tm, tn), lambda m, n: (m, gate_b0 + n)),
                  pl.BlockSpec((tm, tn), lambda m, n: (m, n))],
        out_specs=[pl.BlockSpec((tm, tn), lambda m, n: (m, n)),
                   pl.BlockSpec((tm, tn), lambda m, n: (m, gate_b0 + n))],
        out_shape=[jax.ShapeDtypeStruct((T, D_INNER), BF16),
                   jax.ShapeDtypeStruct((T, 2 * QK_WIDTH + 2 * D_INNER), BF16)],
        compiler_params=_params(2),
    )(dh, w_out, qkvz, o)


def _attn_bwd(qkv, do, ltot, dproj1, S, name):
    T = qkv.shape[0]
    nq = S // TQ
    HEADS, QK_W, V_W = BWD_HEADS, BWD_HEADS * 64, BWD_HEADS * HEAD_V
    k_b0 = QK_WIDTH // QK_W
    v_b0 = 2 * QK_WIDTH // V_W
    hs = range(HEADS)
    pairs = range(HEADS // 2)
    n_groups = N_HEADS // HEADS
    n_steps = (T // S) * n_groups

    def body(q_ref, k_ref, v_ref, do_ref, lt_ref, _, out_ref, dq_s, dk_s, dv_s, dkb_s, dvb_s, sems):
        b, p = pl.program_id(0), pl.program_id(1)
        row = lax.broadcasted_iota(jnp.int32, (TQ, TQ), 0)
        col = lax.broadcasted_iota(jnp.int32, (TQ, TQ), 1)
        causal = col < row
        upto = (row <= col).astype(BF16)
        before = (row < col).astype(BF16)
        dk_s[...] = jnp.zeros_like(dk_s)
        dv_s[...] = jnp.zeros_like(dv_s)
        rows = pl.ds(pl.multiple_of(b * S, TQ), S)
        copies = [
            pltpu.make_async_copy(
                dq_s, out_ref.at[rows, pl.ds(pl.multiple_of(p * QK_W, 128), QK_W)], sems.at[0]),
            pltpu.make_async_copy(
                dkb_s, out_ref.at[rows, pl.ds(pl.multiple_of(QK_WIDTH + p * QK_W, 128), QK_W)],
                sems.at[1]),
            pltpu.make_async_copy(
                dvb_s, out_ref.at[rows, pl.ds(pl.multiple_of(2 * QK_WIDTH + p * V_W, 128), V_W)],
                sems.at[2]),
        ]
        step = b * n_groups + p

        @pl.when(step > 0)
        def _():
            for cp in copies:
                cp.wait()

        def q_block(qi, _):
            qoff = pl.multiple_of(qi * TQ, TQ)
            qms = [qm * 0.125 for qm in _masked_heads(q_ref[pl.ds(qoff, TQ), :], HEADS)]
            vsl = [slice(hh * HEAD_V, (hh + 1) * HEAD_V) for hh in hs]
            psl = [slice(pp * HEAD_PAIR_QK, (pp + 1) * HEAD_PAIR_QK) for pp in pairs]
            do_h = [do_ref[pl.ds(qoff, TQ), sl] for sl in vsl]
            total = [lt_ref[pl.ds(qoff, TQ), hh:hh + 1] for hh in hs]

            def k_block(j, carry, diagonal):
                koff = pl.multiple_of(j * TQ, TQ)
                kms = _masked_heads(k_ref[pl.ds(koff, TQ), :], HEADS)
                g_before = [carry[2 * hh] for hh in hs]
                lom_before = [carry[2 * hh + 1] for hh in hs]
                z = [_dot_nt(qms[hh], kms[hh]) for hh in hs]
                da = [_dot_nt(do_h[hh], v_ref[pl.ds(koff, TQ), vsl[hh]]) for hh in hs]
                terms = [_log_terms(z[hh]) for hh in hs]
                log_om = [jnp.where(causal, t[1], 0.0) if diagonal else t[1] for t in terms]
                prefix = [_split_dot(log_om[hh], upto) for hh in hs]
                a = [jnp.exp(terms[hh][0] + ((total[hh] - lom_before[hh]) - prefix[hh])) for hh in hs]
                if diagonal:
                    a = [jnp.where(causal, a[hh], 0.0) for hh in hs]
                g = [a[hh] * da[hh] for hh in hs]
                g_prefix = [_dot(g[hh].astype(BF16), before) for hh in hs]
                out, dzs = [], []
                for hh in hs:
                    beta = jnp.exp(terms[hh][0])
                    g_excl = (g_before[hh] + g_prefix[hh]) * beta
                    if diagonal:
                        g_excl = jnp.where(causal, g_excl, 0.0)
                    dzs.append((g[hh] * (1.0 - beta) - g_excl).astype(BF16))
                    out += [g_before[hh] + jnp.sum(g[hh], axis=1, keepdims=True),
                            lom_before[hh] + jnp.sum(log_om[hh], axis=1, keepdims=True)]
                for hh in hs:
                    dv_s[pl.ds(koff, TQ), vsl[hh]] += _dot_tn(a[hh].astype(BF16), do_h[hh])
                dq = []
                for pp in pairs:
                    pair = slice(2 * pp, 2 * pp + 2)
                    dq.append(carry[2 * HEADS + pp] + _dot(jnp.concatenate(dzs[pair], axis=1),
                                                           jnp.concatenate(kms[pair], axis=0)))
                    dk_s[pl.ds(koff, TQ), psl[pp]] += _dot_tn(jnp.concatenate(dzs[pair], axis=0),
                                                              jnp.concatenate(qms[pair], axis=0))
                return tuple(out) + tuple(dq)

            zero = jnp.zeros((TQ, 1), F32)
            carry = (zero,) * (2 * HEADS) + (jnp.zeros((TQ, HEAD_PAIR_QK), F32),) * (HEADS // 2)
            carry = lax.fori_loop(0, qi, lambda j, c: k_block(j, c, False), carry)
            carry = k_block(qi, carry, True)
            for pp in pairs:
                dq_s[pl.ds(qoff, TQ), psl[pp]] = (carry[2 * HEADS + pp] * 0.125).astype(BF16)
            return 0

        lax.fori_loop(0, nq, q_block, 0)
        dkb_s[...] = dk_s[...].astype(BF16)
        dvb_s[...] = dv_s[...].astype(BF16)
        for cp in copies:
            cp.start()

        @pl.when(step == n_steps - 1)
        def _():
            for cp in copies:
                cp.wait()

    return pl.pallas_call(
        body, name=name, grid=(T // S, N_HEADS // HEADS),
        in_specs=[pl.BlockSpec((S, QK_W), lambda b, p: (b, p)),
                  pl.BlockSpec((S, QK_W), lambda b, p: (b, k_b0 + p)),
                  pl.BlockSpec((S, V_W), lambda b, p: (b, v_b0 + p)),
                  pl.BlockSpec((S, V_W), lambda b, p: (b, p)),
                  pl.BlockSpec((None, S, HEADS), lambda b, p: (p, b, 0)),
                  HBM_SPEC],
        out_specs=HBM_SPEC,
        out_shape=jax.ShapeDtypeStruct(dproj1.shape, dproj1.dtype),
        input_output_aliases={5: 0},
        scratch_shapes=[pltpu.VMEM((S, QK_W), BF16),
                        pltpu.VMEM((S, QK_W), F32),
                        pltpu.VMEM((S, V_W), F32),
                        pltpu.VMEM((S, QK_W), BF16),
                        pltpu.VMEM((S, V_W), BF16),
                        pltpu.SemaphoreType.DMA((3,))],
        compiler_params=_params(2),
    )(qkv, qkv, qkv, do, ltot, dproj1)


def _out_proj_loss_head(a, w, res, g_row, target, name, tm=512):
    T, K = a.shape

    def body(a_ref, w_ref, r_ref, g_ref, t_ref, dh_ref, dg_ref, loss_ref, dhb_ref):
        @pl.when(pl.program_id(0) == 0)
        def _():
            dg_ref[...] = jnp.zeros_like(dg_ref)
            loss_ref[...] = jnp.zeros_like(loss_ref)

        gain = g_ref[...]
        dg, loss = dg_ref[...], loss_ref[...]
        for rows in _row_blocks(tm):
            x = r_ref[rows, :] + _dot(a_ref[rows, :], w_ref[...])
            inv = lax.rsqrt(jnp.mean(x * x, axis=-1, keepdims=True) + RMS_EPS)
            xhat = x * inv
            err = xhat * gain - t_ref[rows, :]
            per_token = jnp.mean(err * err, axis=-1, keepdims=True)
            loss = loss + 0.5 * jnp.sum(per_token, axis=0, keepdims=True)
            dy = err * (1.0 / D_MODEL)
            dg = dg + jnp.sum(dy * xhat, axis=0, keepdims=True)
            dxh = dy * gain
            proj = jnp.mean(dxh * xhat, axis=-1, keepdims=True)
            dh = inv * (dxh - xhat * proj)
            dh_ref[rows, :] = dh
            dhb_ref[rows, :] = dh.astype(BF16)
        dg_ref[...] = dg
        loss_ref[...] = loss

    return pl.pallas_call(
        body, name=name, grid=(T // tm,),
        in_specs=[pl.BlockSpec((tm, K), lambda m: (m, 0)),
                  pl.BlockSpec((K, D_MODEL), lambda m: (0, 0)),
                  pl.BlockSpec((tm, D_MODEL), lambda m: (m, 0)),
                  pl.BlockSpec((1, D_MODEL), lambda m: (0, 0)),
                  pl.BlockSpec((tm, D_MODEL), lambda m: (m, 0))],
        out_specs=[pl.BlockSpec((tm, D_MODEL), lambda m: (m, 0)),
                   pl.BlockSpec((1, D_MODEL), lambda m: (0, 0)),
                   pl.BlockSpec((1, 128), lambda m: (0, 0)),
                   pl.BlockSpec((tm, D_MODEL), lambda m: (m, 0))],
        out_shape=[jax.ShapeDtypeStruct((T, D_MODEL), F32),
                   jax.ShapeDtypeStruct((1, D_MODEL), F32),
                   jax.ShapeDtypeStruct((1, 128), F32),
                   jax.ShapeDtypeStruct((T, D_MODEL), BF16)],
        compiler_params=_params(1),
    )(a, w, res, g_row, target)


def _place():
    return lax.axis_index("x"), lax.axis_index("y"), lax.axis_index("c")


def _other_chips(x, y):
    return [(1 - x, y), (x, 1 - y), (1 - x, 1 - y)]


def _half(ref, c):
    hr = ref.shape[-2] // 2
    return pl.ds(pl.multiple_of(c * hr, 8), hr)


def _cast_to_slot(shard, chip, name, tr=256):
    R, C = shard.shape

    def body(chip_ref, w_ref, o_ref):
        o_ref[0] = w_ref[...].astype(BF16)

    return pl.pallas_call(
        body, name=name,
        grid_spec=pltpu.PrefetchScalarGridSpec(
            num_scalar_prefetch=1, grid=(R // tr,),
            in_specs=[pl.BlockSpec((tr, C), lambda i, chip_ref: (i, 0))],
            out_specs=pl.BlockSpec((1, tr, C), lambda i, chip_ref: (chip_ref[0], i, 0))),
        out_shape=jax.ShapeDtypeStruct((N_CHIPS, R, C), BF16),
        compiler_params=_params(1),
    )(chip, shard)


def _weight_plan(bufs):
    x, y, c = _place()
    plan = []
    for buf in bufs:
        mine = buf.at[2 * x + y, _half(buf, c)]
        for ox, oy in _other_chips(x, y):
            plan.append((mine, mine, (ox, oy, c), buf.at[2 * ox + oy, _half(buf, c)]))
    return plan


def _chip_sum_plan(bufs):
    x, y, c = _place()
    n = len(bufs) // 2
    plan = []
    for sums, land in zip(bufs[:n], bufs[n:]):
        for k, (ox, oy) in enumerate(_other_chips(x, y)):
            plan.append((sums.at[2 * ox + oy], land.at[k], (ox, oy, c), land.at[k]))
    return plan


def _sibling_plan(bufs):
    x, y, c = _place()
    n = len(bufs) // 2
    return [(p.at[:, _half(p, 1 - c)], land, (x, y, 1 - c), land)
            for p, land in zip(bufs[:n], bufs[n:])]


SEM_SPEC = pl.BlockSpec(memory_space=pltpu.SEMAPHORE)
ANY_SPEC = pl.BlockSpec(memory_space=pl.ANY)
DATAFLOW = pltpu.SideEffectType.DATAFLOW_SIDE_EFFECTING


def _in_hbm(a):
    return pltpu.with_memory_space_constraint(a, pltpu.HBM)


def _exchange_start(bufs, after, plan, n_copies, name):
    nb = len(bufs)

    def body(*refs):
        send_sems, recv_sems = refs[nb + 1], refs[nb + 2]
        for i, (src, dst, dev, _) in enumerate(plan(refs[:nb])):
            pltpu.make_async_remote_copy(
                src_ref=src, dst_ref=dst, send_sem=send_sems.at[i], recv_sem=recv_sems.at[i],
                device_id=dev, device_id_type=MESH).start()
        token = refs[-1]
        token[...] = jnp.zeros_like(token)

    res = pl.pallas_call(
        body, name=name,
        in_specs=[HBM_SPEC] * nb + [ANY_SPEC],
        out_specs=[SEM_SPEC, SEM_SPEC] + [HBM_SPEC] * nb + [pl.BlockSpec(memory_space=pltpu.VMEM)],
        out_shape=[pltpu.SemaphoreType.DMA((n_copies,)), pltpu.SemaphoreType.DMA((n_copies,))]
        + [pltpu.HBM(b.shape, b.dtype) for b in bufs] + [jax.ShapeDtypeStruct((8, 128), F32)],
        input_output_aliases={i: 2 + i for i in range(nb)},
        compiler_params=pltpu.CompilerParams(has_side_effects=DATAFLOW),
    )(*[_in_hbm(b) for b in bufs], after)
    return res[0], res[1], list(res[2:2 + nb]), res[-1]


def _exchange_wait(bufs, send_sems, recv_sems, after, plan, name):
    nb = len(bufs)

    def body(*refs):
        sends, recvs = refs[nb], refs[nb + 1]
        for i, (src, dst, dev, landing) in enumerate(plan(refs[:nb])):
            pltpu.make_async_remote_copy(
                src_ref=src, dst_ref=landing, send_sem=sends.at[i], recv_sem=recvs.at[i],
                device_id=dev, device_id_type=MESH).wait()

    res = pl.pallas_call(
        body, name=name,
        in_specs=[HBM_SPEC] * nb + [SEM_SPEC, SEM_SPEC, ANY_SPEC],
        out_specs=[HBM_SPEC] * nb,
        out_shape=[pltpu.HBM(b.shape, b.dtype) for b in bufs],
        input_output_aliases={i: i for i in range(nb)},
        compiler_params=pltpu.CompilerParams(has_side_effects=DATAFLOW),
    )(*bufs, send_sems, recv_sems, after)
    return list(res)


def _allgather_weights(slots, name, landed=False):
    n = len(slots)

    def body(*refs):
        outs = refs[n:2 * n]
        send_sems, recv_sems, fwd_send, fwd_recv = refs[2 * n:]
        x, y, c = _place()
        chips = _other_chips(x, y)

        def landing(a, chip, half_of):
            return outs[a].at[2 * chip[0] + chip[1], _half(outs[a], half_of)]

        def ici(a, k, chip_from, to):
            return pltpu.make_async_remote_copy(
                src_ref=landing(a, chip_from, c), dst_ref=landing(a, chip_from, c),
                send_sem=send_sems.at[a, k], recv_sem=recv_sems.at[a, k],
                device_id=to, device_id_type=MESH)

        def d2d(a, k, chip_from, half_of):
            return pltpu.make_async_remote_copy(
                src_ref=landing(a, chip_from, half_of), dst_ref=landing(a, chip_from, half_of),
                send_sem=fwd_send.at[a, k], recv_sem=fwd_recv.at[a, k],
                device_id=(x, y, 1 - c), device_id_type=MESH)

        sends = []
        if not landed:
            sends = [ici(a, k, (x, y), (*chips[k], c)) for a in range(n) for k in range(3)]
        for cp in sends:
            cp.start()
        forwards = []
        for a in range(n):
            for k in range(3):
                if not landed:
                    ici(a, k, chips[k], (x, y, c)).wait_recv()
                fw = d2d(a, k, chips[k], c)
                fw.start()
                forwards.append(fw)
        for a in range(n):
            for k in range(3):
                d2d(a, k, chips[k], 1 - c).wait_recv()
        for cp in sends + forwards:
            cp.wait_send()

    return pl.pallas_call(
        body, name=name,
        in_specs=[HBM_SPEC] * n, out_specs=[HBM_SPEC] * n,
        out_shape=[jax.ShapeDtypeStruct(s.shape, s.dtype) for s in slots],
        input_output_aliases={a: a for a in range(n)},
        scratch_shapes=[pltpu.SemaphoreType.DMA((n, 3)), pltpu.SemaphoreType.DMA((n, 3)),
                        pltpu.SemaphoreType.DMA((n, 3)), pltpu.SemaphoreType.DMA((n, 3))],
    )(*slots)


def _sibling_exchange(partials, name):
    n = len(partials)

    def body(*refs):
        ins, outs = refs[:n], refs[n:2 * n]
        send_sems, recv_sems = refs[2 * n:]
        x, y, c = _place()
        sends = [pltpu.make_async_remote_copy(
            src_ref=ins[a].at[:, _half(ins[a], 1 - c)], dst_ref=outs[a],
            send_sem=send_sems.at[a], recv_sem=recv_sems.at[a],
            device_id=(x, y, 1 - c), device_id_type=MESH) for a in range(n)]
        for cp in sends:
            cp.start()
        for cp in sends:
            cp.wait()

    return pl.pallas_call(
        body, name=name,
        in_specs=[HBM_SPEC] * n, out_specs=[HBM_SPEC] * n,
        out_shape=[jax.ShapeDtypeStruct((N_CHIPS, p.shape[1] // 2, p.shape[2]), F32)
                   for p in partials],
        scratch_shapes=[pltpu.SemaphoreType.DMA((n,)), pltpu.SemaphoreType.DMA((n,))],
    )(*partials)


def _chip_sum(partial, from_sibling, c, name, tr=256):
    _, hr, C = from_sibling.shape
    nb = hr // tr

    def body(c_ref, p_ref, s_ref, o_ref):
        o_ref[...] = (p_ref[...] + s_ref[...]).astype(BF16)

    return pl.pallas_call(
        body, name=name,
        grid_spec=pltpu.PrefetchScalarGridSpec(
            num_scalar_prefetch=1, grid=(N_CHIPS, nb),
            in_specs=[pl.BlockSpec((1, tr, C), lambda j, i, c_ref: (j, c_ref[0] * nb + i, 0)),
                      pl.BlockSpec((1, tr, C), lambda j, i, c_ref: (j, i, 0))],
            out_specs=pl.BlockSpec((1, tr, C), lambda j, i, c_ref: (j, i, 0))),
        out_shape=jax.ShapeDtypeStruct(from_sibling.shape, BF16),
        compiler_params=_params(2),
    )(c, partial, from_sibling)


def _reduce_half(partial, from_sibling, received, place, name, tr=256):
    _, hr, C = from_sibling.shape
    nb = hr // tr

    def body(p_ref, mine_ref, sib_ref, r_ref, o_ref):
        acc = mine_ref[0] + sib_ref[0]
        for k in range(3):
            acc = acc + r_ref[k].astype(F32)
        o_ref[...] = acc

    return pl.pallas_call(
        body, name=name,
        grid_spec=pltpu.PrefetchScalarGridSpec(
            num_scalar_prefetch=1, grid=(nb,),
            in_specs=[pl.BlockSpec((1, tr, C), lambda i, p: (p[0], p[1] * nb + i, 0)),
                      pl.BlockSpec((1, tr, C), lambda i, p: (p[0], i, 0)),
                      pl.BlockSpec((3, tr, C), lambda i, p: (0, i, 0))],
            out_specs=pl.BlockSpec((tr, C), lambda i, p: (p[1] * nb + i, 0))),
        out_shape=jax.ShapeDtypeStruct((2 * hr, C), F32),
        compiler_params=_params(1),
    )(place, partial, from_sibling, received)


def _join_halves(fulls, small, name):
    n = len(fulls)

    def body(*refs):
        small_ref, outs, small_all = refs[n], refs[n + 1:2 * n + 1], refs[2 * n + 1]
        send_sems, recv_sems, s_send, s_recv, loc_sem = refs[2 * n + 2:]
        x, y, c = _place()
        me = 4 * x + 2 * y + c

        def copy(a, half_of, to):
            rows = outs[a].at[_half(outs[a], half_of)]
            return pltpu.make_async_remote_copy(
                src_ref=rows, dst_ref=rows, send_sem=send_sems.at[a], recv_sem=recv_sems.at[a],
                device_id=to, device_id_type=MESH)

        def small_copy(d, landing, to):
            return pltpu.make_async_remote_copy(
                src_ref=small_ref, dst_ref=small_all.at[landing],
                send_sem=s_send.at[d - 1], recv_sem=s_recv.at[d - 1],
                device_id=to, device_id_type=MESH)

        local = pltpu.make_async_copy(small_ref, small_all.at[me], loc_sem)
        local.start()
        sends = [copy(a, c, (x, y, 1 - c)) for a in range(n)]
        sends += [small_copy(d, me, (x ^ ((d >> 2) & 1), y ^ ((d >> 1) & 1), c ^ (d & 1)))
                  for d in range(1, N_DEV)]
        for cp in sends:
            cp.start()
        for d in range(1, N_DEV):
            small_copy(d, me ^ d, (x, y, c)).wait_recv()
        for a in range(n):
            copy(a, 1 - c, (x, y, c)).wait_recv()
        for cp in sends:
            cp.wait_send()
        local.wait()

    res = pl.pallas_call(
        body, name=name,
        in_specs=[HBM_SPEC] * (n + 1), out_specs=[HBM_SPEC] * (n + 1),
        out_shape=[jax.ShapeDtypeStruct(f.shape, F32) for f in fulls]
        + [jax.ShapeDtypeStruct((N_DEV,) + small.shape, F32)],
        input_output_aliases={a: a for a in range(n)},
        scratch_shapes=[pltpu.SemaphoreType.DMA((n,)), pltpu.SemaphoreType.DMA((n,)),
                        pltpu.SemaphoreType.DMA((N_DEV - 1,)), pltpu.SemaphoreType.DMA((N_DEV - 1,)),
                        pltpu.SemaphoreType.DMA],
    )(*fulls, small)
    return list(res[:n]), res[n]


def _adamw_math(w, g, m, v):
    m = ADAM_B1 * m + (1.0 - ADAM_B1) * g
    v = ADAM_B2 * v + (1.0 - ADAM_B2) * (g * g)
    m_hat = m / (1.0 - ADAM_B1 ** ADAM_STEP)
    v_hat = v / (1.0 - ADAM_B2 ** ADAM_STEP)
    delta = -ADAM_LR * (m_hat / (jnp.sqrt(v_hat) + ADAM_EPS) + ADAM_WD * w)
    return delta, m, v


def _adamw(w, g, m, v, name, tr=256):
    R, C = w.shape
    tr = min(tr, R)

    def body(w_ref, g_ref, m_ref, v_ref, d_out, m_out, v_out):
        d_out[...], m_out[...], v_out[...] = _adamw_math(w_ref[...], g_ref[...], m_ref[...], v_ref[...])

    spec = pl.BlockSpec((tr, C), lambda i: (i, 0))
    return pl.pallas_call(
        body, name=name, grid=(R // tr,),
        in_specs=[spec] * 4, out_specs=[spec] * 3,
        out_shape=[jax.ShapeDtypeStruct((R, C), F32)] * 3,
        compiler_params=_params(1),
    )(w, g, m, v)


def _adamw_small(small_all, w, m, v, name):
    def body(s_ref, w_ref, m_ref, v_ref, g_out, d_out, m_out, v_out):
        g = s_ref[0]
        for d in range(1, N_DEV):
            g = g + s_ref[d]
        g_out[...] = g
        d_out[...], m_out[...], v_out[...] = _adamw_math(w_ref[...], g, m_ref[...], v_ref[...])

    vm = pl.BlockSpec(memory_space=pltpu.VMEM)
    return pl.pallas_call(
        body, name=name, in_specs=[vm] * 4, out_specs=[vm] * 4,
        out_shape=[jax.ShapeDtypeStruct(w.shape, F32)] * 4,
    )(small_all, w, m, v)


def _pack_small(norm_g, pool_scale, norm_f, extra_row):
    return jnp.concatenate([norm_g.reshape(2, D_MODEL), pool_scale.reshape(2, D_MODEL),
                            norm_f.reshape(1, D_MODEL), extra_row,
                            jnp.zeros((2, D_MODEL), F32)], axis=0)


def kernel(x, norm_g, pool_w_in, pool_w, pool_scale, pool_w_out, sb_w_in, sb_w_out, norm_f, loss_target, m_norm_g, m_pool_w_in, m_pool_w, m_pool_scale, m_pool_w_out, m_sb_w_in, m_sb_w_out, m_norm_f, v_norm_g, v_pool_w_in, v_pool_w, v_pool_scale, v_pool_w_out, v_sb_w_in, v_sb_w_out, v_norm_f):
    nb, S, _ = x.shape
    T = nb * S
    xt = x.reshape(T, D_MODEL)
    target = loss_target.reshape(T, D_MODEL)
    cx, cy, cc = _place()

    def shard2d(w):
        return w.reshape(-1, w.shape[-1])

    names = ("pool_w_in", "pool_w", "pool_w_out", "sb_w_in", "sb_w_out")
    w_shards = [shard2d(w) for w in (pool_w_in, pool_w, pool_w_out, sb_w_in, sb_w_out)]
    m_shards = [shard2d(w) for w in (m_pool_w_in, m_pool_w, m_pool_w_out, m_sb_w_in, m_sb_w_out)]
    v_shards = [shard2d(w) for w in (v_pool_w_in, v_pool_w, v_pool_w_out, v_sb_w_in, v_sb_w_out)]

    chip = (2 * cx + cy).reshape(1).astype(jnp.int32)
    c_arr = cc.reshape(1).astype(jnp.int32)
    place = jnp.stack([2 * cx + cy, cc]).astype(jnp.int32)
    slots = [_cast_to_slot(w, chip, "cast_" + nm) for w, nm in zip(w_shards, names)]
    g0, g1, gf = norm_g[0:1], norm_g[1:2], norm_f.reshape(1, D_MODEL)

    w_pin, = _allgather_weights(slots[:1], "allgather_pool_in_weights")
    mix_send, mix_recv, mix_slots, token = _exchange_start(slots[1:3], w_pin, _weight_plan, 6,
                                                           "pool_weights_start")
    sb_send, sb_recv, sb_slots, token = _exchange_start(slots[3:], token, _weight_plan, 6,
                                                        "sb_weights_start")

    proj0, u0 = _rms_matmul(xt, g0 + token[0:1, 0:1], w_pin, "pool_in_proj")
    mix_slots = _exchange_wait(mix_slots, mix_send, mix_recv, proj0, _weight_plan, "pool_weights_wait")
    w_g, w_pout = _allgather_weights(mix_slots, "pool_weights_forward", landed=True)
    w_pout = w_pout.reshape(D_INNER, D_MODEL)
    y0, pooled, mixed = _pool_fwd(proj0, w_g, pool_scale, S, "pool_mix")
    sb_slots = _exchange_wait(sb_slots, sb_send, sb_recv, y0, _weight_plan, "sb_weights_wait")
    w_sin, w_sout = _allgather_weights(sb_slots, "sb_weights_forward", landed=True)
    w_sout = w_sout.reshape(D_INNER, D_MODEL)
    h1 = _matmul_residual(y0, w_pout, xt, "pool_out_proj")
    n1 = 2 * QK_WIDTH + 2 * D_INNER
    qkvz, u1 = _rms_matmul(h1, g1, w_sin, "sb_in_proj")
    o, y1, ltot = _attn_fwd(qkvz, S, "sb_attention")
    dh2, d_norm_f, loss_row, dh2_b = _out_proj_loss_head(y1, w_sout, h1, gf, target,
                                                         "sb_out_proj_loss_head")

    def reduce_start(partials, tag, after=None, behind=None):
        n, done = len(partials), None
        after = c_arr if after is None else after
        if behind is None:
            from_sibling = list(_sibling_exchange(partials, "grad_sibling_exchange_" + tag))
        else:
            lands = [lax.empty((N_CHIPS, p.shape[1] // 2, p.shape[2]), F32) for p in partials]
            send, recv, bufs, token = _exchange_start(partials + lands, after, _sibling_plan, n,
                                                      "grad_sibling_start_" + tag)
            done = behind(token[0:1, 0:1])
            bufs = _exchange_wait(bufs, send, recv, done[0], _sibling_plan, "grad_sibling_wait_" + tag)
            partials, from_sibling, after = bufs[:n], bufs[n:], c_arr
        sums = [_chip_sum(p, s, c_arr, "grad_chip_sum_%s_%d" % (tag, i))
                for i, (p, s) in enumerate(zip(partials, from_sibling))]
        lands = [lax.empty((3,) + s.shape[1:], BF16) for s in sums]
        send, recv, bufs, token = _exchange_start(sums + lands, after, _chip_sum_plan, 3 * n,
                                                  "grad_chip_exchange_start_" + tag)
        return (partials, from_sibling, send, recv, bufs), token[0:1, 0:1], done

    def reduce_finish(started, after, tag):
        partials, from_sibling, send, recv, bufs = started
        received = _exchange_wait(bufs, send, recv, after, _chip_sum_plan,
                                  "grad_chip_exchange_wait_" + tag)[len(partials):]
        return [_reduce_half(p, s, r, place, "grad_reduce_%s_%d" % (tag, i))
                for i, (p, s, r) in enumerate(zip(partials, from_sibling, received))]

    shard = lambda i, j, t: (j, 0, 0)
    gw_sout = _matmul_tn(y1, dh2_b, D_INNER, D_MODEL, (D_INNER, D_MODEL), (1024, 1024),
                         lambda i, j, t: (i, j), "grad_sb_w_out", bm=1024, bn=1024)
    do, dproj1 = _attn_gate_bwd(dh2_b, w_sout, qkvz, o, "sb_gate_bwd")
    dproj1 = _attn_bwd(qkvz, do, ltot, dproj1, S, "sb_attention_bwd")
    gw_sin = _matmul_tn(u1, dproj1, D_MODEL, n1, (N_CHIPS, D_MODEL, n1 // 4), (1, D_MODEL, n1 // 4),
                        shard, "grad_sb_w_in", bm=D_MODEL, bn=n1 // 4)
    sb_started, token, (dh1, d_g1, dh1_b) = reduce_start(
        [gw_sin, gw_sout.reshape(N_CHIPS, -1, D_MODEL)], "sb",
        behind=lambda tok: _matmul_nt_rms_bwd(dproj1, w_sin, h1, g1 + tok, dh2, "sb_in_bwd", True))
    gw_pout = _matmul_tn(y0, dh1_b, D_INNER, D_MODEL, (D_INNER, D_MODEL), (1024, 1024),
                         lambda i, j, t: (i, j), "grad_pool_w_out", bm=1024, bn=1024)
    dmixed, dproj0, d_scale = _pool_gate_bwd(dh1_b, w_pout, proj0, mixed, pool_scale + token,
                                             "pool_gate_bwd")
    gw_g = _matmul_tn(pooled, dmixed, D_INNER, D_INNER, (N_CHIPS, GROUP_DIM, GROUP_DIM),
                      (N_CHIPS, GROUP_DIM // N_CHIPS, GROUP_DIM), lambda i, j, t: (0, i, 0),
                      "grad_pool_w", bm=GROUP_DIM, bn=GROUP_DIM, diagonal_blocks=True)
    mix_started, token, (dproj0,) = reduce_start(
        [gw_g, gw_pout.reshape(N_CHIPS, -1, D_MODEL)], "pool_mix",
        behind=lambda tok: (_pool_bwd(dmixed, w_g, dproj0, tok, S, "pool_bwd"),))
    n0 = 2 * D_INNER
    gw_pin = _matmul_tn(u0, dproj0, D_MODEL, n0, (N_CHIPS, D_MODEL, n0 // 4), (1, D_MODEL, n0 // 4),
                        shard, "grad_pool_w_in", bm=D_MODEL, bn=n0 // 4)
    pin_started, token, _ = reduce_start([gw_pin], "pool_in", after=token)
    dx, d_g0 = _matmul_nt_rms_bwd(dproj0, w_pin, xt, g0 + token, dh1, "pool_in_bwd", False)

    small = _pack_small(jnp.concatenate([d_g0, d_g1], axis=0), d_scale, d_norm_f,
                        jnp.broadcast_to(loss_row[:, :1], (1, D_MODEL)))
    grads, small_all = _join_halves(reduce_finish(pin_started, dx, "pool_in")
                                    + reduce_finish(mix_started, dx, "pool_mix")
                                    + reduce_finish(sb_started, dx, "sb"), small, "grad_join_halves")

    deltas, new_m, new_v = [], [], []
    for w, g, m, v, nm in zip(w_shards, grads, m_shards, v_shards, names):
        d, mm, vv = _adamw(w, g, m, v, "adamw_" + nm)
        deltas.append(d)
        new_m.append(mm)
        new_v.append(vv)

    zero_row = jnp.zeros((1, D_MODEL), F32)
    g_small, d_small, m_small, v_small = _adamw_small(
        small_all, _pack_small(norm_g, pool_scale, norm_f, zero_row),
        _pack_small(m_norm_g, m_pool_scale, m_norm_f, zero_row),
        _pack_small(v_norm_g, v_pool_scale, v_norm_f, zero_row + 1.0), "adamw_small")
    loss = g_small[5, 0]

    def unpack_small(a):
        return a[0:2], a[2:4].reshape(1, D_INNER), a[4]

    def assemble(big, small3):
        ng, ps, nf = small3
        return [ng, big[0].reshape(pool_w_in.shape), big[1].reshape(pool_w.shape), ps,
                big[2].reshape(pool_w_out.shape), big[3].reshape(sb_w_in.shape),
                big[4].reshape(sb_w_out.shape), nf]

    return (loss, dx.reshape(x.shape),
            *assemble(grads, unpack_small(g_small)),
            *assemble(deltas, unpack_small(d_small)),
            *assemble(new_m, unpack_small(m_small)),
            *assemble(new_v, unpack_small(v_small)))
```

```python
import jax
import jax.numpy as jnp
from jax import lax
from jax.experimental import pallas as pl
from jax.experimental.pallas import tpu as pltpu

F32 = jnp.float32
BF16 = jnp.bfloat16
MESH = pl.DeviceIdType.MESH

D_MODEL = 1024
D_INNER = 2048
N_GROUPS = 4
GROUP_DIM = 512
HEAD_PAIR_QK = 128
HEAD_V = 128
QK_WIDTH = 1024
RMS_EPS = 1e-6
HALO = 16
N_CHIPS = 4
N_DEV = 8

ADAM_LR = 0.001
ADAM_B1 = 0.9
ADAM_B2 = 0.999
ADAM_EPS = 1e-08
ADAM_WD = 0.01
ADAM_STEP = 10

VMEM_LIMIT = 56 * 1024 * 1024

HBM_SPEC = pl.BlockSpec(memory_space=pltpu.HBM)


def _params(n_axes):
    return pltpu.CompilerParams(dimension_semantics=("arbitrary",) * n_axes,
                                vmem_limit_bytes=VMEM_LIMIT)


def _dot(a, b):
    return jnp.dot(a, b, preferred_element_type=F32)


def _dot_nt(a, b):
    return lax.dot_general(a, b, (((1,), (1,)), ((), ())), preferred_element_type=F32)


def _dot_tn(a, b):
    return lax.dot_general(a, b, (((0,), (0,)), ((), ())), preferred_element_type=F32)


def _sigmoid(z):
    return 1.0 / (1.0 + jnp.exp(-z))


def _row_blocks(tm, rows=256):
    return [slice(r, r + rows) for r in range(0, tm, rows)]


def _rms_matmul(h, g_row, w4, name, tm=1024):
    T = h.shape[0]
    n_shards, _, tn = w4.shape
    nm = T // tm

    def body(h_ref, g_ref, w_ref, o_ref, u_out, u_all):
        n, m = pl.program_id(0), pl.program_id(1)
        rows = pl.ds(pl.multiple_of(m * tm, tm), tm)

        @pl.when(n == 0)
        def _():
            x = h_ref[...]
            inv = lax.rsqrt(jnp.mean(x * x, axis=-1, keepdims=True) + RMS_EPS)
            u = (x * inv * g_ref[...]).astype(BF16)
            u_all[rows, :] = u
            u_out[...] = u

        o_ref[...] = _dot(u_all[rows, :], w_ref[0]).astype(BF16)

    return pl.pallas_call(
        body, name=name, grid=(n_shards, nm),
        in_specs=[pl.BlockSpec((tm, D_MODEL), lambda n, m: (jnp.where(n == 0, m, nm - 1), 0)),
                  pl.BlockSpec((1, D_MODEL), lambda n, m: (0, 0)),
                  pl.BlockSpec((1, D_MODEL, tn), lambda n, m: (n, 0, 0))],
        out_specs=[pl.BlockSpec((tm, tn), lambda n, m: (m, n)),
                   pl.BlockSpec((tm, D_MODEL), lambda n, m: (jnp.where(n == 0, m, nm - 1), 0))],
        out_shape=[jax.ShapeDtypeStruct((T, n_shards * tn), BF16),
                   jax.ShapeDtypeStruct((T, D_MODEL), BF16)],
        scratch_shapes=[pltpu.VMEM((T, D_MODEL), BF16)],
        compiler_params=_params(2),
    )(h, g_row, w4)


def _matmul_residual(a, w, res, name, tm=1024, tn=1024):
    T, K = a.shape
    N = w.shape[1]

    def body(a_ref, w_ref, r_ref, o_ref):
        o_ref[...] = r_ref[...] + _dot(a_ref[...], w_ref[...])

    return pl.pallas_call(
        body, name=name, grid=(T // tm, N // tn),
        in_specs=[pl.BlockSpec((tm, K), lambda m, n: (m, 0)),
                  pl.BlockSpec((K, tn), lambda m, n: (0, n)),
                  pl.BlockSpec((tm, tn), lambda m, n: (m, n))],
        out_specs=pl.BlockSpec((tm, tn), lambda m, n: (m, n)),
        out_shape=jax.ShapeDtypeStruct((T, N), F32),
        compiler_params=_params(2),
    )(a, w, res)


def _matmul_tn(a, b, a_cols, b_cols, out_shape, out_block, out_map, name, bm, bn, tk=2048,
               diagonal_blocks=False):
    T = a.shape[0]
    tk = min(tk, T)

    def body(a_ref, b_ref, o_ref):
        @pl.when(pl.program_id(2) == 0)
        def _():
            o_ref[...] = jnp.zeros_like(o_ref)

        part = _dot_tn(a_ref[...].astype(BF16), b_ref[...].astype(BF16))
        o_ref[...] += part.reshape(o_ref.shape)

    b_map = (lambda i, j, t: (t, i)) if diagonal_blocks else (lambda i, j, t: (t, j))
    return pl.pallas_call(
        body, name=name, grid=(a_cols // bm, 1 if diagonal_blocks else b_cols // bn, T // tk),
        in_specs=[pl.BlockSpec((tk, bm), lambda i, j, t: (t, i)),
                  pl.BlockSpec((tk, bn), b_map)],
        out_specs=pl.BlockSpec(out_block, out_map),
        out_shape=jax.ShapeDtypeStruct(out_shape, F32),
        compiler_params=_params(3),
    )(a, b)


def _matmul_nt_rms_bwd(dproj, w4, h, g_row, dres, name, with_bf16, tm=512):
    T, cols = dproj.shape
    nk, _, tk = w4.shape

    def body(dp_ref, w_ref, h_ref, g_ref, r_ref, dx_ref, dg_ref, *rest):
        acc_all = rest[-1]
        k, m = pl.program_id(0), pl.program_id(1)
        acc = acc_all.at[pl.ds(pl.multiple_of(m * tm, tm), tm), :]

        @pl.when(k == 0)
        def _():
            acc[...] = jnp.zeros_like(acc)

        @pl.when((k == 0) & (m == 0))
        def _():
            dg_ref[...] = jnp.zeros_like(dg_ref)

        acc[...] += _dot_nt(dp_ref[...], w_ref[0])

        @pl.when(k == nk - 1)
        def _():
            du = acc[...]
            x = h_ref[...]
            inv = lax.rsqrt(jnp.mean(x * x, axis=-1, keepdims=True) + RMS_EPS)
            xhat = x * inv
            dg_ref[...] += jnp.sum(du * xhat, axis=0, keepdims=True)
            dxh = du * g_ref[...]
            proj = jnp.mean(dxh * xhat, axis=-1, keepdims=True)
            dx = r_ref[...] + inv * (dxh - xhat * proj)
            dx_ref[...] = dx
            if with_bf16:
                rest[0][...] = dx.astype(BF16)

    rows = pl.BlockSpec((tm, D_MODEL), lambda k, m: (jnp.where(k == nk - 1, m, 0), 0))
    gain = pl.BlockSpec((1, D_MODEL), lambda k, m: (0, 0))
    return pl.pallas_call(
        body, name=name, grid=(nk, T // tm),
        in_specs=[pl.BlockSpec((tm, tk), lambda k, m: (m, k)),
                  pl.BlockSpec((1, D_MODEL, tk), lambda k, m: (k, 0, 0)),
                  rows, gain, rows],
        out_specs=[rows, gain] + [rows] * with_bf16,
        out_shape=[jax.ShapeDtypeStruct((T, D_MODEL), F32), jax.ShapeDtypeStruct((1, D_MODEL), F32)]
        + [jax.ShapeDtypeStruct((T, D_MODEL), BF16)] * with_bf16,
        scratch_shapes=[pltpu.VMEM((T, D_MODEL), F32)],
        compiler_params=_params(2),
    )(dproj, w4, h, g_row, dres)


def _window_of(g):
    return jnp.left_shift(2, g)


def _select_stage(g, stages):
    res = stages[0]
    for i in range(1, len(stages)):
        res = jnp.where(g >= i, stages[i], res)
    return res


def _pool_fwd(proj0, wg4, scale_row, S, name, tm=2048):
    T = proj0.shape[0]
    tm = min(tm, S)
    blocks_per_seq = S // tm
    hb = tm // HALO

    def body(x_ref, halo_ref, z_ref, w_ref, s_ref, y_ref, p_ref, mix_ref):
        m, g = pl.program_id(0), pl.program_id(1)
        first = (m % blocks_per_seq) == 0
        halo = jnp.where(first, 0.0, halo_ref[...].astype(F32))
        x = x_ref[...].astype(F32)
        ext = jnp.concatenate([halo, x], axis=0)
        stages = []
        cur = ext
        for sh in (1, 2, 4, 8):
            cur = cur + pltpu.roll(cur, sh, 0)
            stages.append(cur[HALO:, :])
        win_sum = _select_stage(g, stages)
        pos = (m % blocks_per_seq) * tm + lax.broadcasted_iota(jnp.int32, (tm, 1), 0)
        count = jnp.minimum(pos + 1, _window_of(g)).astype(F32)
        p_ref[...] = (win_sum / count - x).astype(BF16)
        w = w_ref[...].reshape(GROUP_DIM, GROUP_DIM)
        for rows in _row_blocks(tm):
            mixed = _dot(p_ref[rows, :], w)
            z = z_ref[rows, :].astype(F32)
            y_ref[rows, :] = (mixed * s_ref[...] * (z * _sigmoid(z))).astype(BF16)
            mix_ref[rows, :] = mixed.astype(BF16)

    blk = lambda m, g: (m, g)
    return pl.pallas_call(
        body, name=name, grid=(T // tm, N_GROUPS),
        in_specs=[pl.BlockSpec((tm, GROUP_DIM), blk),
                  pl.BlockSpec((HALO, GROUP_DIM), lambda m, g: (jnp.maximum(m * hb - 1, 0), g)),
                  pl.BlockSpec((tm, GROUP_DIM), lambda m, g: (m, N_GROUPS + g)),
                  pl.BlockSpec((N_CHIPS, GROUP_DIM // N_CHIPS, GROUP_DIM), lambda m, g: (0, g, 0)),
                  pl.BlockSpec((1, GROUP_DIM), lambda m, g: (0, g))],
        out_specs=[pl.BlockSpec((tm, GROUP_DIM), blk)] * 3,
        out_shape=[jax.ShapeDtypeStruct((T, D_INNER), BF16)] * 3,
        compiler_params=_params(2),
    )(proj0, proj0, proj0, wg4, scale_row)


def _pool_gate_bwd(dh, w_out, proj0, mixed, scale_row, name, tm=1024, tn=1024):
    T = dh.shape[0]
    gate_b0 = D_INNER // tn

    def body(dh_ref, w_ref, z_ref, mix_ref, s_ref, dm_ref, dz_ref, ds_ref):
        m, n = pl.program_id(0), pl.program_id(1)

        @pl.when((m == 0) & (n == 0))
        def _():
            ds_ref[...] = jnp.zeros_like(ds_ref)

        cols = pl.ds(pl.multiple_of(n * tn, tn), tn)
        s = s_ref[...]
        ds = ds_ref[:, cols]
        for rows in _row_blocks(tm):
            dy = _dot_nt(dh_ref[rows, :], w_ref[...])
            z = z_ref[rows, :].astype(F32)
            sig = _sigmoid(z)
            silu = z * sig
            mixed = mix_ref[rows, :].astype(F32)
            dm_ref[rows, :] = (dy * s * silu).astype(BF16)
            dz_ref[rows, :] = (dy * mixed * s * (sig * (1.0 + z * (1.0 - sig)))).astype(BF16)
            ds = ds + jnp.sum(dy * mixed * silu, axis=0, keepdims=True)
        ds_ref[:, cols] = ds

    return pl.pallas_call(
        body, name=name, grid=(T // tm, D_INNER // tn),
        in_specs=[pl.BlockSpec((tm, D_MODEL), lambda m, n: (m, 0)),
                  pl.BlockSpec((tn, D_MODEL), lambda m, n: (n, 0)),
                  pl.BlockSpec((tm, tn), lambda m, n: (m, gate_b0 + n)),
                  pl.BlockSpec((tm, tn), lambda m, n: (m, n)),
                  pl.BlockSpec((1, tn), lambda m, n: (0, n))],
        out_specs=[pl.BlockSpec((tm, tn), lambda m, n: (m, n)),
                   pl.BlockSpec((tm, tn), lambda m, n: (m, gate_b0 + n)),
                   pl.BlockSpec((1, D_INNER), lambda m, n: (0, 0))],
        out_shape=[jax.ShapeDtypeStruct((T, D_INNER), BF16),
                   jax.ShapeDtypeStruct((T, 2 * D_INNER), BF16),
                   jax.ShapeDtypeStruct((1, D_INNER), F32)],
        compiler_params=_params(2),
    )(dh, w_out, proj0, mixed, scale_row)


def _pool_bwd(dmixed, wg4, dproj0, after, S, name, tm=2048):
    T = dmixed.shape[0]
    tm = min(tm, S)
    blocks_per_seq = S // tm
    hb = tm // HALO
    n_halo_blocks = T // HALO

    def body(dm_ref, halo_ref, w_ref, _, __, o_ref):
        m, g = pl.program_id(0), pl.program_id(1)
        ext = jnp.concatenate([dm_ref[...], halo_ref[...]], axis=0)
        dp = _dot_nt(ext, w_ref[...].reshape(GROUP_DIM, GROUP_DIM))
        pos = (m % blocks_per_seq) * tm + lax.broadcasted_iota(jnp.int32, (tm + HALO, 1), 0)
        count = jnp.minimum(pos + 1, _window_of(g)).astype(F32)
        c = jnp.where(pos < S, dp / count, 0.0)
        n = tm + HALO
        stages = []
        cur = c
        for sh in (1, 2, 4, 8):
            cur = cur + pltpu.roll(cur, n - sh, 0)
            stages.append(cur[:tm, :])
        o_ref[...] = (_select_stage(g, stages) - dp[:tm, :]).astype(BF16)

    blk = lambda m, g: (m, g)
    return pl.pallas_call(
        body, name=name, grid=(T // tm, N_GROUPS),
        in_specs=[pl.BlockSpec((tm, GROUP_DIM), blk),
                  pl.BlockSpec((HALO, GROUP_DIM),
                               lambda m, g: (jnp.minimum((m + 1) * hb, n_halo_blocks - 1), g)),
                  pl.BlockSpec((N_CHIPS, GROUP_DIM // N_CHIPS, GROUP_DIM), lambda m, g: (0, g, 0)),
                  HBM_SPEC, ANY_SPEC],
        out_specs=pl.BlockSpec((tm, GROUP_DIM), blk),
        out_shape=jax.ShapeDtypeStruct(dproj0.shape, dproj0.dtype),
        input_output_aliases={3: 0},
        compiler_params=_params(2),
    )(dmixed, dmixed, wg4, dproj0, after)


TQ = 256


def _split_dot(x, m):
    hi = x.astype(BF16)
    lo = (x - hi.astype(F32)).astype(BF16)
    return _dot(hi, m) + _dot(lo, m)


NEG_LOG2E = -1.4426950408889634


def _log_terms(z):
    soft = jnp.log(1.0 + jnp.exp2(jnp.abs(z) * NEG_LOG2E))
    log_beta = jnp.minimum(z, 0.0) - soft
    return log_beta, log_beta - z


N_HEADS = 16
FWD_HEADS = BWD_HEADS = 4


def _masked_heads(x, heads):
    lane = lax.broadcasted_iota(jnp.int32, (1, HEAD_PAIR_QK), 1)
    out = []
    for hh in range(heads):
        slab = x[:, (hh // 2) * HEAD_PAIR_QK:(hh // 2 + 1) * HEAD_PAIR_QK]
        out.append(jnp.where((lane // 64) == hh % 2, slab, jnp.zeros_like(slab)))
    return out


def _attn_fwd(qkvz, S, name):
    T = qkvz.shape[0]
    nq = S // TQ
    HEADS, QK_W, V_W = FWD_HEADS, FWD_HEADS * 64, FWD_HEADS * HEAD_V
    k_b0 = QK_WIDTH // QK_W
    v_b0 = 2 * QK_WIDTH // V_W
    z_b0 = (2 * QK_WIDTH + D_INNER) // V_W
    hs = range(HEADS)

    def body(q_ref, k_ref, v_ref, z_ref, o_ref, y_ref, lt_ref):
        row = lax.broadcasted_iota(jnp.int32, (TQ, TQ), 0)
        col = lax.broadcasted_iota(jnp.int32, (TQ, TQ), 1)
        causal = col < row
        later_in_block = (row > col).astype(BF16)
        lax.fori_loop(0, nq, lambda qi, _: q_block(qi, causal, later_in_block,
                                                   q_ref, k_ref, v_ref, z_ref, o_ref, y_ref, lt_ref), 0)

    def q_block(qi, causal, later_in_block, q_ref, k_ref, v_ref, z_ref, o_ref, y_ref, lt_ref):
        rows = pl.ds(pl.multiple_of(qi * TQ, TQ), TQ)
        qms = [qm * 0.125 for qm in _masked_heads(q_ref[rows, :], HEADS)]

        def step(j, carry, diagonal):
            koff = pl.multiple_of(j * TQ, TQ)
            kbs = [k_ref[pl.ds(koff, TQ), p * HEAD_PAIR_QK:(p + 1) * HEAD_PAIR_QK]
                   for p in range(HEADS // 2)]
            run, acc = [carry[2 * hh] for hh in hs], [carry[2 * hh + 1] for hh in hs]
            z = [_dot_nt(qms[hh], kbs[hh // 2]) for hh in hs]
            terms = [_log_terms(z[hh]) for hh in hs]
            log_om = [jnp.where(causal, t[1], 0.0) if diagonal else t[1] for t in terms]
            later = [_split_dot(log_om[hh], later_in_block) for hh in hs]
            a = [jnp.exp(terms[hh][0] + (run[hh] + later[hh])) for hh in hs]
            if diagonal:
                a = [jnp.where(causal, a[hh], 0.0) for hh in hs]
            out = []
            for hh in hs:
                vb = v_ref[pl.ds(koff, TQ), hh * HEAD_V:(hh + 1) * HEAD_V]
                out += [run[hh] + jnp.sum(log_om[hh], axis=1, keepdims=True),
                        acc[hh] + _dot(a[hh].astype(BF16), vb)]
            return tuple(out)

        zero = (jnp.zeros((TQ, 1), F32), jnp.zeros((TQ, HEAD_V), F32))
        carry = step(qi, zero * HEADS, True)
        carry = lax.fori_loop(0, qi, lambda i, c: step(qi - 1 - i, c, False), carry)
        for hh in hs:
            sl = slice(hh * HEAD_V, (hh + 1) * HEAD_V)
            acc = carry[2 * hh + 1]
            z = z_ref[rows, sl].astype(F32)
            o_ref[rows, sl] = acc.astype(BF16)
            y_ref[rows, sl] = (acc * (z * _sigmoid(z))).astype(BF16)
            lt_ref[rows, hh:hh + 1] = carry[2 * hh]
        return 0

    blk = lambda b, p: (b, p)
    return pl.pallas_call(
        body, name=name, grid=(T // S, N_HEADS // HEADS),
        in_specs=[pl.BlockSpec((S, QK_W), blk),
                  pl.BlockSpec((S, QK_W), lambda b, p: (b, k_b0 + p)),
                  pl.BlockSpec((S, V_W), lambda b, p: (b, v_b0 + p)),
                  pl.BlockSpec((S, V_W), lambda b, p: (b, z_b0 + p))],
        out_specs=[pl.BlockSpec((S, V_W), blk),
                   pl.BlockSpec((S, V_W), blk),
                   pl.BlockSpec((None, S, HEADS), lambda b, p: (p, b, 0))],
        out_shape=[jax.ShapeDtypeStruct((T, D_INNER), BF16),
                   jax.ShapeDtypeStruct((T, D_INNER), BF16),
                   jax.ShapeDtypeStruct((N_HEADS // HEADS, T, HEADS), F32)],
        compiler_params=_params(2),
    )(qkvz, qkvz, qkvz, qkvz)


def _attn_gate_bwd(dh, w_out, qkvz, o, name, tm=1024, tn=1024):
    T = dh.shape[0]
    gate_b0 = (2 * QK_WIDTH + D_INNER) // tn

    def body(dh_ref, w_ref, z_ref, o_ref, do_ref, dz_ref):
        for rows in _row_blocks(tm):
            dy = _dot_nt(dh_ref[rows, :], w_ref[...])
            z = z_ref[rows, :].astype(F32)
            sig = _sigmoid(z)
            do_ref[rows, :] = (dy * (z * sig)).astype(BF16)
            dz_ref[rows, :] = (dy * o_ref[rows, :].astype(F32)
                               * (sig * (1.0 + z * (1.0 - sig)))).astype(BF16)

    return pl.pallas_call(
        body, name=name, grid=(T // tm, D_INNER // tn),
        in_specs=[pl.BlockSpec((tm, D_MODEL), lambda m, n: (m, 0)),
                  pl.BlockSpec((tn, D_MODEL), lambda m, n: (n, 0)),
                  pl.BlockSpec((tm, tn), lambda m, n: (m, gate_b0 + n)),
                  pl.BlockSpec((tm, tn), lambda m, n: (m, n))],
        out_specs=[pl.BlockSpec((tm, tn), lambda m, n: (m, n)),
                   pl.BlockSpec((tm, tn), lambda m, n: (m, gate_b0 + n))],
        out_shape=[jax.ShapeDtypeStruct((T, D_INNER), BF16),
                   jax.ShapeDtypeStruct((T, 2 * QK_WIDTH + 2 * D_INNER), BF16)],
        compiler_params=_params(2),
    )(dh, w_out, qkvz, o)


def _attn_bwd(qkv, do, ltot, dproj1, S, name):
    T = qkv.shape[0]
    nq = S // TQ
    HEADS, QK_W, V_W = BWD_HEADS, BWD_HEADS * 64, BWD_HEADS * HEAD_V
    k_b0 = QK_WIDTH // QK_W
    v_b0 = 2 * QK_WIDTH // V_W
    hs = range(HEADS)
    pairs = range(HEADS // 2)
    n_groups = N_HEADS // HEADS
    n_steps = (T // S) * n_groups

    def body(q_ref, k_ref, v_ref, do_ref, lt_ref, _, out_ref, dq_s, dk_s, dv_s, dkb_s, dvb_s, sems):
        b, p = pl.program_id(0), pl.program_id(1)
        row = lax.broadcasted_iota(jnp.int32, (TQ, TQ), 0)
        col = lax.broadcasted_iota(jnp.int32, (TQ, TQ), 1)
        causal = col < row
        upto = (row <= col).astype(BF16)
        before = (row < col).astype(BF16)
        dk_s[...] = jnp.zeros_like(dk_s)
        dv_s[...] = jnp.zeros_like(dv_s)
        rows = pl.ds(pl.multiple_of(b * S, TQ), S)
        copies = [
            pltpu.make_async_copy(
                dq_s, out_ref.at[rows, pl.ds(pl.multiple_of(p * QK_W, 128), QK_W)], sems.at[0]),
            pltpu.make_async_copy(
                dkb_s, out_ref.at[rows, pl.ds(pl.multiple_of(QK_WIDTH + p * QK_W, 128), QK_W)],
                sems.at[1]),
            pltpu.make_async_copy(
                dvb_s, out_ref.at[rows, pl.ds(pl.multiple_of(2 * QK_WIDTH + p * V_W, 128), V_W)],
                sems.at[2]),
        ]
        step = b * n_groups + p

        @pl.when(step > 0)
        def _():
            for cp in copies:
                cp.wait()

        def q_block(qi, _):
            qoff = pl.multiple_of(qi * TQ, TQ)
            qms = [qm * 0.125 for qm in _masked_heads(q_ref[pl.ds(qoff, TQ), :], HEADS)]
            vsl = [slice(hh * HEAD_V, (hh + 1) * HEAD_V) for hh in hs]
            psl = [slice(pp * HEAD_PAIR_QK, (pp + 1) * HEAD_PAIR_QK) for pp in pairs]
            do_h = [do_ref[pl.ds(qoff, TQ), sl] for sl in vsl]
            total = [lt_ref[pl.ds(qoff, TQ), hh:hh + 1] for hh in hs]

            def k_block(j, carry, diagonal):
                koff = pl.multiple_of(j * TQ, TQ)
                kms = _masked_heads(k_ref[pl.ds(koff, TQ), :], HEADS)
                g_before = [carry[2 * hh] for hh in hs]
                lom_before = [carry[2 * hh + 1] for hh in hs]
                z = [_dot_nt(qms[hh], kms[hh]) for hh in hs]
                da = [_dot_nt(do_h[hh], v_ref[pl.ds(koff, TQ), vsl[hh]]) for hh in hs]
                terms = [_log_terms(z[hh]) for hh in hs]
                log_om = [jnp.where(causal, t[1], 0.0) if diagonal else t[1] for t in terms]
                prefix = [_split_dot(log_om[hh], upto) for hh in hs]
                a = [jnp.exp(terms[hh][0] + ((total[hh] - lom_before[hh]) - prefix[hh])) for hh in hs]
                if diagonal:
                    a = [jnp.where(causal, a[hh], 0.0) for hh in hs]
                g = [a[hh] * da[hh] for hh in hs]
                g_prefix = [_dot(g[hh].astype(BF16), before) for hh in hs]
                out, dzs = [], []
                for hh in hs:
                    beta = jnp.exp(terms[hh][0])
                    g_excl = (g_before[hh] + g_prefix[hh]) * beta
                    if diagonal:
                        g_excl = jnp.where(causal, g_excl, 0.0)
                    dzs.append((g[hh] * (1.0 - beta) - g_excl).astype(BF16))
                    out += [g_before[hh] + jnp.sum(g[hh], axis=1, keepdims=True),
                            lom_before[hh] + jnp.sum(log_om[hh], axis=1, keepdims=True)]
                for hh in hs:
                    dv_s[pl.ds(koff, TQ), vsl[hh]] += _dot_tn(a[hh].astype(BF16), do_h[hh])
                dq = []
                for pp in pairs:
                    pair = slice(2 * pp, 2 * pp + 2)
                    dq.append(carry[2 * HEADS + pp] + _dot(jnp.concatenate(dzs[pair], axis=1),
                                                           jnp.concatenate(kms[pair], axis=0)))
                    dk_s[pl.ds(koff, TQ), psl[pp]] += _dot_tn(jnp.concatenate(dzs[pair], axis=0),
                                                              jnp.concatenate(qms[pair], axis=0))
                return tuple(out) + tuple(dq)

            zero = jnp.zeros((TQ, 1), F32)
            carry = (zero,) * (2 * HEADS) + (jnp.zeros((TQ, HEAD_PAIR_QK), F32),) * (HEADS // 2)
            carry = lax.fori_loop(0, qi, lambda j, c: k_block(j, c, False), carry)
            carry = k_block(qi, carry, True)
            for pp in pairs:
                dq_s[pl.ds(qoff, TQ), psl[pp]] = (carry[2 * HEADS + pp] * 0.125).astype(BF16)
            return 0

        lax.fori_loop(0, nq, q_block, 0)
        dkb_s[...] = dk_s[...].astype(BF16)
        dvb_s[...] = dv_s[...].astype(BF16)
        for cp in copies:
            cp.start()

        @pl.when(step == n_steps - 1)
        def _():
            for cp in copies:
                cp.wait()

    return pl.pallas_call(
        body, name=name, grid=(T // S, N_HEADS // HEADS),
        in_specs=[pl.BlockSpec((S, QK_W), lambda b, p: (b, p)),
                  pl.BlockSpec((S, QK_W), lambda b, p: (b, k_b0 + p)),
                  pl.BlockSpec((S, V_W), lambda b, p: (b, v_b0 + p)),
                  pl.BlockSpec((S, V_W), lambda b, p: (b, p)),
                  pl.BlockSpec((None, S, HEADS), lambda b, p: (p, b, 0)),
                  HBM_SPEC],
        out_specs=HBM_SPEC,
        out_shape=jax.ShapeDtypeStruct(dproj1.shape, dproj1.dtype),
        input_output_aliases={5: 0},
        scratch_shapes=[pltpu.VMEM((S, QK_W), BF16),
                        pltpu.VMEM((S, QK_W), F32),
                        pltpu.VMEM((S, V_W), F32),
                        pltpu.VMEM((S, QK_W), BF16),
                        pltpu.VMEM((S, V_W), BF16),
                        pltpu.SemaphoreType.DMA((3,))],
        compiler_params=_params(2),
    )(qkv, qkv, qkv, do, ltot, dproj1)


def _out_proj_loss_head(a, w, res, g_row, target, name, tm=512):
    T, K = a.shape

    def body(a_ref, w_ref, r_ref, g_ref, t_ref, dh_ref, dg_ref, loss_ref, dhb_ref):
        @pl.when(pl.program_id(0) == 0)
        def _():
            dg_ref[...] = jnp.zeros_like(dg_ref)
            loss_ref[...] = jnp.zeros_like(loss_ref)

        gain = g_ref[...]
        dg, loss = dg_ref[...], loss_ref[...]
        for rows in _row_blocks(tm):
            x = r_ref[rows, :] + _dot(a_ref[rows, :], w_ref[...])
            inv = lax.rsqrt(jnp.mean(x * x, axis=-1, keepdims=True) + RMS_EPS)
            xhat = x * inv
            err = xhat * gain - t_ref[rows, :]
            per_token = jnp.mean(err * err, axis=-1, keepdims=True)
            loss = loss + 0.5 * jnp.sum(per_token, axis=0, keepdims=True)
            dy = err * (1.0 / D_MODEL)
            dg = dg + jnp.sum(dy * xhat, axis=0, keepdims=True)
            dxh = dy * gain
            proj = jnp.mean(dxh * xhat, axis=-1, keepdims=True)
            dh = inv * (dxh - xhat * proj)
            dh_ref[rows, :] = dh
            dhb_ref[rows, :] = dh.astype(BF16)
        dg_ref[...] = dg
        loss_ref[...] = loss

    return pl.pallas_call(
        body, name=name, grid=(T // tm,),
        in_specs=[pl.BlockSpec((tm, K), lambda m: (m, 0)),
                  pl.BlockSpec((K, D_MODEL), lambda m: (0, 0)),
                  pl.BlockSpec((tm, D_MODEL), lambda m: (m, 0)),
                  pl.BlockSpec((1, D_MODEL), lambda m: (0, 0)),
                  pl.BlockSpec((tm, D_MODEL), lambda m: (m, 0))],
        out_specs=[pl.BlockSpec((tm, D_MODEL), lambda m: (m, 0)),
                   pl.BlockSpec((1, D_MODEL), lambda m: (0, 0)),
                   pl.BlockSpec((1, 128), lambda m: (0, 0)),
                   pl.BlockSpec((tm, D_MODEL), lambda m: (m, 0))],
        out_shape=[jax.ShapeDtypeStruct((T, D_MODEL), F32),
                   jax.ShapeDtypeStruct((1, D_MODEL), F32),
                   jax.ShapeDtypeStruct((1, 128), F32),
                   jax.ShapeDtypeStruct((T, D_MODEL), BF16)],
        compiler_params=_params(1),
    )(a, w, res, g_row, target)


def _place():
    return lax.axis_index("x"), lax.axis_index("y"), lax.axis_index("c")


def _other_chips(x, y):
    return [(1 - x, y), (x, 1 - y), (1 - x, 1 - y)]


def _half(ref, c):
    hr = ref.shape[-2] // 2
    return pl.ds(pl.multiple_of(c * hr, 8), hr)


def _cast_to_slot(shard, chip, name, tr=256):
    R, C = shard.shape

    def body(chip_ref, w_ref, o_ref):
        o_ref[0] = w_ref[...].astype(BF16)

    return pl.pallas_call(
        body, name=name,
        grid_spec=pltpu.PrefetchScalarGridSpec(
            num_scalar_prefetch=1, grid=(R // tr,),
            in_specs=[pl.BlockSpec((tr, C), lambda i, chip_ref: (i, 0))],
            out_specs=pl.BlockSpec((1, tr, C), lambda i, chip_ref: (chip_ref[0], i, 0))),
        out_shape=jax.ShapeDtypeStruct((N_CHIPS, R, C), BF16),
        compiler_params=_params(1),
    )(chip, shard)


def _weight_plan(bufs):
    x, y, c = _place()
    plan = []
    for buf in bufs:
        mine = buf.at[2 * x + y, _half(buf, c)]
        for ox, oy in _other_chips(x, y):
            plan.append((mine, mine, (ox, oy, c), buf.at[2 * ox + oy, _half(buf, c)]))
    return plan


def _chip_sum_plan(bufs):
    x, y, c = _place()
    n = len(bufs) // 2
    plan = []
    for sums, land in zip(bufs[:n], bufs[n:]):
        for k, (ox, oy) in enumerate(_other_chips(x, y)):
            plan.append((sums.at[2 * ox + oy], land.at[k], (ox, oy, c), land.at[k]))
    return plan


def _sibling_plan(bufs):
    x, y, c = _place()
    n = len(bufs) // 2
    return [(p.at[:, _half(p, 1 - c)], land, (x, y, 1 - c), land)
            for p, land in zip(bufs[:n], bufs[n:])]


SEM_SPEC = pl.BlockSpec(memory_space=pltpu.SEMAPHORE)
ANY_SPEC = pl.BlockSpec(memory_space=pl.ANY)
DATAFLOW = pltpu.SideEffectType.DATAFLOW_SIDE_EFFECTING


def _in_hbm(a):
    return pltpu.with_memory_space_constraint(a, pltpu.HBM)


def _exchange_start(bufs, after, plan, n_copies, name):
    nb = len(bufs)

    def body(*refs):
        send_sems, recv_sems = refs[nb + 1], refs[nb + 2]
        for i, (src, dst, dev, _) in enumerate(plan(refs[:nb])):
            pltpu.make_async_remote_copy(
                src_ref=src, dst_ref=dst, send_sem=send_sems.at[i], recv_sem=recv_sems.at[i],
                device_id=dev, device_id_type=MESH).start()
        token = refs[-1]
        token[...] = jnp.zeros_like(token)

    res = pl.pallas_call(
        body, name=name,
        in_specs=[HBM_SPEC] * nb + [ANY_SPEC],
        out_specs=[SEM_SPEC, SEM_SPEC] + [HBM_SPEC] * nb + [pl.BlockSpec(memory_space=pltpu.VMEM)],
        out_shape=[pltpu.SemaphoreType.DMA((n_copies,)), pltpu.SemaphoreType.DMA((n_copies,))]
        + [pltpu.HBM(b.shape, b.dtype) for b in bufs] + [jax.ShapeDtypeStruct((8, 128), F32)],
        input_output_aliases={i: 2 + i for i in range(nb)},
        compiler_params=pltpu.CompilerParams(has_side_effects=DATAFLOW),
    )(*[_in_hbm(b) for b in bufs], after)
    return res[0], res[1], list(res[2:2 + nb]), res[-1]


def _exchange_wait(bufs, send_sems, recv_sems, after, plan, name):
    nb = len(bufs)

    def body(*refs):
        sends, recvs = refs[nb], refs[nb + 1]
        for i, (src, dst, dev, landing) in enumerate(plan(refs[:nb])):
            pltpu.make_async_remote_copy(
                src_ref=src, dst_ref=landing, send_sem=sends.at[i], recv_sem=recvs.at[i],
                device_id=dev, device_id_type=MESH).wait()

    res = pl.pallas_call(
        body, name=name,
        in_specs=[HBM_SPEC] * nb + [SEM_SPEC, SEM_SPEC, ANY_SPEC],
        out_specs=[HBM_SPEC] * nb,
        out_shape=[pltpu.HBM(b.shape, b.dtype) for b in bufs],
        input_output_aliases={i: i for i in range(nb)},
        compiler_params=pltpu.CompilerParams(has_side_effects=DATAFLOW),
    )(*bufs, send_sems, recv_sems, after)
    return list(res)


def _allgather_weights(slots, name, landed=False):
    n = len(slots)

    def body(*refs):
        outs = refs[n:2 * n]
        send_sems, recv_sems, fwd_send, fwd_recv = refs[2 * n:]
        x, y, c = _place()
        chips = _other_chips(x, y)

        def landing(a, chip, half_of):
            return outs[a].at[2 * chip[0] + chip[1], _half(outs[a], half_of)]

        def ici(a, k, chip_from, to):
            return pltpu.make_async_remote_copy(
                src_ref=landing(a, chip_from, c), dst_ref=landing(a, chip_from, c),
                send_sem=send_sems.at[a, k], recv_sem=recv_sems.at[a, k],
                device_id=to, device_id_type=MESH)

        def d2d(a, k, chip_from, half_of):
            return pltpu.make_async_remote_copy(
                src_ref=landing(a, chip_from, half_of), dst_ref=landing(a, chip_from, half_of),
                send_sem=fwd_send.at[a, k], recv_sem=fwd_recv.at[a, k],
                device_id=(x, y, 1 - c), device_id_type=MESH)

        sends = []
        if not landed:
            sends = [ici(a, k, (x, y), (*chips[k], c)) for a in range(n) for k in range(3)]
        for cp in sends:
            cp.start()
        forwards = []
        for a in range(n):
            for k in range(3):
                if not landed:
                    ici(a, k, chips[k], (x, y, c)).wait_recv()
                fw = d2d(a, k, chips[k], c)
                fw.start()
                forwards.append(fw)
        for a in range(n):
            for k in range(3):
                d2d(a, k, chips[k], 1 - c).wait_recv()
        for cp in sends + forwards:
            cp.wait_send()

    return pl.pallas_call(
        body, name=name,
        in_specs=[HBM_SPEC] * n, out_specs=[HBM_SPEC] * n,
        out_shape=[jax.ShapeDtypeStruct(s.shape, s.dtype) for s in slots],
        input_output_aliases={a: a for a in range(n)},
        scratch_shapes=[pltpu.SemaphoreType.DMA((n, 3)), pltpu.SemaphoreType.DMA((n, 3)),
                        pltpu.SemaphoreType.DMA((n, 3)), pltpu.SemaphoreType.DMA((n, 3))],
    )(*slots)


def _sibling_exchange(partials, name):
    n = len(partials)

    def body(*refs):
        ins, outs = refs[:n], refs[n:2 * n]
        send_sems, recv_sems = refs[2 * n:]
        x, y, c = _place()
        sends = [pltpu.make_async_remote_copy(
            src_ref=ins[a].at[:, _half(ins[a], 1 - c)], dst_ref=outs[a],
            send_sem=send_sems.at[a], recv_sem=recv_sems.at[a],
            device_id=(x, y, 1 - c), device_id_type=MESH) for a in range(n)]
        for cp in sends:
            cp.start()
        for cp in sends:
            cp.wait()

    return pl.pallas_call(
        body, name=name,
        in_specs=[HBM_SPEC] * n, out_specs=[HBM_SPEC] * n,
        out_shape=[jax.ShapeDtypeStruct((N_CHIPS, p.shape[1] // 2, p.shape[2]), F32)
                   for p in partials],
        scratch_shapes=[pltpu.SemaphoreType.DMA((n,)), pltpu.SemaphoreType.DMA((n,))],
    )(*partials)


def _chip_sum(partial, from_sibling, c, name, tr=256):
    _, hr, C = from_sibling.shape
    nb = hr // tr

    def body(c_ref, p_ref, s_ref, o_ref):
        o_ref[...] = (p_ref[...] + s_ref[...]).astype(BF16)

    return pl.pallas_call(
        body, name=name,
        grid_spec=pltpu.PrefetchScalarGridSpec(
            num_scalar_prefetch=1, grid=(N_CHIPS, nb),
            in_specs=[pl.BlockSpec((1, tr, C), lambda j, i, c_ref: (j, c_ref[0] * nb + i, 0)),
                      pl.BlockSpec((1, tr, C), lambda j, i, c_ref: (j, i, 0))],
            out_specs=pl.BlockSpec((1, tr, C), lambda j, i, c_ref: (j, i, 0))),
        out_shape=jax.ShapeDtypeStruct(from_sibling.shape, BF16),
        compiler_params=_params(2),
    )(c, partial, from_sibling)


def _reduce_half(partial, from_sibling, received, place, name, tr=256):
    _, hr, C = from_sibling.shape
    nb = hr // tr

    def body(p_ref, mine_ref, sib_ref, r_ref, o_ref):
        acc = mine_ref[0] + sib_ref[0]
        for k in range(3):
            acc = acc + r_ref[k].astype(F32)
        o_ref[...] = acc

    return pl.pallas_call(
        body, name=name,
        grid_spec=pltpu.PrefetchScalarGridSpec(
            num_scalar_prefetch=1, grid=(nb,),
            in_specs=[pl.BlockSpec((1, tr, C), lambda i, p: (p[0], p[1] * nb + i, 0)),
                      pl.BlockSpec((1, tr, C), lambda i, p: (p[0], i, 0)),
                      pl.BlockSpec((3, tr, C), lambda i, p: (0, i, 0))],
            out_specs=pl.BlockSpec((tr, C), lambda i, p: (p[1] * nb + i, 0))),
        out_shape=jax.ShapeDtypeStruct((2 * hr, C), F32),
        compiler_params=_params(1),
    )(place, partial, from_sibling, received)


def _join_halves(fulls, small, name):
    n = len(fulls)

    def body(*refs):
        small_ref, outs, small_all = refs[n], refs[n + 1:2 * n + 1], refs[2 * n + 1]
        send_sems, recv_sems, s_send, s_recv, loc_sem = refs[2 * n + 2:]
        x, y, c = _place()
        me = 4 * x + 2 * y + c

        def copy(a, half_of, to):
            rows = outs[a].at[_half(outs[a], half_of)]
            return pltpu.make_async_remote_copy(
                src_ref=rows, dst_ref=rows, send_sem=send_sems.at[a], recv_sem=recv_sems.at[a],
                device_id=to, device_id_type=MESH)

        def small_copy(d, landing, to):
            return pltpu.make_async_remote_copy(
                src_ref=small_ref, dst_ref=small_all.at[landing],
                send_sem=s_send.at[d - 1], recv_sem=s_recv.at[d - 1],
                device_id=to, device_id_type=MESH)

        local = pltpu.make_async_copy(small_ref, small_all.at[me], loc_sem)
        local.start()
        sends = [copy(a, c, (x, y, 1 - c)) for a in range(n)]
        sends += [small_copy(d, me, (x ^ ((d >> 2) & 1), y ^ ((d >> 1) & 1), c ^ (d & 1)))
                  for d in range(1, N_DEV)]
        for cp in sends:
            cp.start()
        for d in range(1, N_DEV):
            small_copy(d, me ^ d, (x, y, c)).wait_recv()
        for a in range(n):
            copy(a, 1 - c, (x, y, c)).wait_recv()
        for cp in sends:
            cp.wait_send()
        local.wait()

    res = pl.pallas_call(
        body, name=name,
        in_specs=[HBM_SPEC] * (n + 1), out_specs=[HBM_SPEC] * (n + 1),
        out_shape=[jax.ShapeDtypeStruct(f.shape, F32) for f in fulls]
        + [jax.ShapeDtypeStruct((N_DEV,) + small.shape, F32)],
        input_output_aliases={a: a for a in range(n)},
        scratch_shapes=[pltpu.SemaphoreType.DMA((n,)), pltpu.SemaphoreType.DMA((n,)),
                        pltpu.SemaphoreType.DMA((N_DEV - 1,)), pltpu.SemaphoreType.DMA((N_DEV - 1,)),
                        pltpu.SemaphoreType.DMA],
    )(*fulls, small)
    return list(res[:n]), res[n]


def _adamw_math(w, g, m, v):
    m = ADAM_B1 * m + (1.0 - ADAM_B1) * g
    v = ADAM_B2 * v + (1.0 - ADAM_B2) * (g * g)
    m_hat = m / (1.0 - ADAM_B1 ** ADAM_STEP)
    v_hat = v / (1.0 - ADAM_B2 ** ADAM_STEP)
    delta = -ADAM_LR * (m_hat / (jnp.sqrt(v_hat) + ADAM_EPS) + ADAM_WD * w)
    return delta, m, v


def _adamw(w, g, m, v, name, tr=256):
    R, C = w.shape
    tr = min(tr, R)

    def body(w_ref, g_ref, m_ref, v_ref, d_out, m_out, v_out):
        d_out[...], m_out[...], v_out[...] = _adamw_math(w_ref[...], g_ref[...], m_ref[...], v_ref[...])

    spec = pl.BlockSpec((tr, C), lambda i: (i, 0))
    return pl.pallas_call(
        body, name=name, grid=(R // tr,),
        in_specs=[spec] * 4, out_specs=[spec] * 3,
        out_shape=[jax.ShapeDtypeStruct((R, C), F32)] * 3,
        compiler_params=_params(1),
    )(w, g, m, v)


def _adamw_small(small_all, w, m, v, name):
    def body(s_ref, w_ref, m_ref, v_ref, g_out, d_out, m_out, v_out):
        g = s_ref[0]
        for d in range(1, N_DEV):
            g = g + s_ref[d]
        g_out[...] = g
        d_out[...], m_out[...], v_out[...] = _adamw_math(w_ref[...], g, m_ref[...], v_ref[...])

    vm = pl.BlockSpec(memory_space=pltpu.VMEM)
    return pl.pallas_call(
        body, name=name, in_specs=[vm] * 4, out_specs=[vm] * 4,
        out_shape=[jax.ShapeDtypeStruct(w.shape, F32)] * 4,
    )(small_all, w, m, v)


def _pack_small(norm_g, pool_scale, norm_f, extra_row):
    return jnp.concatenate([norm_g.reshape(2, D_MODEL), pool_scale.reshape(2, D_MODEL),
                            norm_f.reshape(1, D_MODEL), extra_row,
                            jnp.zeros((2, D_MODEL), F32)], axis=0)


def kernel(x, norm_g, pool_w_in, pool_w, pool_scale, pool_w_out, sb_w_in, sb_w_out, norm_f, loss_target, m_norm_g, m_pool_w_in, m_pool_w, m_pool_scale, m_pool_w_out, m_sb_w_in, m_sb_w_out, m_norm_f, v_norm_g, v_pool_w_in, v_pool_w, v_pool_scale, v_pool_w_out, v_sb_w_in, v_sb_w_out, v_norm_f):
    nb, S, _ = x.shape
    T = nb * S
    xt = x.reshape(T, D_MODEL)
    target = loss_target.reshape(T, D_MODEL)
    cx, cy, cc = _place()

    def shard2d(w):
        return w.reshape(-1, w.shape[-1])

    names = ("pool_w_in", "pool_w", "pool_w_out", "sb_w_in", "sb_w_out")
    w_shards = [shard2d(w) for w in (pool_w_in, pool_w, pool_w_out, sb_w_in, sb_w_out)]
    m_shards = [shard2d(w) for w in (m_pool_w_in, m_pool_w, m_pool_w_out, m_sb_w_in, m_sb_w_out)]
    v_shards = [shard2d(w) for w in (v_pool_w_in, v_pool_w, v_pool_w_out, v_sb_w_in, v_sb_w_out)]

    chip = (2 * cx + cy).reshape(1).astype(jnp.int32)
    c_arr = cc.reshape(1).astype(jnp.int32)
    place = jnp.stack([2 * cx + cy, cc]).astype(jnp.int32)
    slots = [_cast_to_slot(w, chip, "cast_" + nm) for w, nm in zip(w_shards, names)]
    g0, g1, gf = norm_g[0:1], norm_g[1:2], norm_f.reshape(1, D_MODEL)

    w_pin, = _allgather_weights(slots[:1], "allgather_pool_in_weights")
    mix_send, mix_recv, mix_slots, token = _exchange_start(slots[1:3], w_pin, _weight_plan, 6,
                                                           "pool_weights_start")
    sb_send, sb_recv, sb_slots, token = _exchange_start(slots[3:], token, _weight_plan, 6,
                                                        "sb_weights_start")

    proj0, u0 = _rms_matmul(xt, g0 + token[0:1, 0:1], w_pin, "pool_in_proj")
    mix_slots = _exchange_wait(mix_slots, mix_send, mix_recv, proj0, _weight_plan, "pool_weights_wait")
    w_g, w_pout = _allgather_weights(mix_slots, "pool_weights_forward", landed=True)
    w_pout = w_pout.reshape(D_INNER, D_MODEL)
    y0, pooled, mixed = _pool_fwd(proj0, w_g, pool_scale, S, "pool_mix")
    sb_slots = _exchange_wait(sb_slots, sb_send, sb_recv, y0, _weight_plan, "sb_weights_wait")
    w_sin, w_sout = _allgather_weights(sb_slots, "sb_weights_forward", landed=True)
    w_sout = w_sout.reshape(D_INNER, D_MODEL)
    h1 = _matmul_residual(y0, w_pout, xt, "pool_out_proj")
    n1 = 2 * QK_WIDTH + 2 * D_INNER
    qkvz, u1 = _rms_matmul(h1, g1, w_sin, "sb_in_proj")
    o, y1, ltot = _attn_fwd(qkvz, S, "sb_attention")
    dh2, d_norm_f, loss_row, dh2_b = _out_proj_loss_head(y1, w_sout, h1, gf, target,
                                                         "sb_out_proj_loss_head")

    def reduce_start(partials, tag, after=None, behind=None):
        n, done = len(partials), None
        after = c_arr if after is None else after
        if behind is None:
            from_sibling = list(_sibling_exchange(partials, "grad_sibling_exchange_" + tag))
        else:
            lands = [lax.empty((N_CHIPS, p.shape[1] // 2, p.shape[2]), F32) for p in partials]
            send, recv, bufs, token = _exchange_start(partials + lands, after, _sibling_plan, n,
                                                      "grad_sibling_start_" + tag)
            done = behind(token[0:1, 0:1])
            bufs = _exchange_wait(bufs, send, recv, done[0], _sibling_plan, "grad_sibling_wait_" + tag)
            partials, from_sibling, after = bufs[:n], bufs[n:], c_arr
        sums = [_chip_sum(p, s, c_arr, "grad_chip_sum_%s_%d" % (tag, i))
                for i, (p, s) in enumerate(zip(partials, from_sibling))]
        lands = [lax.empty((3,) + s.shape[1:], BF16) for s in sums]
        send, recv, bufs, token = _exchange_start(sums + lands, after, _chip_sum_plan, 3 * n,
                                                  "grad_chip_exchange_start_" + tag)
        return (partials, from_sibling, send, recv, bufs), token[0:1, 0:1], done

    def reduce_finish(started, after, tag):
        partials, from_sibling, send, recv, bufs = started
        received = _exchange_wait(bufs, send, recv, after, _chip_sum_plan,
                                  "grad_chip_exchange_wait_" + tag)[len(partials):]
        return [_reduce_half(p, s, r, place, "grad_reduce_%s_%d" % (tag, i))
                for i, (p, s, r) in enumerate(zip(partials, from_sibling, received))]

    shard = lambda i, j, t: (j, 0, 0)
    gw_sout = _matmul_tn(y1, dh2_b, D_INNER, D_MODEL, (D_INNER, D_MODEL), (1024, 1024),
                         lambda i, j, t: (i, j), "grad_sb_w_out", bm=1024, bn=1024)
    do, dproj1 = _attn_gate_bwd(dh2_b, w_sout, qkvz, o, "sb_gate_bwd")
    dproj1 = _attn_bwd(qkvz, do, ltot, dproj1, S, "sb_attention_bwd")
    gw_sin = _matmul_tn(u1, dproj1, D_MODEL, n1, (N_CHIPS, D_MODEL, n1 // 4), (1, D_MODEL, n1 // 4),
                        shard, "grad_sb_w_in", bm=D_MODEL, bn=n1 // 4)
    sb_started, token, (dh1, d_g1, dh1_b) = reduce_start(
        [gw_sin, gw_sout.reshape(N_CHIPS, -1, D_MODEL)], "sb",
        behind=lambda tok: _matmul_nt_rms_bwd(dproj1, w_sin, h1, g1 + tok, dh2, "sb_in_bwd", True))
    gw_pout = _matmul_tn(y0, dh1_b, D_INNER, D_MODEL, (D_INNER, D_MODEL), (1024, 1024),
                         lambda i, j, t: (i, j), "grad_pool_w_out", bm=1024, bn=1024)
    dmixed, dproj0, d_scale = _pool_gate_bwd(dh1_b, w_pout, proj0, mixed, pool_scale + token,
                                             "pool_gate_bwd")
    gw_g = _matmul_tn(pooled, dmixed, D_INNER, D_INNER, (N_CHIPS, GROUP_DIM, GROUP_DIM),
                      (N_CHIPS, GROUP_DIM // N_CHIPS, GROUP_DIM), lambda i, j, t: (0, i, 0),
                      "grad_pool_w", bm=GROUP_DIM, bn=GROUP_DIM, diagonal_blocks=True)
    mix_started, token, (dproj0,) = reduce_start(
        [gw_g, gw_pout.reshape(N_CHIPS, -1, D_MODEL)], "pool_mix",
        behind=lambda tok: (_pool_bwd(dmixed, w_g, dproj0, tok, S, "pool_bwd"),))
    n0 = 2 * D_INNER
    gw_pin = _matmul_tn(u0, dproj0, D_MODEL, n0, (N_CHIPS, D_MODEL, n0 // 4), (1, D_MODEL, n0 // 4),
                        shard, "grad_pool_w_in", bm=D_MODEL, bn=n0 // 4)
    pin_started, token, _ = reduce_start([gw_pin], "pool_in", after=token)
    dx, d_g0 = _matmul_nt_rms_bwd(dproj0, w_pin, xt, g0 + token, dh1, "pool_in_bwd", False)

    small = _pack_small(jnp.concatenate([d_g0, d_g1], axis=0), d_scale, d_norm_f,
                        jnp.broadcast_to(loss_row[:, :1], (1, D_MODEL)))
    grads, small_all = _join_halves(reduce_finish(pin_started, dx, "pool_in")
                                    + reduce_finish(mix_started, dx, "pool_mix")
                                    + reduce_finish(sb_started, dx, "sb"), small, "grad_join_halves")

    deltas, new_m, new_v = [], [], []
    for w, g, m, v, nm in zip(w_shards, grads, m_shards, v_shards, names):
        d, mm, vv = _adamw(w, g, m, v, "adamw_" + nm)
        deltas.append(d)
        new_m.append(mm)
        new_v.append(vv)

    zero_row = jnp.zeros((1, D_MODEL), F32)
    g_small, d_small, m_small, v_small = _adamw_small(
        small_all, _pack_small(norm_g, pool_scale, norm_f, zero_row),
        _pack_small(m_norm_g, m_pool_scale, m_norm_f, zero_row),
        _pack_small(v_norm_g, v_pool_scale, v_norm_f, zero_row + 1.0), "adamw_small")
    loss = g_small[5, 0]

    def unpack_small(a):
        return a[0:2], a[2:4].reshape(1, D_INNER), a[4]

    def assemble(big, small3):
        ng, ps, nf = small3
        return [ng, big[0].reshape(pool_w_in.shape), big[1].reshape(pool_w.shape), ps,
                big[2].reshape(pool_w_out.shape), big[3].reshape(sb_w_in.shape),
                big[4].reshape(sb_w_out.shape), nf]

    return (loss, dx.reshape(x.shape),
            *assemble(grads, unpack_small(g_small)),
            *assemble(deltas, unpack_small(d_small)),
            *assemble(new_m, unpack_small(m_small)),
            *assemble(new_v, unpack_small(v_small)))
```
